```python
import jax, jax.numpy as jnp
from jax import lax
import numpy as np

D_MODEL = 1024
BATCH = 8
SEQ = 2048
DEPTH = 4

POOL_GROUPS = 4
POOL_GROUP_DIM = 128
POOL_WIDTH = POOL_GROUPS * POOL_GROUP_DIM
POOL_WINDOWS = (2, 4, 8, 16)
LRU_HEADS = 10
LRU_HEAD_DIM = 128
LRU_WIDTH = LRU_HEADS * LRU_HEAD_DIM
CONV_WIDTH = 4
LRU_C = 8.0
D_FF = 2816
EPS = 1e-6
IN_WIDTH = POOL_WIDTH + 2 * LRU_WIDTH + 2 * D_MODEL

kernel_name = "macaron_pool_rglru_gated_hybrid"


def rmsnorm(x, g):
    xf = x.astype(jnp.float32)
    var = jnp.mean(xf * xf, axis=-1, keepdims=True)
    return (xf * lax.rsqrt(var + EPS) * g.astype(jnp.float32)).astype(x.dtype)


def swiglu_ffn(h, w_up, w_down):
    u = h @ w_up
    a, b = jnp.split(u, 2, axis=-1)
    return (jax.nn.silu(a) * b) @ w_down


def causal_pool_minus_self(u, window):
    b, s, c = u.shape
    uf = u.astype(jnp.float32)
    cs = jnp.cumsum(uf, axis=1)
    cs_pad = jnp.concatenate([jnp.zeros((b, 1, c), jnp.float32), cs], axis=1)
    prev = jnp.concatenate([jnp.zeros((b, window - 1, c), jnp.float32), cs_pad[:, : s - window + 1]], axis=1)
    count = jnp.minimum(jnp.arange(1, s + 1, dtype=jnp.float32), float(window))[None, :, None]
    return ((cs - prev) / count - uf).astype(u.dtype)


def pool_mixer(u, w_grp, b_grp, scale):
    b, s, _ = u.shape
    ug = u.reshape(b, s, POOL_GROUPS, POOL_GROUP_DIM)
    pooled = jnp.stack([causal_pool_minus_self(ug[:, :, g], POOL_WINDOWS[g]) for g in range(POOL_GROUPS)], axis=2)
    mixed = jnp.einsum('bsgc,gcd->bsgd', pooled, w_grp) + b_grp
    return mixed.reshape(b, s, POOL_WIDTH) * scale


def causal_depthwise_conv(u, w, bias):
    s = u.shape[1]
    up = jnp.pad(u, ((0, 0), (CONV_WIDTH - 1, 0), (0, 0)))
    y = sum(up[:, k:k + s] * w[k] for k in range(CONV_WIDTH))
    return y + bias


def _lru_combine(left, right):
    a_l, b_l = left
    a_r, b_r = right
    return a_l * a_r, a_r * b_l + b_r


def rg_lru(u, w_a, b_a, w_x, b_x, lam):
    b, s, _ = u.shape
    uh = u.reshape(b, s, LRU_HEADS, LRU_HEAD_DIM)
    r = jax.nn.sigmoid(jnp.einsum('bshd,hde->bshe', uh, w_a) + b_a).reshape(b, s, LRU_WIDTH)
    i = jax.nn.sigmoid(jnp.einsum('bshd,hde->bshe', uh, w_x) + b_x).reshape(b, s, LRU_WIDTH)
    log_a = -LRU_C * r.astype(jnp.float32) * jax.nn.softplus(-lam.astype(jnp.float32))
    a = jnp.exp(log_a)
    mult = jnp.sqrt(-jnp.expm1(2.0 * log_a))
    bx = mult * (i * u).astype(jnp.float32)
    _, h = lax.associative_scan(_lru_combine, (a, bx), axis=1)
    return h.astype(u.dtype)


def hybrid_mixer(h, w_in, pool_w, pool_b, pool_scale, w_pool_up, conv_w, conv_b,
                 lru_w_a, lru_b_a, lru_w_x, lru_b_x, lru_lambda, w_lru_up, w_out):
    proj = h @ w_in
    s1 = POOL_WIDTH
    s2 = s1 + LRU_WIDTH
    s3 = s2 + LRU_WIDTH
    u_pool, u_lru, u_gelu, g_logits = proj[..., :s1], proj[..., s1:s2], proj[..., s2:s3], proj[..., s3:]
    y_pool = pool_mixer(u_pool, pool_w, pool_b, pool_scale) @ w_pool_up
    v = causal_depthwise_conv(u_lru, conv_w, conv_b)
    y_lru = (rg_lru(v, lru_w_a, lru_b_a, lru_w_x, lru_b_x, lru_lambda) * jax.nn.gelu(u_gelu)) @ w_lru_up
    g = jax.nn.sigmoid(g_logits)
    g_pool, g_lru = g[..., :D_MODEL], g[..., D_MODEL:]
    return (g_pool * y_pool + g_lru * y_lru) @ w_out


def _fwd_setup_inputs(seed: int = 0) -> dict:
    key = jax.random.key(seed)
    ks = jax.random.split(key, 26)
    f32 = jnp.float32

    def nrm(k, shape, fan_in):
        return jax.random.normal(k, shape, f32) * (fan_in ** -0.5)

    def gain(k, shape):
        return 1.0 + 0.02 * jax.random.normal(k, shape, f32)

    def small(k, shape):
        return 0.01 * jax.random.normal(k, shape, f32)

    L = DEPTH
    a_c = jax.random.uniform(ks[17], (L, LRU_WIDTH), f32, 0.9, 0.999)
    sig = a_c ** (1.0 / LRU_C)
    lam = jnp.log(sig) - jnp.log1p(-sig)
    return {
        "x": jax.random.normal(ks[0], (BATCH, SEQ, D_MODEL), f32),
        "norm_ffn1": gain(ks[1], (L, D_MODEL)),
        "ffn1_w_up": nrm(ks[2], (L, D_MODEL, 2 * D_FF), D_MODEL),
        "ffn1_w_down": nrm(ks[3], (L, D_FF, D_MODEL), D_FF),
        "norm_mix": gain(ks[4], (L, D_MODEL)),
        "w_in": nrm(ks[5], (L, D_MODEL, IN_WIDTH), D_MODEL),
        "pool_w": nrm(ks[6], (L, POOL_GROUPS, POOL_GROUP_DIM, POOL_GROUP_DIM), POOL_GROUP_DIM),
        "pool_b": small(ks[7], (L, POOL_GROUPS, POOL_GROUP_DIM)),
        "pool_scale": 1.0 + 0.1 * jax.random.normal(ks[8], (L, POOL_WIDTH), f32),
        "w_pool_up": nrm(ks[9], (L, POOL_WIDTH, D_MODEL), POOL_WIDTH),
        "conv_w": nrm(ks[10], (L, CONV_WIDTH, LRU_WIDTH), CONV_WIDTH),
        "conv_b": small(ks[11], (L, LRU_WIDTH)),
        "lru_w_a": nrm(ks[12], (L, LRU_HEADS, LRU_HEAD_DIM, LRU_HEAD_DIM), LRU_HEAD_DIM),
        "lru_b_a": small(ks[13], (L, LRU_HEADS, LRU_HEAD_DIM)),
        "lru_w_x": nrm(ks[14], (L, LRU_HEADS, LRU_HEAD_DIM, LRU_HEAD_DIM), LRU_HEAD_DIM),
        "lru_b_x": small(ks[15], (L, LRU_HEADS, LRU_HEAD_DIM)),
        "lru_lambda": lam,
        "w_lru_up": nrm(ks[16], (L, LRU_WIDTH, D_MODEL), LRU_WIDTH),
        "w_out": nrm(ks[18], (L, D_MODEL, D_MODEL), D_MODEL),
        "norm_ffn2": gain(ks[19], (L, D_MODEL)),
        "ffn2_w_up": nrm(ks[20], (L, D_MODEL, 2 * D_FF), D_MODEL),
        "ffn2_w_down": nrm(ks[21], (L, D_FF, D_MODEL), D_FF),
        "final_norm": gain(ks[22], (D_MODEL,)),
    }


def _fwd_reference(x, norm_ffn1, ffn1_w_up, ffn1_w_down, norm_mix, w_in, pool_w, pool_b, pool_scale,
              w_pool_up, conv_w, conv_b, lru_w_a, lru_b_a, lru_w_x, lru_b_x, lru_lambda, w_lru_up,
              w_out, norm_ffn2, ffn2_w_up, ffn2_w_down, final_norm):
    for l in range(DEPTH):
        x = x + 0.5 * swiglu_ffn(rmsnorm(x, norm_ffn1[l]), ffn1_w_up[l], ffn1_w_down[l])
        x = x + hybrid_mixer(rmsnorm(x, norm_mix[l]), w_in[l], pool_w[l], pool_b[l], pool_scale[l],
                             w_pool_up[l], conv_w[l], conv_b[l], lru_w_a[l], lru_b_a[l],
                             lru_w_x[l], lru_b_x[l], lru_lambda[l], w_lru_up[l], w_out[l])
        x = x + 0.5 * swiglu_ffn(rmsnorm(x, norm_ffn2[l]), ffn2_w_up[l], ffn2_w_down[l])
    return rmsnorm(x, final_norm)


import jax as _jax
import jax.numpy as _jnp

TWIN_FORMAT = 'train_step'
FWD_PARAMS = ['x', 'norm_ffn1', 'ffn1_w_up', 'ffn1_w_down', 'norm_mix', 'w_in', 'pool_w', 'pool_b', 'pool_scale', 'w_pool_up', 'conv_w', 'conv_b', 'lru_w_a', 'lru_b_a', 'lru_w_x', 'lru_b_x', 'lru_lambda', 'w_lru_up', 'w_out', 'norm_ffn2', 'ffn2_w_up', 'ffn2_w_down', 'final_norm']
TWIN_WEIGHTS = ['norm_ffn1', 'ffn1_w_up', 'ffn1_w_down', 'norm_mix', 'w_in', 'pool_w', 'pool_b', 'pool_scale', 'w_pool_up', 'conv_w', 'conv_b', 'lru_w_a', 'lru_b_a', 'lru_w_x', 'lru_b_x', 'lru_lambda', 'w_lru_up', 'w_out', 'norm_ffn2', 'ffn2_w_up', 'ffn2_w_down', 'final_norm']
TWIN_DIFF_INPUT = 'x'
TWIN_INPUTS = ['x', 'norm_ffn1', 'ffn1_w_up', 'ffn1_w_down', 'norm_mix', 'w_in', 'pool_w', 'pool_b', 'pool_scale', 'w_pool_up', 'conv_w', 'conv_b', 'lru_w_a', 'lru_b_a', 'lru_w_x', 'lru_b_x', 'lru_lambda', 'w_lru_up', 'w_out', 'norm_ffn2', 'ffn2_w_up', 'ffn2_w_down', 'final_norm', 'loss_target', 'm_norm_ffn1', 'm_ffn1_w_up', 'm_ffn1_w_down', 'm_norm_mix', 'm_w_in', 'm_pool_w', 'm_pool_b', 'm_pool_scale', 'm_w_pool_up', 'm_conv_w', 'm_conv_b', 'm_lru_w_a', 'm_lru_b_a', 'm_lru_w_x', 'm_lru_b_x', 'm_lru_lambda', 'm_w_lru_up', 'm_w_out', 'm_norm_ffn2', 'm_ffn2_w_up', 'm_ffn2_w_down', 'm_final_norm', 'v_norm_ffn1', 'v_ffn1_w_up', 'v_ffn1_w_down', 'v_norm_mix', 'v_w_in', 'v_pool_w', 'v_pool_b', 'v_pool_scale', 'v_w_pool_up', 'v_conv_w', 'v_conv_b', 'v_lru_w_a', 'v_lru_b_a', 'v_lru_w_x', 'v_lru_b_x', 'v_lru_lambda', 'v_w_lru_up', 'v_w_out', 'v_norm_ffn2', 'v_ffn2_w_up', 'v_ffn2_w_down', 'v_final_norm']
TWIN_OUTPUTS = ['loss', 'grad_x', 'grad_norm_ffn1', 'grad_ffn1_w_up', 'grad_ffn1_w_down', 'grad_norm_mix', 'grad_w_in', 'grad_pool_w', 'grad_pool_b', 'grad_pool_scale', 'grad_w_pool_up', 'grad_conv_w', 'grad_conv_b', 'grad_lru_w_a', 'grad_lru_b_a', 'grad_lru_w_x', 'grad_lru_b_x', 'grad_lru_lambda', 'grad_w_lru_up', 'grad_w_out', 'grad_norm_ffn2', 'grad_ffn2_w_up', 'grad_ffn2_w_down', 'grad_final_norm', 'delta_norm_ffn1', 'delta_ffn1_w_up', 'delta_ffn1_w_down', 'delta_norm_mix', 'delta_w_in', 'delta_pool_w', 'delta_pool_b', 'delta_pool_scale', 'delta_w_pool_up', 'delta_conv_w', 'delta_conv_b', 'delta_lru_w_a', 'delta_lru_b_a', 'delta_lru_w_x', 'delta_lru_b_x', 'delta_lru_lambda', 'delta_w_lru_up', 'delta_w_out', 'delta_norm_ffn2', 'delta_ffn2_w_up', 'delta_ffn2_w_down', 'delta_final_norm', 'new_m_norm_ffn1', 'new_m_ffn1_w_up', 'new_m_ffn1_w_down', 'new_m_norm_mix', 'new_m_w_in', 'new_m_pool_w', 'new_m_pool_b', 'new_m_pool_scale', 'new_m_w_pool_up', 'new_m_conv_w', 'new_m_conv_b', 'new_m_lru_w_a', 'new_m_lru_b_a', 'new_m_lru_w_x', 'new_m_lru_b_x', 'new_m_lru_lambda', 'new_m_w_lru_up', 'new_m_w_out', 'new_m_norm_ffn2', 'new_m_ffn2_w_up', 'new_m_ffn2_w_down', 'new_m_final_norm', 'new_v_norm_ffn1', 'new_v_ffn1_w_up', 'new_v_ffn1_w_down', 'new_v_norm_mix', 'new_v_w_in', 'new_v_pool_w', 'new_v_pool_b', 'new_v_pool_scale', 'new_v_w_pool_up', 'new_v_conv_w', 'new_v_conv_b', 'new_v_lru_w_a', 'new_v_lru_b_a', 'new_v_lru_w_x', 'new_v_lru_b_x', 'new_v_lru_lambda', 'new_v_w_lru_up', 'new_v_w_out', 'new_v_norm_ffn2', 'new_v_ffn2_w_up', 'new_v_ffn2_w_down', 'new_v_final_norm']
TWIN_LEAF_KINDS = {'loss': 'loss', 'grad_x': 'grad_x', 'grad_norm_ffn1': 'grad_w', 'grad_ffn1_w_up': 'grad_w', 'grad_ffn1_w_down': 'grad_w', 'grad_norm_mix': 'grad_w', 'grad_w_in': 'grad_w', 'grad_pool_w': 'grad_w', 'grad_pool_b': 'grad_w', 'grad_pool_scale': 'grad_w', 'grad_w_pool_up': 'grad_w', 'grad_conv_w': 'grad_w', 'grad_conv_b': 'grad_w', 'grad_lru_w_a': 'grad_w', 'grad_lru_b_a': 'grad_w', 'grad_lru_w_x': 'grad_w', 'grad_lru_b_x': 'grad_w', 'grad_lru_lambda': 'grad_w', 'grad_w_lru_up': 'grad_w', 'grad_w_out': 'grad_w', 'grad_norm_ffn2': 'grad_w', 'grad_ffn2_w_up': 'grad_w', 'grad_ffn2_w_down': 'grad_w', 'grad_final_norm': 'grad_w', 'delta_norm_ffn1': 'delta_w', 'delta_ffn1_w_up': 'delta_w', 'delta_ffn1_w_down': 'delta_w', 'delta_norm_mix': 'delta_w', 'delta_w_in': 'delta_w', 'delta_pool_w': 'delta_w', 'delta_pool_b': 'delta_w', 'delta_pool_scale': 'delta_w', 'delta_w_pool_up': 'delta_w', 'delta_conv_w': 'delta_w', 'delta_conv_b': 'delta_w', 'delta_lru_w_a': 'delta_w', 'delta_lru_b_a': 'delta_w', 'delta_lru_w_x': 'delta_w', 'delta_lru_b_x': 'delta_w', 'delta_lru_lambda': 'delta_w', 'delta_w_lru_up': 'delta_w', 'delta_w_out': 'delta_w', 'delta_norm_ffn2': 'delta_w', 'delta_ffn2_w_up': 'delta_w', 'delta_ffn2_w_down': 'delta_w', 'delta_final_norm': 'delta_w', 'new_m_norm_ffn1': 'new_m', 'new_m_ffn1_w_up': 'new_m', 'new_m_ffn1_w_down': 'new_m', 'new_m_norm_mix': 'new_m', 'new_m_w_in': 'new_m', 'new_m_pool_w': 'new_m', 'new_m_pool_b': 'new_m', 'new_m_pool_scale': 'new_m', 'new_m_w_pool_up': 'new_m', 'new_m_conv_w': 'new_m', 'new_m_conv_b': 'new_m', 'new_m_lru_w_a': 'new_m', 'new_m_lru_b_a': 'new_m', 'new_m_lru_w_x': 'new_m', 'new_m_lru_b_x': 'new_m', 'new_m_lru_lambda': 'new_m', 'new_m_w_lru_up': 'new_m', 'new_m_w_out': 'new_m', 'new_m_norm_ffn2': 'new_m', 'new_m_ffn2_w_up': 'new_m', 'new_m_ffn2_w_down': 'new_m', 'new_m_final_norm': 'new_m', 'new_v_norm_ffn1': 'new_v', 'new_v_ffn1_w_up': 'new_v', 'new_v_ffn1_w_down': 'new_v', 'new_v_norm_mix': 'new_v', 'new_v_w_in': 'new_v', 'new_v_pool_w': 'new_v', 'new_v_pool_b': 'new_v', 'new_v_pool_scale': 'new_v', 'new_v_w_pool_up': 'new_v', 'new_v_conv_w': 'new_v', 'new_v_conv_b': 'new_v', 'new_v_lru_w_a': 'new_v', 'new_v_lru_b_a': 'new_v', 'new_v_lru_w_x': 'new_v', 'new_v_lru_b_x': 'new_v', 'new_v_lru_lambda': 'new_v', 'new_v_w_lru_up': 'new_v', 'new_v_w_out': 'new_v', 'new_v_norm_ffn2': 'new_v', 'new_v_ffn2_w_up': 'new_v', 'new_v_ffn2_w_down': 'new_v', 'new_v_final_norm': 'new_v'}


def _forward(args):
    return _fwd_reference(*[args[k] for k in FWD_PARAMS])


def _output_shape():
    out = _jax.eval_shape(lambda: _forward(_fwd_setup_inputs(0)))
    return out.shape, out.dtype

N_MICROBATCH = 1
ADAM_LR = 0.001
ADAM_B1 = 0.9
ADAM_B2 = 0.999
ADAM_EPS = 1e-08
ADAM_WD = 0.01
ADAM_STEP = 10
PER_EXAMPLE_BATCH_AXIS = {'x': 0, 'loss_target': 0}
SHARED_INPUTS = []
_WEIGHT_DTYPES = {'norm_ffn1': _jnp.float32, 'ffn1_w_up': _jnp.float32, 'ffn1_w_down': _jnp.float32, 'norm_mix': _jnp.float32, 'w_in': _jnp.float32, 'pool_w': _jnp.float32, 'pool_b': _jnp.float32, 'pool_scale': _jnp.float32, 'w_pool_up': _jnp.float32, 'conv_w': _jnp.float32, 'conv_b': _jnp.float32, 'lru_w_a': _jnp.float32, 'lru_b_a': _jnp.float32, 'lru_w_x': _jnp.float32, 'lru_b_x': _jnp.float32, 'lru_lambda': _jnp.float32, 'w_lru_up': _jnp.float32, 'w_out': _jnp.float32, 'norm_ffn2': _jnp.float32, 'ffn2_w_up': _jnp.float32, 'ffn2_w_down': _jnp.float32, 'final_norm': _jnp.float32}
MOMENT_SCALE = {'norm_ffn1': 5.136649e-02, 'ffn1_w_up': 2.140696e-02, 'ffn1_w_down': 3.491050e-02, 'norm_mix': 6.399274e-02, 'w_in': 2.935980e-02, 'pool_w': 7.068994e-02, 'pool_b': 1.468270e-01, 'pool_scale': 6.888277e-02, 'w_pool_up': 4.983153e-02, 'conv_w': 2.605641e-02, 'conv_b': 2.746328e-01, 'lru_w_a': 7.801586e-03, 'lru_b_a': 6.491118e-03, 'lru_w_x': 1.398923e-02, 'lru_b_x': 9.551401e-03, 'lru_lambda': 1.263369e-02, 'w_lru_up': 2.846612e-02, 'w_out': 5.670945e-02, 'norm_ffn2': 4.336204e-02, 'ffn2_w_up': 1.821975e-02, 'ffn2_w_down': 2.975034e-02, 'final_norm': 1.601480e+01}


def _to_microbatches(a, axis):
    t = _jnp.moveaxis(a, axis, 0)
    t = t.reshape((N_MICROBATCH, t.shape[0] // N_MICROBATCH) + t.shape[1:])
    return _jnp.moveaxis(t, 1, axis + 1)


def setup_inputs(seed: int = 0) -> dict:
    inp = _fwd_setup_inputs(seed)
    key = _jax.random.fold_in(_jax.random.key(seed), 7919)
    shape, _ = _output_shape()
    out = dict(inp)
    out["loss_target"] = _jax.random.normal(_jax.random.fold_in(key, 0), shape, _jnp.float32)
    for i, name in enumerate(TWIN_WEIGHTS):
        w = inp[name].astype(_jnp.float32)
        if MOMENT_SCALE is None:
            s = _jnp.sqrt(_jnp.mean(_jnp.square(w)) + 1e-30)
        else:
            s = MOMENT_SCALE[name]
        km, kv = _jax.random.split(_jax.random.fold_in(key, i + 1))
        out[name] = w
        out["m_" + name] = s * _jax.random.normal(km, w.shape, _jnp.float32)
        out["v_" + name] = (s * s) * _jax.random.uniform(kv, w.shape, _jnp.float32, 0.5, 1.5)
    if N_MICROBATCH > 1:
        for name, axis in PER_EXAMPLE_BATCH_AXIS.items():
            out[name] = _to_microbatches(out[name], axis)
    return {'x': out['x'], 'norm_ffn1': out['norm_ffn1'], 'ffn1_w_up': out['ffn1_w_up'], 'ffn1_w_down': out['ffn1_w_down'], 'norm_mix': out['norm_mix'], 'w_in': out['w_in'], 'pool_w': out['pool_w'], 'pool_b': out['pool_b'], 'pool_scale': out['pool_scale'], 'w_pool_up': out['w_pool_up'], 'conv_w': out['conv_w'], 'conv_b': out['conv_b'], 'lru_w_a': out['lru_w_a'], 'lru_b_a': out['lru_b_a'], 'lru_w_x': out['lru_w_x'], 'lru_b_x': out['lru_b_x'], 'lru_lambda': out['lru_lambda'], 'w_lru_up': out['w_lru_up'], 'w_out': out['w_out'], 'norm_ffn2': out['norm_ffn2'], 'ffn2_w_up': out['ffn2_w_up'], 'ffn2_w_down': out['ffn2_w_down'], 'final_norm': out['final_norm'], 'loss_target': out['loss_target'], 'm_norm_ffn1': out['m_norm_ffn1'], 'm_ffn1_w_up': out['m_ffn1_w_up'], 'm_ffn1_w_down': out['m_ffn1_w_down'], 'm_norm_mix': out['m_norm_mix'], 'm_w_in': out['m_w_in'], 'm_pool_w': out['m_pool_w'], 'm_pool_b': out['m_pool_b'], 'm_pool_scale': out['m_pool_scale'], 'm_w_pool_up': out['m_w_pool_up'], 'm_conv_w': out['m_conv_w'], 'm_conv_b': out['m_conv_b'], 'm_lru_w_a': out['m_lru_w_a'], 'm_lru_b_a': out['m_lru_b_a'], 'm_lru_w_x': out['m_lru_w_x'], 'm_lru_b_x': out['m_lru_b_x'], 'm_lru_lambda': out['m_lru_lambda'], 'm_w_lru_up': out['m_w_lru_up'], 'm_w_out': out['m_w_out'], 'm_norm_ffn2': out['m_norm_ffn2'], 'm_ffn2_w_up': out['m_ffn2_w_up'], 'm_ffn2_w_down': out['m_ffn2_w_down'], 'm_final_norm': out['m_final_norm'], 'v_norm_ffn1': out['v_norm_ffn1'], 'v_ffn1_w_up': out['v_ffn1_w_up'], 'v_ffn1_w_down': out['v_ffn1_w_down'], 'v_norm_mix': out['v_norm_mix'], 'v_w_in': out['v_w_in'], 'v_pool_w': out['v_pool_w'], 'v_pool_b': out['v_pool_b'], 'v_pool_scale': out['v_pool_scale'], 'v_w_pool_up': out['v_w_pool_up'], 'v_conv_w': out['v_conv_w'], 'v_conv_b': out['v_conv_b'], 'v_lru_w_a': out['v_lru_w_a'], 'v_lru_b_a': out['v_lru_b_a'], 'v_lru_w_x': out['v_lru_w_x'], 'v_lru_b_x': out['v_lru_b_x'], 'v_lru_lambda': out['v_lru_lambda'], 'v_w_lru_up': out['v_w_lru_up'], 'v_w_out': out['v_w_out'], 'v_norm_ffn2': out['v_norm_ffn2'], 'v_ffn2_w_up': out['v_ffn2_w_up'], 'v_ffn2_w_down': out['v_ffn2_w_down'], 'v_final_norm': out['v_final_norm']}


def _loss(weights, diff, rest, loss_target):
    with _jax.named_scope("forward"):
        args = {**rest, TWIN_DIFF_INPUT: diff, **{k: w.astype(_WEIGHT_DTYPES[k]) for k, w in weights.items()}}
        y = _forward(args)
    with _jax.named_scope("loss_head"):
        err = _jnp.square(y.astype(_jnp.float32) - loss_target)
        return 0.5 * _jnp.sum(_jnp.mean(err, axis=-1)) if err.ndim else 0.5 * err


def _adamw(w, g, m, v):
    m = ADAM_B1 * m + (1.0 - ADAM_B1) * g
    v = ADAM_B2 * v + (1.0 - ADAM_B2) * _jnp.square(g)
    m_hat = m / (1.0 - ADAM_B1 ** ADAM_STEP)
    v_hat = v / (1.0 - ADAM_B2 ** ADAM_STEP)
    delta = -ADAM_LR * (m_hat / (_jnp.sqrt(v_hat) + ADAM_EPS) + ADAM_WD * w)
    return delta, m, v


def reference(x, norm_ffn1, ffn1_w_up, ffn1_w_down, norm_mix, w_in, pool_w, pool_b, pool_scale, w_pool_up, conv_w, conv_b, lru_w_a, lru_b_a, lru_w_x, lru_b_x, lru_lambda, w_lru_up, w_out, norm_ffn2, ffn2_w_up, ffn2_w_down, final_norm, loss_target, m_norm_ffn1, m_ffn1_w_up, m_ffn1_w_down, m_norm_mix, m_w_in, m_pool_w, m_pool_b, m_pool_scale, m_w_pool_up, m_conv_w, m_conv_b, m_lru_w_a, m_lru_b_a, m_lru_w_x, m_lru_b_x, m_lru_lambda, m_w_lru_up, m_w_out, m_norm_ffn2, m_ffn2_w_up, m_ffn2_w_down, m_final_norm, v_norm_ffn1, v_ffn1_w_up, v_ffn1_w_down, v_norm_mix, v_w_in, v_pool_w, v_pool_b, v_pool_scale, v_w_pool_up, v_conv_w, v_conv_b, v_lru_w_a, v_lru_b_a, v_lru_w_x, v_lru_b_x, v_lru_lambda, v_w_lru_up, v_w_out, v_norm_ffn2, v_ffn2_w_up, v_ffn2_w_down, v_final_norm):
    given = dict(x=x, norm_ffn1=norm_ffn1, ffn1_w_up=ffn1_w_up, ffn1_w_down=ffn1_w_down, norm_mix=norm_mix, w_in=w_in, pool_w=pool_w, pool_b=pool_b, pool_scale=pool_scale, w_pool_up=w_pool_up, conv_w=conv_w, conv_b=conv_b, lru_w_a=lru_w_a, lru_b_a=lru_b_a, lru_w_x=lru_w_x, lru_b_x=lru_b_x, lru_lambda=lru_lambda, w_lru_up=w_lru_up, w_out=w_out, norm_ffn2=norm_ffn2, ffn2_w_up=ffn2_w_up, ffn2_w_down=ffn2_w_down, final_norm=final_norm, loss_target=loss_target, m_norm_ffn1=m_norm_ffn1, m_ffn1_w_up=m_ffn1_w_up, m_ffn1_w_down=m_ffn1_w_down, m_norm_mix=m_norm_mix, m_w_in=m_w_in, m_pool_w=m_pool_w, m_pool_b=m_pool_b, m_pool_scale=m_pool_scale, m_w_pool_up=m_w_pool_up, m_conv_w=m_conv_w, m_conv_b=m_conv_b, m_lru_w_a=m_lru_w_a, m_lru_b_a=m_lru_b_a, m_lru_w_x=m_lru_w_x, m_lru_b_x=m_lru_b_x, m_lru_lambda=m_lru_lambda, m_w_lru_up=m_w_lru_up, m_w_out=m_w_out, m_norm_ffn2=m_norm_ffn2, m_ffn2_w_up=m_ffn2_w_up, m_ffn2_w_down=m_ffn2_w_down, m_final_norm=m_final_norm, v_norm_ffn1=v_norm_ffn1, v_ffn1_w_up=v_ffn1_w_up, v_ffn1_w_down=v_ffn1_w_down, v_norm_mix=v_norm_mix, v_w_in=v_w_in, v_pool_w=v_pool_w, v_pool_b=v_pool_b, v_pool_scale=v_pool_scale, v_w_pool_up=v_w_pool_up, v_conv_w=v_conv_w, v_conv_b=v_conv_b, v_lru_w_a=v_lru_w_a, v_lru_b_a=v_lru_b_a, v_lru_w_x=v_lru_w_x, v_lru_b_x=v_lru_b_x, v_lru_lambda=v_lru_lambda, v_w_lru_up=v_w_lru_up, v_w_out=v_w_out, v_norm_ffn2=v_norm_ffn2, v_ffn2_w_up=v_ffn2_w_up, v_ffn2_w_down=v_ffn2_w_down, v_final_norm=v_final_norm)
    weights = {n: given[n] for n in TWIN_WEIGHTS}
    shared = {n: given[n] for n in SHARED_INPUTS}
    per_example = {n: given[n] for n in ['x']}
    grad_fn = _jax.value_and_grad(_loss, argnums=(0, 1))

    def one_microbatch(ex, loss_target):
        ex = dict(ex)
        diff = ex.pop(TWIN_DIFF_INPUT)
        return grad_fn(weights, diff, {**shared, **ex}, loss_target)

    if N_MICROBATCH == 1:
        loss, (grad_w, grad_x) = one_microbatch(per_example, given["loss_target"])
    else:
        def body(carry, xs):
            loss_sum, grad_sum = carry
            l_k, (gw_k, gx_k) = one_microbatch(xs[0], xs[1])
            with _jax.named_scope("update"):
                return (loss_sum + l_k, _jax.tree.map(_jnp.add, grad_sum, gw_k)), gx_k

        init = (_jnp.zeros((), _jnp.float32), _jax.tree.map(_jnp.zeros_like, weights))
        (loss, grad_w), grad_x = _jax.lax.scan(body, init, (per_example, given["loss_target"]))
    with _jax.named_scope("update"):
        delta_w, new_m, new_v = {}, {}, {}
        for n in TWIN_WEIGHTS:
            delta_w[n], new_m[n], new_v[n] = _adamw(weights[n], grad_w[n], given["m_" + n], given["v_" + n])
    return (loss, grad_x, *[grad_w[n] for n in TWIN_WEIGHTS], *[delta_w[n] for n in TWIN_WEIGHTS],
            *[new_m[n] for n in TWIN_WEIGHTS], *[new_v[n] for n in TWIN_WEIGHTS])
```

```python
import functools

import jax
import jax.numpy as jnp
from jax import lax
from jax.experimental import pallas as pl
from jax.experimental.pallas import tpu as pltpu

F32, BF16 = jnp.float32, jnp.bfloat16
EPS = 1e-6
LRU_C = 8.0
POOL_WINDOWS = (2, 4, 8, 16)
ADAM_LR, ADAM_B1, ADAM_B2, ADAM_EPS, ADAM_WD, ADAM_STEP = 0.001, 0.9, 0.999, 1e-08, 0.01, 10
N_DEV = 8
N_CHIP = 4
MESH = pl.DeviceIdType.MESH
V7X_VMEM_LIMIT = 56 * 1024 * 1024
ROW_TILE = 512
ANY = pl.BlockSpec(memory_space=pl.ANY)

_pallas_call = pl.pallas_call


def _cp(*sem):
    return pltpu.CompilerParams(dimension_semantics=sem if sem else None, vmem_limit_bytes=V7X_VMEM_LIMIT)


def _tile(n, t=ROW_TILE):
    t = min(n, t)
    assert n % t == 0, (n, t)
    return t


def _dot(a, b):
    return jnp.dot(a, b, preferred_element_type=F32)


def _dot_nt(a, b):
    return lax.dot_general(a, b, (((1,), (1,)), ((), ())), preferred_element_type=F32)


def _dot_tn(a, b):
    return lax.dot_general(a, b, (((0,), (0,)), ((), ())), preferred_element_type=F32)


def _rms(xv):
    r = lax.rsqrt(jnp.mean(xv * xv, axis=-1, keepdims=True) + EPS)
    return xv * r, r


def _rms_bwd(dh, xv, gv, dy):
    n, r = _rms(xv)
    dn = dh * gv
    dx = dy + r * (dn - n * jnp.mean(dn * n, axis=-1, keepdims=True))
    return dx, jnp.sum(dh * n, axis=0, keepdims=True)


def _shift_down(x, k, fill=0.0):
    if k == 0:
        return x
    rows = lax.broadcasted_iota(jnp.int32, x.shape, 0)
    return jnp.where(rows >= k, pltpu.roll(x, k, 0), fill)


def _shift_up(x, k, fill=0.0):
    if k == 0:
        return x
    n = x.shape[0]
    rows = lax.broadcasted_iota(jnp.int32, x.shape, 0)
    return jnp.where(rows < n - k, pltpu.roll(x, n - k, 0), fill)


def _sigmoid(x):
    return 1.0 / (1.0 + jnp.exp(-x))


_GELU_K = 0.7978845608028654
_GELU_C = 0.044715


def _gelu(x):
    th = jnp.tanh(_GELU_K * (x + _GELU_C * x * x * x))
    return 0.5 * x * (1.0 + th), th


def _gelu_grad(x, th):
    return 0.5 * (1.0 + th) + 0.5 * x * (1.0 - th * th) * _GELU_K * (1.0 + 3.0 * _GELU_C * x * x)


def ffn_up(x, g, wup, l):
    T, D = x.shape
    cs = wup.shape[-1]
    tm = _tile(T)

    def body(x_ref, g_ref, wa_ref, wb_ref, h_ref, a_ref, b_ref, s_ref, hs_ref):
        @pl.when(pl.program_id(1) == 0)
        def _():
            n, _r = _rms(x_ref[...])
            hv = (n * g_ref[0]).astype(BF16)
            hs_ref[...] = hv
            h_ref[...] = hv

        hv = hs_ref[...]
        a = _dot(hv, wa_ref[0, 0])
        b = _dot(hv, wb_ref[0, 0])
        a_ref[0] = a.astype(BF16)
        b_ref[0] = b.astype(BF16)
        s_ref[0] = (a * _sigmoid(a) * b).astype(BF16)

    blk = pl.BlockSpec((1, tm, cs), lambda i, j: (j, i, 0))
    return _pallas_call(
        body, name="ffn_up", grid=(T // tm, 4),
        in_specs=[pl.BlockSpec((tm, D), lambda i, j: (i, 0)), pl.BlockSpec((1, 1, D), lambda i, j: (l, 0, 0)),
                  pl.BlockSpec((1, 1, D, cs), lambda i, j: (l, j, 0, 0)), pl.BlockSpec((1, 1, D, cs), lambda i, j: (l, j + 4, 0, 0))],
        out_specs=[pl.BlockSpec((tm, D), lambda i, j: (i, 0)), blk, blk, blk],
        out_shape=[jax.ShapeDtypeStruct((T, D), BF16)] + [jax.ShapeDtypeStruct((4, T, cs), BF16)] * 3,
        scratch_shapes=[pltpu.VMEM((tm, D), BF16)],
        compiler_params=_cp("parallel", "arbitrary"),
    )(x, g, wup, wup)


def ffn_down(s, wd, x, l):
    _, T, cs = s.shape
    D = x.shape[1]
    tm = _tile(T)

    def body(s_ref, w_ref, x_ref, o_ref, acc_ref):
        j = pl.program_id(1)

        @pl.when(j == 0)
        def _():
            acc_ref[...] = jnp.zeros_like(acc_ref)

        acc_ref[...] += _dot(s_ref[0], w_ref[0])

        @pl.when(j == 3)
        def _():
            o_ref[...] = x_ref[...] + 0.5 * acc_ref[...]

    return _pallas_call(
        body, name="ffn_down", grid=(T // tm, 4),
        in_specs=[pl.BlockSpec((1, tm, cs), lambda i, j: (j, i, 0)), pl.BlockSpec((1, cs, D), lambda i, j: (l, j, 0)),
                  pl.BlockSpec((tm, D), lambda i, j: (i, 0))],
        out_specs=pl.BlockSpec((tm, D), lambda i, j: (i, 0)),
        out_shape=jax.ShapeDtypeStruct((T, D), F32),
        scratch_shapes=[pltpu.VMEM((tm, D), F32)],
        compiler_params=_cp("parallel", "arbitrary"),
    )(s, wd, x)


def mix_in(x, g, win, l):
    T, D = x.shape
    ci = win.shape[-1]
    tm = _tile(T)

    def body(x_ref, g_ref, w_ref, h_ref, p_ref, hs_ref):
        @pl.when(pl.program_id(1) == 0)
        def _():
            n, _r = _rms(x_ref[...])
            hv = (n * g_ref[0]).astype(BF16)
            hs_ref[...] = hv
            h_ref[...] = hv

        p_ref[...] = _dot(hs_ref[...], w_ref[0, 0]).astype(BF16)

    return _pallas_call(
        body, name="mix_in", grid=(T // tm, N_DEV),
        in_specs=[pl.BlockSpec((tm, D), lambda i, j: (i, 0)), pl.BlockSpec((1, 1, D), lambda i, j: (l, 0, 0)),
                  pl.BlockSpec((1, 1, D, ci), lambda i, j: (l, j, 0, 0))],
        out_specs=[pl.BlockSpec((tm, D), lambda i, j: (i, 0)), pl.BlockSpec((tm, ci), lambda i, j: (i, j))],
        out_shape=[jax.ShapeDtypeStruct((T, D), BF16), jax.ShapeDtypeStruct((T, N_DEV * ci), BF16)],
        scratch_shapes=[pltpu.VMEM((tm, D), BF16)],
        compiler_params=_cp("parallel", "arbitrary"),
    )(x, g, win)


def _inv_count(T, w):
    t = lax.broadcasted_iota(jnp.int32, (T, 1), 0)
    return 1.0 / jnp.minimum(t + 1, w).astype(F32)


def _pooled(ug, w, inv):
    s = ug
    k = 1
    while k < w:
        s = s + _shift_down(s, k)
        k *= 2
    return s * inv - ug


def pool_fwd(proj, pw, pb, ps, l):
    T = proj.shape[0]
    _, G, gd, _ = pw.shape
    P = G * gd

    def body(u_ref, w_ref, b_ref, s_ref, o_ref):
        for gi in range(G):
            cols = slice(gi * gd, (gi + 1) * gd)
            ug = u_ref[:, cols].astype(F32)
            pooled = _pooled(ug, POOL_WINDOWS[gi], _inv_count(T, POOL_WINDOWS[gi]))
            mixed = _dot(pooled.astype(BF16), w_ref[0, gi].astype(BF16)) + b_ref[0, :, cols]
            o_ref[:, cols] = (mixed * s_ref[0, :, cols]).astype(BF16)

    return _pallas_call(
        body, name="pool_fwd", grid=(1,),
        in_specs=[pl.BlockSpec((T, P), lambda i: (0, 0)), pl.BlockSpec((1, G, gd, gd), lambda i: (l, 0, 0, 0)),
                  pl.BlockSpec((1, 1, P), lambda i: (l, 0, 0)), pl.BlockSpec((1, 1, P), lambda i: (l, 0, 0))],
        out_specs=pl.BlockSpec((T, P), lambda i: (0, 0)),
        out_shape=jax.ShapeDtypeStruct((T, P), BF16),
        compiler_params=_cp("arbitrary"),
    )(proj, pw, pb, ps)


def _conv(u, cw_ref, cb):
    CW = cw_ref.shape[1]
    v = cb
    for k in range(CW):
        v = v + cw_ref[0, k:k + 1, :] * _shift_down(u, CW - 1 - k)
    return v


def _softplus(z):
    return jnp.maximum(z, 0.0) + jnp.log1p(jnp.exp(-jnp.abs(z)))


def _lru_gates(v, wa_ref, ba, wx_ref, bx, lam):
    vb = v.astype(BF16)
    r = _sigmoid(_dot(vb, wa_ref[0, 0].astype(BF16)) + ba)
    i = _sigmoid(_dot(vb, wx_ref[0, 0].astype(BF16)) + bx)
    sp = _softplus(-lam)
    log_a = -LRU_C * r * sp
    a = jnp.exp(log_a)
    mult = jnp.sqrt(-jnp.tanh(log_a) * (a * a + 1.0))
    return r, i, sp, a, mult


def _scan_fwd(a_ref, b_ref, o_ref):
    T, W = a_ref.shape
    rows = lax.broadcasted_iota(jnp.int32, (8, W), 0)

    def step(t, carry):
        r0 = pl.multiple_of(t * 8, 8)
        A = a_ref[pl.ds(r0, 8), :]
        B = b_ref[pl.ds(r0, 8), :]
        for s in (1, 2, 4):
            keep = rows >= s
            As = jnp.where(keep, pltpu.roll(A, s, 0), 1.0)
            Bs = jnp.where(keep, pltpu.roll(B, s, 0), 0.0)
            B = A * Bs + B
            A = A * As
        h = B + A * carry
        o_ref[pl.ds(r0, 8), :] = h
        return jnp.broadcast_to(h[7:8, :], (8, W))

    lax.fori_loop(0, T // 8, step, jnp.zeros((8, W), F32), unroll=8)


def _scan_bwd(a_ref, b_ref, o_ref):
    T, W = a_ref.shape
    rows = lax.broadcasted_iota(jnp.int32, (8, W), 0)
    nt = T // 8

    def step(t, carry):
        r0 = pl.multiple_of((nt - 1 - t) * 8, 8)
        A = a_ref[pl.ds(r0, 8), :]
        B = b_ref[pl.ds(r0, 8), :]
        for s in (1, 2, 4):
            keep = rows < 8 - s
            As = jnp.where(keep, pltpu.roll(A, 8 - s, 0), 1.0)
            Bs = jnp.where(keep, pltpu.roll(B, 8 - s, 0), 0.0)
            B = A * Bs + B
            A = A * As
        y = B + A * carry
        o_ref[pl.ds(r0, 8), :] = y
        return jnp.broadcast_to(y[0:1, :], (8, W))

    lax.fori_loop(0, nt, step, jnp.zeros((8, W), F32), unroll=8)


def _lru_specs(T, hd, P, R, CW, l):
    ob, gb = P // hd, (P + R) // hd
    vec = pl.BlockSpec((1, 1, hd), lambda h: (l, 0, h))
    mat = pl.BlockSpec((1, 1, hd, hd), lambda h: (l, h, 0, 0))
    return [pl.BlockSpec((T, hd), lambda h: (0, ob + h)), pl.BlockSpec((T, hd), lambda h: (0, gb + h)),
            pl.BlockSpec((1, CW, hd), lambda h: (l, 0, h)), vec, mat, vec, mat, vec, vec]


def lru_fwd(proj, cw, cb, wa, ba, wx, bx, lam, P, l):
    T = proj.shape[0]
    _, H, hd, _ = wa.shape
    R = H * hd
    CW = cw.shape[1]
    assert P % hd == 0 and T % 8 == 0

    def body(u_ref, ug_ref, cw_ref, cb_ref, wa_ref, ba_ref, wx_ref, bx_ref, lam_ref, hl_ref, hs_ref, a_s, b_s):
        v = _conv(u_ref[...].astype(F32), cw_ref, cb_ref[0])
        _r, i, _sp, a, mult = _lru_gates(v, wa_ref, ba_ref[0], wx_ref, bx_ref[0], lam_ref[0])
        a_s[...] = a
        b_s[...] = mult * (i * v)
        _scan_fwd(a_s, b_s, hs_ref)
        ge, _th = _gelu(ug_ref[...].astype(F32))
        hl_ref[...] = (hs_ref[...] * ge).astype(BF16)

    out = pl.BlockSpec((T, hd), lambda h: (0, h))
    return _pallas_call(
        body, name="lru_fwd", grid=(H,),
        in_specs=_lru_specs(T, hd, P, R, CW, l),
        out_specs=[out, out],
        out_shape=[jax.ShapeDtypeStruct((T, R), BF16), jax.ShapeDtypeStruct((T, R), F32)],
        scratch_shapes=[pltpu.VMEM((T, hd), F32)] * 2,
        compiler_params=_cp("parallel"),
    )(proj, proj, cw, cb, wa, ba, wx, bx, lam)


def mix_out(pm, hl, proj, x, wpu, wlu, wout, P, l):
    T, D = x.shape
    R = hl.shape[1]
    tm = _tile(T)
    assert (P + 2 * R) % D == 0
    gb = (P + 2 * R) // D

    def body(pm_ref, hl_ref, gp_ref, gl_ref, x_ref, wpu_ref, wlu_ref, wo_ref, o_ref, yp_ref, yl_ref, z_ref):
        yp = _dot(pm_ref[...], wpu_ref[0])
        yl = _dot(hl_ref[...], wlu_ref[0])
        z = (_sigmoid(gp_ref[...].astype(F32)) * yp + _sigmoid(gl_ref[...].astype(F32)) * yl).astype(BF16)
        yp_ref[...] = yp.astype(BF16)
        yl_ref[...] = yl.astype(BF16)
        z_ref[...] = z
        o_ref[...] = x_ref[...] + _dot(z, wo_ref[0])

    row = lambda w: pl.BlockSpec((tm, w), lambda i: (i, 0))
    return _pallas_call(
        body, name="mix_out", grid=(T // tm,),
        in_specs=[row(P), row(R), pl.BlockSpec((tm, D), lambda i: (i, gb)), pl.BlockSpec((tm, D), lambda i: (i, gb + 1)), row(D),
                  pl.BlockSpec((1, P, D), lambda i: (l, 0, 0)), pl.BlockSpec((1, R, D), lambda i: (l, 0, 0)),
                  pl.BlockSpec((1, D, D), lambda i: (l, 0, 0))],
        out_specs=[row(D)] * 4,
        out_shape=[jax.ShapeDtypeStruct((T, D), F32)] + [jax.ShapeDtypeStruct((T, D), BF16)] * 3,
        compiler_params=_cp("parallel"),
    )(pm, hl, proj, proj, x, wpu, wlu, wout)


def loss_head(x, gf, tgt):
    T, D = x.shape
    tm = _tile(T)

    def body(x_ref, g_ref, t_ref, loss_ref, dx_ref, dg_ref):
        @pl.when(pl.program_id(0) == 0)
        def _():
            loss_ref[...] = jnp.zeros_like(loss_ref)
            dg_ref[...] = jnp.zeros_like(dg_ref)

        xv = x_ref[...]
        gv = g_ref[...]
        n, _r = _rms(xv)
        e = n * gv - t_ref[...]
        loss_ref[...] += 0.5 * jnp.sum(jnp.sum(e * e, axis=-1, keepdims=True), axis=0, keepdims=True) / D
        dx, dg = _rms_bwd(e * (1.0 / D), xv, gv, 0.0)
        dx_ref[...] = dx
        dg_ref[...] += dg

    return _pallas_call(
        body, name="loss_head", grid=(T // tm,),
        in_specs=[pl.BlockSpec((tm, D), lambda i: (i, 0)), pl.BlockSpec((1, D), lambda i: (0, 0)), pl.BlockSpec((tm, D), lambda i: (i, 0))],
        out_specs=[pl.BlockSpec((1, 1), lambda i: (0, 0)), pl.BlockSpec((tm, D), lambda i: (i, 0)), pl.BlockSpec((1, D), lambda i: (0, 0))],
        out_shape=[jax.ShapeDtypeStruct((1, 1), F32), jax.ShapeDtypeStruct((T, D), F32), jax.ShapeDtypeStruct((1, D), F32)],
        compiler_params=_cp("arbitrary"),
    )(x, gf, tgt)


def ffn_down_bwd(dy, wd, ua, ub, l):
    T, D = dy.shape
    cs = ua.shape[-1]
    tm = _tile(T)

    def body(dy_ref, w_ref, a_ref, b_ref, do_ref, da_ref, db_ref, dyb_ref):
        @pl.when(pl.program_id(1) == 0)
        def _():
            d = (0.5 * dy_ref[...]).astype(BF16)
            dyb_ref[...] = d
            do_ref[...] = d

        ds = _dot_nt(dyb_ref[...], w_ref[0])
        a = a_ref[0].astype(F32)
        b = b_ref[0].astype(F32)
        sg = _sigmoid(a)
        da_ref[0] = (ds * b * (sg * (1.0 + a * (1.0 - sg)))).astype(BF16)
        db_ref[0] = (ds * (a * sg)).astype(BF16)

    blk = pl.BlockSpec((1, tm, cs), lambda i, j: (j, i, 0))
    return _pallas_call(
        body, name="ffn_down_bwd", grid=(T // tm, 4),
        in_specs=[pl.BlockSpec((tm, D), lambda i, j: (i, 0)), pl.BlockSpec((1, cs, D), lambda i, j: (l, j, 0)), blk, blk],
        out_specs=[pl.BlockSpec((tm, D), lambda i, j: (i, 0)), blk, blk],
        out_shape=[jax.ShapeDtypeStruct((T, D), BF16)] + [jax.ShapeDtypeStruct((4, T, cs), BF16)] * 2,
        scratch_shapes=[pltpu.VMEM((tm, D), BF16)],
        compiler_params=_cp("parallel", "arbitrary"),
    )(dy, wd, ua, ub)


def dw_tn(name, a_ops, a_spec, b_ops, b_spec, G, M, N, T, l, L, prev):
    tk = _tile(T)
    nk = T // tk

    def body(*refs):
        a_refs, b_refs = refs[:len(a_ops)], refs[len(a_ops):len(a_ops) + len(b_ops)]
        o32_ref, o16_ref, acc_ref = refs[-3:]
        k = pl.program_id(1)

        @pl.when(k == 0)
        def _():
            acc_ref[...] = jnp.zeros_like(acc_ref)

        av = a_refs[0][0] if len(a_refs[0].shape) == 3 else a_refs[0][...]
        bv = b_refs[0][0] if len(b_refs[0].shape) == 3 else b_refs[0][...]
        acc_ref[...] += _dot_tn(av, bv)

        @pl.when(k == nk - 1)
        def _():
            o32_ref[0, 0] = acc_ref[...]
            o16_ref[0, 0] = acc_ref[...].astype(BF16)

    n_in = len(a_ops) + len(b_ops)
    in_specs = [a_spec(tk), b_spec(tk)]
    args = list(a_ops) + list(b_ops)
    aliases = {}
    if prev is not None:
        in_specs += [ANY, ANY]
        args += list(prev)
        aliases = {n_in: 0, n_in + 1: 1}

    def body_wrap(*refs):
        if prev is not None:
            refs = refs[:n_in] + refs[n_in + 2:]
        body(*refs)

    out = pl.BlockSpec((1, 1, M, N), lambda g, k: (l, g, 0, 0))
    return _pallas_call(
        body_wrap, name=name, grid=(G, nk),
        in_specs=in_specs, out_specs=[out, out],
        out_shape=[jax.ShapeDtypeStruct((L, G, M, N), F32), jax.ShapeDtypeStruct((L, G, M, N), BF16)],
        scratch_shapes=[pltpu.VMEM((M, N), F32)],
        input_output_aliases=aliases,
        compiler_params=_cp("parallel", "arbitrary"),
    )(*args)


def dx_norm_bwd(name, dact, d_spec, w, G, x, g, dy, l):
    T, D = x.shape
    c = w.shape[-1]
    tm = _tile(T)

    def body(d_ref, w_ref, x_ref, g_ref, dy_ref, dx_ref, dg_ref, acc_ref):
        i, j = pl.program_id(0), pl.program_id(1)

        @pl.when(jnp.logical_and(i == 0, j == 0))
        def _():
            dg_ref[...] = jnp.zeros_like(dg_ref)

        @pl.when(j == 0)
        def _():
            acc_ref[...] = jnp.zeros_like(acc_ref)

        dv = d_ref[0] if len(d_ref.shape) == 3 else d_ref[...]
        acc_ref[...] += _dot_nt(dv, w_ref[0, 0])

        @pl.when(j == G - 1)
        def _():
            dx, dg = _rms_bwd(acc_ref[...], x_ref[...], g_ref[0], dy_ref[...])
            dx_ref[...] = dx
            dg_ref[...] += dg

    row = pl.BlockSpec((tm, D), lambda i, j: (i, 0))
    return _pallas_call(
        body, name=name, grid=(T // tm, G),
        in_specs=[d_spec(tm), pl.BlockSpec((1, 1, D, c), lambda i, j: (l, j, 0, 0)), row, pl.BlockSpec((1, 1, D), lambda i, j: (l, 0, 0)), row],
        out_specs=[row, pl.BlockSpec((1, D), lambda i, j: (0, 0))],
        out_shape=[jax.ShapeDtypeStruct((T, D), F32), jax.ShapeDtypeStruct((1, D), F32)],
        scratch_shapes=[pltpu.VMEM((tm, D), F32)],
        compiler_params=_cp("arbitrary", "arbitrary"),
    )(dact, w, x, g, dy)


def mix_out_bwd(dy, proj, yp, yl, wpu, wlu, wout, P, R, l):
    T, D = dy.shape
    tm = _tile(T)
    gb = (P + 2 * R) // D

    def body(dy_ref, gp_ref, gl_ref, yp_ref, yl_ref, wpu_ref, wlu_ref, wo_ref,
             dyb_ref, dyp_ref, dyl_ref, dgp_ref, dgl_ref, dpm_ref, dhl_ref):
        dyb = dy_ref[...].astype(BF16)
        dyb_ref[...] = dyb
        dz = _dot_nt(dyb, wo_ref[0])
        sp = _sigmoid(gp_ref[...].astype(F32))
        sl = _sigmoid(gl_ref[...].astype(F32))
        dgp_ref[...] = (dz * yp_ref[...].astype(F32) * sp * (1.0 - sp)).astype(BF16)
        dgl_ref[...] = (dz * yl_ref[...].astype(F32) * sl * (1.0 - sl)).astype(BF16)
        dyp = (dz * sp).astype(BF16)
        dyl = (dz * sl).astype(BF16)
        dyp_ref[...] = dyp
        dyl_ref[...] = dyl
        dpm_ref[...] = _dot_nt(dyp, wpu_ref[0]).astype(BF16)
        dhl_ref[...] = _dot_nt(dyl, wlu_ref[0]).astype(BF16)

    row = lambda w: pl.BlockSpec((tm, w), lambda i: (i, 0))
    return _pallas_call(
        body, name="mix_out_bwd", grid=(T // tm,),
        in_specs=[row(D), pl.BlockSpec((tm, D), lambda i: (i, gb)), pl.BlockSpec((tm, D), lambda i: (i, gb + 1)), row(D), row(D),
                  pl.BlockSpec((1, P, D), lambda i: (l, 0, 0)), pl.BlockSpec((1, R, D), lambda i: (l, 0, 0)),
                  pl.BlockSpec((1, D, D), lambda i: (l, 0, 0))],
        out_specs=[row(D)] * 5 + [row(P), row(R)],
        out_shape=[jax.ShapeDtypeStruct((T, D), BF16)] * 5 + [jax.ShapeDtypeStruct((T, P), BF16), jax.ShapeDtypeStruct((T, R), BF16)],
        compiler_params=_cp("parallel"),
    )(dy, proj, proj, yp, yl, wpu, wlu, wout)


def lru_bwd(proj, hs, dhl, cw, cb, wa, ba, wx, bx, lam, P, l):
    T = proj.shape[0]
    _, H, hd, _ = wa.shape
    R = H * hd
    CW = cw.shape[1]

    def body(u_ref, ug_ref, cw_ref, cb_ref, wa_ref, ba_ref, wx_ref, bx_ref, lam_ref, hs_ref, dhl_ref,
             du_ref, dug_ref, dcw_ref, dcb_ref, dwa_ref, dba_ref, dwx_ref, dbx_ref, dlam_ref, c_s, g_s, y_s):
        u = u_ref[...].astype(F32)
        v = _conv(u, cw_ref, cb_ref[0])
        lam = lam_ref[0]
        r, i, sp, a, mult = _lru_gates(v, wa_ref, ba_ref[0], wx_ref, bx_ref[0], lam)
        ug = ug_ref[...].astype(F32)
        ge, th = _gelu(ug)
        hs = hs_ref[...]
        dhl = dhl_ref[...].astype(F32)
        dug_ref[...] = (dhl * hs * _gelu_grad(ug, th)).astype(BF16)
        c_s[...] = _shift_up(a, 1)
        g_s[...] = dhl * ge
        _scan_bwd(c_s, g_s, y_s)
        y = y_s[...]
        da = y * _shift_down(hs, 1)
        iv = i * v
        dlog_a = da * a - (y * iv) * (a * a) / mult
        div = y * mult
        dpa = (dlog_a * (-LRU_C) * sp) * r * (1.0 - r)
        dpx = (div * v) * i * (1.0 - i)
        dsp = jnp.sum(dlog_a * (-LRU_C) * r, axis=0, keepdims=True)
        dlam_ref[0] = -dsp * _sigmoid(-lam)
        vb = v.astype(BF16)
        dpab, dpxb = dpa.astype(BF16), dpx.astype(BF16)
        dwa_ref[0, 0] = _dot_tn(vb, dpab)
        dwx_ref[0, 0] = _dot_tn(vb, dpxb)
        dba_ref[0] = jnp.sum(dpa, axis=0, keepdims=True)
        dbx_ref[0] = jnp.sum(dpx, axis=0, keepdims=True)
        dv = div * i + _dot_nt(dpab, wa_ref[0, 0].astype(BF16)) + _dot_nt(dpxb, wx_ref[0, 0].astype(BF16))
        dcb_ref[0] = jnp.sum(dv, axis=0, keepdims=True)
        du = jnp.zeros_like(dv)
        for k in range(CW):
            du = du + cw_ref[0, k:k + 1, :] * _shift_up(dv, CW - 1 - k)
            dcw_ref[0, k:k + 1, :] = jnp.sum(dv * _shift_down(u, CW - 1 - k), axis=0, keepdims=True)
        du_ref[...] = du.astype(BF16)

    col = pl.BlockSpec((T, hd), lambda h: (0, h))
    vec = pl.BlockSpec((1, 1, hd), lambda h: (0, 0, h))
    mat = pl.BlockSpec((1, 1, hd, hd), lambda h: (0, h, 0, 0))
    vshape = jax.ShapeDtypeStruct((1, 1, R), F32)
    mshape = jax.ShapeDtypeStruct((1, H, hd, hd), F32)
    return _pallas_call(
        body, name="lru_bwd", grid=(H,),
        in_specs=_lru_specs(T, hd, P, R, CW, l) + [col, col],
        out_specs=[col, col, pl.BlockSpec((1, CW, hd), lambda h: (0, 0, h)), vec, mat, vec, mat, vec, vec],
        out_shape=[jax.ShapeDtypeStruct((T, R), BF16)] * 2 + [jax.ShapeDtypeStruct((1, CW, R), F32), vshape, mshape, vshape, mshape, vshape, vshape],
        scratch_shapes=[pltpu.VMEM((T, hd), F32)] * 3,
        compiler_params=_cp("parallel"),
    )(proj, proj, cw, cb, wa, ba, wx, bx, lam, hs, dhl)


def pool_bwd(proj, dpm, pw, pb, ps, l):
    T = proj.shape[0]
    _, G, gd, _ = pw.shape
    P = G * gd

    def body(u_ref, d_ref, w_ref, b_ref, s_ref, du_ref, dw_ref, db_ref, dsc_ref):
        for gi in range(G):
            cols = slice(gi * gd, (gi + 1) * gd)
            w = POOL_WINDOWS[gi]
            inv = _inv_count(T, w)
            ug = u_ref[:, cols].astype(F32)
            pooled = _pooled(ug, w, inv).astype(BF16)
            wb = w_ref[0, gi].astype(BF16)
            mixed = _dot(pooled, wb) + b_ref[0, :, cols]
            dpm_g = d_ref[:, cols].astype(F32)
            dsc_ref[0, :, cols] = jnp.sum(dpm_g * mixed, axis=0, keepdims=True)
            dmixed = dpm_g * s_ref[0, :, cols]
            db_ref[0, :, cols] = jnp.sum(dmixed, axis=0, keepdims=True)
            dmb = dmixed.astype(BF16)
            dw_ref[0, gi] = _dot_tn(pooled, dmb)
            dpooled = _dot_nt(dmb, wb)
            s = dpooled * inv
            k = 1
            while k < w:
                s = s + _shift_up(s, k)
                k *= 2
            du_ref[:, cols] = (s - dpooled).astype(BF16)

    vec = pl.BlockSpec((1, 1, P), lambda i: (l, 0, 0))
    ovec = pl.BlockSpec((1, 1, P), lambda i: (0, 0, 0))
    return _pallas_call(
        body, name="pool_bwd", grid=(1,),
        in_specs=[pl.BlockSpec((T, P), lambda i: (0, 0)), pl.BlockSpec((T, P), lambda i: (0, 0)),
                  pl.BlockSpec((1, G, gd, gd), lambda i: (l, 0, 0, 0)), vec, vec],
        out_specs=[pl.BlockSpec((T, P), lambda i: (0, 0)), pl.BlockSpec((1, G, gd, gd), lambda i: (0, 0, 0, 0)), ovec, ovec],
        out_shape=[jax.ShapeDtypeStruct((T, P), BF16), jax.ShapeDtypeStruct((1, G, gd, gd), F32),
                   jax.ShapeDtypeStruct((1, 1, P), F32), jax.ShapeDtypeStruct((1, 1, P), F32)],
        compiler_params=_cp("arbitrary"),
    )(proj, dpm, pw, pb, ps)


def _place():
    x, y, c = lax.axis_index("x"), lax.axis_index("y"), lax.axis_index("c")
    return x, y, c


def all_gather(name, shards):
    n = len(shards)

    def body(*refs):
        src, out = refs[:n], refs[n:2 * n]
        send_sems, recv_sems, local_sems = refs[2 * n:]
        x, y, c = _place()
        sibling = (x, y, 1 - c)
        chips = [(x, 1 - y), (1 - x, y), (1 - x, 1 - y)]

        def slot(a, px, py, pc):
            return out[a].at[:, 4 * px + 2 * py + pc]

        def copy(a, k, block, to, from_src=False):
            return pltpu.make_async_remote_copy(
                src_ref=src[a] if from_src else slot(a, *block), dst_ref=slot(a, *block),
                send_sem=send_sems.at[a, k], recv_sem=recv_sems.at[a, k], device_id=to, device_id_type=MESH)

        me = (x, y, c)
        mine = [pltpu.make_async_copy(src[a], slot(a, *me), local_sems.at[a]) for a in range(n)]
        first = []
        for j, chip in enumerate(chips):
            for a in range(n):
                first.append(copy(a, 1 + j, me, (*chip, c), from_src=True))
        for a in range(n):
            first.append(copy(a, 0, me, sibling, from_src=True))
        for cp in mine + first:
            cp.start()
        passed = []
        for j, chip in enumerate(chips):
            for a in range(n):
                copy(a, 1 + j, (*chip, c), me).wait_recv()
                fwd = copy(a, 4 + j, (*chip, c), sibling)
                fwd.start()
                passed.append(fwd)
        for a in range(n):
            copy(a, 0, (x, y, 1 - c), me).wait_recv()
        for j, chip in enumerate(chips):
            for a in range(n):
                copy(a, 4 + j, (*chip, 1 - c), me).wait_recv()
        for cp in first + passed:
            cp.wait_send()
        for cp in mine:
            cp.wait()

    outs = _pallas_call(
        body, name=name,
        in_specs=[ANY] * n, out_specs=[ANY] * n,
        out_shape=[jax.ShapeDtypeStruct((s.shape[0], N_DEV) + s.shape[1:], s.dtype) for s in shards],
        scratch_shapes=[pltpu.SemaphoreType.DMA((n, 7)), pltpu.SemaphoreType.DMA((n, 7)), pltpu.SemaphoreType.DMA((n,))],
        compiler_params=pltpu.CompilerParams(has_side_effects=True),
    )(*shards)
    return list(outs)


def pair_exchange(name, g32, g16):
    n = len(g32)

    def body(*refs):
        s32, s16 = refs[:n], refs[n:2 * n]
        recv, mine = refs[2 * n:3 * n], refs[3 * n:4 * n]
        send_sems, recv_sems, local_sems = refs[4 * n:]
        x, y, c = _place()
        sibling = (x, y, 1 - c)
        rem, loc = [], []
        for a in range(n):
            for j in range(N_CHIP):
                rem.append(pltpu.make_async_remote_copy(
                    src_ref=s16[a].at[:, 2 * j + 1 - c], dst_ref=recv[a].at[:, j],
                    send_sem=send_sems.at[a, j], recv_sem=recv_sems.at[a, j], device_id=sibling, device_id_type=MESH))
                loc.append(pltpu.make_async_copy(s32[a].at[:, 2 * j + c], mine[a].at[:, j], local_sems.at[a, j]))
        for cp in rem + loc:
            cp.start()
        for cp in rem:
            cp.wait_recv()
        for cp in rem:
            cp.wait_send()
        for cp in loc:
            cp.wait()

    def quarter(s):
        return (s.shape[0], N_CHIP) + s.shape[2:]

    outs = _pallas_call(
        body, name=name,
        in_specs=[ANY] * (2 * n), out_specs=[ANY] * (2 * n),
        out_shape=[jax.ShapeDtypeStruct(quarter(s), s.dtype) for s in g16] + [jax.ShapeDtypeStruct(quarter(s), s.dtype) for s in g32],
        scratch_shapes=[pltpu.SemaphoreType.DMA((n, N_CHIP))] * 3,
        compiler_params=pltpu.CompilerParams(has_side_effects=True),
    )(*g32, *g16)
    return list(outs[:n]), list(outs[n:])


def chip_exchange(name, pair16, mine, recv1):
    n = len(pair16)

    def body(*refs):
        p16, m32, r1 = refs[:n], refs[n:2 * n], refs[2 * n:3 * n]
        recv2, own32, own16 = refs[3 * n:4 * n], refs[4 * n:5 * n], refs[5 * n:6 * n]
        send_sems, recv_sems, local_sems = refs[6 * n:]
        x, y, c = _place()
        j0 = 2 * x + y
        rem, loc = [], []
        for d in (1, 2, 3):
            px = 1 - x if d & 2 else x
            py = 1 - y if d & 1 else y
            for a in range(n):
                rem.append(pltpu.make_async_remote_copy(
                    src_ref=p16[a].at[:, 2 * px + py], dst_ref=recv2[a].at[:, d - 1],
                    send_sem=send_sems.at[a, d - 1], recv_sem=recv_sems.at[a, d - 1], device_id=(px, py, c), device_id_type=MESH))
        for a in range(n):
            loc.append(pltpu.make_async_copy(m32[a].at[:, j0], own32[a], local_sems.at[a, 0]))
            loc.append(pltpu.make_async_copy(r1[a].at[:, j0], own16[a], local_sems.at[a, 1]))
        for cp in rem + loc:
            cp.start()
        for cp in rem:
            cp.wait_recv()
        for cp in rem:
            cp.wait_send()
        for cp in loc:
            cp.wait()

    def drop(s, k=None):
        return (s.shape[0],) + (() if k is None else (k,)) + s.shape[2:]

    outs = _pallas_call(
        body, name=name,
        in_specs=[ANY] * (3 * n), out_specs=[ANY] * (3 * n),
        out_shape=[jax.ShapeDtypeStruct(drop(s, 3), s.dtype) for s in pair16]
        + [jax.ShapeDtypeStruct(drop(s), s.dtype) for s in mine] + [jax.ShapeDtypeStruct(drop(s), s.dtype) for s in recv1],
        scratch_shapes=[pltpu.SemaphoreType.DMA((n, 3)), pltpu.SemaphoreType.DMA((n, 3)), pltpu.SemaphoreType.DMA((n, 2))],
        compiler_params=pltpu.CompilerParams(has_side_effects=True),
    )(*pair16, *mine, *recv1)
    return list(outs[:n]), list(outs[n:2 * n]), list(outs[2 * n:])


def _rows_tile(rows, cols, budget=1 << 20):
    t = rows
    while t % 2 == 0 and t * cols > budget and (t // 2) % 16 == 0:
        t //= 2
    return t


def pair_sum(mine, recv1):
    shp = recv1.shape
    cols = shp[-1]
    m2, r2 = mine.reshape(-1, cols), recv1.reshape(-1, cols)
    rows = m2.shape[0]
    tr = _rows_tile(rows, cols)

    def body(m_ref, r_ref, o_ref):
        o_ref[...] = (m_ref[...] + r_ref[...].astype(F32)).astype(o_ref.dtype)

    blk = pl.BlockSpec((tr, cols), lambda i: (i, 0))
    out = _pallas_call(body, name="pair_sum", grid=(rows // tr,), in_specs=[blk, blk], out_specs=blk,
                       out_shape=jax.ShapeDtypeStruct((rows, cols), recv1.dtype), compiler_params=_cp("parallel"))(m2, r2)
    return out.reshape(shp)


def grad_sum(own32, own16, recv2):
    shp = own32.shape
    L = shp[0]
    cols = shp[-1]
    rows = own32.size // (L * cols)
    tr = _rows_tile(rows, cols)

    def body(o32, o16, r0, r1, r2, g_ref):
        g_ref[0] = (o32[0] + o16[0].astype(F32)) + r0[0, 0].astype(F32) + r1[0, 0].astype(F32) + r2[0, 0].astype(F32)

    blk = pl.BlockSpec((1, tr, cols), lambda l, i: (l, i, 0))
    rblk = [pl.BlockSpec((1, 1, tr, cols), lambda l, i, d=d: (l, d, i, 0)) for d in range(3)]
    r4 = recv2.reshape(L, 3, rows, cols)
    out = _pallas_call(body, name="grad_sum", grid=(L, rows // tr), in_specs=[blk, blk] + rblk, out_specs=blk,
                       out_shape=jax.ShapeDtypeStruct((L, rows, cols), F32), compiler_params=_cp("parallel", "parallel"),
                       )(own32.reshape(L, rows, cols), own16.reshape(L, rows, cols), r4, r4, r4)
    return out.reshape(shp)


def _adamw_math(w, g, m, v):
    m = ADAM_B1 * m + (1.0 - ADAM_B1) * g
    v = ADAM_B2 * v + (1.0 - ADAM_B2) * (g * g)
    m_hat = m / (1.0 - ADAM_B1 ** ADAM_STEP)
    v_hat = v / (1.0 - ADAM_B2 ** ADAM_STEP)
    delta = -ADAM_LR * (m_hat / (jnp.sqrt(v_hat) + ADAM_EPS) + ADAM_WD * w)
    return delta, m, v


def grad_sum_adamw(own32, own16, recv2, w, m, v):
    shp = w.shape
    L = shp[0]
    cols = shp[-1]
    rows = w.size // (L * cols)
    tr = _rows_tile(rows, cols, 1 << 18)

    def body(o32, o16, r0, r1, r2, w_ref, m_ref, v_ref, g_ref, d_ref, nm_ref, nv_ref):
        g = (o32[0] + o16[0].astype(F32)) + r0[0, 0].astype(F32) + r1[0, 0].astype(F32) + r2[0, 0].astype(F32)
        d, nm, nv = _adamw_math(w_ref[0], g, m_ref[0], v_ref[0])
        g_ref[0] = g
        d_ref[0] = d
        nm_ref[0] = nm
        nv_ref[0] = nv

    blk = pl.BlockSpec((1, tr, cols), lambda l, i: (l, i, 0))
    rblk = [pl.BlockSpec((1, 1, tr, cols), lambda l, i, d=d: (l, d, i, 0)) for d in range(3)]
    r4 = recv2.reshape(L, 3, rows, cols)
    v3 = lambda a: a.reshape(L, rows, cols)
    outs = _pallas_call(body, name="grad_sum_adamw", grid=(L, rows // tr), in_specs=[blk, blk] + rblk + [blk] * 3, out_specs=[blk] * 4,
                        out_shape=[jax.ShapeDtypeStruct((L, rows, cols), F32)] * 4, compiler_params=_cp("parallel", "parallel"),
                        )(v3(own32), v3(own16), r4, r4, r4, v3(w), v3(m), v3(v))
    return [o.reshape(shp) for o in outs]


def adamw(w, g, m, v):
    rows, cols = w.shape
    tr = _rows_tile(rows, cols, 1 << 18)

    def body(w_ref, g_ref, m_ref, v_ref, d_ref, nm_ref, nv_ref):
        d, nm, nv = _adamw_math(w_ref[...], g_ref[...], m_ref[...], v_ref[...])
        d_ref[...] = d
        nm_ref[...] = nm
        nv_ref[...] = nv

    blk = pl.BlockSpec((tr, cols), lambda i: (i, 0))
    return _pallas_call(body, name="adamw_small", grid=(rows // tr,), in_specs=[blk] * 4, out_specs=[blk] * 3,
                        out_shape=[jax.ShapeDtypeStruct((rows, cols), F32)] * 3, compiler_params=_cp("parallel"))(w, g, m, v)


SMALL = ("norm_ffn1", "norm_mix", "pool_w", "pool_b", "pool_scale", "conv_w", "conv_b", "lru_w_a", "lru_b_a", "lru_w_x", "lru_b_x",
         "lru_lambda", "norm_ffn2", "final_norm")
BIG = ("ffn1_w_up", "ffn1_w_down", "w_in", "w_pool_up", "w_lru_up", "w_out", "ffn2_w_up", "ffn2_w_down")
NAMES = ("norm_ffn1", "ffn1_w_up", "ffn1_w_down", "norm_mix", "w_in", "pool_w", "pool_b", "pool_scale", "w_pool_up", "conv_w", "conv_b",
         "lru_w_a", "lru_b_a", "lru_w_x", "lru_b_x", "lru_lambda", "w_lru_up", "w_out", "norm_ffn2", "ffn2_w_up", "ffn2_w_down", "final_norm")
PACK_ROWS = 16 * N_DEV


def _pack(parts):
    flat = jnp.concatenate([p.reshape(-1) for p in parts])
    unit = 128 * PACK_ROWS
    padded = -(-flat.size // unit) * unit
    return jnp.pad(flat, (0, padded - flat.size)).reshape(-1, 128)


def _unpack(packed, shapes):
    flat = packed.reshape(-1)
    out, off = [], 0
    for s in shapes:
        n = 1
        for d in s:
            n *= d
        out.append(flat[off:off + n].reshape(s))
        off += n
    return out


def kernel(x, norm_ffn1, ffn1_w_up, ffn1_w_down, norm_mix, w_in, pool_w, pool_b, pool_scale, w_pool_up, conv_w, conv_b, lru_w_a, lru_b_a, lru_w_x, lru_b_x, lru_lambda, w_lru_up, w_out, norm_ffn2, ffn2_w_up, ffn2_w_down, final_norm, loss_target, m_norm_ffn1, m_ffn1_w_up, m_ffn1_w_down, m_norm_mix, m_w_in, m_pool_w, m_pool_b, m_pool_scale, m_w_pool_up, m_conv_w, m_conv_b, m_lru_w_a, m_lru_b_a, m_lru_w_x, m_lru_b_x, m_lru_lambda, m_w_lru_up, m_w_out, m_norm_ffn2, m_ffn2_w_up, m_ffn2_w_down, m_final_norm, v_norm_ffn1, v_ffn1_w_up, v_ffn1_w_down, v_norm_mix, v_w_in, v_pool_w, v_pool_b, v_pool_scale, v_w_pool_up, v_conv_w, v_conv_b, v_lru_w_a, v_lru_b_a, v_lru_w_x, v_lru_b_x, v_lru_lambda, v_w_lru_up, v_w_out, v_norm_ffn2, v_ffn2_w_up, v_ffn2_w_down, v_final_norm):
    W = dict(norm_ffn1=norm_ffn1, ffn1_w_up=ffn1_w_up, ffn1_w_down=ffn1_w_down, norm_mix=norm_mix, w_in=w_in, pool_w=pool_w, pool_b=pool_b,
             pool_scale=pool_scale, w_pool_up=w_pool_up, conv_w=conv_w, conv_b=conv_b, lru_w_a=lru_w_a, lru_b_a=lru_b_a, lru_w_x=lru_w_x,
             lru_b_x=lru_b_x, lru_lambda=lru_lambda, w_lru_up=w_lru_up, w_out=w_out, norm_ffn2=norm_ffn2, ffn2_w_up=ffn2_w_up,
             ffn2_w_down=ffn2_w_down, final_norm=final_norm)
    M = dict(norm_ffn1=m_norm_ffn1, ffn1_w_up=m_ffn1_w_up, ffn1_w_down=m_ffn1_w_down, norm_mix=m_norm_mix, w_in=m_w_in, pool_w=m_pool_w,
             pool_b=m_pool_b, pool_scale=m_pool_scale, w_pool_up=m_w_pool_up, conv_w=m_conv_w, conv_b=m_conv_b, lru_w_a=m_lru_w_a,
             lru_b_a=m_lru_b_a, lru_w_x=m_lru_w_x, lru_b_x=m_lru_b_x, lru_lambda=m_lru_lambda, w_lru_up=m_w_lru_up, w_out=m_w_out,
             norm_ffn2=m_norm_ffn2, ffn2_w_up=m_ffn2_w_up, ffn2_w_down=m_ffn2_w_down, final_norm=m_final_norm)
    V = dict(norm_ffn1=v_norm_ffn1, ffn1_w_up=v_ffn1_w_up, ffn1_w_down=v_ffn1_w_down, norm_mix=v_norm_mix, w_in=v_w_in, pool_w=v_pool_w,
             pool_b=v_pool_b, pool_scale=v_pool_scale, w_pool_up=v_w_pool_up, conv_w=v_conv_w, conv_b=v_conv_b, lru_w_a=v_lru_w_a,
             lru_b_a=v_lru_b_a, lru_w_x=v_lru_w_x, lru_b_x=v_lru_b_x, lru_lambda=v_lru_lambda, w_lru_up=v_w_lru_up, w_out=v_w_out,
             norm_ffn2=v_norm_ffn2, ffn2_w_up=v_ffn2_w_up, ffn2_w_down=v_ffn2_w_down, final_norm=v_final_norm)

    T, D = x.shape[1], x.shape[2]
    L = norm_ffn1.shape[0]
    P = pool_scale.shape[1]
    R = lru_lambda.shape[1]
    H, hd = lru_w_a.shape[1], lru_w_a.shape[2]
    CW = conv_w.shape[1]
    cs = ffn1_w_up.shape[2]
    ci = w_in.shape[2]
    xin = x.reshape(T, D)
    tgt = loss_target.reshape(T, D)
    dev = 4 * lax.axis_index("x") + 2 * lax.axis_index("y") + lax.axis_index("c")

    cw_flat = conv_w.reshape(L, -1)
    cw_pad = (-cw_flat.shape[1]) % 1024
    cw_tiles = jnp.pad(cw_flat, ((0, 0), (0, cw_pad))).reshape(L, -1, 128)
    gathered = all_gather("all_gather_weights", [W[n].astype(BF16) for n in BIG] + [cw_tiles])
    Wg = dict(zip(BIG + ("conv_w",), gathered))
    Wg["conv_w"] = Wg["conv_w"].reshape(L, N_DEV, -1)[:, :, :cw_flat.shape[1]].reshape((L, N_DEV) + conv_w.shape[1:])
    wup1, wup2, win = Wg["ffn1_w_up"], Wg["ffn2_w_up"], Wg["w_in"]
    wd1 = Wg["ffn1_w_down"].reshape(L, -1, D)
    wd2 = Wg["ffn2_w_down"].reshape(L, -1, D)
    wlu = Wg["w_lru_up"].reshape(L, R, D)
    wout = Wg["w_out"].reshape(L, D, D)
    wpu = Wg["w_pool_up"].transpose(0, 2, 1, 3).reshape(L, P, D)
    cw_full = Wg["conv_w"].transpose(0, 2, 1, 3).reshape(L, CW, R)

    vec = lambda a: a.reshape(L, 1, -1)
    g1, gm, g2 = vec(norm_ffn1), vec(norm_mix), vec(norm_ffn2)
    pb, ps = vec(pool_b), vec(pool_scale)
    cb, ba, bx, lam = vec(conv_b), vec(lru_b_a), vec(lru_b_x), vec(lru_lambda)

    saved = []
    xc = xin
    for l in range(L):
        sv = {"x1": xc}
        sv["h1"], sv["a1"], sv["b1"], sv["s1"] = ffn_up(xc, g1, wup1, l)
        xc = ffn_down(sv["s1"], wd1, xc, l)
        sv["x2"] = xc
        sv["h2"], sv["proj"] = mix_in(xc, gm, win, l)
        sv["pm"] = pool_fwd(sv["proj"], pool_w, pb, ps, l)
        sv["hl"], sv["hs"] = lru_fwd(sv["proj"], cw_full, cb, lru_w_a, ba, lru_w_x, bx, lam, P, l)
        xc, sv["yp"], sv["yl"], sv["z"] = mix_out(sv["pm"], sv["hl"], sv["proj"], xc, wpu, wlu, wout, P, l)
        sv["x3"] = xc
        sv["h3"], sv["a3"], sv["b3"], sv["s3"] = ffn_up(xc, g2, wup2, l)
        xc = ffn_down(sv["s3"], wd2, xc, l)
        saved.append(sv)

    loss_part, dx, d_final = loss_head(xc, final_norm.reshape(1, D), tgt)
    loss = lax.psum(loss_part[0, 0], ("x", "y", "c"))

    G = {n: None for n in BIG}
    small = {n: [None] * L for n in SMALL if n != "final_norm"}

    def ffn_bwd(dy, sv, tag, wup, wd, gn, up_name, dn_name, norm_name, l):
        dout, da, db = ffn_down_bwd(dy, wd, sv["a" + tag], sv["b" + tag], l)
        G[dn_name] = dw_tn("dw_down", [sv["s" + tag]], lambda tk: pl.BlockSpec((1, tk, cs), lambda g, k: (g, k, 0)),
                           [dout], lambda tk: pl.BlockSpec((tk, D), lambda g, k: (k, 0)), 4, cs, D, T, l, L, G[dn_name])
        du = jnp.concatenate([da, db], axis=0)
        G[up_name] = dw_tn("dw_up", [sv["h" + tag]], lambda tk: pl.BlockSpec((tk, D), lambda g, k: (k, 0)),
                           [du], lambda tk: pl.BlockSpec((1, tk, cs), lambda g, k: (g, k, 0)), N_DEV, D, cs, T, l, L, G[up_name])
        dxn, dg = dx_norm_bwd("ffn_dx", du, lambda tm: pl.BlockSpec((1, tm, cs), lambda i, j: (j, i, 0)), wup, N_DEV,
                              sv["x" + tag], gn, dy, l)
        small[norm_name][l] = dg.reshape(D)
        return dxn

    for l in reversed(range(L)):
        sv = saved[l]
        dx = ffn_bwd(dx, sv, "3", wup2, wd2, g2, "ffn2_w_up", "ffn2_w_down", "norm_ffn2", l)
        dyb, dyp, dyl, dgp, dgl, dpm, dhl = mix_out_bwd(dx, sv["proj"], sv["yp"], sv["yl"], wpu, wlu, wout, P, R, l)
        row = lambda w: (lambda tk: pl.BlockSpec((tk, w), lambda g, k: (k, 0)))
        G["w_out"] = dw_tn("dw_out", [sv["z"]], row(D), [dyb], row(D), 1, D, D, T, l, L, G["w_out"])
        G["w_lru_up"] = dw_tn("dw_lru_up", [sv["hl"]], row(R), [dyl], row(D), 1, R, D, T, l, L, G["w_lru_up"])
        G["w_pool_up"] = dw_tn("dw_pool_up", [sv["pm"]], row(P), [dyp], row(D), 1, P, D, T, l, L, G["w_pool_up"])
        du_lru, du_gelu, dcw, dcb, dwa, dba, dwx, dbx, dlam = lru_bwd(
            sv["proj"], sv["hs"], dhl, cw_full, cb, lru_w_a, ba, lru_w_x, bx, lam, P, l)
        du_pool, dpw, dpb, dpsc = pool_bwd(sv["proj"], dpm, pool_w, pb, ps, l)
        dproj = jnp.concatenate([du_pool, du_lru, du_gelu, dgp, dgl], axis=1)
        G["w_in"] = dw_tn("dw_in", [sv["h2"]], row(D), [dproj], lambda tk: pl.BlockSpec((tk, ci), lambda g, k: (k, g)),
                          N_DEV, D, ci, T, l, L, G["w_in"])
        dx, dgm = dx_norm_bwd("mix_dx", dproj, lambda tm: pl.BlockSpec((tm, ci), lambda i, j: (i, j)), win, N_DEV,
                              sv["x2"], gm, dx, l)
        small["norm_mix"][l] = dgm.reshape(D)
        small["pool_w"][l], small["pool_b"][l], small["pool_scale"][l] = dpw[0], dpb.reshape(pool_b.shape[1:]), dpsc.reshape(P)
        small["conv_w"][l], small["conv_b"][l] = dcw[0], dcb.reshape(R)
        small["lru_w_a"][l], small["lru_b_a"][l] = dwa[0], dba.reshape(H, hd)
        small["lru_w_x"][l], small["lru_b_x"][l] = dwx[0], dbx.reshape(H, hd)
        small["lru_lambda"][l] = dlam.reshape(R)
        dx = ffn_bwd(dx, sv, "1", wup1, wd1, g1, "ffn1_w_up", "ffn1_w_down", "norm_ffn1", l)

    grad_x = dx.reshape(x.shape)

    def to_slots(name, a):
        if name == "w_pool_up":
            return a.reshape(L, P, N_DEV, D // N_DEV).transpose(0, 2, 1, 3)
        return a.reshape((L, N_DEV) + W[name].shape[1:])

    small_parts = [jnp.stack(small[n]) for n in SMALL if n != "final_norm"] + [d_final.reshape(D)]
    small_shapes = [p.shape for p in small_parts]
    gpack = _pack(small_parts)
    gpack = gpack.reshape(1, N_DEV, -1, 128)
    g32 = [to_slots(n, G[n][0]) for n in BIG] + [gpack]
    g16 = [to_slots(n, G[n][1]) for n in BIG] + [gpack]
    recv1, mine = pair_exchange("rs_pair_exchange", g32, g16)
    pair16 = [pair_sum(m_, r_) for m_, r_ in zip(mine, recv1)]
    recv2, own32, own16 = chip_exchange("rs_chip_exchange", pair16, mine, recv1)

    out_g, out_d, out_m, out_v = {}, {}, {}, {}
    for i, n in enumerate(BIG):
        out_g[n], out_d[n], out_m[n], out_v[n] = grad_sum_adamw(own32[i], own16[i], recv2[i], W[n], M[n], V[n])

    gs = grad_sum(own32[-1], own16[-1], recv2[-1])
    gs_all = all_gather("all_gather_small_grads", [gs])[0]
    gs_all = gs_all.reshape(-1, 128)
    small_g = dict(zip(SMALL, _unpack(gs_all, small_shapes)))
    full_shapes = [W[n].shape if n != "conv_w" else small_shapes[SMALL.index("conv_w")] for n in SMALL]
    rep = [n for n in SMALL if n != "conv_w"]
    rep_shapes = [W[n].shape for n in rep]
    wp, mp, vp = (_pack([S[n] for n in rep]) for S in (W, M, V))
    gp = _pack([small_g[n] for n in rep])
    dp, nmp, nvp = adamw(wp, gp, mp, vp)
    for S, packed in ((out_d, dp), (out_m, nmp), (out_v, nvp)):
        S.update(zip(rep, _unpack(packed, rep_shapes)))
    for n in rep:
        out_g[n] = small_g[n]
    cwc = conv_w.shape[2]
    gcw = lax.dynamic_slice_in_dim(small_g["conv_w"], dev * cwc, cwc, axis=2)
    cw2 = lambda a: a.reshape(-1, cwc)
    pad_rows = (-cw2(conv_w).shape[0]) % 8
    padr = lambda a: jnp.pad(cw2(a), ((0, pad_rows), (0, 0)))
    dcw_, mcw_, vcw_ = adamw(padr(conv_w), padr(gcw), padr(M["conv_w"]), padr(V["conv_w"]))
    nrow = cw2(conv_w).shape[0]
    out_g["conv_w"] = gcw
    out_d["conv_w"], out_m["conv_w"], out_v["conv_w"] = (a[:nrow].reshape(conv_w.shape) for a in (dcw_, mcw_, vcw_))
    del full_shapes

    return (loss, grad_x, *[out_g[n] for n in NAMES], *[out_d[n] for n in NAMES], *[out_m[n] for n in NAMES], *[out_v[n] for n in NAMES])
```

```python
import functools

import jax
import jax.numpy as jnp
from jax import lax
from jax.experimental import pallas as pl
from jax.experimental.pallas import tpu as pltpu

F32, BF16 = jnp.float32, jnp.bfloat16
EPS = 1e-6
LRU_C = 8.0
POOL_WINDOWS = (2, 4, 8, 16)
ADAM_LR, ADAM_B1, ADAM_B2, ADAM_EPS, ADAM_WD, ADAM_STEP = 0.001, 0.9, 0.999, 1e-08, 0.01, 10
N_DEV = 8
N_CHIP = 4
MESH = pl.DeviceIdType.MESH
V7X_VMEM_LIMIT = 56 * 1024 * 1024
ROW_TILE = 512
ANY = pl.BlockSpec(memory_space=pl.ANY)

_pallas_call = pl.pallas_call


def _cp(*sem):
    return pltpu.CompilerParams(dimension_semantics=sem if sem else None, vmem_limit_bytes=V7X_VMEM_LIMIT)


def _tile(n, t=ROW_TILE):
    t = min(n, t)
    assert n % t == 0, (n, t)
    return t


def _dot(a, b):
    return jnp.dot(a, b, preferred_element_type=F32)


def _dot_nt(a, b):
    return lax.dot_general(a, b, (((1,), (1,)), ((), ())), preferred_element_type=F32)


def _dot_tn(a, b):
    return lax.dot_general(a, b, (((0,), (0,)), ((), ())), preferred_element_type=F32)


def _rms(xv):
    r = lax.rsqrt(jnp.mean(xv * xv, axis=-1, keepdims=True) + EPS)
    return xv * r, r


def _rms_bwd(dh, xv, gv, dy):
    n, r = _rms(xv)
    dn = dh * gv
    dx = dy + r * (dn - n * jnp.mean(dn * n, axis=-1, keepdims=True))
    return dx, jnp.sum(dh * n, axis=0, keepdims=True)


def _shift_down(x, k, fill=0.0):
    if k == 0:
        return x
    rows = lax.broadcasted_iota(jnp.int32, x.shape, 0)
    return jnp.where(rows >= k, pltpu.roll(x, k, 0), fill)


def _shift_up(x, k, fill=0.0):
    if k == 0:
        return x
    n = x.shape[0]
    rows = lax.broadcasted_iota(jnp.int32, x.shape, 0)
    return jnp.where(rows < n - k, pltpu.roll(x, n - k, 0), fill)


def _sigmoid(x):
    return 1.0 / (1.0 + jnp.exp(-x))


_GELU_K = 0.7978845608028654
_GELU_C = 0.044715


def _gelu(x):
    th = jnp.tanh(_GELU_K * (x + _GELU_C * x * x * x))
    return 0.5 * x * (1.0 + th), th


def _gelu_grad(x, th):
    return 0.5 * (1.0 + th) + 0.5 * x * (1.0 - th * th) * _GELU_K * (1.0 + 3.0 * _GELU_C * x * x)


def ffn_up(x, g, wup, l):
    T, D = x.shape
    cs = wup.shape[-1]
    tm = _tile(T)

    def body(x_ref, g_ref, wa_ref, wb_ref, h_ref, u_ref, s_ref, hs_ref):
        @pl.when(pl.program_id(1) == 0)
        def _():
            n, _r = _rms(x_ref[...])
            hv = (n * g_ref[0]).astype(BF16)
            hs_ref[...] = hv
            h_ref[...] = hv

        hv = hs_ref[...]
        a = _dot(hv, wa_ref[0, 0])
        b = _dot(hv, wb_ref[0, 0])
        u_ref[0, 0] = a.astype(BF16)
        u_ref[1, 0] = b.astype(BF16)
        s_ref[0] = (a * _sigmoid(a) * b).astype(BF16)

    return _pallas_call(
        body, name="ffn_up", grid=(T // tm, 4),
        in_specs=[pl.BlockSpec((tm, D), lambda i, j: (i, 0)), pl.BlockSpec((1, 1, D), lambda i, j: (l, 0, 0)),
                  pl.BlockSpec((1, 1, D, cs), lambda i, j: (l, j, 0, 0)), pl.BlockSpec((1, 1, D, cs), lambda i, j: (l, j + 4, 0, 0))],
        out_specs=[pl.BlockSpec((tm, D), lambda i, j: (i, 0)), pl.BlockSpec((2, 1, tm, cs), lambda i, j: (0, j, i, 0)),
                   pl.BlockSpec((1, tm, cs), lambda i, j: (j, i, 0))],
        out_shape=[jax.ShapeDtypeStruct((T, D), BF16), jax.ShapeDtypeStruct((2, 4, T, cs), BF16), jax.ShapeDtypeStruct((4, T, cs), BF16)],
        scratch_shapes=[pltpu.VMEM((tm, D), BF16)],
        compiler_params=_cp("parallel", "arbitrary"),
    )(x, g, wup, wup)


def ffn_down(s, wd, x, l):
    _, T, cs = s.shape
    D = x.shape[1]
    tm = _tile(T)

    def body(s_ref, w_ref, x_ref, o_ref, acc_ref):
        j = pl.program_id(1)

        @pl.when(j == 0)
        def _():
            acc_ref[...] = jnp.zeros_like(acc_ref)

        acc_ref[...] += _dot(s_ref[0], w_ref[0])

        @pl.when(j == 3)
        def _():
            o_ref[...] = x_ref[...] + 0.5 * acc_ref[...]

    return _pallas_call(
        body, name="ffn_down", grid=(T // tm, 4),
        in_specs=[pl.BlockSpec((1, tm, cs), lambda i, j: (j, i, 0)), pl.BlockSpec((1, cs, D), lambda i, j: (l, j, 0)),
                  pl.BlockSpec((tm, D), lambda i, j: (i, 0))],
        out_specs=pl.BlockSpec((tm, D), lambda i, j: (i, 0)),
        out_shape=jax.ShapeDtypeStruct((T, D), F32),
        scratch_shapes=[pltpu.VMEM((tm, D), F32)],
        compiler_params=_cp("parallel", "arbitrary"),
    )(s, wd, x)


def mix_in(x, g, win, l):
    T, D = x.shape
    ci = win.shape[-1]
    tm = _tile(T)

    def body(x_ref, g_ref, w_ref, h_ref, p_ref, hs_ref):
        @pl.when(pl.program_id(1) == 0)
        def _():
            n, _r = _rms(x_ref[...])
            hv = (n * g_ref[0]).astype(BF16)
            hs_ref[...] = hv
            h_ref[...] = hv

        p_ref[...] = _dot(hs_ref[...], w_ref[0, 0]).astype(BF16)

    return _pallas_call(
        body, name="mix_in", grid=(T // tm, N_DEV),
        in_specs=[pl.BlockSpec((tm, D), lambda i, j: (i, 0)), pl.BlockSpec((1, 1, D), lambda i, j: (l, 0, 0)),
                  pl.BlockSpec((1, 1, D, ci), lambda i, j: (l, j, 0, 0))],
        out_specs=[pl.BlockSpec((tm, D), lambda i, j: (i, 0)), pl.BlockSpec((tm, ci), lambda i, j: (i, j))],
        out_shape=[jax.ShapeDtypeStruct((T, D), BF16), jax.ShapeDtypeStruct((T, N_DEV * ci), BF16)],
        scratch_shapes=[pltpu.VMEM((tm, D), BF16)],
        compiler_params=_cp("parallel", "arbitrary"),
    )(x, g, win)


def _inv_count(T, w):
    t = lax.broadcasted_iota(jnp.int32, (T, 1), 0)
    return 1.0 / jnp.minimum(t + 1, w).astype(F32)


def _pooled(ug, w, inv):
    s = ug
    k = 1
    while k < w:
        s = s + _shift_down(s, k)
        k *= 2
    return s * inv - ug


def pool_fwd(proj, pw, pb, ps, l):
    T = proj.shape[0]
    _, G, gd, _ = pw.shape
    P = G * gd

    def body(u_ref, w_ref, b_ref, s_ref, o_ref):
        for gi in range(G):
            cols = slice(gi * gd, (gi + 1) * gd)
            ug = u_ref[:, cols].astype(F32)
            pooled = _pooled(ug, POOL_WINDOWS[gi], _inv_count(T, POOL_WINDOWS[gi]))
            mixed = _dot(pooled.astype(BF16), w_ref[0, gi].astype(BF16)) + b_ref[0, :, cols]
            o_ref[:, cols] = (mixed * s_ref[0, :, cols]).astype(BF16)

    return _pallas_call(
        body, name="pool_fwd", grid=(1,),
        in_specs=[pl.BlockSpec((T, P), lambda i: (0, 0)), pl.BlockSpec((1, G, gd, gd), lambda i: (l, 0, 0, 0)),
                  pl.BlockSpec((1, 1, P), lambda i: (l, 0, 0)), pl.BlockSpec((1, 1, P), lambda i: (l, 0, 0))],
        out_specs=pl.BlockSpec((T, P), lambda i: (0, 0)),
        out_shape=jax.ShapeDtypeStruct((T, P), BF16),
        compiler_params=_cp("arbitrary"),
    )(proj, pw, pb, ps)


def _conv(u, cw_ref, cb):
    CW = cw_ref.shape[1]
    v = cb
    for k in range(CW):
        v = v + cw_ref[0, k:k + 1, :] * _shift_down(u, CW - 1 - k)
    return v


def _softplus(z):
    return jnp.maximum(z, 0.0) + jnp.log1p(jnp.exp(-jnp.abs(z)))


def _lru_gates(v, wa_ref, ba, wx_ref, bx, lam):
    vb = v.astype(BF16)
    r = _sigmoid(_dot(vb, wa_ref[0, 0].astype(BF16)) + ba)
    i = _sigmoid(_dot(vb, wx_ref[0, 0].astype(BF16)) + bx)
    sp = _softplus(-lam)
    log_a = -LRU_C * r * sp
    a = jnp.exp(log_a)
    mult = jnp.sqrt(-jnp.tanh(log_a) * (a * a + 1.0))
    return r, i, sp, a, mult


def _scan_fwd(a_ref, b_ref, o_ref):
    T, W = a_ref.shape
    rows = lax.broadcasted_iota(jnp.int32, (8, W), 0)

    def step(t, carry):
        r0 = pl.multiple_of(t * 8, 8)
        A = a_ref[pl.ds(r0, 8), :]
        B = b_ref[pl.ds(r0, 8), :]
        for s in (1, 2, 4):
            keep = rows >= s
            As = jnp.where(keep, pltpu.roll(A, s, 0), 1.0)
            Bs = jnp.where(keep, pltpu.roll(B, s, 0), 0.0)
            B = A * Bs + B
            A = A * As
        h = B + A * carry
        o_ref[pl.ds(r0, 8), :] = h
        return jnp.broadcast_to(h[7:8, :], (8, W))

    lax.fori_loop(0, T // 8, step, jnp.zeros((8, W), F32), unroll=8)


def _scan_bwd(a_ref, b_ref, o_ref):
    T, W = a_ref.shape
    rows = lax.broadcasted_iota(jnp.int32, (8, W), 0)
    nt = T // 8

    def step(t, carry):
        r0 = pl.multiple_of((nt - 1 - t) * 8, 8)
        A = a_ref[pl.ds(r0, 8), :]
        B = b_ref[pl.ds(r0, 8), :]
        for s in (1, 2, 4):
            keep = rows < 8 - s
            As = jnp.where(keep, pltpu.roll(A, 8 - s, 0), 1.0)
            Bs = jnp.where(keep, pltpu.roll(B, 8 - s, 0), 0.0)
            B = A * Bs + B
            A = A * As
        y = B + A * carry
        o_ref[pl.ds(r0, 8), :] = y
        return jnp.broadcast_to(y[0:1, :], (8, W))

    lax.fori_loop(0, nt, step, jnp.zeros((8, W), F32), unroll=8)


def _lru_specs(T, hd, P, R, CW, l):
    ob, gb = P // hd, (P + R) // hd
    vec = pl.BlockSpec((1, 1, hd), lambda h: (l, 0, h))
    mat = pl.BlockSpec((1, 1, hd, hd), lambda h: (l, h, 0, 0))
    return [pl.BlockSpec((T, hd), lambda h: (0, ob + h)), pl.BlockSpec((T, hd), lambda h: (0, gb + h)),
            pl.BlockSpec((1, CW, hd), lambda h: (l, 0, h)), vec, mat, vec, mat, vec, vec]


def lru_fwd(proj, cw, cb, wa, ba, wx, bx, lam, P, l):
    T = proj.shape[0]
    _, H, hd, _ = wa.shape
    R = H * hd
    CW = cw.shape[1]
    assert P % hd == 0 and T % 8 == 0

    def body(u_ref, ug_ref, cw_ref, cb_ref, wa_ref, ba_ref, wx_ref, bx_ref, lam_ref, hl_ref, hs_ref, a_s, b_s):
        v = _conv(u_ref[...].astype(F32), cw_ref, cb_ref[0])
        _r, i, _sp, a, mult = _lru_gates(v, wa_ref, ba_ref[0], wx_ref, bx_ref[0], lam_ref[0])
        a_s[...] = a
        b_s[...] = mult * (i * v)
        _scan_fwd(a_s, b_s, hs_ref)
        ge, _th = _gelu(ug_ref[...].astype(F32))
        hl_ref[...] = (hs_ref[...] * ge).astype(BF16)

    out = pl.BlockSpec((T, hd), lambda h: (0, h))
    return _pallas_call(
        body, name="lru_fwd", grid=(H,),
        in_specs=_lru_specs(T, hd, P, R, CW, l),
        out_specs=[out, out],
        out_shape=[jax.ShapeDtypeStruct((T, R), BF16), jax.ShapeDtypeStruct((T, R), F32)],
        scratch_shapes=[pltpu.VMEM((T, hd), F32)] * 2,
        compiler_params=_cp("parallel"),
    )(proj, proj, cw, cb, wa, ba, wx, bx, lam)


def mix_out(pm, hl, proj, x, wpu, wlu, wout, P, l):
    T, D = x.shape
    R = hl.shape[1]
    tm = _tile(T)
    assert (P + 2 * R) % D == 0
    gb = (P + 2 * R) // D

    def body(pm_ref, hl_ref, gp_ref, gl_ref, x_ref, wpu_ref, wlu_ref, wo_ref, o_ref, yp_ref, yl_ref, z_ref):
        yp = _dot(pm_ref[...], wpu_ref[0])
        yl = _dot(hl_ref[...], wlu_ref[0])
        z = (_sigmoid(gp_ref[...].astype(F32)) * yp + _sigmoid(gl_ref[...].astype(F32)) * yl).astype(BF16)
        yp_ref[...] = yp.astype(BF16)
        yl_ref[...] = yl.astype(BF16)
        z_ref[...] = z
        o_ref[...] = x_ref[...] + _dot(z, wo_ref[0])

    row = lambda w: pl.BlockSpec((tm, w), lambda i: (i, 0))
    return _pallas_call(
        body, name="mix_out", grid=(T // tm,),
        in_specs=[row(P), row(R), pl.BlockSpec((tm, D), lambda i: (i, gb)), pl.BlockSpec((tm, D), lambda i: (i, gb + 1)), row(D),
                  pl.BlockSpec((1, P, D), lambda i: (l, 0, 0)), pl.BlockSpec((1, R, D), lambda i: (l, 0, 0)),
                  pl.BlockSpec((1, D, D), lambda i: (l, 0, 0))],
        out_specs=[row(D)] * 4,
        out_shape=[jax.ShapeDtypeStruct((T, D), F32)] + [jax.ShapeDtypeStruct((T, D), BF16)] * 3,
        compiler_params=_cp("parallel"),
    )(pm, hl, proj, proj, x, wpu, wlu, wout)


def loss_head(x, gf, tgt):
    T, D = x.shape
    tm = _tile(T)

    def body(x_ref, g_ref, t_ref, loss_ref, dx_ref, dg_ref):
        @pl.when(pl.program_id(0) == 0)
        def _():
            loss_ref[...] = jnp.zeros_like(loss_ref)
            dg_ref[...] = jnp.zeros_like(dg_ref)

        xv = x_ref[...]
        gv = g_ref[...]
        n, _r = _rms(xv)
        e = n * gv - t_ref[...]
        loss_ref[...] += 0.5 * jnp.sum(jnp.sum(e * e, axis=-1, keepdims=True), axis=0, keepdims=True) / D
        dx, dg = _rms_bwd(e * (1.0 / D), xv, gv, 0.0)
        dx_ref[...] = dx
        dg_ref[...] += dg

    return _pallas_call(
        body, name="loss_head", grid=(T // tm,),
        in_specs=[pl.BlockSpec((tm, D), lambda i: (i, 0)), pl.BlockSpec((1, D), lambda i: (0, 0)), pl.BlockSpec((tm, D), lambda i: (i, 0))],
        out_specs=[pl.BlockSpec((1, 1), lambda i: (0, 0)), pl.BlockSpec((tm, D), lambda i: (i, 0)), pl.BlockSpec((1, D), lambda i: (0, 0))],
        out_shape=[jax.ShapeDtypeStruct((1, 1), F32), jax.ShapeDtypeStruct((T, D), F32), jax.ShapeDtypeStruct((1, D), F32)],
        compiler_params=_cp("arbitrary"),
    )(x, gf, tgt)


def ffn_down_bwd(dy, wd, u, l):
    T, D = dy.shape
    cs = u.shape[-1]
    tm = _tile(T)

    def body(dy_ref, w_ref, u_ref, do_ref, du_ref, dyb_ref):
        @pl.when(pl.program_id(1) == 0)
        def _():
            d = (0.5 * dy_ref[...]).astype(BF16)
            dyb_ref[...] = d
            do_ref[...] = d

        ds = _dot_nt(dyb_ref[...], w_ref[0])
        a = u_ref[0, 0].astype(F32)
        b = u_ref[1, 0].astype(F32)
        sg = _sigmoid(a)
        du_ref[0, 0] = (ds * b * (sg * (1.0 + a * (1.0 - sg)))).astype(BF16)
        du_ref[1, 0] = (ds * (a * sg)).astype(BF16)

    blk = pl.BlockSpec((2, 1, tm, cs), lambda i, j: (0, j, i, 0))
    return _pallas_call(
        body, name="ffn_down_bwd", grid=(T // tm, 4),
        in_specs=[pl.BlockSpec((tm, D), lambda i, j: (i, 0)), pl.BlockSpec((1, cs, D), lambda i, j: (l, j, 0)), blk],
        out_specs=[pl.BlockSpec((tm, D), lambda i, j: (i, 0)), blk],
        out_shape=[jax.ShapeDtypeStruct((T, D), BF16), jax.ShapeDtypeStruct((2, 4, T, cs), BF16)],
        scratch_shapes=[pltpu.VMEM((tm, D), BF16)],
        compiler_params=_cp("parallel", "arbitrary"),
    )(dy, wd, u)


def dw_tn(name, a_ops, a_spec, b_ops, b_spec, G, M, N, T, l, L, prev):
    tk = _tile(T)
    nk = T // tk

    def body(*refs):
        a_refs, b_refs = refs[:len(a_ops)], refs[len(a_ops):len(a_ops) + len(b_ops)]
        o32_ref, o16_ref, acc_ref = refs[-3:]
        k = pl.program_id(1)

        @pl.when(k == 0)
        def _():
            acc_ref[...] = jnp.zeros_like(acc_ref)

        av = a_refs[0][0] if len(a_refs[0].shape) == 3 else a_refs[0][...]
        bv = b_refs[0][0] if len(b_refs[0].shape) == 3 else b_refs[0][...]
        acc_ref[...] += _dot_tn(av, bv)

        @pl.when(k == nk - 1)
        def _():
            o32_ref[0, 0] = acc_ref[...]
            o16_ref[0, 0] = acc_ref[...].astype(BF16)

    n_in = len(a_ops) + len(b_ops)
    in_specs = [a_spec(tk), b_spec(tk)]
    args = list(a_ops) + list(b_ops)
    aliases = {}
    if prev is not None:
        in_specs += [ANY, ANY]
        args += list(prev)
        aliases = {n_in: 0, n_in + 1: 1}

    def body_wrap(*refs):
        if prev is not None:
            refs = refs[:n_in] + refs[n_in + 2:]
        body(*refs)

    out = pl.BlockSpec((1, 1, M, N), lambda g, k: (l, g, 0, 0))
    return _pallas_call(
        body_wrap, name=name, grid=(G, nk),
        in_specs=in_specs, out_specs=[out, out],
        out_shape=[jax.ShapeDtypeStruct((L, G, M, N), F32), jax.ShapeDtypeStruct((L, G, M, N), BF16)],
        scratch_shapes=[pltpu.VMEM((M, N), F32)],
        input_output_aliases=aliases,
        compiler_params=_cp("parallel", "arbitrary"),
    )(*args)


def dx_norm_bwd(name, dact, d_spec, w, G, x, g, dy, l):
    T, D = x.shape
    c = w.shape[-1]
    tm = _tile(T)

    def body(d_ref, w_ref, x_ref, g_ref, dy_ref, dx_ref, dg_ref, acc_ref):
        i, j = pl.program_id(0), pl.program_id(1)

        @pl.when(jnp.logical_and(i == 0, j == 0))
        def _():
            dg_ref[...] = jnp.zeros_like(dg_ref)

        @pl.when(j == 0)
        def _():
            acc_ref[...] = jnp.zeros_like(acc_ref)

        dv = d_ref[0] if len(d_ref.shape) == 3 else d_ref[...]
        acc_ref[...] += _dot_nt(dv, w_ref[0, 0])

        @pl.when(j == G - 1)
        def _():
            dx, dg = _rms_bwd(acc_ref[...], x_ref[...], g_ref[0], dy_ref[...])
            dx_ref[...] = dx
            dg_ref[...] += dg

    row = pl.BlockSpec((tm, D), lambda i, j: (i, 0))
    return _pallas_call(
        body, name=name, grid=(T // tm, G),
        in_specs=[d_spec(tm), pl.BlockSpec((1, 1, D, c), lambda i, j: (l, j, 0, 0)), row, pl.BlockSpec((1, 1, D), lambda i, j: (l, 0, 0)), row],
        out_specs=[row, pl.BlockSpec((1, D), lambda i, j: (0, 0))],
        out_shape=[jax.ShapeDtypeStruct((T, D), F32), jax.ShapeDtypeStruct((1, D), F32)],
        scratch_shapes=[pltpu.VMEM((tm, D), F32)],
        compiler_params=_cp("arbitrary", "arbitrary"),
    )(dact, w, x, g, dy)


def mix_out_bwd(dy, proj, yp, yl, wpu, wlu, wout, P, R, l):
    T, D = dy.shape
    tm = _tile(T)
    gb = (P + 2 * R) // D

    def body(dy_ref, gp_ref, gl_ref, yp_ref, yl_ref, wpu_ref, wlu_ref, wo_ref,
             dyb_ref, dyp_ref, dyl_ref, dgp_ref, dgl_ref, dpm_ref, dhl_ref):
        dyb = dy_ref[...].astype(BF16)
        dyb_ref[...] = dyb
        dz = _dot_nt(dyb, wo_ref[0])
        sp = _sigmoid(gp_ref[...].astype(F32))
        sl = _sigmoid(gl_ref[...].astype(F32))
        dgp_ref[...] = (dz * yp_ref[...].astype(F32) * sp * (1.0 - sp)).astype(BF16)
        dgl_ref[...] = (dz * yl_ref[...].astype(F32) * sl * (1.0 - sl)).astype(BF16)
        dyp = (dz * sp).astype(BF16)
        dyl = (dz * sl).astype(BF16)
        dyp_ref[...] = dyp
        dyl_ref[...] = dyl
        dpm_ref[...] = _dot_nt(dyp, wpu_ref[0]).astype(BF16)
        dhl_ref[...] = _dot_nt(dyl, wlu_ref[0]).astype(BF16)

    row = lambda w: pl.BlockSpec((tm, w), lambda i: (i, 0))
    return _pallas_call(
        body, name="mix_out_bwd", grid=(T // tm,),
        in_specs=[row(D), pl.BlockSpec((tm, D), lambda i: (i, gb)), pl.BlockSpec((tm, D), lambda i: (i, gb + 1)), row(D), row(D),
                  pl.BlockSpec((1, P, D), lambda i: (l, 0, 0)), pl.BlockSpec((1, R, D), lambda i: (l, 0, 0)),
                  pl.BlockSpec((1, D, D), lambda i: (l, 0, 0))],
        out_specs=[row(D)] * 5 + [row(P), row(R)],
        out_shape=[jax.ShapeDtypeStruct((T, D), BF16)] * 5 + [jax.ShapeDtypeStruct((T, P), BF16), jax.ShapeDtypeStruct((T, R), BF16)],
        compiler_params=_cp("parallel"),
    )(dy, proj, proj, yp, yl, wpu, wlu, wout)


def lru_bwd(proj, hs, dhl, cw, cb, wa, ba, wx, bx, lam, P, l):
    T = proj.shape[0]
    _, H, hd, _ = wa.shape
    R = H * hd
    CW = cw.shape[1]

    def body(u_ref, ug_ref, cw_ref, cb_ref, wa_ref, ba_ref, wx_ref, bx_ref, lam_ref, hs_ref, dhl_ref,
             du_ref, dug_ref, dcw_ref, dcb_ref, dwa_ref, dba_ref, dwx_ref, dbx_ref, dlam_ref, c_s, g_s, y_s):
        u = u_ref[...].astype(F32)
        v = _conv(u, cw_ref, cb_ref[0])
        lam = lam_ref[0]
        r, i, sp, a, mult = _lru_gates(v, wa_ref, ba_ref[0], wx_ref, bx_ref[0], lam)
        ug = ug_ref[...].astype(F32)
        ge, th = _gelu(ug)
        hs = hs_ref[...]
        dhl = dhl_ref[...].astype(F32)
        dug_ref[...] = (dhl * hs * _gelu_grad(ug, th)).astype(BF16)
        c_s[...] = _shift_up(a, 1)
        g_s[...] = dhl * ge
        _scan_bwd(c_s, g_s, y_s)
        y = y_s[...]
        da = y * _shift_down(hs, 1)
        iv = i * v
        dlog_a = da * a - (y * iv) * (a * a) / mult
        div = y * mult
        dpa = (dlog_a * (-LRU_C) * sp) * r * (1.0 - r)
        dpx = (div * v) * i * (1.0 - i)
        dsp = jnp.sum(dlog_a * (-LRU_C) * r, axis=0, keepdims=True)
        dlam_ref[0] = -dsp * _sigmoid(-lam)
        vb = v.astype(BF16)
        dpab, dpxb = dpa.astype(BF16), dpx.astype(BF16)
        dwa_ref[0, 0] = _dot_tn(vb, dpab)
        dwx_ref[0, 0] = _dot_tn(vb, dpxb)
        dba_ref[0] = jnp.sum(dpa, axis=0, keepdims=True)
        dbx_ref[0] = jnp.sum(dpx, axis=0, keepdims=True)
        dv = div * i + _dot_nt(dpab, wa_ref[0, 0].astype(BF16)) + _dot_nt(dpxb, wx_ref[0, 0].astype(BF16))
        dcb_ref[0] = jnp.sum(dv, axis=0, keepdims=True)
        du = jnp.zeros_like(dv)
        for k in range(CW):
            du = du + cw_ref[0, k:k + 1, :] * _shift_up(dv, CW - 1 - k)
            dcw_ref[0, k:k + 1, :] = jnp.sum(dv * _shift_down(u, CW - 1 - k), axis=0, keepdims=True)
        du_ref[...] = du.astype(BF16)

    col = pl.BlockSpec((T, hd), lambda h: (0, h))
    vec = pl.BlockSpec((1, 1, hd), lambda h: (0, 0, h))
    mat = pl.BlockSpec((1, 1, hd, hd), lambda h: (0, h, 0, 0))
    vshape = jax.ShapeDtypeStruct((1, 1, R), F32)
    mshape = jax.ShapeDtypeStruct((1, H, hd, hd), F32)
    return _pallas_call(
        body, name="lru_bwd", grid=(H,),
        in_specs=_lru_specs(T, hd, P, R, CW, l) + [col, col],
        out_specs=[col, col, pl.BlockSpec((1, CW, hd), lambda h: (0, 0, h)), vec, mat, vec, mat, vec, vec],
        out_shape=[jax.ShapeDtypeStruct((T, R), BF16)] * 2 + [jax.ShapeDtypeStruct((1, CW, R), F32), vshape, mshape, vshape, mshape, vshape, vshape],
        scratch_shapes=[pltpu.VMEM((T, hd), F32)] * 3,
        compiler_params=_cp("parallel"),
    )(proj, proj, cw, cb, wa, ba, wx, bx, lam, hs, dhl)


def pool_bwd(proj, dpm, pw, pb, ps, l):
    T = proj.shape[0]
    _, G, gd, _ = pw.shape
    P = G * gd

    def body(u_ref, d_ref, w_ref, b_ref, s_ref, du_ref, dw_ref, db_ref, dsc_ref):
        for gi in range(G):
            cols = slice(gi * gd, (gi + 1) * gd)
            w = POOL_WINDOWS[gi]
            inv = _inv_count(T, w)
            ug = u_ref[:, cols].astype(F32)
            pooled = _pooled(ug, w, inv).astype(BF16)
            wb = w_ref[0, gi].astype(BF16)
            mixed = _dot(pooled, wb) + b_ref[0, :, cols]
            dpm_g = d_ref[:, cols].astype(F32)
            dsc_ref[0, :, cols] = jnp.sum(dpm_g * mixed, axis=0, keepdims=True)
            dmixed = dpm_g * s_ref[0, :, cols]
            db_ref[0, :, cols] = jnp.sum(dmixed, axis=0, keepdims=True)
            dmb = dmixed.astype(BF16)
            dw_ref[0, gi] = _dot_tn(pooled, dmb)
            dpooled = _dot_nt(dmb, wb)
            s = dpooled * inv
            k = 1
            while k < w:
                s = s + _shift_up(s, k)
                k *= 2
            du_ref[:, cols] = (s - dpooled).astype(BF16)

    vec = pl.BlockSpec((1, 1, P), lambda i: (l, 0, 0))
    ovec = pl.BlockSpec((1, 1, P), lambda i: (0, 0, 0))
    return _pallas_call(
        body, name="pool_bwd", grid=(1,),
        in_specs=[pl.BlockSpec((T, P), lambda i: (0, 0)), pl.BlockSpec((T, P), lambda i: (0, 0)),
                  pl.BlockSpec((1, G, gd, gd), lambda i: (l, 0, 0, 0)), vec, vec],
        out_specs=[pl.BlockSpec((T, P), lambda i: (0, 0)), pl.BlockSpec((1, G, gd, gd), lambda i: (0, 0, 0, 0)), ovec, ovec],
        out_shape=[jax.ShapeDtypeStruct((T, P), BF16), jax.ShapeDtypeStruct((1, G, gd, gd), F32),
                   jax.ShapeDtypeStruct((1, 1, P), F32), jax.ShapeDtypeStruct((1, 1, P), F32)],
        compiler_params=_cp("arbitrary"),
    )(proj, dpm, pw, pb, ps)


def _place():
    x, y, c = lax.axis_index("x"), lax.axis_index("y"), lax.axis_index("c")
    return x, y, c


def all_gather(name, shards):
    n = len(shards)

    def body(*refs):
        src, out = refs[:n], refs[n:2 * n]
        send_sems, recv_sems, local_sems = refs[2 * n:]
        x, y, c = _place()
        sibling = (x, y, 1 - c)
        chips = [(x, 1 - y), (1 - x, y), (1 - x, 1 - y)]

        def slot(a, px, py, pc):
            return out[a].at[:, 4 * px + 2 * py + pc]

        def copy(a, k, block, to, from_src=False):
            return pltpu.make_async_remote_copy(
                src_ref=src[a] if from_src else slot(a, *block), dst_ref=slot(a, *block),
                send_sem=send_sems.at[a, k], recv_sem=recv_sems.at[a, k], device_id=to, device_id_type=MESH)

        me = (x, y, c)
        mine = [pltpu.make_async_copy(src[a], slot(a, *me), local_sems.at[a]) for a in range(n)]
        first = []
        for j, chip in enumerate(chips):
            for a in range(n):
                first.append(copy(a, 1 + j, me, (*chip, c), from_src=True))
        for a in range(n):
            first.append(copy(a, 0, me, sibling, from_src=True))
        for cp in mine + first:
            cp.start()
        passed = []
        for j, chip in enumerate(chips):
            for a in range(n):
                copy(a, 1 + j, (*chip, c), me).wait_recv()
                fwd = copy(a, 4 + j, (*chip, c), sibling)
                fwd.start()
                passed.append(fwd)
        for a in range(n):
            copy(a, 0, (x, y, 1 - c), me).wait_recv()
        for j, chip in enumerate(chips):
            for a in range(n):
                copy(a, 4 + j, (*chip, 1 - c), me).wait_recv()
        for cp in first + passed:
            cp.wait_send()
        for cp in mine:
            cp.wait()

    outs = _pallas_call(
        body, name=name,
        in_specs=[ANY] * n, out_specs=[ANY] * n,
        out_shape=[jax.ShapeDtypeStruct((s.shape[0], N_DEV) + s.shape[1:], s.dtype) for s in shards],
        scratch_shapes=[pltpu.SemaphoreType.DMA((n, 7)), pltpu.SemaphoreType.DMA((n, 7)), pltpu.SemaphoreType.DMA((n,))],
        compiler_params=pltpu.CompilerParams(has_side_effects=True),
    )(*shards)
    return list(outs)


def pair_exchange(name, g16):
    n = len(g16)

    def body(*refs):
        s16, recv = refs[:n], refs[n:2 * n]
        send_sems, recv_sems = refs[2 * n:]
        x, y, c = _place()
        sibling = (x, y, 1 - c)
        rem = []
        for a in range(n):
            for j in range(N_CHIP):
                rem.append(pltpu.make_async_remote_copy(
                    src_ref=s16[a].at[:, 2 * j + 1 - c], dst_ref=recv[a].at[:, j],
                    send_sem=send_sems.at[a, j], recv_sem=recv_sems.at[a, j], device_id=sibling, device_id_type=MESH))
        for cp in rem:
            cp.start()
        for cp in rem:
            cp.wait_recv()
        for cp in rem:
            cp.wait_send()

    outs = _pallas_call(
        body, name=name,
        in_specs=[ANY] * n, out_specs=[ANY] * n,
        out_shape=[jax.ShapeDtypeStruct((s.shape[0], N_CHIP) + s.shape[2:], s.dtype) for s in g16],
        scratch_shapes=[pltpu.SemaphoreType.DMA((n, N_CHIP))] * 2,
        compiler_params=pltpu.CompilerParams(has_side_effects=True),
    )(*g16)
    return list(outs)


def chip_exchange(name, pair16):
    n = len(pair16)

    def body(*refs):
        p16, recv2 = refs[:n], refs[n:2 * n]
        send_sems, recv_sems = refs[2 * n:]
        x, y, c = _place()
        rem = []
        for d in (1, 2, 3):
            px = 1 - x if d & 2 else x
            py = 1 - y if d & 1 else y
            for a in range(n):
                rem.append(pltpu.make_async_remote_copy(
                    src_ref=p16[a].at[:, 2 * px + py], dst_ref=recv2[a].at[:, d - 1],
                    send_sem=send_sems.at[a, d - 1], recv_sem=recv_sems.at[a, d - 1], device_id=(px, py, c), device_id_type=MESH))
        for cp in rem:
            cp.start()
        for cp in rem:
            cp.wait_recv()
        for cp in rem:
            cp.wait_send()

    outs = _pallas_call(
        body, name=name,
        in_specs=[ANY] * n, out_specs=[ANY] * n,
        out_shape=[jax.ShapeDtypeStruct((s.shape[0], 3) + s.shape[2:], s.dtype) for s in pair16],
        scratch_shapes=[pltpu.SemaphoreType.DMA((n, 3))] * 2,
        compiler_params=pltpu.CompilerParams(has_side_effects=True),
    )(*pair16)
    return list(outs)


def _rows_tile(rows, cols, budget=1 << 20):
    t = rows
    while t % 2 == 0 and t * cols > budget and (t // 2) % 16 == 0:
        t //= 2
    return t


def pair_sum(g32, recv1, place):
    L, _, rows, cols = recv1.shape
    tr = _rows_tile(rows, cols)

    def body(p_ref, m_ref, r_ref, o_ref):
        o_ref[...] = (m_ref[...] + r_ref[...].astype(F32)).astype(o_ref.dtype)

    blk = pl.BlockSpec((1, 1, tr, cols), lambda l, j, i, p: (l, j, i, 0))
    return _pallas_call(
        body, name="pair_sum",
        grid_spec=pltpu.PrefetchScalarGridSpec(
            num_scalar_prefetch=1, grid=(L, N_CHIP, rows // tr),
            in_specs=[pl.BlockSpec((1, 1, tr, cols), lambda l, j, i, p: (l, 2 * j + p[0], i, 0)), blk], out_specs=blk),
        out_shape=jax.ShapeDtypeStruct(recv1.shape, recv1.dtype), compiler_params=_cp("parallel", "parallel", "parallel"),
    )(place, g32, recv1)


def _grad_in_specs(tr, cols):
    return ([pl.BlockSpec((1, 1, tr, cols), lambda l, i, p: (l, p[2], i, 0)), pl.BlockSpec((1, 1, tr, cols), lambda l, i, p: (l, p[1], i, 0))]
            + [pl.BlockSpec((1, 1, tr, cols), lambda l, i, p, d=d: (l, d, i, 0)) for d in range(3)])


def _grad_total(o32, o16, r0, r1, r2):
    return (o32[0, 0] + o16[0, 0].astype(F32)) + r0[0, 0].astype(F32) + r1[0, 0].astype(F32) + r2[0, 0].astype(F32)


def grad_sum(g32, recv1, recv2, place):
    L, _, rows, cols = recv1.shape
    tr = _rows_tile(rows, cols)

    def body(p_ref, o32, o16, r0, r1, r2, g_ref):
        g_ref[0] = _grad_total(o32, o16, r0, r1, r2)

    return _pallas_call(
        body, name="grad_sum",
        grid_spec=pltpu.PrefetchScalarGridSpec(
            num_scalar_prefetch=1, grid=(L, rows // tr), in_specs=_grad_in_specs(tr, cols),
            out_specs=pl.BlockSpec((1, tr, cols), lambda l, i, p: (l, i, 0))),
        out_shape=jax.ShapeDtypeStruct((L, rows, cols), F32), compiler_params=_cp("parallel", "parallel"),
    )(place, g32, recv1, recv2, recv2, recv2)


def _adamw_math(w, g, m, v):
    m = ADAM_B1 * m + (1.0 - ADAM_B1) * g
    v = ADAM_B2 * v + (1.0 - ADAM_B2) * (g * g)
    m_hat = m / (1.0 - ADAM_B1 ** ADAM_STEP)
    v_hat = v / (1.0 - ADAM_B2 ** ADAM_STEP)
    delta = -ADAM_LR * (m_hat / (jnp.sqrt(v_hat) + ADAM_EPS) + ADAM_WD * w)
    return delta, m, v


def grad_sum_adamw(g32, recv1, recv2, w, m, v, place):
    L, rows, cols = w.shape
    tr = _rows_tile(rows, cols, 1 << 18)

    def body(p_ref, o32, o16, r0, r1, r2, w_ref, m_ref, v_ref, g_ref, d_ref, nm_ref, nv_ref):
        g = _grad_total(o32, o16, r0, r1, r2)
        d, nm, nv = _adamw_math(w_ref[0], g, m_ref[0], v_ref[0])
        g_ref[0] = g
        d_ref[0] = d
        nm_ref[0] = nm
        nv_ref[0] = nv

    blk = pl.BlockSpec((1, tr, cols), lambda l, i, p: (l, i, 0))
    return _pallas_call(
        body, name="grad_sum_adamw",
        grid_spec=pltpu.PrefetchScalarGridSpec(
            num_scalar_prefetch=1, grid=(L, rows // tr), in_specs=_grad_in_specs(tr, cols) + [blk] * 3, out_specs=[blk] * 4),
        out_shape=[jax.ShapeDtypeStruct((L, rows, cols), F32)] * 4, compiler_params=_cp("parallel", "parallel"),
    )(place, g32, recv1, recv2, recv2, recv2, w, m, v)


def adamw(w, g, m, v):
    rows, cols = w.shape
    tr = _rows_tile(rows, cols, 1 << 18)

    def body(w_ref, g_ref, m_ref, v_ref, d_ref, nm_ref, nv_ref):
        d, nm, nv = _adamw_math(w_ref[...], g_ref[...], m_ref[...], v_ref[...])
        d_ref[...] = d
        nm_ref[...] = nm
        nv_ref[...] = nv

    blk = pl.BlockSpec((tr, cols), lambda i: (i, 0))
    return _pallas_call(body, name="adamw_small", grid=(rows // tr,), in_specs=[blk] * 4, out_specs=[blk] * 3,
                        out_shape=[jax.ShapeDtypeStruct((rows, cols), F32)] * 3, compiler_params=_cp("parallel"))(w, g, m, v)


SMALL = ("norm_ffn1", "norm_mix", "pool_w", "pool_b", "pool_scale", "conv_w", "conv_b", "lru_w_a", "lru_b_a", "lru_w_x", "lru_b_x",
         "lru_lambda", "norm_ffn2", "final_norm")
BIG = ("ffn1_w_up", "ffn1_w_down", "w_in", "w_pool_up", "w_lru_up", "w_out", "ffn2_w_up", "ffn2_w_down")
NAMES = ("norm_ffn1", "ffn1_w_up", "ffn1_w_down", "norm_mix", "w_in", "pool_w", "pool_b", "pool_scale", "w_pool_up", "conv_w", "conv_b",
         "lru_w_a", "lru_b_a", "lru_w_x", "lru_b_x", "lru_lambda", "w_lru_up", "w_out", "norm_ffn2", "ffn2_w_up", "ffn2_w_down", "final_norm")
PACK_ROWS = 16 * N_DEV


def _pack(parts):
    flat = jnp.concatenate([p.reshape(-1) for p in parts])
    unit = 128 * PACK_ROWS
    padded = -(-flat.size // unit) * unit
    return jnp.pad(flat, (0, padded - flat.size)).reshape(-1, 128)


def _unpack(packed, shapes):
    flat = packed.reshape(-1)
    out, off = [], 0
    for s in shapes:
        n = 1
        for d in s:
            n *= d
        out.append(flat[off:off + n].reshape(s))
        off += n
    return out


def kernel(x, norm_ffn1, ffn1_w_up, ffn1_w_down, norm_mix, w_in, pool_w, pool_b, pool_scale, w_pool_up, conv_w, conv_b, lru_w_a, lru_b_a, lru_w_x, lru_b_x, lru_lambda, w_lru_up, w_out, norm_ffn2, ffn2_w_up, ffn2_w_down, final_norm, loss_target, m_norm_ffn1, m_ffn1_w_up, m_ffn1_w_down, m_norm_mix, m_w_in, m_pool_w, m_pool_b, m_pool_scale, m_w_pool_up, m_conv_w, m_conv_b, m_lru_w_a, m_lru_b_a, m_lru_w_x, m_lru_b_x, m_lru_lambda, m_w_lru_up, m_w_out, m_norm_ffn2, m_ffn2_w_up, m_ffn2_w_down, m_final_norm, v_norm_ffn1, v_ffn1_w_up, v_ffn1_w_down, v_norm_mix, v_w_in, v_pool_w, v_pool_b, v_pool_scale, v_w_pool_up, v_conv_w, v_conv_b, v_lru_w_a, v_lru_b_a, v_lru_w_x, v_lru_b_x, v_lru_lambda, v_w_lru_up, v_w_out, v_norm_ffn2, v_ffn2_w_up, v_ffn2_w_down, v_final_norm):
    W = dict(norm_ffn1=norm_ffn1, ffn1_w_up=ffn1_w_up, ffn1_w_down=ffn1_w_down, norm_mix=norm_mix, w_in=w_in, pool_w=pool_w, pool_b=pool_b,
             pool_scale=pool_scale, w_pool_up=w_pool_up, conv_w=conv_w, conv_b=conv_b, lru_w_a=lru_w_a, lru_b_a=lru_b_a, lru_w_x=lru_w_x,
             lru_b_x=lru_b_x, lru_lambda=lru_lambda, w_lru_up=w_lru_up, w_out=w_out, norm_ffn2=norm_ffn2, ffn2_w_up=ffn2_w_up,
             ffn2_w_down=ffn2_w_down, final_norm=final_norm)
    M = dict(norm_ffn1=m_norm_ffn1, ffn1_w_up=m_ffn1_w_up, ffn1_w_down=m_ffn1_w_down, norm_mix=m_norm_mix, w_in=m_w_in, pool_w=m_pool_w,
             pool_b=m_pool_b, pool_scale=m_pool_scale, w_pool_up=m_w_pool_up, conv_w=m_conv_w, conv_b=m_conv_b, lru_w_a=m_lru_w_a,
             lru_b_a=m_lru_b_a, lru_w_x=m_lru_w_x, lru_b_x=m_lru_b_x, lru_lambda=m_lru_lambda, w_lru_up=m_w_lru_up, w_out=m_w_out,
             norm_ffn2=m_norm_ffn2, ffn2_w_up=m_ffn2_w_up, ffn2_w_down=m_ffn2_w_down, final_norm=m_final_norm)
    V = dict(norm_ffn1=v_norm_ffn1, ffn1_w_up=v_ffn1_w_up, ffn1_w_down=v_ffn1_w_down, norm_mix=v_norm_mix, w_in=v_w_in, pool_w=v_pool_w,
             pool_b=v_pool_b, pool_scale=v_pool_scale, w_pool_up=v_w_pool_up, conv_w=v_conv_w, conv_b=v_conv_b, lru_w_a=v_lru_w_a,
             lru_b_a=v_lru_b_a, lru_w_x=v_lru_w_x, lru_b_x=v_lru_b_x, lru_lambda=v_lru_lambda, w_lru_up=v_w_lru_up, w_out=v_w_out,
             norm_ffn2=v_norm_ffn2, ffn2_w_up=v_ffn2_w_up, ffn2_w_down=v_ffn2_w_down, final_norm=v_final_norm)

    T, D = x.shape[1], x.shape[2]
    L = norm_ffn1.shape[0]
    P = pool_scale.shape[1]
    R = lru_lambda.shape[1]
    H, hd = lru_w_a.shape[1], lru_w_a.shape[2]
    CW = conv_w.shape[1]
    cs = ffn1_w_up.shape[2]
    ci = w_in.shape[2]
    xin = x.reshape(T, D)
    tgt = loss_target.reshape(T, D)
    dev = 4 * lax.axis_index("x") + 2 * lax.axis_index("y") + lax.axis_index("c")

    cw_flat = conv_w.reshape(L, -1)
    cw_pad = (-cw_flat.shape[1]) % 1024
    cw_tiles = jnp.pad(cw_flat, ((0, 0), (0, cw_pad))).reshape(L, -1, 128)
    gathered = all_gather("all_gather_weights", [W[n].astype(BF16) for n in BIG] + [cw_tiles])
    Wg = dict(zip(BIG + ("conv_w",), gathered))
    Wg["conv_w"] = Wg["conv_w"].reshape(L, N_DEV, -1)[:, :, :cw_flat.shape[1]].reshape((L, N_DEV) + conv_w.shape[1:])
    wup1, wup2, win = Wg["ffn1_w_up"], Wg["ffn2_w_up"], Wg["w_in"]
    wd1 = Wg["ffn1_w_down"].reshape(L, -1, D)
    wd2 = Wg["ffn2_w_down"].reshape(L, -1, D)
    wlu = Wg["w_lru_up"].reshape(L, R, D)
    wout = Wg["w_out"].reshape(L, D, D)
    wpu = Wg["w_pool_up"].transpose(0, 2, 1, 3).reshape(L, P, D)
    cw_full = Wg["conv_w"].transpose(0, 2, 1, 3).reshape(L, CW, R)

    vec = lambda a: a.reshape(L, 1, -1)
    g1, gm, g2 = vec(norm_ffn1), vec(norm_mix), vec(norm_ffn2)
    pb, ps = vec(pool_b), vec(pool_scale)
    cb, ba, bx, lam = vec(conv_b), vec(lru_b_a), vec(lru_b_x), vec(lru_lambda)

    saved = []
    xc = xin
    for l in range(L):
        sv = {"x1": xc}
        sv["h1"], sv["u1"], sv["s1"] = ffn_up(xc, g1, wup1, l)
        xc = ffn_down(sv["s1"], wd1, xc, l)
        sv["x2"] = xc
        sv["h2"], sv["proj"] = mix_in(xc, gm, win, l)
        sv["pm"] = pool_fwd(sv["proj"], pool_w, pb, ps, l)
        sv["hl"], sv["hs"] = lru_fwd(sv["proj"], cw_full, cb, lru_w_a, ba, lru_w_x, bx, lam, P, l)
        xc, sv["yp"], sv["yl"], sv["z"] = mix_out(sv["pm"], sv["hl"], sv["proj"], xc, wpu, wlu, wout, P, l)
        sv["x3"] = xc
        sv["h3"], sv["u3"], sv["s3"] = ffn_up(xc, g2, wup2, l)
        xc = ffn_down(sv["s3"], wd2, xc, l)
        saved.append(sv)

    loss_part, dx, d_final = loss_head(xc, final_norm.reshape(1, D), tgt)
    loss = lax.psum(loss_part[0, 0], ("x", "y", "c"))

    G = {n: None for n in BIG}
    small = {n: [None] * L for n in SMALL if n != "final_norm"}

    def ffn_bwd(dy, sv, tag, wup, wd, gn, up_name, dn_name, norm_name, l):
        dout, du = ffn_down_bwd(dy, wd, sv["u" + tag], l)
        du = du.reshape(N_DEV, T, cs)
        G[dn_name] = dw_tn("dw_down", [sv["s" + tag]], lambda tk: pl.BlockSpec((1, tk, cs), lambda g, k: (g, k, 0)),
                           [dout], lambda tk: pl.BlockSpec((tk, D), lambda g, k: (k, 0)), 4, cs, D, T, l, L, G[dn_name])
        G[up_name] = dw_tn("dw_up", [sv["h" + tag]], lambda tk: pl.BlockSpec((tk, D), lambda g, k: (k, 0)),
                           [du], lambda tk: pl.BlockSpec((1, tk, cs), lambda g, k: (g, k, 0)), N_DEV, D, cs, T, l, L, G[up_name])
        dxn, dg = dx_norm_bwd("ffn_dx", du, lambda tm: pl.BlockSpec((1, tm, cs), lambda i, j: (j, i, 0)), wup, N_DEV,
                              sv["x" + tag], gn, dy, l)
        small[norm_name][l] = dg.reshape(D)
        return dxn

    for l in reversed(range(L)):
        sv = saved[l]
        dx = ffn_bwd(dx, sv, "3", wup2, wd2, g2, "ffn2_w_up", "ffn2_w_down", "norm_ffn2", l)
        dyb, dyp, dyl, dgp, dgl, dpm, dhl = mix_out_bwd(dx, sv["proj"], sv["yp"], sv["yl"], wpu, wlu, wout, P, R, l)
        row = lambda w: (lambda tk: pl.BlockSpec((tk, w), lambda g, k: (k, 0)))
        G["w_out"] = dw_tn("dw_out", [sv["z"]], row(D), [dyb], row(D), 1, D, D, T, l, L, G["w_out"])
        G["w_lru_up"] = dw_tn("dw_lru_up", [sv["hl"]], row(R), [dyl], row(D), 1, R, D, T, l, L, G["w_lru_up"])
        G["w_pool_up"] = dw_tn("dw_pool_up", [sv["pm"]], row(P), [dyp], row(D), 1, P, D, T, l, L, G["w_pool_up"])
        du_lru, du_gelu, dcw, dcb, dwa, dba, dwx, dbx, dlam = lru_bwd(
            sv["proj"], sv["hs"], dhl, cw_full, cb, lru_w_a, ba, lru_w_x, bx, lam, P, l)
        du_pool, dpw, dpb, dpsc = pool_bwd(sv["proj"], dpm, pool_w, pb, ps, l)
        dproj = jnp.concatenate([du_pool, du_lru, du_gelu, dgp, dgl], axis=1)
        G["w_in"] = dw_tn("dw_in", [sv["h2"]], row(D), [dproj], lambda tk: pl.BlockSpec((tk, ci), lambda g, k: (k, g)),
                          N_DEV, D, ci, T, l, L, G["w_in"])
        dx, dgm = dx_norm_bwd("mix_dx", dproj, lambda tm: pl.BlockSpec((tm, ci), lambda i, j: (i, j)), win, N_DEV,
                              sv["x2"], gm, dx, l)
        small["norm_mix"][l] = dgm.reshape(D)
        small["pool_w"][l], small["pool_b"][l], small["pool_scale"][l] = dpw[0], dpb.reshape(pool_b.shape[1:]), dpsc.reshape(P)
        small["conv_w"][l], small["conv_b"][l] = dcw[0], dcb.reshape(R)
        small["lru_w_a"][l], small["lru_b_a"][l] = dwa[0], dba.reshape(H, hd)
        small["lru_w_x"][l], small["lru_b_x"][l] = dwx[0], dbx.reshape(H, hd)
        small["lru_lambda"][l] = dlam.reshape(R)
        dx = ffn_bwd(dx, sv, "1", wup1, wd1, g1, "ffn1_w_up", "ffn1_w_down", "norm_ffn1", l)

    grad_x = dx.reshape(x.shape)

    def to_slots(name, a):
        if name == "w_pool_up":
            return a.reshape(L, P, N_DEV, D // N_DEV).transpose(0, 2, 1, 3)
        return a.reshape((L, N_DEV) + W[name].shape[1:])

    small_parts = [jnp.stack(small[n]) for n in SMALL if n != "final_norm"] + [d_final.reshape(D)]
    small_shapes = [p.shape for p in small_parts]
    gpack = _pack(small_parts)
    gpack = gpack.reshape(1, N_DEV, -1, 128)
    g32 = [to_slots(n, G[n][0]) for n in BIG] + [gpack]
    g16 = [to_slots(n, G[n][1]) for n in BIG] + [gpack]
    place = jnp.stack([lax.axis_index("c"), 2 * lax.axis_index("x") + lax.axis_index("y"), dev]).astype(jnp.int32)
    recv1 = pair_exchange("rs_pair_exchange", g16)
    pair16 = [pair_sum(g_, r_, place) for g_, r_ in zip(g32, recv1)]
    recv2 = chip_exchange("rs_chip_exchange", pair16)

    out_g, out_d, out_m, out_v = {}, {}, {}, {}
    for i, n in enumerate(BIG):
        out_g[n], out_d[n], out_m[n], out_v[n] = grad_sum_adamw(g32[i], recv1[i], recv2[i], W[n], M[n], V[n], place)

    gs = grad_sum(g32[-1], recv1[-1], recv2[-1], place)
    gs_all = all_gather("all_gather_small_grads", [gs])[0]
    gs_all = gs_all.reshape(-1, 128)
    small_g = dict(zip(SMALL, _unpack(gs_all, small_shapes)))
    full_shapes = [W[n].shape if n != "conv_w" else small_shapes[SMALL.index("conv_w")] for n in SMALL]
    rep = [n for n in SMALL if n != "conv_w"]
    rep_shapes = [W[n].shape for n in rep]
    wp, mp, vp = (_pack([S[n] for n in rep]) for S in (W, M, V))
    gp = _pack([small_g[n] for n in rep])
    dp, nmp, nvp = adamw(wp, gp, mp, vp)
    for S, packed in ((out_d, dp), (out_m, nmp), (out_v, nvp)):
        S.update(zip(rep, _unpack(packed, rep_shapes)))
    for n in rep:
        out_g[n] = small_g[n]
    cwc = conv_w.shape[2]
    gcw = lax.dynamic_slice_in_dim(small_g["conv_w"], dev * cwc, cwc, axis=2)
    cw2 = lambda a: a.reshape(-1, cwc)
    pad_rows = (-cw2(conv_w).shape[0]) % 8
    padr = lambda a: jnp.pad(cw2(a), ((0, pad_rows), (0, 0)))
    dcw_, mcw_, vcw_ = adamw(padr(conv_w), padr(gcw), padr(M["conv_w"]), padr(V["conv_w"]))
    nrow = cw2(conv_w).shape[0]
    out_g["conv_w"] = gcw
    out_d["conv_w"], out_m["conv_w"], out_v["conv_w"] = (a[:nrow].reshape(conv_w.shape) for a in (dcw_, mcw_, vcw_))
    del full_shapes

    return (loss, grad_x, *[out_g[n] for n in NAMES], *[out_d[n] for n in NAMES], *[out_m[n] for n in NAMES], *[out_v[n] for n in NAMES])
```

```python
import functools

import jax
import jax.numpy as jnp
from jax import lax
from jax.experimental import pallas as pl
from jax.experimental.pallas import tpu as pltpu

F32, BF16 = jnp.float32, jnp.bfloat16
EPS = 1e-6
LRU_C = 8.0
POOL_WINDOWS = (2, 4, 8, 16)
ADAM_LR, ADAM_B1, ADAM_B2, ADAM_EPS, ADAM_WD, ADAM_STEP = 0.001, 0.9, 0.999, 1e-08, 0.01, 10
N_DEV = 8
N_CHIP = 4
MESH = pl.DeviceIdType.MESH
V7X_VMEM_LIMIT = 56 * 1024 * 1024
ROW_TILE = 512
ANY = pl.BlockSpec(memory_space=pl.ANY)

_pallas_call = pl.pallas_call


def _cp(*sem):
    return pltpu.CompilerParams(dimension_semantics=sem if sem else None, vmem_limit_bytes=V7X_VMEM_LIMIT)


def _tile(n, t=ROW_TILE):
    t = min(n, t)
    assert n % t == 0, (n, t)
    return t


def _dot(a, b):
    return jnp.dot(a, b, preferred_element_type=F32)


def _dot_nt(a, b):
    return lax.dot_general(a, b, (((1,), (1,)), ((), ())), preferred_element_type=F32)


def _dot_tn(a, b):
    return lax.dot_general(a, b, (((0,), (0,)), ((), ())), preferred_element_type=F32)


def _rms(xv):
    r = lax.rsqrt(jnp.mean(xv * xv, axis=-1, keepdims=True) + EPS)
    return xv * r, r


def _rms_bwd(dh, xv, gv, dy):
    n, r = _rms(xv)
    dn = dh * gv
    dx = dy + r * (dn - n * jnp.mean(dn * n, axis=-1, keepdims=True))
    return dx, jnp.sum(dh * n, axis=0, keepdims=True)


def _shift_down(x, k, fill=0.0):
    if k == 0:
        return x
    rows = lax.broadcasted_iota(jnp.int32, x.shape, 0)
    return jnp.where(rows >= k, pltpu.roll(x, k, 0), fill)


def _shift_up(x, k, fill=0.0):
    if k == 0:
        return x
    n = x.shape[0]
    rows = lax.broadcasted_iota(jnp.int32, x.shape, 0)
    return jnp.where(rows < n - k, pltpu.roll(x, n - k, 0), fill)


def _sigmoid(x):
    return 1.0 / (1.0 + jnp.exp(-x))


_GELU_K = 0.7978845608028654
_GELU_C = 0.044715


def _gelu(x):
    th = jnp.tanh(_GELU_K * (x + _GELU_C * x * x * x))
    return 0.5 * x * (1.0 + th), th


def _gelu_grad(x, th):
    return 0.5 * (1.0 + th) + 0.5 * x * (1.0 - th * th) * _GELU_K * (1.0 + 3.0 * _GELU_C * x * x)


def ffn_up(x, g, wup, l):
    T, D = x.shape
    cs = wup.shape[-1]
    tm = _tile(T)

    def body(x_ref, g_ref, wa_ref, wb_ref, h_ref, u_ref, s_ref, hs_ref):
        @pl.when(pl.program_id(1) == 0)
        def _():
            n, _r = _rms(x_ref[...])
            hv = (n * g_ref[0]).astype(BF16)
            hs_ref[...] = hv
            h_ref[...] = hv

        hv = hs_ref[...]
        a = _dot(hv, wa_ref[0, 0])
        b = _dot(hv, wb_ref[0, 0])
        u_ref[0, 0] = a.astype(BF16)
        u_ref[1, 0] = b.astype(BF16)
        s_ref[0] = (a * _sigmoid(a) * b).astype(BF16)

    return _pallas_call(
        body, name="ffn_up", grid=(T // tm, 4),
        in_specs=[pl.BlockSpec((tm, D), lambda i, j: (i, 0)), pl.BlockSpec((1, 1, D), lambda i, j: (l, 0, 0)),
                  pl.BlockSpec((1, 1, D, cs), lambda i, j: (l, j, 0, 0)), pl.BlockSpec((1, 1, D, cs), lambda i, j: (l, j + 4, 0, 0))],
        out_specs=[pl.BlockSpec((tm, D), lambda i, j: (i, 0)), pl.BlockSpec((2, 1, tm, cs), lambda i, j: (0, j, i, 0)),
                   pl.BlockSpec((1, tm, cs), lambda i, j: (j, i, 0))],
        out_shape=[jax.ShapeDtypeStruct((T, D), BF16), jax.ShapeDtypeStruct((2, 4, T, cs), BF16), jax.ShapeDtypeStruct((4, T, cs), BF16)],
        scratch_shapes=[pltpu.VMEM((tm, D), BF16)],
        compiler_params=_cp("parallel", "arbitrary"),
    )(x, g, wup, wup)


def ffn_down(s, wd, x, l):
    _, T, cs = s.shape
    D = x.shape[1]
    tm = _tile(T)

    def body(s_ref, w_ref, x_ref, o_ref, acc_ref):
        j = pl.program_id(1)

        @pl.when(j == 0)
        def _():
            acc_ref[...] = jnp.zeros_like(acc_ref)

        acc_ref[...] += _dot(s_ref[0], w_ref[0])

        @pl.when(j == 3)
        def _():
            o_ref[...] = x_ref[...] + 0.5 * acc_ref[...]

    return _pallas_call(
        body, name="ffn_down", grid=(T // tm, 4),
        in_specs=[pl.BlockSpec((1, tm, cs), lambda i, j: (j, i, 0)), pl.BlockSpec((1, cs, D), lambda i, j: (l, j, 0)),
                  pl.BlockSpec((tm, D), lambda i, j: (i, 0))],
        out_specs=pl.BlockSpec((tm, D), lambda i, j: (i, 0)),
        out_shape=jax.ShapeDtypeStruct((T, D), F32),
        scratch_shapes=[pltpu.VMEM((tm, D), F32)],
        compiler_params=_cp("parallel", "arbitrary"),
    )(s, wd, x)


def mix_in(x, g, win, l):
    T, D = x.shape
    ci = win.shape[-1]
    tm = _tile(T)

    def body(x_ref, g_ref, w_ref, h_ref, p_ref, hs_ref):
        @pl.when(pl.program_id(1) == 0)
        def _():
            n, _r = _rms(x_ref[...])
            hv = (n * g_ref[0]).astype(BF16)
            hs_ref[...] = hv
            h_ref[...] = hv

        p_ref[...] = _dot(hs_ref[...], w_ref[0, 0]).astype(BF16)

    return _pallas_call(
        body, name="mix_in", grid=(T // tm, N_DEV),
        in_specs=[pl.BlockSpec((tm, D), lambda i, j: (i, 0)), pl.BlockSpec((1, 1, D), lambda i, j: (l, 0, 0)),
                  pl.BlockSpec((1, 1, D, ci), lambda i, j: (l, j, 0, 0))],
        out_specs=[pl.BlockSpec((tm, D), lambda i, j: (i, 0)), pl.BlockSpec((tm, ci), lambda i, j: (i, j))],
        out_shape=[jax.ShapeDtypeStruct((T, D), BF16), jax.ShapeDtypeStruct((T, N_DEV * ci), BF16)],
        scratch_shapes=[pltpu.VMEM((tm, D), BF16)],
        compiler_params=_cp("parallel", "arbitrary"),
    )(x, g, win)


def _inv_count(T, w):
    t = lax.broadcasted_iota(jnp.int32, (T, 1), 0)
    return 1.0 / jnp.minimum(t + 1, w).astype(F32)


def _pooled(ug, w, inv):
    s = ug
    k = 1
    while k < w:
        s = s + _shift_down(s, k)
        k *= 2
    return s * inv - ug


def pool_fwd(proj, pw, pb, ps, l):
    T = proj.shape[0]
    _, G, gd, _ = pw.shape
    P = G * gd

    def body(u_ref, w_ref, b_ref, s_ref, o_ref):
        for gi in range(G):
            cols = slice(gi * gd, (gi + 1) * gd)
            ug = u_ref[:, cols].astype(F32)
            pooled = _pooled(ug, POOL_WINDOWS[gi], _inv_count(T, POOL_WINDOWS[gi]))
            mixed = _dot(pooled.astype(BF16), w_ref[0, gi].astype(BF16)) + b_ref[0, :, cols]
            o_ref[:, cols] = (mixed * s_ref[0, :, cols]).astype(BF16)

    return _pallas_call(
        body, name="pool_fwd", grid=(1,),
        in_specs=[pl.BlockSpec((T, P), lambda i: (0, 0)), pl.BlockSpec((1, G, gd, gd), lambda i: (l, 0, 0, 0)),
                  pl.BlockSpec((1, 1, P), lambda i: (l, 0, 0)), pl.BlockSpec((1, 1, P), lambda i: (l, 0, 0))],
        out_specs=pl.BlockSpec((T, P), lambda i: (0, 0)),
        out_shape=jax.ShapeDtypeStruct((T, P), BF16),
        compiler_params=_cp("arbitrary"),
    )(proj, pw, pb, ps)


def _conv(u, cw_ref, cb):
    CW = cw_ref.shape[1]
    v = cb
    for k in range(CW):
        v = v + cw_ref[0, k:k + 1, :] * _shift_down(u, CW - 1 - k)
    return v


def _softplus(z):
    return jnp.maximum(z, 0.0) + jnp.log1p(jnp.exp(-jnp.abs(z)))


def _lru_gates(v, wa_ref, ba, wx_ref, bx, lam):
    vb = v.astype(BF16)
    r = _sigmoid(_dot(vb, wa_ref[0, 0].astype(BF16)) + ba)
    i = _sigmoid(_dot(vb, wx_ref[0, 0].astype(BF16)) + bx)
    sp = _softplus(-lam)
    log_a = -LRU_C * r * sp
    a = jnp.exp(log_a)
    mult = jnp.sqrt(-jnp.tanh(log_a) * (a * a + 1.0))
    return r, i, sp, a, mult


def _scan_fwd(a_ref, b_ref, o_ref):
    T, W = a_ref.shape
    rows = lax.broadcasted_iota(jnp.int32, (8, W), 0)

    def step(t, carry):
        r0 = pl.multiple_of(t * 8, 8)
        A = a_ref[pl.ds(r0, 8), :]
        B = b_ref[pl.ds(r0, 8), :]
        for s in (1, 2, 4):
            keep = rows >= s
            As = jnp.where(keep, pltpu.roll(A, s, 0), 1.0)
            Bs = jnp.where(keep, pltpu.roll(B, s, 0), 0.0)
            B = A * Bs + B
            A = A * As
        h = B + A * carry
        o_ref[pl.ds(r0, 8), :] = h
        return jnp.broadcast_to(h[7:8, :], (8, W))

    lax.fori_loop(0, T // 8, step, jnp.zeros((8, W), F32), unroll=8)


def _scan_bwd(a_ref, b_ref, o_ref):
    T, W = a_ref.shape
    rows = lax.broadcasted_iota(jnp.int32, (8, W), 0)
    nt = T // 8

    def step(t, carry):
        r0 = pl.multiple_of((nt - 1 - t) * 8, 8)
        A = a_ref[pl.ds(r0, 8), :]
        B = b_ref[pl.ds(r0, 8), :]
        for s in (1, 2, 4):
            keep = rows < 8 - s
            As = jnp.where(keep, pltpu.roll(A, 8 - s, 0), 1.0)
            Bs = jnp.where(keep, pltpu.roll(B, 8 - s, 0), 0.0)
            B = A * Bs + B
            A = A * As
        y = B + A * carry
        o_ref[pl.ds(r0, 8), :] = y
        return jnp.broadcast_to(y[0:1, :], (8, W))

    lax.fori_loop(0, nt, step, jnp.zeros((8, W), F32), unroll=8)


def _lru_specs(T, hd, P, R, CW, l):
    ob, gb = P // hd, (P + R) // hd
    vec = pl.BlockSpec((1, 1, hd), lambda h: (l, 0, h))
    mat = pl.BlockSpec((1, 1, hd, hd), lambda h: (l, h, 0, 0))
    return [pl.BlockSpec((T, hd), lambda h: (0, ob + h)), pl.BlockSpec((T, hd), lambda h: (0, gb + h)),
            pl.BlockSpec((1, CW, hd), lambda h: (l, 0, h)), vec, mat, vec, mat, vec, vec]


def lru_fwd(proj, cw, cb, wa, ba, wx, bx, lam, P, l):
    T = proj.shape[0]
    _, H, hd, _ = wa.shape
    R = H * hd
    CW = cw.shape[1]
    assert P % hd == 0 and T % 8 == 0

    def body(u_ref, ug_ref, cw_ref, cb_ref, wa_ref, ba_ref, wx_ref, bx_ref, lam_ref, hl_ref, hs_ref, a_s, b_s):
        v = _conv(u_ref[...].astype(F32), cw_ref, cb_ref[0])
        _r, i, _sp, a, mult = _lru_gates(v, wa_ref, ba_ref[0], wx_ref, bx_ref[0], lam_ref[0])
        a_s[...] = a
        b_s[...] = mult * (i * v)
        _scan_fwd(a_s, b_s, hs_ref)
        ge, _th = _gelu(ug_ref[...].astype(F32))
        hl_ref[...] = (hs_ref[...] * ge).astype(BF16)

    out = pl.BlockSpec((T, hd), lambda h: (0, h))
    return _pallas_call(
        body, name="lru_fwd", grid=(H,),
        in_specs=_lru_specs(T, hd, P, R, CW, l),
        out_specs=[out, out],
        out_shape=[jax.ShapeDtypeStruct((T, R), BF16), jax.ShapeDtypeStruct((T, R), F32)],
        scratch_shapes=[pltpu.VMEM((T, hd), F32)] * 2,
        compiler_params=_cp("parallel"),
    )(proj, proj, cw, cb, wa, ba, wx, bx, lam)


def mix_out(pm, hl, proj, x, wpu, wlu, wout, P, l):
    T, D = x.shape
    R = hl.shape[1]
    tm = _tile(T)
    assert (P + 2 * R) % D == 0
    gb = (P + 2 * R) // D

    def body(pm_ref, hl_ref, gp_ref, gl_ref, x_ref, wpu_ref, wlu_ref, wo_ref, o_ref, yp_ref, yl_ref, z_ref):
        yp = _dot(pm_ref[...], wpu_ref[0])
        yl = _dot(hl_ref[...], wlu_ref[0])
        z = (_sigmoid(gp_ref[...].astype(F32)) * yp + _sigmoid(gl_ref[...].astype(F32)) * yl).astype(BF16)
        yp_ref[...] = yp.astype(BF16)
        yl_ref[...] = yl.astype(BF16)
        z_ref[...] = z
        o_ref[...] = x_ref[...] + _dot(z, wo_ref[0])

    row = lambda w: pl.BlockSpec((tm, w), lambda i: (i, 0))
    return _pallas_call(
        body, name="mix_out", grid=(T // tm,),
        in_specs=[row(P), row(R), pl.BlockSpec((tm, D), lambda i: (i, gb)), pl.BlockSpec((tm, D), lambda i: (i, gb + 1)), row(D),
                  pl.BlockSpec((1, P, D), lambda i: (l, 0, 0)), pl.BlockSpec((1, R, D), lambda i: (l, 0, 0)),
                  pl.BlockSpec((1, D, D), lambda i: (l, 0, 0))],
        out_specs=[row(D)] * 4,
        out_shape=[jax.ShapeDtypeStruct((T, D), F32)] + [jax.ShapeDtypeStruct((T, D), BF16)] * 3,
        compiler_params=_cp("parallel"),
    )(pm, hl, proj, proj, x, wpu, wlu, wout)


def loss_head(x, gf, tgt):
    T, D = x.shape
    tm = _tile(T)

    def body(x_ref, g_ref, t_ref, loss_ref, dx_ref, dg_ref):
        @pl.when(pl.program_id(0) == 0)
        def _():
            loss_ref[...] = jnp.zeros_like(loss_ref)
            dg_ref[...] = jnp.zeros_like(dg_ref)

        xv = x_ref[...]
        gv = g_ref[...]
        n, _r = _rms(xv)
        e = n * gv - t_ref[...]
        loss_ref[...] += 0.5 * jnp.sum(jnp.sum(e * e, axis=-1, keepdims=True), axis=0, keepdims=True) / D
        dx, dg = _rms_bwd(e * (1.0 / D), xv, gv, 0.0)
        dx_ref[...] = dx
        dg_ref[...] += dg

    return _pallas_call(
        body, name="loss_head", grid=(T // tm,),
        in_specs=[pl.BlockSpec((tm, D), lambda i: (i, 0)), pl.BlockSpec((1, D), lambda i: (0, 0)), pl.BlockSpec((tm, D), lambda i: (i, 0))],
        out_specs=[pl.BlockSpec((1, 1), lambda i: (0, 0)), pl.BlockSpec((tm, D), lambda i: (i, 0)), pl.BlockSpec((1, D), lambda i: (0, 0))],
        out_shape=[jax.ShapeDtypeStruct((1, 1), F32), jax.ShapeDtypeStruct((T, D), F32), jax.ShapeDtypeStruct((1, D), F32)],
        compiler_params=_cp("arbitrary"),
    )(x, gf, tgt)


def ffn_down_bwd(dy, wd, u, l, deps=()):
    T, D = dy.shape
    cs = u.shape[-1]
    tm = _tile(T)

    def body(dy_ref, w_ref, u_ref, *rest):
        do_ref, du_ref, dyb_ref = rest[len(deps):]
        @pl.when(pl.program_id(1) == 0)
        def _():
            d = (0.5 * dy_ref[...]).astype(BF16)
            dyb_ref[...] = d
            do_ref[...] = d

        ds = _dot_nt(dyb_ref[...], w_ref[0])
        a = u_ref[0, 0].astype(F32)
        b = u_ref[1, 0].astype(F32)
        sg = _sigmoid(a)
        du_ref[0, 0] = (ds * b * (sg * (1.0 + a * (1.0 - sg)))).astype(BF16)
        du_ref[1, 0] = (ds * (a * sg)).astype(BF16)

    blk = pl.BlockSpec((2, 1, tm, cs), lambda i, j: (0, j, i, 0))
    return _pallas_call(
        body, name="ffn_down_bwd", grid=(T // tm, 4),
        in_specs=[pl.BlockSpec((tm, D), lambda i, j: (i, 0)), pl.BlockSpec((1, cs, D), lambda i, j: (l, j, 0)), blk] + [ANY] * len(deps),
        out_specs=[pl.BlockSpec((tm, D), lambda i, j: (i, 0)), blk],
        out_shape=[jax.ShapeDtypeStruct((T, D), BF16), jax.ShapeDtypeStruct((2, 4, T, cs), BF16)],
        scratch_shapes=[pltpu.VMEM((tm, D), BF16)],
        compiler_params=_cp("parallel", "arbitrary"),
    )(dy, wd, u, *deps)


def dw_tn(name, a_ops, a_spec, b_ops, b_spec, G, M, N, T, l, L, prev):
    tk = _tile(T)
    nk = T // tk

    def body(*refs):
        a_refs, b_refs = refs[:len(a_ops)], refs[len(a_ops):len(a_ops) + len(b_ops)]
        o32_ref, o16_ref, acc_ref = refs[-3:]
        k = pl.program_id(1)

        @pl.when(k == 0)
        def _():
            acc_ref[...] = jnp.zeros_like(acc_ref)

        av = a_refs[0][0] if len(a_refs[0].shape) == 3 else a_refs[0][...]
        bv = b_refs[0][0] if len(b_refs[0].shape) == 3 else b_refs[0][...]
        acc_ref[...] += _dot_tn(av, bv)

        @pl.when(k == nk - 1)
        def _():
            o32_ref[0, 0] = acc_ref[...]
            o16_ref[0, 0] = acc_ref[...].astype(BF16)

    n_in = len(a_ops) + len(b_ops)
    in_specs = [a_spec(tk), b_spec(tk)]
    args = list(a_ops) + list(b_ops)
    aliases = {}
    if prev is not None:
        in_specs += [ANY, ANY]
        args += list(prev)
        aliases = {n_in: 0, n_in + 1: 1}

    def body_wrap(*refs):
        if prev is not None:
            refs = refs[:n_in] + refs[n_in + 2:]
        body(*refs)

    out = pl.BlockSpec((1, 1, M, N), lambda g, k: (l, g, 0, 0))
    return _pallas_call(
        body_wrap, name=name, grid=(G, nk),
        in_specs=in_specs, out_specs=[out, out],
        out_shape=[jax.ShapeDtypeStruct((L, G, M, N), F32), jax.ShapeDtypeStruct((L, G, M, N), BF16)],
        scratch_shapes=[pltpu.VMEM((M, N), F32)],
        input_output_aliases=aliases,
        compiler_params=_cp("parallel", "arbitrary"),
    )(*args)


def dx_norm_bwd(name, dact, d_spec, w, G, x, g, dy, l):
    T, D = x.shape
    c = w.shape[-1]
    tm = _tile(T)

    def body(d_ref, w_ref, x_ref, g_ref, dy_ref, dx_ref, dg_ref, acc_ref):
        i, j = pl.program_id(0), pl.program_id(1)

        @pl.when(jnp.logical_and(i == 0, j == 0))
        def _():
            dg_ref[...] = jnp.zeros_like(dg_ref)

        @pl.when(j == 0)
        def _():
            acc_ref[...] = jnp.zeros_like(acc_ref)

        dv = d_ref[0] if len(d_ref.shape) == 3 else d_ref[...]
        acc_ref[...] += _dot_nt(dv, w_ref[0, 0])

        @pl.when(j == G - 1)
        def _():
            dx, dg = _rms_bwd(acc_ref[...], x_ref[...], g_ref[0], dy_ref[...])
            dx_ref[...] = dx
            dg_ref[...] += dg

    row = pl.BlockSpec((tm, D), lambda i, j: (i, 0))
    return _pallas_call(
        body, name=name, grid=(T // tm, G),
        in_specs=[d_spec(tm), pl.BlockSpec((1, 1, D, c), lambda i, j: (l, j, 0, 0)), row, pl.BlockSpec((1, 1, D), lambda i, j: (l, 0, 0)), row],
        out_specs=[row, pl.BlockSpec((1, D), lambda i, j: (0, 0))],
        out_shape=[jax.ShapeDtypeStruct((T, D), F32), jax.ShapeDtypeStruct((1, D), F32)],
        scratch_shapes=[pltpu.VMEM((tm, D), F32)],
        compiler_params=_cp("arbitrary", "arbitrary"),
    )(dact, w, x, g, dy)


def mix_out_bwd(dy, proj, yp, yl, wpu, wlu, wout, P, R, l):
    T, D = dy.shape
    tm = _tile(T)
    gb = (P + 2 * R) // D

    def body(dy_ref, gp_ref, gl_ref, yp_ref, yl_ref, wpu_ref, wlu_ref, wo_ref,
             dyb_ref, dyp_ref, dyl_ref, dgp_ref, dgl_ref, dpm_ref, dhl_ref):
        dyb = dy_ref[...].astype(BF16)
        dyb_ref[...] = dyb
        dz = _dot_nt(dyb, wo_ref[0])
        sp = _sigmoid(gp_ref[...].astype(F32))
        sl = _sigmoid(gl_ref[...].astype(F32))
        dgp_ref[...] = (dz * yp_ref[...].astype(F32) * sp * (1.0 - sp)).astype(BF16)
        dgl_ref[...] = (dz * yl_ref[...].astype(F32) * sl * (1.0 - sl)).astype(BF16)
        dyp = (dz * sp).astype(BF16)
        dyl = (dz * sl).astype(BF16)
        dyp_ref[...] = dyp
        dyl_ref[...] = dyl
        dpm_ref[...] = _dot_nt(dyp, wpu_ref[0]).astype(BF16)
        dhl_ref[...] = _dot_nt(dyl, wlu_ref[0]).astype(BF16)

    row = lambda w: pl.BlockSpec((tm, w), lambda i: (i, 0))
    return _pallas_call(
        body, name="mix_out_bwd", grid=(T // tm,),
        in_specs=[row(D), pl.BlockSpec((tm, D), lambda i: (i, gb)), pl.BlockSpec((tm, D), lambda i: (i, gb + 1)), row(D), row(D),
                  pl.BlockSpec((1, P, D), lambda i: (l, 0, 0)), pl.BlockSpec((1, R, D), lambda i: (l, 0, 0)),
                  pl.BlockSpec((1, D, D), lambda i: (l, 0, 0))],
        out_specs=[row(D)] * 5 + [row(P), row(R)],
        out_shape=[jax.ShapeDtypeStruct((T, D), BF16)] * 5 + [jax.ShapeDtypeStruct((T, P), BF16), jax.ShapeDtypeStruct((T, R), BF16)],
        compiler_params=_cp("parallel"),
    )(dy, proj, proj, yp, yl, wpu, wlu, wout)


def lru_bwd(proj, hs, dhl, cw, cb, wa, ba, wx, bx, lam, P, l):
    T = proj.shape[0]
    _, H, hd, _ = wa.shape
    R = H * hd
    CW = cw.shape[1]

    def body(u_ref, ug_ref, cw_ref, cb_ref, wa_ref, ba_ref, wx_ref, bx_ref, lam_ref, hs_ref, dhl_ref,
             du_ref, dug_ref, dcw_ref, dcb_ref, dwa_ref, dba_ref, dwx_ref, dbx_ref, dlam_ref, c_s, g_s, y_s):
        u = u_ref[...].astype(F32)
        v = _conv(u, cw_ref, cb_ref[0])
        lam = lam_ref[0]
        r, i, sp, a, mult = _lru_gates(v, wa_ref, ba_ref[0], wx_ref, bx_ref[0], lam)
        ug = ug_ref[...].astype(F32)
        ge, th = _gelu(ug)
        hs = hs_ref[...]
        dhl = dhl_ref[...].astype(F32)
        dug_ref[...] = (dhl * hs * _gelu_grad(ug, th)).astype(BF16)
        c_s[...] = _shift_up(a, 1)
        g_s[...] = dhl * ge
        _scan_bwd(c_s, g_s, y_s)
        y = y_s[...]
        da = y * _shift_down(hs, 1)
        iv = i * v
        dlog_a = da * a - (y * iv) * (a * a) / mult
        div = y * mult
        dpa = (dlog_a * (-LRU_C) * sp) * r * (1.0 - r)
        dpx = (div * v) * i * (1.0 - i)
        dsp = jnp.sum(dlog_a * (-LRU_C) * r, axis=0, keepdims=True)
        dlam_ref[0] = -dsp * _sigmoid(-lam)
        vb = v.astype(BF16)
        dpab, dpxb = dpa.astype(BF16), dpx.astype(BF16)
        dwa_ref[0, 0] = _dot_tn(vb, dpab)
        dwx_ref[0, 0] = _dot_tn(vb, dpxb)
        dba_ref[0] = jnp.sum(dpa, axis=0, keepdims=True)
        dbx_ref[0] = jnp.sum(dpx, axis=0, keepdims=True)
        dv = div * i + _dot_nt(dpab, wa_ref[0, 0].astype(BF16)) + _dot_nt(dpxb, wx_ref[0, 0].astype(BF16))
        dcb_ref[0] = jnp.sum(dv, axis=0, keepdims=True)
        du = jnp.zeros_like(dv)
        for k in range(CW):
            du = du + cw_ref[0, k:k + 1, :] * _shift_up(dv, CW - 1 - k)
            dcw_ref[0, k:k + 1, :] = jnp.sum(dv * _shift_down(u, CW - 1 - k), axis=0, keepdims=True)
        du_ref[...] = du.astype(BF16)

    col = pl.BlockSpec((T, hd), lambda h: (0, h))
    vec = pl.BlockSpec((1, 1, hd), lambda h: (0, 0, h))
    mat = pl.BlockSpec((1, 1, hd, hd), lambda h: (0, h, 0, 0))
    vshape = jax.ShapeDtypeStruct((1, 1, R), F32)
    mshape = jax.ShapeDtypeStruct((1, H, hd, hd), F32)
    return _pallas_call(
        body, name="lru_bwd", grid=(H,),
        in_specs=_lru_specs(T, hd, P, R, CW, l) + [col, col],
        out_specs=[col, col, pl.BlockSpec((1, CW, hd), lambda h: (0, 0, h)), vec, mat, vec, mat, vec, vec],
        out_shape=[jax.ShapeDtypeStruct((T, R), BF16)] * 2 + [jax.ShapeDtypeStruct((1, CW, R), F32), vshape, mshape, vshape, mshape, vshape, vshape],
        scratch_shapes=[pltpu.VMEM((T, hd), F32)] * 3,
        compiler_params=_cp("parallel"),
    )(proj, proj, cw, cb, wa, ba, wx, bx, lam, hs, dhl)


def pool_bwd(proj, dpm, pw, pb, ps, l):
    T = proj.shape[0]
    _, G, gd, _ = pw.shape
    P = G * gd

    def body(u_ref, d_ref, w_ref, b_ref, s_ref, du_ref, dw_ref, db_ref, dsc_ref):
        for gi in range(G):
            cols = slice(gi * gd, (gi + 1) * gd)
            w = POOL_WINDOWS[gi]
            inv = _inv_count(T, w)
            ug = u_ref[:, cols].astype(F32)
            pooled = _pooled(ug, w, inv).astype(BF16)
            wb = w_ref[0, gi].astype(BF16)
            mixed = _dot(pooled, wb) + b_ref[0, :, cols]
            dpm_g = d_ref[:, cols].astype(F32)
            dsc_ref[0, :, cols] = jnp.sum(dpm_g * mixed, axis=0, keepdims=True)
            dmixed = dpm_g * s_ref[0, :, cols]
            db_ref[0, :, cols] = jnp.sum(dmixed, axis=0, keepdims=True)
            dmb = dmixed.astype(BF16)
            dw_ref[0, gi] = _dot_tn(pooled, dmb)
            dpooled = _dot_nt(dmb, wb)
            s = dpooled * inv
            k = 1
            while k < w:
                s = s + _shift_up(s, k)
                k *= 2
            du_ref[:, cols] = (s - dpooled).astype(BF16)

    vec = pl.BlockSpec((1, 1, P), lambda i: (l, 0, 0))
    ovec = pl.BlockSpec((1, 1, P), lambda i: (0, 0, 0))
    return _pallas_call(
        body, name="pool_bwd", grid=(1,),
        in_specs=[pl.BlockSpec((T, P), lambda i: (0, 0)), pl.BlockSpec((T, P), lambda i: (0, 0)),
                  pl.BlockSpec((1, G, gd, gd), lambda i: (l, 0, 0, 0)), vec, vec],
        out_specs=[pl.BlockSpec((T, P), lambda i: (0, 0)), pl.BlockSpec((1, G, gd, gd), lambda i: (0, 0, 0, 0)), ovec, ovec],
        out_shape=[jax.ShapeDtypeStruct((T, P), BF16), jax.ShapeDtypeStruct((1, G, gd, gd), F32),
                   jax.ShapeDtypeStruct((1, 1, P), F32), jax.ShapeDtypeStruct((1, 1, P), F32)],
        compiler_params=_cp("arbitrary"),
    )(proj, dpm, pw, pb, ps)


def _place():
    x, y, c = lax.axis_index("x"), lax.axis_index("y"), lax.axis_index("c")
    return x, y, c


def all_gather(name, shards):
    n = len(shards)

    def body(*refs):
        src, out = refs[:n], refs[n:2 * n]
        send_sems, recv_sems, local_sems = refs[2 * n:]
        x, y, c = _place()
        sibling = (x, y, 1 - c)
        chips = [(x, 1 - y), (1 - x, y), (1 - x, 1 - y)]

        def slot(a, px, py, pc):
            return out[a].at[:, 4 * px + 2 * py + pc]

        def copy(a, k, block, to, from_src=False):
            return pltpu.make_async_remote_copy(
                src_ref=src[a] if from_src else slot(a, *block), dst_ref=slot(a, *block),
                send_sem=send_sems.at[a, k], recv_sem=recv_sems.at[a, k], device_id=to, device_id_type=MESH)

        me = (x, y, c)
        mine = [pltpu.make_async_copy(src[a], slot(a, *me), local_sems.at[a]) for a in range(n)]
        first = []
        for j, chip in enumerate(chips):
            for a in range(n):
                first.append(copy(a, 1 + j, me, (*chip, c), from_src=True))
        for a in range(n):
            first.append(copy(a, 0, me, sibling, from_src=True))
        for cp in mine + first:
            cp.start()
        passed = []
        for j, chip in enumerate(chips):
            for a in range(n):
                copy(a, 1 + j, (*chip, c), me).wait_recv()
                fwd = copy(a, 4 + j, (*chip, c), sibling)
                fwd.start()
                passed.append(fwd)
        for a in range(n):
            copy(a, 0, (x, y, 1 - c), me).wait_recv()
        for j, chip in enumerate(chips):
            for a in range(n):
                copy(a, 4 + j, (*chip, 1 - c), me).wait_recv()
        for cp in first + passed:
            cp.wait_send()
        for cp in mine:
            cp.wait()

    outs = _pallas_call(
        body, name=name,
        in_specs=[ANY] * n, out_specs=[ANY] * n,
        out_shape=[jax.ShapeDtypeStruct((s.shape[0], N_DEV) + s.shape[1:], s.dtype) for s in shards],
        scratch_shapes=[pltpu.SemaphoreType.DMA((n, 7)), pltpu.SemaphoreType.DMA((n, 7)), pltpu.SemaphoreType.DMA((n,))],
        compiler_params=pltpu.CompilerParams(has_side_effects=True),
    )(*shards)
    return list(outs)


def pair_exchange(name, g16, layer=None):
    n = len(g16)
    layers = slice(None) if layer is None else pl.ds(layer, 1)

    def body(*refs):
        s16, recv = refs[:n], refs[n:2 * n]
        send_sems, recv_sems = refs[2 * n:]
        x, y, c = _place()
        sibling = (x, y, 1 - c)
        rem = []
        for a in range(n):
            for j in range(N_CHIP):
                rem.append(pltpu.make_async_remote_copy(
                    src_ref=s16[a].at[layers, 2 * j + 1 - c], dst_ref=recv[a].at[:, j],
                    send_sem=send_sems.at[a, j], recv_sem=recv_sems.at[a, j], device_id=sibling, device_id_type=MESH))
        for cp in rem:
            cp.start()
        for cp in rem:
            cp.wait_recv()
        for cp in rem:
            cp.wait_send()

    outs = _pallas_call(
        body, name=name,
        in_specs=[ANY] * n, out_specs=[ANY] * n,
        out_shape=[jax.ShapeDtypeStruct((s.shape[0] if layer is None else 1, N_CHIP) + s.shape[2:], s.dtype) for s in g16],
        scratch_shapes=[pltpu.SemaphoreType.DMA((n, N_CHIP))] * 2,
        compiler_params=pltpu.CompilerParams(has_side_effects=True),
    )(*g16)
    return list(outs)


def chip_exchange(name, pair16):
    n = len(pair16)

    def body(*refs):
        p16, recv2 = refs[:n], refs[n:2 * n]
        send_sems, recv_sems = refs[2 * n:]
        x, y, c = _place()
        rem = []
        for d in (1, 2, 3):
            px = 1 - x if d & 2 else x
            py = 1 - y if d & 1 else y
            for a in range(n):
                rem.append(pltpu.make_async_remote_copy(
                    src_ref=p16[a].at[:, 2 * px + py], dst_ref=recv2[a].at[:, d - 1],
                    send_sem=send_sems.at[a, d - 1], recv_sem=recv_sems.at[a, d - 1], device_id=(px, py, c), device_id_type=MESH))
        for cp in rem:
            cp.start()
        for cp in rem:
            cp.wait_recv()
        for cp in rem:
            cp.wait_send()

    outs = _pallas_call(
        body, name=name,
        in_specs=[ANY] * n, out_specs=[ANY] * n,
        out_shape=[jax.ShapeDtypeStruct((s.shape[0], 3) + s.shape[2:], s.dtype) for s in pair16],
        scratch_shapes=[pltpu.SemaphoreType.DMA((n, 3))] * 2,
        compiler_params=pltpu.CompilerParams(has_side_effects=True),
    )(*pair16)
    return list(outs)


HBM = pl.BlockSpec(memory_space=pltpu.HBM)
SEM = pl.BlockSpec(memory_space=pltpu.SEMAPHORE)
EFFECT = pltpu.SideEffectType.DATAFLOW_SIDE_EFFECTING


def _in_hbm(a):
    return pltpu.with_memory_space_constraint(a, pltpu.HBM)


def split_start(name, bufs, n_copies, copies_of):
    nb = len(bufs)

    def body(*refs):
        buf = refs[:nb]
        send_sems, recv_sems = refs[nb], refs[nb + 1]
        token = refs[-1]
        for i, (src, dst, dev) in enumerate(copies_of(buf)):
            pltpu.make_async_remote_copy(src_ref=src, dst_ref=dst, send_sem=send_sems.at[i], recv_sem=recv_sems.at[i],
                                         device_id=dev, device_id_type=MESH).start()
        token[...] = jnp.zeros_like(token)

    outs = _pallas_call(
        body, name=name,
        in_specs=[HBM] * nb,
        out_specs=(SEM, SEM, *([HBM] * nb), pl.BlockSpec(memory_space=pltpu.VMEM)),
        out_shape=(pltpu.SemaphoreType.DMA((n_copies,)), pltpu.SemaphoreType.DMA((n_copies,)),
                   *[pltpu.HBM(b.shape, b.dtype) for b in bufs], jax.ShapeDtypeStruct((8, 128), F32)),
        input_output_aliases={i: 2 + i for i in range(nb)},
        compiler_params=pltpu.CompilerParams(has_side_effects=EFFECT),
    )(*[_in_hbm(b) for b in bufs])
    return outs[0], outs[1], list(outs[2:2 + nb]), outs[-1]


def split_wait(name, bufs, send_sems, recv_sems, after, copies_of):
    nb = len(bufs)

    def body(*refs):
        buf = refs[:nb]
        send, recv = refs[nb], refs[nb + 1]
        for i, (src, dst, dev) in enumerate(copies_of(buf)):
            cp = pltpu.make_async_remote_copy(src_ref=src, dst_ref=dst, send_sem=send.at[i], recv_sem=recv.at[i],
                                              device_id=dev, device_id_type=MESH)
            cp.wait_send()
            cp.wait_recv()

    outs = _pallas_call(
        body, name=name,
        in_specs=[HBM] * nb + [SEM, SEM] + [ANY] * len(after),
        out_specs=[HBM] * nb,
        out_shape=[pltpu.HBM(b.shape, b.dtype) for b in bufs],
        input_output_aliases={i: i for i in range(nb)},
        compiler_params=pltpu.CompilerParams(has_side_effects=EFFECT),
    )(*bufs, send_sems, recv_sems, *after)
    return list(outs)


def place_own(w, l, place, dtype):
    _, rows, cols = w.shape
    tr = _rows_tile(rows, cols, 1 << 19)

    def body(p_ref, w_ref, o_ref):
        o_ref[0] = w_ref[0].astype(dtype)

    return _pallas_call(
        body, name="place_own",
        grid_spec=pltpu.PrefetchScalarGridSpec(
            num_scalar_prefetch=1, grid=(rows // tr,),
            in_specs=[pl.BlockSpec((1, tr, cols), lambda i, p: (l, i, 0))],
            out_specs=pl.BlockSpec((1, tr, cols), lambda i, p: (p[2], i, 0))),
        out_shape=jax.ShapeDtypeStruct((N_DEV, rows, cols), dtype), compiler_params=_cp("parallel"),
    )(place, w)


def _gather_copies(land):
    x, y, c = _place()
    k = 4 * x + 2 * y + c
    peers = [(x, 1 - y, c), (1 - x, y, c), (1 - x, 1 - y, c), (x, y, 1 - c)]
    return [(b.at[k], b.at[k], p) for p in peers for b in land]


def gather_start(name, land):
    return split_start(name, land, 4 * len(land), _gather_copies)


def gather_wait(name, land, send_sems, recv_sems, after):
    return split_wait(name, land, send_sems, recv_sems, after, _gather_copies)


def gather_forward(name, land):
    n = len(land)

    def body(*refs):
        buf = refs[:n]
        send_sems, recv_sems = refs[2 * n:]
        x, y, c = _place()
        rem = []
        for j, (px, py) in enumerate([(x, 1 - y), (1 - x, y), (1 - x, 1 - y)]):
            k = 4 * px + 2 * py + c
            for a in range(n):
                rem.append(pltpu.make_async_remote_copy(
                    src_ref=buf[a].at[k], dst_ref=buf[a].at[k], send_sem=send_sems.at[a, j], recv_sem=recv_sems.at[a, j],
                    device_id=(x, y, 1 - c), device_id_type=MESH))
        for cp in rem:
            cp.start()
        for cp in rem:
            cp.wait_recv()
        for cp in rem:
            cp.wait_send()

    outs = _pallas_call(
        body, name=name,
        in_specs=[ANY] * n, out_specs=[ANY] * n,
        out_shape=[jax.ShapeDtypeStruct(b.shape, b.dtype) for b in land],
        scratch_shapes=[pltpu.SemaphoreType.DMA((n, 3))] * 2,
        input_output_aliases={i: i for i in range(n)},
        compiler_params=pltpu.CompilerParams(has_side_effects=True),
    )(*land)
    return list(outs)


def _chip_copies(nsrc):
    def copies(buf):
        p16, recv2 = buf[:nsrc], buf[nsrc:]
        x, y, c = _place()
        out = []
        for d in (1, 2, 3):
            px = 1 - x if d & 2 else x
            py = 1 - y if d & 1 else y
            out += [(p16[a].at[:, 2 * px + py], recv2[a].at[:, d - 1], (px, py, c)) for a in range(nsrc)]
        return out
    return copies


def chip_exchange_start(name, pair16):
    n = len(pair16)
    land = [lax.empty((s.shape[0], 3) + s.shape[2:], s.dtype) for s in pair16]
    return split_start(name, list(pair16) + land, 3 * n, _chip_copies(n))


def chip_exchange_wait(name, bufs, send_sems, recv_sems, after):
    n = len(bufs) // 2
    return split_wait(name, bufs, send_sems, recv_sems, after, _chip_copies(n))[n:]


def _rows_tile(rows, cols, budget=1 << 20):
    t = rows
    while t % 2 == 0 and t * cols > budget and (t // 2) % 16 == 0:
        t //= 2
    return t


def pair_sum(g32, recv1, place, l):
    _, _, rows, cols = recv1.shape
    tr = _rows_tile(rows, cols)

    def body(p_ref, m_ref, r_ref, o_ref):
        o_ref[...] = (m_ref[...] + r_ref[...].astype(F32)).astype(o_ref.dtype)

    blk = pl.BlockSpec((1, 1, tr, cols), lambda j, i, p: (0, j, i, 0))
    return _pallas_call(
        body, name="pair_sum",
        grid_spec=pltpu.PrefetchScalarGridSpec(
            num_scalar_prefetch=1, grid=(N_CHIP, rows // tr),
            in_specs=[pl.BlockSpec((1, 1, tr, cols), lambda j, i, p: (l, 2 * j + p[0], i, 0)), blk], out_specs=blk),
        out_shape=jax.ShapeDtypeStruct(recv1.shape, recv1.dtype), compiler_params=_cp("parallel", "parallel"),
    )(place, g32, recv1)


def _grad_in_specs(tr, cols, l):
    return ([pl.BlockSpec((1, 1, tr, cols), lambda i, p: (l, p[2], i, 0)), pl.BlockSpec((1, 1, tr, cols), lambda i, p: (0, p[1], i, 0))]
            + [pl.BlockSpec((1, 1, tr, cols), lambda i, p, d=d: (0, d, i, 0)) for d in range(3)])


def _grad_total(o32, o16, r0, r1, r2):
    return (o32[0, 0] + o16[0, 0].astype(F32)) + r0[0, 0].astype(F32) + r1[0, 0].astype(F32) + r2[0, 0].astype(F32)


def grad_sum(g32, recv1, recv2, place):
    _, _, rows, cols = recv1.shape
    tr = _rows_tile(rows, cols)

    def body(p_ref, o32, o16, r0, r1, r2, g_ref):
        g_ref[...] = _grad_total(o32, o16, r0, r1, r2)

    return _pallas_call(
        body, name="grad_sum",
        grid_spec=pltpu.PrefetchScalarGridSpec(
            num_scalar_prefetch=1, grid=(rows // tr,), in_specs=_grad_in_specs(tr, cols, 0),
            out_specs=pl.BlockSpec((tr, cols), lambda i, p: (i, 0))),
        out_shape=jax.ShapeDtypeStruct((rows, cols), F32), compiler_params=_cp("parallel"),
    )(place, g32, recv1, recv2, recv2, recv2)


def _adamw_math(w, g, m, v):
    m = ADAM_B1 * m + (1.0 - ADAM_B1) * g
    v = ADAM_B2 * v + (1.0 - ADAM_B2) * (g * g)
    m_hat = m / (1.0 - ADAM_B1 ** ADAM_STEP)
    v_hat = v / (1.0 - ADAM_B2 ** ADAM_STEP)
    delta = -ADAM_LR * (m_hat / (jnp.sqrt(v_hat) + ADAM_EPS) + ADAM_WD * w)
    return delta, m, v


def grad_sum_adamw(g32, recv1, recv2, w, m, v, place, l, prev):
    L, rows, cols = w.shape
    tr = _rows_tile(rows, cols, 1 << 18)

    def body(p_ref, o32, o16, r0, r1, r2, w_ref, m_ref, v_ref, *rest):
        g_ref, d_ref, nm_ref, nv_ref = rest[-4:]
        g = _grad_total(o32, o16, r0, r1, r2)
        d, nm, nv = _adamw_math(w_ref[0], g, m_ref[0], v_ref[0])
        g_ref[0] = g
        d_ref[0] = d
        nm_ref[0] = nm
        nv_ref[0] = nv

    blk = pl.BlockSpec((1, tr, cols), lambda i, p: (l, i, 0))
    args = [g32, recv1, recv2, recv2, recv2, w, m, v]
    in_specs = _grad_in_specs(tr, cols, l) + [blk] * 3
    aliases = {}
    if prev is not None:
        aliases = {1 + len(args) + k: k for k in range(4)}
        args += list(prev)
        in_specs += [ANY] * 4
    return _pallas_call(
        body, name="grad_sum_adamw",
        grid_spec=pltpu.PrefetchScalarGridSpec(num_scalar_prefetch=1, grid=(rows // tr,), in_specs=in_specs, out_specs=[blk] * 4),
        out_shape=[jax.ShapeDtypeStruct((L, rows, cols), F32)] * 4, input_output_aliases=aliases,
        compiler_params=_cp("parallel"),
    )(place, *args)


def adamw(w, g, m, v):
    rows, cols = w.shape
    tr = _rows_tile(rows, cols, 1 << 18)

    def body(w_ref, g_ref, m_ref, v_ref, d_ref, nm_ref, nv_ref):
        d, nm, nv = _adamw_math(w_ref[...], g_ref[...], m_ref[...], v_ref[...])
        d_ref[...] = d
        nm_ref[...] = nm
        nv_ref[...] = nv

    blk = pl.BlockSpec((tr, cols), lambda i: (i, 0))
    return _pallas_call(body, name="adamw_small", grid=(rows // tr,), in_specs=[blk] * 4, out_specs=[blk] * 3,
                        out_shape=[jax.ShapeDtypeStruct((rows, cols), F32)] * 3, compiler_params=_cp("parallel"))(w, g, m, v)


SMALL = ("norm_ffn1", "norm_mix", "pool_w", "pool_b", "pool_scale", "conv_w", "conv_b", "lru_w_a", "lru_b_a", "lru_w_x", "lru_b_x",
         "lru_lambda", "norm_ffn2", "final_norm")
BIG = ("ffn1_w_up", "ffn1_w_down", "w_in", "w_pool_up", "w_lru_up", "w_out", "ffn2_w_up", "ffn2_w_down")
NAMES = ("norm_ffn1", "ffn1_w_up", "ffn1_w_down", "norm_mix", "w_in", "pool_w", "pool_b", "pool_scale", "w_pool_up", "conv_w", "conv_b",
         "lru_w_a", "lru_b_a", "lru_w_x", "lru_b_x", "lru_lambda", "w_lru_up", "w_out", "norm_ffn2", "ffn2_w_up", "ffn2_w_down", "final_norm")
PACK_ROWS = 16 * N_DEV


def _pack(parts):
    flat = jnp.concatenate([p.reshape(-1) for p in parts])
    unit = 128 * PACK_ROWS
    padded = -(-flat.size // unit) * unit
    return jnp.pad(flat, (0, padded - flat.size)).reshape(-1, 128)


def _unpack(packed, shapes):
    flat = packed.reshape(-1)
    out, off = [], 0
    for s in shapes:
        n = 1
        for d in s:
            n *= d
        out.append(flat[off:off + n].reshape(s))
        off += n
    return out


def kernel(x, norm_ffn1, ffn1_w_up, ffn1_w_down, norm_mix, w_in, pool_w, pool_b, pool_scale, w_pool_up, conv_w, conv_b, lru_w_a, lru_b_a, lru_w_x, lru_b_x, lru_lambda, w_lru_up, w_out, norm_ffn2, ffn2_w_up, ffn2_w_down, final_norm, loss_target, m_norm_ffn1, m_ffn1_w_up, m_ffn1_w_down, m_norm_mix, m_w_in, m_pool_w, m_pool_b, m_pool_scale, m_w_pool_up, m_conv_w, m_conv_b, m_lru_w_a, m_lru_b_a, m_lru_w_x, m_lru_b_x, m_lru_lambda, m_w_lru_up, m_w_out, m_norm_ffn2, m_ffn2_w_up, m_ffn2_w_down, m_final_norm, v_norm_ffn1, v_ffn1_w_up, v_ffn1_w_down, v_norm_mix, v_w_in, v_pool_w, v_pool_b, v_pool_scale, v_w_pool_up, v_conv_w, v_conv_b, v_lru_w_a, v_lru_b_a, v_lru_w_x, v_lru_b_x, v_lru_lambda, v_w_lru_up, v_w_out, v_norm_ffn2, v_ffn2_w_up, v_ffn2_w_down, v_final_norm):
    W = dict(norm_ffn1=norm_ffn1, ffn1_w_up=ffn1_w_up, ffn1_w_down=ffn1_w_down, norm_mix=norm_mix, w_in=w_in, pool_w=pool_w, pool_b=pool_b,
             pool_scale=pool_scale, w_pool_up=w_pool_up, conv_w=conv_w, conv_b=conv_b, lru_w_a=lru_w_a, lru_b_a=lru_b_a, lru_w_x=lru_w_x,
             lru_b_x=lru_b_x, lru_lambda=lru_lambda, w_lru_up=w_lru_up, w_out=w_out, norm_ffn2=norm_ffn2, ffn2_w_up=ffn2_w_up,
             ffn2_w_down=ffn2_w_down, final_norm=final_norm)
    M = dict(norm_ffn1=m_norm_ffn1, ffn1_w_up=m_ffn1_w_up, ffn1_w_down=m_ffn1_w_down, norm_mix=m_norm_mix, w_in=m_w_in, pool_w=m_pool_w,
             pool_b=m_pool_b, pool_scale=m_pool_scale, w_pool_up=m_w_pool_up, conv_w=m_conv_w, conv_b=m_conv_b, lru_w_a=m_lru_w_a,
             lru_b_a=m_lru_b_a, lru_w_x=m_lru_w_x, lru_b_x=m_lru_b_x, lru_lambda=m_lru_lambda, w_lru_up=m_w_lru_up, w_out=m_w_out,
             norm_ffn2=m_norm_ffn2, ffn2_w_up=m_ffn2_w_up, ffn2_w_down=m_ffn2_w_down, final_norm=m_final_norm)
    V = dict(norm_ffn1=v_norm_ffn1, ffn1_w_up=v_ffn1_w_up, ffn1_w_down=v_ffn1_w_down, norm_mix=v_norm_mix, w_in=v_w_in, pool_w=v_pool_w,
             pool_b=v_pool_b, pool_scale=v_pool_scale, w_pool_up=v_w_pool_up, conv_w=v_conv_w, conv_b=v_conv_b, lru_w_a=v_lru_w_a,
             lru_b_a=v_lru_b_a, lru_w_x=v_lru_w_x, lru_b_x=v_lru_b_x, lru_lambda=v_lru_lambda, w_lru_up=v_w_lru_up, w_out=v_w_out,
             norm_ffn2=v_norm_ffn2, ffn2_w_up=v_ffn2_w_up, ffn2_w_down=v_ffn2_w_down, final_norm=v_final_norm)

    T, D = x.shape[1], x.shape[2]
    L = norm_ffn1.shape[0]
    P = pool_scale.shape[1]
    R = lru_lambda.shape[1]
    H, hd = lru_w_a.shape[1], lru_w_a.shape[2]
    CW = conv_w.shape[1]
    cs = ffn1_w_up.shape[2]
    ci = w_in.shape[2]
    xin = x.reshape(T, D)
    tgt = loss_target.reshape(T, D)
    dev = 4 * lax.axis_index("x") + 2 * lax.axis_index("y") + lax.axis_index("c")
    place = jnp.stack([lax.axis_index("c"), 2 * lax.axis_index("x") + lax.axis_index("y"), dev]).astype(jnp.int32)

    cw_flat = conv_w.reshape(L, -1)
    cw_pad = (-cw_flat.shape[1]) % 1024
    cw_tiles = jnp.pad(cw_flat, ((0, 0), (0, cw_pad))).reshape(L, -1, 128)

    def gather_layer_start(l):
        land = [place_own(W[n], l, place, BF16) for n in BIG] + [place_own(cw_tiles, l, place, F32)]
        return gather_start(f"gather_start_l{l}", land)

    def layer_weights(land):
        g = dict(zip(BIG + ("conv_w",), land))
        one = lambda a: a.reshape((1,) + a.shape)
        cw_l = g["conv_w"].reshape(N_DEV, -1)[:, :cw_flat.shape[1]].reshape((N_DEV,) + conv_w.shape[1:])
        return dict(wup1=one(g["ffn1_w_up"]), wup2=one(g["ffn2_w_up"]), win=one(g["w_in"]),
                    wd1=g["ffn1_w_down"].reshape(1, -1, D), wd2=g["ffn2_w_down"].reshape(1, -1, D),
                    wlu=g["w_lru_up"].reshape(1, R, D), wout=g["w_out"].reshape(1, D, D),
                    wpu=g["w_pool_up"].transpose(1, 0, 2).reshape(1, P, D),
                    cw=cw_l.transpose(1, 0, 2).reshape(1, CW, R))

    def layer_params(l):
        vec = lambda a: a[l:l + 1].reshape(1, 1, -1)
        return dict(g1=vec(norm_ffn1), gm=vec(norm_mix), g2=vec(norm_ffn2), pb=vec(pool_b), ps=vec(pool_scale), cb=vec(conv_b),
                    ba=vec(lru_b_a), bx=vec(lru_b_x), lam=vec(lru_lambda), pw=pool_w[l:l + 1], wa=lru_w_a[l:l + 1], wx=lru_w_x[l:l + 1])

    AHEAD = 2
    started = {l: gather_layer_start(l) for l in range(min(AHEAD, L))}
    saved, LW, LP = [], [], []
    xc = xin
    for l in range(L):
        send_sems, recv_sems, land, _tok = started.pop(l)
        after = [xc] + [s[3] for s in started.values()]
        land = gather_wait(f"gather_wait_l{l}", land, send_sems, recv_sems, after)
        land = gather_forward(f"gather_forward_l{l}", land)
        if l + AHEAD < L:
            started[l + AHEAD] = gather_layer_start(l + AHEAD)
        w, p = layer_weights(land), layer_params(l)
        LW.append(w)
        LP.append(p)
        sv = {"x1": xc}
        sv["h1"], sv["u1"], sv["s1"] = ffn_up(xc, p["g1"], w["wup1"], 0)
        xc = ffn_down(sv["s1"], w["wd1"], xc, 0)
        sv["x2"] = xc
        sv["h2"], sv["proj"] = mix_in(xc, p["gm"], w["win"], 0)
        sv["pm"] = pool_fwd(sv["proj"], p["pw"], p["pb"], p["ps"], 0)
        sv["hl"], sv["hs"] = lru_fwd(sv["proj"], w["cw"], p["cb"], p["wa"], p["ba"], p["wx"], p["bx"], p["lam"], P, 0)
        xc, sv["yp"], sv["yl"], sv["z"] = mix_out(sv["pm"], sv["hl"], sv["proj"], xc, w["wpu"], w["wlu"], w["wout"], P, 0)
        sv["x3"] = xc
        sv["h3"], sv["u3"], sv["s3"] = ffn_up(xc, p["g2"], w["wup2"], 0)
        xc = ffn_down(sv["s3"], w["wd2"], xc, 0)
        saved.append(sv)

    loss_part, dx, d_final = loss_head(xc, final_norm.reshape(1, D), tgt)
    loss = lax.psum(loss_part[0, 0], ("x", "y", "c"))

    G = {n: None for n in BIG}
    small = {n: [None] * L for n in SMALL if n != "final_norm"}

    def to_slots(name, a):
        if name == "w_pool_up":
            return a.reshape(L, P, N_DEV, D // N_DEV).transpose(0, 2, 1, 3)
        return a.reshape((L, N_DEV) + W[name].shape[1:])

    def ffn_bwd(dy, sv, tag, wup, wd, gn, up_name, dn_name, norm_name, l, deps=()):
        dout, du = ffn_down_bwd(dy, wd, sv["u" + tag], 0, deps)
        du = du.reshape(N_DEV, T, cs)
        G[dn_name] = dw_tn("dw_down", [sv["s" + tag]], lambda tk: pl.BlockSpec((1, tk, cs), lambda g, k: (g, k, 0)),
                           [dout], lambda tk: pl.BlockSpec((tk, D), lambda g, k: (k, 0)), 4, cs, D, T, l, L, G[dn_name])
        G[up_name] = dw_tn("dw_up", [sv["h" + tag]], lambda tk: pl.BlockSpec((tk, D), lambda g, k: (k, 0)),
                           [du], lambda tk: pl.BlockSpec((1, tk, cs), lambda g, k: (g, k, 0)), N_DEV, D, cs, T, l, L, G[up_name])
        dxn, dg = dx_norm_bwd("ffn_dx", du, lambda tm: pl.BlockSpec((1, tm, cs), lambda i, j: (j, i, 0)), wup, N_DEV,
                              sv["x" + tag], gn, dy, 0)
        small[norm_name][l] = dg.reshape(D)
        return dxn

    in_flight = []
    deps = ()
    for l in reversed(range(L)):
        sv, w, p = saved[l], LW[l], LP[l]
        dx = ffn_bwd(dx, sv, "3", w["wup2"], w["wd2"], p["g2"], "ffn2_w_up", "ffn2_w_down", "norm_ffn2", l, deps)
        dyb, dyp, dyl, dgp, dgl, dpm, dhl = mix_out_bwd(dx, sv["proj"], sv["yp"], sv["yl"], w["wpu"], w["wlu"], w["wout"], P, R, 0)
        row = lambda wd_: (lambda tk: pl.BlockSpec((tk, wd_), lambda g, k: (k, 0)))
        G["w_out"] = dw_tn("dw_out", [sv["z"]], row(D), [dyb], row(D), 1, D, D, T, l, L, G["w_out"])
        G["w_lru_up"] = dw_tn("dw_lru_up", [sv["hl"]], row(R), [dyl], row(D), 1, R, D, T, l, L, G["w_lru_up"])
        G["w_pool_up"] = dw_tn("dw_pool_up", [sv["pm"]], row(P), [dyp], row(D), 1, P, D, T, l, L, G["w_pool_up"])
        du_lru, du_gelu, dcw, dcb, dwa, dba, dwx, dbx, dlam = lru_bwd(
            sv["proj"], sv["hs"], dhl, w["cw"], p["cb"], p["wa"], p["ba"], p["wx"], p["bx"], p["lam"], P, 0)
        du_pool, dpw, dpb, dpsc = pool_bwd(sv["proj"], dpm, p["pw"], p["pb"], p["ps"], 0)
        dproj = jnp.concatenate([du_pool, du_lru, du_gelu, dgp, dgl], axis=1)
        G["w_in"] = dw_tn("dw_in", [sv["h2"]], row(D), [dproj], lambda tk: pl.BlockSpec((tk, ci), lambda g, k: (k, g)),
                          N_DEV, D, ci, T, l, L, G["w_in"])
        dx, dgm = dx_norm_bwd("mix_dx", dproj, lambda tm: pl.BlockSpec((tm, ci), lambda i, j: (i, j)), w["win"], N_DEV,
                              sv["x2"], p["gm"], dx, 0)
        small["norm_mix"][l] = dgm.reshape(D)
        small["pool_w"][l], small["pool_b"][l], small["pool_scale"][l] = dpw[0], dpb.reshape(pool_b.shape[1:]), dpsc.reshape(P)
        small["conv_w"][l], small["conv_b"][l] = dcw[0], dcb.reshape(R)
        small["lru_w_a"][l], small["lru_b_a"][l] = dwa[0], dba.reshape(H, hd)
        small["lru_w_x"][l], small["lru_b_x"][l] = dwx[0], dbx.reshape(H, hd)
        small["lru_lambda"][l] = dlam.reshape(R)
        dx = ffn_bwd(dx, sv, "1", w["wup1"], w["wd1"], p["g1"], "ffn1_w_up", "ffn1_w_down", "norm_ffn1", l)
        recv1_l = pair_exchange(f"rs_pair_exchange_l{l}", [to_slots(n, G[n][1]) for n in BIG], layer=l)
        pair16_l = [pair_sum(to_slots(n, G[n][0]), r_, place, l) for n, r_ in zip(BIG, recv1_l)]
        send_sems, recv_sems, bufs, tok = chip_exchange_start(f"rs_chip_start_l{l}", pair16_l)
        in_flight.append((l, send_sems, recv_sems, bufs, recv1_l))
        deps = (tok,)

    grad_x = dx.reshape(x.shape)

    small_parts = [jnp.stack(small[n]) for n in SMALL if n != "final_norm"] + [d_final.reshape(D)]
    small_shapes = [p.shape for p in small_parts]
    gpack = _pack(small_parts).reshape(1, N_DEV, -1, 128)
    recv1_s = pair_exchange("rs_pair_exchange_small", [gpack])[0]
    pair_s = pair_sum(gpack, recv1_s, place, 0)
    recv2_s = chip_exchange("rs_chip_exchange_small", [pair_s])[0]

    g32 = [to_slots(n, G[n][0]) for n in BIG]
    outs = {n: None for n in BIG}
    after = [dx, recv2_s]
    for l, send_sems, recv_sems, bufs, recv1_l in in_flight:
        recv2_l = chip_exchange_wait(f"rs_chip_wait_l{l}", bufs, send_sems, recv_sems, after)
        for i, n in enumerate(BIG):
            outs[n] = grad_sum_adamw(g32[i], recv1_l[i], recv2_l[i], W[n], M[n], V[n], place, l, outs[n])
        after = [outs[BIG[-1]][0]]
    out_g, out_d, out_m, out_v = ({n: outs[n][k] for n in BIG} for k in range(4))

    gs = grad_sum(gpack, recv1_s, recv2_s, place)
    gs_all = all_gather("all_gather_small_grads", [gs.reshape((1,) + gs.shape)])[0]
    gs_all = gs_all.reshape(-1, 128)
    small_g = dict(zip(SMALL, _unpack(gs_all, small_shapes)))
    full_shapes = [W[n].shape if n != "conv_w" else small_shapes[SMALL.index("conv_w")] for n in SMALL]
    rep = [n for n in SMALL if n != "conv_w"]
    rep_shapes = [W[n].shape for n in rep]
    wp, mp, vp = (_pack([S[n] for n in rep]) for S in (W, M, V))
    gp = _pack([small_g[n] for n in rep])
    dp, nmp, nvp = adamw(wp, gp, mp, vp)
    for S, packed in ((out_d, dp), (out_m, nmp), (out_v, nvp)):
        S.update(zip(rep, _unpack(packed, rep_shapes)))
    for n in rep:
        out_g[n] = small_g[n]
    cwc = conv_w.shape[2]
    gcw = lax.dynamic_slice_in_dim(small_g["conv_w"], dev * cwc, cwc, axis=2)
    cw2 = lambda a: a.reshape(-1, cwc)
    pad_rows = (-cw2(conv_w).shape[0]) % 8
    padr = lambda a: jnp.pad(cw2(a), ((0, pad_rows), (0, 0)))
    dcw_, mcw_, vcw_ = adamw(padr(conv_w), padr(gcw), padr(M["conv_w"]), padr(V["conv_w"]))
    nrow = cw2(conv_w).shape[0]
    out_g["conv_w"] = gcw
    out_d["conv_w"], out_m["conv_w"], out_v["conv_w"] = (a[:nrow].reshape(conv_w.shape) for a in (dcw_, mcw_, vcw_))
    del full_shapes

    return (loss, grad_x, *[out_g[n] for n in NAMES], *[out_d[n] for n in NAMES], *[out_m[n] for n in NAMES], *[out_v[n] for n in NAMES])
```

```python
import functools

import jax
import jax.numpy as jnp
from jax import lax
from jax.experimental import pallas as pl
from jax.experimental.pallas import tpu as pltpu

F32, BF16 = jnp.float32, jnp.bfloat16
EPS = 1e-6
LRU_C = 8.0
POOL_WINDOWS = (2, 4, 8, 16)
ADAM_LR, ADAM_B1, ADAM_B2, ADAM_EPS, ADAM_WD, ADAM_STEP = 0.001, 0.9, 0.999, 1e-08, 0.01, 10
N_DEV = 8
N_CHIP = 4
MESH = pl.DeviceIdType.MESH
V7X_VMEM_LIMIT = 56 * 1024 * 1024
ROW_TILE = 512
WIDE_TILE = 1024
SUM_TILE = 2048
ANY = pl.BlockSpec(memory_space=pl.ANY)

_pallas_call = pl.pallas_call


def _cp(*sem):
    return pltpu.CompilerParams(dimension_semantics=sem if sem else None, vmem_limit_bytes=V7X_VMEM_LIMIT)


def _tile(n, t):
    t = min(n, t)
    assert n % t == 0, (n, t)
    return t


def _dot(a, b):
    return jnp.dot(a, b, preferred_element_type=F32)


def _dot_nt(a, b):
    return lax.dot_general(a, b, (((1,), (1,)), ((), ())), preferred_element_type=F32)


def _dot_tn(a, b):
    return lax.dot_general(a, b, (((0,), (0,)), ((), ())), preferred_element_type=F32)


def _rms(xv):
    r = lax.rsqrt(jnp.mean(xv * xv, axis=-1, keepdims=True) + EPS)
    return xv * r, r


def _rms_bwd(dh, xv, gv, dy):
    n, r = _rms(xv)
    dn = dh * gv
    dx = dy + r * (dn - n * jnp.mean(dn * n, axis=-1, keepdims=True))
    return dx, jnp.sum(dh * n, axis=0, keepdims=True)


def _shift_down(x, k, fill=0.0):
    if k == 0:
        return x
    rows = lax.broadcasted_iota(jnp.int32, x.shape, 0)
    return jnp.where(rows >= k, pltpu.roll(x, k, 0), fill)


def _shift_up(x, k, fill=0.0):
    if k == 0:
        return x
    n = x.shape[0]
    rows = lax.broadcasted_iota(jnp.int32, x.shape, 0)
    return jnp.where(rows < n - k, pltpu.roll(x, n - k, 0), fill)


def _sigmoid(x):
    return 1.0 / (1.0 + jnp.exp(-x))


_GELU_K = 0.7978845608028654
_GELU_C = 0.044715


def _gelu(x):
    th = jnp.tanh(_GELU_K * (x + _GELU_C * x * x * x))
    return 0.5 * x * (1.0 + th), th


def _gelu_grad(x, th):
    return 0.5 * (1.0 + th) + 0.5 * x * (1.0 - th * th) * _GELU_K * (1.0 + 3.0 * _GELU_C * x * x)


def ffn_up(x, g, wup, l):
    T, D = x.shape
    cs = wup.shape[-1]
    tm = _tile(T, WIDE_TILE)
    ni = T // tm

    def body(x_ref, g_ref, wa_ref, wb_ref, h_ref, u_ref, s_ref, hs_ref):
        rows = pl.ds(pl.multiple_of(pl.program_id(1) * tm, tm), tm)

        @pl.when(pl.program_id(0) == 0)
        def _():
            n, _r = _rms(x_ref[...])
            hv = (n * g_ref[0]).astype(BF16)
            hs_ref[rows, :] = hv
            h_ref[...] = hv

        hv = hs_ref[rows, :]
        a = _dot(hv, wa_ref[0, 0])
        b = _dot(hv, wb_ref[0, 0])
        u_ref[0, 0] = a.astype(BF16)
        u_ref[1, 0] = b.astype(BF16)
        s_ref[0] = (a * _sigmoid(a) * b).astype(BF16)

    first = lambda j, i: (jnp.where(j == 0, i, ni - 1), 0)
    return _pallas_call(
        body, name="ffn_up", grid=(4, ni),
        in_specs=[pl.BlockSpec((tm, D), first), pl.BlockSpec((1, 1, D), lambda j, i: (l, 0, 0)),
                  pl.BlockSpec((1, 1, D, cs), lambda j, i: (l, j, 0, 0)), pl.BlockSpec((1, 1, D, cs), lambda j, i: (l, j + 4, 0, 0))],
        out_specs=[pl.BlockSpec((tm, D), first), pl.BlockSpec((2, 1, tm, cs), lambda j, i: (0, j, i, 0)),
                   pl.BlockSpec((1, tm, cs), lambda j, i: (j, i, 0))],
        out_shape=[jax.ShapeDtypeStruct((T, D), BF16), jax.ShapeDtypeStruct((2, 4, T, cs), BF16), jax.ShapeDtypeStruct((4, T, cs), BF16)],
        scratch_shapes=[pltpu.VMEM((T, D), BF16)],
        compiler_params=_cp("arbitrary", "arbitrary"),
    )(x, g, wup, wup)


def ffn_down(s, wd, x, l):
    _, T, cs = s.shape
    D = x.shape[1]
    tm = _tile(T, WIDE_TILE)

    def body(s_ref, w_ref, x_ref, o_ref, acc_ref):
        j = pl.program_id(1)

        @pl.when(j == 0)
        def _():
            acc_ref[...] = jnp.zeros_like(acc_ref)

        acc_ref[...] += _dot(s_ref[0], w_ref[0])

        @pl.when(j == 3)
        def _():
            o_ref[...] = x_ref[...] + 0.5 * acc_ref[...]

    return _pallas_call(
        body, name="ffn_down", grid=(T // tm, 4),
        in_specs=[pl.BlockSpec((1, tm, cs), lambda i, j: (j, i, 0)), pl.BlockSpec((1, cs, D), lambda i, j: (l, j, 0)),
                  pl.BlockSpec((tm, D), lambda i, j: (i, 0))],
        out_specs=pl.BlockSpec((tm, D), lambda i, j: (i, 0)),
        out_shape=jax.ShapeDtypeStruct((T, D), F32),
        scratch_shapes=[pltpu.VMEM((tm, D), F32)],
        compiler_params=_cp("parallel", "arbitrary"),
    )(s, wd, x)


def mix_in(x, g, win, l):
    T, D = x.shape
    ci = win.shape[-1]
    tm = _tile(T, WIDE_TILE)
    ni = T // tm

    def body(x_ref, g_ref, w_ref, h_ref, p_ref, hs_ref):
        rows = pl.ds(pl.multiple_of(pl.program_id(1) * tm, tm), tm)

        @pl.when(pl.program_id(0) == 0)
        def _():
            n, _r = _rms(x_ref[...])
            hv = (n * g_ref[0]).astype(BF16)
            hs_ref[rows, :] = hv
            h_ref[...] = hv

        p_ref[...] = _dot(hs_ref[rows, :], w_ref[0, 0]).astype(BF16)

    first = lambda j, i: (jnp.where(j == 0, i, ni - 1), 0)
    return _pallas_call(
        body, name="mix_in", grid=(N_DEV, ni),
        in_specs=[pl.BlockSpec((tm, D), first), pl.BlockSpec((1, 1, D), lambda j, i: (l, 0, 0)),
                  pl.BlockSpec((1, 1, D, ci), lambda j, i: (l, j, 0, 0))],
        out_specs=[pl.BlockSpec((tm, D), first), pl.BlockSpec((tm, ci), lambda j, i: (i, j))],
        out_shape=[jax.ShapeDtypeStruct((T, D), BF16), jax.ShapeDtypeStruct((T, N_DEV * ci), BF16)],
        scratch_shapes=[pltpu.VMEM((T, D), BF16)],
        compiler_params=_cp("arbitrary", "arbitrary"),
    )(x, g, win)


def _inv_count(T, w):
    t = lax.broadcasted_iota(jnp.int32, (T, 1), 0)
    return 1.0 / jnp.minimum(t + 1, w).astype(F32)


def _pooled(ug, w, inv):
    s = ug
    k = 1
    while k < w:
        s = s + _shift_down(s, k)
        k *= 2
    return s * inv - ug


def pool_fwd(proj, pw, pb, ps, l):
    T = proj.shape[0]
    _, G, gd, _ = pw.shape
    P = G * gd

    def body(u_ref, w_ref, b_ref, s_ref, o_ref):
        for gi in range(G):
            cols = slice(gi * gd, (gi + 1) * gd)
            ug = u_ref[:, cols].astype(F32)
            pooled = _pooled(ug, POOL_WINDOWS[gi], _inv_count(T, POOL_WINDOWS[gi]))
            mixed = _dot(pooled.astype(BF16), w_ref[0, gi].astype(BF16)) + b_ref[0, :, cols]
            o_ref[:, cols] = (mixed * s_ref[0, :, cols]).astype(BF16)

    return _pallas_call(
        body, name="pool_fwd", grid=(1,),
        in_specs=[pl.BlockSpec((T, P), lambda i: (0, 0)), pl.BlockSpec((1, G, gd, gd), lambda i: (l, 0, 0, 0)),
                  pl.BlockSpec((1, 1, P), lambda i: (l, 0, 0)), pl.BlockSpec((1, 1, P), lambda i: (l, 0, 0))],
        out_specs=pl.BlockSpec((T, P), lambda i: (0, 0)),
        out_shape=jax.ShapeDtypeStruct((T, P), BF16),
        compiler_params=_cp("arbitrary"),
    )(proj, pw, pb, ps)


def _conv(u, cw_ref, cb):
    CW = cw_ref.shape[1]
    v = cb
    for k in range(CW):
        v = v + cw_ref[0, k:k + 1, :] * _shift_down(u, CW - 1 - k)
    return v


def _softplus(z):
    return jnp.maximum(z, 0.0) + jnp.log1p(jnp.exp(-jnp.abs(z)))


def _lru_gates(v, wa_ref, ba, wx_ref, bx, lam):
    vb = v.astype(BF16)
    r = _sigmoid(_dot(vb, wa_ref[0, 0].astype(BF16)) + ba)
    i = _sigmoid(_dot(vb, wx_ref[0, 0].astype(BF16)) + bx)
    sp = _softplus(-lam)
    log_a = -LRU_C * r * sp
    a = jnp.exp(log_a)
    mult = jnp.sqrt(-jnp.tanh(log_a) * (a * a + 1.0))
    return r, i, sp, a, mult


def _scan_fwd(a_ref, b_ref, o_ref):
    T, W = a_ref.shape
    rows = lax.broadcasted_iota(jnp.int32, (8, W), 0)

    def step(t, carry):
        r0 = pl.multiple_of(t * 8, 8)
        A = a_ref[pl.ds(r0, 8), :]
        B = b_ref[pl.ds(r0, 8), :]
        for s in (1, 2, 4):
            keep = rows >= s
            As = jnp.where(keep, pltpu.roll(A, s, 0), 1.0)
            Bs = jnp.where(keep, pltpu.roll(B, s, 0), 0.0)
            B = A * Bs + B
            A = A * As
        h = B + A * carry
        o_ref[pl.ds(r0, 8), :] = h
        return jnp.broadcast_to(h[7:8, :], (8, W))

    lax.fori_loop(0, T // 8, step, jnp.zeros((8, W), F32), unroll=8)


def _scan_bwd(a_ref, b_ref, o_ref):
    T, W = a_ref.shape
    rows = lax.broadcasted_iota(jnp.int32, (8, W), 0)
    nt = T // 8

    def step(t, carry):
        r0 = pl.multiple_of((nt - 1 - t) * 8, 8)
        A = a_ref[pl.ds(r0, 8), :]
        B = b_ref[pl.ds(r0, 8), :]
        for s in (1, 2, 4):
            keep = rows < 8 - s
            As = jnp.where(keep, pltpu.roll(A, 8 - s, 0), 1.0)
            Bs = jnp.where(keep, pltpu.roll(B, 8 - s, 0), 0.0)
            B = A * Bs + B
            A = A * As
        y = B + A * carry
        o_ref[pl.ds(r0, 8), :] = y
        return jnp.broadcast_to(y[0:1, :], (8, W))

    lax.fori_loop(0, nt, step, jnp.zeros((8, W), F32), unroll=8)


def _lru_specs(T, hd, P, R, CW, l):
    ob, gb = P // hd, (P + R) // hd
    vec = pl.BlockSpec((1, 1, hd), lambda h: (l, 0, h))
    mat = pl.BlockSpec((1, 1, hd, hd), lambda h: (l, h, 0, 0))
    return [pl.BlockSpec((T, hd), lambda h: (0, ob + h)), pl.BlockSpec((T, hd), lambda h: (0, gb + h)),
            pl.BlockSpec((1, CW, hd), lambda h: (l, 0, h)), vec, mat, vec, mat, vec, vec]


def lru_fwd(proj, cw, cb, wa, ba, wx, bx, lam, P, l):
    T = proj.shape[0]
    _, H, hd, _ = wa.shape
    R = H * hd
    CW = cw.shape[1]
    assert P % hd == 0 and T % 8 == 0

    def body(u_ref, ug_ref, cw_ref, cb_ref, wa_ref, ba_ref, wx_ref, bx_ref, lam_ref, hl_ref, hs_ref, a_s, b_s):
        v = _conv(u_ref[...].astype(F32), cw_ref, cb_ref[0])
        _r, i, _sp, a, mult = _lru_gates(v, wa_ref, ba_ref[0], wx_ref, bx_ref[0], lam_ref[0])
        a_s[...] = a
        b_s[...] = mult * (i * v)
        _scan_fwd(a_s, b_s, hs_ref)
        ge, _th = _gelu(ug_ref[...].astype(F32))
        hl_ref[...] = (hs_ref[...] * ge).astype(BF16)

    out = pl.BlockSpec((T, hd), lambda h: (0, h))
    return _pallas_call(
        body, name="lru_fwd", grid=(H,),
        in_specs=_lru_specs(T, hd, P, R, CW, l),
        out_specs=[out, out],
        out_shape=[jax.ShapeDtypeStruct((T, R), BF16), jax.ShapeDtypeStruct((T, R), F32)],
        scratch_shapes=[pltpu.VMEM((T, hd), F32)] * 2,
        compiler_params=_cp("parallel"),
    )(proj, proj, cw, cb, wa, ba, wx, bx, lam)


def mix_out(pm, hl, proj, x, wpu, wlu, wout, P, l):
    T, D = x.shape
    R = hl.shape[1]
    tm = _tile(T, ROW_TILE)
    assert (P + 2 * R) % D == 0
    gb = (P + 2 * R) // D

    def body(pm_ref, hl_ref, gp_ref, gl_ref, x_ref, wpu_ref, wlu_ref, wo_ref, o_ref, yp_ref, yl_ref, z_ref):
        yp = _dot(pm_ref[...], wpu_ref[0])
        yl = _dot(hl_ref[...], wlu_ref[0])
        z = (_sigmoid(gp_ref[...].astype(F32)) * yp + _sigmoid(gl_ref[...].astype(F32)) * yl).astype(BF16)
        yp_ref[...] = yp.astype(BF16)
        yl_ref[...] = yl.astype(BF16)
        z_ref[...] = z
        o_ref[...] = x_ref[...] + _dot(z, wo_ref[0])

    row = lambda w: pl.BlockSpec((tm, w), lambda i: (i, 0))
    return _pallas_call(
        body, name="mix_out", grid=(T // tm,),
        in_specs=[row(P), row(R), pl.BlockSpec((tm, D), lambda i: (i, gb)), pl.BlockSpec((tm, D), lambda i: (i, gb + 1)), row(D),
                  pl.BlockSpec((1, P, D), lambda i: (l, 0, 0)), pl.BlockSpec((1, R, D), lambda i: (l, 0, 0)),
                  pl.BlockSpec((1, D, D), lambda i: (l, 0, 0))],
        out_specs=[row(D)] * 4,
        out_shape=[jax.ShapeDtypeStruct((T, D), F32)] + [jax.ShapeDtypeStruct((T, D), BF16)] * 3,
        compiler_params=_cp("parallel"),
    )(pm, hl, proj, proj, x, wpu, wlu, wout)


def loss_head(x, gf, tgt):
    T, D = x.shape
    tm = _tile(T, ROW_TILE)

    def body(x_ref, g_ref, t_ref, loss_ref, dx_ref, dg_ref):
        @pl.when(pl.program_id(0) == 0)
        def _():
            loss_ref[...] = jnp.zeros_like(loss_ref)
            dg_ref[...] = jnp.zeros_like(dg_ref)

        xv = x_ref[...]
        gv = g_ref[...]
        n, _r = _rms(xv)
        e = n * gv - t_ref[...]
        loss_ref[...] += 0.5 * jnp.sum(jnp.sum(e * e, axis=-1, keepdims=True), axis=0, keepdims=True) / D
        dx, dg = _rms_bwd(e * (1.0 / D), xv, gv, 0.0)
        dx_ref[...] = dx
        dg_ref[...] += dg

    return _pallas_call(
        body, name="loss_head", grid=(T // tm,),
        in_specs=[pl.BlockSpec((tm, D), lambda i: (i, 0)), pl.BlockSpec((1, D), lambda i: (0, 0)), pl.BlockSpec((tm, D), lambda i: (i, 0))],
        out_specs=[pl.BlockSpec((1, 1), lambda i: (0, 0)), pl.BlockSpec((tm, D), lambda i: (i, 0)), pl.BlockSpec((1, D), lambda i: (0, 0))],
        out_shape=[jax.ShapeDtypeStruct((1, 1), F32), jax.ShapeDtypeStruct((T, D), F32), jax.ShapeDtypeStruct((1, D), F32)],
        compiler_params=_cp("arbitrary"),
    )(x, gf, tgt)


def ffn_down_bwd(dy, wd, u, l, deps=()):
    T, D = dy.shape
    cs = u.shape[-1]
    tm = _tile(T, WIDE_TILE)
    ni = T // tm

    def body(dy_ref, w_ref, u_ref, *rest):
        do_ref, du_ref, dyb_ref = rest[len(deps):]
        rows = pl.ds(pl.multiple_of(pl.program_id(1) * tm, tm), tm)

        @pl.when(pl.program_id(0) == 0)
        def _():
            d = (0.5 * dy_ref[...]).astype(BF16)
            dyb_ref[rows, :] = d
            do_ref[...] = d

        ds = _dot_nt(dyb_ref[rows, :], w_ref[0])
        a = u_ref[0, 0].astype(F32)
        b = u_ref[1, 0].astype(F32)
        sg = _sigmoid(a)
        du_ref[0, 0] = (ds * b * (sg * (1.0 + a * (1.0 - sg)))).astype(BF16)
        du_ref[1, 0] = (ds * (a * sg)).astype(BF16)

    first = lambda j, i: (jnp.where(j == 0, i, ni - 1), 0)
    blk = pl.BlockSpec((2, 1, tm, cs), lambda j, i: (0, j, i, 0))
    return _pallas_call(
        body, name="ffn_down_bwd", grid=(4, ni),
        in_specs=[pl.BlockSpec((tm, D), first), pl.BlockSpec((1, cs, D), lambda j, i: (l, j, 0)), blk] + [ANY] * len(deps),
        out_specs=[pl.BlockSpec((tm, D), first), blk],
        out_shape=[jax.ShapeDtypeStruct((T, D), BF16), jax.ShapeDtypeStruct((2, 4, T, cs), BF16)],
        scratch_shapes=[pltpu.VMEM((T, D), BF16)],
        compiler_params=_cp("arbitrary", "arbitrary"),
    )(dy, wd, u, *deps)


def dw_tn(name, a_ops, a_spec, b_ops, b_spec, G, M, N, T, l, L, prev):
    tk = _tile(T, SUM_TILE)
    nk = T // tk

    def body(*refs):
        a_refs, b_refs = refs[:len(a_ops)], refs[len(a_ops):len(a_ops) + len(b_ops)]
        o32_ref, o16_ref, acc_ref = refs[-3:]
        k = pl.program_id(1)

        @pl.when(k == 0)
        def _():
            acc_ref[...] = jnp.zeros_like(acc_ref)

        av = a_refs[0][0] if len(a_refs[0].shape) == 3 else a_refs[0][...]
        bv = b_refs[0][0] if len(b_refs[0].shape) == 3 else b_refs[0][...]
        acc_ref[...] += _dot_tn(av, bv)

        @pl.when(k == nk - 1)
        def _():
            o32_ref[0, 0] = acc_ref[...]
            o16_ref[0, 0] = acc_ref[...].astype(BF16)

    n_in = len(a_ops) + len(b_ops)
    in_specs = [a_spec(tk), b_spec(tk)]
    args = list(a_ops) + list(b_ops)
    aliases = {}
    if prev is not None:
        in_specs += [ANY, ANY]
        args += list(prev)
        aliases = {n_in: 0, n_in + 1: 1}

    def body_wrap(*refs):
        if prev is not None:
            refs = refs[:n_in] + refs[n_in + 2:]
        body(*refs)

    out = pl.BlockSpec((1, 1, M, N), lambda g, k: (l, g, 0, 0))
    return _pallas_call(
        body_wrap, name=name, grid=(G, nk),
        in_specs=in_specs, out_specs=[out, out],
        out_shape=[jax.ShapeDtypeStruct((L, G, M, N), F32), jax.ShapeDtypeStruct((L, G, M, N), BF16)],
        scratch_shapes=[pltpu.VMEM((M, N), F32)],
        input_output_aliases=aliases,
        compiler_params=_cp("parallel", "arbitrary"),
    )(*args)


def dx_norm_bwd(name, dact, d_spec, w, G, x, g, dy, l):
    T, D = x.shape
    c = w.shape[-1]
    tm = _tile(T, ROW_TILE)
    ni = T // tm

    def body(d_ref, w_ref, x_ref, g_ref, dy_ref, dx_ref, dg_ref, acc_ref):
        j, i = pl.program_id(0), pl.program_id(1)
        rows = pl.ds(pl.multiple_of(i * tm, tm), tm)

        @pl.when(jnp.logical_and(i == 0, j == 0))
        def _():
            dg_ref[...] = jnp.zeros_like(dg_ref)

        dv = d_ref[0] if len(d_ref.shape) == 3 else d_ref[...]
        part = _dot_nt(dv, w_ref[0, 0])

        @pl.when(j == 0)
        def _():
            acc_ref[rows, :] = part

        @pl.when(j > 0)
        def _():
            acc_ref[rows, :] += part

        @pl.when(j == G - 1)
        def _():
            dx, dg = _rms_bwd(acc_ref[rows, :], x_ref[...], g_ref[0], dy_ref[...])
            dx_ref[...] = dx
            dg_ref[...] += dg

    last = pl.BlockSpec((tm, D), lambda j, i: (jnp.where(j == G - 1, i, 0), 0))
    return _pallas_call(
        body, name=name, grid=(G, ni),
        in_specs=[d_spec(tm), pl.BlockSpec((1, 1, D, c), lambda j, i: (l, j, 0, 0)), last, pl.BlockSpec((1, 1, D), lambda j, i: (l, 0, 0)), last],
        out_specs=[last, pl.BlockSpec((1, D), lambda j, i: (0, 0))],
        out_shape=[jax.ShapeDtypeStruct((T, D), F32), jax.ShapeDtypeStruct((1, D), F32)],
        scratch_shapes=[pltpu.VMEM((T, D), F32)],
        compiler_params=_cp("arbitrary", "arbitrary"),
    )(dact, w, x, g, dy)


def mix_out_bwd(dy, proj, yp, yl, wpu, wlu, wout, P, R, l):
    T, D = dy.shape
    tm = _tile(T, ROW_TILE)
    gb = (P + 2 * R) // D

    def body(dy_ref, gp_ref, gl_ref, yp_ref, yl_ref, wpu_ref, wlu_ref, wo_ref,
             dyb_ref, dyp_ref, dyl_ref, dgp_ref, dgl_ref, dpm_ref, dhl_ref):
        dyb = dy_ref[...].astype(BF16)
        dyb_ref[...] = dyb
        dz = _dot_nt(dyb, wo_ref[0])
        sp = _sigmoid(gp_ref[...].astype(F32))
        sl = _sigmoid(gl_ref[...].astype(F32))
        dgp_ref[...] = (dz * yp_ref[...].astype(F32) * sp * (1.0 - sp)).astype(BF16)
        dgl_ref[...] = (dz * yl_ref[...].astype(F32) * sl * (1.0 - sl)).astype(BF16)
        dyp = (dz * sp).astype(BF16)
        dyl = (dz * sl).astype(BF16)
        dyp_ref[...] = dyp
        dyl_ref[...] = dyl
        dpm_ref[...] = _dot_nt(dyp, wpu_ref[0]).astype(BF16)
        dhl_ref[...] = _dot_nt(dyl, wlu_ref[0]).astype(BF16)

    row = lambda w: pl.BlockSpec((tm, w), lambda i: (i, 0))
    return _pallas_call(
        body, name="mix_out_bwd", grid=(T // tm,),
        in_specs=[row(D), pl.BlockSpec((tm, D), lambda i: (i, gb)), pl.BlockSpec((tm, D), lambda i: (i, gb + 1)), row(D), row(D),
                  pl.BlockSpec((1, P, D), lambda i: (l, 0, 0)), pl.BlockSpec((1, R, D), lambda i: (l, 0, 0)),
                  pl.BlockSpec((1, D, D), lambda i: (l, 0, 0))],
        out_specs=[row(D)] * 5 + [row(P), row(R)],
        out_shape=[jax.ShapeDtypeStruct((T, D), BF16)] * 5 + [jax.ShapeDtypeStruct((T, P), BF16), jax.ShapeDtypeStruct((T, R), BF16)],
        compiler_params=_cp("parallel"),
    )(dy, proj, proj, yp, yl, wpu, wlu, wout)


def lru_bwd(proj, hs, dhl, cw, cb, wa, ba, wx, bx, lam, P, l):
    T = proj.shape[0]
    _, H, hd, _ = wa.shape
    R = H * hd
    CW = cw.shape[1]

    def body(u_ref, ug_ref, cw_ref, cb_ref, wa_ref, ba_ref, wx_ref, bx_ref, lam_ref, hs_ref, dhl_ref,
             du_ref, dug_ref, dcw_ref, dcb_ref, dwa_ref, dba_ref, dwx_ref, dbx_ref, dlam_ref, c_s, g_s, y_s):
        u = u_ref[...].astype(F32)
        v = _conv(u, cw_ref, cb_ref[0])
        lam = lam_ref[0]
        r, i, sp, a, mult = _lru_gates(v, wa_ref, ba_ref[0], wx_ref, bx_ref[0], lam)
        ug = ug_ref[...].astype(F32)
        ge, th = _gelu(ug)
        hs = hs_ref[...]
        dhl = dhl_ref[...].astype(F32)
        dug_ref[...] = (dhl * hs * _gelu_grad(ug, th)).astype(BF16)
        c_s[...] = _shift_up(a, 1)
        g_s[...] = dhl * ge
        _scan_bwd(c_s, g_s, y_s)
        y = y_s[...]
        da = y * _shift_down(hs, 1)
        iv = i * v
        dlog_a = da * a - (y * iv) * (a * a) / mult
        div = y * mult
        dpa = (dlog_a * (-LRU_C) * sp) * r * (1.0 - r)
        dpx = (div * v) * i * (1.0 - i)
        dsp = jnp.sum(dlog_a * (-LRU_C) * r, axis=0, keepdims=True)
        dlam_ref[0] = -dsp * _sigmoid(-lam)
        vb = v.astype(BF16)
        dpab, dpxb = dpa.astype(BF16), dpx.astype(BF16)
        dwa_ref[0, 0] = _dot_tn(vb, dpab)
        dwx_ref[0, 0] = _dot_tn(vb, dpxb)
        dba_ref[0] = jnp.sum(dpa, axis=0, keepdims=True)
        dbx_ref[0] = jnp.sum(dpx, axis=0, keepdims=True)
        dv = div * i + _dot_nt(dpab, wa_ref[0, 0].astype(BF16)) + _dot_nt(dpxb, wx_ref[0, 0].astype(BF16))
        dcb_ref[0] = jnp.sum(dv, axis=0, keepdims=True)
        du = jnp.zeros_like(dv)
        for k in range(CW):
            du = du + cw_ref[0, k:k + 1, :] * _shift_up(dv, CW - 1 - k)
            dcw_ref[0, k:k + 1, :] = jnp.sum(dv * _shift_down(u, CW - 1 - k), axis=0, keepdims=True)
        du_ref[...] = du.astype(BF16)

    col = pl.BlockSpec((T, hd), lambda h: (0, h))
    vec = pl.BlockSpec((1, 1, hd), lambda h: (0, 0, h))
    mat = pl.BlockSpec((1, 1, hd, hd), lambda h: (0, h, 0, 0))
    vshape = jax.ShapeDtypeStruct((1, 1, R), F32)
    mshape = jax.ShapeDtypeStruct((1, H, hd, hd), F32)
    return _pallas_call(
        body, name="lru_bwd", grid=(H,),
        in_specs=_lru_specs(T, hd, P, R, CW, l) + [col, col],
        out_specs=[col, col, pl.BlockSpec((1, CW, hd), lambda h: (0, 0, h)), vec, mat, vec, mat, vec, vec],
        out_shape=[jax.ShapeDtypeStruct((T, R), BF16)] * 2 + [jax.ShapeDtypeStruct((1, CW, R), F32), vshape, mshape, vshape, mshape, vshape, vshape],
        scratch_shapes=[pltpu.VMEM((T, hd), F32)] * 3,
        compiler_params=_cp("parallel"),
    )(proj, proj, cw, cb, wa, ba, wx, bx, lam, hs, dhl)


def pool_bwd(proj, dpm, pw, pb, ps, l):
    T = proj.shape[0]
    _, G, gd, _ = pw.shape
    P = G * gd

    def body(u_ref, d_ref, w_ref, b_ref, s_ref, du_ref, dw_ref, db_ref, dsc_ref):
        for gi in range(G):
            cols = slice(gi * gd, (gi + 1) * gd)
            w = POOL_WINDOWS[gi]
            inv = _inv_count(T, w)
            ug = u_ref[:, cols].astype(F32)
            pooled = _pooled(ug, w, inv).astype(BF16)
            wb = w_ref[0, gi].astype(BF16)
            mixed = _dot(pooled, wb) + b_ref[0, :, cols]
            dpm_g = d_ref[:, cols].astype(F32)
            dsc_ref[0, :, cols] = jnp.sum(dpm_g * mixed, axis=0, keepdims=True)
            dmixed = dpm_g * s_ref[0, :, cols]
            db_ref[0, :, cols] = jnp.sum(dmixed, axis=0, keepdims=True)
            dmb = dmixed.astype(BF16)
            dw_ref[0, gi] = _dot_tn(pooled, dmb)
            dpooled = _dot_nt(dmb, wb)
            s = dpooled * inv
            k = 1
            while k < w:
                s = s + _shift_up(s, k)
                k *= 2
            du_ref[:, cols] = (s - dpooled).astype(BF16)

    vec = pl.BlockSpec((1, 1, P), lambda i: (l, 0, 0))
    ovec = pl.BlockSpec((1, 1, P), lambda i: (0, 0, 0))
    return _pallas_call(
        body, name="pool_bwd", grid=(1,),
        in_specs=[pl.BlockSpec((T, P), lambda i: (0, 0)), pl.BlockSpec((T, P), lambda i: (0, 0)),
                  pl.BlockSpec((1, G, gd, gd), lambda i: (l, 0, 0, 0)), vec, vec],
        out_specs=[pl.BlockSpec((T, P), lambda i: (0, 0)), pl.BlockSpec((1, G, gd, gd), lambda i: (0, 0, 0, 0)), ovec, ovec],
        out_shape=[jax.ShapeDtypeStruct((T, P), BF16), jax.ShapeDtypeStruct((1, G, gd, gd), F32),
                   jax.ShapeDtypeStruct((1, 1, P), F32), jax.ShapeDtypeStruct((1, 1, P), F32)],
        compiler_params=_cp("arbitrary"),
    )(proj, dpm, pw, pb, ps)


def _place():
    x, y, c = lax.axis_index("x"), lax.axis_index("y"), lax.axis_index("c")
    return x, y, c


def all_gather(name, shards):
    n = len(shards)

    def body(*refs):
        src, out = refs[:n], refs[n:2 * n]
        send_sems, recv_sems, local_sems = refs[2 * n:]
        x, y, c = _place()
        sibling = (x, y, 1 - c)
        chips = [(x, 1 - y), (1 - x, y), (1 - x, 1 - y)]

        def slot(a, px, py, pc):
            return out[a].at[:, 4 * px + 2 * py + pc]

        def copy(a, k, block, to, from_src=False):
            return pltpu.make_async_remote_copy(
                src_ref=src[a] if from_src else slot(a, *block), dst_ref=slot(a, *block),
                send_sem=send_sems.at[a, k], recv_sem=recv_sems.at[a, k], device_id=to, device_id_type=MESH)

        me = (x, y, c)
        mine = [pltpu.make_async_copy(src[a], slot(a, *me), local_sems.at[a]) for a in range(n)]
        first = []
        for j, chip in enumerate(chips):
            for a in range(n):
                first.append(copy(a, 1 + j, me, (*chip, c), from_src=True))
        for a in range(n):
            first.append(copy(a, 0, me, sibling, from_src=True))
        for cp in mine + first:
            cp.start()
        passed = []
        for j, chip in enumerate(chips):
            for a in range(n):
                copy(a, 1 + j, (*chip, c), me).wait_recv()
                fwd = copy(a, 4 + j, (*chip, c), sibling)
                fwd.start()
                passed.append(fwd)
        for a in range(n):
            copy(a, 0, (x, y, 1 - c), me).wait_recv()
        for j, chip in enumerate(chips):
            for a in range(n):
                copy(a, 4 + j, (*chip, 1 - c), me).wait_recv()
        for cp in first + passed:
            cp.wait_send()
        for cp in mine:
            cp.wait()

    outs = _pallas_call(
        body, name=name,
        in_specs=[ANY] * n, out_specs=[ANY] * n,
        out_shape=[jax.ShapeDtypeStruct((s.shape[0], N_DEV) + s.shape[1:], s.dtype) for s in shards],
        scratch_shapes=[pltpu.SemaphoreType.DMA((n, 7)), pltpu.SemaphoreType.DMA((n, 7)), pltpu.SemaphoreType.DMA((n,))],
        compiler_params=pltpu.CompilerParams(has_side_effects=True),
    )(*shards)
    return list(outs)


def pair_exchange(name, g16, layer=None):
    n = len(g16)
    layers = slice(None) if layer is None else pl.ds(layer, 1)

    def body(*refs):
        s16, recv = refs[:n], refs[n:2 * n]
        send_sems, recv_sems = refs[2 * n:]
        x, y, c = _place()
        sibling = (x, y, 1 - c)
        rem = []
        for a in range(n):
            for j in range(N_CHIP):
                rem.append(pltpu.make_async_remote_copy(
                    src_ref=s16[a].at[layers, 2 * j + 1 - c], dst_ref=recv[a].at[:, j],
                    send_sem=send_sems.at[a, j], recv_sem=recv_sems.at[a, j], device_id=sibling, device_id_type=MESH))
        for cp in rem:
            cp.start()
        for cp in rem:
            cp.wait_recv()
        for cp in rem:
            cp.wait_send()

    outs = _pallas_call(
        body, name=name,
        in_specs=[ANY] * n, out_specs=[ANY] * n,
        out_shape=[jax.ShapeDtypeStruct((s.shape[0] if layer is None else 1, N_CHIP) + s.shape[2:], s.dtype) for s in g16],
        scratch_shapes=[pltpu.SemaphoreType.DMA((n, N_CHIP))] * 2,
        compiler_params=pltpu.CompilerParams(has_side_effects=True),
    )(*g16)
    return list(outs)


def chip_exchange(name, pair16):
    n = len(pair16)

    def body(*refs):
        p16, recv2 = refs[:n], refs[n:2 * n]
        send_sems, recv_sems = refs[2 * n:]
        x, y, c = _place()
        rem = []
        for d in (1, 2, 3):
            px = 1 - x if d & 2 else x
            py = 1 - y if d & 1 else y
            for a in range(n):
                rem.append(pltpu.make_async_remote_copy(
                    src_ref=p16[a].at[:, 2 * px + py], dst_ref=recv2[a].at[:, d - 1],
                    send_sem=send_sems.at[a, d - 1], recv_sem=recv_sems.at[a, d - 1], device_id=(px, py, c), device_id_type=MESH))
        for cp in rem:
            cp.start()
        for cp in rem:
            cp.wait_recv()
        for cp in rem:
            cp.wait_send()

    outs = _pallas_call(
        body, name=name,
        in_specs=[ANY] * n, out_specs=[ANY] * n,
        out_shape=[jax.ShapeDtypeStruct((s.shape[0], 3) + s.shape[2:], s.dtype) for s in pair16],
        scratch_shapes=[pltpu.SemaphoreType.DMA((n, 3))] * 2,
        compiler_params=pltpu.CompilerParams(has_side_effects=True),
    )(*pair16)
    return list(outs)


HBM = pl.BlockSpec(memory_space=pltpu.HBM)
SEM = pl.BlockSpec(memory_space=pltpu.SEMAPHORE)
EFFECT = pltpu.SideEffectType.DATAFLOW_SIDE_EFFECTING


def _in_hbm(a):
    return pltpu.with_memory_space_constraint(a, pltpu.HBM)


def split_start(name, bufs, n_copies, copies_of):
    nb = len(bufs)

    def body(*refs):
        buf = refs[:nb]
        send_sems, recv_sems = refs[nb], refs[nb + 1]
        token = refs[-1]
        for i, (src, dst, dev) in enumerate(copies_of(buf)):
            pltpu.make_async_remote_copy(src_ref=src, dst_ref=dst, send_sem=send_sems.at[i], recv_sem=recv_sems.at[i],
                                         device_id=dev, device_id_type=MESH).start()
        token[...] = jnp.zeros_like(token)

    outs = _pallas_call(
        body, name=name,
        in_specs=[HBM] * nb,
        out_specs=(SEM, SEM, *([HBM] * nb), pl.BlockSpec(memory_space=pltpu.VMEM)),
        out_shape=(pltpu.SemaphoreType.DMA((n_copies,)), pltpu.SemaphoreType.DMA((n_copies,)),
                   *[pltpu.HBM(b.shape, b.dtype) for b in bufs], jax.ShapeDtypeStruct((8, 128), F32)),
        input_output_aliases={i: 2 + i for i in range(nb)},
        compiler_params=pltpu.CompilerParams(has_side_effects=EFFECT),
    )(*[_in_hbm(b) for b in bufs])
    return outs[0], outs[1], list(outs[2:2 + nb]), outs[-1]


def split_wait(name, bufs, send_sems, recv_sems, after, copies_of):
    nb = len(bufs)

    def body(*refs):
        buf = refs[:nb]
        send, recv = refs[nb], refs[nb + 1]
        for i, (src, dst, dev) in enumerate(copies_of(buf)):
            cp = pltpu.make_async_remote_copy(src_ref=src, dst_ref=dst, send_sem=send.at[i], recv_sem=recv.at[i],
                                              device_id=dev, device_id_type=MESH)
            cp.wait_send()
            cp.wait_recv()

    outs = _pallas_call(
        body, name=name,
        in_specs=[HBM] * nb + [SEM, SEM] + [ANY] * len(after),
        out_specs=[HBM] * nb,
        out_shape=[pltpu.HBM(b.shape, b.dtype) for b in bufs],
        input_output_aliases={i: i for i in range(nb)},
        compiler_params=pltpu.CompilerParams(has_side_effects=EFFECT),
    )(*bufs, send_sems, recv_sems, *after)
    return list(outs)


def place_own(w, l, place, dtype):
    _, rows, cols = w.shape
    tr = _rows_tile(rows, cols, 1 << 19)

    def body(p_ref, w_ref, o_ref):
        o_ref[0] = w_ref[0].astype(dtype)

    return _pallas_call(
        body, name="place_own",
        grid_spec=pltpu.PrefetchScalarGridSpec(
            num_scalar_prefetch=1, grid=(rows // tr,),
            in_specs=[pl.BlockSpec((1, tr, cols), lambda i, p: (l, i, 0))],
            out_specs=pl.BlockSpec((1, tr, cols), lambda i, p: (p[2], i, 0))),
        out_shape=jax.ShapeDtypeStruct((N_DEV, rows, cols), dtype), compiler_params=_cp("parallel"),
    )(place, w)


def _gather_copies(land):
    x, y, c = _place()
    k = 4 * x + 2 * y + c
    peers = [(x, 1 - y, c), (1 - x, y, c), (1 - x, 1 - y, c), (x, y, 1 - c)]
    return [(b.at[k], b.at[k], p) for p in peers for b in land]


def gather_start(name, land):
    return split_start(name, land, 4 * len(land), _gather_copies)


def gather_wait(name, land, send_sems, recv_sems, after):
    return split_wait(name, land, send_sems, recv_sems, after, _gather_copies)


def gather_forward(name, land):
    n = len(land)

    def body(*refs):
        buf = refs[:n]
        send_sems, recv_sems = refs[2 * n:]
        x, y, c = _place()
        rem = []
        for j, (px, py) in enumerate([(x, 1 - y), (1 - x, y), (1 - x, 1 - y)]):
            k = 4 * px + 2 * py + c
            for a in range(n):
                rem.append(pltpu.make_async_remote_copy(
                    src_ref=buf[a].at[k], dst_ref=buf[a].at[k], send_sem=send_sems.at[a, j], recv_sem=recv_sems.at[a, j],
                    device_id=(x, y, 1 - c), device_id_type=MESH))
        for cp in rem:
            cp.start()
        for cp in rem:
            cp.wait_recv()
        for cp in rem:
            cp.wait_send()

    outs = _pallas_call(
        body, name=name,
        in_specs=[ANY] * n, out_specs=[ANY] * n,
        out_shape=[jax.ShapeDtypeStruct(b.shape, b.dtype) for b in land],
        scratch_shapes=[pltpu.SemaphoreType.DMA((n, 3))] * 2,
        input_output_aliases={i: i for i in range(n)},
        compiler_params=pltpu.CompilerParams(has_side_effects=True),
    )(*land)
    return list(outs)


def _chip_copies(nsrc):
    def copies(buf):
        p16, recv2 = buf[:nsrc], buf[nsrc:]
        x, y, c = _place()
        out = []
        for d in (1, 2, 3):
            px = 1 - x if d & 2 else x
            py = 1 - y if d & 1 else y
            out += [(p16[a].at[:, 2 * px + py], recv2[a].at[:, d - 1], (px, py, c)) for a in range(nsrc)]
        return out
    return copies


def chip_exchange_start(name, pair16):
    n = len(pair16)
    land = [lax.empty((s.shape[0], 3) + s.shape[2:], s.dtype) for s in pair16]
    return split_start(name, list(pair16) + land, 3 * n, _chip_copies(n))


def chip_exchange_wait(name, bufs, send_sems, recv_sems, after):
    n = len(bufs) // 2
    return split_wait(name, bufs, send_sems, recv_sems, after, _chip_copies(n))[n:]


def _rows_tile(rows, cols, budget=1 << 20):
    t = rows
    while t % 2 == 0 and t * cols > budget and (t // 2) % 16 == 0:
        t //= 2
    return t


def pair_sum(g32, recv1, place, l):
    _, _, rows, cols = recv1.shape
    tr = _rows_tile(rows, cols)

    def body(p_ref, m_ref, r_ref, o_ref):
        o_ref[...] = (m_ref[...] + r_ref[...].astype(F32)).astype(o_ref.dtype)

    blk = pl.BlockSpec((1, 1, tr, cols), lambda j, i, p: (0, j, i, 0))
    return _pallas_call(
        body, name="pair_sum",
        grid_spec=pltpu.PrefetchScalarGridSpec(
            num_scalar_prefetch=1, grid=(N_CHIP, rows // tr),
            in_specs=[pl.BlockSpec((1, 1, tr, cols), lambda j, i, p: (l, 2 * j + p[0], i, 0)), blk], out_specs=blk),
        out_shape=jax.ShapeDtypeStruct(recv1.shape, recv1.dtype), compiler_params=_cp("parallel", "parallel"),
    )(place, g32, recv1)


def _grad_in_specs(tr, cols, l):
    return ([pl.BlockSpec((1, 1, tr, cols), lambda i, p: (l, p[2], i, 0)), pl.BlockSpec((1, 1, tr, cols), lambda i, p: (0, p[1], i, 0))]
            + [pl.BlockSpec((1, 1, tr, cols), lambda i, p, d=d: (0, d, i, 0)) for d in range(3)])


def _grad_total(o32, o16, r0, r1, r2):
    return (o32[0, 0] + o16[0, 0].astype(F32)) + r0[0, 0].astype(F32) + r1[0, 0].astype(F32) + r2[0, 0].astype(F32)


def grad_sum(g32, recv1, recv2, place):
    _, _, rows, cols = recv1.shape
    tr = _rows_tile(rows, cols)

    def body(p_ref, o32, o16, r0, r1, r2, g_ref):
        g_ref[...] = _grad_total(o32, o16, r0, r1, r2)

    return _pallas_call(
        body, name="grad_sum",
        grid_spec=pltpu.PrefetchScalarGridSpec(
            num_scalar_prefetch=1, grid=(rows // tr,), in_specs=_grad_in_specs(tr, cols, 0),
            out_specs=pl.BlockSpec((tr, cols), lambda i, p: (i, 0))),
        out_shape=jax.ShapeDtypeStruct((rows, cols), F32), compiler_params=_cp("parallel"),
    )(place, g32, recv1, recv2, recv2, recv2)


def _adamw_math(w, g, m, v):
    m = ADAM_B1 * m + (1.0 - ADAM_B1) * g
    v = ADAM_B2 * v + (1.0 - ADAM_B2) * (g * g)
    m_hat = m / (1.0 - ADAM_B1 ** ADAM_STEP)
    v_hat = v / (1.0 - ADAM_B2 ** ADAM_STEP)
    delta = -ADAM_LR * (m_hat / (jnp.sqrt(v_hat) + ADAM_EPS) + ADAM_WD * w)
    return delta, m, v


def grad_sum_adamw(g32, recv1, recv2, w, m, v, place, l, prev):
    L, rows, cols = w.shape
    tr = _rows_tile(rows, cols, 1 << 18)

    def body(p_ref, o32, o16, r0, r1, r2, w_ref, m_ref, v_ref, *rest):
        g_ref, d_ref, nm_ref, nv_ref = rest[-4:]
        g = _grad_total(o32, o16, r0, r1, r2)
        d, nm, nv = _adamw_math(w_ref[0], g, m_ref[0], v_ref[0])
        g_ref[0] = g
        d_ref[0] = d
        nm_ref[0] = nm
        nv_ref[0] = nv

    blk = pl.BlockSpec((1, tr, cols), lambda i, p: (l, i, 0))
    args = [g32, recv1, recv2, recv2, recv2, w, m, v]
    in_specs = _grad_in_specs(tr, cols, l) + [blk] * 3
    aliases = {}
    if prev is not None:
        aliases = {1 + len(args) + k: k for k in range(4)}
        args += list(prev)
        in_specs += [ANY] * 4
    return _pallas_call(
        body, name="grad_sum_adamw",
        grid_spec=pltpu.PrefetchScalarGridSpec(num_scalar_prefetch=1, grid=(rows // tr,), in_specs=in_specs, out_specs=[blk] * 4),
        out_shape=[jax.ShapeDtypeStruct((L, rows, cols), F32)] * 4, input_output_aliases=aliases,
        compiler_params=_cp("parallel"),
    )(place, *args)


def adamw(w, g, m, v):
    rows, cols = w.shape
    tr = _rows_tile(rows, cols, 1 << 18)

    def body(w_ref, g_ref, m_ref, v_ref, d_ref, nm_ref, nv_ref):
        d, nm, nv = _adamw_math(w_ref[...], g_ref[...], m_ref[...], v_ref[...])
        d_ref[...] = d
        nm_ref[...] = nm
        nv_ref[...] = nv

    blk = pl.BlockSpec((tr, cols), lambda i: (i, 0))
    return _pallas_call(body, name="adamw_small", grid=(rows // tr,), in_specs=[blk] * 4, out_specs=[blk] * 3,
                        out_shape=[jax.ShapeDtypeStruct((rows, cols), F32)] * 3, compiler_params=_cp("parallel"))(w, g, m, v)


SMALL = ("norm_ffn1", "norm_mix", "pool_w", "pool_b", "pool_scale", "conv_w", "conv_b", "lru_w_a", "lru_b_a", "lru_w_x", "lru_b_x",
         "lru_lambda", "norm_ffn2", "final_norm")
BIG = ("ffn1_w_up", "ffn1_w_down", "w_in", "w_pool_up", "w_lru_up", "w_out", "ffn2_w_up", "ffn2_w_down")
NAMES = ("norm_ffn1", "ffn1_w_up", "ffn1_w_down", "norm_mix", "w_in", "pool_w", "pool_b", "pool_scale", "w_pool_up", "conv_w", "conv_b",
         "lru_w_a", "lru_b_a", "lru_w_x", "lru_b_x", "lru_lambda", "w_lru_up", "w_out", "norm_ffn2", "ffn2_w_up", "ffn2_w_down", "final_norm")
PACK_ROWS = 16 * N_DEV


def _pack(parts):
    flat = jnp.concatenate([p.reshape(-1) for p in parts])
    unit = 128 * PACK_ROWS
    padded = -(-flat.size // unit) * unit
    return jnp.pad(flat, (0, padded - flat.size)).reshape(-1, 128)


def _unpack(packed, shapes):
    flat = packed.reshape(-1)
    out, off = [], 0
    for s in shapes:
        n = 1
        for d in s:
            n *= d
        out.append(flat[off:off + n].reshape(s))
        off += n
    return out


def kernel(x, norm_ffn1, ffn1_w_up, ffn1_w_down, norm_mix, w_in, pool_w, pool_b, pool_scale, w_pool_up, conv_w, conv_b, lru_w_a, lru_b_a, lru_w_x, lru_b_x, lru_lambda, w_lru_up, w_out, norm_ffn2, ffn2_w_up, ffn2_w_down, final_norm, loss_target, m_norm_ffn1, m_ffn1_w_up, m_ffn1_w_down, m_norm_mix, m_w_in, m_pool_w, m_pool_b, m_pool_scale, m_w_pool_up, m_conv_w, m_conv_b, m_lru_w_a, m_lru_b_a, m_lru_w_x, m_lru_b_x, m_lru_lambda, m_w_lru_up, m_w_out, m_norm_ffn2, m_ffn2_w_up, m_ffn2_w_down, m_final_norm, v_norm_ffn1, v_ffn1_w_up, v_ffn1_w_down, v_norm_mix, v_w_in, v_pool_w, v_pool_b, v_pool_scale, v_w_pool_up, v_conv_w, v_conv_b, v_lru_w_a, v_lru_b_a, v_lru_w_x, v_lru_b_x, v_lru_lambda, v_w_lru_up, v_w_out, v_norm_ffn2, v_ffn2_w_up, v_ffn2_w_down, v_final_norm):
    W = dict(norm_ffn1=norm_ffn1, ffn1_w_up=ffn1_w_up, ffn1_w_down=ffn1_w_down, norm_mix=norm_mix, w_in=w_in, pool_w=pool_w, pool_b=pool_b,
             pool_scale=pool_scale, w_pool_up=w_pool_up, conv_w=conv_w, conv_b=conv_b, lru_w_a=lru_w_a, lru_b_a=lru_b_a, lru_w_x=lru_w_x,
             lru_b_x=lru_b_x, lru_lambda=lru_lambda, w_lru_up=w_lru_up, w_out=w_out, norm_ffn2=norm_ffn2, ffn2_w_up=ffn2_w_up,
             ffn2_w_down=ffn2_w_down, final_norm=final_norm)
    M = dict(norm_ffn1=m_norm_ffn1, ffn1_w_up=m_ffn1_w_up, ffn1_w_down=m_ffn1_w_down, norm_mix=m_norm_mix, w_in=m_w_in, pool_w=m_pool_w,
             pool_b=m_pool_b, pool_scale=m_pool_scale, w_pool_up=m_w_pool_up, conv_w=m_conv_w, conv_b=m_conv_b, lru_w_a=m_lru_w_a,
             lru_b_a=m_lru_b_a, lru_w_x=m_lru_w_x, lru_b_x=m_lru_b_x, lru_lambda=m_lru_lambda, w_lru_up=m_w_lru_up, w_out=m_w_out,
             norm_ffn2=m_norm_ffn2, ffn2_w_up=m_ffn2_w_up, ffn2_w_down=m_ffn2_w_down, final_norm=m_final_norm)
    V = dict(norm_ffn1=v_norm_ffn1, ffn1_w_up=v_ffn1_w_up, ffn1_w_down=v_ffn1_w_down, norm_mix=v_norm_mix, w_in=v_w_in, pool_w=v_pool_w,
             pool_b=v_pool_b, pool_scale=v_pool_scale, w_pool_up=v_w_pool_up, conv_w=v_conv_w, conv_b=v_conv_b, lru_w_a=v_lru_w_a,
             lru_b_a=v_lru_b_a, lru_w_x=v_lru_w_x, lru_b_x=v_lru_b_x, lru_lambda=v_lru_lambda, w_lru_up=v_w_lru_up, w_out=v_w_out,
             norm_ffn2=v_norm_ffn2, ffn2_w_up=v_ffn2_w_up, ffn2_w_down=v_ffn2_w_down, final_norm=v_final_norm)

    T, D = x.shape[1], x.shape[2]
    L = norm_ffn1.shape[0]
    P = pool_scale.shape[1]
    R = lru_lambda.shape[1]
    H, hd = lru_w_a.shape[1], lru_w_a.shape[2]
    CW = conv_w.shape[1]
    cs = ffn1_w_up.shape[2]
    ci = w_in.shape[2]
    xin = x.reshape(T, D)
    tgt = loss_target.reshape(T, D)
    dev = 4 * lax.axis_index("x") + 2 * lax.axis_index("y") + lax.axis_index("c")
    place = jnp.stack([lax.axis_index("c"), 2 * lax.axis_index("x") + lax.axis_index("y"), dev]).astype(jnp.int32)

    cw_flat = conv_w.reshape(L, -1)
    cw_pad = (-cw_flat.shape[1]) % 1024
    cw_tiles = jnp.pad(cw_flat, ((0, 0), (0, cw_pad))).reshape(L, -1, 128)

    def gather_layer_start(l):
        land = [place_own(W[n], l, place, BF16) for n in BIG] + [place_own(cw_tiles, l, place, F32)]
        return gather_start(f"gather_start_l{l}", land)

    def layer_weights(land):
        g = dict(zip(BIG + ("conv_w",), land))
        one = lambda a: a.reshape((1,) + a.shape)
        cw_l = g["conv_w"].reshape(N_DEV, -1)[:, :cw_flat.shape[1]].reshape((N_DEV,) + conv_w.shape[1:])
        return dict(wup1=one(g["ffn1_w_up"]), wup2=one(g["ffn2_w_up"]), win=one(g["w_in"]),
                    wd1=g["ffn1_w_down"].reshape(1, -1, D), wd2=g["ffn2_w_down"].reshape(1, -1, D),
                    wlu=g["w_lru_up"].reshape(1, R, D), wout=g["w_out"].reshape(1, D, D),
                    wpu=g["w_pool_up"].transpose(1, 0, 2).reshape(1, P, D),
                    cw=cw_l.transpose(1, 0, 2).reshape(1, CW, R))

    def layer_params(l):
        vec = lambda a: a[l:l + 1].reshape(1, 1, -1)
        return dict(g1=vec(norm_ffn1), gm=vec(norm_mix), g2=vec(norm_ffn2), pb=vec(pool_b), ps=vec(pool_scale), cb=vec(conv_b),
                    ba=vec(lru_b_a), bx=vec(lru_b_x), lam=vec(lru_lambda), pw=pool_w[l:l + 1], wa=lru_w_a[l:l + 1], wx=lru_w_x[l:l + 1])

    AHEAD = 2
    started = {l: gather_layer_start(l) for l in range(min(AHEAD, L))}
    saved, LW, LP = [], [], []
    xc = xin
    for l in range(L):
        send_sems, recv_sems, land, _tok = started.pop(l)
        after = [xc] + [s[3] for s in started.values()]
        land = gather_wait(f"gather_wait_l{l}", land, send_sems, recv_sems, after)
        land = gather_forward(f"gather_forward_l{l}", land)
        if l + AHEAD < L:
            started[l + AHEAD] = gather_layer_start(l + AHEAD)
        w, p = layer_weights(land), layer_params(l)
        LW.append(w)
        LP.append(p)
        sv = {"x1": xc}
        sv["h1"], sv["u1"], sv["s1"] = ffn_up(xc, p["g1"], w["wup1"], 0)
        xc = ffn_down(sv["s1"], w["wd1"], xc, 0)
        sv["x2"] = xc
        sv["h2"], sv["proj"] = mix_in(xc, p["gm"], w["win"], 0)
        sv["pm"] = pool_fwd(sv["proj"], p["pw"], p["pb"], p["ps"], 0)
        sv["hl"], sv["hs"] = lru_fwd(sv["proj"], w["cw"], p["cb"], p["wa"], p["ba"], p["wx"], p["bx"], p["lam"], P, 0)
        xc, sv["yp"], sv["yl"], sv["z"] = mix_out(sv["pm"], sv["hl"], sv["proj"], xc, w["wpu"], w["wlu"], w["wout"], P, 0)
        sv["x3"] = xc
        sv["h3"], sv["u3"], sv["s3"] = ffn_up(xc, p["g2"], w["wup2"], 0)
        xc = ffn_down(sv["s3"], w["wd2"], xc, 0)
        saved.append(sv)

    loss_part, dx, d_final = loss_head(xc, final_norm.reshape(1, D), tgt)
    loss = lax.psum(loss_part[0, 0], ("x", "y", "c"))

    G = {n: None for n in BIG}
    small = {n: [None] * L for n in SMALL if n != "final_norm"}

    def to_slots(name, a):
        if name == "w_pool_up":
            return a.reshape(L, P, N_DEV, D // N_DEV).transpose(0, 2, 1, 3)
        return a.reshape((L, N_DEV) + W[name].shape[1:])

    def ffn_bwd(dy, sv, tag, wup, wd, gn, up_name, dn_name, norm_name, l, deps=()):
        dout, du = ffn_down_bwd(dy, wd, sv["u" + tag], 0, deps)
        du = du.reshape(N_DEV, T, cs)
        G[dn_name] = dw_tn("dw_down", [sv["s" + tag]], lambda tk: pl.BlockSpec((1, tk, cs), lambda g, k: (g, k, 0)),
                           [dout], lambda tk: pl.BlockSpec((tk, D), lambda g, k: (k, 0)), 4, cs, D, T, l, L, G[dn_name])
        G[up_name] = dw_tn("dw_up", [sv["h" + tag]], lambda tk: pl.BlockSpec((tk, D), lambda g, k: (k, 0)),
                           [du], lambda tk: pl.BlockSpec((1, tk, cs), lambda g, k: (g, k, 0)), N_DEV, D, cs, T, l, L, G[up_name])
        dxn, dg = dx_norm_bwd("ffn_dx", du, lambda tm: pl.BlockSpec((1, tm, cs), lambda j, i: (j, i, 0)), wup, N_DEV,
                              sv["x" + tag], gn, dy, 0)
        small[norm_name][l] = dg.reshape(D)
        return dxn

    in_flight = []
    deps = ()
    for l in reversed(range(L)):
        sv, w, p = saved[l], LW[l], LP[l]
        dx = ffn_bwd(dx, sv, "3", w["wup2"], w["wd2"], p["g2"], "ffn2_w_up", "ffn2_w_down", "norm_ffn2", l, deps)
        dyb, dyp, dyl, dgp, dgl, dpm, dhl = mix_out_bwd(dx, sv["proj"], sv["yp"], sv["yl"], w["wpu"], w["wlu"], w["wout"], P, R, 0)
        row = lambda wd_: (lambda tk: pl.BlockSpec((tk, wd_), lambda g, k: (k, 0)))
        G["w_out"] = dw_tn("dw_out", [sv["z"]], row(D), [dyb], row(D), 1, D, D, T, l, L, G["w_out"])
        G["w_lru_up"] = dw_tn("dw_lru_up", [sv["hl"]], row(R), [dyl], row(D), 1, R, D, T, l, L, G["w_lru_up"])
        G["w_pool_up"] = dw_tn("dw_pool_up", [sv["pm"]], row(P), [dyp], row(D), 1, P, D, T, l, L, G["w_pool_up"])
        du_lru, du_gelu, dcw, dcb, dwa, dba, dwx, dbx, dlam = lru_bwd(
            sv["proj"], sv["hs"], dhl, w["cw"], p["cb"], p["wa"], p["ba"], p["wx"], p["bx"], p["lam"], P, 0)
        du_pool, dpw, dpb, dpsc = pool_bwd(sv["proj"], dpm, p["pw"], p["pb"], p["ps"], 0)
        dproj = jnp.concatenate([du_pool, du_lru, du_gelu, dgp, dgl], axis=1)
        G["w_in"] = dw_tn("dw_in", [sv["h2"]], row(D), [dproj], lambda tk: pl.BlockSpec((tk, ci), lambda g, k: (k, g)),
                          N_DEV, D, ci, T, l, L, G["w_in"])
        dx, dgm = dx_norm_bwd("mix_dx", dproj, lambda tm: pl.BlockSpec((tm, ci), lambda j, i: (i, j)), w["win"], N_DEV,
                              sv["x2"], p["gm"], dx, 0)
        small["norm_mix"][l] = dgm.reshape(D)
        small["pool_w"][l], small["pool_b"][l], small["pool_scale"][l] = dpw[0], dpb.reshape(pool_b.shape[1:]), dpsc.reshape(P)
        small["conv_w"][l], small["conv_b"][l] = dcw[0], dcb.reshape(R)
        small["lru_w_a"][l], small["lru_b_a"][l] = dwa[0], dba.reshape(H, hd)
        small["lru_w_x"][l], small["lru_b_x"][l] = dwx[0], dbx.reshape(H, hd)
        small["lru_lambda"][l] = dlam.reshape(R)
        dx = ffn_bwd(dx, sv, "1", w["wup1"], w["wd1"], p["g1"], "ffn1_w_up", "ffn1_w_down", "norm_ffn1", l)
        recv1_l = pair_exchange(f"rs_pair_exchange_l{l}", [to_slots(n, G[n][1]) for n in BIG], layer=l)
        pair16_l = [pair_sum(to_slots(n, G[n][0]), r_, place, l) for n, r_ in zip(BIG, recv1_l)]
        send_sems, recv_sems, bufs, tok = chip_exchange_start(f"rs_chip_start_l{l}", pair16_l)
        in_flight.append((l, send_sems, recv_sems, bufs, recv1_l))
        deps = (tok,)

    grad_x = dx.reshape(x.shape)

    small_parts = [jnp.stack(small[n]) for n in SMALL if n != "final_norm"] + [d_final.reshape(D)]
    small_shapes = [p.shape for p in small_parts]
    gpack = _pack(small_parts).reshape(1, N_DEV, -1, 128)
    recv1_s = pair_exchange("rs_pair_exchange_small", [gpack])[0]
    pair_s = pair_sum(gpack, recv1_s, place, 0)
    recv2_s = chip_exchange("rs_chip_exchange_small", [pair_s])[0]

    g32 = [to_slots(n, G[n][0]) for n in BIG]
    outs = {n: None for n in BIG}
    after = [dx, recv2_s]
    for l, send_sems, recv_sems, bufs, recv1_l in in_flight:
        recv2_l = chip_exchange_wait(f"rs_chip_wait_l{l}", bufs, send_sems, recv_sems, after)
        for i, n in enumerate(BIG):
            outs[n] = grad_sum_adamw(g32[i], recv1_l[i], recv2_l[i], W[n], M[n], V[n], place, l, outs[n])
        after = [outs[BIG[-1]][0]]
    out_g, out_d, out_m, out_v = ({n: outs[n][k] for n in BIG} for k in range(4))

    gs = grad_sum(gpack, recv1_s, recv2_s, place)
    gs_all = all_gather("all_gather_small_grads", [gs.reshape((1,) + gs.shape)])[0]
    gs_all = gs_all.reshape(-1, 128)
    small_g = dict(zip(SMALL, _unpack(gs_all, small_shapes)))
    full_shapes = [W[n].shape if n != "conv_w" else small_shapes[SMALL.index("conv_w")] for n in SMALL]
    rep = [n for n in SMALL if n != "conv_w"]
    rep_shapes = [W[n].shape for n in rep]
    wp, mp, vp = (_pack([S[n] for n in rep]) for S in (W, M, V))
    gp = _pack([small_g[n] for n in rep])
    dp, nmp, nvp = adamw(wp, gp, mp, vp)
    for S, packed in ((out_d, dp), (out_m, nmp), (out_v, nvp)):
        S.update(zip(rep, _unpack(packed, rep_shapes)))
    for n in rep:
        out_g[n] = small_g[n]
    cwc = conv_w.shape[2]
    gcw = lax.dynamic_slice_in_dim(small_g["conv_w"], dev * cwc, cwc, axis=2)
    cw2 = lambda a: a.reshape(-1, cwc)
    pad_rows = (-cw2(conv_w).shape[0]) % 8
    padr = lambda a: jnp.pad(cw2(a), ((0, pad_rows), (0, 0)))
    dcw_, mcw_, vcw_ = adamw(padr(conv_w), padr(gcw), padr(M["conv_w"]), padr(V["conv_w"]))
    nrow = cw2(conv_w).shape[0]
    out_g["conv_w"] = gcw
    out_d["conv_w"], out_m["conv_w"], out_v["conv_w"] = (a[:nrow].reshape(conv_w.shape) for a in (dcw_, mcw_, vcw_))
    del full_shapes

    return (loss, grad_x, *[out_g[n] for n in NAMES], *[out_d[n] for n in NAMES], *[out_m[n] for n in NAMES], *[out_v[n] for n in NAMES])
```

```python
import functools

import jax
import jax.numpy as jnp
from jax import lax
from jax.experimental import pallas as pl
from jax.experimental.pallas import tpu as pltpu

F32, BF16 = jnp.float32, jnp.bfloat16
EPS = 1e-6
LRU_C = 8.0
POOL_WINDOWS = (2, 4, 8, 16)
ADAM_LR, ADAM_B1, ADAM_B2, ADAM_EPS, ADAM_WD, ADAM_STEP = 0.001, 0.9, 0.999, 1e-08, 0.01, 10
N_DEV = 8
N_CHIP = 4
MESH = pl.DeviceIdType.MESH
V7X_VMEM_LIMIT = 56 * 1024 * 1024
ROW_TILE = 512
WIDE_TILE = 1024
SUM_TILE = 2048
ANY = pl.BlockSpec(memory_space=pl.ANY)

_pallas_call = pl.pallas_call


def _cp(*sem):
    return pltpu.CompilerParams(dimension_semantics=sem if sem else None, vmem_limit_bytes=V7X_VMEM_LIMIT)


def _tile(n, t):
    t = min(n, t)
    assert n % t == 0, (n, t)
    return t


def _dot(a, b):
    return jnp.dot(a, b, preferred_element_type=F32)


def _dot_nt(a, b):
    return lax.dot_general(a, b, (((1,), (1,)), ((), ())), preferred_element_type=F32)


def _dot_tn(a, b):
    return lax.dot_general(a, b, (((0,), (0,)), ((), ())), preferred_element_type=F32)


def _rms(xv):
    r = lax.rsqrt(jnp.mean(xv * xv, axis=-1, keepdims=True) + EPS)
    return xv * r, r


def _rms_bwd(dh, xv, gv, dy):
    n, r = _rms(xv)
    dn = dh * gv
    dx = dy + r * (dn - n * jnp.mean(dn * n, axis=-1, keepdims=True))
    return dx, jnp.sum(dh * n, axis=0, keepdims=True)


def _shift_down(x, k, fill=0.0):
    if k == 0:
        return x
    rows = lax.broadcasted_iota(jnp.int32, x.shape, 0)
    return jnp.where(rows >= k, pltpu.roll(x, k, 0), fill)


def _shift_up(x, k, fill=0.0):
    if k == 0:
        return x
    n = x.shape[0]
    rows = lax.broadcasted_iota(jnp.int32, x.shape, 0)
    return jnp.where(rows < n - k, pltpu.roll(x, n - k, 0), fill)


def _sigmoid(x):
    return 0.5 * jnp.tanh(0.5 * x) + 0.5


_GELU_K = 0.7978845608028654
_GELU_C = 0.044715


def _gelu(x):
    th = jnp.tanh(_GELU_K * (x + _GELU_C * x * x * x))
    return 0.5 * x * (1.0 + th), th


def _gelu_grad(x, th):
    return 0.5 * (1.0 + th) + 0.5 * x * (1.0 - th * th) * _GELU_K * (1.0 + 3.0 * _GELU_C * x * x)


def ffn_up(x, g, wup, l):
    T, D = x.shape
    cs = wup.shape[-1]
    tm = _tile(T, WIDE_TILE)
    ni = T // tm

    def body(x_ref, g_ref, wa_ref, wb_ref, h_ref, u_ref, s_ref, hs_ref):
        rows = pl.ds(pl.multiple_of(pl.program_id(1) * tm, tm), tm)

        @pl.when(pl.program_id(0) == 0)
        def _():
            n, _r = _rms(x_ref[...])
            hv = (n * g_ref[0]).astype(BF16)
            hs_ref[rows, :] = hv
            h_ref[...] = hv

        hv = hs_ref[rows, :]
        a = _dot(hv, wa_ref[0, 0])
        b = _dot(hv, wb_ref[0, 0])
        u_ref[0, 0] = a.astype(BF16)
        u_ref[1, 0] = b.astype(BF16)
        s_ref[0] = (a * _sigmoid(a) * b).astype(BF16)

    first = lambda j, i: (jnp.where(j == 0, i, ni - 1), 0)
    return _pallas_call(
        body, name="ffn_up", grid=(4, ni),
        in_specs=[pl.BlockSpec((tm, D), first), pl.BlockSpec((1, 1, D), lambda j, i: (l, 0, 0)),
                  pl.BlockSpec((1, 1, D, cs), lambda j, i: (l, j, 0, 0)), pl.BlockSpec((1, 1, D, cs), lambda j, i: (l, j + 4, 0, 0))],
        out_specs=[pl.BlockSpec((tm, D), first), pl.BlockSpec((2, 1, tm, cs), lambda j, i: (0, j, i, 0)),
                   pl.BlockSpec((1, tm, cs), lambda j, i: (j, i, 0))],
        out_shape=[jax.ShapeDtypeStruct((T, D), BF16), jax.ShapeDtypeStruct((2, 4, T, cs), BF16), jax.ShapeDtypeStruct((4, T, cs), BF16)],
        scratch_shapes=[pltpu.VMEM((T, D), BF16)],
        compiler_params=_cp("arbitrary", "arbitrary"),
    )(x, g, wup, wup)


def ffn_down(s, wd, x, l):
    _, T, cs = s.shape
    D = x.shape[1]
    tm = _tile(T, WIDE_TILE)

    def body(s_ref, w_ref, x_ref, o_ref, acc_ref):
        j = pl.program_id(1)

        @pl.when(j == 0)
        def _():
            acc_ref[...] = jnp.zeros_like(acc_ref)

        acc_ref[...] += _dot(s_ref[0], w_ref[0])

        @pl.when(j == 3)
        def _():
            o_ref[...] = x_ref[...] + 0.5 * acc_ref[...]

    return _pallas_call(
        body, name="ffn_down", grid=(T // tm, 4),
        in_specs=[pl.BlockSpec((1, tm, cs), lambda i, j: (j, i, 0)), pl.BlockSpec((1, cs, D), lambda i, j: (l, j, 0)),
                  pl.BlockSpec((tm, D), lambda i, j: (i, 0))],
        out_specs=pl.BlockSpec((tm, D), lambda i, j: (i, 0)),
        out_shape=jax.ShapeDtypeStruct((T, D), F32),
        scratch_shapes=[pltpu.VMEM((tm, D), F32)],
        compiler_params=_cp("parallel", "arbitrary"),
    )(s, wd, x)


def mix_in(x, g, win, l):
    T, D = x.shape
    ci = win.shape[-1]
    tm = _tile(T, WIDE_TILE)
    ni = T // tm

    def body(x_ref, g_ref, w_ref, h_ref, p_ref, hs_ref):
        rows = pl.ds(pl.multiple_of(pl.program_id(1) * tm, tm), tm)

        @pl.when(pl.program_id(0) == 0)
        def _():
            n, _r = _rms(x_ref[...])
            hv = (n * g_ref[0]).astype(BF16)
            hs_ref[rows, :] = hv
            h_ref[...] = hv

        p_ref[...] = _dot(hs_ref[rows, :], w_ref[0, 0]).astype(BF16)

    first = lambda j, i: (jnp.where(j == 0, i, ni - 1), 0)
    return _pallas_call(
        body, name="mix_in", grid=(N_DEV, ni),
        in_specs=[pl.BlockSpec((tm, D), first), pl.BlockSpec((1, 1, D), lambda j, i: (l, 0, 0)),
                  pl.BlockSpec((1, 1, D, ci), lambda j, i: (l, j, 0, 0))],
        out_specs=[pl.BlockSpec((tm, D), first), pl.BlockSpec((tm, ci), lambda j, i: (i, j))],
        out_shape=[jax.ShapeDtypeStruct((T, D), BF16), jax.ShapeDtypeStruct((T, N_DEV * ci), BF16)],
        scratch_shapes=[pltpu.VMEM((T, D), BF16)],
        compiler_params=_cp("arbitrary", "arbitrary"),
    )(x, g, win)


def _inv_count(T, w):
    t = lax.broadcasted_iota(jnp.int32, (T, 1), 0)
    return 1.0 / jnp.minimum(t + 1, w).astype(F32)


def _pooled(ug, w, inv):
    s = ug
    k = 1
    while k < w:
        s = s + _shift_down(s, k)
        k *= 2
    return s * inv - ug


def pool_fwd(proj, pw, pb, ps, l):
    T = proj.shape[0]
    _, G, gd, _ = pw.shape
    P = G * gd

    def body(u_ref, w_ref, b_ref, s_ref, o_ref):
        for gi in range(G):
            cols = slice(gi * gd, (gi + 1) * gd)
            ug = u_ref[:, cols].astype(F32)
            pooled = _pooled(ug, POOL_WINDOWS[gi], _inv_count(T, POOL_WINDOWS[gi]))
            mixed = _dot(pooled.astype(BF16), w_ref[0, gi].astype(BF16)) + b_ref[0, :, cols]
            o_ref[:, cols] = (mixed * s_ref[0, :, cols]).astype(BF16)

    return _pallas_call(
        body, name="pool_fwd", grid=(1,),
        in_specs=[pl.BlockSpec((T, P), lambda i: (0, 0)), pl.BlockSpec((1, G, gd, gd), lambda i: (l, 0, 0, 0)),
                  pl.BlockSpec((1, 1, P), lambda i: (l, 0, 0)), pl.BlockSpec((1, 1, P), lambda i: (l, 0, 0))],
        out_specs=pl.BlockSpec((T, P), lambda i: (0, 0)),
        out_shape=jax.ShapeDtypeStruct((T, P), BF16),
        compiler_params=_cp("arbitrary"),
    )(proj, pw, pb, ps)


def _conv(u, cw_ref, cb):
    CW = cw_ref.shape[1]
    v = cb
    for k in range(CW):
        v = v + cw_ref[0, k:k + 1, :] * _shift_down(u, CW - 1 - k)
    return v


def _softplus(z):
    return jnp.maximum(z, 0.0) + jnp.log1p(jnp.exp(-jnp.abs(z)))


def _lru_gates(v, wa_ref, ba, wx_ref, bx, lam):
    vb = v.astype(BF16)
    r = _sigmoid(_dot(vb, wa_ref[0, 0].astype(BF16)) + ba)
    i = _sigmoid(_dot(vb, wx_ref[0, 0].astype(BF16)) + bx)
    sp = _softplus(-lam)
    log_a = -LRU_C * r * sp
    a = jnp.exp(log_a)
    m2 = -jnp.tanh(log_a) * (a * a + 1.0)
    inv_mult = lax.rsqrt(m2)
    mult = jnp.where(m2 > 0.0, m2 * inv_mult, 0.0)
    return r, i, sp, a, mult, inv_mult


def _scan_fwd(a_ref, b_ref, o_ref):
    T, W = a_ref.shape
    rows = lax.broadcasted_iota(jnp.int32, (8, W), 0)

    def step(t, carry):
        r0 = pl.multiple_of(t * 8, 8)
        A = a_ref[pl.ds(r0, 8), :]
        B = b_ref[pl.ds(r0, 8), :]
        for s in (1, 2, 4):
            keep = rows >= s
            As = jnp.where(keep, pltpu.roll(A, s, 0), 1.0)
            Bs = jnp.where(keep, pltpu.roll(B, s, 0), 0.0)
            B = A * Bs + B
            A = A * As
        h = B + A * carry
        o_ref[pl.ds(r0, 8), :] = h
        return jnp.broadcast_to(h[7:8, :], (8, W))

    lax.fori_loop(0, T // 8, step, jnp.zeros((8, W), F32), unroll=8)


def _scan_bwd(a_ref, b_ref, o_ref):
    T, W = a_ref.shape
    rows = lax.broadcasted_iota(jnp.int32, (8, W), 0)
    nt = T // 8

    def step(t, carry):
        r0 = pl.multiple_of((nt - 1 - t) * 8, 8)
        A = a_ref[pl.ds(r0, 8), :]
        B = b_ref[pl.ds(r0, 8), :]
        for s in (1, 2, 4):
            keep = rows < 8 - s
            As = jnp.where(keep, pltpu.roll(A, 8 - s, 0), 1.0)
            Bs = jnp.where(keep, pltpu.roll(B, 8 - s, 0), 0.0)
            B = A * Bs + B
            A = A * As
        y = B + A * carry
        o_ref[pl.ds(r0, 8), :] = y
        return jnp.broadcast_to(y[0:1, :], (8, W))

    lax.fori_loop(0, nt, step, jnp.zeros((8, W), F32), unroll=8)


def _lru_specs(T, hd, P, R, CW, l):
    ob, gb = P // hd, (P + R) // hd
    vec = pl.BlockSpec((1, 1, hd), lambda h: (l, 0, h))
    mat = pl.BlockSpec((1, 1, hd, hd), lambda h: (l, h, 0, 0))
    return [pl.BlockSpec((T, hd), lambda h: (0, ob + h)), pl.BlockSpec((T, hd), lambda h: (0, gb + h)),
            pl.BlockSpec((1, CW, hd), lambda h: (l, 0, h)), vec, mat, vec, mat, vec, vec]


def lru_fwd(proj, cw, cb, wa, ba, wx, bx, lam, P, l):
    T = proj.shape[0]
    _, H, hd, _ = wa.shape
    R = H * hd
    CW = cw.shape[1]
    assert P % hd == 0 and T % 8 == 0

    def body(u_ref, ug_ref, cw_ref, cb_ref, wa_ref, ba_ref, wx_ref, bx_ref, lam_ref, hl_ref, hs_ref, a_s, b_s):
        v = _conv(u_ref[...].astype(F32), cw_ref, cb_ref[0])
        _r, i, _sp, a, mult, _im = _lru_gates(v, wa_ref, ba_ref[0], wx_ref, bx_ref[0], lam_ref[0])
        a_s[...] = a
        b_s[...] = mult * (i * v)
        _scan_fwd(a_s, b_s, hs_ref)
        ge, _th = _gelu(ug_ref[...].astype(F32))
        hl_ref[...] = (hs_ref[...] * ge).astype(BF16)

    out = pl.BlockSpec((T, hd), lambda h: (0, h))
    return _pallas_call(
        body, name="lru_fwd", grid=(H,),
        in_specs=_lru_specs(T, hd, P, R, CW, l),
        out_specs=[out, out],
        out_shape=[jax.ShapeDtypeStruct((T, R), BF16), jax.ShapeDtypeStruct((T, R), F32)],
        scratch_shapes=[pltpu.VMEM((T, hd), F32)] * 2,
        compiler_params=_cp("parallel"),
    )(proj, proj, cw, cb, wa, ba, wx, bx, lam)


def mix_out(pm, hl, proj, x, wpu, wlu, wout, P, l):
    T, D = x.shape
    R = hl.shape[1]
    tm = _tile(T, ROW_TILE)
    assert (P + 2 * R) % D == 0
    gb = (P + 2 * R) // D

    def body(pm_ref, hl_ref, gp_ref, gl_ref, x_ref, wpu_ref, wlu_ref, wo_ref, o_ref, yp_ref, yl_ref, z_ref):
        yp = _dot(pm_ref[...], wpu_ref[0])
        yl = _dot(hl_ref[...], wlu_ref[0])
        z = (_sigmoid(gp_ref[...].astype(F32)) * yp + _sigmoid(gl_ref[...].astype(F32)) * yl).astype(BF16)
        yp_ref[...] = yp.astype(BF16)
        yl_ref[...] = yl.astype(BF16)
        z_ref[...] = z
        o_ref[...] = x_ref[...] + _dot(z, wo_ref[0])

    row = lambda w: pl.BlockSpec((tm, w), lambda i: (i, 0))
    return _pallas_call(
        body, name="mix_out", grid=(T // tm,),
        in_specs=[row(P), row(R), pl.BlockSpec((tm, D), lambda i: (i, gb)), pl.BlockSpec((tm, D), lambda i: (i, gb + 1)), row(D),
                  pl.BlockSpec((1, P, D), lambda i: (l, 0, 0)), pl.BlockSpec((1, R, D), lambda i: (l, 0, 0)),
                  pl.BlockSpec((1, D, D), lambda i: (l, 0, 0))],
        out_specs=[row(D)] * 4,
        out_shape=[jax.ShapeDtypeStruct((T, D), F32)] + [jax.ShapeDtypeStruct((T, D), BF16)] * 3,
        compiler_params=_cp("parallel"),
    )(pm, hl, proj, proj, x, wpu, wlu, wout)


def loss_head(x, gf, tgt):
    T, D = x.shape
    tm = _tile(T, ROW_TILE)

    def body(x_ref, g_ref, t_ref, loss_ref, dx_ref, dg_ref):
        @pl.when(pl.program_id(0) == 0)
        def _():
            loss_ref[...] = jnp.zeros_like(loss_ref)
            dg_ref[...] = jnp.zeros_like(dg_ref)

        xv = x_ref[...]
        gv = g_ref[...]
        n, _r = _rms(xv)
        e = n * gv - t_ref[...]
        loss_ref[...] += 0.5 * jnp.sum(jnp.sum(e * e, axis=-1, keepdims=True), axis=0, keepdims=True) / D
        dx, dg = _rms_bwd(e * (1.0 / D), xv, gv, 0.0)
        dx_ref[...] = dx
        dg_ref[...] += dg

    return _pallas_call(
        body, name="loss_head", grid=(T // tm,),
        in_specs=[pl.BlockSpec((tm, D), lambda i: (i, 0)), pl.BlockSpec((1, D), lambda i: (0, 0)), pl.BlockSpec((tm, D), lambda i: (i, 0))],
        out_specs=[pl.BlockSpec((1, 1), lambda i: (0, 0)), pl.BlockSpec((tm, D), lambda i: (i, 0)), pl.BlockSpec((1, D), lambda i: (0, 0))],
        out_shape=[jax.ShapeDtypeStruct((1, 1), F32), jax.ShapeDtypeStruct((T, D), F32), jax.ShapeDtypeStruct((1, D), F32)],
        compiler_params=_cp("arbitrary"),
    )(x, gf, tgt)


def ffn_down_bwd(dy, wd, u, l, deps=()):
    T, D = dy.shape
    cs = u.shape[-1]
    tm = _tile(T, WIDE_TILE)
    ni = T // tm

    def body(dy_ref, w_ref, u_ref, *rest):
        do_ref, du_ref, dyb_ref = rest[len(deps):]
        rows = pl.ds(pl.multiple_of(pl.program_id(1) * tm, tm), tm)

        @pl.when(pl.program_id(0) == 0)
        def _():
            d = (0.5 * dy_ref[...]).astype(BF16)
            dyb_ref[rows, :] = d
            do_ref[...] = d

        ds = _dot_nt(dyb_ref[rows, :], w_ref[0])
        a = u_ref[0, 0].astype(F32)
        b = u_ref[1, 0].astype(F32)
        sg = _sigmoid(a)
        du_ref[0, 0] = (ds * b * (sg * (1.0 + a * (1.0 - sg)))).astype(BF16)
        du_ref[1, 0] = (ds * (a * sg)).astype(BF16)

    first = lambda j, i: (jnp.where(j == 0, i, ni - 1), 0)
    blk = pl.BlockSpec((2, 1, tm, cs), lambda j, i: (0, j, i, 0))
    return _pallas_call(
        body, name="ffn_down_bwd", grid=(4, ni),
        in_specs=[pl.BlockSpec((tm, D), first), pl.BlockSpec((1, cs, D), lambda j, i: (l, j, 0)), blk] + [ANY] * len(deps),
        out_specs=[pl.BlockSpec((tm, D), first), blk],
        out_shape=[jax.ShapeDtypeStruct((T, D), BF16), jax.ShapeDtypeStruct((2, 4, T, cs), BF16)],
        scratch_shapes=[pltpu.VMEM((T, D), BF16)],
        compiler_params=_cp("arbitrary", "arbitrary"),
    )(dy, wd, u, *deps)


def dw_tn(name, a_ops, a_spec, b_ops, b_spec, G, M, N, T, l, L, prev):
    tk = _tile(T, SUM_TILE)
    nk = T // tk

    def body(*refs):
        a_refs, b_refs = refs[:len(a_ops)], refs[len(a_ops):len(a_ops) + len(b_ops)]
        o32_ref, o16_ref, acc_ref = refs[-3:]
        k = pl.program_id(1)

        @pl.when(k == 0)
        def _():
            acc_ref[...] = jnp.zeros_like(acc_ref)

        av = a_refs[0][0] if len(a_refs[0].shape) == 3 else a_refs[0][...]
        bv = b_refs[0][0] if len(b_refs[0].shape) == 3 else b_refs[0][...]
        acc_ref[...] += _dot_tn(av, bv)

        @pl.when(k == nk - 1)
        def _():
            o32_ref[0, 0] = acc_ref[...]
            o16_ref[0, 0] = acc_ref[...].astype(BF16)

    n_in = len(a_ops) + len(b_ops)
    in_specs = [a_spec(tk), b_spec(tk)]
    args = list(a_ops) + list(b_ops)
    aliases = {}
    if prev is not None:
        in_specs += [ANY, ANY]
        args += list(prev)
        aliases = {n_in: 0, n_in + 1: 1}

    def body_wrap(*refs):
        if prev is not None:
            refs = refs[:n_in] + refs[n_in + 2:]
        body(*refs)

    out = pl.BlockSpec((1, 1, M, N), lambda g, k: (l, g, 0, 0))
    return _pallas_call(
        body_wrap, name=name, grid=(G, nk),
        in_specs=in_specs, out_specs=[out, out],
        out_shape=[jax.ShapeDtypeStruct((L, G, M, N), F32), jax.ShapeDtypeStruct((L, G, M, N), BF16)],
        scratch_shapes=[pltpu.VMEM((M, N), F32)],
        input_output_aliases=aliases,
        compiler_params=_cp("parallel", "arbitrary"),
    )(*args)


def dx_norm_bwd(name, dact, d_spec, w, G, x, g, dy, l):
    T, D = x.shape
    c = w.shape[-1]
    tm = _tile(T, WIDE_TILE)
    ni = T // tm
    ch = _tile(tm, ROW_TILE // 2)

    def body(d_ref, w_ref, x_ref, g_ref, dy_ref, dx_ref, dg_ref, acc_ref):
        j, i = pl.program_id(0), pl.program_id(1)
        rows = pl.ds(pl.multiple_of(i * tm, tm), tm)

        @pl.when(jnp.logical_and(i == 0, j == 0))
        def _():
            dg_ref[...] = jnp.zeros_like(dg_ref)

        dv = d_ref[0] if len(d_ref.shape) == 3 else d_ref[...]
        part = _dot_nt(dv, w_ref[0, 0])

        @pl.when(j == 0)
        def _():
            acc_ref[rows, :] = part

        @pl.when(j > 0)
        def _():
            acc_ref[rows, :] += part

        @pl.when(j == G - 1)
        def _():
            dg = jnp.zeros((1, D), F32)
            for c0 in range(0, tm, ch):
                part_rows = pl.ds(pl.multiple_of(i * tm + c0, ch), ch)
                dx, dgc = _rms_bwd(acc_ref[part_rows, :], x_ref[c0:c0 + ch, :], g_ref[0], dy_ref[c0:c0 + ch, :])
                dx_ref[c0:c0 + ch, :] = dx
                dg = dg + dgc
            dg_ref[...] += dg

    last = pl.BlockSpec((tm, D), lambda j, i: (jnp.where(j == G - 1, i, 0), 0))
    return _pallas_call(
        body, name=name, grid=(G, ni),
        in_specs=[d_spec(tm), pl.BlockSpec((1, 1, D, c), lambda j, i: (l, j, 0, 0)), last, pl.BlockSpec((1, 1, D), lambda j, i: (l, 0, 0)), last],
        out_specs=[last, pl.BlockSpec((1, D), lambda j, i: (0, 0))],
        out_shape=[jax.ShapeDtypeStruct((T, D), F32), jax.ShapeDtypeStruct((1, D), F32)],
        scratch_shapes=[pltpu.VMEM((T, D), F32)],
        compiler_params=_cp("arbitrary", "arbitrary"),
    )(dact, w, x, g, dy)


def mix_out_bwd(dy, proj, yp, yl, wpu, wlu, wout, P, R, l, deps=()):
    T, D = dy.shape
    tm = _tile(T, ROW_TILE)
    gb = (P + 2 * R) // D

    def body(dy_ref, gp_ref, gl_ref, yp_ref, yl_ref, wpu_ref, wlu_ref, wo_ref, *rest):
        dyb_ref, dyp_ref, dyl_ref, dgp_ref, dgl_ref, dpm_ref, dhl_ref = rest[len(deps):]
        dyb = dy_ref[...].astype(BF16)
        dyb_ref[...] = dyb
        dz = _dot_nt(dyb, wo_ref[0])
        sp = _sigmoid(gp_ref[...].astype(F32))
        sl = _sigmoid(gl_ref[...].astype(F32))
        dgp_ref[...] = (dz * yp_ref[...].astype(F32) * sp * (1.0 - sp)).astype(BF16)
        dgl_ref[...] = (dz * yl_ref[...].astype(F32) * sl * (1.0 - sl)).astype(BF16)
        dyp = (dz * sp).astype(BF16)
        dyl = (dz * sl).astype(BF16)
        dyp_ref[...] = dyp
        dyl_ref[...] = dyl
        dpm_ref[...] = _dot_nt(dyp, wpu_ref[0]).astype(BF16)
        dhl_ref[...] = _dot_nt(dyl, wlu_ref[0]).astype(BF16)

    row = lambda w: pl.BlockSpec((tm, w), lambda i: (i, 0))
    return _pallas_call(
        body, name="mix_out_bwd", grid=(T // tm,),
        in_specs=[row(D), pl.BlockSpec((tm, D), lambda i: (i, gb)), pl.BlockSpec((tm, D), lambda i: (i, gb + 1)), row(D), row(D),
                  pl.BlockSpec((1, P, D), lambda i: (l, 0, 0)), pl.BlockSpec((1, R, D), lambda i: (l, 0, 0)),
                  pl.BlockSpec((1, D, D), lambda i: (l, 0, 0))] + [ANY] * len(deps),
        out_specs=[row(D)] * 5 + [row(P), row(R)],
        out_shape=[jax.ShapeDtypeStruct((T, D), BF16)] * 5 + [jax.ShapeDtypeStruct((T, P), BF16), jax.ShapeDtypeStruct((T, R), BF16)],
        compiler_params=_cp("parallel"),
    )(dy, proj, proj, yp, yl, wpu, wlu, wout, *deps)


def lru_bwd(proj, hs, dhl, cw, cb, wa, ba, wx, bx, lam, P, l):
    T = proj.shape[0]
    _, H, hd, _ = wa.shape
    R = H * hd
    CW = cw.shape[1]

    def body(u_ref, ug_ref, cw_ref, cb_ref, wa_ref, ba_ref, wx_ref, bx_ref, lam_ref, hs_ref, dhl_ref,
             du_ref, dug_ref, dcw_ref, dcb_ref, dwa_ref, dba_ref, dwx_ref, dbx_ref, dlam_ref, c_s, g_s, y_s):
        u = u_ref[...].astype(F32)
        v = _conv(u, cw_ref, cb_ref[0])
        lam = lam_ref[0]
        r, i, sp, a, mult, inv_mult = _lru_gates(v, wa_ref, ba_ref[0], wx_ref, bx_ref[0], lam)
        ug = ug_ref[...].astype(F32)
        ge, th = _gelu(ug)
        hs = hs_ref[...]
        dhl = dhl_ref[...].astype(F32)
        dug_ref[...] = (dhl * hs * _gelu_grad(ug, th)).astype(BF16)
        c_s[...] = _shift_up(a, 1)
        g_s[...] = dhl * ge
        _scan_bwd(c_s, g_s, y_s)
        y = y_s[...]
        da = y * _shift_down(hs, 1)
        iv = i * v
        dlog_a = da * a - (y * iv) * (a * a) * inv_mult
        div = y * mult
        dpa = (dlog_a * (-LRU_C) * sp) * r * (1.0 - r)
        dpx = (div * v) * i * (1.0 - i)
        dsp = jnp.sum(dlog_a * (-LRU_C) * r, axis=0, keepdims=True)
        dlam_ref[0] = -dsp * _sigmoid(-lam)
        vb = v.astype(BF16)
        dpab, dpxb = dpa.astype(BF16), dpx.astype(BF16)
        dwa_ref[0, 0] = _dot_tn(vb, dpab)
        dwx_ref[0, 0] = _dot_tn(vb, dpxb)
        dba_ref[0] = jnp.sum(dpa, axis=0, keepdims=True)
        dbx_ref[0] = jnp.sum(dpx, axis=0, keepdims=True)
        dv = div * i + _dot_nt(dpab, wa_ref[0, 0].astype(BF16)) + _dot_nt(dpxb, wx_ref[0, 0].astype(BF16))
        dcb_ref[0] = jnp.sum(dv, axis=0, keepdims=True)
        du = jnp.zeros_like(dv)
        for k in range(CW):
            du = du + cw_ref[0, k:k + 1, :] * _shift_up(dv, CW - 1 - k)
            dcw_ref[0, k:k + 1, :] = jnp.sum(dv * _shift_down(u, CW - 1 - k), axis=0, keepdims=True)
        du_ref[...] = du.astype(BF16)

    col = pl.BlockSpec((T, hd), lambda h: (0, h))
    vec = pl.BlockSpec((1, 1, hd), lambda h: (0, 0, h))
    mat = pl.BlockSpec((1, 1, hd, hd), lambda h: (0, h, 0, 0))
    vshape = jax.ShapeDtypeStruct((1, 1, R), F32)
    mshape = jax.ShapeDtypeStruct((1, H, hd, hd), F32)
    return _pallas_call(
        body, name="lru_bwd", grid=(H,),
        in_specs=_lru_specs(T, hd, P, R, CW, l) + [col, col],
        out_specs=[col, col, pl.BlockSpec((1, CW, hd), lambda h: (0, 0, h)), vec, mat, vec, mat, vec, vec],
        out_shape=[jax.ShapeDtypeStruct((T, R), BF16)] * 2 + [jax.ShapeDtypeStruct((1, CW, R), F32), vshape, mshape, vshape, mshape, vshape, vshape],
        scratch_shapes=[pltpu.VMEM((T, hd), F32)] * 3,
        compiler_params=_cp("parallel"),
    )(proj, proj, cw, cb, wa, ba, wx, bx, lam, hs, dhl)


def pool_bwd(proj, dpm, pw, pb, ps, l):
    T = proj.shape[0]
    _, G, gd, _ = pw.shape
    P = G * gd

    def body(u_ref, d_ref, w_ref, b_ref, s_ref, du_ref, dw_ref, db_ref, dsc_ref):
        for gi in range(G):
            cols = slice(gi * gd, (gi + 1) * gd)
            w = POOL_WINDOWS[gi]
            inv = _inv_count(T, w)
            ug = u_ref[:, cols].astype(F32)
            pooled = _pooled(ug, w, inv).astype(BF16)
            wb = w_ref[0, gi].astype(BF16)
            mixed = _dot(pooled, wb) + b_ref[0, :, cols]
            dpm_g = d_ref[:, cols].astype(F32)
            dsc_ref[0, :, cols] = jnp.sum(dpm_g * mixed, axis=0, keepdims=True)
            dmixed = dpm_g * s_ref[0, :, cols]
            db_ref[0, :, cols] = jnp.sum(dmixed, axis=0, keepdims=True)
            dmb = dmixed.astype(BF16)
            dw_ref[0, gi] = _dot_tn(pooled, dmb)
            dpooled = _dot_nt(dmb, wb)
            s = dpooled * inv
            k = 1
            while k < w:
                s = s + _shift_up(s, k)
                k *= 2
            du_ref[:, cols] = (s - dpooled).astype(BF16)

    vec = pl.BlockSpec((1, 1, P), lambda i: (l, 0, 0))
    ovec = pl.BlockSpec((1, 1, P), lambda i: (0, 0, 0))
    return _pallas_call(
        body, name="pool_bwd", grid=(1,),
        in_specs=[pl.BlockSpec((T, P), lambda i: (0, 0)), pl.BlockSpec((T, P), lambda i: (0, 0)),
                  pl.BlockSpec((1, G, gd, gd), lambda i: (l, 0, 0, 0)), vec, vec],
        out_specs=[pl.BlockSpec((T, P), lambda i: (0, 0)), pl.BlockSpec((1, G, gd, gd), lambda i: (0, 0, 0, 0)), ovec, ovec],
        out_shape=[jax.ShapeDtypeStruct((T, P), BF16), jax.ShapeDtypeStruct((1, G, gd, gd), F32),
                   jax.ShapeDtypeStruct((1, 1, P), F32), jax.ShapeDtypeStruct((1, 1, P), F32)],
        compiler_params=_cp("arbitrary"),
    )(proj, dpm, pw, pb, ps)


def _place():
    x, y, c = lax.axis_index("x"), lax.axis_index("y"), lax.axis_index("c")
    return x, y, c


def all_gather(name, shards):
    n = len(shards)

    def body(*refs):
        src, out = refs[:n], refs[n:2 * n]
        send_sems, recv_sems, local_sems = refs[2 * n:]
        x, y, c = _place()
        sibling = (x, y, 1 - c)
        chips = [(x, 1 - y), (1 - x, y), (1 - x, 1 - y)]

        def slot(a, px, py, pc):
            return out[a].at[:, 4 * px + 2 * py + pc]

        def copy(a, k, block, to, from_src=False):
            return pltpu.make_async_remote_copy(
                src_ref=src[a] if from_src else slot(a, *block), dst_ref=slot(a, *block),
                send_sem=send_sems.at[a, k], recv_sem=recv_sems.at[a, k], device_id=to, device_id_type=MESH)

        me = (x, y, c)
        mine = [pltpu.make_async_copy(src[a], slot(a, *me), local_sems.at[a]) for a in range(n)]
        first = []
        for j, chip in enumerate(chips):
            for a in range(n):
                first.append(copy(a, 1 + j, me, (*chip, c), from_src=True))
        for a in range(n):
            first.append(copy(a, 0, me, sibling, from_src=True))
        for cp in mine + first:
            cp.start()
        passed = []
        for j, chip in enumerate(chips):
            for a in range(n):
                copy(a, 1 + j, (*chip, c), me).wait_recv()
                fwd = copy(a, 4 + j, (*chip, c), sibling)
                fwd.start()
                passed.append(fwd)
        for a in range(n):
            copy(a, 0, (x, y, 1 - c), me).wait_recv()
        for j, chip in enumerate(chips):
            for a in range(n):
                copy(a, 4 + j, (*chip, 1 - c), me).wait_recv()
        for cp in first + passed:
            cp.wait_send()
        for cp in mine:
            cp.wait()

    outs = _pallas_call(
        body, name=name,
        in_specs=[ANY] * n, out_specs=[ANY] * n,
        out_shape=[jax.ShapeDtypeStruct((s.shape[0], N_DEV) + s.shape[1:], s.dtype) for s in shards],
        scratch_shapes=[pltpu.SemaphoreType.DMA((n, 7)), pltpu.SemaphoreType.DMA((n, 7)), pltpu.SemaphoreType.DMA((n,))],
        compiler_params=pltpu.CompilerParams(has_side_effects=True),
    )(*shards)
    return list(outs)


def pair_exchange(name, g16, layer=None):
    n = len(g16)
    layers = slice(None) if layer is None else pl.ds(layer, 1)

    def body(*refs):
        s16, recv = refs[:n], refs[n:2 * n]
        send_sems, recv_sems = refs[2 * n:]
        x, y, c = _place()
        sibling = (x, y, 1 - c)
        rem = []
        for a in range(n):
            for j in range(N_CHIP):
                rem.append(pltpu.make_async_remote_copy(
                    src_ref=s16[a].at[layers, 2 * j + 1 - c], dst_ref=recv[a].at[:, j],
                    send_sem=send_sems.at[a, j], recv_sem=recv_sems.at[a, j], device_id=sibling, device_id_type=MESH))
        for cp in rem:
            cp.start()
        for cp in rem:
            cp.wait_recv()
        for cp in rem:
            cp.wait_send()

    outs = _pallas_call(
        body, name=name,
        in_specs=[ANY] * n, out_specs=[ANY] * n,
        out_shape=[jax.ShapeDtypeStruct((s.shape[0] if layer is None else 1, N_CHIP) + s.shape[2:], s.dtype) for s in g16],
        scratch_shapes=[pltpu.SemaphoreType.DMA((n, N_CHIP))] * 2,
        compiler_params=pltpu.CompilerParams(has_side_effects=True),
    )(*g16)
    return list(outs)


def chip_exchange(name, pair16):
    n = len(pair16)

    def body(*refs):
        p16, recv2 = refs[:n], refs[n:2 * n]
        send_sems, recv_sems = refs[2 * n:]
        x, y, c = _place()
        rem = []
        for d in (1, 2, 3):
            px = 1 - x if d & 2 else x
            py = 1 - y if d & 1 else y
            for a in range(n):
                rem.append(pltpu.make_async_remote_copy(
                    src_ref=p16[a].at[:, 2 * px + py], dst_ref=recv2[a].at[:, d - 1],
                    send_sem=send_sems.at[a, d - 1], recv_sem=recv_sems.at[a, d - 1], device_id=(px, py, c), device_id_type=MESH))
        for cp in rem:
            cp.start()
        for cp in rem:
            cp.wait_recv()
        for cp in rem:
            cp.wait_send()

    outs = _pallas_call(
        body, name=name,
        in_specs=[ANY] * n, out_specs=[ANY] * n,
        out_shape=[jax.ShapeDtypeStruct((s.shape[0], 3) + s.shape[2:], s.dtype) for s in pair16],
        scratch_shapes=[pltpu.SemaphoreType.DMA((n, 3))] * 2,
        compiler_params=pltpu.CompilerParams(has_side_effects=True),
    )(*pair16)
    return list(outs)


HBM = pl.BlockSpec(memory_space=pltpu.HBM)
SEM = pl.BlockSpec(memory_space=pltpu.SEMAPHORE)
EFFECT = pltpu.SideEffectType.DATAFLOW_SIDE_EFFECTING


def _in_hbm(a):
    return pltpu.with_memory_space_constraint(a, pltpu.HBM)


def split_start(name, bufs, n_copies, copies_of):
    nb = len(bufs)

    def body(*refs):
        buf = refs[:nb]
        send_sems, recv_sems = refs[nb], refs[nb + 1]
        token = refs[-1]
        for i, (src, dst, dev) in enumerate(copies_of(buf)):
            pltpu.make_async_remote_copy(src_ref=src, dst_ref=dst, send_sem=send_sems.at[i], recv_sem=recv_sems.at[i],
                                         device_id=dev, device_id_type=MESH).start()
        token[...] = jnp.zeros_like(token)

    outs = _pallas_call(
        body, name=name,
        in_specs=[HBM] * nb,
        out_specs=(SEM, SEM, *([HBM] * nb), pl.BlockSpec(memory_space=pltpu.VMEM)),
        out_shape=(pltpu.SemaphoreType.DMA((n_copies,)), pltpu.SemaphoreType.DMA((n_copies,)),
                   *[pltpu.HBM(b.shape, b.dtype) for b in bufs], jax.ShapeDtypeStruct((8, 128), F32)),
        input_output_aliases={i: 2 + i for i in range(nb)},
        compiler_params=pltpu.CompilerParams(has_side_effects=EFFECT),
    )(*[_in_hbm(b) for b in bufs])
    return outs[0], outs[1], list(outs[2:2 + nb]), outs[-1]


def split_wait(name, bufs, send_sems, recv_sems, after, copies_of):
    nb = len(bufs)

    def body(*refs):
        buf = refs[:nb]
        send, recv = refs[nb], refs[nb + 1]
        for i, (src, dst, dev) in enumerate(copies_of(buf)):
            cp = pltpu.make_async_remote_copy(src_ref=src, dst_ref=dst, send_sem=send.at[i], recv_sem=recv.at[i],
                                              device_id=dev, device_id_type=MESH)
            cp.wait_send()
            cp.wait_recv()

    outs = _pallas_call(
        body, name=name,
        in_specs=[HBM] * nb + [SEM, SEM] + [ANY] * len(after),
        out_specs=[HBM] * nb,
        out_shape=[pltpu.HBM(b.shape, b.dtype) for b in bufs],
        input_output_aliases={i: i for i in range(nb)},
        compiler_params=pltpu.CompilerParams(has_side_effects=EFFECT),
    )(*bufs, send_sems, recv_sems, *after)
    return list(outs)


def place_own(w, l, place, dtype):
    _, rows, cols = w.shape
    tr = _rows_tile(rows, cols, 1 << 19)

    def body(p_ref, w_ref, o_ref):
        o_ref[0] = w_ref[0].astype(dtype)

    return _pallas_call(
        body, name="place_own",
        grid_spec=pltpu.PrefetchScalarGridSpec(
            num_scalar_prefetch=1, grid=(rows // tr,),
            in_specs=[pl.BlockSpec((1, tr, cols), lambda i, p: (l, i, 0))],
            out_specs=pl.BlockSpec((1, tr, cols), lambda i, p: (p[2], i, 0))),
        out_shape=jax.ShapeDtypeStruct((N_DEV, rows, cols), dtype), compiler_params=_cp("parallel"),
    )(place, w)


def _gather_copies(land):
    x, y, c = _place()
    k = 4 * x + 2 * y + c
    peers = [(x, 1 - y, c), (1 - x, y, c), (1 - x, 1 - y, c), (x, y, 1 - c)]
    return [(b.at[k], b.at[k], p) for p in peers for b in land]


def gather_start(name, land):
    return split_start(name, land, 4 * len(land), _gather_copies)


def gather_wait(name, land, send_sems, recv_sems, after):
    return split_wait(name, land, send_sems, recv_sems, after, _gather_copies)


def gather_forward(name, land):
    n = len(land)

    def body(*refs):
        buf = refs[:n]
        send_sems, recv_sems = refs[2 * n:]
        x, y, c = _place()
        rem = []
        for j, (px, py) in enumerate([(x, 1 - y), (1 - x, y), (1 - x, 1 - y)]):
            k = 4 * px + 2 * py + c
            for a in range(n):
                rem.append(pltpu.make_async_remote_copy(
                    src_ref=buf[a].at[k], dst_ref=buf[a].at[k], send_sem=send_sems.at[a, j], recv_sem=recv_sems.at[a, j],
                    device_id=(x, y, 1 - c), device_id_type=MESH))
        for cp in rem:
            cp.start()
        for cp in rem:
            cp.wait_recv()
        for cp in rem:
            cp.wait_send()

    outs = _pallas_call(
        body, name=name,
        in_specs=[ANY] * n, out_specs=[ANY] * n,
        out_shape=[jax.ShapeDtypeStruct(b.shape, b.dtype) for b in land],
        scratch_shapes=[pltpu.SemaphoreType.DMA((n, 3))] * 2,
        input_output_aliases={i: i for i in range(n)},
        compiler_params=pltpu.CompilerParams(has_side_effects=True),
    )(*land)
    return list(outs)


def _chip_copies(nsrc):
    def copies(buf):
        p16, recv2 = buf[:nsrc], buf[nsrc:]
        x, y, c = _place()
        out = []
        for d in (1, 2, 3):
            px = 1 - x if d & 2 else x
            py = 1 - y if d & 1 else y
            out += [(p16[a].at[:, 2 * px + py], recv2[a].at[:, d - 1], (px, py, c)) for a in range(nsrc)]
        return out
    return copies


def chip_exchange_start(name, pair16):
    n = len(pair16)
    land = [lax.empty((s.shape[0], 3) + s.shape[2:], s.dtype) for s in pair16]
    return split_start(name, list(pair16) + land, 3 * n, _chip_copies(n))


def chip_exchange_wait(name, bufs, send_sems, recv_sems, after):
    n = len(bufs) // 2
    return split_wait(name, bufs, send_sems, recv_sems, after, _chip_copies(n))[n:]


def _rows_tile(rows, cols, budget=1 << 20):
    t = rows
    while t % 2 == 0 and t * cols > budget and (t // 2) % 16 == 0:
        t //= 2
    return t


def pair_sum(g32, recv1, place, l):
    _, _, rows, cols = recv1.shape
    tr = _rows_tile(rows, cols)

    def body(p_ref, m_ref, r_ref, o_ref):
        o_ref[...] = (m_ref[...] + r_ref[...].astype(F32)).astype(o_ref.dtype)

    blk = pl.BlockSpec((1, 1, tr, cols), lambda j, i, p: (0, j, i, 0))
    return _pallas_call(
        body, name="pair_sum",
        grid_spec=pltpu.PrefetchScalarGridSpec(
            num_scalar_prefetch=1, grid=(N_CHIP, rows // tr),
            in_specs=[pl.BlockSpec((1, 1, tr, cols), lambda j, i, p: (l, 2 * j + p[0], i, 0)), blk], out_specs=blk),
        out_shape=jax.ShapeDtypeStruct(recv1.shape, recv1.dtype), compiler_params=_cp("parallel", "parallel"),
    )(place, g32, recv1)


def _grad_in_specs(tr, cols, l):
    return ([pl.BlockSpec((1, 1, tr, cols), lambda i, p: (l, p[2], i, 0)), pl.BlockSpec((1, 1, tr, cols), lambda i, p: (0, p[1], i, 0))]
            + [pl.BlockSpec((1, 1, tr, cols), lambda i, p, d=d: (0, d, i, 0)) for d in range(3)])


def _grad_total(o32, o16, r0, r1, r2):
    return (o32[0, 0] + o16[0, 0].astype(F32)) + r0[0, 0].astype(F32) + r1[0, 0].astype(F32) + r2[0, 0].astype(F32)


def grad_sum(g32, recv1, recv2, place):
    _, _, rows, cols = recv1.shape
    tr = _rows_tile(rows, cols)

    def body(p_ref, o32, o16, r0, r1, r2, g_ref):
        g_ref[...] = _grad_total(o32, o16, r0, r1, r2)

    return _pallas_call(
        body, name="grad_sum",
        grid_spec=pltpu.PrefetchScalarGridSpec(
            num_scalar_prefetch=1, grid=(rows // tr,), in_specs=_grad_in_specs(tr, cols, 0),
            out_specs=pl.BlockSpec((tr, cols), lambda i, p: (i, 0))),
        out_shape=jax.ShapeDtypeStruct((rows, cols), F32), compiler_params=_cp("parallel"),
    )(place, g32, recv1, recv2, recv2, recv2)


def _adamw_math(w, g, m, v):
    m = ADAM_B1 * m + (1.0 - ADAM_B1) * g
    v = ADAM_B2 * v + (1.0 - ADAM_B2) * (g * g)
    m_hat = m / (1.0 - ADAM_B1 ** ADAM_STEP)
    v_hat = v / (1.0 - ADAM_B2 ** ADAM_STEP)
    delta = -ADAM_LR * (m_hat / (jnp.sqrt(v_hat) + ADAM_EPS) + ADAM_WD * w)
    return delta, m, v


def grad_sum_adamw(g32, recv1, recv2, w, m, v, place, l, prev):
    L, rows, cols = w.shape
    tr = _rows_tile(rows, cols, 1 << 18)

    def body(p_ref, o32, o16, r0, r1, r2, w_ref, m_ref, v_ref, *rest):
        g_ref, d_ref, nm_ref, nv_ref = rest[-4:]
        g = _grad_total(o32, o16, r0, r1, r2)
        d, nm, nv = _adamw_math(w_ref[0], g, m_ref[0], v_ref[0])
        g_ref[0] = g
        d_ref[0] = d
        nm_ref[0] = nm
        nv_ref[0] = nv

    blk = pl.BlockSpec((1, tr, cols), lambda i, p: (l, i, 0))
    args = [g32, recv1, recv2, recv2, recv2, w, m, v]
    in_specs = _grad_in_specs(tr, cols, l) + [blk] * 3
    aliases = {}
    if prev is not None:
        aliases = {1 + len(args) + k: k for k in range(4)}
        args += list(prev)
        in_specs += [ANY] * 4
    return _pallas_call(
        body, name="grad_sum_adamw",
        grid_spec=pltpu.PrefetchScalarGridSpec(num_scalar_prefetch=1, grid=(rows // tr,), in_specs=in_specs, out_specs=[blk] * 4),
        out_shape=[jax.ShapeDtypeStruct((L, rows, cols), F32)] * 4, input_output_aliases=aliases,
        compiler_params=_cp("parallel"),
    )(place, *args)


def adamw(w, g, m, v):
    rows, cols = w.shape
    tr = _rows_tile(rows, cols, 1 << 18)

    def body(w_ref, g_ref, m_ref, v_ref, d_ref, nm_ref, nv_ref):
        d, nm, nv = _adamw_math(w_ref[...], g_ref[...], m_ref[...], v_ref[...])
        d_ref[...] = d
        nm_ref[...] = nm
        nv_ref[...] = nv

    blk = pl.BlockSpec((tr, cols), lambda i: (i, 0))
    return _pallas_call(body, name="adamw_small", grid=(rows // tr,), in_specs=[blk] * 4, out_specs=[blk] * 3,
                        out_shape=[jax.ShapeDtypeStruct((rows, cols), F32)] * 3, compiler_params=_cp("parallel"))(w, g, m, v)


SMALL = ("norm_ffn1", "norm_mix", "pool_w", "pool_b", "pool_scale", "conv_w", "conv_b", "lru_w_a", "lru_b_a", "lru_w_x", "lru_b_x",
         "lru_lambda", "norm_ffn2", "final_norm")
BIG = ("ffn1_w_up", "ffn1_w_down", "w_in", "w_pool_up", "w_lru_up", "w_out", "ffn2_w_up", "ffn2_w_down")
NAMES = ("norm_ffn1", "ffn1_w_up", "ffn1_w_down", "norm_mix", "w_in", "pool_w", "pool_b", "pool_scale", "w_pool_up", "conv_w", "conv_b",
         "lru_w_a", "lru_b_a", "lru_w_x", "lru_b_x", "lru_lambda", "w_lru_up", "w_out", "norm_ffn2", "ffn2_w_up", "ffn2_w_down", "final_norm")
SUBLAYERS = (("ffn1_w_up", "ffn1_w_down"), ("w_in", "w_pool_up", "w_lru_up", "w_out", "conv_w"), ("ffn2_w_up", "ffn2_w_down"))
PACK_ROWS = 16 * N_DEV


def _pack(parts):
    flat = jnp.concatenate([p.reshape(-1) for p in parts])
    unit = 128 * PACK_ROWS
    padded = -(-flat.size // unit) * unit
    return jnp.pad(flat, (0, padded - flat.size)).reshape(-1, 128)


def _unpack(packed, shapes):
    flat = packed.reshape(-1)
    out, off = [], 0
    for s in shapes:
        n = 1
        for d in s:
            n *= d
        out.append(flat[off:off + n].reshape(s))
        off += n
    return out


def kernel(x, norm_ffn1, ffn1_w_up, ffn1_w_down, norm_mix, w_in, pool_w, pool_b, pool_scale, w_pool_up, conv_w, conv_b, lru_w_a, lru_b_a, lru_w_x, lru_b_x, lru_lambda, w_lru_up, w_out, norm_ffn2, ffn2_w_up, ffn2_w_down, final_norm, loss_target, m_norm_ffn1, m_ffn1_w_up, m_ffn1_w_down, m_norm_mix, m_w_in, m_pool_w, m_pool_b, m_pool_scale, m_w_pool_up, m_conv_w, m_conv_b, m_lru_w_a, m_lru_b_a, m_lru_w_x, m_lru_b_x, m_lru_lambda, m_w_lru_up, m_w_out, m_norm_ffn2, m_ffn2_w_up, m_ffn2_w_down, m_final_norm, v_norm_ffn1, v_ffn1_w_up, v_ffn1_w_down, v_norm_mix, v_w_in, v_pool_w, v_pool_b, v_pool_scale, v_w_pool_up, v_conv_w, v_conv_b, v_lru_w_a, v_lru_b_a, v_lru_w_x, v_lru_b_x, v_lru_lambda, v_w_lru_up, v_w_out, v_norm_ffn2, v_ffn2_w_up, v_ffn2_w_down, v_final_norm):
    W = dict(norm_ffn1=norm_ffn1, ffn1_w_up=ffn1_w_up, ffn1_w_down=ffn1_w_down, norm_mix=norm_mix, w_in=w_in, pool_w=pool_w, pool_b=pool_b,
             pool_scale=pool_scale, w_pool_up=w_pool_up, conv_w=conv_w, conv_b=conv_b, lru_w_a=lru_w_a, lru_b_a=lru_b_a, lru_w_x=lru_w_x,
             lru_b_x=lru_b_x, lru_lambda=lru_lambda, w_lru_up=w_lru_up, w_out=w_out, norm_ffn2=norm_ffn2, ffn2_w_up=ffn2_w_up,
             ffn2_w_down=ffn2_w_down, final_norm=final_norm)
    M = dict(norm_ffn1=m_norm_ffn1, ffn1_w_up=m_ffn1_w_up, ffn1_w_down=m_ffn1_w_down, norm_mix=m_norm_mix, w_in=m_w_in, pool_w=m_pool_w,
             pool_b=m_pool_b, pool_scale=m_pool_scale, w_pool_up=m_w_pool_up, conv_w=m_conv_w, conv_b=m_conv_b, lru_w_a=m_lru_w_a,
             lru_b_a=m_lru_b_a, lru_w_x=m_lru_w_x, lru_b_x=m_lru_b_x, lru_lambda=m_lru_lambda, w_lru_up=m_w_lru_up, w_out=m_w_out,
             norm_ffn2=m_norm_ffn2, ffn2_w_up=m_ffn2_w_up, ffn2_w_down=m_ffn2_w_down, final_norm=m_final_norm)
    V = dict(norm_ffn1=v_norm_ffn1, ffn1_w_up=v_ffn1_w_up, ffn1_w_down=v_ffn1_w_down, norm_mix=v_norm_mix, w_in=v_w_in, pool_w=v_pool_w,
             pool_b=v_pool_b, pool_scale=v_pool_scale, w_pool_up=v_w_pool_up, conv_w=v_conv_w, conv_b=v_conv_b, lru_w_a=v_lru_w_a,
             lru_b_a=v_lru_b_a, lru_w_x=v_lru_w_x, lru_b_x=v_lru_b_x, lru_lambda=v_lru_lambda, w_lru_up=v_w_lru_up, w_out=v_w_out,
             norm_ffn2=v_norm_ffn2, ffn2_w_up=v_ffn2_w_up, ffn2_w_down=v_ffn2_w_down, final_norm=v_final_norm)

    T, D = x.shape[1], x.shape[2]
    L = norm_ffn1.shape[0]
    P = pool_scale.shape[1]
    R = lru_lambda.shape[1]
    H, hd = lru_w_a.shape[1], lru_w_a.shape[2]
    CW = conv_w.shape[1]
    cs = ffn1_w_up.shape[2]
    ci = w_in.shape[2]
    xin = x.reshape(T, D)
    tgt = loss_target.reshape(T, D)
    dev = 4 * lax.axis_index("x") + 2 * lax.axis_index("y") + lax.axis_index("c")
    place = jnp.stack([lax.axis_index("c"), 2 * lax.axis_index("x") + lax.axis_index("y"), dev]).astype(jnp.int32)

    cw_flat = conv_w.reshape(L, -1)
    cw_pad = (-cw_flat.shape[1]) % 1024
    cw_tiles = jnp.pad(cw_flat, ((0, 0), (0, cw_pad))).reshape(L, -1, 128)

    def units(l):
        return SUBLAYERS if l == 0 else (tuple(n for u in SUBLAYERS for n in u),)

    def gather_units_start(l):
        started_units = []
        for k, names in enumerate(units(l)):
            land = [place_own(cw_tiles, l, place, F32) if n == "conv_w" else place_own(W[n], l, place, BF16) for n in names]
            started_units.append((names, f"l{l}_u{k}") + gather_start(f"gather_start_l{l}_u{k}", land))
        return started_units

    def gather_unit_finish(unit, after):
        names, tag, send_sems, recv_sems, land, _tok = unit
        land = gather_wait(f"gather_wait_{tag}", land, send_sems, recv_sems, after)
        land = gather_forward(f"gather_forward_{tag}", land)
        g = dict(zip(names, land))
        one = lambda a: a.reshape((1,) + a.shape)
        w = {}
        for tag_, up, dn in (("1", "ffn1_w_up", "ffn1_w_down"), ("2", "ffn2_w_up", "ffn2_w_down")):
            if up in g:
                w["wup" + tag_], w["wd" + tag_] = one(g[up]), g[dn].reshape(1, -1, D)
        if "w_in" in g:
            cw_l = g["conv_w"].reshape(N_DEV, -1)[:, :cw_flat.shape[1]].reshape((N_DEV,) + conv_w.shape[1:])
            w.update(win=one(g["w_in"]), wlu=g["w_lru_up"].reshape(1, R, D), wout=g["w_out"].reshape(1, D, D),
                     wpu=g["w_pool_up"].transpose(1, 0, 2).reshape(1, P, D),
                     cw=cw_l.transpose(1, 0, 2).reshape(1, CW, R))
        return w

    def layer_params(l):
        vec = lambda a: a[l:l + 1].reshape(1, 1, -1)
        return dict(g1=vec(norm_ffn1), gm=vec(norm_mix), g2=vec(norm_ffn2), pb=vec(pool_b), ps=vec(pool_scale), cb=vec(conv_b),
                    ba=vec(lru_b_a), bx=vec(lru_b_x), lam=vec(lru_lambda), pw=pool_w[l:l + 1], wa=lru_w_a[l:l + 1], wx=lru_w_x[l:l + 1])

    AHEAD = 2
    started = {l: gather_units_start(l) for l in range(min(AHEAD, L))}
    saved, LW, LP = [], [], []
    xc = xin
    for l in range(L):
        todo = started.pop(l)
        w, p = {}, layer_params(l)

        def need(key, xc_now):
            while key not in w:
                tokens = [u[-1] for us in started.values() for u in us] + [u[-1] for u in todo[1:]]
                w.update(gather_unit_finish(todo.pop(0), [xc_now] + tokens))

        need("wup1", xc)
        if l + AHEAD < L:
            started[l + AHEAD] = gather_units_start(l + AHEAD)
        sv = {"x1": xc}
        sv["h1"], sv["u1"], sv["s1"] = ffn_up(xc, p["g1"], w["wup1"], 0)
        xc = ffn_down(sv["s1"], w["wd1"], xc, 0)
        sv["x2"] = xc
        need("win", xc)
        sv["h2"], sv["proj"] = mix_in(xc, p["gm"], w["win"], 0)
        sv["pm"] = pool_fwd(sv["proj"], p["pw"], p["pb"], p["ps"], 0)
        sv["hl"], sv["hs"] = lru_fwd(sv["proj"], w["cw"], p["cb"], p["wa"], p["ba"], p["wx"], p["bx"], p["lam"], P, 0)
        xc, sv["yp"], sv["yl"], sv["z"] = mix_out(sv["pm"], sv["hl"], sv["proj"], xc, w["wpu"], w["wlu"], w["wout"], P, 0)
        sv["x3"] = xc
        need("wup2", xc)
        sv["h3"], sv["u3"], sv["s3"] = ffn_up(xc, p["g2"], w["wup2"], 0)
        xc = ffn_down(sv["s3"], w["wd2"], xc, 0)
        saved.append(sv)
        LW.append(w)
        LP.append(p)

    loss_part, dx, d_final = loss_head(xc, final_norm.reshape(1, D), tgt)
    loss = lax.psum(loss_part[0, 0], ("x", "y", "c"))

    G = {n: None for n in BIG}
    small = {n: [None] * L for n in SMALL if n != "final_norm"}

    def to_slots(name, a):
        if name == "w_pool_up":
            return a.reshape(L, P, N_DEV, D // N_DEV).transpose(0, 2, 1, 3)
        return a.reshape((L, N_DEV) + W[name].shape[1:])

    def ffn_bwd(dy, sv, tag, wup, wd, gn, up_name, dn_name, norm_name, l, deps=()):
        dout, du = ffn_down_bwd(dy, wd, sv["u" + tag], 0, deps)
        du = du.reshape(N_DEV, T, cs)
        G[dn_name] = dw_tn("dw_down", [sv["s" + tag]], lambda tk: pl.BlockSpec((1, tk, cs), lambda g, k: (g, k, 0)),
                           [dout], lambda tk: pl.BlockSpec((tk, D), lambda g, k: (k, 0)), 4, cs, D, T, l, L, G[dn_name])
        G[up_name] = dw_tn("dw_up", [sv["h" + tag]], lambda tk: pl.BlockSpec((tk, D), lambda g, k: (k, 0)),
                           [du], lambda tk: pl.BlockSpec((1, tk, cs), lambda g, k: (g, k, 0)), N_DEV, D, cs, T, l, L, G[up_name])
        dxn, dg = dx_norm_bwd("ffn_dx", du, lambda tm: pl.BlockSpec((1, tm, cs), lambda j, i: (j, i, 0)), wup, N_DEV,
                              sv["x" + tag], gn, dy, 0)
        small[norm_name][l] = dg.reshape(D)
        return dxn

    in_flight = []

    def reduce_start(l, names, tag):
        names = [n for n in names if n != "conv_w"]
        recv1 = pair_exchange(f"rs_pair_exchange_{tag}", [to_slots(n, G[n][1]) for n in names], layer=l)
        pair16 = [pair_sum(to_slots(n, G[n][0]), r_, place, l) for n, r_ in zip(names, recv1)]
        send_sems, recv_sems, bufs, tok = chip_exchange_start(f"rs_chip_start_{tag}", pair16)
        in_flight.append((l, names, tag, send_sems, recv_sems, bufs, recv1))
        return (tok,)

    deps = ()
    for l in reversed(range(L)):
        sv, w, p = saved[l], LW[l], LP[l]
        split = len(units(l)) > 1
        dx = ffn_bwd(dx, sv, "3", w["wup2"], w["wd2"], p["g2"], "ffn2_w_up", "ffn2_w_down", "norm_ffn2", l, deps)
        deps = reduce_start(l, units(l)[2], f"l{l}_u2") if split else ()
        dyb, dyp, dyl, dgp, dgl, dpm, dhl = mix_out_bwd(dx, sv["proj"], sv["yp"], sv["yl"], w["wpu"], w["wlu"], w["wout"], P, R, 0, deps)
        row = lambda wd_: (lambda tk: pl.BlockSpec((tk, wd_), lambda g, k: (k, 0)))
        G["w_out"] = dw_tn("dw_out", [sv["z"]], row(D), [dyb], row(D), 1, D, D, T, l, L, G["w_out"])
        G["w_lru_up"] = dw_tn("dw_lru_up", [sv["hl"]], row(R), [dyl], row(D), 1, R, D, T, l, L, G["w_lru_up"])
        G["w_pool_up"] = dw_tn("dw_pool_up", [sv["pm"]], row(P), [dyp], row(D), 1, P, D, T, l, L, G["w_pool_up"])
        du_lru, du_gelu, dcw, dcb, dwa, dba, dwx, dbx, dlam = lru_bwd(
            sv["proj"], sv["hs"], dhl, w["cw"], p["cb"], p["wa"], p["ba"], p["wx"], p["bx"], p["lam"], P, 0)
        du_pool, dpw, dpb, dpsc = pool_bwd(sv["proj"], dpm, p["pw"], p["pb"], p["ps"], 0)
        dproj = jnp.concatenate([du_pool, du_lru, du_gelu, dgp, dgl], axis=1)
        G["w_in"] = dw_tn("dw_in", [sv["h2"]], row(D), [dproj], lambda tk: pl.BlockSpec((tk, ci), lambda g, k: (k, g)),
                          N_DEV, D, ci, T, l, L, G["w_in"])
        dx, dgm = dx_norm_bwd("mix_dx", dproj, lambda tm: pl.BlockSpec((tm, ci), lambda j, i: (i, j)), w["win"], N_DEV,
                              sv["x2"], p["gm"], dx, 0)
        small["norm_mix"][l] = dgm.reshape(D)
        small["pool_w"][l], small["pool_b"][l], small["pool_scale"][l] = dpw[0], dpb.reshape(pool_b.shape[1:]), dpsc.reshape(P)
        small["conv_w"][l], small["conv_b"][l] = dcw[0], dcb.reshape(R)
        small["lru_w_a"][l], small["lru_b_a"][l] = dwa[0], dba.reshape(H, hd)
        small["lru_w_x"][l], small["lru_b_x"][l] = dwx[0], dbx.reshape(H, hd)
        small["lru_lambda"][l] = dlam.reshape(R)
        deps = reduce_start(l, units(l)[1], f"l{l}_u1") if split else ()
        dx = ffn_bwd(dx, sv, "1", w["wup1"], w["wd1"], p["g1"], "ffn1_w_up", "ffn1_w_down", "norm_ffn1", l, deps)
        deps = reduce_start(l, units(l)[0], f"l{l}_u0")

    grad_x = dx.reshape(x.shape)

    small_parts = [jnp.stack(small[n]) for n in SMALL if n != "final_norm"] + [d_final.reshape(D)]
    small_shapes = [p.shape for p in small_parts]
    gpack = _pack(small_parts).reshape(1, N_DEV, -1, 128)
    recv1_s = pair_exchange("rs_pair_exchange_small", [gpack])[0]
    pair_s = pair_sum(gpack, recv1_s, place, 0)
    recv2_s = chip_exchange("rs_chip_exchange_small", [pair_s])[0]

    g32 = {n: to_slots(n, G[n][0]) for n in BIG}
    outs = {n: None for n in BIG}
    after = [dx, recv2_s]
    for l, names, tag, send_sems, recv_sems, bufs, recv1 in in_flight:
        recv2 = chip_exchange_wait(f"rs_chip_wait_{tag}", bufs, send_sems, recv_sems, after)
        for i, n in enumerate(names):
            outs[n] = grad_sum_adamw(g32[n], recv1[i], recv2[i], W[n], M[n], V[n], place, l, outs[n])
        after = [outs[names[-1]][0]]
    out_g, out_d, out_m, out_v = ({n: outs[n][k] for n in BIG} for k in range(4))

    gs = grad_sum(gpack, recv1_s, recv2_s, place)
    gs_all = all_gather("all_gather_small_grads", [gs.reshape((1,) + gs.shape)])[0]
    gs_all = gs_all.reshape(-1, 128)
    small_g = dict(zip(SMALL, _unpack(gs_all, small_shapes)))
    full_shapes = [W[n].shape if n != "conv_w" else small_shapes[SMALL.index("conv_w")] for n in SMALL]
    rep = [n for n in SMALL if n != "conv_w"]
    rep_shapes = [W[n].shape for n in rep]
    wp, mp, vp = (_pack([S[n] for n in rep]) for S in (W, M, V))
    gp = _pack([small_g[n] for n in rep])
    dp, nmp, nvp = adamw(wp, gp, mp, vp)
    for S, packed in ((out_d, dp), (out_m, nmp), (out_v, nvp)):
        S.update(zip(rep, _unpack(packed, rep_shapes)))
    for n in rep:
        out_g[n] = small_g[n]
    cwc = conv_w.shape[2]
    gcw = lax.dynamic_slice_in_dim(small_g["conv_w"], dev * cwc, cwc, axis=2)
    cw2 = lambda a: a.reshape(-1, cwc)
    pad_rows = (-cw2(conv_w).shape[0]) % 8
    padr = lambda a: jnp.pad(cw2(a), ((0, pad_rows), (0, 0)))
    dcw_, mcw_, vcw_ = adamw(padr(conv_w), padr(gcw), padr(M["conv_w"]), padr(V["conv_w"]))
    nrow = cw2(conv_w).shape[0]
    out_g["conv_w"] = gcw
    out_d["conv_w"], out_m["conv_w"], out_v["conv_w"] = (a[:nrow].reshape(conv_w.shape) for a in (dcw_, mcw_, vcw_))
    del full_shapes

    return (loss, grad_x, *[out_g[n] for n in NAMES], *[out_d[n] for n in NAMES], *[out_m[n] for n in NAMES], *[out_v[n] for n in NAMES])
```

```python
import functools

import jax
import jax.numpy as jnp
from jax import lax
from jax.experimental import pallas as pl
from jax.experimental.pallas import tpu as pltpu

F32, BF16 = jnp.float32, jnp.bfloat16
EPS = 1e-6
LRU_C = 8.0
POOL_WINDOWS = (2, 4, 8, 16)
ADAM_LR, ADAM_B1, ADAM_B2, ADAM_EPS, ADAM_WD, ADAM_STEP = 0.001, 0.9, 0.999, 1e-08, 0.01, 10
N_DEV = 8
N_CHIP = 4
MESH = pl.DeviceIdType.MESH
V7X_VMEM_LIMIT = 56 * 1024 * 1024
ROW_TILE = 512
WIDE_TILE = 1024
SUM_TILE = 2048
ANY = pl.BlockSpec(memory_space=pl.ANY)

_pallas_call = pl.pallas_call


def _cp(*sem):
    return pltpu.CompilerParams(dimension_semantics=sem if sem else None, vmem_limit_bytes=V7X_VMEM_LIMIT)


def _tile(n, t):
    t = min(n, t)
    assert n % t == 0, (n, t)
    return t


def _dot(a, b):
    return jnp.dot(a, b, preferred_element_type=F32)


def _dot_nt(a, b):
    return lax.dot_general(a, b, (((1,), (1,)), ((), ())), preferred_element_type=F32)


def _dot_tn(a, b):
    return lax.dot_general(a, b, (((0,), (0,)), ((), ())), preferred_element_type=F32)


def _rms(xv):
    r = lax.rsqrt(jnp.mean(xv * xv, axis=-1, keepdims=True) + EPS)
    return xv * r, r


def _rms_bwd(dh, xv, gv, dy):
    n, r = _rms(xv)
    dn = dh * gv
    dx = dy + r * (dn - n * jnp.mean(dn * n, axis=-1, keepdims=True))
    return dx, jnp.sum(dh * n, axis=0, keepdims=True)


def _shift_down(x, k, fill=0.0):
    if k == 0:
        return x
    rows = lax.broadcasted_iota(jnp.int32, x.shape, 0)
    return jnp.where(rows >= k, pltpu.roll(x, k, 0), fill)


def _shift_up(x, k, fill=0.0):
    if k == 0:
        return x
    n = x.shape[0]
    rows = lax.broadcasted_iota(jnp.int32, x.shape, 0)
    return jnp.where(rows < n - k, pltpu.roll(x, n - k, 0), fill)


def _sigmoid(x):
    return 0.5 * jnp.tanh(0.5 * x) + 0.5


_GELU_K = 0.7978845608028654
_GELU_C = 0.044715


def _gelu(x):
    th = jnp.tanh(_GELU_K * (x + _GELU_C * x * x * x))
    return 0.5 * x * (1.0 + th), th


def _gelu_grad(x, th):
    return 0.5 * (1.0 + th) + 0.5 * x * (1.0 - th * th) * _GELU_K * (1.0 + 3.0 * _GELU_C * x * x)


def ffn_up(x, g, wup, l):
    T, D = x.shape
    cs = wup.shape[-1]
    tm = _tile(T, WIDE_TILE)
    ni = T // tm

    def body(x_ref, g_ref, wa_ref, wb_ref, h_ref, u_ref, s_ref, hs_ref):
        rows = pl.ds(pl.multiple_of(pl.program_id(1) * tm, tm), tm)

        @pl.when(pl.program_id(0) == 0)
        def _():
            n, _r = _rms(x_ref[...])
            hv = (n * g_ref[0]).astype(BF16)
            hs_ref[rows, :] = hv
            h_ref[...] = hv

        hv = hs_ref[rows, :]
        a = _dot(hv, wa_ref[0, 0])
        b = _dot(hv, wb_ref[0, 0])
        u_ref[0, 0] = a.astype(BF16)
        u_ref[1, 0] = b.astype(BF16)
        s_ref[0] = (a * _sigmoid(a) * b).astype(BF16)

    first = lambda j, i: (jnp.where(j == 0, i, ni - 1), 0)
    return _pallas_call(
        body, name="ffn_up", grid=(4, ni),
        in_specs=[pl.BlockSpec((tm, D), first), pl.BlockSpec((1, 1, D), lambda j, i: (l, 0, 0)),
                  pl.BlockSpec((1, 1, D, cs), lambda j, i: (l, j, 0, 0)), pl.BlockSpec((1, 1, D, cs), lambda j, i: (l, j + 4, 0, 0))],
        out_specs=[pl.BlockSpec((tm, D), first), pl.BlockSpec((2, 1, tm, cs), lambda j, i: (0, j, i, 0)),
                   pl.BlockSpec((1, tm, cs), lambda j, i: (j, i, 0))],
        out_shape=[jax.ShapeDtypeStruct((T, D), BF16), jax.ShapeDtypeStruct((2, 4, T, cs), BF16), jax.ShapeDtypeStruct((4, T, cs), BF16)],
        scratch_shapes=[pltpu.VMEM((T, D), BF16)],
        compiler_params=_cp("arbitrary", "arbitrary"),
    )(x, g, wup, wup)


def ffn_down(s, wd, x, l):
    _, T, cs = s.shape
    D = x.shape[1]
    tm = _tile(T, WIDE_TILE)

    def body(s_ref, w_ref, x_ref, o_ref, acc_ref):
        j = pl.program_id(1)

        @pl.when(j == 0)
        def _():
            acc_ref[...] = jnp.zeros_like(acc_ref)

        acc_ref[...] += _dot(s_ref[0], w_ref[0])

        @pl.when(j == 3)
        def _():
            o_ref[...] = x_ref[...] + 0.5 * acc_ref[...]

    return _pallas_call(
        body, name="ffn_down", grid=(T // tm, 4),
        in_specs=[pl.BlockSpec((1, tm, cs), lambda i, j: (j, i, 0)), pl.BlockSpec((1, cs, D), lambda i, j: (l, j, 0)),
                  pl.BlockSpec((tm, D), lambda i, j: (i, 0))],
        out_specs=pl.BlockSpec((tm, D), lambda i, j: (i, 0)),
        out_shape=jax.ShapeDtypeStruct((T, D), F32),
        scratch_shapes=[pltpu.VMEM((tm, D), F32)],
        compiler_params=_cp("parallel", "arbitrary"),
    )(s, wd, x)


def mix_in(x, g, win, l):
    T, D = x.shape
    ci = win.shape[-1]
    tm = _tile(T, WIDE_TILE)
    ni = T // tm

    def body(x_ref, g_ref, w_ref, h_ref, p_ref, hs_ref):
        rows = pl.ds(pl.multiple_of(pl.program_id(1) * tm, tm), tm)

        @pl.when(pl.program_id(0) == 0)
        def _():
            n, _r = _rms(x_ref[...])
            hv = (n * g_ref[0]).astype(BF16)
            hs_ref[rows, :] = hv
            h_ref[...] = hv

        p_ref[...] = _dot(hs_ref[rows, :], w_ref[0, 0]).astype(BF16)

    first = lambda j, i: (jnp.where(j == 0, i, ni - 1), 0)
    return _pallas_call(
        body, name="mix_in", grid=(N_DEV, ni),
        in_specs=[pl.BlockSpec((tm, D), first), pl.BlockSpec((1, 1, D), lambda j, i: (l, 0, 0)),
                  pl.BlockSpec((1, 1, D, ci), lambda j, i: (l, j, 0, 0))],
        out_specs=[pl.BlockSpec((tm, D), first), pl.BlockSpec((tm, ci), lambda j, i: (i, j))],
        out_shape=[jax.ShapeDtypeStruct((T, D), BF16), jax.ShapeDtypeStruct((T, N_DEV * ci), BF16)],
        scratch_shapes=[pltpu.VMEM((T, D), BF16)],
        compiler_params=_cp("arbitrary", "arbitrary"),
    )(x, g, win)


def _inv_count(T, w):
    t = lax.broadcasted_iota(jnp.int32, (T, 1), 0)
    return 1.0 / jnp.minimum(t + 1, w).astype(F32)


def _pooled(ug, w, inv):
    s = ug
    k = 1
    while k < w:
        s = s + _shift_down(s, k)
        k *= 2
    return s * inv - ug


def pool_fwd(proj, pw, pb, ps, l):
    T = proj.shape[0]
    _, G, gd, _ = pw.shape
    P = G * gd

    def body(u_ref, w_ref, b_ref, s_ref, o_ref):
        for gi in range(G):
            cols = slice(gi * gd, (gi + 1) * gd)
            ug = u_ref[:, cols].astype(F32)
            pooled = _pooled(ug, POOL_WINDOWS[gi], _inv_count(T, POOL_WINDOWS[gi]))
            mixed = _dot(pooled.astype(BF16), w_ref[0, gi].astype(BF16)) + b_ref[0, :, cols]
            o_ref[:, cols] = (mixed * s_ref[0, :, cols]).astype(BF16)

    return _pallas_call(
        body, name="pool_fwd", grid=(1,),
        in_specs=[pl.BlockSpec((T, P), lambda i: (0, 0)), pl.BlockSpec((1, G, gd, gd), lambda i: (l, 0, 0, 0)),
                  pl.BlockSpec((1, 1, P), lambda i: (l, 0, 0)), pl.BlockSpec((1, 1, P), lambda i: (l, 0, 0))],
        out_specs=pl.BlockSpec((T, P), lambda i: (0, 0)),
        out_shape=jax.ShapeDtypeStruct((T, P), BF16),
        compiler_params=_cp("arbitrary"),
    )(proj, pw, pb, ps)


def _conv(u, cw_ref, cb):
    CW = cw_ref.shape[1]
    v = cb
    for k in range(CW):
        v = v + cw_ref[0, k:k + 1, :] * _shift_down(u, CW - 1 - k)
    return v


def _softplus(z):
    return jnp.maximum(z, 0.0) + jnp.log1p(jnp.exp(-jnp.abs(z)))


def _lru_gates(v, wa_ref, ba, wx_ref, bx, lam):
    vb = v.astype(BF16)
    r = _sigmoid(_dot(vb, wa_ref[0, 0].astype(BF16)) + ba)
    i = _sigmoid(_dot(vb, wx_ref[0, 0].astype(BF16)) + bx)
    sp = _softplus(-lam)
    log_a = -LRU_C * r * sp
    a = jnp.exp(log_a)
    m2 = -jnp.tanh(log_a) * (a * a + 1.0)
    inv_mult = lax.rsqrt(m2)
    mult = jnp.where(m2 > 0.0, m2 * inv_mult, 0.0)
    return r, i, sp, a, mult, inv_mult


def _scan_fwd(a_ref, b_ref, o_ref):
    T, W = a_ref.shape
    rows = lax.broadcasted_iota(jnp.int32, (8, W), 0)

    def step(t, carry):
        r0 = pl.multiple_of(t * 8, 8)
        A = a_ref[pl.ds(r0, 8), :]
        B = b_ref[pl.ds(r0, 8), :]
        for s in (1, 2, 4):
            keep = rows >= s
            As = jnp.where(keep, pltpu.roll(A, s, 0), 1.0)
            Bs = jnp.where(keep, pltpu.roll(B, s, 0), 0.0)
            B = A * Bs + B
            A = A * As
        h = B + A * carry
        o_ref[pl.ds(r0, 8), :] = h
        return jnp.broadcast_to(h[7:8, :], (8, W))

    lax.fori_loop(0, T // 8, step, jnp.zeros((8, W), F32), unroll=8)


def _scan_bwd(a_ref, b_ref, o_ref):
    T, W = a_ref.shape
    rows = lax.broadcasted_iota(jnp.int32, (8, W), 0)
    nt = T // 8

    def step(t, carry):
        r0 = pl.multiple_of((nt - 1 - t) * 8, 8)
        A = a_ref[pl.ds(r0, 8), :]
        B = b_ref[pl.ds(r0, 8), :]
        for s in (1, 2, 4):
            keep = rows < 8 - s
            As = jnp.where(keep, pltpu.roll(A, 8 - s, 0), 1.0)
            Bs = jnp.where(keep, pltpu.roll(B, 8 - s, 0), 0.0)
            B = A * Bs + B
            A = A * As
        y = B + A * carry
        o_ref[pl.ds(r0, 8), :] = y
        return jnp.broadcast_to(y[0:1, :], (8, W))

    lax.fori_loop(0, nt, step, jnp.zeros((8, W), F32), unroll=8)


def _lru_specs(T, hd, P, R, CW, l):
    ob, gb = P // hd, (P + R) // hd
    vec = pl.BlockSpec((1, 1, hd), lambda h: (l, 0, h))
    mat = pl.BlockSpec((1, 1, hd, hd), lambda h: (l, h, 0, 0))
    return [pl.BlockSpec((T, hd), lambda h: (0, ob + h)), pl.BlockSpec((T, hd), lambda h: (0, gb + h)),
            pl.BlockSpec((1, CW, hd), lambda h: (l, 0, h)), vec, mat, vec, mat, vec, vec]


def lru_fwd(proj, cw, cb, wa, ba, wx, bx, lam, P, l):
    T = proj.shape[0]
    _, H, hd, _ = wa.shape
    R = H * hd
    CW = cw.shape[1]
    assert P % hd == 0 and T % 8 == 0

    def body(u_ref, ug_ref, cw_ref, cb_ref, wa_ref, ba_ref, wx_ref, bx_ref, lam_ref, hl_ref, hs_ref, a_s, b_s):
        v = _conv(u_ref[...].astype(F32), cw_ref, cb_ref[0])
        _r, i, _sp, a, mult, _im = _lru_gates(v, wa_ref, ba_ref[0], wx_ref, bx_ref[0], lam_ref[0])
        a_s[...] = a
        b_s[...] = mult * (i * v)
        _scan_fwd(a_s, b_s, hs_ref)
        ge, _th = _gelu(ug_ref[...].astype(F32))
        hl_ref[...] = (hs_ref[...] * ge).astype(BF16)

    out = pl.BlockSpec((T, hd), lambda h: (0, h))
    return _pallas_call(
        body, name="lru_fwd", grid=(H,),
        in_specs=_lru_specs(T, hd, P, R, CW, l),
        out_specs=[out, out],
        out_shape=[jax.ShapeDtypeStruct((T, R), BF16), jax.ShapeDtypeStruct((T, R), F32)],
        scratch_shapes=[pltpu.VMEM((T, hd), F32)] * 2,
        compiler_params=_cp("parallel"),
    )(proj, proj, cw, cb, wa, ba, wx, bx, lam)


def mix_out(pm, hl, proj, x, wpu, wlu, wout, P, l):
    T, D = x.shape
    R = hl.shape[1]
    tm = _tile(T, ROW_TILE)
    assert (P + 2 * R) % D == 0
    gb = (P + 2 * R) // D

    def body(pm_ref, hl_ref, gp_ref, gl_ref, x_ref, wpu_ref, wlu_ref, wo_ref, o_ref, yp_ref, yl_ref, z_ref):
        yp = _dot(pm_ref[...], wpu_ref[0])
        yl = _dot(hl_ref[...], wlu_ref[0])
        z = (_sigmoid(gp_ref[...].astype(F32)) * yp + _sigmoid(gl_ref[...].astype(F32)) * yl).astype(BF16)
        yp_ref[...] = yp.astype(BF16)
        yl_ref[...] = yl.astype(BF16)
        z_ref[...] = z
        o_ref[...] = x_ref[...] + _dot(z, wo_ref[0])

    row = lambda w: pl.BlockSpec((tm, w), lambda i: (i, 0))
    return _pallas_call(
        body, name="mix_out", grid=(T // tm,),
        in_specs=[row(P), row(R), pl.BlockSpec((tm, D), lambda i: (i, gb)), pl.BlockSpec((tm, D), lambda i: (i, gb + 1)), row(D),
                  pl.BlockSpec((1, P, D), lambda i: (l, 0, 0)), pl.BlockSpec((1, R, D), lambda i: (l, 0, 0)),
                  pl.BlockSpec((1, D, D), lambda i: (l, 0, 0))],
        out_specs=[row(D)] * 4,
        out_shape=[jax.ShapeDtypeStruct((T, D), F32)] + [jax.ShapeDtypeStruct((T, D), BF16)] * 3,
        compiler_params=_cp("parallel"),
    )(pm, hl, proj, proj, x, wpu, wlu, wout)


def loss_head(x, gf, tgt):
    T, D = x.shape
    tm = _tile(T, ROW_TILE)

    def body(x_ref, g_ref, t_ref, loss_ref, dx_ref, dg_ref):
        @pl.when(pl.program_id(0) == 0)
        def _():
            loss_ref[...] = jnp.zeros_like(loss_ref)
            dg_ref[...] = jnp.zeros_like(dg_ref)

        xv = x_ref[...]
        gv = g_ref[...]
        n, _r = _rms(xv)
        e = n * gv - t_ref[...]
        loss_ref[...] += 0.5 * jnp.sum(jnp.sum(e * e, axis=-1, keepdims=True), axis=0, keepdims=True) / D
        dx, dg = _rms_bwd(e * (1.0 / D), xv, gv, 0.0)
        dx_ref[...] = dx
        dg_ref[...] += dg

    return _pallas_call(
        body, name="loss_head", grid=(T // tm,),
        in_specs=[pl.BlockSpec((tm, D), lambda i: (i, 0)), pl.BlockSpec((1, D), lambda i: (0, 0)), pl.BlockSpec((tm, D), lambda i: (i, 0))],
        out_specs=[pl.BlockSpec((1, 1), lambda i: (0, 0)), pl.BlockSpec((tm, D), lambda i: (i, 0)), pl.BlockSpec((1, D), lambda i: (0, 0))],
        out_shape=[jax.ShapeDtypeStruct((1, 1), F32), jax.ShapeDtypeStruct((T, D), F32), jax.ShapeDtypeStruct((1, D), F32)],
        compiler_params=_cp("arbitrary"),
    )(x, gf, tgt)


def ffn_down_bwd(dy, wd, u, l, deps=()):
    T, D = dy.shape
    cs = u.shape[-1]
    tm = _tile(T, WIDE_TILE)
    ni = T // tm

    def body(dy_ref, w_ref, u_ref, *rest):
        do_ref, du_ref, dyb_ref = rest[len(deps):]
        rows = pl.ds(pl.multiple_of(pl.program_id(1) * tm, tm), tm)

        @pl.when(pl.program_id(0) == 0)
        def _():
            d = (0.5 * dy_ref[...]).astype(BF16)
            dyb_ref[rows, :] = d
            do_ref[...] = d

        ds = _dot_nt(dyb_ref[rows, :], w_ref[0])
        a = u_ref[0, 0].astype(F32)
        b = u_ref[1, 0].astype(F32)
        sg = _sigmoid(a)
        du_ref[0, 0] = (ds * b * (sg * (1.0 + a * (1.0 - sg)))).astype(BF16)
        du_ref[1, 0] = (ds * (a * sg)).astype(BF16)

    first = lambda j, i: (jnp.where(j == 0, i, ni - 1), 0)
    blk = pl.BlockSpec((2, 1, tm, cs), lambda j, i: (0, j, i, 0))
    return _pallas_call(
        body, name="ffn_down_bwd", grid=(4, ni),
        in_specs=[pl.BlockSpec((tm, D), first), pl.BlockSpec((1, cs, D), lambda j, i: (l, j, 0)), blk] + [ANY] * len(deps),
        out_specs=[pl.BlockSpec((tm, D), first), blk],
        out_shape=[jax.ShapeDtypeStruct((T, D), BF16), jax.ShapeDtypeStruct((2, 4, T, cs), BF16)],
        scratch_shapes=[pltpu.VMEM((T, D), BF16)],
        compiler_params=_cp("arbitrary", "arbitrary"),
    )(dy, wd, u, *deps)


def dw_tn(name, a, a_spec, b, b_spec, G, M, N, T):
    tk = _tile(T, SUM_TILE)
    nk = T // tk

    def body(a_ref, b_ref, o32_ref, o16_ref, acc_ref):
        k = pl.program_id(1)

        @pl.when(k == 0)
        def _():
            acc_ref[...] = jnp.zeros_like(acc_ref)

        av = a_ref[0] if len(a_ref.shape) == 3 else a_ref[...]
        bv = b_ref[0] if len(b_ref.shape) == 3 else b_ref[...]
        acc_ref[...] += _dot_tn(av, bv)

        @pl.when(k == nk - 1)
        def _():
            o32_ref[0, 0] = acc_ref[...]
            o16_ref[0, 0] = acc_ref[...].astype(BF16)

    out = pl.BlockSpec((1, 1, M, N), lambda g, k: (0, g, 0, 0))
    return _pallas_call(
        body, name=name, grid=(G, nk),
        in_specs=[a_spec(tk), b_spec(tk)], out_specs=[out, out],
        out_shape=[jax.ShapeDtypeStruct((1, G, M, N), F32), jax.ShapeDtypeStruct((1, G, M, N), BF16)],
        scratch_shapes=[pltpu.VMEM((M, N), F32)],
        compiler_params=_cp("parallel", "arbitrary"),
    )(a, b)


def dx_norm_bwd(name, dact, d_spec, w, G, x, g, dy, l):
    T, D = x.shape
    c = w.shape[-1]
    tm = _tile(T, WIDE_TILE)
    ni = T // tm
    ch = _tile(tm, ROW_TILE // 2)

    def body(d_ref, w_ref, x_ref, g_ref, dy_ref, dx_ref, dg_ref, acc_ref):
        j, i = pl.program_id(0), pl.program_id(1)
        rows = pl.ds(pl.multiple_of(i * tm, tm), tm)

        @pl.when(jnp.logical_and(i == 0, j == 0))
        def _():
            dg_ref[...] = jnp.zeros_like(dg_ref)

        dv = d_ref[0] if len(d_ref.shape) == 3 else d_ref[...]
        part = _dot_nt(dv, w_ref[0, 0])

        @pl.when(j == 0)
        def _():
            acc_ref[rows, :] = part

        @pl.when(j > 0)
        def _():
            acc_ref[rows, :] += part

        @pl.when(j == G - 1)
        def _():
            dg = jnp.zeros((1, D), F32)
            for c0 in range(0, tm, ch):
                part_rows = pl.ds(pl.multiple_of(i * tm + c0, ch), ch)
                dx, dgc = _rms_bwd(acc_ref[part_rows, :], x_ref[c0:c0 + ch, :], g_ref[0], dy_ref[c0:c0 + ch, :])
                dx_ref[c0:c0 + ch, :] = dx
                dg = dg + dgc
            dg_ref[...] += dg

    last = pl.BlockSpec((tm, D), lambda j, i: (jnp.where(j == G - 1, i, 0), 0))
    return _pallas_call(
        body, name=name, grid=(G, ni),
        in_specs=[d_spec(tm), pl.BlockSpec((1, 1, D, c), lambda j, i: (l, j, 0, 0)), last, pl.BlockSpec((1, 1, D), lambda j, i: (l, 0, 0)), last],
        out_specs=[last, pl.BlockSpec((1, D), lambda j, i: (0, 0))],
        out_shape=[jax.ShapeDtypeStruct((T, D), F32), jax.ShapeDtypeStruct((1, D), F32)],
        scratch_shapes=[pltpu.VMEM((T, D), F32)],
        compiler_params=_cp("arbitrary", "arbitrary"),
    )(dact, w, x, g, dy)


def mix_out_bwd(dy, proj, yp, yl, wpu, wlu, wout, P, R, l, deps=()):
    T, D = dy.shape
    tm = _tile(T, ROW_TILE)
    gb = (P + 2 * R) // D

    def body(dy_ref, gp_ref, gl_ref, yp_ref, yl_ref, wpu_ref, wlu_ref, wo_ref, *rest):
        dyb_ref, dyp_ref, dyl_ref, dgp_ref, dgl_ref, dpm_ref, dhl_ref = rest[len(deps):]
        dyb = dy_ref[...].astype(BF16)
        dyb_ref[...] = dyb
        dz = _dot_nt(dyb, wo_ref[0])
        sp = _sigmoid(gp_ref[...].astype(F32))
        sl = _sigmoid(gl_ref[...].astype(F32))
        dgp_ref[...] = (dz * yp_ref[...].astype(F32) * sp * (1.0 - sp)).astype(BF16)
        dgl_ref[...] = (dz * yl_ref[...].astype(F32) * sl * (1.0 - sl)).astype(BF16)
        dyp = (dz * sp).astype(BF16)
        dyl = (dz * sl).astype(BF16)
        dyp_ref[...] = dyp
        dyl_ref[...] = dyl
        dpm_ref[...] = _dot_nt(dyp, wpu_ref[0]).astype(BF16)
        dhl_ref[...] = _dot_nt(dyl, wlu_ref[0]).astype(BF16)

    row = lambda w: pl.BlockSpec((tm, w), lambda i: (i, 0))
    return _pallas_call(
        body, name="mix_out_bwd", grid=(T // tm,),
        in_specs=[row(D), pl.BlockSpec((tm, D), lambda i: (i, gb)), pl.BlockSpec((tm, D), lambda i: (i, gb + 1)), row(D), row(D),
                  pl.BlockSpec((1, P, D), lambda i: (l, 0, 0)), pl.BlockSpec((1, R, D), lambda i: (l, 0, 0)),
                  pl.BlockSpec((1, D, D), lambda i: (l, 0, 0))] + [ANY] * len(deps),
        out_specs=[row(D)] * 5 + [row(P), row(R)],
        out_shape=[jax.ShapeDtypeStruct((T, D), BF16)] * 5 + [jax.ShapeDtypeStruct((T, P), BF16), jax.ShapeDtypeStruct((T, R), BF16)],
        compiler_params=_cp("parallel"),
    )(dy, proj, proj, yp, yl, wpu, wlu, wout, *deps)


def lru_bwd(proj, hs, dhl, cw, cb, wa, ba, wx, bx, lam, P, l):
    T = proj.shape[0]
    _, H, hd, _ = wa.shape
    R = H * hd
    CW = cw.shape[1]

    def body(u_ref, ug_ref, cw_ref, cb_ref, wa_ref, ba_ref, wx_ref, bx_ref, lam_ref, hs_ref, dhl_ref,
             du_ref, dug_ref, dcw_ref, dcb_ref, dwa_ref, dba_ref, dwx_ref, dbx_ref, dlam_ref, c_s, g_s, y_s):
        u = u_ref[...].astype(F32)
        v = _conv(u, cw_ref, cb_ref[0])
        lam = lam_ref[0]
        r, i, sp, a, mult, inv_mult = _lru_gates(v, wa_ref, ba_ref[0], wx_ref, bx_ref[0], lam)
        ug = ug_ref[...].astype(F32)
        ge, th = _gelu(ug)
        hs = hs_ref[...]
        dhl = dhl_ref[...].astype(F32)
        dug_ref[...] = (dhl * hs * _gelu_grad(ug, th)).astype(BF16)
        c_s[...] = _shift_up(a, 1)
        g_s[...] = dhl * ge
        _scan_bwd(c_s, g_s, y_s)
        y = y_s[...]
        da = y * _shift_down(hs, 1)
        iv = i * v
        dlog_a = da * a - (y * iv) * (a * a) * inv_mult
        div = y * mult
        dpa = (dlog_a * (-LRU_C) * sp) * r * (1.0 - r)
        dpx = (div * v) * i * (1.0 - i)
        dsp = jnp.sum(dlog_a * (-LRU_C) * r, axis=0, keepdims=True)
        dlam_ref[0] = -dsp * _sigmoid(-lam)
        vb = v.astype(BF16)
        dpab, dpxb = dpa.astype(BF16), dpx.astype(BF16)
        dwa_ref[0, 0] = _dot_tn(vb, dpab)
        dwx_ref[0, 0] = _dot_tn(vb, dpxb)
        dba_ref[0] = jnp.sum(dpa, axis=0, keepdims=True)
        dbx_ref[0] = jnp.sum(dpx, axis=0, keepdims=True)
        dv = div * i + _dot_nt(dpab, wa_ref[0, 0].astype(BF16)) + _dot_nt(dpxb, wx_ref[0, 0].astype(BF16))
        dcb_ref[0] = jnp.sum(dv, axis=0, keepdims=True)
        du = jnp.zeros_like(dv)
        for k in range(CW):
            du = du + cw_ref[0, k:k + 1, :] * _shift_up(dv, CW - 1 - k)
            dcw_ref[0, k:k + 1, :] = jnp.sum(dv * _shift_down(u, CW - 1 - k), axis=0, keepdims=True)
        du_ref[...] = du.astype(BF16)

    col = pl.BlockSpec((T, hd), lambda h: (0, h))
    vec = pl.BlockSpec((1, 1, hd), lambda h: (0, 0, h))
    mat = pl.BlockSpec((1, 1, hd, hd), lambda h: (0, h, 0, 0))
    vshape = jax.ShapeDtypeStruct((1, 1, R), F32)
    mshape = jax.ShapeDtypeStruct((1, H, hd, hd), F32)
    return _pallas_call(
        body, name="lru_bwd", grid=(H,),
        in_specs=_lru_specs(T, hd, P, R, CW, l) + [col, col],
        out_specs=[col, col, pl.BlockSpec((1, CW, hd), lambda h: (0, 0, h)), vec, mat, vec, mat, vec, vec],
        out_shape=[jax.ShapeDtypeStruct((T, R), BF16)] * 2 + [jax.ShapeDtypeStruct((1, CW, R), F32), vshape, mshape, vshape, mshape, vshape, vshape],
        scratch_shapes=[pltpu.VMEM((T, hd), F32)] * 3,
        compiler_params=_cp("parallel"),
    )(proj, proj, cw, cb, wa, ba, wx, bx, lam, hs, dhl)


def pool_bwd(proj, dpm, pw, pb, ps, l):
    T = proj.shape[0]
    _, G, gd, _ = pw.shape
    P = G * gd

    def body(u_ref, d_ref, w_ref, b_ref, s_ref, du_ref, dw_ref, db_ref, dsc_ref):
        for gi in range(G):
            cols = slice(gi * gd, (gi + 1) * gd)
            w = POOL_WINDOWS[gi]
            inv = _inv_count(T, w)
            ug = u_ref[:, cols].astype(F32)
            pooled = _pooled(ug, w, inv).astype(BF16)
            wb = w_ref[0, gi].astype(BF16)
            mixed = _dot(pooled, wb) + b_ref[0, :, cols]
            dpm_g = d_ref[:, cols].astype(F32)
            dsc_ref[0, :, cols] = jnp.sum(dpm_g * mixed, axis=0, keepdims=True)
            dmixed = dpm_g * s_ref[0, :, cols]
            db_ref[0, :, cols] = jnp.sum(dmixed, axis=0, keepdims=True)
            dmb = dmixed.astype(BF16)
            dw_ref[0, gi] = _dot_tn(pooled, dmb)
            dpooled = _dot_nt(dmb, wb)
            s = dpooled * inv
            k = 1
            while k < w:
                s = s + _shift_up(s, k)
                k *= 2
            du_ref[:, cols] = (s - dpooled).astype(BF16)

    vec = pl.BlockSpec((1, 1, P), lambda i: (l, 0, 0))
    ovec = pl.BlockSpec((1, 1, P), lambda i: (0, 0, 0))
    return _pallas_call(
        body, name="pool_bwd", grid=(1,),
        in_specs=[pl.BlockSpec((T, P), lambda i: (0, 0)), pl.BlockSpec((T, P), lambda i: (0, 0)),
                  pl.BlockSpec((1, G, gd, gd), lambda i: (l, 0, 0, 0)), vec, vec],
        out_specs=[pl.BlockSpec((T, P), lambda i: (0, 0)), pl.BlockSpec((1, G, gd, gd), lambda i: (0, 0, 0, 0)), ovec, ovec],
        out_shape=[jax.ShapeDtypeStruct((T, P), BF16), jax.ShapeDtypeStruct((1, G, gd, gd), F32),
                   jax.ShapeDtypeStruct((1, 1, P), F32), jax.ShapeDtypeStruct((1, 1, P), F32)],
        compiler_params=_cp("arbitrary"),
    )(proj, dpm, pw, pb, ps)


def _place():
    x, y, c = lax.axis_index("x"), lax.axis_index("y"), lax.axis_index("c")
    return x, y, c


def all_gather(name, shards):
    n = len(shards)

    def body(*refs):
        src, out = refs[:n], refs[n:2 * n]
        send_sems, recv_sems, local_sems = refs[2 * n:]
        x, y, c = _place()
        sibling = (x, y, 1 - c)
        chips = [(x, 1 - y), (1 - x, y), (1 - x, 1 - y)]

        def slot(a, px, py, pc):
            return out[a].at[:, 4 * px + 2 * py + pc]

        def copy(a, k, block, to, from_src=False):
            return pltpu.make_async_remote_copy(
                src_ref=src[a] if from_src else slot(a, *block), dst_ref=slot(a, *block),
                send_sem=send_sems.at[a, k], recv_sem=recv_sems.at[a, k], device_id=to, device_id_type=MESH)

        me = (x, y, c)
        mine = [pltpu.make_async_copy(src[a], slot(a, *me), local_sems.at[a]) for a in range(n)]
        first = []
        for j, chip in enumerate(chips):
            for a in range(n):
                first.append(copy(a, 1 + j, me, (*chip, c), from_src=True))
        for a in range(n):
            first.append(copy(a, 0, me, sibling, from_src=True))
        for cp in mine + first:
            cp.start()
        passed = []
        for j, chip in enumerate(chips):
            for a in range(n):
                copy(a, 1 + j, (*chip, c), me).wait_recv()
                fwd = copy(a, 4 + j, (*chip, c), sibling)
                fwd.start()
                passed.append(fwd)
        for a in range(n):
            copy(a, 0, (x, y, 1 - c), me).wait_recv()
        for j, chip in enumerate(chips):
            for a in range(n):
                copy(a, 4 + j, (*chip, 1 - c), me).wait_recv()
        for cp in first + passed:
            cp.wait_send()
        for cp in mine:
            cp.wait()

    outs = _pallas_call(
        body, name=name,
        in_specs=[ANY] * n, out_specs=[ANY] * n,
        out_shape=[jax.ShapeDtypeStruct((s.shape[0], N_DEV) + s.shape[1:], s.dtype) for s in shards],
        scratch_shapes=[pltpu.SemaphoreType.DMA((n, 7)), pltpu.SemaphoreType.DMA((n, 7)), pltpu.SemaphoreType.DMA((n,))],
        compiler_params=pltpu.CompilerParams(has_side_effects=True),
    )(*shards)
    return list(outs)


def pair_exchange(name, g16, layer=None):
    n = len(g16)
    layers = slice(None) if layer is None else pl.ds(layer, 1)

    def body(*refs):
        s16, recv = refs[:n], refs[n:2 * n]
        send_sems, recv_sems = refs[2 * n:]
        x, y, c = _place()
        sibling = (x, y, 1 - c)
        rem = []
        for a in range(n):
            for j in range(N_CHIP):
                rem.append(pltpu.make_async_remote_copy(
                    src_ref=s16[a].at[layers, 2 * j + 1 - c], dst_ref=recv[a].at[:, j],
                    send_sem=send_sems.at[a, j], recv_sem=recv_sems.at[a, j], device_id=sibling, device_id_type=MESH))
        for cp in rem:
            cp.start()
        for cp in rem:
            cp.wait_recv()
        for cp in rem:
            cp.wait_send()

    outs = _pallas_call(
        body, name=name,
        in_specs=[ANY] * n, out_specs=[ANY] * n,
        out_shape=[jax.ShapeDtypeStruct((s.shape[0] if layer is None else 1, N_CHIP) + s.shape[2:], s.dtype) for s in g16],
        scratch_shapes=[pltpu.SemaphoreType.DMA((n, N_CHIP))] * 2,
        compiler_params=pltpu.CompilerParams(has_side_effects=True),
    )(*g16)
    return list(outs)


def chip_exchange(name, pair16):
    n = len(pair16)

    def body(*refs):
        p16, recv2 = refs[:n], refs[n:2 * n]
        send_sems, recv_sems = refs[2 * n:]
        x, y, c = _place()
        rem = []
        for d in (1, 2, 3):
            px = 1 - x if d & 2 else x
            py = 1 - y if d & 1 else y
            for a in range(n):
                rem.append(pltpu.make_async_remote_copy(
                    src_ref=p16[a].at[:, 2 * px + py], dst_ref=recv2[a].at[:, d - 1],
                    send_sem=send_sems.at[a, d - 1], recv_sem=recv_sems.at[a, d - 1], device_id=(px, py, c), device_id_type=MESH))
        for cp in rem:
            cp.start()
        for cp in rem:
            cp.wait_recv()
        for cp in rem:
            cp.wait_send()

    outs = _pallas_call(
        body, name=name,
        in_specs=[ANY] * n, out_specs=[ANY] * n,
        out_shape=[jax.ShapeDtypeStruct((s.shape[0], 3) + s.shape[2:], s.dtype) for s in pair16],
        scratch_shapes=[pltpu.SemaphoreType.DMA((n, 3))] * 2,
        compiler_params=pltpu.CompilerParams(has_side_effects=True),
    )(*pair16)
    return list(outs)


HBM = pl.BlockSpec(memory_space=pltpu.HBM)
SEM = pl.BlockSpec(memory_space=pltpu.SEMAPHORE)
EFFECT = pltpu.SideEffectType.DATAFLOW_SIDE_EFFECTING


def _in_hbm(a):
    return pltpu.with_memory_space_constraint(a, pltpu.HBM)


def split_start(name, bufs, n_copies, copies_of):
    nb = len(bufs)

    def body(*refs):
        buf = refs[:nb]
        send_sems, recv_sems = refs[nb], refs[nb + 1]
        token = refs[-1]
        for i, (src, dst, dev) in enumerate(copies_of(buf)):
            pltpu.make_async_remote_copy(src_ref=src, dst_ref=dst, send_sem=send_sems.at[i], recv_sem=recv_sems.at[i],
                                         device_id=dev, device_id_type=MESH).start()
        token[...] = jnp.zeros_like(token)

    outs = _pallas_call(
        body, name=name,
        in_specs=[HBM] * nb,
        out_specs=(SEM, SEM, *([HBM] * nb), pl.BlockSpec(memory_space=pltpu.VMEM)),
        out_shape=(pltpu.SemaphoreType.DMA((n_copies,)), pltpu.SemaphoreType.DMA((n_copies,)),
                   *[pltpu.HBM(b.shape, b.dtype) for b in bufs], jax.ShapeDtypeStruct((8, 128), F32)),
        input_output_aliases={i: 2 + i for i in range(nb)},
        compiler_params=pltpu.CompilerParams(has_side_effects=EFFECT),
    )(*[_in_hbm(b) for b in bufs])
    return outs[0], outs[1], list(outs[2:2 + nb]), outs[-1]


def split_wait(name, bufs, send_sems, recv_sems, after, copies_of):
    nb = len(bufs)

    def body(*refs):
        buf = refs[:nb]
        send, recv = refs[nb], refs[nb + 1]
        for i, (src, dst, dev) in enumerate(copies_of(buf)):
            cp = pltpu.make_async_remote_copy(src_ref=src, dst_ref=dst, send_sem=send.at[i], recv_sem=recv.at[i],
                                              device_id=dev, device_id_type=MESH)
            cp.wait_send()
            cp.wait_recv()

    outs = _pallas_call(
        body, name=name,
        in_specs=[HBM] * nb + [SEM, SEM] + [ANY] * len(after),
        out_specs=[HBM] * nb,
        out_shape=[pltpu.HBM(b.shape, b.dtype) for b in bufs],
        input_output_aliases={i: i for i in range(nb)},
        compiler_params=pltpu.CompilerParams(has_side_effects=EFFECT),
    )(*bufs, send_sems, recv_sems, *after)
    return list(outs)


def place_own(w, l, place, dtype):
    _, rows, cols = w.shape
    tr = _rows_tile(rows, cols, 1 << 19)

    def body(p_ref, w_ref, o_ref):
        o_ref[0] = w_ref[0].astype(dtype)

    return _pallas_call(
        body, name="place_own",
        grid_spec=pltpu.PrefetchScalarGridSpec(
            num_scalar_prefetch=1, grid=(rows // tr,),
            in_specs=[pl.BlockSpec((1, tr, cols), lambda i, p: (l, i, 0))],
            out_specs=pl.BlockSpec((1, tr, cols), lambda i, p: (p[2], i, 0))),
        out_shape=jax.ShapeDtypeStruct((N_DEV, rows, cols), dtype), compiler_params=_cp("parallel"),
    )(place, w)


def _gather_copies(land):
    x, y, c = _place()
    k = 4 * x + 2 * y + c
    peers = [(x, 1 - y, c), (1 - x, y, c), (1 - x, 1 - y, c), (x, y, 1 - c)]
    return [(b.at[k], b.at[k], p) for p in peers for b in land]


def gather_start(name, land):
    return split_start(name, land, 4 * len(land), _gather_copies)


def gather_wait(name, land, send_sems, recv_sems, after):
    return split_wait(name, land, send_sems, recv_sems, after, _gather_copies)


def gather_forward(name, land):
    n = len(land)

    def body(*refs):
        buf = refs[:n]
        send_sems, recv_sems = refs[2 * n:]
        x, y, c = _place()
        rem = []
        for j, (px, py) in enumerate([(x, 1 - y), (1 - x, y), (1 - x, 1 - y)]):
            k = 4 * px + 2 * py + c
            for a in range(n):
                rem.append(pltpu.make_async_remote_copy(
                    src_ref=buf[a].at[k], dst_ref=buf[a].at[k], send_sem=send_sems.at[a, j], recv_sem=recv_sems.at[a, j],
                    device_id=(x, y, 1 - c), device_id_type=MESH))
        for cp in rem:
            cp.start()
        for cp in rem:
            cp.wait_recv()
        for cp in rem:
            cp.wait_send()

    outs = _pallas_call(
        body, name=name,
        in_specs=[ANY] * n, out_specs=[ANY] * n,
        out_shape=[jax.ShapeDtypeStruct(b.shape, b.dtype) for b in land],
        scratch_shapes=[pltpu.SemaphoreType.DMA((n, 3))] * 2,
        input_output_aliases={i: i for i in range(n)},
        compiler_params=pltpu.CompilerParams(has_side_effects=True),
    )(*land)
    return list(outs)


def _chip_copies(nsrc):
    def copies(buf):
        p16, recv2 = buf[:nsrc], buf[nsrc:]
        x, y, c = _place()
        out = []
        for d in (1, 2, 3):
            px = 1 - x if d & 2 else x
            py = 1 - y if d & 1 else y
            out += [(p16[a].at[:, 2 * px + py], recv2[a].at[:, d - 1], (px, py, c)) for a in range(nsrc)]
        return out
    return copies


def _pair_copies(nsrc):
    def copies(buf):
        g16, recv = buf[:nsrc], buf[nsrc:]
        x, y, c = _place()
        return [(g16[a].at[:, 2 * j + 1 - c], recv[a].at[:, j], (x, y, 1 - c)) for a in range(nsrc) for j in range(N_CHIP)]
    return copies


def pair_exchange_start(name, g16):
    n = len(g16)
    land = [lax.empty((1, N_CHIP) + s.shape[2:], s.dtype) for s in g16]
    return split_start(name, list(g16) + land, N_CHIP * n, _pair_copies(n))


def pair_exchange_wait(name, bufs, send_sems, recv_sems, after):
    n = len(bufs) // 2
    return split_wait(name, bufs, send_sems, recv_sems, after, _pair_copies(n))[n:]


def chip_exchange_start(name, pair16):
    n = len(pair16)
    land = [lax.empty((s.shape[0], 3) + s.shape[2:], s.dtype) for s in pair16]
    return split_start(name, list(pair16) + land, 3 * n, _chip_copies(n))


def chip_exchange_wait(name, bufs, send_sems, recv_sems, after):
    n = len(bufs) // 2
    return split_wait(name, bufs, send_sems, recv_sems, after, _chip_copies(n))[n:]


def _rows_tile(rows, cols, budget=1 << 20):
    t = rows
    while t % 2 == 0 and t * cols > budget and (t // 2) % 16 == 0:
        t //= 2
    return t


def pair_sum(g32, recv1, place, l):
    _, _, rows, cols = recv1.shape
    tr = _rows_tile(rows, cols)

    def body(p_ref, m_ref, r_ref, o_ref):
        o_ref[...] = (m_ref[...] + r_ref[...].astype(F32)).astype(o_ref.dtype)

    blk = pl.BlockSpec((1, 1, tr, cols), lambda j, i, p: (0, j, i, 0))
    return _pallas_call(
        body, name="pair_sum",
        grid_spec=pltpu.PrefetchScalarGridSpec(
            num_scalar_prefetch=1, grid=(N_CHIP, rows // tr),
            in_specs=[pl.BlockSpec((1, 1, tr, cols), lambda j, i, p: (l, 2 * j + p[0], i, 0)), blk], out_specs=blk),
        out_shape=jax.ShapeDtypeStruct(recv1.shape, recv1.dtype), compiler_params=_cp("parallel", "parallel"),
    )(place, g32, recv1)


def _grad_in_specs(tr, cols, l):
    return ([pl.BlockSpec((1, 1, tr, cols), lambda i, p: (l, p[2], i, 0)), pl.BlockSpec((1, 1, tr, cols), lambda i, p: (0, p[1], i, 0))]
            + [pl.BlockSpec((1, 1, tr, cols), lambda i, p, d=d: (0, d, i, 0)) for d in range(3)])


def _grad_total(o32, o16, r0, r1, r2):
    return (o32[0, 0] + o16[0, 0].astype(F32)) + r0[0, 0].astype(F32) + r1[0, 0].astype(F32) + r2[0, 0].astype(F32)


def grad_sum(g32, recv1, recv2, place):
    _, _, rows, cols = recv1.shape
    tr = _rows_tile(rows, cols)

    def body(p_ref, o32, o16, r0, r1, r2, g_ref):
        g_ref[...] = _grad_total(o32, o16, r0, r1, r2)

    return _pallas_call(
        body, name="grad_sum",
        grid_spec=pltpu.PrefetchScalarGridSpec(
            num_scalar_prefetch=1, grid=(rows // tr,), in_specs=_grad_in_specs(tr, cols, 0),
            out_specs=pl.BlockSpec((tr, cols), lambda i, p: (i, 0))),
        out_shape=jax.ShapeDtypeStruct((rows, cols), F32), compiler_params=_cp("parallel"),
    )(place, g32, recv1, recv2, recv2, recv2)


def _adamw_math(w, g, m, v):
    m = ADAM_B1 * m + (1.0 - ADAM_B1) * g
    v = ADAM_B2 * v + (1.0 - ADAM_B2) * (g * g)
    m_hat = m / (1.0 - ADAM_B1 ** ADAM_STEP)
    v_hat = v / (1.0 - ADAM_B2 ** ADAM_STEP)
    delta = -ADAM_LR * (m_hat / (jnp.sqrt(v_hat) + ADAM_EPS) + ADAM_WD * w)
    return delta, m, v


def grad_sum_adamw(g32, recv1, recv2, w, m, v, place, l, prev):
    L, rows, cols = w.shape
    tr = _rows_tile(rows, cols, 1 << 18)

    def body(p_ref, o32, o16, r0, r1, r2, w_ref, m_ref, v_ref, *rest):
        g_ref, d_ref, nm_ref, nv_ref = rest[-4:]
        g = _grad_total(o32, o16, r0, r1, r2)
        d, nm, nv = _adamw_math(w_ref[0], g, m_ref[0], v_ref[0])
        g_ref[0] = g
        d_ref[0] = d
        nm_ref[0] = nm
        nv_ref[0] = nv

    blk = pl.BlockSpec((1, tr, cols), lambda i, p: (l, i, 0))
    args = [g32, recv1, recv2, recv2, recv2, w, m, v]
    in_specs = _grad_in_specs(tr, cols, 0) + [blk] * 3
    aliases = {}
    if prev is not None:
        aliases = {1 + len(args) + k: k for k in range(4)}
        args += list(prev)
        in_specs += [ANY] * 4
    return _pallas_call(
        body, name="grad_sum_adamw",
        grid_spec=pltpu.PrefetchScalarGridSpec(num_scalar_prefetch=1, grid=(rows // tr,), in_specs=in_specs, out_specs=[blk] * 4),
        out_shape=[jax.ShapeDtypeStruct((L, rows, cols), F32)] * 4, input_output_aliases=aliases,
        compiler_params=_cp("parallel"),
    )(place, *args)


def adamw(w, g, m, v):
    rows, cols = w.shape
    tr = _rows_tile(rows, cols, 1 << 18)

    def body(w_ref, g_ref, m_ref, v_ref, d_ref, nm_ref, nv_ref):
        d, nm, nv = _adamw_math(w_ref[...], g_ref[...], m_ref[...], v_ref[...])
        d_ref[...] = d
        nm_ref[...] = nm
        nv_ref[...] = nv

    blk = pl.BlockSpec((tr, cols), lambda i: (i, 0))
    return _pallas_call(body, name="adamw_small", grid=(rows // tr,), in_specs=[blk] * 4, out_specs=[blk] * 3,
                        out_shape=[jax.ShapeDtypeStruct((rows, cols), F32)] * 3, compiler_params=_cp("parallel"))(w, g, m, v)


SMALL = ("norm_ffn1", "norm_mix", "pool_w", "pool_b", "pool_scale", "conv_w", "conv_b", "lru_w_a", "lru_b_a", "lru_w_x", "lru_b_x",
         "lru_lambda", "norm_ffn2", "final_norm")
BIG = ("ffn1_w_up", "ffn1_w_down", "w_in", "w_pool_up", "w_lru_up", "w_out", "ffn2_w_up", "ffn2_w_down")
NAMES = ("norm_ffn1", "ffn1_w_up", "ffn1_w_down", "norm_mix", "w_in", "pool_w", "pool_b", "pool_scale", "w_pool_up", "conv_w", "conv_b",
         "lru_w_a", "lru_b_a", "lru_w_x", "lru_b_x", "lru_lambda", "w_lru_up", "w_out", "norm_ffn2", "ffn2_w_up", "ffn2_w_down", "final_norm")
SUBLAYERS = (("ffn1_w_up", "ffn1_w_down"), ("w_in", "w_pool_up", "w_lru_up", "w_out", "conv_w"), ("ffn2_w_up", "ffn2_w_down"))
PACK_ROWS = 16 * N_DEV


def _pack(parts):
    flat = jnp.concatenate([p.reshape(-1) for p in parts])
    unit = 128 * PACK_ROWS
    padded = -(-flat.size // unit) * unit
    return jnp.pad(flat, (0, padded - flat.size)).reshape(-1, 128)


def _unpack(packed, shapes):
    flat = packed.reshape(-1)
    out, off = [], 0
    for s in shapes:
        n = 1
        for d in s:
            n *= d
        out.append(flat[off:off + n].reshape(s))
        off += n
    return out


def kernel(x, norm_ffn1, ffn1_w_up, ffn1_w_down, norm_mix, w_in, pool_w, pool_b, pool_scale, w_pool_up, conv_w, conv_b, lru_w_a, lru_b_a, lru_w_x, lru_b_x, lru_lambda, w_lru_up, w_out, norm_ffn2, ffn2_w_up, ffn2_w_down, final_norm, loss_target, m_norm_ffn1, m_ffn1_w_up, m_ffn1_w_down, m_norm_mix, m_w_in, m_pool_w, m_pool_b, m_pool_scale, m_w_pool_up, m_conv_w, m_conv_b, m_lru_w_a, m_lru_b_a, m_lru_w_x, m_lru_b_x, m_lru_lambda, m_w_lru_up, m_w_out, m_norm_ffn2, m_ffn2_w_up, m_ffn2_w_down, m_final_norm, v_norm_ffn1, v_ffn1_w_up, v_ffn1_w_down, v_norm_mix, v_w_in, v_pool_w, v_pool_b, v_pool_scale, v_w_pool_up, v_conv_w, v_conv_b, v_lru_w_a, v_lru_b_a, v_lru_w_x, v_lru_b_x, v_lru_lambda, v_w_lru_up, v_w_out, v_norm_ffn2, v_ffn2_w_up, v_ffn2_w_down, v_final_norm):
    W = dict(norm_ffn1=norm_ffn1, ffn1_w_up=ffn1_w_up, ffn1_w_down=ffn1_w_down, norm_mix=norm_mix, w_in=w_in, pool_w=pool_w, pool_b=pool_b,
             pool_scale=pool_scale, w_pool_up=w_pool_up, conv_w=conv_w, conv_b=conv_b, lru_w_a=lru_w_a, lru_b_a=lru_b_a, lru_w_x=lru_w_x,
             lru_b_x=lru_b_x, lru_lambda=lru_lambda, w_lru_up=w_lru_up, w_out=w_out, norm_ffn2=norm_ffn2, ffn2_w_up=ffn2_w_up,
             ffn2_w_down=ffn2_w_down, final_norm=final_norm)
    M = dict(norm_ffn1=m_norm_ffn1, ffn1_w_up=m_ffn1_w_up, ffn1_w_down=m_ffn1_w_down, norm_mix=m_norm_mix, w_in=m_w_in, pool_w=m_pool_w,
             pool_b=m_pool_b, pool_scale=m_pool_scale, w_pool_up=m_w_pool_up, conv_w=m_conv_w, conv_b=m_conv_b, lru_w_a=m_lru_w_a,
             lru_b_a=m_lru_b_a, lru_w_x=m_lru_w_x, lru_b_x=m_lru_b_x, lru_lambda=m_lru_lambda, w_lru_up=m_w_lru_up, w_out=m_w_out,
             norm_ffn2=m_norm_ffn2, ffn2_w_up=m_ffn2_w_up, ffn2_w_down=m_ffn2_w_down, final_norm=m_final_norm)
    V = dict(norm_ffn1=v_norm_ffn1, ffn1_w_up=v_ffn1_w_up, ffn1_w_down=v_ffn1_w_down, norm_mix=v_norm_mix, w_in=v_w_in, pool_w=v_pool_w,
             pool_b=v_pool_b, pool_scale=v_pool_scale, w_pool_up=v_w_pool_up, conv_w=v_conv_w, conv_b=v_conv_b, lru_w_a=v_lru_w_a,
             lru_b_a=v_lru_b_a, lru_w_x=v_lru_w_x, lru_b_x=v_lru_b_x, lru_lambda=v_lru_lambda, w_lru_up=v_w_lru_up, w_out=v_w_out,
             norm_ffn2=v_norm_ffn2, ffn2_w_up=v_ffn2_w_up, ffn2_w_down=v_ffn2_w_down, final_norm=v_final_norm)

    T, D = x.shape[1], x.shape[2]
    L = norm_ffn1.shape[0]
    P = pool_scale.shape[1]
    R = lru_lambda.shape[1]
    H, hd = lru_w_a.shape[1], lru_w_a.shape[2]
    CW = conv_w.shape[1]
    cs = ffn1_w_up.shape[2]
    ci = w_in.shape[2]
    xin = x.reshape(T, D)
    tgt = loss_target.reshape(T, D)
    dev = 4 * lax.axis_index("x") + 2 * lax.axis_index("y") + lax.axis_index("c")
    place = jnp.stack([lax.axis_index("c"), 2 * lax.axis_index("x") + lax.axis_index("y"), dev]).astype(jnp.int32)

    cw_flat = conv_w.reshape(L, -1)
    cw_pad = (-cw_flat.shape[1]) % 1024
    cw_tiles = jnp.pad(cw_flat, ((0, 0), (0, cw_pad))).reshape(L, -1, 128)

    def units(l):
        return SUBLAYERS if l == 0 else (tuple(n for u in SUBLAYERS for n in u),)

    def gather_units_start(l):
        started_units = []
        for k, names in enumerate(units(l)):
            land = [place_own(cw_tiles, l, place, F32) if n == "conv_w" else place_own(W[n], l, place, BF16) for n in names]
            started_units.append((names, f"l{l}_u{k}") + gather_start(f"gather_start_l{l}_u{k}", land))
        return started_units

    def gather_unit_finish(unit, after):
        names, tag, send_sems, recv_sems, land, _tok = unit
        land = gather_wait(f"gather_wait_{tag}", land, send_sems, recv_sems, after)
        land = gather_forward(f"gather_forward_{tag}", land)
        g = dict(zip(names, land))
        one = lambda a: a.reshape((1,) + a.shape)
        w = {}
        for tag_, up, dn in (("1", "ffn1_w_up", "ffn1_w_down"), ("2", "ffn2_w_up", "ffn2_w_down")):
            if up in g:
                w["wup" + tag_], w["wd" + tag_] = one(g[up]), g[dn].reshape(1, -1, D)
        if "w_in" in g:
            cw_l = g["conv_w"].reshape(N_DEV, -1)[:, :cw_flat.shape[1]].reshape((N_DEV,) + conv_w.shape[1:])
            w.update(win=one(g["w_in"]), wlu=g["w_lru_up"].reshape(1, R, D), wout=g["w_out"].reshape(1, D, D),
                     wpu=g["w_pool_up"].transpose(1, 0, 2).reshape(1, P, D),
                     cw=cw_l.transpose(1, 0, 2).reshape(1, CW, R))
        return w

    def layer_params(l):
        vec = lambda a: a[l:l + 1].reshape(1, 1, -1)
        return dict(g1=vec(norm_ffn1), gm=vec(norm_mix), g2=vec(norm_ffn2), pb=vec(pool_b), ps=vec(pool_scale), cb=vec(conv_b),
                    ba=vec(lru_b_a), bx=vec(lru_b_x), lam=vec(lru_lambda), pw=pool_w[l:l + 1], wa=lru_w_a[l:l + 1], wx=lru_w_x[l:l + 1])

    AHEAD = 2
    started = {l: gather_units_start(l) for l in range(min(AHEAD, L))}
    saved, LW, LP = [], [], []
    xc = xin
    for l in range(L):
        todo = started.pop(l)
        w, p = {}, layer_params(l)

        def need(key, xc_now):
            while key not in w:
                tokens = [u[-1] for us in started.values() for u in us] + [u[-1] for u in todo[1:]]
                w.update(gather_unit_finish(todo.pop(0), [xc_now] + tokens))

        need("wup1", xc)
        if l + AHEAD < L:
            started[l + AHEAD] = gather_units_start(l + AHEAD)
        sv = {"x1": xc}
        sv["h1"], sv["u1"], sv["s1"] = ffn_up(xc, p["g1"], w["wup1"], 0)
        xc = ffn_down(sv["s1"], w["wd1"], xc, 0)
        sv["x2"] = xc
        need("win", xc)
        sv["h2"], sv["proj"] = mix_in(xc, p["gm"], w["win"], 0)
        sv["pm"] = pool_fwd(sv["proj"], p["pw"], p["pb"], p["ps"], 0)
        sv["hl"], sv["hs"] = lru_fwd(sv["proj"], w["cw"], p["cb"], p["wa"], p["ba"], p["wx"], p["bx"], p["lam"], P, 0)
        xc, sv["yp"], sv["yl"], sv["z"] = mix_out(sv["pm"], sv["hl"], sv["proj"], xc, w["wpu"], w["wlu"], w["wout"], P, 0)
        sv["x3"] = xc
        need("wup2", xc)
        sv["h3"], sv["u3"], sv["s3"] = ffn_up(xc, p["g2"], w["wup2"], 0)
        xc = ffn_down(sv["s3"], w["wd2"], xc, 0)
        saved.append(sv)
        LW.append(w)
        LP.append(p)

    loss_part, dx, d_final = loss_head(xc, final_norm.reshape(1, D), tgt)
    loss = lax.psum(loss_part[0, 0], ("x", "y", "c"))

    G = [dict() for _ in range(L)]
    small = {n: [None] * L for n in SMALL if n != "final_norm"}

    def to_slots(name, pair):
        if name == "w_pool_up":
            return tuple(a.reshape(1, P, N_DEV, D // N_DEV).transpose(0, 2, 1, 3) for a in pair)
        return tuple(a.reshape((1, N_DEV) + W[name].shape[1:]) for a in pair)

    def ffn_bwd(dy, sv, tag, wup, wd, gn, up_name, dn_name, norm_name, l, deps=()):
        dout, du = ffn_down_bwd(dy, wd, sv["u" + tag], 0, deps)
        du = du.reshape(N_DEV, T, cs)
        G[l][dn_name] = to_slots(dn_name, dw_tn("dw_down", sv["s" + tag], lambda tk: pl.BlockSpec((1, tk, cs), lambda g, k: (g, k, 0)),
                                                dout, lambda tk: pl.BlockSpec((tk, D), lambda g, k: (k, 0)), 4, cs, D, T))
        G[l][up_name] = to_slots(up_name, dw_tn("dw_up", sv["h" + tag], lambda tk: pl.BlockSpec((tk, D), lambda g, k: (k, 0)),
                                                du, lambda tk: pl.BlockSpec((1, tk, cs), lambda g, k: (g, k, 0)), N_DEV, D, cs, T))
        dxn, dg = dx_norm_bwd("ffn_dx", du, lambda tm: pl.BlockSpec((1, tm, cs), lambda j, i: (j, i, 0)), wup, N_DEV,
                              sv["x" + tag], gn, dy, 0)
        small[norm_name][l] = dg.reshape(D)
        return dxn

    pairing, in_flight = [], []

    def reduce_start(l, names, tag):
        names = [n for n in names if n != "conv_w"]
        send_sems, recv_sems, bufs, tok = pair_exchange_start(f"rs_pair_start_{tag}", [G[l][n][1] for n in names])
        pairing.append((l, names, tag, send_sems, recv_sems, bufs))
        return (tok,)

    def reduce_continue(after):
        l, names, tag, send_sems, recv_sems, bufs = pairing.pop(0)
        recv1 = pair_exchange_wait(f"rs_pair_wait_{tag}", bufs, send_sems, recv_sems, after)
        pair16 = [pair_sum(G[l][n][0], r_, place, 0) for n, r_ in zip(names, recv1)]
        send_sems, recv_sems, bufs, tok = chip_exchange_start(f"rs_chip_start_{tag}", pair16)
        in_flight.append((l, names, tag, send_sems, recv_sems, bufs, recv1))
        return (tok,)

    def boundary(l, k, dx_now):
        deps = reduce_continue([dx_now]) if pairing else ()
        if len(units(l)) > 1:
            deps += reduce_start(l, units(l)[k], f"l{l}_u{k}")
        elif k == 0:
            deps += reduce_start(l, units(l)[0], f"l{l}_u0")
        return deps

    deps = ()
    for l in reversed(range(L)):
        sv, w, p = saved[l], LW[l], LP[l]
        dx = ffn_bwd(dx, sv, "3", w["wup2"], w["wd2"], p["g2"], "ffn2_w_up", "ffn2_w_down", "norm_ffn2", l, deps)
        deps = boundary(l, 2, dx)
        dyb, dyp, dyl, dgp, dgl, dpm, dhl = mix_out_bwd(dx, sv["proj"], sv["yp"], sv["yl"], w["wpu"], w["wlu"], w["wout"], P, R, 0, deps)
        row = lambda wd_: (lambda tk: pl.BlockSpec((tk, wd_), lambda g, k: (k, 0)))
        G[l]["w_out"] = to_slots("w_out", dw_tn("dw_out", sv["z"], row(D), dyb, row(D), 1, D, D, T))
        G[l]["w_lru_up"] = to_slots("w_lru_up", dw_tn("dw_lru_up", sv["hl"], row(R), dyl, row(D), 1, R, D, T))
        G[l]["w_pool_up"] = to_slots("w_pool_up", dw_tn("dw_pool_up", sv["pm"], row(P), dyp, row(D), 1, P, D, T))
        du_lru, du_gelu, dcw, dcb, dwa, dba, dwx, dbx, dlam = lru_bwd(
            sv["proj"], sv["hs"], dhl, w["cw"], p["cb"], p["wa"], p["ba"], p["wx"], p["bx"], p["lam"], P, 0)
        du_pool, dpw, dpb, dpsc = pool_bwd(sv["proj"], dpm, p["pw"], p["pb"], p["ps"], 0)
        dproj = jnp.concatenate([du_pool, du_lru, du_gelu, dgp, dgl], axis=1)
        G[l]["w_in"] = to_slots("w_in", dw_tn("dw_in", sv["h2"], row(D), dproj, lambda tk: pl.BlockSpec((tk, ci), lambda g, k: (k, g)),
                                              N_DEV, D, ci, T))
        dx, dgm = dx_norm_bwd("mix_dx", dproj, lambda tm: pl.BlockSpec((tm, ci), lambda j, i: (i, j)), w["win"], N_DEV,
                              sv["x2"], p["gm"], dx, 0)
        small["norm_mix"][l] = dgm.reshape(D)
        small["pool_w"][l], small["pool_b"][l], small["pool_scale"][l] = dpw[0], dpb.reshape(pool_b.shape[1:]), dpsc.reshape(P)
        small["conv_w"][l], small["conv_b"][l] = dcw[0], dcb.reshape(R)
        small["lru_w_a"][l], small["lru_b_a"][l] = dwa[0], dba.reshape(H, hd)
        small["lru_w_x"][l], small["lru_b_x"][l] = dwx[0], dbx.reshape(H, hd)
        small["lru_lambda"][l] = dlam.reshape(R)
        deps = boundary(l, 1, dx)
        dx = ffn_bwd(dx, sv, "1", w["wup1"], w["wd1"], p["g1"], "ffn1_w_up", "ffn1_w_down", "norm_ffn1", l, deps)
        deps = boundary(l, 0, dx)

    grad_x = dx.reshape(x.shape)

    small_parts = [jnp.stack(small[n]) for n in SMALL if n != "final_norm"] + [d_final.reshape(D)]
    small_shapes = [p.shape for p in small_parts]
    gpack = _pack(small_parts).reshape(1, N_DEV, -1, 128)
    recv1_s = pair_exchange("rs_pair_exchange_small", [gpack])[0]
    pair_s = pair_sum(gpack, recv1_s, place, 0)
    recv2_s = chip_exchange("rs_chip_exchange_small", [pair_s])[0]
    while pairing:
        reduce_continue([recv2_s])

    outs = {n: None for n in BIG}
    after = [dx, recv2_s]
    for l, names, tag, send_sems, recv_sems, bufs, recv1 in in_flight:
        recv2 = chip_exchange_wait(f"rs_chip_wait_{tag}", bufs, send_sems, recv_sems, after)
        for i, n in enumerate(names):
            outs[n] = grad_sum_adamw(G[l][n][0], recv1[i], recv2[i], W[n], M[n], V[n], place, l, outs[n])
        after = [outs[names[-1]][0]]
    out_g, out_d, out_m, out_v = ({n: outs[n][k] for n in BIG} for k in range(4))

    gs = grad_sum(gpack, recv1_s, recv2_s, place)
    gs_all = all_gather("all_gather_small_grads", [gs.reshape((1,) + gs.shape)])[0]
    gs_all = gs_all.reshape(-1, 128)
    small_g = dict(zip(SMALL, _unpack(gs_all, small_shapes)))
    full_shapes = [W[n].shape if n != "conv_w" else small_shapes[SMALL.index("conv_w")] for n in SMALL]
    rep = [n for n in SMALL if n != "conv_w"]
    rep_shapes = [W[n].shape for n in rep]
    wp, mp, vp = (_pack([S[n] for n in rep]) for S in (W, M, V))
    gp = _pack([small_g[n] for n in rep])
    dp, nmp, nvp = adamw(wp, gp, mp, vp)
    for S, packed in ((out_d, dp), (out_m, nmp), (out_v, nvp)):
        S.update(zip(rep, _unpack(packed, rep_shapes)))
    for n in rep:
        out_g[n] = small_g[n]
    cwc = conv_w.shape[2]
    gcw = lax.dynamic_slice_in_dim(small_g["conv_w"], dev * cwc, cwc, axis=2)
    cw2 = lambda a: a.reshape(-1, cwc)
    pad_rows = (-cw2(conv_w).shape[0]) % 8
    padr = lambda a: jnp.pad(cw2(a), ((0, pad_rows), (0, 0)))
    dcw_, mcw_, vcw_ = adamw(padr(conv_w), padr(gcw), padr(M["conv_w"]), padr(V["conv_w"]))
    nrow = cw2(conv_w).shape[0]
    out_g["conv_w"] = gcw
    out_d["conv_w"], out_m["conv_w"], out_v["conv_w"] = (a[:nrow].reshape(conv_w.shape) for a in (dcw_, mcw_, vcw_))
    del full_shapes

    return (loss, grad_x, *[out_g[n] for n in NAMES], *[out_d[n] for n in NAMES], *[out_m[n] for n in NAMES], *[out_v[n] for n in NAMES])
```

```python
import functools

import jax
import jax.numpy as jnp
from jax import lax
from jax.experimental import pallas as pl
from jax.experimental.pallas import tpu as pltpu

F32, BF16 = jnp.float32, jnp.bfloat16
EPS = 1e-6
LRU_C = 8.0
POOL_WINDOWS = (2, 4, 8, 16)
ADAM_LR, ADAM_B1, ADAM_B2, ADAM_EPS, ADAM_WD, ADAM_STEP = 0.001, 0.9, 0.999, 1e-08, 0.01, 10
N_DEV = 8
N_CHIP = 4
MESH = pl.DeviceIdType.MESH
V7X_VMEM_LIMIT = 56 * 1024 * 1024
ROW_TILE = 512
WIDE_TILE = 1024
SUM_TILE = 2048
ANY = pl.BlockSpec(memory_space=pl.ANY)

_pallas_call = pl.pallas_call


def _cp(*sem):
    return pltpu.CompilerParams(dimension_semantics=sem if sem else None, vmem_limit_bytes=V7X_VMEM_LIMIT)


def _tile(n, t):
    t = min(n, t)
    assert n % t == 0, (n, t)
    return t


def _dot(a, b):
    return jnp.dot(a, b, preferred_element_type=F32)


def _dot_nt(a, b):
    return lax.dot_general(a, b, (((1,), (1,)), ((), ())), preferred_element_type=F32)


def _dot_tn(a, b):
    return lax.dot_general(a, b, (((0,), (0,)), ((), ())), preferred_element_type=F32)


def _rms(xv):
    r = lax.rsqrt(jnp.mean(xv * xv, axis=-1, keepdims=True) + EPS)
    return xv * r, r


def _rms_bwd(dh, xv, gv, dy):
    n, r = _rms(xv)
    dn = dh * gv
    dx = dy + r * (dn - n * jnp.mean(dn * n, axis=-1, keepdims=True))
    return dx, jnp.sum(dh * n, axis=0, keepdims=True)


def _shift_down(x, k, fill=0.0):
    if k == 0:
        return x
    rows = lax.broadcasted_iota(jnp.int32, x.shape, 0)
    return jnp.where(rows >= k, pltpu.roll(x, k, 0), fill)


def _shift_up(x, k, fill=0.0):
    if k == 0:
        return x
    n = x.shape[0]
    rows = lax.broadcasted_iota(jnp.int32, x.shape, 0)
    return jnp.where(rows < n - k, pltpu.roll(x, n - k, 0), fill)


def _sigmoid(x):
    return 0.5 * jnp.tanh(0.5 * x) + 0.5


_GELU_K = 0.7978845608028654
_GELU_C = 0.044715


def _gelu(x):
    th = jnp.tanh(_GELU_K * (x + _GELU_C * x * x * x))
    return 0.5 * x * (1.0 + th), th


def _gelu_grad(x, th):
    return 0.5 * (1.0 + th) + 0.5 * x * (1.0 - th * th) * _GELU_K * (1.0 + 3.0 * _GELU_C * x * x)


def ffn_up(x, g, wup, l):
    T, D = x.shape
    cs = wup.shape[-1]
    tm = _tile(T, WIDE_TILE)
    ni = T // tm

    def body(x_ref, g_ref, wa_ref, wb_ref, h_ref, u_ref, s_ref, hs_ref):
        rows = pl.ds(pl.multiple_of(pl.program_id(1) * tm, tm), tm)

        @pl.when(pl.program_id(0) == 0)
        def _():
            n, _r = _rms(x_ref[...])
            hv = (n * g_ref[0]).astype(BF16)
            hs_ref[rows, :] = hv
            h_ref[...] = hv

        hv = hs_ref[rows, :]
        a = _dot(hv, wa_ref[0, 0])
        b = _dot(hv, wb_ref[0, 0])
        u_ref[0, 0] = a.astype(BF16)
        u_ref[1, 0] = b.astype(BF16)
        s_ref[0] = (a * _sigmoid(a) * b).astype(BF16)

    first = lambda j, i: (jnp.where(j == 0, i, ni - 1), 0)
    return _pallas_call(
        body, name="ffn_up", grid=(4, ni),
        in_specs=[pl.BlockSpec((tm, D), first), pl.BlockSpec((1, 1, D), lambda j, i: (l, 0, 0)),
                  pl.BlockSpec((1, 1, D, cs), lambda j, i: (l, j, 0, 0)), pl.BlockSpec((1, 1, D, cs), lambda j, i: (l, j + 4, 0, 0))],
        out_specs=[pl.BlockSpec((tm, D), first), pl.BlockSpec((2, 1, tm, cs), lambda j, i: (0, j, i, 0)),
                   pl.BlockSpec((1, tm, cs), lambda j, i: (j, i, 0))],
        out_shape=[jax.ShapeDtypeStruct((T, D), BF16), jax.ShapeDtypeStruct((2, 4, T, cs), BF16), jax.ShapeDtypeStruct((4, T, cs), BF16)],
        scratch_shapes=[pltpu.VMEM((T, D), BF16)],
        compiler_params=_cp("arbitrary", "arbitrary"),
    )(x, g, wup, wup)


def ffn_down(s, wd, x, l):
    _, T, cs = s.shape
    D = x.shape[1]
    tm = _tile(T, WIDE_TILE)

    def body(s_ref, w_ref, x_ref, o_ref, acc_ref):
        j = pl.program_id(1)

        @pl.when(j == 0)
        def _():
            acc_ref[...] = jnp.zeros_like(acc_ref)

        acc_ref[...] += _dot(s_ref[0], w_ref[0])

        @pl.when(j == 3)
        def _():
            o_ref[...] = x_ref[...] + 0.5 * acc_ref[...]

    return _pallas_call(
        body, name="ffn_down", grid=(T // tm, 4),
        in_specs=[pl.BlockSpec((1, tm, cs), lambda i, j: (j, i, 0)), pl.BlockSpec((1, cs, D), lambda i, j: (l, j, 0)),
                  pl.BlockSpec((tm, D), lambda i, j: (i, 0))],
        out_specs=pl.BlockSpec((tm, D), lambda i, j: (i, 0)),
        out_shape=jax.ShapeDtypeStruct((T, D), F32),
        scratch_shapes=[pltpu.VMEM((tm, D), F32)],
        compiler_params=_cp("parallel", "arbitrary"),
    )(s, wd, x)


def mix_in(x, g, win, l):
    T, D = x.shape
    ci = win.shape[-1]
    tm = _tile(T, WIDE_TILE)
    ni = T // tm

    def body(x_ref, g_ref, w_ref, h_ref, p_ref, hs_ref):
        rows = pl.ds(pl.multiple_of(pl.program_id(1) * tm, tm), tm)

        @pl.when(pl.program_id(0) == 0)
        def _():
            n, _r = _rms(x_ref[...])
            hv = (n * g_ref[0]).astype(BF16)
            hs_ref[rows, :] = hv
            h_ref[...] = hv

        p_ref[...] = _dot(hs_ref[rows, :], w_ref[0, 0]).astype(BF16)

    first = lambda j, i: (jnp.where(j == 0, i, ni - 1), 0)
    return _pallas_call(
        body, name="mix_in", grid=(N_DEV, ni),
        in_specs=[pl.BlockSpec((tm, D), first), pl.BlockSpec((1, 1, D), lambda j, i: (l, 0, 0)),
                  pl.BlockSpec((1, 1, D, ci), lambda j, i: (l, j, 0, 0))],
        out_specs=[pl.BlockSpec((tm, D), first), pl.BlockSpec((tm, ci), lambda j, i: (i, j))],
        out_shape=[jax.ShapeDtypeStruct((T, D), BF16), jax.ShapeDtypeStruct((T, N_DEV * ci), BF16)],
        scratch_shapes=[pltpu.VMEM((T, D), BF16)],
        compiler_params=_cp("arbitrary", "arbitrary"),
    )(x, g, win)


def _inv_count(T, w):
    t = lax.broadcasted_iota(jnp.int32, (T, 1), 0)
    return 1.0 / jnp.minimum(t + 1, w).astype(F32)


def _pooled(ug, w, inv):
    s = ug
    k = 1
    while k < w:
        s = s + _shift_down(s, k)
        k *= 2
    return s * inv - ug


def pool_fwd(proj, pw, pb, ps, l):
    T = proj.shape[0]
    _, G, gd, _ = pw.shape
    P = G * gd

    def body(u_ref, w_ref, b_ref, s_ref, o_ref):
        for gi in range(G):
            cols = slice(gi * gd, (gi + 1) * gd)
            ug = u_ref[:, cols].astype(F32)
            pooled = _pooled(ug, POOL_WINDOWS[gi], _inv_count(T, POOL_WINDOWS[gi]))
            mixed = _dot(pooled.astype(BF16), w_ref[0, gi].astype(BF16)) + b_ref[0, :, cols]
            o_ref[:, cols] = (mixed * s_ref[0, :, cols]).astype(BF16)

    return _pallas_call(
        body, name="pool_fwd", grid=(1,),
        in_specs=[pl.BlockSpec((T, P), lambda i: (0, 0)), pl.BlockSpec((1, G, gd, gd), lambda i: (l, 0, 0, 0)),
                  pl.BlockSpec((1, 1, P), lambda i: (l, 0, 0)), pl.BlockSpec((1, 1, P), lambda i: (l, 0, 0))],
        out_specs=pl.BlockSpec((T, P), lambda i: (0, 0)),
        out_shape=jax.ShapeDtypeStruct((T, P), BF16),
        compiler_params=_cp("arbitrary"),
    )(proj, pw, pb, ps)


def _conv(u, cw_ref, cb):
    CW = cw_ref.shape[1]
    v = cb
    for k in range(CW):
        v = v + cw_ref[0, k:k + 1, :] * _shift_down(u, CW - 1 - k)
    return v


def _softplus(z):
    return jnp.maximum(z, 0.0) + jnp.log1p(jnp.exp(-jnp.abs(z)))


def _lru_gates(v, wa_ref, ba, wx_ref, bx, lam):
    vb = v.astype(BF16)
    r = _sigmoid(_dot(vb, wa_ref[0, 0].astype(BF16)) + ba)
    i = _sigmoid(_dot(vb, wx_ref[0, 0].astype(BF16)) + bx)
    sp = _softplus(-lam)
    log_a = -LRU_C * r * sp
    a = jnp.exp(log_a)
    m2 = -jnp.tanh(log_a) * (a * a + 1.0)
    inv_mult = lax.rsqrt(m2)
    mult = jnp.where(m2 > 0.0, m2 * inv_mult, 0.0)
    return r, i, sp, a, mult, inv_mult


def _scan_fwd(a_ref, b_ref, o_ref):
    T, W = a_ref.shape
    rows = lax.broadcasted_iota(jnp.int32, (8, W), 0)

    def step(t, carry):
        r0 = pl.multiple_of(t * 8, 8)
        A = a_ref[pl.ds(r0, 8), :]
        B = b_ref[pl.ds(r0, 8), :]
        for s in (1, 2, 4):
            keep = rows >= s
            As = jnp.where(keep, pltpu.roll(A, s, 0), 1.0)
            Bs = jnp.where(keep, pltpu.roll(B, s, 0), 0.0)
            B = A * Bs + B
            A = A * As
        h = B + A * carry
        o_ref[pl.ds(r0, 8), :] = h
        return jnp.broadcast_to(h[7:8, :], (8, W))

    lax.fori_loop(0, T // 8, step, jnp.zeros((8, W), F32), unroll=8)


def _scan_bwd(a_ref, b_ref, o_ref):
    T, W = a_ref.shape
    rows = lax.broadcasted_iota(jnp.int32, (8, W), 0)
    nt = T // 8

    def step(t, carry):
        r0 = pl.multiple_of((nt - 1 - t) * 8, 8)
        A = a_ref[pl.ds(r0, 8), :]
        B = b_ref[pl.ds(r0, 8), :]
        for s in (1, 2, 4):
            keep = rows < 8 - s
            As = jnp.where(keep, pltpu.roll(A, 8 - s, 0), 1.0)
            Bs = jnp.where(keep, pltpu.roll(B, 8 - s, 0), 0.0)
            B = A * Bs + B
            A = A * As
        y = B + A * carry
        o_ref[pl.ds(r0, 8), :] = y
        return jnp.broadcast_to(y[0:1, :], (8, W))

    lax.fori_loop(0, nt, step, jnp.zeros((8, W), F32), unroll=8)


def _lru_specs(T, hd, P, R, CW, l):
    ob, gb = P // hd, (P + R) // hd
    vec = pl.BlockSpec((1, 1, hd), lambda h: (l, 0, h))
    mat = pl.BlockSpec((1, 1, hd, hd), lambda h: (l, h, 0, 0))
    return [pl.BlockSpec((T, hd), lambda h: (0, ob + h)), pl.BlockSpec((T, hd), lambda h: (0, gb + h)),
            pl.BlockSpec((1, CW, hd), lambda h: (l, 0, h)), vec, mat, vec, mat, vec, vec]


def lru_fwd(proj, cw, cb, wa, ba, wx, bx, lam, P, l):
    T = proj.shape[0]
    _, H, hd, _ = wa.shape
    R = H * hd
    CW = cw.shape[1]
    assert P % hd == 0 and T % 8 == 0

    def body(u_ref, ug_ref, cw_ref, cb_ref, wa_ref, ba_ref, wx_ref, bx_ref, lam_ref, hl_ref, hs_ref, a_s, b_s):
        v = _conv(u_ref[...].astype(F32), cw_ref, cb_ref[0])
        _r, i, _sp, a, mult, _im = _lru_gates(v, wa_ref, ba_ref[0], wx_ref, bx_ref[0], lam_ref[0])
        a_s[...] = a
        b_s[...] = mult * (i * v)
        _scan_fwd(a_s, b_s, hs_ref)
        ge, _th = _gelu(ug_ref[...].astype(F32))
        hl_ref[...] = (hs_ref[...] * ge).astype(BF16)

    out = pl.BlockSpec((T, hd), lambda h: (0, h))
    return _pallas_call(
        body, name="lru_fwd", grid=(H,),
        in_specs=_lru_specs(T, hd, P, R, CW, l),
        out_specs=[out, out],
        out_shape=[jax.ShapeDtypeStruct((T, R), BF16), jax.ShapeDtypeStruct((T, R), F32)],
        scratch_shapes=[pltpu.VMEM((T, hd), F32)] * 2,
        compiler_params=_cp("parallel"),
    )(proj, proj, cw, cb, wa, ba, wx, bx, lam)


def mix_out(pm, hl, proj, x, wpu, wlu, wout, P, l):
    T, D = x.shape
    R = hl.shape[1]
    tm = _tile(T, ROW_TILE)
    assert (P + 2 * R) % D == 0
    gb = (P + 2 * R) // D

    def body(pm_ref, hl_ref, gp_ref, gl_ref, x_ref, wpu_ref, wlu_ref, wo_ref, o_ref, yp_ref, yl_ref, z_ref):
        yp = _dot(pm_ref[...], wpu_ref[0])
        yl = _dot(hl_ref[...], wlu_ref[0])
        z = (_sigmoid(gp_ref[...].astype(F32)) * yp + _sigmoid(gl_ref[...].astype(F32)) * yl).astype(BF16)
        yp_ref[...] = yp.astype(BF16)
        yl_ref[...] = yl.astype(BF16)
        z_ref[...] = z
        o_ref[...] = x_ref[...] + _dot(z, wo_ref[0])

    row = lambda w: pl.BlockSpec((tm, w), lambda i: (i, 0))
    return _pallas_call(
        body, name="mix_out", grid=(T // tm,),
        in_specs=[row(P), row(R), pl.BlockSpec((tm, D), lambda i: (i, gb)), pl.BlockSpec((tm, D), lambda i: (i, gb + 1)), row(D),
                  pl.BlockSpec((1, P, D), lambda i: (l, 0, 0)), pl.BlockSpec((1, R, D), lambda i: (l, 0, 0)),
                  pl.BlockSpec((1, D, D), lambda i: (l, 0, 0))],
        out_specs=[row(D)] * 4,
        out_shape=[jax.ShapeDtypeStruct((T, D), F32)] + [jax.ShapeDtypeStruct((T, D), BF16)] * 3,
        compiler_params=_cp("parallel"),
    )(pm, hl, proj, proj, x, wpu, wlu, wout)


def loss_head(x, gf, tgt):
    T, D = x.shape
    tm = _tile(T, ROW_TILE)

    def body(x_ref, g_ref, t_ref, loss_ref, dx_ref, dg_ref):
        @pl.when(pl.program_id(0) == 0)
        def _():
            loss_ref[...] = jnp.zeros_like(loss_ref)
            dg_ref[...] = jnp.zeros_like(dg_ref)

        xv = x_ref[...]
        gv = g_ref[...]
        n, _r = _rms(xv)
        e = n * gv - t_ref[...]
        loss_ref[...] += 0.5 * jnp.sum(jnp.sum(e * e, axis=-1, keepdims=True), axis=0, keepdims=True) / D
        dx, dg = _rms_bwd(e * (1.0 / D), xv, gv, 0.0)
        dx_ref[...] = dx
        dg_ref[...] += dg

    return _pallas_call(
        body, name="loss_head", grid=(T // tm,),
        in_specs=[pl.BlockSpec((tm, D), lambda i: (i, 0)), pl.BlockSpec((1, D), lambda i: (0, 0)), pl.BlockSpec((tm, D), lambda i: (i, 0))],
        out_specs=[pl.BlockSpec((1, 1), lambda i: (0, 0)), pl.BlockSpec((tm, D), lambda i: (i, 0)), pl.BlockSpec((1, D), lambda i: (0, 0))],
        out_shape=[jax.ShapeDtypeStruct((1, 1), F32), jax.ShapeDtypeStruct((T, D), F32), jax.ShapeDtypeStruct((1, D), F32)],
        compiler_params=_cp("arbitrary"),
    )(x, gf, tgt)


def ffn_down_bwd(dy, wd, u, l, deps=()):
    T, D = dy.shape
    cs = u.shape[-1]
    tm = _tile(T, WIDE_TILE)
    ni = T // tm

    def body(dy_ref, w_ref, u_ref, *rest):
        do_ref, du_ref, dyb_ref = rest[len(deps):]
        rows = pl.ds(pl.multiple_of(pl.program_id(1) * tm, tm), tm)

        @pl.when(pl.program_id(0) == 0)
        def _():
            d = (0.5 * dy_ref[...]).astype(BF16)
            dyb_ref[rows, :] = d
            do_ref[...] = d

        ds = _dot_nt(dyb_ref[rows, :], w_ref[0])
        a = u_ref[0, 0].astype(F32)
        b = u_ref[1, 0].astype(F32)
        sg = _sigmoid(a)
        du_ref[0, 0] = (ds * b * (sg * (1.0 + a * (1.0 - sg)))).astype(BF16)
        du_ref[1, 0] = (ds * (a * sg)).astype(BF16)

    first = lambda j, i: (jnp.where(j == 0, i, ni - 1), 0)
    blk = pl.BlockSpec((2, 1, tm, cs), lambda j, i: (0, j, i, 0))
    return _pallas_call(
        body, name="ffn_down_bwd", grid=(4, ni),
        in_specs=[pl.BlockSpec((tm, D), first), pl.BlockSpec((1, cs, D), lambda j, i: (l, j, 0)), blk] + [ANY] * len(deps),
        out_specs=[pl.BlockSpec((tm, D), first), blk],
        out_shape=[jax.ShapeDtypeStruct((T, D), BF16), jax.ShapeDtypeStruct((2, 4, T, cs), BF16)],
        scratch_shapes=[pltpu.VMEM((T, D), BF16)],
        compiler_params=_cp("arbitrary", "arbitrary"),
    )(dy, wd, u, *deps)


def dw_tn(name, a, a_spec, b, b_spec, G, M, N, T):
    tk = _tile(T, SUM_TILE)
    nk = T // tk

    def body(a_ref, b_ref, o32_ref, o16_ref, acc_ref):
        k = pl.program_id(1)

        @pl.when(k == 0)
        def _():
            acc_ref[...] = jnp.zeros_like(acc_ref)

        av = a_ref[0] if len(a_ref.shape) == 3 else a_ref[...]
        bv = b_ref[0] if len(b_ref.shape) == 3 else b_ref[...]
        acc_ref[...] += _dot_tn(av, bv)

        @pl.when(k == nk - 1)
        def _():
            o32_ref[0, 0] = acc_ref[...]
            o16_ref[0, 0] = acc_ref[...].astype(BF16)

    out = pl.BlockSpec((1, 1, M, N), lambda g, k: (0, g, 0, 0))
    return _pallas_call(
        body, name=name, grid=(G, nk),
        in_specs=[a_spec(tk), b_spec(tk)], out_specs=[out, out],
        out_shape=[jax.ShapeDtypeStruct((1, G, M, N), F32), jax.ShapeDtypeStruct((1, G, M, N), BF16)],
        scratch_shapes=[pltpu.VMEM((M, N), F32)],
        compiler_params=_cp("parallel", "arbitrary"),
    )(a, b)


def dx_norm_bwd(name, dact, d_spec, w, G, x, g, dy, l):
    T, D = x.shape
    c = w.shape[-1]
    tm = _tile(T, WIDE_TILE)
    ni = T // tm
    ch = _tile(tm, ROW_TILE // 2)

    def body(d_ref, w_ref, x_ref, g_ref, dy_ref, dx_ref, dg_ref, acc_ref):
        j, i = pl.program_id(0), pl.program_id(1)
        rows = pl.ds(pl.multiple_of(i * tm, tm), tm)

        @pl.when(jnp.logical_and(i == 0, j == 0))
        def _():
            dg_ref[...] = jnp.zeros_like(dg_ref)

        dv = d_ref[0] if len(d_ref.shape) == 3 else d_ref[...]
        part = _dot_nt(dv, w_ref[0, 0])

        @pl.when(j == 0)
        def _():
            acc_ref[rows, :] = part

        @pl.when(j > 0)
        def _():
            acc_ref[rows, :] += part

        @pl.when(j == G - 1)
        def _():
            dg = jnp.zeros((1, D), F32)
            for c0 in range(0, tm, ch):
                part_rows = pl.ds(pl.multiple_of(i * tm + c0, ch), ch)
                dx, dgc = _rms_bwd(acc_ref[part_rows, :], x_ref[c0:c0 + ch, :], g_ref[0], dy_ref[c0:c0 + ch, :])
                dx_ref[c0:c0 + ch, :] = dx
                dg = dg + dgc
            dg_ref[...] += dg

    last = pl.BlockSpec((tm, D), lambda j, i: (jnp.where(j == G - 1, i, 0), 0))
    return _pallas_call(
        body, name=name, grid=(G, ni),
        in_specs=[d_spec(tm), pl.BlockSpec((1, 1, D, c), lambda j, i: (l, j, 0, 0)), last, pl.BlockSpec((1, 1, D), lambda j, i: (l, 0, 0)), last],
        out_specs=[last, pl.BlockSpec((1, D), lambda j, i: (0, 0))],
        out_shape=[jax.ShapeDtypeStruct((T, D), F32), jax.ShapeDtypeStruct((1, D), F32)],
        scratch_shapes=[pltpu.VMEM((T, D), F32)],
        compiler_params=_cp("arbitrary", "arbitrary"),
    )(dact, w, x, g, dy)


def mix_out_bwd(dy, proj, yp, yl, wpu, wlu, wout, P, R, l, deps=()):
    T, D = dy.shape
    tm = _tile(T, ROW_TILE)
    gb = (P + 2 * R) // D

    def body(dy_ref, gp_ref, gl_ref, yp_ref, yl_ref, wpu_ref, wlu_ref, wo_ref, *rest):
        dyb_ref, dyp_ref, dyl_ref, dgp_ref, dgl_ref, dpm_ref, dhl_ref = rest[len(deps):]
        dyb = dy_ref[...].astype(BF16)
        dyb_ref[...] = dyb
        dz = _dot_nt(dyb, wo_ref[0])
        sp = _sigmoid(gp_ref[...].astype(F32))
        sl = _sigmoid(gl_ref[...].astype(F32))
        dgp_ref[...] = (dz * yp_ref[...].astype(F32) * sp * (1.0 - sp)).astype(BF16)
        dgl_ref[...] = (dz * yl_ref[...].astype(F32) * sl * (1.0 - sl)).astype(BF16)
        dyp = (dz * sp).astype(BF16)
        dyl = (dz * sl).astype(BF16)
        dyp_ref[...] = dyp
        dyl_ref[...] = dyl
        dpm_ref[...] = _dot_nt(dyp, wpu_ref[0]).astype(BF16)
        dhl_ref[...] = _dot_nt(dyl, wlu_ref[0]).astype(BF16)

    row = lambda w: pl.BlockSpec((tm, w), lambda i: (i, 0))
    return _pallas_call(
        body, name="mix_out_bwd", grid=(T // tm,),
        in_specs=[row(D), pl.BlockSpec((tm, D), lambda i: (i, gb)), pl.BlockSpec((tm, D), lambda i: (i, gb + 1)), row(D), row(D),
                  pl.BlockSpec((1, P, D), lambda i: (l, 0, 0)), pl.BlockSpec((1, R, D), lambda i: (l, 0, 0)),
                  pl.BlockSpec((1, D, D), lambda i: (l, 0, 0))] + [ANY] * len(deps),
        out_specs=[row(D)] * 5 + [row(P), row(R)],
        out_shape=[jax.ShapeDtypeStruct((T, D), BF16)] * 5 + [jax.ShapeDtypeStruct((T, P), BF16), jax.ShapeDtypeStruct((T, R), BF16)],
        compiler_params=_cp("parallel"),
    )(dy, proj, proj, yp, yl, wpu, wlu, wout, *deps)


def lru_bwd(proj, hs, dhl, cw, cb, wa, ba, wx, bx, lam, P, l):
    T = proj.shape[0]
    _, H, hd, _ = wa.shape
    R = H * hd
    CW = cw.shape[1]

    def body(u_ref, ug_ref, cw_ref, cb_ref, wa_ref, ba_ref, wx_ref, bx_ref, lam_ref, hs_ref, dhl_ref,
             du_ref, dug_ref, dcw_ref, dcb_ref, dwa_ref, dba_ref, dwx_ref, dbx_ref, dlam_ref, c_s, g_s, y_s):
        u = u_ref[...].astype(F32)
        v = _conv(u, cw_ref, cb_ref[0])
        lam = lam_ref[0]
        r, i, sp, a, mult, inv_mult = _lru_gates(v, wa_ref, ba_ref[0], wx_ref, bx_ref[0], lam)
        ug = ug_ref[...].astype(F32)
        ge, th = _gelu(ug)
        hs = hs_ref[...]
        dhl = dhl_ref[...].astype(F32)
        dug_ref[...] = (dhl * hs * _gelu_grad(ug, th)).astype(BF16)
        c_s[...] = _shift_up(a, 1)
        g_s[...] = dhl * ge
        _scan_bwd(c_s, g_s, y_s)
        y = y_s[...]
        da = y * _shift_down(hs, 1)
        iv = i * v
        dlog_a = da * a - (y * iv) * (a * a) * inv_mult
        div = y * mult
        dpa = (dlog_a * (-LRU_C) * sp) * r * (1.0 - r)
        dpx = (div * v) * i * (1.0 - i)
        dsp = jnp.sum(dlog_a * (-LRU_C) * r, axis=0, keepdims=True)
        dlam_ref[0] = -dsp * _sigmoid(-lam)
        vb = v.astype(BF16)
        dpab, dpxb = dpa.astype(BF16), dpx.astype(BF16)
        dwa_ref[0, 0] = _dot_tn(vb, dpab)
        dwx_ref[0, 0] = _dot_tn(vb, dpxb)
        dba_ref[0] = jnp.sum(dpa, axis=0, keepdims=True)
        dbx_ref[0] = jnp.sum(dpx, axis=0, keepdims=True)
        dv = div * i + _dot_nt(dpab, wa_ref[0, 0].astype(BF16)) + _dot_nt(dpxb, wx_ref[0, 0].astype(BF16))
        dcb_ref[0] = jnp.sum(dv, axis=0, keepdims=True)
        du = jnp.zeros_like(dv)
        for k in range(CW):
            du = du + cw_ref[0, k:k + 1, :] * _shift_up(dv, CW - 1 - k)
            dcw_ref[0, k:k + 1, :] = jnp.sum(dv * _shift_down(u, CW - 1 - k), axis=0, keepdims=True)
        du_ref[...] = du.astype(BF16)

    col = pl.BlockSpec((T, hd), lambda h: (0, h))
    vec = pl.BlockSpec((1, 1, hd), lambda h: (0, 0, h))
    mat = pl.BlockSpec((1, 1, hd, hd), lambda h: (0, h, 0, 0))
    vshape = jax.ShapeDtypeStruct((1, 1, R), F32)
    mshape = jax.ShapeDtypeStruct((1, H, hd, hd), F32)
    return _pallas_call(
        body, name="lru_bwd", grid=(H,),
        in_specs=_lru_specs(T, hd, P, R, CW, l) + [col, col],
        out_specs=[col, col, pl.BlockSpec((1, CW, hd), lambda h: (0, 0, h)), vec, mat, vec, mat, vec, vec],
        out_shape=[jax.ShapeDtypeStruct((T, R), BF16)] * 2 + [jax.ShapeDtypeStruct((1, CW, R), F32), vshape, mshape, vshape, mshape, vshape, vshape],
        scratch_shapes=[pltpu.VMEM((T, hd), F32)] * 3,
        compiler_params=_cp("parallel"),
    )(proj, proj, cw, cb, wa, ba, wx, bx, lam, hs, dhl)


def pool_bwd(proj, dpm, pw, pb, ps, l):
    T = proj.shape[0]
    _, G, gd, _ = pw.shape
    P = G * gd

    def body(u_ref, d_ref, w_ref, b_ref, s_ref, du_ref, dw_ref, db_ref, dsc_ref):
        for gi in range(G):
            cols = slice(gi * gd, (gi + 1) * gd)
            w = POOL_WINDOWS[gi]
            inv = _inv_count(T, w)
            ug = u_ref[:, cols].astype(F32)
            pooled = _pooled(ug, w, inv).astype(BF16)
            wb = w_ref[0, gi].astype(BF16)
            mixed = _dot(pooled, wb) + b_ref[0, :, cols]
            dpm_g = d_ref[:, cols].astype(F32)
            dsc_ref[0, :, cols] = jnp.sum(dpm_g * mixed, axis=0, keepdims=True)
            dmixed = dpm_g * s_ref[0, :, cols]
            db_ref[0, :, cols] = jnp.sum(dmixed, axis=0, keepdims=True)
            dmb = dmixed.astype(BF16)
            dw_ref[0, gi] = _dot_tn(pooled, dmb)
            dpooled = _dot_nt(dmb, wb)
            s = dpooled * inv
            k = 1
            while k < w:
                s = s + _shift_up(s, k)
                k *= 2
            du_ref[:, cols] = (s - dpooled).astype(BF16)

    vec = pl.BlockSpec((1, 1, P), lambda i: (l, 0, 0))
    ovec = pl.BlockSpec((1, 1, P), lambda i: (0, 0, 0))
    return _pallas_call(
        body, name="pool_bwd", grid=(1,),
        in_specs=[pl.BlockSpec((T, P), lambda i: (0, 0)), pl.BlockSpec((T, P), lambda i: (0, 0)),
                  pl.BlockSpec((1, G, gd, gd), lambda i: (l, 0, 0, 0)), vec, vec],
        out_specs=[pl.BlockSpec((T, P), lambda i: (0, 0)), pl.BlockSpec((1, G, gd, gd), lambda i: (0, 0, 0, 0)), ovec, ovec],
        out_shape=[jax.ShapeDtypeStruct((T, P), BF16), jax.ShapeDtypeStruct((1, G, gd, gd), F32),
                   jax.ShapeDtypeStruct((1, 1, P), F32), jax.ShapeDtypeStruct((1, 1, P), F32)],
        compiler_params=_cp("arbitrary"),
    )(proj, dpm, pw, pb, ps)


def _place():
    x, y, c = lax.axis_index("x"), lax.axis_index("y"), lax.axis_index("c")
    return x, y, c


def all_gather(name, shards):
    n = len(shards)

    def body(*refs):
        src, out = refs[:n], refs[n:2 * n]
        send_sems, recv_sems, local_sems = refs[2 * n:]
        x, y, c = _place()
        sibling = (x, y, 1 - c)
        chips = [(x, 1 - y), (1 - x, y), (1 - x, 1 - y)]

        def slot(a, px, py, pc):
            return out[a].at[:, 4 * px + 2 * py + pc]

        def copy(a, k, block, to, from_src=False):
            return pltpu.make_async_remote_copy(
                src_ref=src[a] if from_src else slot(a, *block), dst_ref=slot(a, *block),
                send_sem=send_sems.at[a, k], recv_sem=recv_sems.at[a, k], device_id=to, device_id_type=MESH)

        me = (x, y, c)
        mine = [pltpu.make_async_copy(src[a], slot(a, *me), local_sems.at[a]) for a in range(n)]
        first = []
        for j, chip in enumerate(chips):
            for a in range(n):
                first.append(copy(a, 1 + j, me, (*chip, c), from_src=True))
        for a in range(n):
            first.append(copy(a, 0, me, sibling, from_src=True))
        for cp in mine + first:
            cp.start()
        passed = []
        for j, chip in enumerate(chips):
            for a in range(n):
                copy(a, 1 + j, (*chip, c), me).wait_recv()
                fwd = copy(a, 4 + j, (*chip, c), sibling)
                fwd.start()
                passed.append(fwd)
        for a in range(n):
            copy(a, 0, (x, y, 1 - c), me).wait_recv()
        for j, chip in enumerate(chips):
            for a in range(n):
                copy(a, 4 + j, (*chip, 1 - c), me).wait_recv()
        for cp in first + passed:
            cp.wait_send()
        for cp in mine:
            cp.wait()

    outs = _pallas_call(
        body, name=name,
        in_specs=[ANY] * n, out_specs=[ANY] * n,
        out_shape=[jax.ShapeDtypeStruct((s.shape[0], N_DEV) + s.shape[1:], s.dtype) for s in shards],
        scratch_shapes=[pltpu.SemaphoreType.DMA((n, 7)), pltpu.SemaphoreType.DMA((n, 7)), pltpu.SemaphoreType.DMA((n,))],
        compiler_params=pltpu.CompilerParams(has_side_effects=True),
    )(*shards)
    return list(outs)


def pair_exchange(name, g16, layer=None):
    n = len(g16)
    layers = slice(None) if layer is None else pl.ds(layer, 1)

    def body(*refs):
        s16, recv = refs[:n], refs[n:2 * n]
        send_sems, recv_sems = refs[2 * n:]
        x, y, c = _place()
        sibling = (x, y, 1 - c)
        rem = []
        for a in range(n):
            for j in range(N_CHIP):
                rem.append(pltpu.make_async_remote_copy(
                    src_ref=s16[a].at[layers, 2 * j + 1 - c], dst_ref=recv[a].at[:, j],
                    send_sem=send_sems.at[a, j], recv_sem=recv_sems.at[a, j], device_id=sibling, device_id_type=MESH))
        for cp in rem:
            cp.start()
        for cp in rem:
            cp.wait_recv()
        for cp in rem:
            cp.wait_send()

    outs = _pallas_call(
        body, name=name,
        in_specs=[ANY] * n, out_specs=[ANY] * n,
        out_shape=[jax.ShapeDtypeStruct((s.shape[0] if layer is None else 1, N_CHIP) + s.shape[2:], s.dtype) for s in g16],
        scratch_shapes=[pltpu.SemaphoreType.DMA((n, N_CHIP))] * 2,
        compiler_params=pltpu.CompilerParams(has_side_effects=True),
    )(*g16)
    return list(outs)


def chip_exchange(name, pair16):
    n = len(pair16)

    def body(*refs):
        p16, recv2 = refs[:n], refs[n:2 * n]
        send_sems, recv_sems = refs[2 * n:]
        x, y, c = _place()
        rem = []
        for d in (1, 2, 3):
            px = 1 - x if d & 2 else x
            py = 1 - y if d & 1 else y
            for a in range(n):
                rem.append(pltpu.make_async_remote_copy(
                    src_ref=p16[a].at[:, 2 * px + py], dst_ref=recv2[a].at[:, d - 1],
                    send_sem=send_sems.at[a, d - 1], recv_sem=recv_sems.at[a, d - 1], device_id=(px, py, c), device_id_type=MESH))
        for cp in rem:
            cp.start()
        for cp in rem:
            cp.wait_recv()
        for cp in rem:
            cp.wait_send()

    outs = _pallas_call(
        body, name=name,
        in_specs=[ANY] * n, out_specs=[ANY] * n,
        out_shape=[jax.ShapeDtypeStruct((s.shape[0], 3) + s.shape[2:], s.dtype) for s in pair16],
        scratch_shapes=[pltpu.SemaphoreType.DMA((n, 3))] * 2,
        compiler_params=pltpu.CompilerParams(has_side_effects=True),
    )(*pair16)
    return list(outs)


HBM = pl.BlockSpec(memory_space=pltpu.HBM)
SEM = pl.BlockSpec(memory_space=pltpu.SEMAPHORE)
EFFECT = pltpu.SideEffectType.DATAFLOW_SIDE_EFFECTING


def _in_hbm(a):
    return pltpu.with_memory_space_constraint(a, pltpu.HBM)


def split_start(name, bufs, n_copies, copies_of, deps=()):
    nb = len(bufs)

    def body(*refs):
        buf = refs[:nb]
        send_sems, recv_sems = refs[nb + len(deps)], refs[nb + len(deps) + 1]
        token = refs[-1]
        for i, (src, dst, dev) in enumerate(copies_of(buf)):
            pltpu.make_async_remote_copy(src_ref=src, dst_ref=dst, send_sem=send_sems.at[i], recv_sem=recv_sems.at[i],
                                         device_id=dev, device_id_type=MESH).start()
        token[...] = jnp.zeros_like(token)

    outs = _pallas_call(
        body, name=name,
        in_specs=[HBM] * nb + [ANY] * len(deps),
        out_specs=(SEM, SEM, *([HBM] * nb), pl.BlockSpec(memory_space=pltpu.VMEM)),
        out_shape=(pltpu.SemaphoreType.DMA((n_copies,)), pltpu.SemaphoreType.DMA((n_copies,)),
                   *[pltpu.HBM(b.shape, b.dtype) for b in bufs], jax.ShapeDtypeStruct((8, 128), F32)),
        input_output_aliases={i: 2 + i for i in range(nb)},
        compiler_params=pltpu.CompilerParams(has_side_effects=EFFECT),
    )(*[_in_hbm(b) for b in bufs], *deps)
    return outs[0], outs[1], list(outs[2:2 + nb]), outs[-1]


def split_wait(name, bufs, send_sems, recv_sems, after, copies_of):
    nb = len(bufs)

    def body(*refs):
        buf = refs[:nb]
        send, recv = refs[nb], refs[nb + 1]
        for i, (src, dst, dev) in enumerate(copies_of(buf)):
            cp = pltpu.make_async_remote_copy(src_ref=src, dst_ref=dst, send_sem=send.at[i], recv_sem=recv.at[i],
                                              device_id=dev, device_id_type=MESH)
            cp.wait_send()
            cp.wait_recv()

    outs = _pallas_call(
        body, name=name,
        in_specs=[HBM] * nb + [SEM, SEM] + [ANY] * len(after),
        out_specs=[HBM] * nb,
        out_shape=[pltpu.HBM(b.shape, b.dtype) for b in bufs],
        input_output_aliases={i: i for i in range(nb)},
        compiler_params=pltpu.CompilerParams(has_side_effects=EFFECT),
    )(*bufs, send_sems, recv_sems, *after)
    return list(outs)


def place_own(w, l, place, dtype):
    _, rows, cols = w.shape
    tr = _rows_tile(rows, cols, 1 << 19)

    def body(p_ref, w_ref, o_ref):
        o_ref[0] = w_ref[0].astype(dtype)

    return _pallas_call(
        body, name="place_own",
        grid_spec=pltpu.PrefetchScalarGridSpec(
            num_scalar_prefetch=1, grid=(rows // tr,),
            in_specs=[pl.BlockSpec((1, tr, cols), lambda i, p: (l, i, 0))],
            out_specs=pl.BlockSpec((1, tr, cols), lambda i, p: (p[2], i, 0))),
        out_shape=jax.ShapeDtypeStruct((N_DEV, rows, cols), dtype), compiler_params=_cp("parallel"),
    )(place, w)


def _gather_copies(land):
    x, y, c = _place()
    k = 4 * x + 2 * y + c
    peers = [(x, 1 - y, c), (1 - x, y, c), (1 - x, 1 - y, c), (x, y, 1 - c)]
    return [(b.at[k], b.at[k], p) for p in peers for b in land]


def gather_start(name, land, deps=()):
    return split_start(name, land, 4 * len(land), _gather_copies, deps)


def gather_wait(name, land, send_sems, recv_sems, after):
    return split_wait(name, land, send_sems, recv_sems, after, _gather_copies)


def gather_forward(name, land):
    n = len(land)

    def body(*refs):
        buf = refs[:n]
        send_sems, recv_sems = refs[2 * n:]
        x, y, c = _place()
        rem = []
        for j, (px, py) in enumerate([(x, 1 - y), (1 - x, y), (1 - x, 1 - y)]):
            k = 4 * px + 2 * py + c
            for a in range(n):
                rem.append(pltpu.make_async_remote_copy(
                    src_ref=buf[a].at[k], dst_ref=buf[a].at[k], send_sem=send_sems.at[a, j], recv_sem=recv_sems.at[a, j],
                    device_id=(x, y, 1 - c), device_id_type=MESH))
        for cp in rem:
            cp.start()
        for cp in rem:
            cp.wait_recv()
        for cp in rem:
            cp.wait_send()

    outs = _pallas_call(
        body, name=name,
        in_specs=[ANY] * n, out_specs=[ANY] * n,
        out_shape=[jax.ShapeDtypeStruct(b.shape, b.dtype) for b in land],
        scratch_shapes=[pltpu.SemaphoreType.DMA((n, 3))] * 2,
        input_output_aliases={i: i for i in range(n)},
        compiler_params=pltpu.CompilerParams(has_side_effects=True),
    )(*land)
    return list(outs)


def _chip_copies(nsrc):
    def copies(buf):
        p16, recv2 = buf[:nsrc], buf[nsrc:]
        x, y, c = _place()
        out = []
        for d in (1, 2, 3):
            px = 1 - x if d & 2 else x
            py = 1 - y if d & 1 else y
            out += [(p16[a].at[:, 2 * px + py], recv2[a].at[:, d - 1], (px, py, c)) for a in range(nsrc)]
        return out
    return copies


def _pair_copies(nsrc):
    def copies(buf):
        g16, recv = buf[:nsrc], buf[nsrc:]
        x, y, c = _place()
        return [(g16[a].at[:, 2 * j + 1 - c], recv[a].at[:, j], (x, y, 1 - c)) for a in range(nsrc) for j in range(N_CHIP)]
    return copies


def pair_exchange_start(name, g16, deps=()):
    n = len(g16)
    land = [lax.empty((1, N_CHIP) + s.shape[2:], s.dtype) for s in g16]
    return split_start(name, list(g16) + land, N_CHIP * n, _pair_copies(n), deps)


def pair_exchange_wait(name, bufs, send_sems, recv_sems, after):
    n = len(bufs) // 2
    return split_wait(name, bufs, send_sems, recv_sems, after, _pair_copies(n))[n:]


def chip_exchange_start(name, pair16, deps=()):
    n = len(pair16)
    land = [lax.empty((s.shape[0], 3) + s.shape[2:], s.dtype) for s in pair16]
    return split_start(name, list(pair16) + land, 3 * n, _chip_copies(n), deps)


def chip_exchange_wait(name, bufs, send_sems, recv_sems, after):
    n = len(bufs) // 2
    return split_wait(name, bufs, send_sems, recv_sems, after, _chip_copies(n))[n:]


def _rows_tile(rows, cols, budget=1 << 20):
    t = rows
    while t % 2 == 0 and t * cols > budget and (t // 2) % 16 == 0:
        t //= 2
    return t


def pair_sum(g32, recv1, place, l):
    _, _, rows, cols = recv1.shape
    tr = _rows_tile(rows, cols)

    def body(p_ref, m_ref, r_ref, o_ref):
        o_ref[...] = (m_ref[...] + r_ref[...].astype(F32)).astype(o_ref.dtype)

    blk = pl.BlockSpec((1, 1, tr, cols), lambda j, i, p: (0, j, i, 0))
    return _pallas_call(
        body, name="pair_sum",
        grid_spec=pltpu.PrefetchScalarGridSpec(
            num_scalar_prefetch=1, grid=(N_CHIP, rows // tr),
            in_specs=[pl.BlockSpec((1, 1, tr, cols), lambda j, i, p: (l, 2 * j + p[0], i, 0)), blk], out_specs=blk),
        out_shape=jax.ShapeDtypeStruct(recv1.shape, recv1.dtype), compiler_params=_cp("parallel", "parallel"),
    )(place, g32, recv1)


def _grad_in_specs(tr, cols, l):
    return ([pl.BlockSpec((1, 1, tr, cols), lambda i, p: (l, p[2], i, 0)), pl.BlockSpec((1, 1, tr, cols), lambda i, p: (0, p[1], i, 0))]
            + [pl.BlockSpec((1, 1, tr, cols), lambda i, p, d=d: (0, d, i, 0)) for d in range(3)])


def _grad_total(o32, o16, r0, r1, r2):
    return (o32[0, 0] + o16[0, 0].astype(F32)) + r0[0, 0].astype(F32) + r1[0, 0].astype(F32) + r2[0, 0].astype(F32)


def grad_sum(g32, recv1, recv2, place):
    _, _, rows, cols = recv1.shape
    tr = _rows_tile(rows, cols)

    def body(p_ref, o32, o16, r0, r1, r2, g_ref):
        g_ref[...] = _grad_total(o32, o16, r0, r1, r2)

    return _pallas_call(
        body, name="grad_sum",
        grid_spec=pltpu.PrefetchScalarGridSpec(
            num_scalar_prefetch=1, grid=(rows // tr,), in_specs=_grad_in_specs(tr, cols, 0),
            out_specs=pl.BlockSpec((tr, cols), lambda i, p: (i, 0))),
        out_shape=jax.ShapeDtypeStruct((rows, cols), F32), compiler_params=_cp("parallel"),
    )(place, g32, recv1, recv2, recv2, recv2)


def _adamw_math(w, g, m, v):
    m = ADAM_B1 * m + (1.0 - ADAM_B1) * g
    v = ADAM_B2 * v + (1.0 - ADAM_B2) * (g * g)
    m_hat = m / (1.0 - ADAM_B1 ** ADAM_STEP)
    v_hat = v / (1.0 - ADAM_B2 ** ADAM_STEP)
    delta = -ADAM_LR * (m_hat / (jnp.sqrt(v_hat) + ADAM_EPS) + ADAM_WD * w)
    return delta, m, v


def grad_sum_adamw(g32, recv1, recv2, w, m, v, place, l, prev):
    L, rows, cols = w.shape
    tr = _rows_tile(rows, cols, 1 << 18)

    def body(p_ref, o32, o16, r0, r1, r2, w_ref, m_ref, v_ref, *rest):
        g_ref, d_ref, nm_ref, nv_ref = rest[-4:]
        g = _grad_total(o32, o16, r0, r1, r2)
        d, nm, nv = _adamw_math(w_ref[0], g, m_ref[0], v_ref[0])
        g_ref[0] = g
        d_ref[0] = d
        nm_ref[0] = nm
        nv_ref[0] = nv

    blk = pl.BlockSpec((1, tr, cols), lambda i, p: (l, i, 0))
    args = [g32, recv1, recv2, recv2, recv2, w, m, v]
    in_specs = _grad_in_specs(tr, cols, 0) + [blk] * 3
    aliases = {}
    if prev is not None:
        aliases = {1 + len(args) + k: k for k in range(4)}
        args += list(prev)
        in_specs += [ANY] * 4
    return _pallas_call(
        body, name="grad_sum_adamw",
        grid_spec=pltpu.PrefetchScalarGridSpec(num_scalar_prefetch=1, grid=(rows // tr,), in_specs=in_specs, out_specs=[blk] * 4),
        out_shape=[jax.ShapeDtypeStruct((L, rows, cols), F32)] * 4, input_output_aliases=aliases,
        compiler_params=_cp("parallel"),
    )(place, *args)


def adamw(w, g, m, v):
    rows, cols = w.shape
    tr = _rows_tile(rows, cols, 1 << 18)

    def body(w_ref, g_ref, m_ref, v_ref, d_ref, nm_ref, nv_ref):
        d, nm, nv = _adamw_math(w_ref[...], g_ref[...], m_ref[...], v_ref[...])
        d_ref[...] = d
        nm_ref[...] = nm
        nv_ref[...] = nv

    blk = pl.BlockSpec((tr, cols), lambda i: (i, 0))
    return _pallas_call(body, name="adamw_small", grid=(rows // tr,), in_specs=[blk] * 4, out_specs=[blk] * 3,
                        out_shape=[jax.ShapeDtypeStruct((rows, cols), F32)] * 3, compiler_params=_cp("parallel"))(w, g, m, v)


SMALL = ("norm_ffn1", "norm_mix", "pool_w", "pool_b", "pool_scale", "conv_w", "conv_b", "lru_w_a", "lru_b_a", "lru_w_x", "lru_b_x",
         "lru_lambda", "norm_ffn2", "final_norm")
BIG = ("ffn1_w_up", "ffn1_w_down", "w_in", "w_pool_up", "w_lru_up", "w_out", "ffn2_w_up", "ffn2_w_down")
NAMES = ("norm_ffn1", "ffn1_w_up", "ffn1_w_down", "norm_mix", "w_in", "pool_w", "pool_b", "pool_scale", "w_pool_up", "conv_w", "conv_b",
         "lru_w_a", "lru_b_a", "lru_w_x", "lru_b_x", "lru_lambda", "w_lru_up", "w_out", "norm_ffn2", "ffn2_w_up", "ffn2_w_down", "final_norm")
SUBLAYERS = (("ffn1_w_up", "ffn1_w_down"), ("w_in", "w_pool_up", "w_lru_up", "w_out", "conv_w"), ("ffn2_w_up", "ffn2_w_down"))
PACK_ROWS = 16 * N_DEV


def _pack(parts):
    flat = jnp.concatenate([p.reshape(-1) for p in parts])
    unit = 128 * PACK_ROWS
    padded = -(-flat.size // unit) * unit
    return jnp.pad(flat, (0, padded - flat.size)).reshape(-1, 128)


def _unpack(packed, shapes):
    flat = packed.reshape(-1)
    out, off = [], 0
    for s in shapes:
        n = 1
        for d in s:
            n *= d
        out.append(flat[off:off + n].reshape(s))
        off += n
    return out


def kernel(x, norm_ffn1, ffn1_w_up, ffn1_w_down, norm_mix, w_in, pool_w, pool_b, pool_scale, w_pool_up, conv_w, conv_b, lru_w_a, lru_b_a, lru_w_x, lru_b_x, lru_lambda, w_lru_up, w_out, norm_ffn2, ffn2_w_up, ffn2_w_down, final_norm, loss_target, m_norm_ffn1, m_ffn1_w_up, m_ffn1_w_down, m_norm_mix, m_w_in, m_pool_w, m_pool_b, m_pool_scale, m_w_pool_up, m_conv_w, m_conv_b, m_lru_w_a, m_lru_b_a, m_lru_w_x, m_lru_b_x, m_lru_lambda, m_w_lru_up, m_w_out, m_norm_ffn2, m_ffn2_w_up, m_ffn2_w_down, m_final_norm, v_norm_ffn1, v_ffn1_w_up, v_ffn1_w_down, v_norm_mix, v_w_in, v_pool_w, v_pool_b, v_pool_scale, v_w_pool_up, v_conv_w, v_conv_b, v_lru_w_a, v_lru_b_a, v_lru_w_x, v_lru_b_x, v_lru_lambda, v_w_lru_up, v_w_out, v_norm_ffn2, v_ffn2_w_up, v_ffn2_w_down, v_final_norm):
    W = dict(norm_ffn1=norm_ffn1, ffn1_w_up=ffn1_w_up, ffn1_w_down=ffn1_w_down, norm_mix=norm_mix, w_in=w_in, pool_w=pool_w, pool_b=pool_b,
             pool_scale=pool_scale, w_pool_up=w_pool_up, conv_w=conv_w, conv_b=conv_b, lru_w_a=lru_w_a, lru_b_a=lru_b_a, lru_w_x=lru_w_x,
             lru_b_x=lru_b_x, lru_lambda=lru_lambda, w_lru_up=w_lru_up, w_out=w_out, norm_ffn2=norm_ffn2, ffn2_w_up=ffn2_w_up,
             ffn2_w_down=ffn2_w_down, final_norm=final_norm)
    M = dict(norm_ffn1=m_norm_ffn1, ffn1_w_up=m_ffn1_w_up, ffn1_w_down=m_ffn1_w_down, norm_mix=m_norm_mix, w_in=m_w_in, pool_w=m_pool_w,
             pool_b=m_pool_b, pool_scale=m_pool_scale, w_pool_up=m_w_pool_up, conv_w=m_conv_w, conv_b=m_conv_b, lru_w_a=m_lru_w_a,
             lru_b_a=m_lru_b_a, lru_w_x=m_lru_w_x, lru_b_x=m_lru_b_x, lru_lambda=m_lru_lambda, w_lru_up=m_w_lru_up, w_out=m_w_out,
             norm_ffn2=m_norm_ffn2, ffn2_w_up=m_ffn2_w_up, ffn2_w_down=m_ffn2_w_down, final_norm=m_final_norm)
    V = dict(norm_ffn1=v_norm_ffn1, ffn1_w_up=v_ffn1_w_up, ffn1_w_down=v_ffn1_w_down, norm_mix=v_norm_mix, w_in=v_w_in, pool_w=v_pool_w,
             pool_b=v_pool_b, pool_scale=v_pool_scale, w_pool_up=v_w_pool_up, conv_w=v_conv_w, conv_b=v_conv_b, lru_w_a=v_lru_w_a,
             lru_b_a=v_lru_b_a, lru_w_x=v_lru_w_x, lru_b_x=v_lru_b_x, lru_lambda=v_lru_lambda, w_lru_up=v_w_lru_up, w_out=v_w_out,
             norm_ffn2=v_norm_ffn2, ffn2_w_up=v_ffn2_w_up, ffn2_w_down=v_ffn2_w_down, final_norm=v_final_norm)

    T, D = x.shape[1], x.shape[2]
    L = norm_ffn1.shape[0]
    P = pool_scale.shape[1]
    R = lru_lambda.shape[1]
    H, hd = lru_w_a.shape[1], lru_w_a.shape[2]
    CW = conv_w.shape[1]
    cs = ffn1_w_up.shape[2]
    ci = w_in.shape[2]
    xin = x.reshape(T, D)
    tgt = loss_target.reshape(T, D)
    dev = 4 * lax.axis_index("x") + 2 * lax.axis_index("y") + lax.axis_index("c")
    place = jnp.stack([lax.axis_index("c"), 2 * lax.axis_index("x") + lax.axis_index("y"), dev]).astype(jnp.int32)

    cw_flat = conv_w.reshape(L, -1)
    cw_pad = (-cw_flat.shape[1]) % 1024
    cw_tiles = jnp.pad(cw_flat, ((0, 0), (0, cw_pad))).reshape(L, -1, 128)

    def units(l):
        return SUBLAYERS if l == 0 else (tuple(n for u in SUBLAYERS for n in u),)

    queued = {"gather": (), "pair": (), "chip": ()}

    def gather_units_start(l):
        started_units = []
        for k, names in enumerate(SUBLAYERS):
            land = [place_own(cw_tiles, l, place, F32) if n == "conv_w" else place_own(W[n], l, place, BF16) for n in names]
            started_units.append((names, f"l{l}_u{k}") + gather_start(f"gather_start_l{l}_u{k}", land, queued["gather"]))
            queued["gather"] = (started_units[-1][-1],)
        return started_units

    def gather_unit_finish(unit, after):
        names, tag, send_sems, recv_sems, land, _tok = unit
        land = gather_wait(f"gather_wait_{tag}", land, send_sems, recv_sems, after)
        land = gather_forward(f"gather_forward_{tag}", land)
        g = dict(zip(names, land))
        one = lambda a: a.reshape((1,) + a.shape)
        w = {}
        for tag_, up, dn in (("1", "ffn1_w_up", "ffn1_w_down"), ("2", "ffn2_w_up", "ffn2_w_down")):
            if up in g:
                w["wup" + tag_], w["wd" + tag_] = one(g[up]), g[dn].reshape(1, -1, D)
        if "w_in" in g:
            cw_l = g["conv_w"].reshape(N_DEV, -1)[:, :cw_flat.shape[1]].reshape((N_DEV,) + conv_w.shape[1:])
            w.update(win=one(g["w_in"]), wlu=g["w_lru_up"].reshape(1, R, D), wout=g["w_out"].reshape(1, D, D),
                     wpu=g["w_pool_up"].transpose(1, 0, 2).reshape(1, P, D),
                     cw=cw_l.transpose(1, 0, 2).reshape(1, CW, R))
        return w

    def layer_params(l):
        vec = lambda a: a[l:l + 1].reshape(1, 1, -1)
        return dict(g1=vec(norm_ffn1), gm=vec(norm_mix), g2=vec(norm_ffn2), pb=vec(pool_b), ps=vec(pool_scale), cb=vec(conv_b),
                    ba=vec(lru_b_a), bx=vec(lru_b_x), lam=vec(lru_lambda), pw=pool_w[l:l + 1], wa=lru_w_a[l:l + 1], wx=lru_w_x[l:l + 1])

    AHEAD = 2
    started = {l: gather_units_start(l) for l in range(min(AHEAD, L))}
    saved, LW, LP = [], [], []
    xc = xin
    for l in range(L):
        todo = started.pop(l)
        w, p = {}, layer_params(l)

        def need(key, xc_now):
            while key not in w:
                tokens = [u[-1] for us in started.values() for u in us] + [u[-1] for u in todo[1:]]
                w.update(gather_unit_finish(todo.pop(0), [xc_now] + tokens))

        need("wup1", xc)
        if l + AHEAD < L:
            started[l + AHEAD] = gather_units_start(l + AHEAD)
        sv = {"x1": xc}
        sv["h1"], sv["u1"], sv["s1"] = ffn_up(xc, p["g1"], w["wup1"], 0)
        xc = ffn_down(sv["s1"], w["wd1"], xc, 0)
        sv["x2"] = xc
        need("win", xc)
        sv["h2"], sv["proj"] = mix_in(xc, p["gm"], w["win"], 0)
        sv["pm"] = pool_fwd(sv["proj"], p["pw"], p["pb"], p["ps"], 0)
        sv["hl"], sv["hs"] = lru_fwd(sv["proj"], w["cw"], p["cb"], p["wa"], p["ba"], p["wx"], p["bx"], p["lam"], P, 0)
        xc, sv["yp"], sv["yl"], sv["z"] = mix_out(sv["pm"], sv["hl"], sv["proj"], xc, w["wpu"], w["wlu"], w["wout"], P, 0)
        sv["x3"] = xc
        need("wup2", xc)
        sv["h3"], sv["u3"], sv["s3"] = ffn_up(xc, p["g2"], w["wup2"], 0)
        xc = ffn_down(sv["s3"], w["wd2"], xc, 0)
        saved.append(sv)
        LW.append(w)
        LP.append(p)

    loss_part, dx, d_final = loss_head(xc, final_norm.reshape(1, D), tgt)
    loss = lax.psum(loss_part[0, 0], ("x", "y", "c"))

    G = [dict() for _ in range(L)]
    small = {n: [None] * L for n in SMALL if n != "final_norm"}

    def to_slots(name, pair):
        if name == "w_pool_up":
            return tuple(a.reshape(1, P, N_DEV, D // N_DEV).transpose(0, 2, 1, 3) for a in pair)
        return tuple(a.reshape((1, N_DEV) + W[name].shape[1:]) for a in pair)

    def ffn_bwd(dy, sv, tag, wup, wd, gn, up_name, dn_name, norm_name, l, deps=()):
        dout, du = ffn_down_bwd(dy, wd, sv["u" + tag], 0, deps)
        du = du.reshape(N_DEV, T, cs)
        G[l][dn_name] = to_slots(dn_name, dw_tn("dw_down", sv["s" + tag], lambda tk: pl.BlockSpec((1, tk, cs), lambda g, k: (g, k, 0)),
                                                dout, lambda tk: pl.BlockSpec((tk, D), lambda g, k: (k, 0)), 4, cs, D, T))
        G[l][up_name] = to_slots(up_name, dw_tn("dw_up", sv["h" + tag], lambda tk: pl.BlockSpec((tk, D), lambda g, k: (k, 0)),
                                                du, lambda tk: pl.BlockSpec((1, tk, cs), lambda g, k: (g, k, 0)), N_DEV, D, cs, T))
        dxn, dg = dx_norm_bwd("ffn_dx", du, lambda tm: pl.BlockSpec((1, tm, cs), lambda j, i: (j, i, 0)), wup, N_DEV,
                              sv["x" + tag], gn, dy, 0)
        small[norm_name][l] = dg.reshape(D)
        return dxn

    pairing, in_flight = [], []

    def reduce_start(l, names, tag):
        names = [n for n in names if n != "conv_w"]
        send_sems, recv_sems, bufs, tok = pair_exchange_start(f"rs_pair_start_{tag}", [G[l][n][1] for n in names], queued["pair"])
        pairing.append((l, names, tag, send_sems, recv_sems, bufs))
        queued["pair"] = (tok,)
        return (tok,)

    def reduce_continue(after):
        l, names, tag, send_sems, recv_sems, bufs = pairing.pop(0)
        recv1 = pair_exchange_wait(f"rs_pair_wait_{tag}", bufs, send_sems, recv_sems, after)
        pair16 = [pair_sum(G[l][n][0], r_, place, 0) for n, r_ in zip(names, recv1)]
        send_sems, recv_sems, bufs, tok = chip_exchange_start(f"rs_chip_start_{tag}", pair16, queued["chip"])
        in_flight.append((l, names, tag, send_sems, recv_sems, bufs, recv1))
        queued["chip"] = (tok,)
        return (tok,)

    def boundary(l, k, dx_now):
        deps = reduce_continue([dx_now]) if pairing else ()
        if len(units(l)) > 1:
            deps += reduce_start(l, units(l)[k], f"l{l}_u{k}")
        elif k == 0:
            deps += reduce_start(l, units(l)[0], f"l{l}_u0")
        return deps

    deps = ()
    for l in reversed(range(L)):
        sv, w, p = saved[l], LW[l], LP[l]
        dx = ffn_bwd(dx, sv, "3", w["wup2"], w["wd2"], p["g2"], "ffn2_w_up", "ffn2_w_down", "norm_ffn2", l, deps)
        deps = boundary(l, 2, dx)
        dyb, dyp, dyl, dgp, dgl, dpm, dhl = mix_out_bwd(dx, sv["proj"], sv["yp"], sv["yl"], w["wpu"], w["wlu"], w["wout"], P, R, 0, deps)
        row = lambda wd_: (lambda tk: pl.BlockSpec((tk, wd_), lambda g, k: (k, 0)))
        G[l]["w_out"] = to_slots("w_out", dw_tn("dw_out", sv["z"], row(D), dyb, row(D), 1, D, D, T))
        G[l]["w_lru_up"] = to_slots("w_lru_up", dw_tn("dw_lru_up", sv["hl"], row(R), dyl, row(D), 1, R, D, T))
        G[l]["w_pool_up"] = to_slots("w_pool_up", dw_tn("dw_pool_up", sv["pm"], row(P), dyp, row(D), 1, P, D, T))
        du_lru, du_gelu, dcw, dcb, dwa, dba, dwx, dbx, dlam = lru_bwd(
            sv["proj"], sv["hs"], dhl, w["cw"], p["cb"], p["wa"], p["ba"], p["wx"], p["bx"], p["lam"], P, 0)
        du_pool, dpw, dpb, dpsc = pool_bwd(sv["proj"], dpm, p["pw"], p["pb"], p["ps"], 0)
        dproj = jnp.concatenate([du_pool, du_lru, du_gelu, dgp, dgl], axis=1)
        G[l]["w_in"] = to_slots("w_in", dw_tn("dw_in", sv["h2"], row(D), dproj, lambda tk: pl.BlockSpec((tk, ci), lambda g, k: (k, g)),
                                              N_DEV, D, ci, T))
        dx, dgm = dx_norm_bwd("mix_dx", dproj, lambda tm: pl.BlockSpec((tm, ci), lambda j, i: (i, j)), w["win"], N_DEV,
                              sv["x2"], p["gm"], dx, 0)
        small["norm_mix"][l] = dgm.reshape(D)
        small["pool_w"][l], small["pool_b"][l], small["pool_scale"][l] = dpw[0], dpb.reshape(pool_b.shape[1:]), dpsc.reshape(P)
        small["conv_w"][l], small["conv_b"][l] = dcw[0], dcb.reshape(R)
        small["lru_w_a"][l], small["lru_b_a"][l] = dwa[0], dba.reshape(H, hd)
        small["lru_w_x"][l], small["lru_b_x"][l] = dwx[0], dbx.reshape(H, hd)
        small["lru_lambda"][l] = dlam.reshape(R)
        deps = boundary(l, 1, dx)
        dx = ffn_bwd(dx, sv, "1", w["wup1"], w["wd1"], p["g1"], "ffn1_w_up", "ffn1_w_down", "norm_ffn1", l, deps)
        deps = boundary(l, 0, dx)

    grad_x = dx.reshape(x.shape)

    small_parts = [jnp.stack(small[n]) for n in SMALL if n != "final_norm"] + [d_final.reshape(D)]
    small_shapes = [p.shape for p in small_parts]
    gpack = _pack(small_parts).reshape(1, N_DEV, -1, 128)
    recv1_s = pair_exchange("rs_pair_exchange_small", [gpack])[0]
    pair_s = pair_sum(gpack, recv1_s, place, 0)
    recv2_s = chip_exchange("rs_chip_exchange_small", [pair_s])[0]
    while pairing:
        reduce_continue([recv2_s])

    outs = {n: None for n in BIG}
    after = [dx, recv2_s]
    for l, names, tag, send_sems, recv_sems, bufs, recv1 in in_flight:
        recv2 = chip_exchange_wait(f"rs_chip_wait_{tag}", bufs, send_sems, recv_sems, after)
        for i, n in enumerate(names):
            outs[n] = grad_sum_adamw(G[l][n][0], recv1[i], recv2[i], W[n], M[n], V[n], place, l, outs[n])
        after = [outs[n][0] for n in names]
    out_g, out_d, out_m, out_v = ({n: outs[n][k] for n in BIG} for k in range(4))

    gs = grad_sum(gpack, recv1_s, recv2_s, place)
    gs_all = all_gather("all_gather_small_grads", [gs.reshape((1,) + gs.shape)])[0]
    gs_all = gs_all.reshape(-1, 128)
    small_g = dict(zip(SMALL, _unpack(gs_all, small_shapes)))
    full_shapes = [W[n].shape if n != "conv_w" else small_shapes[SMALL.index("conv_w")] for n in SMALL]
    rep = [n for n in SMALL if n != "conv_w"]
    rep_shapes = [W[n].shape for n in rep]
    wp, mp, vp = (_pack([S[n] for n in rep]) for S in (W, M, V))
    gp = _pack([small_g[n] for n in rep])
    dp, nmp, nvp = adamw(wp, gp, mp, vp)
    for S, packed in ((out_d, dp), (out_m, nmp), (out_v, nvp)):
        S.update(zip(rep, _unpack(packed, rep_shapes)))
    for n in rep:
        out_g[n] = small_g[n]
    cwc = conv_w.shape[2]
    gcw = lax.dynamic_slice_in_dim(small_g["conv_w"], dev * cwc, cwc, axis=2)
    cw2 = lambda a: a.reshape(-1, cwc)
    pad_rows = (-cw2(conv_w).shape[0]) % 8
    padr = lambda a: jnp.pad(cw2(a), ((0, pad_rows), (0, 0)))
    dcw_, mcw_, vcw_ = adamw(padr(conv_w), padr(gcw), padr(M["conv_w"]), padr(V["conv_w"]))
    nrow = cw2(conv_w).shape[0]
    out_g["conv_w"] = gcw
    out_d["conv_w"], out_m["conv_w"], out_v["conv_w"] = (a[:nrow].reshape(conv_w.shape) for a in (dcw_, mcw_, vcw_))
    del full_shapes

    return (loss, grad_x, *[out_g[n] for n in NAMES], *[out_d[n] for n in NAMES], *[out_m[n] for n in NAMES], *[out_v[n] for n in NAMES])
```

```python
import functools

import jax
import jax.numpy as jnp
from jax import lax
from jax.experimental import pallas as pl
from jax.experimental.pallas import tpu as pltpu

F32, BF16 = jnp.float32, jnp.bfloat16
EPS = 1e-6
LRU_C = 8.0
POOL_WINDOWS = (2, 4, 8, 16)
ADAM_LR, ADAM_B1, ADAM_B2, ADAM_EPS, ADAM_WD, ADAM_STEP = 0.001, 0.9, 0.999, 1e-08, 0.01, 10
N_DEV = 8
N_CHIP = 4
MESH = pl.DeviceIdType.MESH
V7X_VMEM_LIMIT = 56 * 1024 * 1024
ROW_TILE = 512
WIDE_TILE = 1024
SUM_TILE = 2048
ANY = pl.BlockSpec(memory_space=pl.ANY)

_pallas_call = pl.pallas_call


def _cp(*sem):
    return pltpu.CompilerParams(dimension_semantics=sem if sem else None, vmem_limit_bytes=V7X_VMEM_LIMIT)


def _tile(n, t):
    t = min(n, t)
    assert n % t == 0, (n, t)
    return t


def _dot(a, b):
    return jnp.dot(a, b, preferred_element_type=F32)


def _dot_nt(a, b):
    return lax.dot_general(a, b, (((1,), (1,)), ((), ())), preferred_element_type=F32)


def _dot_tn(a, b):
    return lax.dot_general(a, b, (((0,), (0,)), ((), ())), preferred_element_type=F32)


def _rms(xv):
    r = lax.rsqrt(jnp.mean(xv * xv, axis=-1, keepdims=True) + EPS)
    return xv * r, r


def _rms_bwd(dh, xv, gv, dy):
    n, r = _rms(xv)
    dn = dh * gv
    dx = dy + r * (dn - n * jnp.mean(dn * n, axis=-1, keepdims=True))
    return dx, jnp.sum(dh * n, axis=0, keepdims=True)


def _shift_down(x, k, fill=0.0):
    if k == 0:
        return x
    rows = lax.broadcasted_iota(jnp.int32, x.shape, 0)
    return jnp.where(rows >= k, pltpu.roll(x, k, 0), fill)


def _shift_up(x, k, fill=0.0):
    if k == 0:
        return x
    n = x.shape[0]
    rows = lax.broadcasted_iota(jnp.int32, x.shape, 0)
    return jnp.where(rows < n - k, pltpu.roll(x, n - k, 0), fill)


def _sigmoid(x):
    return 0.5 * jnp.tanh(0.5 * x) + 0.5


_GELU_K = 0.7978845608028654
_GELU_C = 0.044715


def _gelu(x):
    th = jnp.tanh(_GELU_K * (x + _GELU_C * x * x * x))
    return 0.5 * x * (1.0 + th), th


def _gelu_grad(x, th):
    return 0.5 * (1.0 + th) + 0.5 * x * (1.0 - th * th) * _GELU_K * (1.0 + 3.0 * _GELU_C * x * x)


def ffn_up(x, g, wup, l):
    T, D = x.shape
    cs = wup.shape[-2]
    tm = _tile(T, WIDE_TILE)
    ni = T // tm

    def body(x_ref, g_ref, wa_ref, wb_ref, h_ref, u_ref, s_ref, hs_ref):
        rows = pl.ds(pl.multiple_of(pl.program_id(1) * tm, tm), tm)

        @pl.when(pl.program_id(0) == 0)
        def _():
            n, _r = _rms(x_ref[...])
            hv = (n * g_ref[0]).astype(BF16)
            hs_ref[rows, :] = hv
            h_ref[...] = hv

        hv = hs_ref[rows, :]
        a = _dot_nt(hv, wa_ref[0, 0])
        b = _dot_nt(hv, wb_ref[0, 0])
        u_ref[0, 0] = a.astype(BF16)
        u_ref[1, 0] = b.astype(BF16)
        s_ref[0] = (a * _sigmoid(a) * b).astype(BF16)

    first = lambda j, i: (jnp.where(j == 0, i, ni - 1), 0)
    return _pallas_call(
        body, name="ffn_up", grid=(4, ni),
        in_specs=[pl.BlockSpec((tm, D), first), pl.BlockSpec((1, 1, D), lambda j, i: (l, 0, 0)),
                  pl.BlockSpec((1, 1, cs, D), lambda j, i: (l, j, 0, 0)), pl.BlockSpec((1, 1, cs, D), lambda j, i: (l, j + 4, 0, 0))],
        out_specs=[pl.BlockSpec((tm, D), first), pl.BlockSpec((2, 1, tm, cs), lambda j, i: (0, j, i, 0)),
                   pl.BlockSpec((1, tm, cs), lambda j, i: (j, i, 0))],
        out_shape=[jax.ShapeDtypeStruct((T, D), BF16), jax.ShapeDtypeStruct((2, 4, T, cs), BF16), jax.ShapeDtypeStruct((4, T, cs), BF16)],
        scratch_shapes=[pltpu.VMEM((T, D), BF16)],
        compiler_params=_cp("arbitrary", "arbitrary"),
    )(x, g, wup, wup)


def ffn_down(s, wd, x, l, deps=()):
    _, T, cs = s.shape
    D = x.shape[1]
    tm = _tile(T, WIDE_TILE)

    def body(s_ref, w_ref, x_ref, *rest):
        o_ref, acc_ref = rest[len(deps):]
        j = pl.program_id(1)

        @pl.when(j == 0)
        def _():
            acc_ref[...] = jnp.zeros_like(acc_ref)

        acc_ref[...] += _dot(s_ref[0], w_ref[0])

        @pl.when(j == 3)
        def _():
            o_ref[...] = x_ref[...] + 0.5 * acc_ref[...]

    return _pallas_call(
        body, name="ffn_down", grid=(T // tm, 4),
        in_specs=[pl.BlockSpec((1, tm, cs), lambda i, j: (j, i, 0)), pl.BlockSpec((1, cs, D), lambda i, j: (l, j, 0)),
                  pl.BlockSpec((tm, D), lambda i, j: (i, 0))] + [ANY] * len(deps),
        out_specs=pl.BlockSpec((tm, D), lambda i, j: (i, 0)),
        out_shape=jax.ShapeDtypeStruct((T, D), F32),
        scratch_shapes=[pltpu.VMEM((tm, D), F32)],
        compiler_params=_cp("parallel", "arbitrary"),
    )(s, wd, x, *deps)


def mix_in(x, g, win, l):
    T, D = x.shape
    ci = win.shape[-1]
    tm = _tile(T, WIDE_TILE)
    ni = T // tm

    def body(x_ref, g_ref, w_ref, h_ref, p_ref, hs_ref):
        rows = pl.ds(pl.multiple_of(pl.program_id(1) * tm, tm), tm)

        @pl.when(pl.program_id(0) == 0)
        def _():
            n, _r = _rms(x_ref[...])
            hv = (n * g_ref[0]).astype(BF16)
            hs_ref[rows, :] = hv
            h_ref[...] = hv

        p_ref[...] = _dot(hs_ref[rows, :], w_ref[0, 0]).astype(BF16)

    first = lambda j, i: (jnp.where(j == 0, i, ni - 1), 0)
    return _pallas_call(
        body, name="mix_in", grid=(N_DEV, ni),
        in_specs=[pl.BlockSpec((tm, D), first), pl.BlockSpec((1, 1, D), lambda j, i: (l, 0, 0)),
                  pl.BlockSpec((1, 1, D, ci), lambda j, i: (l, j, 0, 0))],
        out_specs=[pl.BlockSpec((tm, D), first), pl.BlockSpec((tm, ci), lambda j, i: (i, j))],
        out_shape=[jax.ShapeDtypeStruct((T, D), BF16), jax.ShapeDtypeStruct((T, N_DEV * ci), BF16)],
        scratch_shapes=[pltpu.VMEM((T, D), BF16)],
        compiler_params=_cp("arbitrary", "arbitrary"),
    )(x, g, win)


def _inv_count(T, w):
    t = lax.broadcasted_iota(jnp.int32, (T, 1), 0)
    return 1.0 / jnp.minimum(t + 1, w).astype(F32)


def _pooled(ug, w, inv):
    s = ug
    k = 1
    while k < w:
        s = s + _shift_down(s, k)
        k *= 2
    return s * inv - ug


def pool_fwd(proj, pw, pb, ps, l):
    T = proj.shape[0]
    _, G, gd, _ = pw.shape
    P = G * gd

    def body(u_ref, w_ref, b_ref, s_ref, o_ref):
        for gi in range(G):
            cols = slice(gi * gd, (gi + 1) * gd)
            ug = u_ref[:, cols].astype(F32)
            pooled = _pooled(ug, POOL_WINDOWS[gi], _inv_count(T, POOL_WINDOWS[gi]))
            mixed = _dot(pooled.astype(BF16), w_ref[0, gi].astype(BF16)) + b_ref[0, :, cols]
            o_ref[:, cols] = (mixed * s_ref[0, :, cols]).astype(BF16)

    return _pallas_call(
        body, name="pool_fwd", grid=(1,),
        in_specs=[pl.BlockSpec((T, P), lambda i: (0, 0)), pl.BlockSpec((1, G, gd, gd), lambda i: (l, 0, 0, 0)),
                  pl.BlockSpec((1, 1, P), lambda i: (l, 0, 0)), pl.BlockSpec((1, 1, P), lambda i: (l, 0, 0))],
        out_specs=pl.BlockSpec((T, P), lambda i: (0, 0)),
        out_shape=jax.ShapeDtypeStruct((T, P), BF16),
        compiler_params=_cp("arbitrary"),
    )(proj, pw, pb, ps)


def _conv(u, cw_ref, cb):
    CW = cw_ref.shape[1]
    v = cb
    for k in range(CW):
        v = v + cw_ref[0, k:k + 1, :] * _shift_down(u, CW - 1 - k)
    return v


def _softplus(z):
    return jnp.maximum(z, 0.0) + jnp.log1p(jnp.exp(-jnp.abs(z)))


def _lru_gates(v, wa_ref, ba, wx_ref, bx, lam):
    vb = v.astype(BF16)
    r = _sigmoid(_dot(vb, wa_ref[0, 0].astype(BF16)) + ba)
    i = _sigmoid(_dot(vb, wx_ref[0, 0].astype(BF16)) + bx)
    sp = _softplus(-lam)
    log_a = -LRU_C * r * sp
    a = jnp.exp(log_a)
    m2 = -jnp.tanh(log_a) * (a * a + 1.0)
    inv_mult = lax.rsqrt(m2)
    mult = jnp.where(m2 > 0.0, m2 * inv_mult, 0.0)
    return r, i, sp, a, mult, inv_mult


def _scan_fwd(a_ref, b_ref, o_ref):
    T, W = a_ref.shape
    rows = lax.broadcasted_iota(jnp.int32, (8, W), 0)

    def step(t, carry):
        r0 = pl.multiple_of(t * 8, 8)
        A = a_ref[pl.ds(r0, 8), :]
        B = b_ref[pl.ds(r0, 8), :]
        for s in (1, 2, 4):
            keep = rows >= s
            As = jnp.where(keep, pltpu.roll(A, s, 0), 1.0)
            Bs = jnp.where(keep, pltpu.roll(B, s, 0), 0.0)
            B = A * Bs + B
            A = A * As
        h = B + A * carry
        o_ref[pl.ds(r0, 8), :] = h
        return jnp.broadcast_to(h[7:8, :], (8, W))

    lax.fori_loop(0, T // 8, step, jnp.zeros((8, W), F32), unroll=8)


def _scan_bwd(a_ref, b_ref, o_ref):
    T, W = a_ref.shape
    rows = lax.broadcasted_iota(jnp.int32, (8, W), 0)
    nt = T // 8

    def step(t, carry):
        r0 = pl.multiple_of((nt - 1 - t) * 8, 8)
        A = a_ref[pl.ds(r0, 8), :]
        B = b_ref[pl.ds(r0, 8), :]
        for s in (1, 2, 4):
            keep = rows < 8 - s
            As = jnp.where(keep, pltpu.roll(A, 8 - s, 0), 1.0)
            Bs = jnp.where(keep, pltpu.roll(B, 8 - s, 0), 0.0)
            B = A * Bs + B
            A = A * As
        y = B + A * carry
        o_ref[pl.ds(r0, 8), :] = y
        return jnp.broadcast_to(y[0:1, :], (8, W))

    lax.fori_loop(0, nt, step, jnp.zeros((8, W), F32), unroll=8)


def _lru_specs(T, hd, P, R, CW, l):
    ob, gb = P // hd, (P + R) // hd
    vec = pl.BlockSpec((1, 1, hd), lambda h: (l, 0, h))
    mat = pl.BlockSpec((1, 1, hd, hd), lambda h: (l, h, 0, 0))
    return [pl.BlockSpec((T, hd), lambda h: (0, ob + h)), pl.BlockSpec((T, hd), lambda h: (0, gb + h)),
            pl.BlockSpec((1, CW, hd), lambda h: (l, 0, h)), vec, mat, vec, mat, vec, vec]


def lru_fwd(proj, cw, cb, wa, ba, wx, bx, lam, P, l):
    T = proj.shape[0]
    _, H, hd, _ = wa.shape
    R = H * hd
    CW = cw.shape[1]
    assert P % hd == 0 and T % 8 == 0

    def body(u_ref, ug_ref, cw_ref, cb_ref, wa_ref, ba_ref, wx_ref, bx_ref, lam_ref, hl_ref, hs_ref, a_s, b_s):
        v = _conv(u_ref[...].astype(F32), cw_ref, cb_ref[0])
        _r, i, _sp, a, mult, _im = _lru_gates(v, wa_ref, ba_ref[0], wx_ref, bx_ref[0], lam_ref[0])
        a_s[...] = a
        b_s[...] = mult * (i * v)
        _scan_fwd(a_s, b_s, hs_ref)
        ge, _th = _gelu(ug_ref[...].astype(F32))
        hl_ref[...] = (hs_ref[...] * ge).astype(BF16)

    out = pl.BlockSpec((T, hd), lambda h: (0, h))
    return _pallas_call(
        body, name="lru_fwd", grid=(H,),
        in_specs=_lru_specs(T, hd, P, R, CW, l),
        out_specs=[out, out],
        out_shape=[jax.ShapeDtypeStruct((T, R), BF16), jax.ShapeDtypeStruct((T, R), F32)],
        scratch_shapes=[pltpu.VMEM((T, hd), F32)] * 2,
        compiler_params=_cp("parallel"),
    )(proj, proj, cw, cb, wa, ba, wx, bx, lam)


def mix_out(pm, hl, proj, x, wpu, wlu, wout, P, l, deps=()):
    T, D = x.shape
    R = hl.shape[1]
    tm = _tile(T, ROW_TILE)
    assert (P + 2 * R) % D == 0
    gb = (P + 2 * R) // D

    def body(pm_ref, hl_ref, gp_ref, gl_ref, x_ref, wpu_ref, wlu_ref, wo_ref, *rest):
        o_ref, yp_ref, yl_ref, z_ref = rest[len(deps):]
        yp = _dot(pm_ref[...], wpu_ref[0])
        yl = _dot(hl_ref[...], wlu_ref[0])
        z = (_sigmoid(gp_ref[...].astype(F32)) * yp + _sigmoid(gl_ref[...].astype(F32)) * yl).astype(BF16)
        yp_ref[...] = yp.astype(BF16)
        yl_ref[...] = yl.astype(BF16)
        z_ref[...] = z
        o_ref[...] = x_ref[...] + _dot(z, wo_ref[0])

    row = lambda w: pl.BlockSpec((tm, w), lambda i: (i, 0))
    return _pallas_call(
        body, name="mix_out", grid=(T // tm,),
        in_specs=[row(P), row(R), pl.BlockSpec((tm, D), lambda i: (i, gb)), pl.BlockSpec((tm, D), lambda i: (i, gb + 1)), row(D),
                  pl.BlockSpec((1, P, D), lambda i: (l, 0, 0)), pl.BlockSpec((1, R, D), lambda i: (l, 0, 0)),
                  pl.BlockSpec((1, D, D), lambda i: (l, 0, 0))] + [ANY] * len(deps),
        out_specs=[row(D)] * 4,
        out_shape=[jax.ShapeDtypeStruct((T, D), F32)] + [jax.ShapeDtypeStruct((T, D), BF16)] * 3,
        compiler_params=_cp("parallel"),
    )(pm, hl, proj, proj, x, wpu, wlu, wout, *deps)


def loss_head(x, gf, tgt):
    T, D = x.shape
    tm = _tile(T, ROW_TILE)

    def body(x_ref, g_ref, t_ref, loss_ref, dx_ref, dg_ref):
        @pl.when(pl.program_id(0) == 0)
        def _():
            loss_ref[...] = jnp.zeros_like(loss_ref)
            dg_ref[...] = jnp.zeros_like(dg_ref)

        xv = x_ref[...]
        gv = g_ref[...]
        n, _r = _rms(xv)
        e = n * gv - t_ref[...]
        loss_ref[...] += 0.5 * jnp.sum(jnp.sum(e * e, axis=-1, keepdims=True), axis=0, keepdims=True) / D
        dx, dg = _rms_bwd(e * (1.0 / D), xv, gv, 0.0)
        dx_ref[...] = dx
        dg_ref[...] += dg

    return _pallas_call(
        body, name="loss_head", grid=(T // tm,),
        in_specs=[pl.BlockSpec((tm, D), lambda i: (i, 0)), pl.BlockSpec((1, D), lambda i: (0, 0)), pl.BlockSpec((tm, D), lambda i: (i, 0))],
        out_specs=[pl.BlockSpec((1, 1), lambda i: (0, 0)), pl.BlockSpec((tm, D), lambda i: (i, 0)), pl.BlockSpec((1, D), lambda i: (0, 0))],
        out_shape=[jax.ShapeDtypeStruct((1, 1), F32), jax.ShapeDtypeStruct((T, D), F32), jax.ShapeDtypeStruct((1, D), F32)],
        compiler_params=_cp("arbitrary"),
    )(x, gf, tgt)


def ffn_down_bwd(dy, wd, u, l, deps=()):
    T, D = dy.shape
    cs = u.shape[-1]
    tm = _tile(T, WIDE_TILE)
    ni = T // tm

    def body(dy_ref, w_ref, u_ref, *rest):
        do_ref, du_ref, dyb_ref = rest[len(deps):]
        rows = pl.ds(pl.multiple_of(pl.program_id(1) * tm, tm), tm)

        @pl.when(pl.program_id(0) == 0)
        def _():
            d = (0.5 * dy_ref[...]).astype(BF16)
            dyb_ref[rows, :] = d
            do_ref[...] = d

        ds = _dot_nt(dyb_ref[rows, :], w_ref[0])
        a = u_ref[0, 0].astype(F32)
        b = u_ref[1, 0].astype(F32)
        sg = _sigmoid(a)
        du_ref[0, 0] = (ds * b * (sg * (1.0 + a * (1.0 - sg)))).astype(BF16)
        du_ref[1, 0] = (ds * (a * sg)).astype(BF16)

    first = lambda j, i: (jnp.where(j == 0, i, ni - 1), 0)
    blk = pl.BlockSpec((2, 1, tm, cs), lambda j, i: (0, j, i, 0))
    return _pallas_call(
        body, name="ffn_down_bwd", grid=(4, ni),
        in_specs=[pl.BlockSpec((tm, D), first), pl.BlockSpec((1, cs, D), lambda j, i: (l, j, 0)), blk] + [ANY] * len(deps),
        out_specs=[pl.BlockSpec((tm, D), first), blk],
        out_shape=[jax.ShapeDtypeStruct((T, D), BF16), jax.ShapeDtypeStruct((2, 4, T, cs), BF16)],
        scratch_shapes=[pltpu.VMEM((T, D), BF16)],
        compiler_params=_cp("arbitrary", "arbitrary"),
    )(dy, wd, u, *deps)


def dw_tn(name, a, a_spec, b, b_spec, G, M, N, T):
    tk = _tile(T, SUM_TILE)
    nk = T // tk

    def body(a_ref, b_ref, o32_ref, o16_ref, acc_ref):
        k = pl.program_id(1)

        @pl.when(k == 0)
        def _():
            acc_ref[...] = jnp.zeros_like(acc_ref)

        av = a_ref[0] if len(a_ref.shape) == 3 else a_ref[...]
        bv = b_ref[0] if len(b_ref.shape) == 3 else b_ref[...]
        acc_ref[...] += _dot_tn(av, bv)

        @pl.when(k == nk - 1)
        def _():
            o32_ref[0, 0] = acc_ref[...]
            o16_ref[0, 0] = acc_ref[...].astype(BF16)

    out = pl.BlockSpec((1, 1, M, N), lambda g, k: (0, g, 0, 0))
    return _pallas_call(
        body, name=name, grid=(G, nk),
        in_specs=[a_spec(tk), b_spec(tk)], out_specs=[out, out],
        out_shape=[jax.ShapeDtypeStruct((1, G, M, N), F32), jax.ShapeDtypeStruct((1, G, M, N), BF16)],
        scratch_shapes=[pltpu.VMEM((M, N), F32)],
        compiler_params=_cp("parallel", "arbitrary"),
    )(a, b)


def dx_norm_bwd(name, dact, d_spec, w, G, x, g, dy, l, w_transposed=False):
    T, D = x.shape
    wblk = w.shape[-2:]
    tm = _tile(T, WIDE_TILE)
    ni = T // tm
    ch = _tile(tm, ROW_TILE // 2)

    def body(d_ref, w_ref, x_ref, g_ref, dy_ref, dx_ref, dg_ref, acc_ref):
        j, i = pl.program_id(0), pl.program_id(1)
        rows = pl.ds(pl.multiple_of(i * tm, tm), tm)

        @pl.when(jnp.logical_and(i == 0, j == 0))
        def _():
            dg_ref[...] = jnp.zeros_like(dg_ref)

        dv = d_ref[0] if len(d_ref.shape) == 3 else d_ref[...]
        part = _dot(dv, w_ref[0, 0]) if w_transposed else _dot_nt(dv, w_ref[0, 0])

        @pl.when(j == 0)
        def _():
            acc_ref[rows, :] = part

        @pl.when(j > 0)
        def _():
            acc_ref[rows, :] += part

        @pl.when(j == G - 1)
        def _():
            dg = jnp.zeros((1, D), F32)
            for c0 in range(0, tm, ch):
                part_rows = pl.ds(pl.multiple_of(i * tm + c0, ch), ch)
                dx, dgc = _rms_bwd(acc_ref[part_rows, :], x_ref[c0:c0 + ch, :], g_ref[0], dy_ref[c0:c0 + ch, :])
                dx_ref[c0:c0 + ch, :] = dx
                dg = dg + dgc
            dg_ref[...] += dg

    last = pl.BlockSpec((tm, D), lambda j, i: (jnp.where(j == G - 1, i, 0), 0))
    return _pallas_call(
        body, name=name, grid=(G, ni),
        in_specs=[d_spec(tm), pl.BlockSpec((1, 1) + wblk, lambda j, i: (l, j, 0, 0)), last, pl.BlockSpec((1, 1, D), lambda j, i: (l, 0, 0)), last],
        out_specs=[last, pl.BlockSpec((1, D), lambda j, i: (0, 0))],
        out_shape=[jax.ShapeDtypeStruct((T, D), F32), jax.ShapeDtypeStruct((1, D), F32)],
        scratch_shapes=[pltpu.VMEM((T, D), F32)],
        compiler_params=_cp("arbitrary", "arbitrary"),
    )(dact, w, x, g, dy)


def mix_out_bwd(dy, proj, yp, yl, wpu, wlu, wout, P, R, l, deps=()):
    T, D = dy.shape
    tm = _tile(T, ROW_TILE)
    gb = (P + 2 * R) // D

    def body(dy_ref, gp_ref, gl_ref, yp_ref, yl_ref, wpu_ref, wlu_ref, wo_ref, *rest):
        dyb_ref, dyp_ref, dyl_ref, dgp_ref, dgl_ref, dpm_ref, dhl_ref = rest[len(deps):]
        dyb = dy_ref[...].astype(BF16)
        dyb_ref[...] = dyb
        dz = _dot_nt(dyb, wo_ref[0])
        sp = _sigmoid(gp_ref[...].astype(F32))
        sl = _sigmoid(gl_ref[...].astype(F32))
        dgp_ref[...] = (dz * yp_ref[...].astype(F32) * sp * (1.0 - sp)).astype(BF16)
        dgl_ref[...] = (dz * yl_ref[...].astype(F32) * sl * (1.0 - sl)).astype(BF16)
        dyp = (dz * sp).astype(BF16)
        dyl = (dz * sl).astype(BF16)
        dyp_ref[...] = dyp
        dyl_ref[...] = dyl
        dpm_ref[...] = _dot_nt(dyp, wpu_ref[0]).astype(BF16)
        dhl_ref[...] = _dot_nt(dyl, wlu_ref[0]).astype(BF16)

    row = lambda w: pl.BlockSpec((tm, w), lambda i: (i, 0))
    return _pallas_call(
        body, name="mix_out_bwd", grid=(T // tm,),
        in_specs=[row(D), pl.BlockSpec((tm, D), lambda i: (i, gb)), pl.BlockSpec((tm, D), lambda i: (i, gb + 1)), row(D), row(D),
                  pl.BlockSpec((1, P, D), lambda i: (l, 0, 0)), pl.BlockSpec((1, R, D), lambda i: (l, 0, 0)),
                  pl.BlockSpec((1, D, D), lambda i: (l, 0, 0))] + [ANY] * len(deps),
        out_specs=[row(D)] * 5 + [row(P), row(R)],
        out_shape=[jax.ShapeDtypeStruct((T, D), BF16)] * 5 + [jax.ShapeDtypeStruct((T, P), BF16), jax.ShapeDtypeStruct((T, R), BF16)],
        compiler_params=_cp("parallel"),
    )(dy, proj, proj, yp, yl, wpu, wlu, wout, *deps)


def lru_bwd(proj, hs, dhl, cw, cb, wa, ba, wx, bx, lam, P, l):
    T = proj.shape[0]
    _, H, hd, _ = wa.shape
    R = H * hd
    CW = cw.shape[1]

    def body(u_ref, ug_ref, cw_ref, cb_ref, wa_ref, ba_ref, wx_ref, bx_ref, lam_ref, hs_ref, dhl_ref,
             du_ref, dug_ref, dcw_ref, dcb_ref, dwa_ref, dba_ref, dwx_ref, dbx_ref, dlam_ref, c_s, g_s, y_s):
        u = u_ref[...].astype(F32)
        v = _conv(u, cw_ref, cb_ref[0])
        lam = lam_ref[0]
        r, i, sp, a, mult, inv_mult = _lru_gates(v, wa_ref, ba_ref[0], wx_ref, bx_ref[0], lam)
        ug = ug_ref[...].astype(F32)
        ge, th = _gelu(ug)
        hs = hs_ref[...]
        dhl = dhl_ref[...].astype(F32)
        dug_ref[...] = (dhl * hs * _gelu_grad(ug, th)).astype(BF16)
        c_s[...] = _shift_up(a, 1)
        g_s[...] = dhl * ge
        _scan_bwd(c_s, g_s, y_s)
        y = y_s[...]
        da = y * _shift_down(hs, 1)
        iv = i * v
        dlog_a = da * a - (y * iv) * (a * a) * inv_mult
        div = y * mult
        dpa = (dlog_a * (-LRU_C) * sp) * r * (1.0 - r)
        dpx = (div * v) * i * (1.0 - i)
        dsp = jnp.sum(dlog_a * (-LRU_C) * r, axis=0, keepdims=True)
        dlam_ref[0] = -dsp * _sigmoid(-lam)
        vb = v.astype(BF16)
        dpab, dpxb = dpa.astype(BF16), dpx.astype(BF16)
        dwa_ref[0, 0] = _dot_tn(vb, dpab)
        dwx_ref[0, 0] = _dot_tn(vb, dpxb)
        dba_ref[0] = jnp.sum(dpa, axis=0, keepdims=True)
        dbx_ref[0] = jnp.sum(dpx, axis=0, keepdims=True)
        dv = div * i + _dot_nt(dpab, wa_ref[0, 0].astype(BF16)) + _dot_nt(dpxb, wx_ref[0, 0].astype(BF16))
        dcb_ref[0] = jnp.sum(dv, axis=0, keepdims=True)
        du = jnp.zeros_like(dv)
        for k in range(CW):
            du = du + cw_ref[0, k:k + 1, :] * _shift_up(dv, CW - 1 - k)
            dcw_ref[0, k:k + 1, :] = jnp.sum(dv * _shift_down(u, CW - 1 - k), axis=0, keepdims=True)
        du_ref[...] = du.astype(BF16)

    col = pl.BlockSpec((T, hd), lambda h: (0, h))
    vec = pl.BlockSpec((1, 1, hd), lambda h: (0, 0, h))
    mat = pl.BlockSpec((1, 1, hd, hd), lambda h: (0, h, 0, 0))
    vshape = jax.ShapeDtypeStruct((1, 1, R), F32)
    mshape = jax.ShapeDtypeStruct((1, H, hd, hd), F32)
    return _pallas_call(
        body, name="lru_bwd", grid=(H,),
        in_specs=_lru_specs(T, hd, P, R, CW, l) + [col, col],
        out_specs=[col, col, pl.BlockSpec((1, CW, hd), lambda h: (0, 0, h)), vec, mat, vec, mat, vec, vec],
        out_shape=[jax.ShapeDtypeStruct((T, R), BF16)] * 2 + [jax.ShapeDtypeStruct((1, CW, R), F32), vshape, mshape, vshape, mshape, vshape, vshape],
        scratch_shapes=[pltpu.VMEM((T, hd), F32)] * 3,
        compiler_params=_cp("parallel"),
    )(proj, proj, cw, cb, wa, ba, wx, bx, lam, hs, dhl)


def pool_bwd(proj, dpm, pw, pb, ps, l):
    T = proj.shape[0]
    _, G, gd, _ = pw.shape
    P = G * gd

    def body(u_ref, d_ref, w_ref, b_ref, s_ref, du_ref, dw_ref, db_ref, dsc_ref):
        for gi in range(G):
            cols = slice(gi * gd, (gi + 1) * gd)
            w = POOL_WINDOWS[gi]
            inv = _inv_count(T, w)
            ug = u_ref[:, cols].astype(F32)
            pooled = _pooled(ug, w, inv).astype(BF16)
            wb = w_ref[0, gi].astype(BF16)
            mixed = _dot(pooled, wb) + b_ref[0, :, cols]
            dpm_g = d_ref[:, cols].astype(F32)
            dsc_ref[0, :, cols] = jnp.sum(dpm_g * mixed, axis=0, keepdims=True)
            dmixed = dpm_g * s_ref[0, :, cols]
            db_ref[0, :, cols] = jnp.sum(dmixed, axis=0, keepdims=True)
            dmb = dmixed.astype(BF16)
            dw_ref[0, gi] = _dot_tn(pooled, dmb)
            dpooled = _dot_nt(dmb, wb)
            s = dpooled * inv
            k = 1
            while k < w:
                s = s + _shift_up(s, k)
                k *= 2
            du_ref[:, cols] = (s - dpooled).astype(BF16)

    vec = pl.BlockSpec((1, 1, P), lambda i: (l, 0, 0))
    ovec = pl.BlockSpec((1, 1, P), lambda i: (0, 0, 0))
    return _pallas_call(
        body, name="pool_bwd", grid=(1,),
        in_specs=[pl.BlockSpec((T, P), lambda i: (0, 0)), pl.BlockSpec((T, P), lambda i: (0, 0)),
                  pl.BlockSpec((1, G, gd, gd), lambda i: (l, 0, 0, 0)), vec, vec],
        out_specs=[pl.BlockSpec((T, P), lambda i: (0, 0)), pl.BlockSpec((1, G, gd, gd), lambda i: (0, 0, 0, 0)), ovec, ovec],
        out_shape=[jax.ShapeDtypeStruct((T, P), BF16), jax.ShapeDtypeStruct((1, G, gd, gd), F32),
                   jax.ShapeDtypeStruct((1, 1, P), F32), jax.ShapeDtypeStruct((1, 1, P), F32)],
        compiler_params=_cp("arbitrary"),
    )(proj, dpm, pw, pb, ps)


def _place():
    x, y, c = lax.axis_index("x"), lax.axis_index("y"), lax.axis_index("c")
    return x, y, c


def all_gather(name, shards):
    n = len(shards)

    def body(*refs):
        src, out = refs[:n], refs[n:2 * n]
        send_sems, recv_sems, local_sems = refs[2 * n:]
        x, y, c = _place()
        sibling = (x, y, 1 - c)
        chips = [(x, 1 - y), (1 - x, y), (1 - x, 1 - y)]

        def slot(a, px, py, pc):
            return out[a].at[:, 4 * px + 2 * py + pc]

        def copy(a, k, block, to, from_src=False):
            return pltpu.make_async_remote_copy(
                src_ref=src[a] if from_src else slot(a, *block), dst_ref=slot(a, *block),
                send_sem=send_sems.at[a, k], recv_sem=recv_sems.at[a, k], device_id=to, device_id_type=MESH)

        me = (x, y, c)
        mine = [pltpu.make_async_copy(src[a], slot(a, *me), local_sems.at[a]) for a in range(n)]
        first = []
        for j, chip in enumerate(chips):
            for a in range(n):
                first.append(copy(a, 1 + j, me, (*chip, c), from_src=True))
        for a in range(n):
            first.append(copy(a, 0, me, sibling, from_src=True))
        for cp in mine + first:
            cp.start()
        passed = []
        for j, chip in enumerate(chips):
            for a in range(n):
                copy(a, 1 + j, (*chip, c), me).wait_recv()
                fwd = copy(a, 4 + j, (*chip, c), sibling)
                fwd.start()
                passed.append(fwd)
        for a in range(n):
            copy(a, 0, (x, y, 1 - c), me).wait_recv()
        for j, chip in enumerate(chips):
            for a in range(n):
                copy(a, 4 + j, (*chip, 1 - c), me).wait_recv()
        for cp in first + passed:
            cp.wait_send()
        for cp in mine:
            cp.wait()

    outs = _pallas_call(
        body, name=name,
        in_specs=[ANY] * n, out_specs=[ANY] * n,
        out_shape=[jax.ShapeDtypeStruct((s.shape[0], N_DEV) + s.shape[1:], s.dtype) for s in shards],
        scratch_shapes=[pltpu.SemaphoreType.DMA((n, 7)), pltpu.SemaphoreType.DMA((n, 7)), pltpu.SemaphoreType.DMA((n,))],
        compiler_params=pltpu.CompilerParams(has_side_effects=True),
    )(*shards)
    return list(outs)


def pair_exchange(name, g16, layer=None):
    n = len(g16)
    layers = slice(None) if layer is None else pl.ds(layer, 1)

    def body(*refs):
        s16, recv = refs[:n], refs[n:2 * n]
        send_sems, recv_sems = refs[2 * n:]
        x, y, c = _place()
        sibling = (x, y, 1 - c)
        rem = []
        for a in range(n):
            for j in range(N_CHIP):
                rem.append(pltpu.make_async_remote_copy(
                    src_ref=s16[a].at[layers, 2 * j + 1 - c], dst_ref=recv[a].at[:, j],
                    send_sem=send_sems.at[a, j], recv_sem=recv_sems.at[a, j], device_id=sibling, device_id_type=MESH))
        for cp in rem:
            cp.start()
        for cp in rem:
            cp.wait_recv()
        for cp in rem:
            cp.wait_send()

    outs = _pallas_call(
        body, name=name,
        in_specs=[ANY] * n, out_specs=[ANY] * n,
        out_shape=[jax.ShapeDtypeStruct((s.shape[0] if layer is None else 1, N_CHIP) + s.shape[2:], s.dtype) for s in g16],
        scratch_shapes=[pltpu.SemaphoreType.DMA((n, N_CHIP))] * 2,
        compiler_params=pltpu.CompilerParams(has_side_effects=True),
    )(*g16)
    return list(outs)


def chip_exchange(name, pair16):
    n = len(pair16)

    def body(*refs):
        p16, recv2 = refs[:n], refs[n:2 * n]
        send_sems, recv_sems = refs[2 * n:]
        x, y, c = _place()
        rem = []
        for d in (1, 2, 3):
            px = 1 - x if d & 2 else x
            py = 1 - y if d & 1 else y
            for a in range(n):
                rem.append(pltpu.make_async_remote_copy(
                    src_ref=p16[a].at[:, 2 * px + py], dst_ref=recv2[a].at[:, d - 1],
                    send_sem=send_sems.at[a, d - 1], recv_sem=recv_sems.at[a, d - 1], device_id=(px, py, c), device_id_type=MESH))
        for cp in rem:
            cp.start()
        for cp in rem:
            cp.wait_recv()
        for cp in rem:
            cp.wait_send()

    outs = _pallas_call(
        body, name=name,
        in_specs=[ANY] * n, out_specs=[ANY] * n,
        out_shape=[jax.ShapeDtypeStruct((s.shape[0], 3) + s.shape[2:], s.dtype) for s in pair16],
        scratch_shapes=[pltpu.SemaphoreType.DMA((n, 3))] * 2,
        compiler_params=pltpu.CompilerParams(has_side_effects=True),
    )(*pair16)
    return list(outs)


HBM = pl.BlockSpec(memory_space=pltpu.HBM)
SEM = pl.BlockSpec(memory_space=pltpu.SEMAPHORE)
EFFECT = pltpu.SideEffectType.DATAFLOW_SIDE_EFFECTING


def _in_hbm(a):
    return pltpu.with_memory_space_constraint(a, pltpu.HBM)


def split_start(name, bufs, n_copies, copies_of, deps=()):
    nb = len(bufs)

    def body(*refs):
        buf = refs[:nb]
        send_sems, recv_sems = refs[nb + len(deps)], refs[nb + len(deps) + 1]
        token = refs[-1]
        for i, (src, dst, dev) in enumerate(copies_of(buf)):
            pltpu.make_async_remote_copy(src_ref=src, dst_ref=dst, send_sem=send_sems.at[i], recv_sem=recv_sems.at[i],
                                         device_id=dev, device_id_type=MESH).start()
        token[...] = jnp.zeros_like(token)

    outs = _pallas_call(
        body, name=name,
        in_specs=[HBM] * nb + [ANY] * len(deps),
        out_specs=(SEM, SEM, *([HBM] * nb), pl.BlockSpec(memory_space=pltpu.VMEM)),
        out_shape=(pltpu.SemaphoreType.DMA((n_copies,)), pltpu.SemaphoreType.DMA((n_copies,)),
                   *[pltpu.HBM(b.shape, b.dtype) for b in bufs], jax.ShapeDtypeStruct((8, 128), F32)),
        input_output_aliases={i: 2 + i for i in range(nb)},
        compiler_params=pltpu.CompilerParams(has_side_effects=EFFECT),
    )(*[_in_hbm(b) for b in bufs], *deps)
    return outs[0], outs[1], list(outs[2:2 + nb]), outs[-1]


def split_wait(name, bufs, send_sems, recv_sems, after, copies_of):
    nb = len(bufs)

    def body(*refs):
        buf = refs[:nb]
        send, recv = refs[nb], refs[nb + 1]
        for i, (src, dst, dev) in enumerate(copies_of(buf)):
            cp = pltpu.make_async_remote_copy(src_ref=src, dst_ref=dst, send_sem=send.at[i], recv_sem=recv.at[i],
                                              device_id=dev, device_id_type=MESH)
            cp.wait_send()
            cp.wait_recv()

    outs = _pallas_call(
        body, name=name,
        in_specs=[HBM] * nb + [SEM, SEM] + [ANY] * len(after),
        out_specs=[HBM] * nb,
        out_shape=[pltpu.HBM(b.shape, b.dtype) for b in bufs],
        input_output_aliases={i: i for i in range(nb)},
        compiler_params=pltpu.CompilerParams(has_side_effects=EFFECT),
    )(*bufs, send_sems, recv_sems, *after)
    return list(outs)


def place_own(w, l, place, dtype):
    _, rows, cols = w.shape
    tr = _rows_tile(rows, cols, 1 << 19)

    def body(p_ref, w_ref, o_ref):
        o_ref[0] = w_ref[0].astype(dtype)

    return _pallas_call(
        body, name="place_own",
        grid_spec=pltpu.PrefetchScalarGridSpec(
            num_scalar_prefetch=1, grid=(rows // tr,),
            in_specs=[pl.BlockSpec((1, tr, cols), lambda i, p: (l, i, 0))],
            out_specs=pl.BlockSpec((1, tr, cols), lambda i, p: (p[2], i, 0))),
        out_shape=jax.ShapeDtypeStruct((N_DEV, rows, cols), dtype), compiler_params=_cp("parallel"),
    )(place, w)


def _gather_copies(land):
    x, y, c = _place()
    k = 4 * x + 2 * y + c
    peers = [(x, 1 - y, c), (1 - x, y, c), (1 - x, 1 - y, c), (x, y, 1 - c)]
    return [(b.at[k], b.at[k], p) for p in peers for b in land]


def gather_start(name, land, deps=()):
    return split_start(name, land, 4 * len(land), _gather_copies, deps)


def gather_wait(name, land, send_sems, recv_sems, after):
    return split_wait(name, land, send_sems, recv_sems, after, _gather_copies)


def _forward_copies(land):
    x, y, c = _place()
    slots = [4 * px + 2 * py + c for px, py in [(x, 1 - y), (1 - x, y), (1 - x, 1 - y)]]
    return [(b.at[k], b.at[k], (x, y, 1 - c)) for k in slots for b in land]


def gather_forward_start(name, land, deps=()):
    return split_start(name, land, 3 * len(land), _forward_copies, deps)


def gather_forward_wait(name, land, send_sems, recv_sems, after):
    return split_wait(name, land, send_sems, recv_sems, after, _forward_copies)


def _chip_copies(nsrc):
    def copies(buf):
        p16, recv2 = buf[:nsrc], buf[nsrc:]
        x, y, c = _place()
        out = []
        for d in (1, 2, 3):
            px = 1 - x if d & 2 else x
            py = 1 - y if d & 1 else y
            out += [(p16[a].at[:, 2 * px + py], recv2[a].at[:, d - 1], (px, py, c)) for a in range(nsrc)]
        return out
    return copies


def _pair_copies(nsrc):
    def copies(buf):
        g16, recv = buf[:nsrc], buf[nsrc:]
        x, y, c = _place()
        return [(g16[a].at[:, 2 * j + 1 - c], recv[a].at[:, j], (x, y, 1 - c)) for a in range(nsrc) for j in range(N_CHIP)]
    return copies


def pair_exchange_start(name, g16, deps=()):
    n = len(g16)
    land = [lax.empty((1, N_CHIP) + s.shape[2:], s.dtype) for s in g16]
    return split_start(name, list(g16) + land, N_CHIP * n, _pair_copies(n), deps)


def pair_exchange_wait(name, bufs, send_sems, recv_sems, after):
    n = len(bufs) // 2
    return split_wait(name, bufs, send_sems, recv_sems, after, _pair_copies(n))[n:]


def chip_exchange_start(name, pair16, deps=()):
    n = len(pair16)
    land = [lax.empty((s.shape[0], 3) + s.shape[2:], s.dtype) for s in pair16]
    return split_start(name, list(pair16) + land, 3 * n, _chip_copies(n), deps)


def chip_exchange_wait(name, bufs, send_sems, recv_sems, after):
    n = len(bufs) // 2
    return split_wait(name, bufs, send_sems, recv_sems, after, _chip_copies(n))[n:]


def _rows_tile(rows, cols, budget=1 << 20):
    t = rows
    while t % 2 == 0 and t * cols > budget and (t // 2) % 16 == 0:
        t //= 2
    return t


def pair_sum(g32, recv1, place, l):
    _, _, rows, cols = recv1.shape
    tr = _rows_tile(rows, cols)

    def body(p_ref, m_ref, r_ref, o_ref):
        o_ref[...] = (m_ref[...] + r_ref[...].astype(F32)).astype(o_ref.dtype)

    blk = pl.BlockSpec((1, 1, tr, cols), lambda j, i, p: (0, j, i, 0))
    return _pallas_call(
        body, name="pair_sum",
        grid_spec=pltpu.PrefetchScalarGridSpec(
            num_scalar_prefetch=1, grid=(N_CHIP, rows // tr),
            in_specs=[pl.BlockSpec((1, 1, tr, cols), lambda j, i, p: (l, 2 * j + p[0], i, 0)), blk], out_specs=blk),
        out_shape=jax.ShapeDtypeStruct(recv1.shape, recv1.dtype), compiler_params=_cp("parallel", "parallel"),
    )(place, g32, recv1)


def _grad_in_specs(tr, cols, l):
    return ([pl.BlockSpec((1, 1, tr, cols), lambda i, p: (l, p[2], i, 0)), pl.BlockSpec((1, 1, tr, cols), lambda i, p: (0, p[1], i, 0))]
            + [pl.BlockSpec((1, 1, tr, cols), lambda i, p, d=d: (0, d, i, 0)) for d in range(3)])


def _grad_total(o32, o16, r0, r1, r2):
    return (o32[0, 0] + o16[0, 0].astype(F32)) + r0[0, 0].astype(F32) + r1[0, 0].astype(F32) + r2[0, 0].astype(F32)


def grad_sum(g32, recv1, recv2, place):
    _, _, rows, cols = recv1.shape
    tr = _rows_tile(rows, cols)

    def body(p_ref, o32, o16, r0, r1, r2, g_ref):
        g_ref[...] = _grad_total(o32, o16, r0, r1, r2)

    return _pallas_call(
        body, name="grad_sum",
        grid_spec=pltpu.PrefetchScalarGridSpec(
            num_scalar_prefetch=1, grid=(rows // tr,), in_specs=_grad_in_specs(tr, cols, 0),
            out_specs=pl.BlockSpec((tr, cols), lambda i, p: (i, 0))),
        out_shape=jax.ShapeDtypeStruct((rows, cols), F32), compiler_params=_cp("parallel"),
    )(place, g32, recv1, recv2, recv2, recv2)


def _adamw_math(w, g, m, v):
    m = ADAM_B1 * m + (1.0 - ADAM_B1) * g
    v = ADAM_B2 * v + (1.0 - ADAM_B2) * (g * g)
    m_hat = m / (1.0 - ADAM_B1 ** ADAM_STEP)
    v_hat = v / (1.0 - ADAM_B2 ** ADAM_STEP)
    delta = -ADAM_LR * (m_hat / (jnp.sqrt(v_hat) + ADAM_EPS) + ADAM_WD * w)
    return delta, m, v


def grad_sum_adamw(g32, recv1, recv2, w, m, v, place, l, prev):
    L, rows, cols = w.shape
    tr = _rows_tile(rows, cols, 1 << 18)

    def body(p_ref, o32, o16, r0, r1, r2, w_ref, m_ref, v_ref, *rest):
        g_ref, d_ref, nm_ref, nv_ref = rest[-4:]
        g = _grad_total(o32, o16, r0, r1, r2)
        d, nm, nv = _adamw_math(w_ref[0], g, m_ref[0], v_ref[0])
        g_ref[0] = g
        d_ref[0] = d
        nm_ref[0] = nm
        nv_ref[0] = nv

    blk = pl.BlockSpec((1, tr, cols), lambda i, p: (l, i, 0))
    args = [g32, recv1, recv2, recv2, recv2, w, m, v]
    in_specs = _grad_in_specs(tr, cols, 0) + [blk] * 3
    aliases = {}
    if prev is not None:
        aliases = {1 + len(args) + k: k for k in range(4)}
        args += list(prev)
        in_specs += [ANY] * 4
    return _pallas_call(
        body, name="grad_sum_adamw",
        grid_spec=pltpu.PrefetchScalarGridSpec(num_scalar_prefetch=1, grid=(rows // tr,), in_specs=in_specs, out_specs=[blk] * 4),
        out_shape=[jax.ShapeDtypeStruct((L, rows, cols), F32)] * 4, input_output_aliases=aliases,
        compiler_params=_cp("parallel"),
    )(place, *args)


def adamw(w, g, m, v):
    rows, cols = w.shape
    tr = _rows_tile(rows, cols, 1 << 18)

    def body(w_ref, g_ref, m_ref, v_ref, d_ref, nm_ref, nv_ref):
        d, nm, nv = _adamw_math(w_ref[...], g_ref[...], m_ref[...], v_ref[...])
        d_ref[...] = d
        nm_ref[...] = nm
        nv_ref[...] = nv

    blk = pl.BlockSpec((tr, cols), lambda i: (i, 0))
    return _pallas_call(body, name="adamw_small", grid=(rows // tr,), in_specs=[blk] * 4, out_specs=[blk] * 3,
                        out_shape=[jax.ShapeDtypeStruct((rows, cols), F32)] * 3, compiler_params=_cp("parallel"))(w, g, m, v)


SMALL = ("norm_ffn1", "norm_mix", "pool_w", "pool_b", "pool_scale", "conv_w", "conv_b", "lru_w_a", "lru_b_a", "lru_w_x", "lru_b_x",
         "lru_lambda", "norm_ffn2", "final_norm")
BIG = ("ffn1_w_up", "ffn1_w_down", "w_in", "w_pool_up", "w_lru_up", "w_out", "ffn2_w_up", "ffn2_w_down")
NAMES = ("norm_ffn1", "ffn1_w_up", "ffn1_w_down", "norm_mix", "w_in", "pool_w", "pool_b", "pool_scale", "w_pool_up", "conv_w", "conv_b",
         "lru_w_a", "lru_b_a", "lru_w_x", "lru_b_x", "lru_lambda", "w_lru_up", "w_out", "norm_ffn2", "ffn2_w_up", "ffn2_w_down", "final_norm")
SUBLAYERS = (("ffn1_w_up", "ffn1_w_down"), ("w_in", "w_pool_up", "w_lru_up", "w_out", "conv_w"), ("ffn2_w_up", "ffn2_w_down"))
PACK_ROWS = 16 * N_DEV


def _pack(parts):
    flat = jnp.concatenate([p.reshape(-1) for p in parts])
    unit = 128 * PACK_ROWS
    padded = -(-flat.size // unit) * unit
    return jnp.pad(flat, (0, padded - flat.size)).reshape(-1, 128)


def _unpack(packed, shapes):
    flat = packed.reshape(-1)
    out, off = [], 0
    for s in shapes:
        n = 1
        for d in s:
            n *= d
        out.append(flat[off:off + n].reshape(s))
        off += n
    return out


def kernel(x, norm_ffn1, ffn1_w_up, ffn1_w_down, norm_mix, w_in, pool_w, pool_b, pool_scale, w_pool_up, conv_w, conv_b, lru_w_a, lru_b_a, lru_w_x, lru_b_x, lru_lambda, w_lru_up, w_out, norm_ffn2, ffn2_w_up, ffn2_w_down, final_norm, loss_target, m_norm_ffn1, m_ffn1_w_up, m_ffn1_w_down, m_norm_mix, m_w_in, m_pool_w, m_pool_b, m_pool_scale, m_w_pool_up, m_conv_w, m_conv_b, m_lru_w_a, m_lru_b_a, m_lru_w_x, m_lru_b_x, m_lru_lambda, m_w_lru_up, m_w_out, m_norm_ffn2, m_ffn2_w_up, m_ffn2_w_down, m_final_norm, v_norm_ffn1, v_ffn1_w_up, v_ffn1_w_down, v_norm_mix, v_w_in, v_pool_w, v_pool_b, v_pool_scale, v_w_pool_up, v_conv_w, v_conv_b, v_lru_w_a, v_lru_b_a, v_lru_w_x, v_lru_b_x, v_lru_lambda, v_w_lru_up, v_w_out, v_norm_ffn2, v_ffn2_w_up, v_ffn2_w_down, v_final_norm):
    W = dict(norm_ffn1=norm_ffn1, ffn1_w_up=ffn1_w_up, ffn1_w_down=ffn1_w_down, norm_mix=norm_mix, w_in=w_in, pool_w=pool_w, pool_b=pool_b,
             pool_scale=pool_scale, w_pool_up=w_pool_up, conv_w=conv_w, conv_b=conv_b, lru_w_a=lru_w_a, lru_b_a=lru_b_a, lru_w_x=lru_w_x,
             lru_b_x=lru_b_x, lru_lambda=lru_lambda, w_lru_up=w_lru_up, w_out=w_out, norm_ffn2=norm_ffn2, ffn2_w_up=ffn2_w_up,
             ffn2_w_down=ffn2_w_down, final_norm=final_norm)
    M = dict(norm_ffn1=m_norm_ffn1, ffn1_w_up=m_ffn1_w_up, ffn1_w_down=m_ffn1_w_down, norm_mix=m_norm_mix, w_in=m_w_in, pool_w=m_pool_w,
             pool_b=m_pool_b, pool_scale=m_pool_scale, w_pool_up=m_w_pool_up, conv_w=m_conv_w, conv_b=m_conv_b, lru_w_a=m_lru_w_a,
             lru_b_a=m_lru_b_a, lru_w_x=m_lru_w_x, lru_b_x=m_lru_b_x, lru_lambda=m_lru_lambda, w_lru_up=m_w_lru_up, w_out=m_w_out,
             norm_ffn2=m_norm_ffn2, ffn2_w_up=m_ffn2_w_up, ffn2_w_down=m_ffn2_w_down, final_norm=m_final_norm)
    V = dict(norm_ffn1=v_norm_ffn1, ffn1_w_up=v_ffn1_w_up, ffn1_w_down=v_ffn1_w_down, norm_mix=v_norm_mix, w_in=v_w_in, pool_w=v_pool_w,
             pool_b=v_pool_b, pool_scale=v_pool_scale, w_pool_up=v_w_pool_up, conv_w=v_conv_w, conv_b=v_conv_b, lru_w_a=v_lru_w_a,
             lru_b_a=v_lru_b_a, lru_w_x=v_lru_w_x, lru_b_x=v_lru_b_x, lru_lambda=v_lru_lambda, w_lru_up=v_w_lru_up, w_out=v_w_out,
             norm_ffn2=v_norm_ffn2, ffn2_w_up=v_ffn2_w_up, ffn2_w_down=v_ffn2_w_down, final_norm=v_final_norm)

    for S in (W, M, V):
        for n in ("ffn1_w_up", "ffn2_w_up"):
            S[n] = jnp.swapaxes(S[n], 1, 2)

    T, D = x.shape[1], x.shape[2]
    L = norm_ffn1.shape[0]
    P = pool_scale.shape[1]
    R = lru_lambda.shape[1]
    H, hd = lru_w_a.shape[1], lru_w_a.shape[2]
    CW = conv_w.shape[1]
    cs = ffn1_w_up.shape[2]
    ci = w_in.shape[2]
    xin = x.reshape(T, D)
    tgt = loss_target.reshape(T, D)
    dev = 4 * lax.axis_index("x") + 2 * lax.axis_index("y") + lax.axis_index("c")
    place = jnp.stack([lax.axis_index("c"), 2 * lax.axis_index("x") + lax.axis_index("y"), dev]).astype(jnp.int32)

    cw_flat = conv_w.reshape(L, -1)
    cw_pad = (-cw_flat.shape[1]) % 1024
    cw_tiles = jnp.pad(cw_flat, ((0, 0), (0, cw_pad))).reshape(L, -1, 128)

    def units(l):
        return SUBLAYERS if l == 0 else (tuple(n for u in SUBLAYERS for n in u),)

    queued = {"gather": (), "pair": (), "chip": ()}

    gathering = []

    def gather_units_start(l):
        for k, names in enumerate(SUBLAYERS):
            land = [place_own(cw_tiles, l, place, F32) if n == "conv_w" else place_own(W[n], l, place, BF16) for n in names]
            send_sems, recv_sems, land, tok = gather_start(f"gather_start_l{l}_u{k}", land, queued["gather"])
            gathering.append(dict(names=names, tag=f"l{l}_u{k}", send=send_sems, recv=recv_sems, land=land, tok=tok, arrived=False))
            queued["gather"] = (tok,)

    def gather_unit_arrive(after):
        waiting = [u for u in gathering if not u["arrived"]]
        if not waiting:
            return ()
        unit, tokens = waiting[0], [u["tok"] for u in waiting[1:]]
        land = gather_wait(f"gather_wait_{unit['tag']}", unit["land"], unit["send"], unit["recv"], list(after) + tokens)
        send_sems, recv_sems, land, tok = gather_forward_start(f"gather_pass_start_{unit['tag']}", land)
        unit.update(land=land, send=send_sems, recv=recv_sems, tok=tok, arrived=True)
        return (tok,)

    def gather_unit_weights(after):
        if not gathering[0]["arrived"]:
            gather_unit_arrive(after)
        unit = gathering.pop(0)
        land = gather_forward_wait(f"gather_pass_wait_{unit['tag']}", unit["land"], unit["send"], unit["recv"], after)
        g = dict(zip(unit["names"], land))
        one = lambda a: a.reshape((1,) + a.shape)
        w = {}
        for tag_, up, dn in (("1", "ffn1_w_up", "ffn1_w_down"), ("2", "ffn2_w_up", "ffn2_w_down")):
            if up in g:
                w["wup" + tag_], w["wd" + tag_] = one(g[up]), g[dn].reshape(1, -1, D)
        if "w_in" in g:
            cw_l = g["conv_w"].reshape(N_DEV, -1)[:, :cw_flat.shape[1]].reshape((N_DEV,) + conv_w.shape[1:])
            w.update(win=one(g["w_in"]), wlu=g["w_lru_up"].reshape(1, R, D), wout=g["w_out"].reshape(1, D, D),
                     wpu=g["w_pool_up"].transpose(1, 0, 2).reshape(1, P, D),
                     cw=cw_l.transpose(1, 0, 2).reshape(1, CW, R))
        return w

    def layer_params(l):
        vec = lambda a: a[l:l + 1].reshape(1, 1, -1)
        return dict(g1=vec(norm_ffn1), gm=vec(norm_mix), g2=vec(norm_ffn2), pb=vec(pool_b), ps=vec(pool_scale), cb=vec(conv_b),
                    ba=vec(lru_b_a), bx=vec(lru_b_x), lam=vec(lru_lambda), pw=pool_w[l:l + 1], wa=lru_w_a[l:l + 1], wx=lru_w_x[l:l + 1])

    AHEAD = 2
    for l in range(min(AHEAD, L)):
        gather_units_start(l)
    saved, LW, LP = [], [], []
    xc = xin
    for l in range(L):
        w, p = gather_unit_weights([xc]), layer_params(l)
        if l + AHEAD < L:
            gather_units_start(l + AHEAD)
        sv = {"x1": xc}
        sv["h1"], sv["u1"], sv["s1"] = ffn_up(xc, p["g1"], w["wup1"], 0)
        xc = ffn_down(sv["s1"], w["wd1"], xc, 0, gather_unit_arrive([sv["s1"]]))
        sv["x2"] = xc
        w.update(gather_unit_weights([xc]))
        sv["h2"], sv["proj"] = mix_in(xc, p["gm"], w["win"], 0)
        sv["pm"] = pool_fwd(sv["proj"], p["pw"], p["pb"], p["ps"], 0)
        sv["hl"], sv["hs"] = lru_fwd(sv["proj"], w["cw"], p["cb"], p["wa"], p["ba"], p["wx"], p["bx"], p["lam"], P, 0)
        xc, sv["yp"], sv["yl"], sv["z"] = mix_out(sv["pm"], sv["hl"], sv["proj"], xc, w["wpu"], w["wlu"], w["wout"], P, 0,
                                                  gather_unit_arrive([sv["hl"]]))
        sv["x3"] = xc
        w.update(gather_unit_weights([xc]))
        sv["h3"], sv["u3"], sv["s3"] = ffn_up(xc, p["g2"], w["wup2"], 0)
        xc = ffn_down(sv["s3"], w["wd2"], xc, 0, gather_unit_arrive([sv["s3"]]))
        saved.append(sv)
        LW.append(w)
        LP.append(p)

    loss_part, dx, d_final = loss_head(xc, final_norm.reshape(1, D), tgt)
    loss = lax.psum(loss_part[0, 0], ("x", "y", "c"))

    G = [dict() for _ in range(L)]
    small = {n: [None] * L for n in SMALL if n != "final_norm"}

    def to_slots(name, pair):
        if name == "w_pool_up":
            return tuple(a.reshape(1, P, N_DEV, D // N_DEV).transpose(0, 2, 1, 3) for a in pair)
        return tuple(a.reshape((1, N_DEV) + W[name].shape[1:]) for a in pair)

    def ffn_bwd(dy, sv, tag, wup, wd, gn, up_name, dn_name, norm_name, l, deps=()):
        dout, du = ffn_down_bwd(dy, wd, sv["u" + tag], 0, deps)
        du = du.reshape(N_DEV, T, cs)
        G[l][dn_name] = to_slots(dn_name, dw_tn("dw_down", sv["s" + tag], lambda tk: pl.BlockSpec((1, tk, cs), lambda g, k: (g, k, 0)),
                                                dout, lambda tk: pl.BlockSpec((tk, D), lambda g, k: (k, 0)), 4, cs, D, T))
        G[l][up_name] = to_slots(up_name, dw_tn("dw_up", du, lambda tk: pl.BlockSpec((1, tk, cs), lambda g, k: (g, k, 0)),
                                                sv["h" + tag], lambda tk: pl.BlockSpec((tk, D), lambda g, k: (k, 0)), N_DEV, cs, D, T))
        dxn, dg = dx_norm_bwd("ffn_dx", du, lambda tm: pl.BlockSpec((1, tm, cs), lambda j, i: (j, i, 0)), wup, N_DEV,
                              sv["x" + tag], gn, dy, 0, w_transposed=True)
        small[norm_name][l] = dg.reshape(D)
        return dxn

    pairing, in_flight = [], []

    def reduce_start(l, names, tag):
        names = [n for n in names if n != "conv_w"]
        send_sems, recv_sems, bufs, tok = pair_exchange_start(f"rs_pair_start_{tag}", [G[l][n][1] for n in names], queued["pair"])
        pairing.append((l, names, tag, send_sems, recv_sems, bufs))
        queued["pair"] = (tok,)
        return (tok,)

    def reduce_continue(after):
        l, names, tag, send_sems, recv_sems, bufs = pairing.pop(0)
        recv1 = pair_exchange_wait(f"rs_pair_wait_{tag}", bufs, send_sems, recv_sems, after)
        pair16 = [pair_sum(G[l][n][0], r_, place, 0) for n, r_ in zip(names, recv1)]
        send_sems, recv_sems, bufs, tok = chip_exchange_start(f"rs_chip_start_{tag}", pair16, queued["chip"])
        in_flight.append((l, names, tag, send_sems, recv_sems, bufs, recv1))
        queued["chip"] = (tok,)
        return (tok,)

    def boundary(l, k, dx_now):
        deps = reduce_continue([dx_now]) if pairing else ()
        if len(units(l)) > 1:
            deps += reduce_start(l, units(l)[k], f"l{l}_u{k}")
        elif k == 0:
            deps += reduce_start(l, units(l)[0], f"l{l}_u0")
        return deps

    deps = ()
    for l in reversed(range(L)):
        sv, w, p = saved[l], LW[l], LP[l]
        dx = ffn_bwd(dx, sv, "3", w["wup2"], w["wd2"], p["g2"], "ffn2_w_up", "ffn2_w_down", "norm_ffn2", l, deps)
        deps = boundary(l, 2, dx)
        dyb, dyp, dyl, dgp, dgl, dpm, dhl = mix_out_bwd(dx, sv["proj"], sv["yp"], sv["yl"], w["wpu"], w["wlu"], w["wout"], P, R, 0, deps)
        row = lambda wd_: (lambda tk: pl.BlockSpec((tk, wd_), lambda g, k: (k, 0)))
        G[l]["w_out"] = to_slots("w_out", dw_tn("dw_out", sv["z"], row(D), dyb, row(D), 1, D, D, T))
        G[l]["w_lru_up"] = to_slots("w_lru_up", dw_tn("dw_lru_up", sv["hl"], row(R), dyl, row(D), 1, R, D, T))
        G[l]["w_pool_up"] = to_slots("w_pool_up", dw_tn("dw_pool_up", sv["pm"], row(P), dyp, row(D), 1, P, D, T))
        du_lru, du_gelu, dcw, dcb, dwa, dba, dwx, dbx, dlam = lru_bwd(
            sv["proj"], sv["hs"], dhl, w["cw"], p["cb"], p["wa"], p["ba"], p["wx"], p["bx"], p["lam"], P, 0)
        du_pool, dpw, dpb, dpsc = pool_bwd(sv["proj"], dpm, p["pw"], p["pb"], p["ps"], 0)
        dproj = jnp.concatenate([du_pool, du_lru, du_gelu, dgp, dgl], axis=1)
        G[l]["w_in"] = to_slots("w_in", dw_tn("dw_in", sv["h2"], row(D), dproj, lambda tk: pl.BlockSpec((tk, ci), lambda g, k: (k, g)),
                                              N_DEV, D, ci, T))
        dx, dgm = dx_norm_bwd("mix_dx", dproj, lambda tm: pl.BlockSpec((tm, ci), lambda j, i: (i, j)), w["win"], N_DEV,
                              sv["x2"], p["gm"], dx, 0)
        small["norm_mix"][l] = dgm.reshape(D)
        small["pool_w"][l], small["pool_b"][l], small["pool_scale"][l] = dpw[0], dpb.reshape(pool_b.shape[1:]), dpsc.reshape(P)
        small["conv_w"][l], small["conv_b"][l] = dcw[0], dcb.reshape(R)
        small["lru_w_a"][l], small["lru_b_a"][l] = dwa[0], dba.reshape(H, hd)
        small["lru_w_x"][l], small["lru_b_x"][l] = dwx[0], dbx.reshape(H, hd)
        small["lru_lambda"][l] = dlam.reshape(R)
        deps = boundary(l, 1, dx)
        dx = ffn_bwd(dx, sv, "1", w["wup1"], w["wd1"], p["g1"], "ffn1_w_up", "ffn1_w_down", "norm_ffn1", l, deps)
        deps = boundary(l, 0, dx)

    grad_x = dx.reshape(x.shape)

    small_parts = [jnp.stack(small[n]) for n in SMALL if n != "final_norm"] + [d_final.reshape(D)]
    small_shapes = [p.shape for p in small_parts]
    gpack = _pack(small_parts).reshape(1, N_DEV, -1, 128)
    recv1_s = pair_exchange("rs_pair_exchange_small", [gpack])[0]
    pair_s = pair_sum(gpack, recv1_s, place, 0)
    recv2_s = chip_exchange("rs_chip_exchange_small", [pair_s])[0]
    while pairing:
        reduce_continue([recv2_s])

    outs = {n: None for n in BIG}
    after = [dx, recv2_s]
    for l, names, tag, send_sems, recv_sems, bufs, recv1 in in_flight:
        recv2 = chip_exchange_wait(f"rs_chip_wait_{tag}", bufs, send_sems, recv_sems, after)
        for i, n in enumerate(names):
            outs[n] = grad_sum_adamw(G[l][n][0], recv1[i], recv2[i], W[n], M[n], V[n], place, l, outs[n])
        after = [outs[n][0] for n in names]
    for n in ("ffn1_w_up", "ffn2_w_up"):
        outs[n] = [jnp.swapaxes(o, 1, 2) for o in outs[n]]
    out_g, out_d, out_m, out_v = ({n: outs[n][k] for n in BIG} for k in range(4))

    gs = grad_sum(gpack, recv1_s, recv2_s, place)
    gs_all = all_gather("all_gather_small_grads", [gs.reshape((1,) + gs.shape)])[0]
    gs_all = gs_all.reshape(-1, 128)
    small_g = dict(zip(SMALL, _unpack(gs_all, small_shapes)))
    full_shapes = [W[n].shape if n != "conv_w" else small_shapes[SMALL.index("conv_w")] for n in SMALL]
    rep = [n for n in SMALL if n != "conv_w"]
    rep_shapes = [W[n].shape for n in rep]
    wp, mp, vp = (_pack([S[n] for n in rep]) for S in (W, M, V))
    gp = _pack([small_g[n] for n in rep])
    dp, nmp, nvp = adamw(wp, gp, mp, vp)
    for S, packed in ((out_d, dp), (out_m, nmp), (out_v, nvp)):
        S.update(zip(rep, _unpack(packed, rep_shapes)))
    for n in rep:
        out_g[n] = small_g[n]
    cwc = conv_w.shape[2]
    gcw = lax.dynamic_slice_in_dim(small_g["conv_w"], dev * cwc, cwc, axis=2)
    cw2 = lambda a: a.reshape(-1, cwc)
    pad_rows = (-cw2(conv_w).shape[0]) % 8
    padr = lambda a: jnp.pad(cw2(a), ((0, pad_rows), (0, 0)))
    dcw_, mcw_, vcw_ = adamw(padr(conv_w), padr(gcw), padr(M["conv_w"]), padr(V["conv_w"]))
    nrow = cw2(conv_w).shape[0]
    out_g["conv_w"] = gcw
    out_d["conv_w"], out_m["conv_w"], out_v["conv_w"] = (a[:nrow].reshape(conv_w.shape) for a in (dcw_, mcw_, vcw_))
    del full_shapes

    return (loss, grad_x, *[out_g[n] for n in NAMES], *[out_d[n] for n in NAMES], *[out_m[n] for n in NAMES], *[out_v[n] for n in NAMES])
```

```python
import functools

import jax
import jax.numpy as jnp
from jax import lax
from jax.experimental import pallas as pl
from jax.experimental.pallas import tpu as pltpu

F32, BF16 = jnp.float32, jnp.bfloat16
EPS = 1e-6
LRU_C = 8.0
POOL_WINDOWS = (2, 4, 8, 16)
ADAM_LR, ADAM_B1, ADAM_B2, ADAM_EPS, ADAM_WD, ADAM_STEP = 0.001, 0.9, 0.999, 1e-08, 0.01, 10
N_DEV = 8
N_CHIP = 4
MESH = pl.DeviceIdType.MESH
V7X_VMEM_LIMIT = 56 * 1024 * 1024
ROW_TILE = 512
WIDE_TILE = 1024
SUM_TILE = 2048
ANY = pl.BlockSpec(memory_space=pl.ANY)

_pallas_call = pl.pallas_call


def _cp(*sem):
    return pltpu.CompilerParams(dimension_semantics=sem if sem else None, vmem_limit_bytes=V7X_VMEM_LIMIT)


def _tile(n, t):
    t = min(n, t)
    assert n % t == 0, (n, t)
    return t


def _dot(a, b):
    return jnp.dot(a, b, preferred_element_type=F32)


def _dot_nt(a, b):
    return lax.dot_general(a, b, (((1,), (1,)), ((), ())), preferred_element_type=F32)


def _dot_tn(a, b):
    return lax.dot_general(a, b, (((0,), (0,)), ((), ())), preferred_element_type=F32)


def _rms(xv):
    r = lax.rsqrt(jnp.mean(xv * xv, axis=-1, keepdims=True) + EPS)
    return xv * r, r


def _rms_bwd(dh, xv, gv, dy):
    n, r = _rms(xv)
    dn = dh * gv
    dx = dy + r * (dn - n * jnp.mean(dn * n, axis=-1, keepdims=True))
    return dx, jnp.sum(dh * n, axis=0, keepdims=True)


def _shift_down(x, k, fill=0.0):
    if k == 0:
        return x
    rows = lax.broadcasted_iota(jnp.int32, x.shape, 0)
    return jnp.where(rows >= k, pltpu.roll(x, k, 0), fill)


def _shift_up(x, k, fill=0.0):
    if k == 0:
        return x
    n = x.shape[0]
    rows = lax.broadcasted_iota(jnp.int32, x.shape, 0)
    return jnp.where(rows < n - k, pltpu.roll(x, n - k, 0), fill)


def _sigmoid(x):
    return 0.5 * jnp.tanh(0.5 * x) + 0.5


_GELU_K = 0.7978845608028654
_GELU_C = 0.044715


def _gelu(x):
    th = jnp.tanh(_GELU_K * (x + _GELU_C * x * x * x))
    return 0.5 * x * (1.0 + th), th


def _gelu_grad(x, th):
    return 0.5 * (1.0 + th) + 0.5 * x * (1.0 - th * th) * _GELU_K * (1.0 + 3.0 * _GELU_C * x * x)


def ffn_up(x, g, wup, l):
    T, D = x.shape
    cs = wup.shape[-2]
    tm = _tile(T, WIDE_TILE)
    ni = T // tm

    def body(x_ref, g_ref, wa_ref, wb_ref, h_ref, u_ref, s_ref, hs_ref):
        rows = pl.ds(pl.multiple_of(pl.program_id(1) * tm, tm), tm)

        @pl.when(pl.program_id(0) == 0)
        def _():
            n, _r = _rms(x_ref[...])
            hv = (n * g_ref[0]).astype(BF16)
            hs_ref[rows, :] = hv
            h_ref[...] = hv

        hv = hs_ref[rows, :]
        a = _dot_nt(hv, wa_ref[0, 0])
        b = _dot_nt(hv, wb_ref[0, 0])
        u_ref[0, 0] = a.astype(BF16)
        u_ref[1, 0] = b.astype(BF16)
        s_ref[0] = (a * _sigmoid(a) * b).astype(BF16)

    first = lambda j, i: (jnp.where(j == 0, i, ni - 1), 0)
    return _pallas_call(
        body, name="ffn_up", grid=(4, ni),
        in_specs=[pl.BlockSpec((tm, D), first), pl.BlockSpec((1, 1, D), lambda j, i: (l, 0, 0)),
                  pl.BlockSpec((1, 1, cs, D), lambda j, i: (0, j, 0, 0)), pl.BlockSpec((1, 1, cs, D), lambda j, i: (0, j + 4, 0, 0))],
        out_specs=[pl.BlockSpec((tm, D), first), pl.BlockSpec((2, 1, tm, cs), lambda j, i: (0, j, i, 0)),
                   pl.BlockSpec((1, tm, cs), lambda j, i: (j, i, 0))],
        out_shape=[jax.ShapeDtypeStruct((T, D), BF16), jax.ShapeDtypeStruct((2, 4, T, cs), BF16), jax.ShapeDtypeStruct((4, T, cs), BF16)],
        scratch_shapes=[pltpu.VMEM((T, D), BF16)],
        compiler_params=_cp("arbitrary", "arbitrary"),
    )(x, g, wup, wup)


def ffn_down(s, wd, x, l, deps=()):
    _, T, cs = s.shape
    D = x.shape[1]
    tm = _tile(T, WIDE_TILE)

    def body(s_ref, w_ref, x_ref, *rest):
        o_ref, acc_ref = rest[len(deps):]
        j = pl.program_id(1)

        @pl.when(j == 0)
        def _():
            acc_ref[...] = jnp.zeros_like(acc_ref)

        acc_ref[...] += _dot(s_ref[0], w_ref[0])

        @pl.when(j == 3)
        def _():
            o_ref[...] = x_ref[...] + 0.5 * acc_ref[...]

    return _pallas_call(
        body, name="ffn_down", grid=(T // tm, 4),
        in_specs=[pl.BlockSpec((1, tm, cs), lambda i, j: (j, i, 0)), pl.BlockSpec((1, cs, D), lambda i, j: (0, j, 0)),
                  pl.BlockSpec((tm, D), lambda i, j: (i, 0))] + [ANY] * len(deps),
        out_specs=pl.BlockSpec((tm, D), lambda i, j: (i, 0)),
        out_shape=jax.ShapeDtypeStruct((T, D), F32),
        scratch_shapes=[pltpu.VMEM((tm, D), F32)],
        compiler_params=_cp("parallel", "arbitrary"),
    )(s, wd, x, *deps)


def mix_in(x, g, win, l):
    T, D = x.shape
    ci = win.shape[-1]
    tm = _tile(T, WIDE_TILE)
    ni = T // tm

    def body(x_ref, g_ref, w_ref, h_ref, p_ref, hs_ref):
        rows = pl.ds(pl.multiple_of(pl.program_id(1) * tm, tm), tm)

        @pl.when(pl.program_id(0) == 0)
        def _():
            n, _r = _rms(x_ref[...])
            hv = (n * g_ref[0]).astype(BF16)
            hs_ref[rows, :] = hv
            h_ref[...] = hv

        p_ref[...] = _dot(hs_ref[rows, :], w_ref[0, 0]).astype(BF16)

    first = lambda j, i: (jnp.where(j == 0, i, ni - 1), 0)
    return _pallas_call(
        body, name="mix_in", grid=(N_DEV, ni),
        in_specs=[pl.BlockSpec((tm, D), first), pl.BlockSpec((1, 1, D), lambda j, i: (l, 0, 0)),
                  pl.BlockSpec((1, 1, D, ci), lambda j, i: (0, j, 0, 0))],
        out_specs=[pl.BlockSpec((tm, D), first), pl.BlockSpec((tm, ci), lambda j, i: (i, j))],
        out_shape=[jax.ShapeDtypeStruct((T, D), BF16), jax.ShapeDtypeStruct((T, N_DEV * ci), BF16)],
        scratch_shapes=[pltpu.VMEM((T, D), BF16)],
        compiler_params=_cp("arbitrary", "arbitrary"),
    )(x, g, win)


def _inv_count(T, w):
    t = lax.broadcasted_iota(jnp.int32, (T, 1), 0)
    return 1.0 / jnp.minimum(t + 1, w).astype(F32)


def _pooled(ug, w, inv):
    s = ug
    k = 1
    while k < w:
        s = s + _shift_down(s, k)
        k *= 2
    return s * inv - ug


def pool_fwd(proj, pw, pb, ps, l):
    T = proj.shape[0]
    _, G, gd, _ = pw.shape
    P = G * gd

    def body(u_ref, w_ref, b_ref, s_ref, o_ref):
        for gi in range(G):
            cols = slice(gi * gd, (gi + 1) * gd)
            ug = u_ref[:, cols].astype(F32)
            pooled = _pooled(ug, POOL_WINDOWS[gi], _inv_count(T, POOL_WINDOWS[gi]))
            mixed = _dot(pooled.astype(BF16), w_ref[0, gi].astype(BF16)) + b_ref[0, :, cols]
            o_ref[:, cols] = (mixed * s_ref[0, :, cols]).astype(BF16)

    return _pallas_call(
        body, name="pool_fwd", grid=(1,),
        in_specs=[pl.BlockSpec((T, P), lambda i: (0, 0)), pl.BlockSpec((1, G, gd, gd), lambda i: (l, 0, 0, 0)),
                  pl.BlockSpec((1, 1, P), lambda i: (l, 0, 0)), pl.BlockSpec((1, 1, P), lambda i: (l, 0, 0))],
        out_specs=pl.BlockSpec((T, P), lambda i: (0, 0)),
        out_shape=jax.ShapeDtypeStruct((T, P), BF16),
        compiler_params=_cp("arbitrary"),
    )(proj, pw, pb, ps)


def _conv(u, cw_ref, cb):
    CW = cw_ref.shape[1]
    v = cb
    for k in range(CW):
        v = v + cw_ref[0, k:k + 1, :] * _shift_down(u, CW - 1 - k)
    return v


def _softplus(z):
    return jnp.maximum(z, 0.0) + jnp.log1p(jnp.exp(-jnp.abs(z)))


def _lru_gates(v, wa_ref, ba, wx_ref, bx, lam):
    vb = v.astype(BF16)
    r = _sigmoid(_dot(vb, wa_ref[0, 0].astype(BF16)) + ba)
    i = _sigmoid(_dot(vb, wx_ref[0, 0].astype(BF16)) + bx)
    sp = _softplus(-lam)
    log_a = -LRU_C * r * sp
    a = jnp.exp(log_a)
    m2 = -jnp.tanh(log_a) * (a * a + 1.0)
    inv_mult = lax.rsqrt(m2)
    mult = jnp.where(m2 > 0.0, m2 * inv_mult, 0.0)
    return r, i, sp, a, mult, inv_mult


def _scan_fwd(a_ref, b_ref, o_ref):
    T, W = a_ref.shape
    rows = lax.broadcasted_iota(jnp.int32, (8, W), 0)

    def step(t, carry):
        r0 = pl.multiple_of(t * 8, 8)
        A = a_ref[pl.ds(r0, 8), :]
        B = b_ref[pl.ds(r0, 8), :]
        for s in (1, 2, 4):
            keep = rows >= s
            As = jnp.where(keep, pltpu.roll(A, s, 0), 1.0)
            Bs = jnp.where(keep, pltpu.roll(B, s, 0), 0.0)
            B = A * Bs + B
            A = A * As
        h = B + A * carry
        o_ref[pl.ds(r0, 8), :] = h
        return jnp.broadcast_to(h[7:8, :], (8, W))

    lax.fori_loop(0, T // 8, step, jnp.zeros((8, W), F32), unroll=8)


def _scan_bwd(a_ref, b_ref, o_ref):
    T, W = a_ref.shape
    rows = lax.broadcasted_iota(jnp.int32, (8, W), 0)
    nt = T // 8

    def step(t, carry):
        r0 = pl.multiple_of((nt - 1 - t) * 8, 8)
        A = a_ref[pl.ds(r0, 8), :]
        B = b_ref[pl.ds(r0, 8), :]
        for s in (1, 2, 4):
            keep = rows < 8 - s
            As = jnp.where(keep, pltpu.roll(A, 8 - s, 0), 1.0)
            Bs = jnp.where(keep, pltpu.roll(B, 8 - s, 0), 0.0)
            B = A * Bs + B
            A = A * As
        y = B + A * carry
        o_ref[pl.ds(r0, 8), :] = y
        return jnp.broadcast_to(y[0:1, :], (8, W))

    lax.fori_loop(0, nt, step, jnp.zeros((8, W), F32), unroll=8)


def _lru_specs(T, hd, P, R, CW, l):
    ob, gb = P // hd, (P + R) // hd
    vec = pl.BlockSpec((1, 1, hd), lambda h: (l, 0, h))
    mat = pl.BlockSpec((1, 1, hd, hd), lambda h: (l, h, 0, 0))
    return [pl.BlockSpec((T, hd), lambda h: (0, ob + h)), pl.BlockSpec((T, hd), lambda h: (0, gb + h)),
            pl.BlockSpec((1, CW, hd), lambda h: (0, 0, h)), vec, mat, vec, mat, vec, vec]


def lru_fwd(proj, cw, cb, wa, ba, wx, bx, lam, P, l):
    T = proj.shape[0]
    _, H, hd, _ = wa.shape
    R = H * hd
    CW = cw.shape[1]
    assert P % hd == 0 and T % 8 == 0

    def body(u_ref, ug_ref, cw_ref, cb_ref, wa_ref, ba_ref, wx_ref, bx_ref, lam_ref, hl_ref, hs_ref, a_s, b_s):
        v = _conv(u_ref[...].astype(F32), cw_ref, cb_ref[0])
        _r, i, _sp, a, mult, _im = _lru_gates(v, wa_ref, ba_ref[0], wx_ref, bx_ref[0], lam_ref[0])
        a_s[...] = a
        b_s[...] = mult * (i * v)
        _scan_fwd(a_s, b_s, hs_ref)
        ge, _th = _gelu(ug_ref[...].astype(F32))
        hl_ref[...] = (hs_ref[...] * ge).astype(BF16)

    out = pl.BlockSpec((T, hd), lambda h: (0, h))
    return _pallas_call(
        body, name="lru_fwd", grid=(H,),
        in_specs=_lru_specs(T, hd, P, R, CW, l),
        out_specs=[out, out],
        out_shape=[jax.ShapeDtypeStruct((T, R), BF16), jax.ShapeDtypeStruct((T, R), F32)],
        scratch_shapes=[pltpu.VMEM((T, hd), F32)] * 2,
        compiler_params=_cp("parallel"),
    )(proj, proj, cw, cb, wa, ba, wx, bx, lam)


def mix_out(pm, hl, proj, x, wpu, wlu, wout, P, l, deps=()):
    T, D = x.shape
    R = hl.shape[1]
    tm = _tile(T, ROW_TILE)
    assert (P + 2 * R) % D == 0
    gb = (P + 2 * R) // D

    def body(pm_ref, hl_ref, gp_ref, gl_ref, x_ref, wpu_ref, wlu_ref, wo_ref, *rest):
        o_ref, yp_ref, yl_ref, z_ref = rest[len(deps):]
        yp = _dot(pm_ref[...], wpu_ref[0])
        yl = _dot(hl_ref[...], wlu_ref[0])
        z = (_sigmoid(gp_ref[...].astype(F32)) * yp + _sigmoid(gl_ref[...].astype(F32)) * yl).astype(BF16)
        yp_ref[...] = yp.astype(BF16)
        yl_ref[...] = yl.astype(BF16)
        z_ref[...] = z
        o_ref[...] = x_ref[...] + _dot(z, wo_ref[0])

    row = lambda w: pl.BlockSpec((tm, w), lambda i: (i, 0))
    return _pallas_call(
        body, name="mix_out", grid=(T // tm,),
        in_specs=[row(P), row(R), pl.BlockSpec((tm, D), lambda i: (i, gb)), pl.BlockSpec((tm, D), lambda i: (i, gb + 1)), row(D),
                  pl.BlockSpec((1, P, D), lambda i: (0, 0, 0)), pl.BlockSpec((1, R, D), lambda i: (0, 0, 0)),
                  pl.BlockSpec((1, D, D), lambda i: (0, 0, 0))] + [ANY] * len(deps),
        out_specs=[row(D)] * 4,
        out_shape=[jax.ShapeDtypeStruct((T, D), F32)] + [jax.ShapeDtypeStruct((T, D), BF16)] * 3,
        compiler_params=_cp("parallel"),
    )(pm, hl, proj, proj, x, wpu, wlu, wout, *deps)


def loss_head(x, gf, tgt):
    T, D = x.shape
    tm = _tile(T, ROW_TILE)

    def body(x_ref, g_ref, t_ref, loss_ref, dx_ref, dg_ref):
        @pl.when(pl.program_id(0) == 0)
        def _():
            loss_ref[...] = jnp.zeros_like(loss_ref)
            dg_ref[...] = jnp.zeros_like(dg_ref)

        xv = x_ref[...]
        gv = g_ref[...]
        n, _r = _rms(xv)
        e = n * gv - t_ref[...]
        loss_ref[...] += 0.5 * jnp.sum(jnp.sum(e * e, axis=-1, keepdims=True), axis=0, keepdims=True) / D
        dx, dg = _rms_bwd(e * (1.0 / D), xv, gv, 0.0)
        dx_ref[...] = dx
        dg_ref[...] += dg

    return _pallas_call(
        body, name="loss_head", grid=(T // tm,),
        in_specs=[pl.BlockSpec((tm, D), lambda i: (i, 0)), pl.BlockSpec((1, D), lambda i: (0, 0)), pl.BlockSpec((tm, D), lambda i: (i, 0))],
        out_specs=[pl.BlockSpec((1, 1), lambda i: (0, 0)), pl.BlockSpec((tm, D), lambda i: (i, 0)), pl.BlockSpec((1, D), lambda i: (0, 0))],
        out_shape=[jax.ShapeDtypeStruct((1, 1), F32), jax.ShapeDtypeStruct((T, D), F32), jax.ShapeDtypeStruct((1, D), F32)],
        compiler_params=_cp("arbitrary"),
    )(x, gf, tgt)


def ffn_down_bwd(dy, wd, u, l, deps=()):
    T, D = dy.shape
    cs = u.shape[-1]
    tm = _tile(T, WIDE_TILE)
    ni = T // tm

    def body(dy_ref, w_ref, u_ref, *rest):
        do_ref, du_ref, dyb_ref = rest[len(deps):]
        rows = pl.ds(pl.multiple_of(pl.program_id(1) * tm, tm), tm)

        @pl.when(pl.program_id(0) == 0)
        def _():
            d = (0.5 * dy_ref[...]).astype(BF16)
            dyb_ref[rows, :] = d
            do_ref[...] = d

        ds = _dot_nt(dyb_ref[rows, :], w_ref[0])
        a = u_ref[0, 0].astype(F32)
        b = u_ref[1, 0].astype(F32)
        sg = _sigmoid(a)
        du_ref[0, 0] = (ds * b * (sg * (1.0 + a * (1.0 - sg)))).astype(BF16)
        du_ref[1, 0] = (ds * (a * sg)).astype(BF16)

    first = lambda j, i: (jnp.where(j == 0, i, ni - 1), 0)
    blk = pl.BlockSpec((2, 1, tm, cs), lambda j, i: (0, j, i, 0))
    return _pallas_call(
        body, name="ffn_down_bwd", grid=(4, ni),
        in_specs=[pl.BlockSpec((tm, D), first), pl.BlockSpec((1, cs, D), lambda j, i: (0, j, 0)), blk] + [ANY] * len(deps),
        out_specs=[pl.BlockSpec((tm, D), first), blk],
        out_shape=[jax.ShapeDtypeStruct((T, D), BF16), jax.ShapeDtypeStruct((2, 4, T, cs), BF16)],
        scratch_shapes=[pltpu.VMEM((T, D), BF16)],
        compiler_params=_cp("arbitrary", "arbitrary"),
    )(dy, wd, u, *deps)


def dw_tn(name, a, a_spec, b, b_spec, G, M, N, T):
    tk = _tile(T, SUM_TILE)
    nk = T // tk

    def body(a_ref, b_ref, o32_ref, o16_ref, acc_ref):
        k = pl.program_id(1)

        @pl.when(k == 0)
        def _():
            acc_ref[...] = jnp.zeros_like(acc_ref)

        av = a_ref[0] if len(a_ref.shape) == 3 else a_ref[...]
        bv = b_ref[0] if len(b_ref.shape) == 3 else b_ref[...]
        acc_ref[...] += _dot_tn(av, bv)

        @pl.when(k == nk - 1)
        def _():
            o32_ref[0, 0] = acc_ref[...]
            o16_ref[0, 0] = acc_ref[...].astype(BF16)

    out = pl.BlockSpec((1, 1, M, N), lambda g, k: (0, g, 0, 0))
    return _pallas_call(
        body, name=name, grid=(G, nk),
        in_specs=[a_spec(tk), b_spec(tk)], out_specs=[out, out],
        out_shape=[jax.ShapeDtypeStruct((1, G, M, N), F32), jax.ShapeDtypeStruct((1, G, M, N), BF16)],
        scratch_shapes=[pltpu.VMEM((M, N), F32)],
        compiler_params=_cp("parallel", "arbitrary"),
    )(a, b)


def dx_norm_bwd(name, dact, d_spec, w, G, x, g, dy, l, w_transposed=False):
    T, D = x.shape
    wblk = w.shape[-2:]
    tm = _tile(T, WIDE_TILE)
    ni = T // tm
    ch = _tile(tm, ROW_TILE // 2)

    def body(d_ref, w_ref, x_ref, g_ref, dy_ref, dx_ref, dg_ref, acc_ref):
        j, i = pl.program_id(0), pl.program_id(1)
        rows = pl.ds(pl.multiple_of(i * tm, tm), tm)

        @pl.when(jnp.logical_and(i == 0, j == 0))
        def _():
            dg_ref[...] = jnp.zeros_like(dg_ref)

        @pl.when(j == 0)
        def _():
            acc_ref[rows, :] = jnp.zeros((tm, D), F32)

        dv = d_ref[0] if len(d_ref.shape) == 3 else d_ref[...]
        acc_ref[rows, :] += _dot(dv, w_ref[0, 0]) if w_transposed else _dot_nt(dv, w_ref[0, 0])

        @pl.when(j == G - 1)
        def _():
            dg = jnp.zeros((1, D), F32)
            for c0 in range(0, tm, ch):
                part_rows = pl.ds(pl.multiple_of(i * tm + c0, ch), ch)
                dx, dgc = _rms_bwd(acc_ref[part_rows, :], x_ref[c0:c0 + ch, :], g_ref[0], dy_ref[c0:c0 + ch, :])
                dx_ref[c0:c0 + ch, :] = dx
                dg = dg + dgc
            dg_ref[...] += dg

    last = pl.BlockSpec((tm, D), lambda j, i: (jnp.where(j == G - 1, i, 0), 0))
    return _pallas_call(
        body, name=name, grid=(G, ni),
        in_specs=[d_spec(tm), pl.BlockSpec((1, 1) + wblk, lambda j, i: (0, j, 0, 0)), last, pl.BlockSpec((1, 1, D), lambda j, i: (l, 0, 0)), last],
        out_specs=[last, pl.BlockSpec((1, D), lambda j, i: (0, 0))],
        out_shape=[jax.ShapeDtypeStruct((T, D), F32), jax.ShapeDtypeStruct((1, D), F32)],
        scratch_shapes=[pltpu.VMEM((T, D), F32)],
        compiler_params=_cp("arbitrary", "arbitrary"),
    )(dact, w, x, g, dy)


def mix_out_bwd(dy, proj, yp, yl, wpu, wlu, wout, P, R, l, deps=()):
    T, D = dy.shape
    tm = _tile(T, ROW_TILE)
    gb = (P + 2 * R) // D

    def body(dy_ref, gp_ref, gl_ref, yp_ref, yl_ref, wpu_ref, wlu_ref, wo_ref, *rest):
        dyb_ref, dyp_ref, dyl_ref, dgp_ref, dgl_ref, dpm_ref, dhl_ref = rest[len(deps):]
        dyb = dy_ref[...].astype(BF16)
        dyb_ref[...] = dyb
        dz = _dot_nt(dyb, wo_ref[0])
        sp = _sigmoid(gp_ref[...].astype(F32))
        sl = _sigmoid(gl_ref[...].astype(F32))
        dgp_ref[...] = (dz * yp_ref[...].astype(F32) * sp * (1.0 - sp)).astype(BF16)
        dgl_ref[...] = (dz * yl_ref[...].astype(F32) * sl * (1.0 - sl)).astype(BF16)
        dyp = (dz * sp).astype(BF16)
        dyl = (dz * sl).astype(BF16)
        dyp_ref[...] = dyp
        dyl_ref[...] = dyl
        dpm_ref[...] = _dot_nt(dyp, wpu_ref[0]).astype(BF16)
        dhl_ref[...] = _dot_nt(dyl, wlu_ref[0]).astype(BF16)

    row = lambda w: pl.BlockSpec((tm, w), lambda i: (i, 0))
    return _pallas_call(
        body, name="mix_out_bwd", grid=(T // tm,),
        in_specs=[row(D), pl.BlockSpec((tm, D), lambda i: (i, gb)), pl.BlockSpec((tm, D), lambda i: (i, gb + 1)), row(D), row(D),
                  pl.BlockSpec((1, P, D), lambda i: (0, 0, 0)), pl.BlockSpec((1, R, D), lambda i: (0, 0, 0)),
                  pl.BlockSpec((1, D, D), lambda i: (0, 0, 0))] + [ANY] * len(deps),
        out_specs=[row(D)] * 5 + [row(P), row(R)],
        out_shape=[jax.ShapeDtypeStruct((T, D), BF16)] * 5 + [jax.ShapeDtypeStruct((T, P), BF16), jax.ShapeDtypeStruct((T, R), BF16)],
        compiler_params=_cp("parallel"),
    )(dy, proj, proj, yp, yl, wpu, wlu, wout, *deps)


def lru_bwd(proj, hs, dhl, cw, cb, wa, ba, wx, bx, lam, P, l):
    T = proj.shape[0]
    _, H, hd, _ = wa.shape
    R = H * hd
    CW = cw.shape[1]

    def body(u_ref, ug_ref, cw_ref, cb_ref, wa_ref, ba_ref, wx_ref, bx_ref, lam_ref, hs_ref, dhl_ref,
             du_ref, dug_ref, dcw_ref, dcb_ref, dwa_ref, dba_ref, dwx_ref, dbx_ref, dlam_ref, c_s, g_s, y_s):
        u = u_ref[...].astype(F32)
        v = _conv(u, cw_ref, cb_ref[0])
        lam = lam_ref[0]
        r, i, sp, a, mult, inv_mult = _lru_gates(v, wa_ref, ba_ref[0], wx_ref, bx_ref[0], lam)
        ug = ug_ref[...].astype(F32)
        ge, th = _gelu(ug)
        hs = hs_ref[...]
        dhl = dhl_ref[...].astype(F32)
        dug_ref[...] = (dhl * hs * _gelu_grad(ug, th)).astype(BF16)
        c_s[...] = _shift_up(a, 1)
        g_s[...] = dhl * ge
        _scan_bwd(c_s, g_s, y_s)
        y = y_s[...]
        da = y * _shift_down(hs, 1)
        iv = i * v
        dlog_a = da * a - (y * iv) * (a * a) * inv_mult
        div = y * mult
        dpa = (dlog_a * (-LRU_C) * sp) * r * (1.0 - r)
        dpx = (div * v) * i * (1.0 - i)
        dsp = jnp.sum(dlog_a * (-LRU_C) * r, axis=0, keepdims=True)
        dlam_ref[0] = -dsp * _sigmoid(-lam)
        vb = v.astype(BF16)
        dpab, dpxb = dpa.astype(BF16), dpx.astype(BF16)
        dwa_ref[0, 0] = _dot_tn(vb, dpab)
        dwx_ref[0, 0] = _dot_tn(vb, dpxb)
        dba_ref[0] = jnp.sum(dpa, axis=0, keepdims=True)
        dbx_ref[0] = jnp.sum(dpx, axis=0, keepdims=True)
        dv = div * i + _dot_nt(dpab, wa_ref[0, 0].astype(BF16)) + _dot_nt(dpxb, wx_ref[0, 0].astype(BF16))
        dcb_ref[0] = jnp.sum(dv, axis=0, keepdims=True)
        du = jnp.zeros_like(dv)
        for k in range(CW):
            du = du + cw_ref[0, k:k + 1, :] * _shift_up(dv, CW - 1 - k)
            dcw_ref[0, k:k + 1, :] = jnp.sum(dv * _shift_down(u, CW - 1 - k), axis=0, keepdims=True)
        du_ref[...] = du.astype(BF16)

    col = pl.BlockSpec((T, hd), lambda h: (0, h))
    vec = pl.BlockSpec((1, 1, hd), lambda h: (0, 0, h))
    mat = pl.BlockSpec((1, 1, hd, hd), lambda h: (0, h, 0, 0))
    vshape = jax.ShapeDtypeStruct((1, 1, R), F32)
    mshape = jax.ShapeDtypeStruct((1, H, hd, hd), F32)
    return _pallas_call(
        body, name="lru_bwd", grid=(H,),
        in_specs=_lru_specs(T, hd, P, R, CW, l) + [col, col],
        out_specs=[col, col, pl.BlockSpec((1, CW, hd), lambda h: (0, 0, h)), vec, mat, vec, mat, vec, vec],
        out_shape=[jax.ShapeDtypeStruct((T, R), BF16)] * 2 + [jax.ShapeDtypeStruct((1, CW, R), F32), vshape, mshape, vshape, mshape, vshape, vshape],
        scratch_shapes=[pltpu.VMEM((T, hd), F32)] * 3,
        compiler_params=_cp("parallel"),
    )(proj, proj, cw, cb, wa, ba, wx, bx, lam, hs, dhl)


def pool_bwd(proj, dpm, pw, pb, ps, l):
    T = proj.shape[0]
    _, G, gd, _ = pw.shape
    P = G * gd

    def body(u_ref, d_ref, w_ref, b_ref, s_ref, du_ref, dw_ref, db_ref, dsc_ref):
        for gi in range(G):
            cols = slice(gi * gd, (gi + 1) * gd)
            w = POOL_WINDOWS[gi]
            inv = _inv_count(T, w)
            ug = u_ref[:, cols].astype(F32)
            pooled = _pooled(ug, w, inv).astype(BF16)
            wb = w_ref[0, gi].astype(BF16)
            mixed = _dot(pooled, wb) + b_ref[0, :, cols]
            dpm_g = d_ref[:, cols].astype(F32)
            dsc_ref[0, :, cols] = jnp.sum(dpm_g * mixed, axis=0, keepdims=True)
            dmixed = dpm_g * s_ref[0, :, cols]
            db_ref[0, :, cols] = jnp.sum(dmixed, axis=0, keepdims=True)
            dmb = dmixed.astype(BF16)
            dw_ref[0, gi] = _dot_tn(pooled, dmb)
            dpooled = _dot_nt(dmb, wb)
            s = dpooled * inv
            k = 1
            while k < w:
                s = s + _shift_up(s, k)
                k *= 2
            du_ref[:, cols] = (s - dpooled).astype(BF16)

    vec = pl.BlockSpec((1, 1, P), lambda i: (l, 0, 0))
    ovec = pl.BlockSpec((1, 1, P), lambda i: (0, 0, 0))
    return _pallas_call(
        body, name="pool_bwd", grid=(1,),
        in_specs=[pl.BlockSpec((T, P), lambda i: (0, 0)), pl.BlockSpec((T, P), lambda i: (0, 0)),
                  pl.BlockSpec((1, G, gd, gd), lambda i: (l, 0, 0, 0)), vec, vec],
        out_specs=[pl.BlockSpec((T, P), lambda i: (0, 0)), pl.BlockSpec((1, G, gd, gd), lambda i: (0, 0, 0, 0)), ovec, ovec],
        out_shape=[jax.ShapeDtypeStruct((T, P), BF16), jax.ShapeDtypeStruct((1, G, gd, gd), F32),
                   jax.ShapeDtypeStruct((1, 1, P), F32), jax.ShapeDtypeStruct((1, 1, P), F32)],
        compiler_params=_cp("arbitrary"),
    )(proj, dpm, pw, pb, ps)


def _place():
    x, y, c = lax.axis_index("x"), lax.axis_index("y"), lax.axis_index("c")
    return x, y, c


def all_gather(name, shards):
    n = len(shards)

    def body(*refs):
        src, out = refs[:n], refs[n:2 * n]
        send_sems, recv_sems, local_sems = refs[2 * n:]
        x, y, c = _place()
        sibling = (x, y, 1 - c)
        chips = [(x, 1 - y), (1 - x, y), (1 - x, 1 - y)]

        def slot(a, px, py, pc):
            return out[a].at[:, 4 * px + 2 * py + pc]

        def copy(a, k, block, to, from_src=False):
            return pltpu.make_async_remote_copy(
                src_ref=src[a] if from_src else slot(a, *block), dst_ref=slot(a, *block),
                send_sem=send_sems.at[a, k], recv_sem=recv_sems.at[a, k], device_id=to, device_id_type=MESH)

        me = (x, y, c)
        mine = [pltpu.make_async_copy(src[a], slot(a, *me), local_sems.at[a]) for a in range(n)]
        first = []
        for j, chip in enumerate(chips):
            for a in range(n):
                first.append(copy(a, 1 + j, me, (*chip, c), from_src=True))
        for a in range(n):
            first.append(copy(a, 0, me, sibling, from_src=True))
        for cp in mine + first:
            cp.start()
        passed = []
        for j, chip in enumerate(chips):
            for a in range(n):
                copy(a, 1 + j, (*chip, c), me).wait_recv()
                fwd = copy(a, 4 + j, (*chip, c), sibling)
                fwd.start()
                passed.append(fwd)
        for a in range(n):
            copy(a, 0, (x, y, 1 - c), me).wait_recv()
        for j, chip in enumerate(chips):
            for a in range(n):
                copy(a, 4 + j, (*chip, 1 - c), me).wait_recv()
        for cp in first + passed:
            cp.wait_send()
        for cp in mine:
            cp.wait()

    outs = _pallas_call(
        body, name=name,
        in_specs=[ANY] * n, out_specs=[ANY] * n,
        out_shape=[jax.ShapeDtypeStruct((s.shape[0], N_DEV) + s.shape[1:], s.dtype) for s in shards],
        scratch_shapes=[pltpu.SemaphoreType.DMA((n, 7)), pltpu.SemaphoreType.DMA((n, 7)), pltpu.SemaphoreType.DMA((n,))],
        compiler_params=pltpu.CompilerParams(has_side_effects=True),
    )(*shards)
    return list(outs)


def pair_exchange(name, g16, layer=None):
    n = len(g16)
    layers = slice(None) if layer is None else pl.ds(layer, 1)

    def body(*refs):
        s16, recv = refs[:n], refs[n:2 * n]
        send_sems, recv_sems = refs[2 * n:]
        x, y, c = _place()
        sibling = (x, y, 1 - c)
        rem = []
        for a in range(n):
            for j in range(N_CHIP):
                rem.append(pltpu.make_async_remote_copy(
                    src_ref=s16[a].at[layers, 2 * j + 1 - c], dst_ref=recv[a].at[:, j],
                    send_sem=send_sems.at[a, j], recv_sem=recv_sems.at[a, j], device_id=sibling, device_id_type=MESH))
        for cp in rem:
            cp.start()
        for cp in rem:
            cp.wait_recv()
        for cp in rem:
            cp.wait_send()

    outs = _pallas_call(
        body, name=name,
        in_specs=[ANY] * n, out_specs=[ANY] * n,
        out_shape=[jax.ShapeDtypeStruct((s.shape[0] if layer is None else 1, N_CHIP) + s.shape[2:], s.dtype) for s in g16],
        scratch_shapes=[pltpu.SemaphoreType.DMA((n, N_CHIP))] * 2,
        compiler_params=pltpu.CompilerParams(has_side_effects=True),
    )(*g16)
    return list(outs)


def chip_exchange(name, pair16):
    n = len(pair16)

    def body(*refs):
        p16, recv2 = refs[:n], refs[n:2 * n]
        send_sems, recv_sems = refs[2 * n:]
        x, y, c = _place()
        rem = []
        for d in (1, 2, 3):
            px = 1 - x if d & 2 else x
            py = 1 - y if d & 1 else y
            for a in range(n):
                rem.append(pltpu.make_async_remote_copy(
                    src_ref=p16[a].at[:, 2 * px + py], dst_ref=recv2[a].at[:, d - 1],
                    send_sem=send_sems.at[a, d - 1], recv_sem=recv_sems.at[a, d - 1], device_id=(px, py, c), device_id_type=MESH))
        for cp in rem:
            cp.start()
        for cp in rem:
            cp.wait_recv()
        for cp in rem:
            cp.wait_send()

    outs = _pallas_call(
        body, name=name,
        in_specs=[ANY] * n, out_specs=[ANY] * n,
        out_shape=[jax.ShapeDtypeStruct((s.shape[0], 3) + s.shape[2:], s.dtype) for s in pair16],
        scratch_shapes=[pltpu.SemaphoreType.DMA((n, 3))] * 2,
        compiler_params=pltpu.CompilerParams(has_side_effects=True),
    )(*pair16)
    return list(outs)


HBM = pl.BlockSpec(memory_space=pltpu.HBM)
SEM = pl.BlockSpec(memory_space=pltpu.SEMAPHORE)
EFFECT = pltpu.SideEffectType.DATAFLOW_SIDE_EFFECTING


def _in_hbm(a):
    return pltpu.with_memory_space_constraint(a, pltpu.HBM)


def split_start(name, bufs, n_copies, copies_of, deps=()):
    nb = len(bufs)

    def body(*refs):
        buf = refs[:nb]
        send_sems, recv_sems = refs[nb + len(deps)], refs[nb + len(deps) + 1]
        token = refs[-1]
        for i, (src, dst, dev) in enumerate(copies_of(buf)):
            pltpu.make_async_remote_copy(src_ref=src, dst_ref=dst, send_sem=send_sems.at[i], recv_sem=recv_sems.at[i],
                                         device_id=dev, device_id_type=MESH).start()
        token[...] = jnp.zeros_like(token)

    outs = _pallas_call(
        body, name=name,
        in_specs=[HBM] * nb + [ANY] * len(deps),
        out_specs=(SEM, SEM, *([HBM] * nb), pl.BlockSpec(memory_space=pltpu.VMEM)),
        out_shape=(pltpu.SemaphoreType.DMA((n_copies,)), pltpu.SemaphoreType.DMA((n_copies,)),
                   *[pltpu.HBM(b.shape, b.dtype) for b in bufs], jax.ShapeDtypeStruct((8, 128), F32)),
        input_output_aliases={i: 2 + i for i in range(nb)},
        compiler_params=pltpu.CompilerParams(has_side_effects=EFFECT),
    )(*[_in_hbm(b) for b in bufs], *deps)
    return outs[0], outs[1], list(outs[2:2 + nb]), outs[-1]


def split_wait(name, bufs, send_sems, recv_sems, after, copies_of):
    nb = len(bufs)

    def body(*refs):
        buf = refs[:nb]
        send, recv = refs[nb], refs[nb + 1]
        for i, (src, dst, dev) in enumerate(copies_of(buf)):
            cp = pltpu.make_async_remote_copy(src_ref=src, dst_ref=dst, send_sem=send.at[i], recv_sem=recv.at[i],
                                              device_id=dev, device_id_type=MESH)
            cp.wait_send()
            cp.wait_recv()

    outs = _pallas_call(
        body, name=name,
        in_specs=[HBM] * nb + [SEM, SEM] + [ANY] * len(after),
        out_specs=[HBM] * nb,
        out_shape=[pltpu.HBM(b.shape, b.dtype) for b in bufs],
        input_output_aliases={i: i for i in range(nb)},
        compiler_params=pltpu.CompilerParams(has_side_effects=EFFECT),
    )(*bufs, send_sems, recv_sems, *after)
    return list(outs)


def place_own(w, l, place, dtype):
    _, rows, cols = w.shape
    tr = _rows_tile(rows, cols, 1 << 19)

    def body(p_ref, w_ref, o_ref):
        o_ref[0] = w_ref[0].astype(dtype)

    return _pallas_call(
        body, name="place_own",
        grid_spec=pltpu.PrefetchScalarGridSpec(
            num_scalar_prefetch=1, grid=(rows // tr,),
            in_specs=[pl.BlockSpec((1, tr, cols), lambda i, p: (l, i, 0))],
            out_specs=pl.BlockSpec((1, tr, cols), lambda i, p: (p[2], i, 0))),
        out_shape=jax.ShapeDtypeStruct((N_DEV, rows, cols), dtype), compiler_params=_cp("parallel"),
    )(place, w)


def _gather_copies(land):
    x, y, c = _place()
    k = 4 * x + 2 * y + c
    peers = [(x, 1 - y, c), (1 - x, y, c), (1 - x, 1 - y, c), (x, y, 1 - c)]
    return [(b.at[k], b.at[k], p) for p in peers for b in land]


def gather_start(name, land, deps=()):
    return split_start(name, land, 4 * len(land), _gather_copies, deps)


def gather_wait(name, land, send_sems, recv_sems, after):
    return split_wait(name, land, send_sems, recv_sems, after, _gather_copies)


def _forward_copies(land):
    x, y, c = _place()
    slots = [4 * px + 2 * py + c for px, py in [(x, 1 - y), (1 - x, y), (1 - x, 1 - y)]]
    return [(b.at[k], b.at[k], (x, y, 1 - c)) for k in slots for b in land]


def gather_forward_start(name, land, deps=()):
    return split_start(name, land, 3 * len(land), _forward_copies, deps)


def gather_forward_wait(name, land, send_sems, recv_sems, after):
    return split_wait(name, land, send_sems, recv_sems, after, _forward_copies)


def _chip_copies(nsrc):
    def copies(buf):
        p16, recv2 = buf[:nsrc], buf[nsrc:]
        x, y, c = _place()
        out = []
        for d in (1, 2, 3):
            px = 1 - x if d & 2 else x
            py = 1 - y if d & 1 else y
            out += [(p16[a].at[:, 2 * px + py], recv2[a].at[:, d - 1], (px, py, c)) for a in range(nsrc)]
        return out
    return copies


def _pair_copies(nsrc):
    def copies(buf):
        g16, recv = buf[:nsrc], buf[nsrc:]
        x, y, c = _place()
        return [(g16[a].at[:, 2 * j + 1 - c], recv[a].at[:, j], (x, y, 1 - c)) for a in range(nsrc) for j in range(N_CHIP)]
    return copies


def pair_exchange_start(name, g16, deps=()):
    n = len(g16)
    land = [lax.empty((1, N_CHIP) + s.shape[2:], s.dtype) for s in g16]
    return split_start(name, list(g16) + land, N_CHIP * n, _pair_copies(n), deps)


def pair_exchange_wait(name, bufs, send_sems, recv_sems, after):
    n = len(bufs) // 2
    return split_wait(name, bufs, send_sems, recv_sems, after, _pair_copies(n))[n:]


def chip_exchange_start(name, pair16, deps=()):
    n = len(pair16)
    land = [lax.empty((s.shape[0], 3) + s.shape[2:], s.dtype) for s in pair16]
    return split_start(name, list(pair16) + land, 3 * n, _chip_copies(n), deps)


def chip_exchange_wait(name, bufs, send_sems, recv_sems, after):
    n = len(bufs) // 2
    return split_wait(name, bufs, send_sems, recv_sems, after, _chip_copies(n))[n:]


def _rows_tile(rows, cols, budget=1 << 20):
    t = rows
    while t % 2 == 0 and t * cols > budget and (t // 2) % 16 == 0:
        t //= 2
    return t


def pair_sum(g32, recv1, place, l):
    _, _, rows, cols = recv1.shape
    tr = _rows_tile(rows, cols)

    def body(p_ref, m_ref, r_ref, o_ref):
        o_ref[...] = (m_ref[...] + r_ref[...].astype(F32)).astype(o_ref.dtype)

    blk = pl.BlockSpec((1, 1, tr, cols), lambda j, i, p: (0, j, i, 0))
    return _pallas_call(
        body, name="pair_sum",
        grid_spec=pltpu.PrefetchScalarGridSpec(
            num_scalar_prefetch=1, grid=(N_CHIP, rows // tr),
            in_specs=[pl.BlockSpec((1, 1, tr, cols), lambda j, i, p: (l, 2 * j + p[0], i, 0)), blk], out_specs=blk),
        out_shape=jax.ShapeDtypeStruct(recv1.shape, recv1.dtype), compiler_params=_cp("parallel", "parallel"),
    )(place, g32, recv1)


def _grad_in_specs(tr, cols, l):
    return ([pl.BlockSpec((1, 1, tr, cols), lambda i, p: (l, p[2], i, 0)), pl.BlockSpec((1, 1, tr, cols), lambda i, p: (0, p[1], i, 0))]
            + [pl.BlockSpec((1, 1, tr, cols), lambda i, p, d=d: (0, d, i, 0)) for d in range(3)])


def _grad_total(o32, o16, r0, r1, r2):
    return (o32[0, 0] + o16[0, 0].astype(F32)) + r0[0, 0].astype(F32) + r1[0, 0].astype(F32) + r2[0, 0].astype(F32)


def grad_sum(g32, recv1, recv2, place):
    _, _, rows, cols = recv1.shape
    tr = _rows_tile(rows, cols)

    def body(p_ref, o32, o16, r0, r1, r2, g_ref):
        g_ref[...] = _grad_total(o32, o16, r0, r1, r2)

    return _pallas_call(
        body, name="grad_sum",
        grid_spec=pltpu.PrefetchScalarGridSpec(
            num_scalar_prefetch=1, grid=(rows // tr,), in_specs=_grad_in_specs(tr, cols, 0),
            out_specs=pl.BlockSpec((tr, cols), lambda i, p: (i, 0))),
        out_shape=jax.ShapeDtypeStruct((rows, cols), F32), compiler_params=_cp("parallel"),
    )(place, g32, recv1, recv2, recv2, recv2)


def _adamw_math(w, g, m, v):
    m = ADAM_B1 * m + (1.0 - ADAM_B1) * g
    v = ADAM_B2 * v + (1.0 - ADAM_B2) * (g * g)
    m_hat = m / (1.0 - ADAM_B1 ** ADAM_STEP)
    v_hat = v / (1.0 - ADAM_B2 ** ADAM_STEP)
    delta = -ADAM_LR * (m_hat / (jnp.sqrt(v_hat) + ADAM_EPS) + ADAM_WD * w)
    return delta, m, v


def grad_sum_adamw(g32, recv1, recv2, w, m, v, place, l, prev):
    L, rows, cols = w.shape
    tr = _rows_tile(rows, cols, 1 << 18)

    def body(p_ref, o32, o16, r0, r1, r2, w_ref, m_ref, v_ref, *rest):
        g_ref, d_ref, nm_ref, nv_ref = rest[-4:]
        g = _grad_total(o32, o16, r0, r1, r2)
        d, nm, nv = _adamw_math(w_ref[0], g, m_ref[0], v_ref[0])
        g_ref[0] = g
        d_ref[0] = d
        nm_ref[0] = nm
        nv_ref[0] = nv

    blk = pl.BlockSpec((1, tr, cols), lambda i, p: (l, i, 0))
    args = [g32, recv1, recv2, recv2, recv2, w, m, v]
    in_specs = _grad_in_specs(tr, cols, 0) + [blk] * 3
    aliases = {}
    if prev is not None:
        aliases = {1 + len(args) + k: k for k in range(4)}
        args += list(prev)
        in_specs += [ANY] * 4
    return _pallas_call(
        body, name="grad_sum_adamw",
        grid_spec=pltpu.PrefetchScalarGridSpec(num_scalar_prefetch=1, grid=(rows // tr,), in_specs=in_specs, out_specs=[blk] * 4),
        out_shape=[jax.ShapeDtypeStruct((L, rows, cols), F32)] * 4, input_output_aliases=aliases,
        compiler_params=_cp("parallel"),
    )(place, *args)


def adamw(w, g, m, v):
    rows, cols = w.shape
    tr = _rows_tile(rows, cols, 1 << 18)

    def body(w_ref, g_ref, m_ref, v_ref, d_ref, nm_ref, nv_ref):
        d, nm, nv = _adamw_math(w_ref[...], g_ref[...], m_ref[...], v_ref[...])
        d_ref[...] = d
        nm_ref[...] = nm
        nv_ref[...] = nv

    blk = pl.BlockSpec((tr, cols), lambda i: (i, 0))
    return _pallas_call(body, name="adamw_small", grid=(rows // tr,), in_specs=[blk] * 4, out_specs=[blk] * 3,
                        out_shape=[jax.ShapeDtypeStruct((rows, cols), F32)] * 3, compiler_params=_cp("parallel"))(w, g, m, v)


SMALL = ("norm_ffn1", "norm_mix", "pool_w", "pool_b", "pool_scale", "conv_w", "conv_b", "lru_w_a", "lru_b_a", "lru_w_x", "lru_b_x",
         "lru_lambda", "norm_ffn2", "final_norm")
BIG = ("ffn1_w_up", "ffn1_w_down", "w_in", "w_pool_up", "w_lru_up", "w_out", "ffn2_w_up", "ffn2_w_down")
NAMES = ("norm_ffn1", "ffn1_w_up", "ffn1_w_down", "norm_mix", "w_in", "pool_w", "pool_b", "pool_scale", "w_pool_up", "conv_w", "conv_b",
         "lru_w_a", "lru_b_a", "lru_w_x", "lru_b_x", "lru_lambda", "w_lru_up", "w_out", "norm_ffn2", "ffn2_w_up", "ffn2_w_down", "final_norm")
SUBLAYERS = (("ffn1_w_up", "ffn1_w_down"), ("w_in", "w_pool_up", "w_lru_up", "w_out", "conv_w"), ("ffn2_w_up", "ffn2_w_down"))
PACK_ROWS = 16 * N_DEV


def _pack(parts):
    flat = jnp.concatenate([p.reshape(-1) for p in parts])
    unit = 128 * PACK_ROWS
    padded = -(-flat.size // unit) * unit
    return jnp.pad(flat, (0, padded - flat.size)).reshape(-1, 128)


def _unpack(packed, shapes):
    flat = packed.reshape(-1)
    out, off = [], 0
    for s in shapes:
        n = 1
        for d in s:
            n *= d
        out.append(flat[off:off + n].reshape(s))
        off += n
    return out


def kernel(x, norm_ffn1, ffn1_w_up, ffn1_w_down, norm_mix, w_in, pool_w, pool_b, pool_scale, w_pool_up, conv_w, conv_b, lru_w_a, lru_b_a, lru_w_x, lru_b_x, lru_lambda, w_lru_up, w_out, norm_ffn2, ffn2_w_up, ffn2_w_down, final_norm, loss_target, m_norm_ffn1, m_ffn1_w_up, m_ffn1_w_down, m_norm_mix, m_w_in, m_pool_w, m_pool_b, m_pool_scale, m_w_pool_up, m_conv_w, m_conv_b, m_lru_w_a, m_lru_b_a, m_lru_w_x, m_lru_b_x, m_lru_lambda, m_w_lru_up, m_w_out, m_norm_ffn2, m_ffn2_w_up, m_ffn2_w_down, m_final_norm, v_norm_ffn1, v_ffn1_w_up, v_ffn1_w_down, v_norm_mix, v_w_in, v_pool_w, v_pool_b, v_pool_scale, v_w_pool_up, v_conv_w, v_conv_b, v_lru_w_a, v_lru_b_a, v_lru_w_x, v_lru_b_x, v_lru_lambda, v_w_lru_up, v_w_out, v_norm_ffn2, v_ffn2_w_up, v_ffn2_w_down, v_final_norm):
    W = dict(norm_ffn1=norm_ffn1, ffn1_w_up=ffn1_w_up, ffn1_w_down=ffn1_w_down, norm_mix=norm_mix, w_in=w_in, pool_w=pool_w, pool_b=pool_b,
             pool_scale=pool_scale, w_pool_up=w_pool_up, conv_w=conv_w, conv_b=conv_b, lru_w_a=lru_w_a, lru_b_a=lru_b_a, lru_w_x=lru_w_x,
             lru_b_x=lru_b_x, lru_lambda=lru_lambda, w_lru_up=w_lru_up, w_out=w_out, norm_ffn2=norm_ffn2, ffn2_w_up=ffn2_w_up,
             ffn2_w_down=ffn2_w_down, final_norm=final_norm)
    M = dict(norm_ffn1=m_norm_ffn1, ffn1_w_up=m_ffn1_w_up, ffn1_w_down=m_ffn1_w_down, norm_mix=m_norm_mix, w_in=m_w_in, pool_w=m_pool_w,
             pool_b=m_pool_b, pool_scale=m_pool_scale, w_pool_up=m_w_pool_up, conv_w=m_conv_w, conv_b=m_conv_b, lru_w_a=m_lru_w_a,
             lru_b_a=m_lru_b_a, lru_w_x=m_lru_w_x, lru_b_x=m_lru_b_x, lru_lambda=m_lru_lambda, w_lru_up=m_w_lru_up, w_out=m_w_out,
             norm_ffn2=m_norm_ffn2, ffn2_w_up=m_ffn2_w_up, ffn2_w_down=m_ffn2_w_down, final_norm=m_final_norm)
    V = dict(norm_ffn1=v_norm_ffn1, ffn1_w_up=v_ffn1_w_up, ffn1_w_down=v_ffn1_w_down, norm_mix=v_norm_mix, w_in=v_w_in, pool_w=v_pool_w,
             pool_b=v_pool_b, pool_scale=v_pool_scale, w_pool_up=v_w_pool_up, conv_w=v_conv_w, conv_b=v_conv_b, lru_w_a=v_lru_w_a,
             lru_b_a=v_lru_b_a, lru_w_x=v_lru_w_x, lru_b_x=v_lru_b_x, lru_lambda=v_lru_lambda, w_lru_up=v_w_lru_up, w_out=v_w_out,
             norm_ffn2=v_norm_ffn2, ffn2_w_up=v_ffn2_w_up, ffn2_w_down=v_ffn2_w_down, final_norm=v_final_norm)

    for S in (W, M, V):
        for n in ("ffn1_w_up", "ffn2_w_up"):
            S[n] = jnp.swapaxes(S[n], 1, 2)

    T, D = x.shape[1], x.shape[2]
    L = norm_ffn1.shape[0]
    P = pool_scale.shape[1]
    R = lru_lambda.shape[1]
    H, hd = lru_w_a.shape[1], lru_w_a.shape[2]
    CW = conv_w.shape[1]
    cs = ffn1_w_up.shape[2]
    ci = w_in.shape[2]
    xin = x.reshape(T, D)
    tgt = loss_target.reshape(T, D)
    dev = 4 * lax.axis_index("x") + 2 * lax.axis_index("y") + lax.axis_index("c")
    place = jnp.stack([lax.axis_index("c"), 2 * lax.axis_index("x") + lax.axis_index("y"), dev]).astype(jnp.int32)

    cw_flat = conv_w.reshape(L, -1)
    cw_pad = (-cw_flat.shape[1]) % 1024
    cw_tiles = jnp.pad(cw_flat, ((0, 0), (0, cw_pad))).reshape(L, -1, 128)

    def units(l):
        return SUBLAYERS if l == 0 else (tuple(n for u in SUBLAYERS for n in u),)

    queued = {"gather": (), "pair": (), "chip": ()}

    gathering = []

    def gather_units_start(l):
        for k, names in enumerate(SUBLAYERS):
            land = [place_own(cw_tiles, l, place, F32) if n == "conv_w" else place_own(W[n], l, place, BF16) for n in names]
            send_sems, recv_sems, land, tok = gather_start(f"gather_start_l{l}_u{k}", land, queued["gather"])
            gathering.append(dict(names=names, tag=f"l{l}_u{k}", send=send_sems, recv=recv_sems, land=land, tok=tok, arrived=False))
            queued["gather"] = (tok,)

    def gather_unit_arrive(after):
        waiting = [u for u in gathering if not u["arrived"]]
        if not waiting:
            return ()
        unit, tokens = waiting[0], [u["tok"] for u in waiting[1:]]
        land = gather_wait(f"gather_wait_{unit['tag']}", unit["land"], unit["send"], unit["recv"], list(after) + tokens)
        send_sems, recv_sems, land, tok = gather_forward_start(f"gather_pass_start_{unit['tag']}", land)
        unit.update(land=land, send=send_sems, recv=recv_sems, tok=tok, arrived=True)
        return (tok,)

    def gather_unit_weights(after):
        if not gathering[0]["arrived"]:
            gather_unit_arrive(after)
        unit = gathering.pop(0)
        land = gather_forward_wait(f"gather_pass_wait_{unit['tag']}", unit["land"], unit["send"], unit["recv"], after)
        g = dict(zip(unit["names"], land))
        one = lambda a: a.reshape((1,) + a.shape)
        w = {}
        for tag_, up, dn in (("1", "ffn1_w_up", "ffn1_w_down"), ("2", "ffn2_w_up", "ffn2_w_down")):
            if up in g:
                w["wup" + tag_], w["wd" + tag_] = one(g[up]), g[dn].reshape(1, -1, D)
        if "w_in" in g:
            cw_l = g["conv_w"].reshape(N_DEV, -1)[:, :cw_flat.shape[1]].reshape((N_DEV,) + conv_w.shape[1:])
            w.update(win=one(g["w_in"]), wlu=g["w_lru_up"].reshape(1, R, D), wout=g["w_out"].reshape(1, D, D),
                     wpu=g["w_pool_up"].transpose(1, 0, 2).reshape(1, P, D),
                     cw=cw_l.transpose(1, 0, 2).reshape(1, CW, R))
        return w

    vec = lambda a: a.reshape(L, 1, -1)
    p = dict(g1=vec(norm_ffn1), gm=vec(norm_mix), g2=vec(norm_ffn2), pb=vec(pool_b), ps=vec(pool_scale), cb=vec(conv_b),
             ba=vec(lru_b_a), bx=vec(lru_b_x), lam=vec(lru_lambda), pw=pool_w, wa=lru_w_a, wx=lru_w_x)

    AHEAD = 2
    for l in range(min(AHEAD, L)):
        gather_units_start(l)
    saved, LW = [], []
    xc = xin
    for l in range(L):
        w = gather_unit_weights([xc])
        if l + AHEAD < L:
            gather_units_start(l + AHEAD)
        sv = {"x1": xc}
        sv["h1"], sv["u1"], sv["s1"] = ffn_up(xc, p["g1"], w["wup1"], l)
        xc = ffn_down(sv["s1"], w["wd1"], xc, l, gather_unit_arrive([sv["s1"]]))
        sv["x2"] = xc
        w.update(gather_unit_weights([xc]))
        sv["h2"], sv["proj"] = mix_in(xc, p["gm"], w["win"], l)
        sv["pm"] = pool_fwd(sv["proj"], p["pw"], p["pb"], p["ps"], l)
        sv["hl"], sv["hs"] = lru_fwd(sv["proj"], w["cw"], p["cb"], p["wa"], p["ba"], p["wx"], p["bx"], p["lam"], P, l)
        xc, sv["yp"], sv["yl"], sv["z"] = mix_out(sv["pm"], sv["hl"], sv["proj"], xc, w["wpu"], w["wlu"], w["wout"], P, l,
                                                  gather_unit_arrive([sv["hl"]]))
        sv["x3"] = xc
        w.update(gather_unit_weights([xc]))
        sv["h3"], sv["u3"], sv["s3"] = ffn_up(xc, p["g2"], w["wup2"], l)
        xc = ffn_down(sv["s3"], w["wd2"], xc, l, gather_unit_arrive([sv["s3"]]))
        saved.append(sv)
        LW.append(w)

    loss_part, dx, d_final = loss_head(xc, final_norm.reshape(1, D), tgt)
    loss = lax.psum(loss_part[0, 0], ("x", "y", "c"))

    G = [dict() for _ in range(L)]
    small = {n: [None] * L for n in SMALL if n != "final_norm"}

    def to_slots(name, pair):
        if name == "w_pool_up":
            return tuple(a.reshape(1, P, N_DEV, D // N_DEV).transpose(0, 2, 1, 3) for a in pair)
        return tuple(a.reshape((1, N_DEV) + W[name].shape[1:]) for a in pair)

    def ffn_bwd(dy, sv, tag, wup, wd, gn, up_name, dn_name, norm_name, l, deps=()):
        dout, du = ffn_down_bwd(dy, wd, sv["u" + tag], l, deps)
        du = du.reshape(N_DEV, T, cs)
        G[l][dn_name] = to_slots(dn_name, dw_tn("dw_down", sv["s" + tag], lambda tk: pl.BlockSpec((1, tk, cs), lambda g, k: (g, k, 0)),
                                                dout, lambda tk: pl.BlockSpec((tk, D), lambda g, k: (k, 0)), 4, cs, D, T))
        G[l][up_name] = to_slots(up_name, dw_tn("dw_up", du, lambda tk: pl.BlockSpec((1, tk, cs), lambda g, k: (g, k, 0)),
                                                sv["h" + tag], lambda tk: pl.BlockSpec((tk, D), lambda g, k: (k, 0)), N_DEV, cs, D, T))
        dxn, dg = dx_norm_bwd("ffn_dx", du, lambda tm: pl.BlockSpec((1, tm, cs), lambda j, i: (j, i, 0)), wup, N_DEV,
                              sv["x" + tag], gn, dy, l, w_transposed=True)
        small[norm_name][l] = dg.reshape(D)
        return dxn

    pairing, in_flight = [], []

    def reduce_start(l, names, tag):
        names = [n for n in names if n != "conv_w"]
        send_sems, recv_sems, bufs, tok = pair_exchange_start(f"rs_pair_start_{tag}", [G[l][n][1] for n in names], queued["pair"])
        pairing.append((l, names, tag, send_sems, recv_sems, bufs))
        queued["pair"] = (tok,)
        return (tok,)

    def reduce_continue(after):
        l, names, tag, send_sems, recv_sems, bufs = pairing.pop(0)
        recv1 = pair_exchange_wait(f"rs_pair_wait_{tag}", bufs, send_sems, recv_sems, after)
        pair16 = [pair_sum(G[l][n][0], r_, place, 0) for n, r_ in zip(names, recv1)]
        send_sems, recv_sems, bufs, tok = chip_exchange_start(f"rs_chip_start_{tag}", pair16, queued["chip"])
        in_flight.append((l, names, tag, send_sems, recv_sems, bufs, recv1))
        queued["chip"] = (tok,)
        return (tok,)

    def boundary(l, k, dx_now):
        deps = reduce_continue([dx_now]) if pairing else ()
        if len(units(l)) > 1:
            deps += reduce_start(l, units(l)[k], f"l{l}_u{k}")
        elif k == 0:
            deps += reduce_start(l, units(l)[0], f"l{l}_u0")
        return deps

    deps = ()
    for l in reversed(range(L)):
        sv, w = saved[l], LW[l]
        dx = ffn_bwd(dx, sv, "3", w["wup2"], w["wd2"], p["g2"], "ffn2_w_up", "ffn2_w_down", "norm_ffn2", l, deps)
        deps = boundary(l, 2, dx)
        dyb, dyp, dyl, dgp, dgl, dpm, dhl = mix_out_bwd(dx, sv["proj"], sv["yp"], sv["yl"], w["wpu"], w["wlu"], w["wout"], P, R, l, deps)
        row = lambda wd_: (lambda tk: pl.BlockSpec((tk, wd_), lambda g, k: (k, 0)))
        G[l]["w_out"] = to_slots("w_out", dw_tn("dw_out", sv["z"], row(D), dyb, row(D), 1, D, D, T))
        G[l]["w_lru_up"] = to_slots("w_lru_up", dw_tn("dw_lru_up", sv["hl"], row(R), dyl, row(D), 1, R, D, T))
        G[l]["w_pool_up"] = to_slots("w_pool_up", dw_tn("dw_pool_up", sv["pm"], row(P), dyp, row(D), 1, P, D, T))
        du_lru, du_gelu, dcw, dcb, dwa, dba, dwx, dbx, dlam = lru_bwd(
            sv["proj"], sv["hs"], dhl, w["cw"], p["cb"], p["wa"], p["ba"], p["wx"], p["bx"], p["lam"], P, l)
        du_pool, dpw, dpb, dpsc = pool_bwd(sv["proj"], dpm, p["pw"], p["pb"], p["ps"], l)
        dproj = jnp.concatenate([du_pool, du_lru, du_gelu, dgp, dgl], axis=1)
        G[l]["w_in"] = to_slots("w_in", dw_tn("dw_in", sv["h2"], row(D), dproj, lambda tk: pl.BlockSpec((tk, ci), lambda g, k: (k, g)),
                                              N_DEV, D, ci, T))
        dx, dgm = dx_norm_bwd("mix_dx", dproj, lambda tm: pl.BlockSpec((tm, ci), lambda j, i: (i, j)), w["win"], N_DEV,
                              sv["x2"], p["gm"], dx, l)
        small["norm_mix"][l] = dgm.reshape(D)
        small["pool_w"][l], small["pool_b"][l], small["pool_scale"][l] = dpw[0], dpb.reshape(pool_b.shape[1:]), dpsc.reshape(P)
        small["conv_w"][l], small["conv_b"][l] = dcw[0], dcb.reshape(R)
        small["lru_w_a"][l], small["lru_b_a"][l] = dwa[0], dba.reshape(H, hd)
        small["lru_w_x"][l], small["lru_b_x"][l] = dwx[0], dbx.reshape(H, hd)
        small["lru_lambda"][l] = dlam.reshape(R)
        deps = boundary(l, 1, dx)
        dx = ffn_bwd(dx, sv, "1", w["wup1"], w["wd1"], p["g1"], "ffn1_w_up", "ffn1_w_down", "norm_ffn1", l, deps)
        deps = boundary(l, 0, dx)

    grad_x = dx.reshape(x.shape)

    small_parts = [jnp.stack(small[n]) for n in SMALL if n != "final_norm"] + [d_final.reshape(D)]
    small_shapes = [p.shape for p in small_parts]
    gpack = _pack(small_parts).reshape(1, N_DEV, -1, 128)
    recv1_s = pair_exchange("rs_pair_exchange_small", [gpack])[0]
    pair_s = pair_sum(gpack, recv1_s, place, 0)
    recv2_s = chip_exchange("rs_chip_exchange_small", [pair_s])[0]
    while pairing:
        reduce_continue([recv2_s])

    outs = {n: None for n in BIG}
    after = [dx, recv2_s]
    for l, names, tag, send_sems, recv_sems, bufs, recv1 in in_flight:
        recv2 = chip_exchange_wait(f"rs_chip_wait_{tag}", bufs, send_sems, recv_sems, after)
        for i, n in enumerate(names):
            outs[n] = grad_sum_adamw(G[l][n][0], recv1[i], recv2[i], W[n], M[n], V[n], place, l, outs[n])
        after = [outs[n][0] for n in names]
    for n in ("ffn1_w_up", "ffn2_w_up"):
        outs[n] = [jnp.swapaxes(o, 1, 2) for o in outs[n]]
    out_g, out_d, out_m, out_v = ({n: outs[n][k] for n in BIG} for k in range(4))

    gs = grad_sum(gpack, recv1_s, recv2_s, place)
    gs_all = all_gather("all_gather_small_grads", [gs.reshape((1,) + gs.shape)])[0]
    gs_all = gs_all.reshape(-1, 128)
    small_g = dict(zip(SMALL, _unpack(gs_all, small_shapes)))
    full_shapes = [W[n].shape if n != "conv_w" else small_shapes[SMALL.index("conv_w")] for n in SMALL]
    rep = [n for n in SMALL if n != "conv_w"]
    rep_shapes = [W[n].shape for n in rep]
    wp, mp, vp = (_pack([S[n] for n in rep]) for S in (W, M, V))
    gp = _pack([small_g[n] for n in rep])
    dp, nmp, nvp = adamw(wp, gp, mp, vp)
    for S, packed in ((out_d, dp), (out_m, nmp), (out_v, nvp)):
        S.update(zip(rep, _unpack(packed, rep_shapes)))
    for n in rep:
        out_g[n] = small_g[n]
    cwc = conv_w.shape[2]
    gcw = lax.dynamic_slice_in_dim(small_g["conv_w"], dev * cwc, cwc, axis=2)
    cw2 = lambda a: a.reshape(-1, cwc)
    pad_rows = (-cw2(conv_w).shape[0]) % 8
    padr = lambda a: jnp.pad(cw2(a), ((0, pad_rows), (0, 0)))
    dcw_, mcw_, vcw_ = adamw(padr(conv_w), padr(gcw), padr(M["conv_w"]), padr(V["conv_w"]))
    nrow = cw2(conv_w).shape[0]
    out_g["conv_w"] = gcw
    out_d["conv_w"], out_m["conv_w"], out_v["conv_w"] = (a[:nrow].reshape(conv_w.shape) for a in (dcw_, mcw_, vcw_))
    del full_shapes

    return (loss, grad_x, *[out_g[n] for n in NAMES], *[out_d[n] for n in NAMES], *[out_m[n] for n in NAMES], *[out_v[n] for n in NAMES])
```

```python
import functools

import jax
import jax.numpy as jnp
from jax import lax
from jax.experimental import pallas as pl
from jax.experimental.pallas import tpu as pltpu

F32, BF16 = jnp.float32, jnp.bfloat16
EPS = 1e-6
LRU_C = 8.0
POOL_WINDOWS = (2, 4, 8, 16)
ADAM_LR, ADAM_B1, ADAM_B2, ADAM_EPS, ADAM_WD, ADAM_STEP = 0.001, 0.9, 0.999, 1e-08, 0.01, 10
N_DEV = 8
N_CHIP = 4
MESH = pl.DeviceIdType.MESH
V7X_VMEM_LIMIT = 56 * 1024 * 1024
ROW_TILE = 512
WIDE_TILE = 1024
SUM_TILE = 2048
ANY = pl.BlockSpec(memory_space=pl.ANY)

_pallas_call = pl.pallas_call


def _cp(*sem):
    return pltpu.CompilerParams(dimension_semantics=sem if sem else None, vmem_limit_bytes=V7X_VMEM_LIMIT)


def _tile(n, t):
    t = min(n, t)
    assert n % t == 0, (n, t)
    return t


def _dot(a, b):
    return jnp.dot(a, b, preferred_element_type=F32)


def _dot_nt(a, b):
    return lax.dot_general(a, b, (((1,), (1,)), ((), ())), preferred_element_type=F32)


def _dot_tn(a, b):
    return lax.dot_general(a, b, (((0,), (0,)), ((), ())), preferred_element_type=F32)


def _rms(xv):
    r = lax.rsqrt(jnp.mean(xv * xv, axis=-1, keepdims=True) + EPS)
    return xv * r, r


def _rms_bwd(dh, xv, gv, dy):
    n, r = _rms(xv)
    dn = dh * gv
    dx = dy + r * (dn - n * jnp.mean(dn * n, axis=-1, keepdims=True))
    return dx, jnp.sum(dh * n, axis=0, keepdims=True)


def _shift_down(x, k, fill=0.0):
    if k == 0:
        return x
    rows = lax.broadcasted_iota(jnp.int32, x.shape, 0)
    return jnp.where(rows >= k, pltpu.roll(x, k, 0), fill)


def _shift_up(x, k, fill=0.0):
    if k == 0:
        return x
    n = x.shape[0]
    rows = lax.broadcasted_iota(jnp.int32, x.shape, 0)
    return jnp.where(rows < n - k, pltpu.roll(x, n - k, 0), fill)


def _sigmoid(x):
    return 0.5 * jnp.tanh(0.5 * x) + 0.5


_GELU_K = 0.7978845608028654
_GELU_C = 0.044715


def _gelu(x):
    th = jnp.tanh(_GELU_K * (x + _GELU_C * x * x * x))
    return 0.5 * x * (1.0 + th), th


def _gelu_grad(x, th):
    return 0.5 * (1.0 + th) + 0.5 * x * (1.0 - th * th) * _GELU_K * (1.0 + 3.0 * _GELU_C * x * x)


def ffn_up(x, g, wup, l):
    T, D = x.shape
    cs = wup.shape[-2]
    tm = _tile(T, WIDE_TILE)
    ni = T // tm

    def body(x_ref, g_ref, wa_ref, wb_ref, h_ref, u_ref, s_ref, hs_ref):
        rows = pl.ds(pl.multiple_of(pl.program_id(1) * tm, tm), tm)

        @pl.when(pl.program_id(0) == 0)
        def _():
            n, _r = _rms(x_ref[...])
            hv = (n * g_ref[0]).astype(BF16)
            hs_ref[rows, :] = hv
            h_ref[...] = hv

        hv = hs_ref[rows, :]
        a = _dot_nt(hv, wa_ref[0, 0])
        b = _dot_nt(hv, wb_ref[0, 0])
        u_ref[0, 0] = a.astype(BF16)
        u_ref[1, 0] = b.astype(BF16)
        s_ref[0] = (a * _sigmoid(a) * b).astype(BF16)

    first = lambda j, i: (jnp.where(j == 0, i, ni - 1), 0)
    return _pallas_call(
        body, name="ffn_up", grid=(4, ni),
        in_specs=[pl.BlockSpec((tm, D), first), pl.BlockSpec((1, 1, D), lambda j, i: (l, 0, 0)),
                  pl.BlockSpec((1, 1, cs, D), lambda j, i: (0, j, 0, 0)), pl.BlockSpec((1, 1, cs, D), lambda j, i: (0, j + 4, 0, 0))],
        out_specs=[pl.BlockSpec((tm, D), first), pl.BlockSpec((2, 1, tm, cs), lambda j, i: (0, j, i, 0)),
                   pl.BlockSpec((1, tm, cs), lambda j, i: (j, i, 0))],
        out_shape=[jax.ShapeDtypeStruct((T, D), BF16), jax.ShapeDtypeStruct((2, 4, T, cs), BF16), jax.ShapeDtypeStruct((4, T, cs), BF16)],
        scratch_shapes=[pltpu.VMEM((T, D), BF16)],
        compiler_params=_cp("arbitrary", "arbitrary"),
    )(x, g, wup, wup)


def ffn_down(s, wd, x, l, deps=()):
    _, T, cs = s.shape
    D = x.shape[1]
    tm = _tile(T, WIDE_TILE)

    def body(s_ref, w_ref, x_ref, *rest):
        o_ref, acc_ref = rest[len(deps):]
        j = pl.program_id(1)

        @pl.when(j == 0)
        def _():
            acc_ref[...] = jnp.zeros_like(acc_ref)

        acc_ref[...] += _dot(s_ref[0], w_ref[0])

        @pl.when(j == 3)
        def _():
            o_ref[...] = x_ref[...] + 0.5 * acc_ref[...]

    return _pallas_call(
        body, name="ffn_down", grid=(T // tm, 4),
        in_specs=[pl.BlockSpec((1, tm, cs), lambda i, j: (j, i, 0)), pl.BlockSpec((1, cs, D), lambda i, j: (0, j, 0)),
                  pl.BlockSpec((tm, D), lambda i, j: (i, 0))] + [ANY] * len(deps),
        out_specs=pl.BlockSpec((tm, D), lambda i, j: (i, 0)),
        out_shape=jax.ShapeDtypeStruct((T, D), F32),
        scratch_shapes=[pltpu.VMEM((tm, D), F32)],
        compiler_params=_cp("parallel", "arbitrary"),
    )(s, wd, x, *deps)


def mix_in(x, g, win, l):
    T, D = x.shape
    ci = win.shape[-1]
    tm = _tile(T, WIDE_TILE)
    ni = T // tm

    def body(x_ref, g_ref, w_ref, h_ref, p_ref, hs_ref):
        rows = pl.ds(pl.multiple_of(pl.program_id(1) * tm, tm), tm)

        @pl.when(pl.program_id(0) == 0)
        def _():
            n, _r = _rms(x_ref[...])
            hv = (n * g_ref[0]).astype(BF16)
            hs_ref[rows, :] = hv
            h_ref[...] = hv

        p_ref[...] = _dot(hs_ref[rows, :], w_ref[0, 0]).astype(BF16)

    first = lambda j, i: (jnp.where(j == 0, i, ni - 1), 0)
    return _pallas_call(
        body, name="mix_in", grid=(N_DEV, ni),
        in_specs=[pl.BlockSpec((tm, D), first), pl.BlockSpec((1, 1, D), lambda j, i: (l, 0, 0)),
                  pl.BlockSpec((1, 1, D, ci), lambda j, i: (0, j, 0, 0))],
        out_specs=[pl.BlockSpec((tm, D), first), pl.BlockSpec((tm, ci), lambda j, i: (i, j))],
        out_shape=[jax.ShapeDtypeStruct((T, D), BF16), jax.ShapeDtypeStruct((T, N_DEV * ci), BF16)],
        scratch_shapes=[pltpu.VMEM((T, D), BF16)],
        compiler_params=_cp("arbitrary", "arbitrary"),
    )(x, g, win)


def _inv_count(T, w):
    t = lax.broadcasted_iota(jnp.int32, (T, 1), 0)
    return 1.0 / jnp.minimum(t + 1, w).astype(F32)


def _pooled(ug, w, inv):
    s = ug
    k = 1
    while k < w:
        s = s + _shift_down(s, k)
        k *= 2
    return s * inv - ug


def pool_fwd(proj, pw, pb, ps, l):
    T = proj.shape[0]
    _, G, gd, _ = pw.shape
    P = G * gd

    def body(u_ref, w_ref, b_ref, s_ref, o_ref):
        for gi in range(G):
            cols = slice(gi * gd, (gi + 1) * gd)
            ug = u_ref[:, cols].astype(F32)
            pooled = _pooled(ug, POOL_WINDOWS[gi], _inv_count(T, POOL_WINDOWS[gi]))
            mixed = _dot(pooled.astype(BF16), w_ref[0, gi].astype(BF16)) + b_ref[0, :, cols]
            o_ref[:, cols] = (mixed * s_ref[0, :, cols]).astype(BF16)

    return _pallas_call(
        body, name="pool_fwd", grid=(1,),
        in_specs=[pl.BlockSpec((T, P), lambda i: (0, 0)), pl.BlockSpec((1, G, gd, gd), lambda i: (l, 0, 0, 0)),
                  pl.BlockSpec((1, 1, P), lambda i: (l, 0, 0)), pl.BlockSpec((1, 1, P), lambda i: (l, 0, 0))],
        out_specs=pl.BlockSpec((T, P), lambda i: (0, 0)),
        out_shape=jax.ShapeDtypeStruct((T, P), BF16),
        compiler_params=_cp("arbitrary"),
    )(proj, pw, pb, ps)


def _conv(u, cw_ref, cb):
    CW = cw_ref.shape[1]
    v = cb
    for k in range(CW):
        v = v + cw_ref[0, k:k + 1, :] * _shift_down(u, CW - 1 - k)
    return v


def _softplus(z):
    return jnp.maximum(z, 0.0) + jnp.log1p(jnp.exp(-jnp.abs(z)))


def _lru_gates(v, wa_ref, ba, wx_ref, bx, lam):
    vb = v.astype(BF16)
    r = _sigmoid(_dot(vb, wa_ref[0, 0].astype(BF16)) + ba)
    i = _sigmoid(_dot(vb, wx_ref[0, 0].astype(BF16)) + bx)
    sp = _softplus(-lam)
    log_a = -LRU_C * r * sp
    a = jnp.exp(log_a)
    m2 = -jnp.tanh(log_a) * (a * a + 1.0)
    inv_mult = lax.rsqrt(m2)
    mult = jnp.where(m2 > 0.0, m2 * inv_mult, 0.0)
    return r, i, sp, a, mult, inv_mult


def _scan_fwd(a_ref, b_ref, o_ref):
    T, W = a_ref.shape
    rows = lax.broadcasted_iota(jnp.int32, (8, W), 0)

    def step(t, carry):
        r0 = pl.multiple_of(t * 8, 8)
        A = a_ref[pl.ds(r0, 8), :]
        B = b_ref[pl.ds(r0, 8), :]
        for s in (1, 2, 4):
            keep = rows >= s
            As = jnp.where(keep, pltpu.roll(A, s, 0), 1.0)
            Bs = jnp.where(keep, pltpu.roll(B, s, 0), 0.0)
            B = A * Bs + B
            A = A * As
        h = B + A * carry
        o_ref[pl.ds(r0, 8), :] = h
        return jnp.broadcast_to(h[7:8, :], (8, W))

    lax.fori_loop(0, T // 8, step, jnp.zeros((8, W), F32), unroll=8)


def _scan_bwd(a_ref, b_ref, o_ref):
    T, W = a_ref.shape
    rows = lax.broadcasted_iota(jnp.int32, (8, W), 0)
    nt = T // 8

    def step(t, carry):
        r0 = pl.multiple_of((nt - 1 - t) * 8, 8)
        A = a_ref[pl.ds(r0, 8), :]
        B = b_ref[pl.ds(r0, 8), :]
        for s in (1, 2, 4):
            keep = rows < 8 - s
            As = jnp.where(keep, pltpu.roll(A, 8 - s, 0), 1.0)
            Bs = jnp.where(keep, pltpu.roll(B, 8 - s, 0), 0.0)
            B = A * Bs + B
            A = A * As
        y = B + A * carry
        o_ref[pl.ds(r0, 8), :] = y
        return jnp.broadcast_to(y[0:1, :], (8, W))

    lax.fori_loop(0, nt, step, jnp.zeros((8, W), F32), unroll=8)


def _lru_specs(T, hd, P, R, CW, l):
    ob, gb = P // hd, (P + R) // hd
    vec = pl.BlockSpec((1, 1, hd), lambda h: (l, 0, h))
    mat = pl.BlockSpec((1, 1, hd, hd), lambda h: (l, h, 0, 0))
    return [pl.BlockSpec((T, hd), lambda h: (0, ob + h)), pl.BlockSpec((T, hd), lambda h: (0, gb + h)),
            pl.BlockSpec((1, CW, hd), lambda h: (0, 0, h)), vec, mat, vec, mat, vec, vec]


def lru_fwd(proj, cw, cb, wa, ba, wx, bx, lam, P, l):
    T = proj.shape[0]
    _, H, hd, _ = wa.shape
    R = H * hd
    CW = cw.shape[1]
    assert P % hd == 0 and T % 8 == 0

    def body(u_ref, ug_ref, cw_ref, cb_ref, wa_ref, ba_ref, wx_ref, bx_ref, lam_ref, hl_ref, hs_ref, a_s, b_s):
        v = _conv(u_ref[...].astype(F32), cw_ref, cb_ref[0])
        _r, i, _sp, a, mult, _im = _lru_gates(v, wa_ref, ba_ref[0], wx_ref, bx_ref[0], lam_ref[0])
        a_s[...] = a
        b_s[...] = mult * (i * v)
        _scan_fwd(a_s, b_s, hs_ref)
        ge, _th = _gelu(ug_ref[...].astype(F32))
        hl_ref[...] = (hs_ref[...] * ge).astype(BF16)

    out = pl.BlockSpec((T, hd), lambda h: (0, h))
    return _pallas_call(
        body, name="lru_fwd", grid=(H,),
        in_specs=_lru_specs(T, hd, P, R, CW, l),
        out_specs=[out, out],
        out_shape=[jax.ShapeDtypeStruct((T, R), BF16), jax.ShapeDtypeStruct((T, R), F32)],
        scratch_shapes=[pltpu.VMEM((T, hd), F32)] * 2,
        compiler_params=_cp("parallel"),
    )(proj, proj, cw, cb, wa, ba, wx, bx, lam)


def mix_out(pm, hl, proj, x, wpu, wlu, wout, P, l, deps=()):
    T, D = x.shape
    R = hl.shape[1]
    tm = _tile(T, ROW_TILE)
    assert (P + 2 * R) % D == 0
    gb = (P + 2 * R) // D

    def body(pm_ref, hl_ref, gp_ref, gl_ref, x_ref, wpu_ref, wlu_ref, wo_ref, *rest):
        o_ref, yp_ref, yl_ref, z_ref = rest[len(deps):]
        yp = _dot(pm_ref[...], wpu_ref[0])
        yl = _dot(hl_ref[...], wlu_ref[0])
        z = (_sigmoid(gp_ref[...].astype(F32)) * yp + _sigmoid(gl_ref[...].astype(F32)) * yl).astype(BF16)
        yp_ref[...] = yp.astype(BF16)
        yl_ref[...] = yl.astype(BF16)
        z_ref[...] = z
        o_ref[...] = x_ref[...] + _dot(z, wo_ref[0])

    row = lambda w: pl.BlockSpec((tm, w), lambda i: (i, 0))
    return _pallas_call(
        body, name="mix_out", grid=(T // tm,),
        in_specs=[row(P), row(R), pl.BlockSpec((tm, D), lambda i: (i, gb)), pl.BlockSpec((tm, D), lambda i: (i, gb + 1)), row(D),
                  pl.BlockSpec((1, P, D), lambda i: (0, 0, 0)), pl.BlockSpec((1, R, D), lambda i: (0, 0, 0)),
                  pl.BlockSpec((1, D, D), lambda i: (0, 0, 0))] + [ANY] * len(deps),
        out_specs=[row(D)] * 4,
        out_shape=[jax.ShapeDtypeStruct((T, D), F32)] + [jax.ShapeDtypeStruct((T, D), BF16)] * 3,
        compiler_params=_cp("parallel"),
    )(pm, hl, proj, proj, x, wpu, wlu, wout, *deps)


def loss_head(x, gf, tgt):
    T, D = x.shape
    tm = _tile(T, ROW_TILE)

    def body(x_ref, g_ref, t_ref, loss_ref, dx_ref, dg_ref):
        @pl.when(pl.program_id(0) == 0)
        def _():
            loss_ref[...] = jnp.zeros_like(loss_ref)
            dg_ref[...] = jnp.zeros_like(dg_ref)

        xv = x_ref[...]
        gv = g_ref[...]
        n, _r = _rms(xv)
        e = n * gv - t_ref[...]
        loss_ref[...] += 0.5 * jnp.sum(jnp.sum(e * e, axis=-1, keepdims=True), axis=0, keepdims=True) / D
        dx, dg = _rms_bwd(e * (1.0 / D), xv, gv, 0.0)
        dx_ref[...] = dx
        dg_ref[...] += dg

    return _pallas_call(
        body, name="loss_head", grid=(T // tm,),
        in_specs=[pl.BlockSpec((tm, D), lambda i: (i, 0)), pl.BlockSpec((1, D), lambda i: (0, 0)), pl.BlockSpec((tm, D), lambda i: (i, 0))],
        out_specs=[pl.BlockSpec((1, 1), lambda i: (0, 0)), pl.BlockSpec((tm, D), lambda i: (i, 0)), pl.BlockSpec((1, D), lambda i: (0, 0))],
        out_shape=[jax.ShapeDtypeStruct((1, 1), F32), jax.ShapeDtypeStruct((T, D), F32), jax.ShapeDtypeStruct((1, D), F32)],
        compiler_params=_cp("arbitrary"),
    )(x, gf, tgt)


def ffn_down_bwd(dy, wd, u, l, deps=()):
    T, D = dy.shape
    cs = u.shape[-1]
    tm = _tile(T, WIDE_TILE)
    ni = T // tm

    def body(dy_ref, w_ref, u_ref, *rest):
        do_ref, du_ref, dyb_ref = rest[len(deps):]
        rows = pl.ds(pl.multiple_of(pl.program_id(1) * tm, tm), tm)

        @pl.when(pl.program_id(0) == 0)
        def _():
            d = (0.5 * dy_ref[...]).astype(BF16)
            dyb_ref[rows, :] = d
            do_ref[...] = d

        ds = _dot_nt(dyb_ref[rows, :], w_ref[0])
        a = u_ref[0, 0].astype(F32)
        b = u_ref[1, 0].astype(F32)
        sg = _sigmoid(a)
        du_ref[0, 0] = (ds * b * (sg * (1.0 + a * (1.0 - sg)))).astype(BF16)
        du_ref[1, 0] = (ds * (a * sg)).astype(BF16)

    first = lambda j, i: (jnp.where(j == 0, i, ni - 1), 0)
    blk = pl.BlockSpec((2, 1, tm, cs), lambda j, i: (0, j, i, 0))
    return _pallas_call(
        body, name="ffn_down_bwd", grid=(4, ni),
        in_specs=[pl.BlockSpec((tm, D), first), pl.BlockSpec((1, cs, D), lambda j, i: (0, j, 0)), blk] + [ANY] * len(deps),
        out_specs=[pl.BlockSpec((tm, D), first), blk],
        out_shape=[jax.ShapeDtypeStruct((T, D), BF16), jax.ShapeDtypeStruct((2, 4, T, cs), BF16)],
        scratch_shapes=[pltpu.VMEM((T, D), BF16)],
        compiler_params=_cp("arbitrary", "arbitrary"),
    )(dy, wd, u, *deps)


def dw_tn(name, a, a_spec, b, b_spec, G, M, N, T):
    tk = _tile(T, SUM_TILE)
    nk = T // tk

    def body(a_ref, b_ref, o32_ref, o16_ref, acc_ref):
        k = pl.program_id(1)

        @pl.when(k == 0)
        def _():
            acc_ref[...] = jnp.zeros_like(acc_ref)

        av = a_ref[0] if len(a_ref.shape) == 3 else a_ref[...]
        bv = b_ref[0] if len(b_ref.shape) == 3 else b_ref[...]
        acc_ref[...] += _dot_tn(av, bv)

        @pl.when(k == nk - 1)
        def _():
            o32_ref[0, 0] = acc_ref[...]
            o16_ref[0, 0] = acc_ref[...].astype(BF16)

    out = pl.BlockSpec((1, 1, M, N), lambda g, k: (0, g, 0, 0))
    return _pallas_call(
        body, name=name, grid=(G, nk),
        in_specs=[a_spec(tk), b_spec(tk)], out_specs=[out, out],
        out_shape=[jax.ShapeDtypeStruct((1, G, M, N), F32), jax.ShapeDtypeStruct((1, G, M, N), BF16)],
        scratch_shapes=[pltpu.VMEM((M, N), F32)],
        compiler_params=_cp("parallel", "arbitrary"),
    )(a, b)


def dx_norm_bwd(name, dact, d_spec, w, G, x, g, dy, l, w_transposed=False):
    T, D = x.shape
    wblk = w.shape[-2:]
    tm = _tile(T, WIDE_TILE)
    ni = T // tm
    ch = _tile(tm, ROW_TILE // 2)

    def body(d_ref, w_ref, x_ref, g_ref, dy_ref, dx_ref, dg_ref, acc_ref):
        j, i = pl.program_id(0), pl.program_id(1)
        rows = pl.ds(pl.multiple_of(i * tm, tm), tm)

        @pl.when(jnp.logical_and(i == 0, j == 0))
        def _():
            dg_ref[...] = jnp.zeros_like(dg_ref)

        @pl.when(j == 0)
        def _():
            acc_ref[rows, :] = jnp.zeros((tm, D), F32)

        dv = d_ref[0] if len(d_ref.shape) == 3 else d_ref[...]
        acc_ref[rows, :] += _dot(dv, w_ref[0, 0]) if w_transposed else _dot_nt(dv, w_ref[0, 0])

        @pl.when(j == G - 1)
        def _():
            dg = jnp.zeros((1, D), F32)
            for c0 in range(0, tm, ch):
                part_rows = pl.ds(pl.multiple_of(i * tm + c0, ch), ch)
                dx, dgc = _rms_bwd(acc_ref[part_rows, :], x_ref[c0:c0 + ch, :], g_ref[0], dy_ref[c0:c0 + ch, :])
                dx_ref[c0:c0 + ch, :] = dx
                dg = dg + dgc
            dg_ref[...] += dg

    last = pl.BlockSpec((tm, D), lambda j, i: (jnp.where(j == G - 1, i, 0), 0))
    return _pallas_call(
        body, name=name, grid=(G, ni),
        in_specs=[d_spec(tm), pl.BlockSpec((1, 1) + wblk, lambda j, i: (0, j, 0, 0)), last, pl.BlockSpec((1, 1, D), lambda j, i: (l, 0, 0)), last],
        out_specs=[last, pl.BlockSpec((1, D), lambda j, i: (0, 0))],
        out_shape=[jax.ShapeDtypeStruct((T, D), F32), jax.ShapeDtypeStruct((1, D), F32)],
        scratch_shapes=[pltpu.VMEM((T, D), F32)],
        compiler_params=_cp("arbitrary", "arbitrary"),
    )(dact, w, x, g, dy)


def mix_out_bwd(dy, proj, yp, yl, wpu, wlu, wout, P, R, l, deps=()):
    T, D = dy.shape
    tm = _tile(T, ROW_TILE)
    gb = (P + 2 * R) // D

    def body(dy_ref, gp_ref, gl_ref, yp_ref, yl_ref, wpu_ref, wlu_ref, wo_ref, *rest):
        dyb_ref, dyp_ref, dyl_ref, dgp_ref, dgl_ref, dpm_ref, dhl_ref = rest[len(deps):]
        dyb = dy_ref[...].astype(BF16)
        dyb_ref[...] = dyb
        dz = _dot_nt(dyb, wo_ref[0])
        sp = _sigmoid(gp_ref[...].astype(F32))
        sl = _sigmoid(gl_ref[...].astype(F32))
        dgp_ref[...] = (dz * yp_ref[...].astype(F32) * sp * (1.0 - sp)).astype(BF16)
        dgl_ref[...] = (dz * yl_ref[...].astype(F32) * sl * (1.0 - sl)).astype(BF16)
        dyp = (dz * sp).astype(BF16)
        dyl = (dz * sl).astype(BF16)
        dyp_ref[...] = dyp
        dyl_ref[...] = dyl
        dpm_ref[...] = _dot_nt(dyp, wpu_ref[0]).astype(BF16)
        dhl_ref[...] = _dot_nt(dyl, wlu_ref[0]).astype(BF16)

    row = lambda w: pl.BlockSpec((tm, w), lambda i: (i, 0))
    return _pallas_call(
        body, name="mix_out_bwd", grid=(T // tm,),
        in_specs=[row(D), pl.BlockSpec((tm, D), lambda i: (i, gb)), pl.BlockSpec((tm, D), lambda i: (i, gb + 1)), row(D), row(D),
                  pl.BlockSpec((1, P, D), lambda i: (0, 0, 0)), pl.BlockSpec((1, R, D), lambda i: (0, 0, 0)),
                  pl.BlockSpec((1, D, D), lambda i: (0, 0, 0))] + [ANY] * len(deps),
        out_specs=[row(D)] * 5 + [row(P), row(R)],
        out_shape=[jax.ShapeDtypeStruct((T, D), BF16)] * 5 + [jax.ShapeDtypeStruct((T, P), BF16), jax.ShapeDtypeStruct((T, R), BF16)],
        compiler_params=_cp("parallel"),
    )(dy, proj, proj, yp, yl, wpu, wlu, wout, *deps)


def lru_bwd(proj, hs, dhl, cw, cb, wa, ba, wx, bx, lam, P, l):
    T = proj.shape[0]
    _, H, hd, _ = wa.shape
    R = H * hd
    CW = cw.shape[1]

    def body(u_ref, ug_ref, cw_ref, cb_ref, wa_ref, ba_ref, wx_ref, bx_ref, lam_ref, hs_ref, dhl_ref,
             du_ref, dug_ref, dcw_ref, dcb_ref, dwa_ref, dba_ref, dwx_ref, dbx_ref, dlam_ref, c_s, g_s, y_s):
        u = u_ref[...].astype(F32)
        v = _conv(u, cw_ref, cb_ref[0])
        lam = lam_ref[0]
        r, i, sp, a, mult, inv_mult = _lru_gates(v, wa_ref, ba_ref[0], wx_ref, bx_ref[0], lam)
        ug = ug_ref[...].astype(F32)
        ge, th = _gelu(ug)
        hs = hs_ref[...]
        dhl = dhl_ref[...].astype(F32)
        dug_ref[...] = (dhl * hs * _gelu_grad(ug, th)).astype(BF16)
        c_s[...] = _shift_up(a, 1)
        g_s[...] = dhl * ge
        _scan_bwd(c_s, g_s, y_s)
        y = y_s[...]
        da = y * _shift_down(hs, 1)
        iv = i * v
        dlog_a = da * a - (y * iv) * (a * a) * inv_mult
        div = y * mult
        dpa = (dlog_a * (-LRU_C) * sp) * r * (1.0 - r)
        dpx = (div * v) * i * (1.0 - i)
        dsp = jnp.sum(dlog_a * (-LRU_C) * r, axis=0, keepdims=True)
        dlam_ref[0] = -dsp * _sigmoid(-lam)
        vb = v.astype(BF16)
        dpab, dpxb = dpa.astype(BF16), dpx.astype(BF16)
        dwa_ref[0, 0] = _dot_tn(vb, dpab)
        dwx_ref[0, 0] = _dot_tn(vb, dpxb)
        dba_ref[0] = jnp.sum(dpa, axis=0, keepdims=True)
        dbx_ref[0] = jnp.sum(dpx, axis=0, keepdims=True)
        dv = div * i + _dot_nt(dpab, wa_ref[0, 0].astype(BF16)) + _dot_nt(dpxb, wx_ref[0, 0].astype(BF16))
        dcb_ref[0] = jnp.sum(dv, axis=0, keepdims=True)
        du = jnp.zeros_like(dv)
        for k in range(CW):
            du = du + cw_ref[0, k:k + 1, :] * _shift_up(dv, CW - 1 - k)
            dcw_ref[0, k:k + 1, :] = jnp.sum(dv * _shift_down(u, CW - 1 - k), axis=0, keepdims=True)
        du_ref[...] = du.astype(BF16)

    col = pl.BlockSpec((T, hd), lambda h: (0, h))
    vec = pl.BlockSpec((1, 1, hd), lambda h: (0, 0, h))
    mat = pl.BlockSpec((1, 1, hd, hd), lambda h: (0, h, 0, 0))
    vshape = jax.ShapeDtypeStruct((1, 1, R), F32)
    mshape = jax.ShapeDtypeStruct((1, H, hd, hd), F32)
    return _pallas_call(
        body, name="lru_bwd", grid=(H,),
        in_specs=_lru_specs(T, hd, P, R, CW, l) + [col, col],
        out_specs=[col, col, pl.BlockSpec((1, CW, hd), lambda h: (0, 0, h)), vec, mat, vec, mat, vec, vec],
        out_shape=[jax.ShapeDtypeStruct((T, R), BF16)] * 2 + [jax.ShapeDtypeStruct((1, CW, R), F32), vshape, mshape, vshape, mshape, vshape, vshape],
        scratch_shapes=[pltpu.VMEM((T, hd), F32)] * 3,
        compiler_params=_cp("parallel"),
    )(proj, proj, cw, cb, wa, ba, wx, bx, lam, hs, dhl)


def pool_bwd(proj, dpm, pw, pb, ps, l):
    T = proj.shape[0]
    _, G, gd, _ = pw.shape
    P = G * gd

    def body(u_ref, d_ref, w_ref, b_ref, s_ref, du_ref, dw_ref, db_ref, dsc_ref):
        for gi in range(G):
            cols = slice(gi * gd, (gi + 1) * gd)
            w = POOL_WINDOWS[gi]
            inv = _inv_count(T, w)
            ug = u_ref[:, cols].astype(F32)
            pooled = _pooled(ug, w, inv).astype(BF16)
            wb = w_ref[0, gi].astype(BF16)
            mixed = _dot(pooled, wb) + b_ref[0, :, cols]
            dpm_g = d_ref[:, cols].astype(F32)
            dsc_ref[0, :, cols] = jnp.sum(dpm_g * mixed, axis=0, keepdims=True)
            dmixed = dpm_g * s_ref[0, :, cols]
            db_ref[0, :, cols] = jnp.sum(dmixed, axis=0, keepdims=True)
            dmb = dmixed.astype(BF16)
            dw_ref[0, gi] = _dot_tn(pooled, dmb)
            dpooled = _dot_nt(dmb, wb)
            s = dpooled * inv
            k = 1
            while k < w:
                s = s + _shift_up(s, k)
                k *= 2
            du_ref[:, cols] = (s - dpooled).astype(BF16)

    vec = pl.BlockSpec((1, 1, P), lambda i: (l, 0, 0))
    ovec = pl.BlockSpec((1, 1, P), lambda i: (0, 0, 0))
    return _pallas_call(
        body, name="pool_bwd", grid=(1,),
        in_specs=[pl.BlockSpec((T, P), lambda i: (0, 0)), pl.BlockSpec((T, P), lambda i: (0, 0)),
                  pl.BlockSpec((1, G, gd, gd), lambda i: (l, 0, 0, 0)), vec, vec],
        out_specs=[pl.BlockSpec((T, P), lambda i: (0, 0)), pl.BlockSpec((1, G, gd, gd), lambda i: (0, 0, 0, 0)), ovec, ovec],
        out_shape=[jax.ShapeDtypeStruct((T, P), BF16), jax.ShapeDtypeStruct((1, G, gd, gd), F32),
                   jax.ShapeDtypeStruct((1, 1, P), F32), jax.ShapeDtypeStruct((1, 1, P), F32)],
        compiler_params=_cp("arbitrary"),
    )(proj, dpm, pw, pb, ps)


def _place():
    x, y, c = lax.axis_index("x"), lax.axis_index("y"), lax.axis_index("c")
    return x, y, c


HBM = pl.BlockSpec(memory_space=pltpu.HBM)
SEM = pl.BlockSpec(memory_space=pltpu.SEMAPHORE)
EFFECT = pltpu.SideEffectType.DATAFLOW_SIDE_EFFECTING


def _in_hbm(a):
    return pltpu.with_memory_space_constraint(a, pltpu.HBM)


def split_start(name, bufs, n_copies, copies_of, deps=()):
    nb = len(bufs)

    def body(*refs):
        buf = refs[:nb]
        send_sems, recv_sems = refs[nb + len(deps)], refs[nb + len(deps) + 1]
        token = refs[-1]
        for i, (src, dst, dev) in enumerate(copies_of(buf)):
            pltpu.make_async_remote_copy(src_ref=src, dst_ref=dst, send_sem=send_sems.at[i], recv_sem=recv_sems.at[i],
                                         device_id=dev, device_id_type=MESH).start()
        token[...] = jnp.zeros_like(token)

    outs = _pallas_call(
        body, name=name,
        in_specs=[HBM] * nb + [ANY] * len(deps),
        out_specs=(SEM, SEM, *([HBM] * nb), pl.BlockSpec(memory_space=pltpu.VMEM)),
        out_shape=(pltpu.SemaphoreType.DMA((n_copies,)), pltpu.SemaphoreType.DMA((n_copies,)),
                   *[pltpu.HBM(b.shape, b.dtype) for b in bufs], jax.ShapeDtypeStruct((8, 128), F32)),
        input_output_aliases={i: 2 + i for i in range(nb)},
        compiler_params=pltpu.CompilerParams(has_side_effects=EFFECT),
    )(*[_in_hbm(b) for b in bufs], *deps)
    return outs[0], outs[1], list(outs[2:2 + nb]), outs[-1]


def split_wait(name, bufs, send_sems, recv_sems, after, copies_of):
    nb = len(bufs)

    def body(*refs):
        buf = refs[:nb]
        send, recv = refs[nb], refs[nb + 1]
        for i, (src, dst, dev) in enumerate(copies_of(buf)):
            cp = pltpu.make_async_remote_copy(src_ref=src, dst_ref=dst, send_sem=send.at[i], recv_sem=recv.at[i],
                                              device_id=dev, device_id_type=MESH)
            cp.wait_send()
            cp.wait_recv()

    outs = _pallas_call(
        body, name=name,
        in_specs=[HBM] * nb + [SEM, SEM] + [ANY] * len(after),
        out_specs=[HBM] * nb,
        out_shape=[pltpu.HBM(b.shape, b.dtype) for b in bufs],
        input_output_aliases={i: i for i in range(nb)},
        compiler_params=pltpu.CompilerParams(has_side_effects=EFFECT),
    )(*bufs, send_sems, recv_sems, *after)
    return list(outs)


def place_own(w, l, place, dtype):
    _, rows, cols = w.shape
    tr = _rows_tile(rows, cols, 1 << 19)

    def body(p_ref, w_ref, o_ref):
        o_ref[0] = w_ref[0].astype(dtype)

    return _pallas_call(
        body, name="place_own",
        grid_spec=pltpu.PrefetchScalarGridSpec(
            num_scalar_prefetch=1, grid=(rows // tr,),
            in_specs=[pl.BlockSpec((1, tr, cols), lambda i, p: (l, i, 0))],
            out_specs=pl.BlockSpec((1, tr, cols), lambda i, p: (p[2], i, 0))),
        out_shape=jax.ShapeDtypeStruct((N_DEV, rows, cols), dtype), compiler_params=_cp("parallel"),
    )(place, w)


def _gather_copies(land):
    x, y, c = _place()
    k = 4 * x + 2 * y + c
    peers = [(x, 1 - y, c), (1 - x, y, c), (1 - x, 1 - y, c), (x, y, 1 - c)]
    return [(b.at[k], b.at[k], p) for p in peers for b in land]


def gather_start(name, land, deps=()):
    return split_start(name, land, 4 * len(land), _gather_copies, deps)


def gather_wait(name, land, send_sems, recv_sems, after):
    return split_wait(name, land, send_sems, recv_sems, after, _gather_copies)


def _forward_copies(land):
    x, y, c = _place()
    slots = [4 * px + 2 * py + c for px, py in [(x, 1 - y), (1 - x, y), (1 - x, 1 - y)]]
    return [(b.at[k], b.at[k], (x, y, 1 - c)) for k in slots for b in land]


def gather_forward_start(name, land, deps=()):
    return split_start(name, land, 3 * len(land), _forward_copies, deps)


def gather_forward_wait(name, land, send_sems, recv_sems, after):
    return split_wait(name, land, send_sems, recv_sems, after, _forward_copies)


def _spread_copies(land):
    x, y, c = _place()
    k = 4 * x + 2 * y + c
    peers = [(px, py, pc) for px in (x, 1 - x) for py in (y, 1 - y) for pc in (c, 1 - c)][1:]
    return [(b.at[k], b.at[k], p) for p in peers for b in land]


def _chip_copies(nsrc):
    def copies(buf):
        p16, recv2 = buf[:nsrc], buf[nsrc:]
        x, y, c = _place()
        out = []
        for d in (1, 2, 3):
            px = 1 - x if d & 2 else x
            py = 1 - y if d & 1 else y
            out += [(p16[a].at[:, 2 * px + py], recv2[a].at[:, d - 1], (px, py, c)) for a in range(nsrc)]
        return out
    return copies


def _pair_copies(nsrc):
    def copies(buf):
        g16, recv = buf[:nsrc], buf[nsrc:]
        x, y, c = _place()
        return [(g16[a].at[:, 2 * j + 1 - c], recv[a].at[:, j], (x, y, 1 - c)) for a in range(nsrc) for j in range(N_CHIP)]
    return copies


def pair_exchange_start(name, g16, deps=()):
    n = len(g16)
    land = [lax.empty((1, N_CHIP) + s.shape[2:], s.dtype) for s in g16]
    return split_start(name, list(g16) + land, N_CHIP * n, _pair_copies(n), deps)


def pair_exchange_wait(name, bufs, send_sems, recv_sems, after):
    n = len(bufs) // 2
    return split_wait(name, bufs, send_sems, recv_sems, after, _pair_copies(n))[n:]


def chip_exchange_start(name, pair16, deps=()):
    n = len(pair16)
    land = [lax.empty((s.shape[0], 3) + s.shape[2:], s.dtype) for s in pair16]
    return split_start(name, list(pair16) + land, 3 * n, _chip_copies(n), deps)


def chip_exchange_wait(name, bufs, send_sems, recv_sems, after):
    n = len(bufs) // 2
    return split_wait(name, bufs, send_sems, recv_sems, after, _chip_copies(n))[n:]


def _rows_tile(rows, cols, budget=1 << 20):
    t = rows
    while t % 2 == 0 and t * cols > budget and (t // 2) % 16 == 0:
        t //= 2
    return t


def pair_sum(g32, recv1, place, l):
    _, _, rows, cols = recv1.shape
    tr = _rows_tile(rows, cols)

    def body(p_ref, m_ref, r_ref, o_ref):
        o_ref[...] = (m_ref[...] + r_ref[...].astype(F32)).astype(o_ref.dtype)

    blk = pl.BlockSpec((1, 1, tr, cols), lambda j, i, p: (0, j, i, 0))
    return _pallas_call(
        body, name="pair_sum",
        grid_spec=pltpu.PrefetchScalarGridSpec(
            num_scalar_prefetch=1, grid=(N_CHIP, rows // tr),
            in_specs=[pl.BlockSpec((1, 1, tr, cols), lambda j, i, p: (l, 2 * j + p[0], i, 0)), blk], out_specs=blk),
        out_shape=jax.ShapeDtypeStruct(recv1.shape, recv1.dtype), compiler_params=_cp("parallel", "parallel"),
    )(place, g32, recv1)


def _grad_in_specs(tr, cols, l):
    return ([pl.BlockSpec((1, 1, tr, cols), lambda i, p: (l, p[2], i, 0)), pl.BlockSpec((1, 1, tr, cols), lambda i, p: (0, p[1], i, 0))]
            + [pl.BlockSpec((1, 1, tr, cols), lambda i, p, d=d: (0, d, i, 0)) for d in range(3)])


def _grad_total(o32, o16, r0, r1, r2):
    return (o32[0, 0] + o16[0, 0].astype(F32)) + r0[0, 0].astype(F32) + r1[0, 0].astype(F32) + r2[0, 0].astype(F32)


def grad_sum(g32, recv1, recv2, place):
    _, _, rows, cols = recv1.shape
    tr = _rows_tile(rows, cols)

    def body(p_ref, o32, o16, r0, r1, r2, g_ref):
        g_ref[...] = _grad_total(o32, o16, r0, r1, r2)

    return _pallas_call(
        body, name="grad_sum",
        grid_spec=pltpu.PrefetchScalarGridSpec(
            num_scalar_prefetch=1, grid=(rows // tr,), in_specs=_grad_in_specs(tr, cols, 0),
            out_specs=pl.BlockSpec((tr, cols), lambda i, p: (i, 0))),
        out_shape=jax.ShapeDtypeStruct((rows, cols), F32), compiler_params=_cp("parallel"),
    )(place, g32, recv1, recv2, recv2, recv2)


def _adamw_math(w, g, m, v):
    m = ADAM_B1 * m + (1.0 - ADAM_B1) * g
    v = ADAM_B2 * v + (1.0 - ADAM_B2) * (g * g)
    m_hat = m / (1.0 - ADAM_B1 ** ADAM_STEP)
    v_hat = v / (1.0 - ADAM_B2 ** ADAM_STEP)
    delta = -ADAM_LR * (m_hat / (jnp.sqrt(v_hat) + ADAM_EPS) + ADAM_WD * w)
    return delta, m, v


def grad_sum_adamw(g32, recv1, recv2, w, m, v, place, l, prev):
    L, rows, cols = w.shape
    tr = _rows_tile(rows, cols, 1 << 19)

    def body(p_ref, o32, o16, r0, r1, r2, w_ref, m_ref, v_ref, *rest):
        g_ref, d_ref, nm_ref, nv_ref = rest[-4:]
        g = _grad_total(o32, o16, r0, r1, r2)
        d, nm, nv = _adamw_math(w_ref[0], g, m_ref[0], v_ref[0])
        g_ref[0] = g
        d_ref[0] = d
        nm_ref[0] = nm
        nv_ref[0] = nv

    blk = pl.BlockSpec((1, tr, cols), lambda i, p: (l, i, 0))
    args = [g32, recv1, recv2, recv2, recv2, w, m, v]
    in_specs = _grad_in_specs(tr, cols, 0) + [blk] * 3
    aliases = {}
    if prev is not None:
        aliases = {1 + len(args) + k: k for k in range(4)}
        args += list(prev)
        in_specs += [ANY] * 4
    return _pallas_call(
        body, name="grad_sum_adamw",
        grid_spec=pltpu.PrefetchScalarGridSpec(num_scalar_prefetch=1, grid=(rows // tr,), in_specs=in_specs, out_specs=[blk] * 4),
        out_shape=[jax.ShapeDtypeStruct((L, rows, cols), F32)] * 4, input_output_aliases=aliases,
        compiler_params=_cp("parallel"),
    )(place, *args)


def adamw(w, g, m, v):
    rows, cols = w.shape
    tr = _rows_tile(rows, cols, 1 << 18)

    def body(w_ref, g_ref, m_ref, v_ref, d_ref, nm_ref, nv_ref):
        d, nm, nv = _adamw_math(w_ref[...], g_ref[...], m_ref[...], v_ref[...])
        d_ref[...] = d
        nm_ref[...] = nm
        nv_ref[...] = nv

    blk = pl.BlockSpec((tr, cols), lambda i: (i, 0))
    return _pallas_call(body, name="adamw_small", grid=(rows // tr,), in_specs=[blk] * 4, out_specs=[blk] * 3,
                        out_shape=[jax.ShapeDtypeStruct((rows, cols), F32)] * 3, compiler_params=_cp("parallel"))(w, g, m, v)


SMALL = ("norm_ffn1", "norm_mix", "pool_w", "pool_b", "pool_scale", "conv_w", "conv_b", "lru_w_a", "lru_b_a", "lru_w_x", "lru_b_x",
         "lru_lambda", "norm_ffn2", "final_norm")
BIG = ("ffn1_w_up", "ffn1_w_down", "w_in", "w_pool_up", "w_lru_up", "w_out", "ffn2_w_up", "ffn2_w_down")
NAMES = ("norm_ffn1", "ffn1_w_up", "ffn1_w_down", "norm_mix", "w_in", "pool_w", "pool_b", "pool_scale", "w_pool_up", "conv_w", "conv_b",
         "lru_w_a", "lru_b_a", "lru_w_x", "lru_b_x", "lru_lambda", "w_lru_up", "w_out", "norm_ffn2", "ffn2_w_up", "ffn2_w_down", "final_norm")
SUBLAYERS = (("ffn1_w_up", "ffn1_w_down"), ("w_in", "w_pool_up", "w_lru_up", "w_out", "conv_w"), ("ffn2_w_up", "ffn2_w_down"))
PACK_ROWS = 16 * N_DEV


def _pack(parts):
    flat = jnp.concatenate([p.reshape(-1) for p in parts])
    unit = 128 * PACK_ROWS
    padded = -(-flat.size // unit) * unit
    return jnp.pad(flat, (0, padded - flat.size)).reshape(-1, 128)


def _unpack(packed, shapes):
    flat = packed.reshape(-1)
    out, off = [], 0
    for s in shapes:
        n = 1
        for d in s:
            n *= d
        out.append(flat[off:off + n].reshape(s))
        off += n
    return out


def kernel(x, norm_ffn1, ffn1_w_up, ffn1_w_down, norm_mix, w_in, pool_w, pool_b, pool_scale, w_pool_up, conv_w, conv_b, lru_w_a, lru_b_a, lru_w_x, lru_b_x, lru_lambda, w_lru_up, w_out, norm_ffn2, ffn2_w_up, ffn2_w_down, final_norm, loss_target, m_norm_ffn1, m_ffn1_w_up, m_ffn1_w_down, m_norm_mix, m_w_in, m_pool_w, m_pool_b, m_pool_scale, m_w_pool_up, m_conv_w, m_conv_b, m_lru_w_a, m_lru_b_a, m_lru_w_x, m_lru_b_x, m_lru_lambda, m_w_lru_up, m_w_out, m_norm_ffn2, m_ffn2_w_up, m_ffn2_w_down, m_final_norm, v_norm_ffn1, v_ffn1_w_up, v_ffn1_w_down, v_norm_mix, v_w_in, v_pool_w, v_pool_b, v_pool_scale, v_w_pool_up, v_conv_w, v_conv_b, v_lru_w_a, v_lru_b_a, v_lru_w_x, v_lru_b_x, v_lru_lambda, v_w_lru_up, v_w_out, v_norm_ffn2, v_ffn2_w_up, v_ffn2_w_down, v_final_norm):
    W = dict(norm_ffn1=norm_ffn1, ffn1_w_up=ffn1_w_up, ffn1_w_down=ffn1_w_down, norm_mix=norm_mix, w_in=w_in, pool_w=pool_w, pool_b=pool_b,
             pool_scale=pool_scale, w_pool_up=w_pool_up, conv_w=conv_w, conv_b=conv_b, lru_w_a=lru_w_a, lru_b_a=lru_b_a, lru_w_x=lru_w_x,
             lru_b_x=lru_b_x, lru_lambda=lru_lambda, w_lru_up=w_lru_up, w_out=w_out, norm_ffn2=norm_ffn2, ffn2_w_up=ffn2_w_up,
             ffn2_w_down=ffn2_w_down, final_norm=final_norm)
    M = dict(norm_ffn1=m_norm_ffn1, ffn1_w_up=m_ffn1_w_up, ffn1_w_down=m_ffn1_w_down, norm_mix=m_norm_mix, w_in=m_w_in, pool_w=m_pool_w,
             pool_b=m_pool_b, pool_scale=m_pool_scale, w_pool_up=m_w_pool_up, conv_w=m_conv_w, conv_b=m_conv_b, lru_w_a=m_lru_w_a,
             lru_b_a=m_lru_b_a, lru_w_x=m_lru_w_x, lru_b_x=m_lru_b_x, lru_lambda=m_lru_lambda, w_lru_up=m_w_lru_up, w_out=m_w_out,
             norm_ffn2=m_norm_ffn2, ffn2_w_up=m_ffn2_w_up, ffn2_w_down=m_ffn2_w_down, final_norm=m_final_norm)
    V = dict(norm_ffn1=v_norm_ffn1, ffn1_w_up=v_ffn1_w_up, ffn1_w_down=v_ffn1_w_down, norm_mix=v_norm_mix, w_in=v_w_in, pool_w=v_pool_w,
             pool_b=v_pool_b, pool_scale=v_pool_scale, w_pool_up=v_w_pool_up, conv_w=v_conv_w, conv_b=v_conv_b, lru_w_a=v_lru_w_a,
             lru_b_a=v_lru_b_a, lru_w_x=v_lru_w_x, lru_b_x=v_lru_b_x, lru_lambda=v_lru_lambda, w_lru_up=v_w_lru_up, w_out=v_w_out,
             norm_ffn2=v_norm_ffn2, ffn2_w_up=v_ffn2_w_up, ffn2_w_down=v_ffn2_w_down, final_norm=v_final_norm)

    for S in (W, M, V):
        for n in ("ffn1_w_up", "ffn2_w_up"):
            S[n] = jnp.swapaxes(S[n], 1, 2)

    T, D = x.shape[1], x.shape[2]
    L = norm_ffn1.shape[0]
    P = pool_scale.shape[1]
    R = lru_lambda.shape[1]
    H, hd = lru_w_a.shape[1], lru_w_a.shape[2]
    CW = conv_w.shape[1]
    cs = ffn1_w_up.shape[2]
    ci = w_in.shape[2]
    xin = x.reshape(T, D)
    tgt = loss_target.reshape(T, D)
    dev = 4 * lax.axis_index("x") + 2 * lax.axis_index("y") + lax.axis_index("c")
    place = jnp.stack([lax.axis_index("c"), 2 * lax.axis_index("x") + lax.axis_index("y"), dev]).astype(jnp.int32)

    cw_flat = conv_w.reshape(L, -1)
    cw_pad = (-cw_flat.shape[1]) % 1024
    cw_tiles = jnp.pad(cw_flat, ((0, 0), (0, cw_pad))).reshape(L, -1, 128)

    def units(l):
        return SUBLAYERS if l == 0 else (tuple(n for u in SUBLAYERS for n in u),)

    queued = {"gather": (), "pair": (), "chip": ()}

    gathering = []

    def gather_units_start(l):
        for k, names in enumerate(SUBLAYERS):
            land = [place_own(cw_tiles, l, place, F32) if n == "conv_w" else place_own(W[n], l, place, BF16) for n in names]
            send_sems, recv_sems, land, tok = gather_start(f"gather_start_l{l}_u{k}", land, queued["gather"])
            gathering.append(dict(names=names, tag=f"l{l}_u{k}", send=send_sems, recv=recv_sems, land=land, tok=tok, arrived=False))
            queued["gather"] = (tok,)

    def gather_unit_arrive(after):
        waiting = [u for u in gathering if not u["arrived"]]
        if not waiting:
            return ()
        unit, tokens = waiting[0], [u["tok"] for u in waiting[1:]]
        land = gather_wait(f"gather_wait_{unit['tag']}", unit["land"], unit["send"], unit["recv"], list(after) + tokens)
        send_sems, recv_sems, land, tok = gather_forward_start(f"gather_pass_start_{unit['tag']}", land)
        unit.update(land=land, send=send_sems, recv=recv_sems, tok=tok, arrived=True)
        return (tok,)

    def gather_unit_weights(after):
        if not gathering[0]["arrived"]:
            gather_unit_arrive(after)
        unit = gathering.pop(0)
        land = gather_forward_wait(f"gather_pass_wait_{unit['tag']}", unit["land"], unit["send"], unit["recv"], after)
        g = dict(zip(unit["names"], land))
        one = lambda a: a.reshape((1,) + a.shape)
        w = {}
        for tag_, up, dn in (("1", "ffn1_w_up", "ffn1_w_down"), ("2", "ffn2_w_up", "ffn2_w_down")):
            if up in g:
                w["wup" + tag_], w["wd" + tag_] = one(g[up]), g[dn].reshape(1, -1, D)
        if "w_in" in g:
            cw_l = g["conv_w"].reshape(N_DEV, -1)[:, :cw_flat.shape[1]].reshape((N_DEV,) + conv_w.shape[1:])
            w.update(win=one(g["w_in"]), wlu=g["w_lru_up"].reshape(1, R, D), wout=g["w_out"].reshape(1, D, D),
                     wpu=g["w_pool_up"].transpose(1, 0, 2).reshape(1, P, D),
                     cw=cw_l.transpose(1, 0, 2).reshape(1, CW, R))
        return w

    vec = lambda a: a.reshape(L, 1, -1)
    p = dict(g1=vec(norm_ffn1), gm=vec(norm_mix), g2=vec(norm_ffn2), pb=vec(pool_b), ps=vec(pool_scale), cb=vec(conv_b),
             ba=vec(lru_b_a), bx=vec(lru_b_x), lam=vec(lru_lambda), pw=pool_w, wa=lru_w_a, wx=lru_w_x)

    AHEAD = 2
    for l in range(min(AHEAD, L)):
        gather_units_start(l)
    saved, LW = [], []
    xc = xin
    for l in range(L):
        w = gather_unit_weights([xc])
        if l + AHEAD < L:
            gather_units_start(l + AHEAD)
        sv = {"x1": xc}
        sv["h1"], sv["u1"], sv["s1"] = ffn_up(xc, p["g1"], w["wup1"], l)
        xc = ffn_down(sv["s1"], w["wd1"], xc, l, gather_unit_arrive([sv["s1"]]))
        sv["x2"] = xc
        w.update(gather_unit_weights([xc]))
        sv["h2"], sv["proj"] = mix_in(xc, p["gm"], w["win"], l)
        sv["pm"] = pool_fwd(sv["proj"], p["pw"], p["pb"], p["ps"], l)
        sv["hl"], sv["hs"] = lru_fwd(sv["proj"], w["cw"], p["cb"], p["wa"], p["ba"], p["wx"], p["bx"], p["lam"], P, l)
        xc, sv["yp"], sv["yl"], sv["z"] = mix_out(sv["pm"], sv["hl"], sv["proj"], xc, w["wpu"], w["wlu"], w["wout"], P, l,
                                                  gather_unit_arrive([sv["hl"]]))
        sv["x3"] = xc
        w.update(gather_unit_weights([xc]))
        sv["h3"], sv["u3"], sv["s3"] = ffn_up(xc, p["g2"], w["wup2"], l)
        xc = ffn_down(sv["s3"], w["wd2"], xc, l, gather_unit_arrive([sv["s3"]]))
        saved.append(sv)
        LW.append(w)

    loss_part, dx, d_final = loss_head(xc, final_norm.reshape(1, D), tgt)
    loss = lax.psum(loss_part[0, 0], ("x", "y", "c"))

    G = [dict() for _ in range(L)]
    small = {n: [None] * L for n in SMALL if n != "final_norm"}

    def to_slots(name, pair):
        if name == "w_pool_up":
            return tuple(a.reshape(1, P, N_DEV, D // N_DEV).transpose(0, 2, 1, 3) for a in pair)
        return tuple(a.reshape((1, N_DEV) + W[name].shape[1:]) for a in pair)

    def ffn_bwd(dy, sv, tag, wup, wd, gn, up_name, dn_name, norm_name, l, deps=()):
        dout, du = ffn_down_bwd(dy, wd, sv["u" + tag], l, deps)
        du = du.reshape(N_DEV, T, cs)
        G[l][dn_name] = to_slots(dn_name, dw_tn("dw_down", sv["s" + tag], lambda tk: pl.BlockSpec((1, tk, cs), lambda g, k: (g, k, 0)),
                                                dout, lambda tk: pl.BlockSpec((tk, D), lambda g, k: (k, 0)), 4, cs, D, T))
        G[l][up_name] = to_slots(up_name, dw_tn("dw_up", du, lambda tk: pl.BlockSpec((1, tk, cs), lambda g, k: (g, k, 0)),
                                                sv["h" + tag], lambda tk: pl.BlockSpec((tk, D), lambda g, k: (k, 0)), N_DEV, cs, D, T))
        dxn, dg = dx_norm_bwd("ffn_dx", du, lambda tm: pl.BlockSpec((1, tm, cs), lambda j, i: (j, i, 0)), wup, N_DEV,
                              sv["x" + tag], gn, dy, l, w_transposed=True)
        small[norm_name][l] = dg.reshape(D)
        return dxn

    pairing, in_flight = [], []

    def reduce_start(l, names, tag):
        names = [n for n in names if n != "conv_w"]
        send_sems, recv_sems, bufs, tok = pair_exchange_start(f"rs_pair_start_{tag}", [G[l][n][1] for n in names], queued["pair"])
        pairing.append((l, names, tag, send_sems, recv_sems, bufs))
        queued["pair"] = (tok,)
        return (tok,)

    def reduce_continue(after):
        l, names, tag, send_sems, recv_sems, bufs = pairing.pop(0)
        recv1 = pair_exchange_wait(f"rs_pair_wait_{tag}", bufs, send_sems, recv_sems, after)
        pair16 = [pair_sum(G[l][n][0], r_, place, 0) for n, r_ in zip(names, recv1)]
        send_sems, recv_sems, bufs, tok = chip_exchange_start(f"rs_chip_start_{tag}", pair16, queued["chip"])
        in_flight.append((l, names, tag, send_sems, recv_sems, bufs, recv1))
        queued["chip"] = (tok,)
        return (tok,)

    def boundary(l, k, dx_now):
        deps = reduce_continue([dx_now]) if pairing else ()
        if len(units(l)) > 1:
            deps += reduce_start(l, units(l)[k], f"l{l}_u{k}")
        elif k == 0:
            deps += reduce_start(l, units(l)[0], f"l{l}_u0")
        return deps

    deps = ()
    for l in reversed(range(L)):
        sv, w = saved[l], LW[l]
        dx = ffn_bwd(dx, sv, "3", w["wup2"], w["wd2"], p["g2"], "ffn2_w_up", "ffn2_w_down", "norm_ffn2", l, deps)
        deps = boundary(l, 2, dx)
        dyb, dyp, dyl, dgp, dgl, dpm, dhl = mix_out_bwd(dx, sv["proj"], sv["yp"], sv["yl"], w["wpu"], w["wlu"], w["wout"], P, R, l, deps)
        row = lambda wd_: (lambda tk: pl.BlockSpec((tk, wd_), lambda g, k: (k, 0)))
        G[l]["w_out"] = to_slots("w_out", dw_tn("dw_out", sv["z"], row(D), dyb, row(D), 1, D, D, T))
        G[l]["w_lru_up"] = to_slots("w_lru_up", dw_tn("dw_lru_up", sv["hl"], row(R), dyl, row(D), 1, R, D, T))
        G[l]["w_pool_up"] = to_slots("w_pool_up", dw_tn("dw_pool_up", sv["pm"], row(P), dyp, row(D), 1, P, D, T))
        du_lru, du_gelu, dcw, dcb, dwa, dba, dwx, dbx, dlam = lru_bwd(
            sv["proj"], sv["hs"], dhl, w["cw"], p["cb"], p["wa"], p["ba"], p["wx"], p["bx"], p["lam"], P, l)
        du_pool, dpw, dpb, dpsc = pool_bwd(sv["proj"], dpm, p["pw"], p["pb"], p["ps"], l)
        dproj = jnp.concatenate([du_pool, du_lru, du_gelu, dgp, dgl], axis=1)
        G[l]["w_in"] = to_slots("w_in", dw_tn("dw_in", sv["h2"], row(D), dproj, lambda tk: pl.BlockSpec((tk, ci), lambda g, k: (k, g)),
                                              N_DEV, D, ci, T))
        dx, dgm = dx_norm_bwd("mix_dx", dproj, lambda tm: pl.BlockSpec((tm, ci), lambda j, i: (i, j)), w["win"], N_DEV,
                              sv["x2"], p["gm"], dx, l)
        small["norm_mix"][l] = dgm.reshape(D)
        small["pool_w"][l], small["pool_b"][l], small["pool_scale"][l] = dpw[0], dpb.reshape(pool_b.shape[1:]), dpsc.reshape(P)
        small["conv_w"][l], small["conv_b"][l] = dcw[0], dcb.reshape(R)
        small["lru_w_a"][l], small["lru_b_a"][l] = dwa[0], dba.reshape(H, hd)
        small["lru_w_x"][l], small["lru_b_x"][l] = dwx[0], dbx.reshape(H, hd)
        small["lru_lambda"][l] = dlam.reshape(R)
        deps = boundary(l, 1, dx)
        dx = ffn_bwd(dx, sv, "1", w["wup1"], w["wd1"], p["g1"], "ffn1_w_up", "ffn1_w_down", "norm_ffn1", l, deps)
        deps = boundary(l, 0, dx)

    grad_x = dx.reshape(x.shape)

    small_parts = [jnp.stack(small[n]) for n in SMALL if n != "final_norm"] + [d_final.reshape(D)]
    small_shapes = [p.shape for p in small_parts]
    gpack = _pack(small_parts).reshape(1, N_DEV, -1, 128)
    small_pair = pair_exchange_start("rs_pair_start_small", [gpack], queued["pair"])
    while pairing:
        reduce_continue([dx])

    outs = {n: None for n in BIG}
    after = [dx]
    for k, (l, names, tag, send_sems, recv_sems, bufs, recv1) in enumerate(in_flight):
        recv2 = chip_exchange_wait(f"rs_chip_wait_{tag}", bufs, send_sems, recv_sems, after)
        for i, n in enumerate(names):
            outs[n] = grad_sum_adamw(G[l][n][0], recv1[i], recv2[i], W[n], M[n], V[n], place, l, outs[n])
        after = [outs[n][0] for n in names]
        if k == 0:
            recv1_s = pair_exchange_wait("rs_pair_wait_small", small_pair[2], small_pair[0], small_pair[1], after)[0]
            pair_s = pair_sum(gpack, recv1_s, place, 0)
            small_chip = chip_exchange_start("rs_chip_start_small", [pair_s], queued["chip"])
    for n in ("ffn1_w_up", "ffn2_w_up"):
        outs[n] = [jnp.swapaxes(o, 1, 2) for o in outs[n]]
    out_g, out_d, out_m, out_v = ({n: outs[n][k] for n in BIG} for k in range(4))

    recv2_s = chip_exchange_wait("rs_chip_wait_small", small_chip[2], small_chip[0], small_chip[1], after)[0]
    gs = grad_sum(gpack, recv1_s, recv2_s, place)
    gs_slots = place_own(gs.reshape((1,) + gs.shape), 0, place, F32)
    send_sems, recv_sems, gs_land, _tok = split_start("spread_start_small", [gs_slots], N_DEV - 1, _spread_copies)
    gs_all = split_wait("spread_wait_small", gs_land, send_sems, recv_sems, [gs], _spread_copies)[0]
    gs_all = gs_all.reshape(-1, 128)
    small_g = dict(zip(SMALL, _unpack(gs_all, small_shapes)))
    full_shapes = [W[n].shape if n != "conv_w" else small_shapes[SMALL.index("conv_w")] for n in SMALL]
    rep = [n for n in SMALL if n != "conv_w"]
    rep_shapes = [W[n].shape for n in rep]
    wp, mp, vp = (_pack([S[n] for n in rep]) for S in (W, M, V))
    gp = _pack([small_g[n] for n in rep])
    dp, nmp, nvp = adamw(wp, gp, mp, vp)
    for S, packed in ((out_d, dp), (out_m, nmp), (out_v, nvp)):
        S.update(zip(rep, _unpack(packed, rep_shapes)))
    for n in rep:
        out_g[n] = small_g[n]
    cwc = conv_w.shape[2]
    gcw = lax.dynamic_slice_in_dim(small_g["conv_w"], dev * cwc, cwc, axis=2)
    cw2 = lambda a: a.reshape(-1, cwc)
    pad_rows = (-cw2(conv_w).shape[0]) % 8
    padr = lambda a: jnp.pad(cw2(a), ((0, pad_rows), (0, 0)))
    dcw_, mcw_, vcw_ = adamw(padr(conv_w), padr(gcw), padr(M["conv_w"]), padr(V["conv_w"]))
    nrow = cw2(conv_w).shape[0]
    out_g["conv_w"] = gcw
    out_d["conv_w"], out_m["conv_w"], out_v["conv_w"] = (a[:nrow].reshape(conv_w.shape) for a in (dcw_, mcw_, vcw_))
    del full_shapes

    return (loss, grad_x, *[out_g[n] for n in NAMES], *[out_d[n] for n in NAMES], *[out_m[n] for n in NAMES], *[out_v[n] for n in NAMES])
```

```python
import functools

import jax
import jax.numpy as jnp
from jax import lax
from jax.experimental import pallas as pl
from jax.experimental.pallas import tpu as pltpu

F32, BF16 = jnp.float32, jnp.bfloat16
EPS = 1e-6
LRU_C = 8.0
POOL_WINDOWS = (2, 4, 8, 16)
ADAM_LR, ADAM_B1, ADAM_B2, ADAM_EPS, ADAM_WD, ADAM_STEP = 0.001, 0.9, 0.999, 1e-08, 0.01, 10
N_DEV = 8
N_CHIP = 4
MESH = pl.DeviceIdType.MESH
V7X_VMEM_LIMIT = 56 * 1024 * 1024
ROW_TILE = 512
WIDE_TILE = 1024
SUM_TILE = 2048
ANY = pl.BlockSpec(memory_space=pl.ANY)

_pallas_call = pl.pallas_call


def _cp(*sem):
    return pltpu.CompilerParams(dimension_semantics=sem if sem else None, vmem_limit_bytes=V7X_VMEM_LIMIT)


def _tile(n, t):
    t = min(n, t)
    assert n % t == 0, (n, t)
    return t


def _dot(a, b):
    return jnp.dot(a, b, preferred_element_type=F32)


def _dot_nt(a, b):
    return lax.dot_general(a, b, (((1,), (1,)), ((), ())), preferred_element_type=F32)


def _dot_tn(a, b):
    return lax.dot_general(a, b, (((0,), (0,)), ((), ())), preferred_element_type=F32)


def _rms(xv):
    r = lax.rsqrt(jnp.mean(xv * xv, axis=-1, keepdims=True) + EPS)
    return xv * r, r


def _rms_bwd(dh, xv, gv, dy):
    n, r = _rms(xv)
    dn = dh * gv
    dx = dy + r * (dn - n * jnp.mean(dn * n, axis=-1, keepdims=True))
    return dx, jnp.sum(dh * n, axis=0, keepdims=True)


def _shift_down(x, k, fill=0.0):
    if k == 0:
        return x
    rows = lax.broadcasted_iota(jnp.int32, x.shape, 0)
    return jnp.where(rows >= k, pltpu.roll(x, k, 0), fill)


def _shift_up(x, k, fill=0.0):
    if k == 0:
        return x
    n = x.shape[0]
    rows = lax.broadcasted_iota(jnp.int32, x.shape, 0)
    return jnp.where(rows < n - k, pltpu.roll(x, n - k, 0), fill)


def _sigmoid(x):
    return 0.5 * jnp.tanh(0.5 * x) + 0.5


_GELU_K = 0.7978845608028654
_GELU_C = 0.044715


def _gelu(x):
    th = jnp.tanh(_GELU_K * (x + _GELU_C * x * x * x))
    return 0.5 * x * (1.0 + th), th


def _gelu_grad(x, th):
    return 0.5 * (1.0 + th) + 0.5 * x * (1.0 - th * th) * _GELU_K * (1.0 + 3.0 * _GELU_C * x * x)


def ffn_up(x, g, wup, l):
    T, D = x.shape
    cs = wup.shape[-2]
    tm = _tile(T, WIDE_TILE)
    ni = T // tm

    def body(x_ref, g_ref, wa_ref, wb_ref, h_ref, u_ref, s_ref, hs_ref):
        rows = pl.ds(pl.multiple_of(pl.program_id(1) * tm, tm), tm)

        @pl.when(pl.program_id(0) == 0)
        def _():
            n, _r = _rms(x_ref[...])
            hv = (n * g_ref[0]).astype(BF16)
            hs_ref[rows, :] = hv
            h_ref[...] = hv

        hv = hs_ref[rows, :]
        a = _dot_nt(hv, wa_ref[0, 0])
        b = _dot_nt(hv, wb_ref[0, 0])
        u_ref[0, 0] = a.astype(BF16)
        u_ref[1, 0] = b.astype(BF16)
        s_ref[0] = (a * _sigmoid(a) * b).astype(BF16)

    first = lambda j, i: (jnp.where(j == 0, i, ni - 1), 0)
    return _pallas_call(
        body, name="ffn_up", grid=(4, ni),
        in_specs=[pl.BlockSpec((tm, D), first), pl.BlockSpec((1, 1, D), lambda j, i: (l, 0, 0)),
                  pl.BlockSpec((1, 1, cs, D), lambda j, i: (0, j, 0, 0)), pl.BlockSpec((1, 1, cs, D), lambda j, i: (0, j + 4, 0, 0))],
        out_specs=[pl.BlockSpec((tm, D), first), pl.BlockSpec((2, 1, tm, cs), lambda j, i: (0, j, i, 0)),
                   pl.BlockSpec((1, tm, cs), lambda j, i: (j, i, 0))],
        out_shape=[jax.ShapeDtypeStruct((T, D), BF16), jax.ShapeDtypeStruct((2, 4, T, cs), BF16), jax.ShapeDtypeStruct((4, T, cs), BF16)],
        scratch_shapes=[pltpu.VMEM((T, D), BF16)],
        compiler_params=_cp("arbitrary", "arbitrary"),
    )(x, g, wup, wup)


def ffn_down(s, wd, x, l, deps=()):
    _, T, cs = s.shape
    D = x.shape[1]
    tm = _tile(T, WIDE_TILE)

    def body(s_ref, w_ref, x_ref, *rest):
        o_ref, acc_ref = rest[len(deps):]
        j = pl.program_id(1)

        @pl.when(j == 0)
        def _():
            acc_ref[...] = jnp.zeros_like(acc_ref)

        acc_ref[...] += _dot(s_ref[0], w_ref[0])

        @pl.when(j == 3)
        def _():
            o_ref[...] = x_ref[...] + 0.5 * acc_ref[...]

    return _pallas_call(
        body, name="ffn_down", grid=(T // tm, 4),
        in_specs=[pl.BlockSpec((1, tm, cs), lambda i, j: (j, i, 0)), pl.BlockSpec((1, cs, D), lambda i, j: (0, j, 0)),
                  pl.BlockSpec((tm, D), lambda i, j: (i, 0))] + [ANY] * len(deps),
        out_specs=pl.BlockSpec((tm, D), lambda i, j: (i, 0)),
        out_shape=jax.ShapeDtypeStruct((T, D), F32),
        scratch_shapes=[pltpu.VMEM((tm, D), F32)],
        compiler_params=_cp("parallel", "arbitrary"),
    )(s, wd, x, *deps)


def mix_in(x, g, win, l):
    T, D = x.shape
    ci = win.shape[-1]
    tm = _tile(T, WIDE_TILE)
    ni = T // tm

    def body(x_ref, g_ref, w_ref, h_ref, p_ref, hs_ref):
        rows = pl.ds(pl.multiple_of(pl.program_id(1) * tm, tm), tm)

        @pl.when(pl.program_id(0) == 0)
        def _():
            n, _r = _rms(x_ref[...])
            hv = (n * g_ref[0]).astype(BF16)
            hs_ref[rows, :] = hv
            h_ref[...] = hv

        p_ref[...] = _dot(hs_ref[rows, :], w_ref[0, 0]).astype(BF16)

    first = lambda j, i: (jnp.where(j == 0, i, ni - 1), 0)
    return _pallas_call(
        body, name="mix_in", grid=(N_DEV, ni),
        in_specs=[pl.BlockSpec((tm, D), first), pl.BlockSpec((1, 1, D), lambda j, i: (l, 0, 0)),
                  pl.BlockSpec((1, 1, D, ci), lambda j, i: (0, j, 0, 0))],
        out_specs=[pl.BlockSpec((tm, D), first), pl.BlockSpec((tm, ci), lambda j, i: (i, j))],
        out_shape=[jax.ShapeDtypeStruct((T, D), BF16), jax.ShapeDtypeStruct((T, N_DEV * ci), BF16)],
        scratch_shapes=[pltpu.VMEM((T, D), BF16)],
        compiler_params=_cp("arbitrary", "arbitrary"),
    )(x, g, win)


def _inv_count(T, w):
    t = lax.broadcasted_iota(jnp.int32, (T, 1), 0)
    return 1.0 / jnp.minimum(t + 1, w).astype(F32)


def _pooled(ug, w, inv):
    s = ug
    k = 1
    while k < w:
        s = s + _shift_down(s, k)
        k *= 2
    return s * inv - ug


def pool_fwd(proj, pw, pb, ps, l):
    T = proj.shape[0]
    _, G, gd, _ = pw.shape
    P = G * gd

    def body(u_ref, w_ref, b_ref, s_ref, o_ref):
        for gi in range(G):
            cols = slice(gi * gd, (gi + 1) * gd)
            ug = u_ref[:, cols].astype(F32)
            pooled = _pooled(ug, POOL_WINDOWS[gi], _inv_count(T, POOL_WINDOWS[gi]))
            mixed = _dot(pooled.astype(BF16), w_ref[0, gi].astype(BF16)) + b_ref[0, :, cols]
            o_ref[:, cols] = (mixed * s_ref[0, :, cols]).astype(BF16)

    return _pallas_call(
        body, name="pool_fwd", grid=(1,),
        in_specs=[pl.BlockSpec((T, P), lambda i: (0, 0)), pl.BlockSpec((1, G, gd, gd), lambda i: (l, 0, 0, 0)),
                  pl.BlockSpec((1, 1, P), lambda i: (l, 0, 0)), pl.BlockSpec((1, 1, P), lambda i: (l, 0, 0))],
        out_specs=pl.BlockSpec((T, P), lambda i: (0, 0)),
        out_shape=jax.ShapeDtypeStruct((T, P), BF16),
        compiler_params=_cp("arbitrary"),
    )(proj, pw, pb, ps)


def _conv(u, cw_ref, cb):
    CW = cw_ref.shape[1]
    v = cb
    for k in range(CW):
        v = v + cw_ref[0, k:k + 1, :] * _shift_down(u, CW - 1 - k)
    return v


def _softplus(z):
    return jnp.maximum(z, 0.0) + jnp.log1p(jnp.exp(-jnp.abs(z)))


def _lru_gates(v, wa_ref, ba, wx_ref, bx, lam):
    vb = v.astype(BF16)
    r = _sigmoid(_dot(vb, wa_ref[0, 0].astype(BF16)) + ba)
    i = _sigmoid(_dot(vb, wx_ref[0, 0].astype(BF16)) + bx)
    sp = _softplus(-lam)
    log_a = -LRU_C * r * sp
    a = jnp.exp(log_a)
    m2 = -jnp.tanh(log_a) * (a * a + 1.0)
    inv_mult = lax.rsqrt(m2)
    mult = jnp.where(m2 > 0.0, m2 * inv_mult, 0.0)
    return r, i, sp, a, mult, inv_mult


def _scan_fwd(a_ref, b_ref, o_ref):
    T, W = a_ref.shape
    rows = lax.broadcasted_iota(jnp.int32, (8, W), 0)

    def step(t, carry):
        r0 = pl.multiple_of(t * 8, 8)
        A = a_ref[pl.ds(r0, 8), :]
        B = b_ref[pl.ds(r0, 8), :]
        for s in (1, 2, 4):
            keep = rows >= s
            As = jnp.where(keep, pltpu.roll(A, s, 0), 1.0)
            Bs = jnp.where(keep, pltpu.roll(B, s, 0), 0.0)
            B = A * Bs + B
            A = A * As
        h = B + A * carry
        o_ref[pl.ds(r0, 8), :] = h
        return jnp.broadcast_to(h[7:8, :], (8, W))

    lax.fori_loop(0, T // 8, step, jnp.zeros((8, W), F32), unroll=8)


def _scan_bwd(a_ref, b_ref, o_ref):
    T, W = a_ref.shape
    rows = lax.broadcasted_iota(jnp.int32, (8, W), 0)
    nt = T // 8

    def step(t, carry):
        r0 = pl.multiple_of((nt - 1 - t) * 8, 8)
        A = a_ref[pl.ds(r0, 8), :]
        B = b_ref[pl.ds(r0, 8), :]
        for s in (1, 2, 4):
            keep = rows < 8 - s
            As = jnp.where(keep, pltpu.roll(A, 8 - s, 0), 1.0)
            Bs = jnp.where(keep, pltpu.roll(B, 8 - s, 0), 0.0)
            B = A * Bs + B
            A = A * As
        y = B + A * carry
        o_ref[pl.ds(r0, 8), :] = y
        return jnp.broadcast_to(y[0:1, :], (8, W))

    lax.fori_loop(0, nt, step, jnp.zeros((8, W), F32), unroll=8)


def _lru_specs(T, hd, P, R, CW, l):
    ob, gb = P // hd, (P + R) // hd
    vec = pl.BlockSpec((1, 1, hd), lambda h: (l, 0, h))
    mat = pl.BlockSpec((1, 1, hd, hd), lambda h: (l, h, 0, 0))
    return [pl.BlockSpec((T, hd), lambda h: (0, ob + h)), pl.BlockSpec((T, hd), lambda h: (0, gb + h)),
            pl.BlockSpec((1, CW, hd), lambda h: (0, 0, h)), vec, mat, vec, mat, vec, vec]


def lru_fwd(proj, cw, cb, wa, ba, wx, bx, lam, P, l):
    T = proj.shape[0]
    _, H, hd, _ = wa.shape
    R = H * hd
    CW = cw.shape[1]
    assert P % hd == 0 and T % 8 == 0

    def body(u_ref, ug_ref, cw_ref, cb_ref, wa_ref, ba_ref, wx_ref, bx_ref, lam_ref, hl_ref, hs_ref, a_s, b_s):
        v = _conv(u_ref[...].astype(F32), cw_ref, cb_ref[0])
        _r, i, _sp, a, mult, _im = _lru_gates(v, wa_ref, ba_ref[0], wx_ref, bx_ref[0], lam_ref[0])
        a_s[...] = a
        b_s[...] = mult * (i * v)
        _scan_fwd(a_s, b_s, hs_ref)
        ge, _th = _gelu(ug_ref[...].astype(F32))
        hl_ref[...] = (hs_ref[...] * ge).astype(BF16)

    out = pl.BlockSpec((T, hd), lambda h: (0, h))
    return _pallas_call(
        body, name="lru_fwd", grid=(H,),
        in_specs=_lru_specs(T, hd, P, R, CW, l),
        out_specs=[out, out],
        out_shape=[jax.ShapeDtypeStruct((T, R), BF16), jax.ShapeDtypeStruct((T, R), F32)],
        scratch_shapes=[pltpu.VMEM((T, hd), F32)] * 2,
        compiler_params=_cp("parallel"),
    )(proj, proj, cw, cb, wa, ba, wx, bx, lam)


def mix_out(pm, hl, proj, x, wpu, wlu, wout, P, l, deps=()):
    T, D = x.shape
    R = hl.shape[1]
    tm = _tile(T, ROW_TILE)
    assert (P + 2 * R) % D == 0
    gb = (P + 2 * R) // D

    def body(pm_ref, hl_ref, gp_ref, gl_ref, x_ref, wpu_ref, wlu_ref, wo_ref, *rest):
        o_ref, yp_ref, yl_ref, z_ref = rest[len(deps):]
        yp = _dot(pm_ref[...], wpu_ref[0])
        yl = _dot(hl_ref[...], wlu_ref[0])
        z = (_sigmoid(gp_ref[...].astype(F32)) * yp + _sigmoid(gl_ref[...].astype(F32)) * yl).astype(BF16)
        yp_ref[...] = yp.astype(BF16)
        yl_ref[...] = yl.astype(BF16)
        z_ref[...] = z
        o_ref[...] = x_ref[...] + _dot(z, wo_ref[0])

    row = lambda w: pl.BlockSpec((tm, w), lambda i: (i, 0))
    return _pallas_call(
        body, name="mix_out", grid=(T // tm,),
        in_specs=[row(P), row(R), pl.BlockSpec((tm, D), lambda i: (i, gb)), pl.BlockSpec((tm, D), lambda i: (i, gb + 1)), row(D),
                  pl.BlockSpec((1, P, D), lambda i: (0, 0, 0)), pl.BlockSpec((1, R, D), lambda i: (0, 0, 0)),
                  pl.BlockSpec((1, D, D), lambda i: (0, 0, 0))] + [ANY] * len(deps),
        out_specs=[row(D)] * 4,
        out_shape=[jax.ShapeDtypeStruct((T, D), F32)] + [jax.ShapeDtypeStruct((T, D), BF16)] * 3,
        compiler_params=_cp("parallel"),
    )(pm, hl, proj, proj, x, wpu, wlu, wout, *deps)


def loss_head(x, gf, tgt):
    T, D = x.shape
    tm = _tile(T, ROW_TILE)

    def body(x_ref, g_ref, t_ref, loss_ref, dx_ref, dg_ref):
        @pl.when(pl.program_id(0) == 0)
        def _():
            loss_ref[...] = jnp.zeros_like(loss_ref)
            dg_ref[...] = jnp.zeros_like(dg_ref)

        xv = x_ref[...]
        gv = g_ref[...]
        n, _r = _rms(xv)
        e = n * gv - t_ref[...]
        loss_ref[...] += 0.5 * jnp.sum(jnp.sum(e * e, axis=-1, keepdims=True), axis=0, keepdims=True) / D
        dx, dg = _rms_bwd(e * (1.0 / D), xv, gv, 0.0)
        dx_ref[...] = dx
        dg_ref[...] += dg

    return _pallas_call(
        body, name="loss_head", grid=(T // tm,),
        in_specs=[pl.BlockSpec((tm, D), lambda i: (i, 0)), pl.BlockSpec((1, D), lambda i: (0, 0)), pl.BlockSpec((tm, D), lambda i: (i, 0))],
        out_specs=[pl.BlockSpec((1, 1), lambda i: (0, 0)), pl.BlockSpec((tm, D), lambda i: (i, 0)), pl.BlockSpec((1, D), lambda i: (0, 0))],
        out_shape=[jax.ShapeDtypeStruct((1, 1), F32), jax.ShapeDtypeStruct((T, D), F32), jax.ShapeDtypeStruct((1, D), F32)],
        compiler_params=_cp("arbitrary"),
    )(x, gf, tgt)


def ffn_down_bwd(dy, wd, u, l, deps=()):
    T, D = dy.shape
    cs = u.shape[-1]
    tm = _tile(T, WIDE_TILE)
    ni = T // tm

    def body(dy_ref, w_ref, u_ref, *rest):
        do_ref, du_ref, dyb_ref = rest[len(deps):]
        rows = pl.ds(pl.multiple_of(pl.program_id(1) * tm, tm), tm)

        @pl.when(pl.program_id(0) == 0)
        def _():
            d = (0.5 * dy_ref[...]).astype(BF16)
            dyb_ref[rows, :] = d
            do_ref[...] = d

        ds = _dot_nt(dyb_ref[rows, :], w_ref[0])
        a = u_ref[0, 0].astype(F32)
        b = u_ref[1, 0].astype(F32)
        sg = _sigmoid(a)
        du_ref[0, 0] = (ds * b * (sg * (1.0 + a * (1.0 - sg)))).astype(BF16)
        du_ref[1, 0] = (ds * (a * sg)).astype(BF16)

    first = lambda j, i: (jnp.where(j == 0, i, ni - 1), 0)
    blk = pl.BlockSpec((2, 1, tm, cs), lambda j, i: (0, j, i, 0))
    return _pallas_call(
        body, name="ffn_down_bwd", grid=(4, ni),
        in_specs=[pl.BlockSpec((tm, D), first), pl.BlockSpec((1, cs, D), lambda j, i: (0, j, 0)), blk] + [ANY] * len(deps),
        out_specs=[pl.BlockSpec((tm, D), first), blk],
        out_shape=[jax.ShapeDtypeStruct((T, D), BF16), jax.ShapeDtypeStruct((2, 4, T, cs), BF16)],
        scratch_shapes=[pltpu.VMEM((T, D), BF16)],
        compiler_params=_cp("arbitrary", "arbitrary"),
    )(dy, wd, u, *deps)


def dw_tn(name, a, a_spec, b, b_spec, G, M, N, T):
    tk = _tile(T, SUM_TILE)
    nk = T // tk

    def body(a_ref, b_ref, o32_ref, o16_ref, acc_ref):
        k = pl.program_id(1)

        @pl.when(k == 0)
        def _():
            acc_ref[...] = jnp.zeros_like(acc_ref)

        av = a_ref[0] if len(a_ref.shape) == 3 else a_ref[...]
        bv = b_ref[0] if len(b_ref.shape) == 3 else b_ref[...]
        acc_ref[...] += _dot_tn(av, bv)

        @pl.when(k == nk - 1)
        def _():
            o32_ref[0, 0] = acc_ref[...]
            o16_ref[0, 0] = acc_ref[...].astype(BF16)

    out = pl.BlockSpec((1, 1, M, N), lambda g, k: (0, g, 0, 0))
    return _pallas_call(
        body, name=name, grid=(G, nk),
        in_specs=[a_spec(tk), b_spec(tk)], out_specs=[out, out],
        out_shape=[jax.ShapeDtypeStruct((1, G, M, N), F32), jax.ShapeDtypeStruct((1, G, M, N), BF16)],
        scratch_shapes=[pltpu.VMEM((M, N), F32)],
        compiler_params=_cp("parallel", "arbitrary"),
    )(a, b)


def dx_norm_bwd(name, dact, d_spec, w, G, x, g, dy, l, w_transposed=False):
    T, D = x.shape
    wblk = w.shape[-2:]
    tm = _tile(T, WIDE_TILE)
    ni = T // tm
    ch = _tile(tm, ROW_TILE // 2)

    def body(d_ref, w_ref, x_ref, g_ref, dy_ref, dx_ref, dg_ref, acc_ref):
        j, i = pl.program_id(0), pl.program_id(1)
        rows = pl.ds(pl.multiple_of(i * tm, tm), tm)

        @pl.when(jnp.logical_and(i == 0, j == 0))
        def _():
            dg_ref[...] = jnp.zeros_like(dg_ref)

        @pl.when(j == 0)
        def _():
            acc_ref[rows, :] = jnp.zeros((tm, D), F32)

        dv = d_ref[0] if len(d_ref.shape) == 3 else d_ref[...]
        acc_ref[rows, :] += _dot(dv, w_ref[0, 0]) if w_transposed else _dot_nt(dv, w_ref[0, 0])

        @pl.when(j == G - 1)
        def _():
            dg = jnp.zeros((1, D), F32)
            for c0 in range(0, tm, ch):
                part_rows = pl.ds(pl.multiple_of(i * tm + c0, ch), ch)
                dx, dgc = _rms_bwd(acc_ref[part_rows, :], x_ref[c0:c0 + ch, :], g_ref[0], dy_ref[c0:c0 + ch, :])
                dx_ref[c0:c0 + ch, :] = dx
                dg = dg + dgc
            dg_ref[...] += dg

    last = pl.BlockSpec((tm, D), lambda j, i: (jnp.where(j == G - 1, i, 0), 0))
    return _pallas_call(
        body, name=name, grid=(G, ni),
        in_specs=[d_spec(tm), pl.BlockSpec((1, 1) + wblk, lambda j, i: (0, j, 0, 0)), last, pl.BlockSpec((1, 1, D), lambda j, i: (l, 0, 0)), last],
        out_specs=[last, pl.BlockSpec((1, D), lambda j, i: (0, 0))],
        out_shape=[jax.ShapeDtypeStruct((T, D), F32), jax.ShapeDtypeStruct((1, D), F32)],
        scratch_shapes=[pltpu.VMEM((T, D), F32)],
        compiler_params=_cp("arbitrary", "arbitrary"),
    )(dact, w, x, g, dy)


def mix_out_bwd(dy, proj, yp, yl, wpu, wlu, wout, P, R, l, deps=()):
    T, D = dy.shape
    tm = _tile(T, ROW_TILE)
    gb = (P + 2 * R) // D

    def body(dy_ref, gp_ref, gl_ref, yp_ref, yl_ref, wpu_ref, wlu_ref, wo_ref, *rest):
        dyb_ref, dyp_ref, dyl_ref, dgp_ref, dgl_ref, dpm_ref, dhl_ref = rest[len(deps):]
        dyb = dy_ref[...].astype(BF16)
        dyb_ref[...] = dyb
        dz = _dot_nt(dyb, wo_ref[0])
        sp = _sigmoid(gp_ref[...].astype(F32))
        sl = _sigmoid(gl_ref[...].astype(F32))
        dgp_ref[...] = (dz * yp_ref[...].astype(F32) * sp * (1.0 - sp)).astype(BF16)
        dgl_ref[...] = (dz * yl_ref[...].astype(F32) * sl * (1.0 - sl)).astype(BF16)
        dyp = (dz * sp).astype(BF16)
        dyl = (dz * sl).astype(BF16)
        dyp_ref[...] = dyp
        dyl_ref[...] = dyl
        dpm_ref[...] = _dot_nt(dyp, wpu_ref[0]).astype(BF16)
        dhl_ref[...] = _dot_nt(dyl, wlu_ref[0]).astype(BF16)

    row = lambda w: pl.BlockSpec((tm, w), lambda i: (i, 0))
    return _pallas_call(
        body, name="mix_out_bwd", grid=(T // tm,),
        in_specs=[row(D), pl.BlockSpec((tm, D), lambda i: (i, gb)), pl.BlockSpec((tm, D), lambda i: (i, gb + 1)), row(D), row(D),
                  pl.BlockSpec((1, P, D), lambda i: (0, 0, 0)), pl.BlockSpec((1, R, D), lambda i: (0, 0, 0)),
                  pl.BlockSpec((1, D, D), lambda i: (0, 0, 0))] + [ANY] * len(deps),
        out_specs=[row(D)] * 5 + [row(P), row(R)],
        out_shape=[jax.ShapeDtypeStruct((T, D), BF16)] * 5 + [jax.ShapeDtypeStruct((T, P), BF16), jax.ShapeDtypeStruct((T, R), BF16)],
        compiler_params=_cp("parallel"),
    )(dy, proj, proj, yp, yl, wpu, wlu, wout, *deps)


def lru_bwd(proj, hs, dhl, cw, cb, wa, ba, wx, bx, lam, P, l):
    T = proj.shape[0]
    _, H, hd, _ = wa.shape
    R = H * hd
    CW = cw.shape[1]

    def body(u_ref, ug_ref, cw_ref, cb_ref, wa_ref, ba_ref, wx_ref, bx_ref, lam_ref, hs_ref, dhl_ref,
             du_ref, dug_ref, dcw_ref, dcb_ref, dwa_ref, dba_ref, dwx_ref, dbx_ref, dlam_ref, c_s, g_s, y_s):
        u = u_ref[...].astype(F32)
        v = _conv(u, cw_ref, cb_ref[0])
        lam = lam_ref[0]
        r, i, sp, a, mult, inv_mult = _lru_gates(v, wa_ref, ba_ref[0], wx_ref, bx_ref[0], lam)
        ug = ug_ref[...].astype(F32)
        ge, th = _gelu(ug)
        hs = hs_ref[...]
        dhl = dhl_ref[...].astype(F32)
        dug_ref[...] = (dhl * hs * _gelu_grad(ug, th)).astype(BF16)
        c_s[...] = _shift_up(a, 1)
        g_s[...] = dhl * ge
        _scan_bwd(c_s, g_s, y_s)
        y = y_s[...]
        da = y * _shift_down(hs, 1)
        iv = i * v
        dlog_a = da * a - (y * iv) * (a * a) * inv_mult
        div = y * mult
        dpa = (dlog_a * (-LRU_C) * sp) * r * (1.0 - r)
        dpx = (div * v) * i * (1.0 - i)
        dsp = jnp.sum(dlog_a * (-LRU_C) * r, axis=0, keepdims=True)
        dlam_ref[0] = -dsp * _sigmoid(-lam)
        vb = v.astype(BF16)
        dpab, dpxb = dpa.astype(BF16), dpx.astype(BF16)
        dwa_ref[0, 0] = _dot_tn(vb, dpab)
        dwx_ref[0, 0] = _dot_tn(vb, dpxb)
        dba_ref[0] = jnp.sum(dpa, axis=0, keepdims=True)
        dbx_ref[0] = jnp.sum(dpx, axis=0, keepdims=True)
        dv = div * i + _dot_nt(dpab, wa_ref[0, 0].astype(BF16)) + _dot_nt(dpxb, wx_ref[0, 0].astype(BF16))
        dcb_ref[0] = jnp.sum(dv, axis=0, keepdims=True)
        du = jnp.zeros_like(dv)
        for k in range(CW):
            du = du + cw_ref[0, k:k + 1, :] * _shift_up(dv, CW - 1 - k)
            dcw_ref[0, k:k + 1, :] = jnp.sum(dv * _shift_down(u, CW - 1 - k), axis=0, keepdims=True)
        du_ref[...] = du.astype(BF16)

    col = pl.BlockSpec((T, hd), lambda h: (0, h))
    vec = pl.BlockSpec((1, 1, hd), lambda h: (0, 0, h))
    mat = pl.BlockSpec((1, 1, hd, hd), lambda h: (0, h, 0, 0))
    vshape = jax.ShapeDtypeStruct((1, 1, R), F32)
    mshape = jax.ShapeDtypeStruct((1, H, hd, hd), F32)
    return _pallas_call(
        body, name="lru_bwd", grid=(H,),
        in_specs=_lru_specs(T, hd, P, R, CW, l) + [col, col],
        out_specs=[col, col, pl.BlockSpec((1, CW, hd), lambda h: (0, 0, h)), vec, mat, vec, mat, vec, vec],
        out_shape=[jax.ShapeDtypeStruct((T, R), BF16)] * 2 + [jax.ShapeDtypeStruct((1, CW, R), F32), vshape, mshape, vshape, mshape, vshape, vshape],
        scratch_shapes=[pltpu.VMEM((T, hd), F32)] * 3,
        compiler_params=_cp("parallel"),
    )(proj, proj, cw, cb, wa, ba, wx, bx, lam, hs, dhl)


def pool_bwd(proj, dpm, pw, pb, ps, l):
    T = proj.shape[0]
    _, G, gd, _ = pw.shape
    P = G * gd

    def body(u_ref, d_ref, w_ref, b_ref, s_ref, du_ref, dw_ref, db_ref, dsc_ref):
        for gi in range(G):
            cols = slice(gi * gd, (gi + 1) * gd)
            w = POOL_WINDOWS[gi]
            inv = _inv_count(T, w)
            ug = u_ref[:, cols].astype(F32)
            pooled = _pooled(ug, w, inv).astype(BF16)
            wb = w_ref[0, gi].astype(BF16)
            mixed = _dot(pooled, wb) + b_ref[0, :, cols]
            dpm_g = d_ref[:, cols].astype(F32)
            dsc_ref[0, :, cols] = jnp.sum(dpm_g * mixed, axis=0, keepdims=True)
            dmixed = dpm_g * s_ref[0, :, cols]
            db_ref[0, :, cols] = jnp.sum(dmixed, axis=0, keepdims=True)
            dmb = dmixed.astype(BF16)
            dw_ref[0, gi] = _dot_tn(pooled, dmb)
            dpooled = _dot_nt(dmb, wb)
            s = dpooled * inv
            k = 1
            while k < w:
                s = s + _shift_up(s, k)
                k *= 2
            du_ref[:, cols] = (s - dpooled).astype(BF16)

    vec = pl.BlockSpec((1, 1, P), lambda i: (l, 0, 0))
    ovec = pl.BlockSpec((1, 1, P), lambda i: (0, 0, 0))
    return _pallas_call(
        body, name="pool_bwd", grid=(1,),
        in_specs=[pl.BlockSpec((T, P), lambda i: (0, 0)), pl.BlockSpec((T, P), lambda i: (0, 0)),
                  pl.BlockSpec((1, G, gd, gd), lambda i: (l, 0, 0, 0)), vec, vec],
        out_specs=[pl.BlockSpec((T, P), lambda i: (0, 0)), pl.BlockSpec((1, G, gd, gd), lambda i: (0, 0, 0, 0)), ovec, ovec],
        out_shape=[jax.ShapeDtypeStruct((T, P), BF16), jax.ShapeDtypeStruct((1, G, gd, gd), F32),
                   jax.ShapeDtypeStruct((1, 1, P), F32), jax.ShapeDtypeStruct((1, 1, P), F32)],
        compiler_params=_cp("arbitrary"),
    )(proj, dpm, pw, pb, ps)


def _place():
    x, y, c = lax.axis_index("x"), lax.axis_index("y"), lax.axis_index("c")
    return x, y, c


HBM = pl.BlockSpec(memory_space=pltpu.HBM)
SEM = pl.BlockSpec(memory_space=pltpu.SEMAPHORE)
EFFECT = pltpu.SideEffectType.DATAFLOW_SIDE_EFFECTING


def _in_hbm(a):
    return pltpu.with_memory_space_constraint(a, pltpu.HBM)


def split_start(name, bufs, n_copies, copies_of, deps=()):
    nb = len(bufs)

    def body(*refs):
        buf = refs[:nb]
        send_sems, recv_sems = refs[nb + len(deps)], refs[nb + len(deps) + 1]
        token = refs[-1]
        for i, (src, dst, dev) in enumerate(copies_of(buf)):
            pltpu.make_async_remote_copy(src_ref=src, dst_ref=dst, send_sem=send_sems.at[i], recv_sem=recv_sems.at[i],
                                         device_id=dev, device_id_type=MESH).start()
        token[...] = jnp.zeros_like(token)

    outs = _pallas_call(
        body, name=name,
        in_specs=[HBM] * nb + [ANY] * len(deps),
        out_specs=(SEM, SEM, *([HBM] * nb), pl.BlockSpec(memory_space=pltpu.VMEM)),
        out_shape=(pltpu.SemaphoreType.DMA((n_copies,)), pltpu.SemaphoreType.DMA((n_copies,)),
                   *[pltpu.HBM(b.shape, b.dtype) for b in bufs], jax.ShapeDtypeStruct((8, 128), F32)),
        input_output_aliases={i: 2 + i for i in range(nb)},
        compiler_params=pltpu.CompilerParams(has_side_effects=EFFECT),
    )(*[_in_hbm(b) for b in bufs], *deps)
    return outs[0], outs[1], list(outs[2:2 + nb]), outs[-1]


def split_wait(name, bufs, send_sems, recv_sems, after, copies_of):
    nb = len(bufs)

    def body(*refs):
        buf = refs[:nb]
        send, recv = refs[nb], refs[nb + 1]
        for i, (src, dst, dev) in enumerate(copies_of(buf)):
            cp = pltpu.make_async_remote_copy(src_ref=src, dst_ref=dst, send_sem=send.at[i], recv_sem=recv.at[i],
                                              device_id=dev, device_id_type=MESH)
            cp.wait_send()
            cp.wait_recv()

    outs = _pallas_call(
        body, name=name,
        in_specs=[HBM] * nb + [SEM, SEM] + [ANY] * len(after),
        out_specs=[HBM] * nb,
        out_shape=[pltpu.HBM(b.shape, b.dtype) for b in bufs],
        input_output_aliases={i: i for i in range(nb)},
        compiler_params=pltpu.CompilerParams(has_side_effects=EFFECT),
    )(*bufs, send_sems, recv_sems, *after)
    return list(outs)


def place_own(w, l, place, dtype):
    _, rows, cols = w.shape
    tr = _rows_tile(rows, cols, 1 << 19)

    def body(p_ref, w_ref, o_ref):
        o_ref[0] = w_ref[0].astype(dtype)

    return _pallas_call(
        body, name="place_own",
        grid_spec=pltpu.PrefetchScalarGridSpec(
            num_scalar_prefetch=1, grid=(rows // tr,),
            in_specs=[pl.BlockSpec((1, tr, cols), lambda i, p: (l, i, 0))],
            out_specs=pl.BlockSpec((1, tr, cols), lambda i, p: (p[2], i, 0))),
        out_shape=jax.ShapeDtypeStruct((N_DEV, rows, cols), dtype), compiler_params=_cp("parallel"),
    )(place, w)


def _gather_copies(land):
    x, y, c = _place()
    k = 4 * x + 2 * y + c
    peers = [(x, 1 - y, c), (1 - x, y, c), (1 - x, 1 - y, c), (x, y, 1 - c)]
    return [(b.at[k], b.at[k], p) for p in peers for b in land]


def gather_start(name, land, deps=()):
    return split_start(name, land, 4 * len(land), _gather_copies, deps)


def gather_wait(name, land, send_sems, recv_sems, after):
    return split_wait(name, land, send_sems, recv_sems, after, _gather_copies)


def _forward_copies(land):
    x, y, c = _place()
    slots = [4 * px + 2 * py + c for px, py in [(x, 1 - y), (1 - x, y), (1 - x, 1 - y)]]
    return [(b.at[k], b.at[k], (x, y, 1 - c)) for k in slots for b in land]


def gather_forward_start(name, land, deps=()):
    return split_start(name, land, 3 * len(land), _forward_copies, deps)


def gather_forward_wait(name, land, send_sems, recv_sems, after):
    return split_wait(name, land, send_sems, recv_sems, after, _forward_copies)


def _chip_copies(nsrc):
    def copies(buf):
        p16, recv2 = buf[:nsrc], buf[nsrc:]
        x, y, c = _place()
        out = []
        for d in (1, 2, 3):
            px = 1 - x if d & 2 else x
            py = 1 - y if d & 1 else y
            out += [(p16[a].at[:, 2 * px + py], recv2[a].at[:, d - 1], (px, py, c)) for a in range(nsrc)]
        return out
    return copies


def _pair_copies(nsrc):
    def copies(buf):
        g16, recv = buf[:nsrc], buf[nsrc:]
        x, y, c = _place()
        return [(g16[a].at[:, 2 * j + 1 - c], recv[a].at[:, j], (x, y, 1 - c)) for a in range(nsrc) for j in range(N_CHIP)]
    return copies


def pair_exchange_start(name, g16, deps=()):
    n = len(g16)
    land = [lax.empty((1, N_CHIP) + s.shape[2:], s.dtype) for s in g16]
    return split_start(name, list(g16) + land, N_CHIP * n, _pair_copies(n), deps)


def pair_exchange_wait(name, bufs, send_sems, recv_sems, after):
    n = len(bufs) // 2
    return split_wait(name, bufs, send_sems, recv_sems, after, _pair_copies(n))[n:]


def chip_exchange_start(name, pair16, deps=()):
    n = len(pair16)
    land = [lax.empty((s.shape[0], 3) + s.shape[2:], s.dtype) for s in pair16]
    return split_start(name, list(pair16) + land, 3 * n, _chip_copies(n), deps)


def chip_exchange_wait(name, bufs, send_sems, recv_sems, after):
    n = len(bufs) // 2
    return split_wait(name, bufs, send_sems, recv_sems, after, _chip_copies(n))[n:]


def _rows_tile(rows, cols, budget=1 << 20):
    t = rows
    while t % 2 == 0 and t * cols > budget and (t // 2) % 16 == 0:
        t //= 2
    return t


def pair_sum(g32, recv1, place, l):
    _, _, rows, cols = recv1.shape
    tr = _rows_tile(rows, cols)

    def body(p_ref, m_ref, r_ref, o_ref):
        o_ref[...] = (m_ref[...] + r_ref[...].astype(F32)).astype(o_ref.dtype)

    blk = pl.BlockSpec((1, 1, tr, cols), lambda j, i, p: (0, j, i, 0))
    return _pallas_call(
        body, name="pair_sum",
        grid_spec=pltpu.PrefetchScalarGridSpec(
            num_scalar_prefetch=1, grid=(N_CHIP, rows // tr),
            in_specs=[pl.BlockSpec((1, 1, tr, cols), lambda j, i, p: (l, 2 * j + p[0], i, 0)), blk], out_specs=blk),
        out_shape=jax.ShapeDtypeStruct(recv1.shape, recv1.dtype), compiler_params=_cp("parallel", "parallel"),
    )(place, g32, recv1)


def _grad_in_specs(tr, cols, l):
    return ([pl.BlockSpec((1, 1, tr, cols), lambda i, p: (l, p[2], i, 0)), pl.BlockSpec((1, 1, tr, cols), lambda i, p: (0, p[1], i, 0))]
            + [pl.BlockSpec((1, 1, tr, cols), lambda i, p, d=d: (0, d, i, 0)) for d in range(3)])


def _grad_total(o32, o16, r0, r1, r2):
    return (o32[0, 0] + o16[0, 0].astype(F32)) + r0[0, 0].astype(F32) + r1[0, 0].astype(F32) + r2[0, 0].astype(F32)


def grad_sum(g32, recv1, recv2, place):
    _, _, rows, cols = recv1.shape
    tr = _rows_tile(rows, cols)

    def body(p_ref, o32, o16, r0, r1, r2, g_ref):
        g_ref[...] = _grad_total(o32, o16, r0, r1, r2)

    return _pallas_call(
        body, name="grad_sum",
        grid_spec=pltpu.PrefetchScalarGridSpec(
            num_scalar_prefetch=1, grid=(rows // tr,), in_specs=_grad_in_specs(tr, cols, 0),
            out_specs=pl.BlockSpec((tr, cols), lambda i, p: (i, 0))),
        out_shape=jax.ShapeDtypeStruct((rows, cols), F32), compiler_params=_cp("parallel"),
    )(place, g32, recv1, recv2, recv2, recv2)


def _adamw_math(w, g, m, v):
    m = ADAM_B1 * m + (1.0 - ADAM_B1) * g
    v = ADAM_B2 * v + (1.0 - ADAM_B2) * (g * g)
    m_hat = m / (1.0 - ADAM_B1 ** ADAM_STEP)
    v_hat = v / (1.0 - ADAM_B2 ** ADAM_STEP)
    delta = -ADAM_LR * (m_hat / (jnp.sqrt(v_hat) + ADAM_EPS) + ADAM_WD * w)
    return delta, m, v


def grad_sum_adamw(g32, recv1, recv2, w, m, v, place, l, prev, deps=()):
    L, rows, cols = w.shape
    tr = _rows_tile(rows, cols, 1 << 19)

    def body(p_ref, o32, o16, r0, r1, r2, w_ref, m_ref, v_ref, *rest):
        g_ref, d_ref, nm_ref, nv_ref = rest[-4:]
        g = _grad_total(o32, o16, r0, r1, r2)
        d, nm, nv = _adamw_math(w_ref[0], g, m_ref[0], v_ref[0])
        g_ref[0] = g
        d_ref[0] = d
        nm_ref[0] = nm
        nv_ref[0] = nv

    blk = pl.BlockSpec((1, tr, cols), lambda i, p: (l, i, 0))
    args = [g32, recv1, recv2, recv2, recv2, w, m, v]
    in_specs = _grad_in_specs(tr, cols, 0) + [blk] * 3
    aliases = {}
    if prev is not None:
        aliases = {1 + len(args) + k: k for k in range(4)}
        args += list(prev)
        in_specs += [ANY] * 4
    args += list(deps)
    in_specs += [ANY] * len(deps)
    return _pallas_call(
        body, name="grad_sum_adamw",
        grid_spec=pltpu.PrefetchScalarGridSpec(num_scalar_prefetch=1, grid=(rows // tr,), in_specs=in_specs, out_specs=[blk] * 4),
        out_shape=[jax.ShapeDtypeStruct((L, rows, cols), F32)] * 4, input_output_aliases=aliases,
        compiler_params=_cp("parallel"),
    )(place, *args)


def adamw(w, g, m, v):
    rows, cols = w.shape
    tr = _rows_tile(rows, cols, 1 << 18)

    def body(w_ref, g_ref, m_ref, v_ref, d_ref, nm_ref, nv_ref):
        d, nm, nv = _adamw_math(w_ref[...], g_ref[...], m_ref[...], v_ref[...])
        d_ref[...] = d
        nm_ref[...] = nm
        nv_ref[...] = nv

    blk = pl.BlockSpec((tr, cols), lambda i: (i, 0))
    return _pallas_call(body, name="adamw_small", grid=(rows // tr,), in_specs=[blk] * 4, out_specs=[blk] * 3,
                        out_shape=[jax.ShapeDtypeStruct((rows, cols), F32)] * 3, compiler_params=_cp("parallel"))(w, g, m, v)


SMALL = ("norm_ffn1", "norm_mix", "pool_w", "pool_b", "pool_scale", "conv_w", "conv_b", "lru_w_a", "lru_b_a", "lru_w_x", "lru_b_x",
         "lru_lambda", "norm_ffn2", "final_norm")
BIG = ("ffn1_w_up", "ffn1_w_down", "w_in", "w_pool_up", "w_lru_up", "w_out", "ffn2_w_up", "ffn2_w_down")
NAMES = ("norm_ffn1", "ffn1_w_up", "ffn1_w_down", "norm_mix", "w_in", "pool_w", "pool_b", "pool_scale", "w_pool_up", "conv_w", "conv_b",
         "lru_w_a", "lru_b_a", "lru_w_x", "lru_b_x", "lru_lambda", "w_lru_up", "w_out", "norm_ffn2", "ffn2_w_up", "ffn2_w_down", "final_norm")
SUBLAYERS = (("ffn1_w_up", "ffn1_w_down"), ("w_in", "w_pool_up", "w_lru_up", "w_out", "conv_w"), ("ffn2_w_up", "ffn2_w_down"))
PACK_ROWS = 16 * N_DEV


def _pack(parts):
    flat = jnp.concatenate([p.reshape(-1) for p in parts])
    unit = 128 * PACK_ROWS
    padded = -(-flat.size // unit) * unit
    return jnp.pad(flat, (0, padded - flat.size)).reshape(-1, 128)


def _unpack(packed, shapes):
    flat = packed.reshape(-1)
    out, off = [], 0
    for s in shapes:
        n = 1
        for d in s:
            n *= d
        out.append(flat[off:off + n].reshape(s))
        off += n
    return out


def kernel(x, norm_ffn1, ffn1_w_up, ffn1_w_down, norm_mix, w_in, pool_w, pool_b, pool_scale, w_pool_up, conv_w, conv_b, lru_w_a, lru_b_a, lru_w_x, lru_b_x, lru_lambda, w_lru_up, w_out, norm_ffn2, ffn2_w_up, ffn2_w_down, final_norm, loss_target, m_norm_ffn1, m_ffn1_w_up, m_ffn1_w_down, m_norm_mix, m_w_in, m_pool_w, m_pool_b, m_pool_scale, m_w_pool_up, m_conv_w, m_conv_b, m_lru_w_a, m_lru_b_a, m_lru_w_x, m_lru_b_x, m_lru_lambda, m_w_lru_up, m_w_out, m_norm_ffn2, m_ffn2_w_up, m_ffn2_w_down, m_final_norm, v_norm_ffn1, v_ffn1_w_up, v_ffn1_w_down, v_norm_mix, v_w_in, v_pool_w, v_pool_b, v_pool_scale, v_w_pool_up, v_conv_w, v_conv_b, v_lru_w_a, v_lru_b_a, v_lru_w_x, v_lru_b_x, v_lru_lambda, v_w_lru_up, v_w_out, v_norm_ffn2, v_ffn2_w_up, v_ffn2_w_down, v_final_norm):
    W = dict(norm_ffn1=norm_ffn1, ffn1_w_up=ffn1_w_up, ffn1_w_down=ffn1_w_down, norm_mix=norm_mix, w_in=w_in, pool_w=pool_w, pool_b=pool_b,
             pool_scale=pool_scale, w_pool_up=w_pool_up, conv_w=conv_w, conv_b=conv_b, lru_w_a=lru_w_a, lru_b_a=lru_b_a, lru_w_x=lru_w_x,
             lru_b_x=lru_b_x, lru_lambda=lru_lambda, w_lru_up=w_lru_up, w_out=w_out, norm_ffn2=norm_ffn2, ffn2_w_up=ffn2_w_up,
             ffn2_w_down=ffn2_w_down, final_norm=final_norm)
    M = dict(norm_ffn1=m_norm_ffn1, ffn1_w_up=m_ffn1_w_up, ffn1_w_down=m_ffn1_w_down, norm_mix=m_norm_mix, w_in=m_w_in, pool_w=m_pool_w,
             pool_b=m_pool_b, pool_scale=m_pool_scale, w_pool_up=m_w_pool_up, conv_w=m_conv_w, conv_b=m_conv_b, lru_w_a=m_lru_w_a,
             lru_b_a=m_lru_b_a, lru_w_x=m_lru_w_x, lru_b_x=m_lru_b_x, lru_lambda=m_lru_lambda, w_lru_up=m_w_lru_up, w_out=m_w_out,
             norm_ffn2=m_norm_ffn2, ffn2_w_up=m_ffn2_w_up, ffn2_w_down=m_ffn2_w_down, final_norm=m_final_norm)
    V = dict(norm_ffn1=v_norm_ffn1, ffn1_w_up=v_ffn1_w_up, ffn1_w_down=v_ffn1_w_down, norm_mix=v_norm_mix, w_in=v_w_in, pool_w=v_pool_w,
             pool_b=v_pool_b, pool_scale=v_pool_scale, w_pool_up=v_w_pool_up, conv_w=v_conv_w, conv_b=v_conv_b, lru_w_a=v_lru_w_a,
             lru_b_a=v_lru_b_a, lru_w_x=v_lru_w_x, lru_b_x=v_lru_b_x, lru_lambda=v_lru_lambda, w_lru_up=v_w_lru_up, w_out=v_w_out,
             norm_ffn2=v_norm_ffn2, ffn2_w_up=v_ffn2_w_up, ffn2_w_down=v_ffn2_w_down, final_norm=v_final_norm)

    for S in (W, M, V):
        for n in ("ffn1_w_up", "ffn2_w_up"):
            S[n] = jnp.swapaxes(S[n], 1, 2)

    T, D = x.shape[1], x.shape[2]
    L = norm_ffn1.shape[0]
    P = pool_scale.shape[1]
    R = lru_lambda.shape[1]
    H, hd = lru_w_a.shape[1], lru_w_a.shape[2]
    CW = conv_w.shape[1]
    cs = ffn1_w_up.shape[2]
    ci = w_in.shape[2]
    xin = x.reshape(T, D)
    tgt = loss_target.reshape(T, D)
    dev = 4 * lax.axis_index("x") + 2 * lax.axis_index("y") + lax.axis_index("c")
    place = jnp.stack([lax.axis_index("c"), 2 * lax.axis_index("x") + lax.axis_index("y"), dev]).astype(jnp.int32)

    cw_flat = conv_w.reshape(L, -1)
    cw_pad = (-cw_flat.shape[1]) % 1024
    cw_tiles = jnp.pad(cw_flat, ((0, 0), (0, cw_pad))).reshape(L, -1, 128)

    def units(l):
        return SUBLAYERS if l == 0 else (tuple(n for u in SUBLAYERS for n in u),)

    queued = {"gather": (), "pair": (), "chip": ()}

    gathering = []

    def gather_units_start(l):
        for k, names in enumerate(SUBLAYERS):
            land = [place_own(cw_tiles, l, place, F32) if n == "conv_w" else place_own(W[n], l, place, BF16) for n in names]
            send_sems, recv_sems, land, tok = gather_start(f"gather_start_l{l}_u{k}", land, queued["gather"])
            gathering.append(dict(names=names, tag=f"l{l}_u{k}", send=send_sems, recv=recv_sems, land=land, tok=tok, arrived=False))
            queued["gather"] = (tok,)

    def gather_unit_arrive(after):
        waiting = [u for u in gathering if not u["arrived"]]
        if not waiting:
            return ()
        unit, tokens = waiting[0], [u["tok"] for u in waiting[1:]]
        land = gather_wait(f"gather_wait_{unit['tag']}", unit["land"], unit["send"], unit["recv"], list(after) + tokens)
        send_sems, recv_sems, land, tok = gather_forward_start(f"gather_pass_start_{unit['tag']}", land)
        unit.update(land=land, send=send_sems, recv=recv_sems, tok=tok, arrived=True)
        return (tok,)

    def gather_unit_weights(after):
        if not gathering[0]["arrived"]:
            gather_unit_arrive(after)
        unit = gathering.pop(0)
        land = gather_forward_wait(f"gather_pass_wait_{unit['tag']}", unit["land"], unit["send"], unit["recv"], after)
        g = dict(zip(unit["names"], land))
        one = lambda a: a.reshape((1,) + a.shape)
        w = {}
        for tag_, up, dn in (("1", "ffn1_w_up", "ffn1_w_down"), ("2", "ffn2_w_up", "ffn2_w_down")):
            if up in g:
                w["wup" + tag_], w["wd" + tag_] = one(g[up]), g[dn].reshape(1, -1, D)
        if "w_in" in g:
            cw_l = g["conv_w"].reshape(N_DEV, -1)[:, :cw_flat.shape[1]].reshape((N_DEV,) + conv_w.shape[1:])
            w.update(win=one(g["w_in"]), wlu=g["w_lru_up"].reshape(1, R, D), wout=g["w_out"].reshape(1, D, D),
                     wpu=g["w_pool_up"].transpose(1, 0, 2).reshape(1, P, D),
                     cw=cw_l.transpose(1, 0, 2).reshape(1, CW, R))
        return w

    vec = lambda a: a.reshape(L, 1, -1)
    p = dict(g1=vec(norm_ffn1), gm=vec(norm_mix), g2=vec(norm_ffn2), pb=vec(pool_b), ps=vec(pool_scale), cb=vec(conv_b),
             ba=vec(lru_b_a), bx=vec(lru_b_x), lam=vec(lru_lambda), pw=pool_w, wa=lru_w_a, wx=lru_w_x)

    AHEAD = 2
    for l in range(min(AHEAD, L)):
        gather_units_start(l)
    saved, LW = [], []
    xc = xin
    for l in range(L):
        w = gather_unit_weights([xc])
        if l + AHEAD < L:
            gather_units_start(l + AHEAD)
        sv = {"x1": xc}
        sv["h1"], sv["u1"], sv["s1"] = ffn_up(xc, p["g1"], w["wup1"], l)
        xc = ffn_down(sv["s1"], w["wd1"], xc, l, gather_unit_arrive([sv["s1"]]))
        sv["x2"] = xc
        w.update(gather_unit_weights([xc]))
        sv["h2"], sv["proj"] = mix_in(xc, p["gm"], w["win"], l)
        sv["pm"] = pool_fwd(sv["proj"], p["pw"], p["pb"], p["ps"], l)
        sv["hl"], sv["hs"] = lru_fwd(sv["proj"], w["cw"], p["cb"], p["wa"], p["ba"], p["wx"], p["bx"], p["lam"], P, l)
        xc, sv["yp"], sv["yl"], sv["z"] = mix_out(sv["pm"], sv["hl"], sv["proj"], xc, w["wpu"], w["wlu"], w["wout"], P, l,
                                                  gather_unit_arrive([sv["hl"]]))
        sv["x3"] = xc
        w.update(gather_unit_weights([xc]))
        sv["h3"], sv["u3"], sv["s3"] = ffn_up(xc, p["g2"], w["wup2"], l)
        xc = ffn_down(sv["s3"], w["wd2"], xc, l, gather_unit_arrive([sv["s3"]]))
        saved.append(sv)
        LW.append(w)

    loss_part, dx, d_final = loss_head(xc, final_norm.reshape(1, D), tgt)
    loss = lax.psum(loss_part[0, 0], ("x", "y", "c"))

    G = [dict() for _ in range(L)]
    small = {n: [None] * L for n in SMALL if n != "final_norm"}

    def to_slots(name, pair):
        if name == "w_pool_up":
            return tuple(a.reshape(1, P, N_DEV, D // N_DEV).transpose(0, 2, 1, 3) for a in pair)
        return tuple(a.reshape((1, N_DEV) + W[name].shape[1:]) for a in pair)

    def ffn_bwd(dy, sv, tag, wup, wd, gn, up_name, dn_name, norm_name, l, deps=()):
        dout, du = ffn_down_bwd(dy, wd, sv["u" + tag], l, deps)
        du = du.reshape(N_DEV, T, cs)
        G[l][dn_name] = to_slots(dn_name, dw_tn("dw_down", sv["s" + tag], lambda tk: pl.BlockSpec((1, tk, cs), lambda g, k: (g, k, 0)),
                                                dout, lambda tk: pl.BlockSpec((tk, D), lambda g, k: (k, 0)), 4, cs, D, T))
        G[l][up_name] = to_slots(up_name, dw_tn("dw_up", du, lambda tk: pl.BlockSpec((1, tk, cs), lambda g, k: (g, k, 0)),
                                                sv["h" + tag], lambda tk: pl.BlockSpec((tk, D), lambda g, k: (k, 0)), N_DEV, cs, D, T))
        dxn, dg = dx_norm_bwd("ffn_dx", du, lambda tm: pl.BlockSpec((1, tm, cs), lambda j, i: (j, i, 0)), wup, N_DEV,
                              sv["x" + tag], gn, dy, l, w_transposed=True)
        small[norm_name][l] = dg.reshape(D)
        return dxn

    pairing, in_flight = [], []

    def reduce_start(l, names, tag):
        names = [n for n in names if n != "conv_w"]
        send_sems, recv_sems, bufs, tok = pair_exchange_start(f"rs_pair_start_{tag}", [G[l][n][1] for n in names], queued["pair"])
        pairing.append((l, names, tag, send_sems, recv_sems, bufs))
        queued["pair"] = (tok,)
        return (tok,)

    def reduce_continue(after):
        l, names, tag, send_sems, recv_sems, bufs = pairing.pop(0)
        recv1 = pair_exchange_wait(f"rs_pair_wait_{tag}", bufs, send_sems, recv_sems, after)
        pair16 = [pair_sum(G[l][n][0], r_, place, 0) for n, r_ in zip(names, recv1)]
        send_sems, recv_sems, bufs, tok = chip_exchange_start(f"rs_chip_start_{tag}", pair16, queued["chip"])
        in_flight.append((l, names, tag, send_sems, recv_sems, bufs, recv1))
        queued["chip"] = (tok,)
        return (tok,)

    def boundary(l, k, dx_now):
        deps = reduce_continue([dx_now]) if pairing else ()
        if len(units(l)) > 1:
            deps += reduce_start(l, units(l)[k], f"l{l}_u{k}")
        elif k == 0:
            deps += reduce_start(l, units(l)[0], f"l{l}_u0")
        return deps

    deps = ()
    for l in reversed(range(L)):
        sv, w = saved[l], LW[l]
        dx = ffn_bwd(dx, sv, "3", w["wup2"], w["wd2"], p["g2"], "ffn2_w_up", "ffn2_w_down", "norm_ffn2", l, deps)
        deps = boundary(l, 2, dx)
        dyb, dyp, dyl, dgp, dgl, dpm, dhl = mix_out_bwd(dx, sv["proj"], sv["yp"], sv["yl"], w["wpu"], w["wlu"], w["wout"], P, R, l, deps)
        row = lambda wd_: (lambda tk: pl.BlockSpec((tk, wd_), lambda g, k: (k, 0)))
        G[l]["w_out"] = to_slots("w_out", dw_tn("dw_out", sv["z"], row(D), dyb, row(D), 1, D, D, T))
        G[l]["w_lru_up"] = to_slots("w_lru_up", dw_tn("dw_lru_up", sv["hl"], row(R), dyl, row(D), 1, R, D, T))
        G[l]["w_pool_up"] = to_slots("w_pool_up", dw_tn("dw_pool_up", sv["pm"], row(P), dyp, row(D), 1, P, D, T))
        du_lru, du_gelu, dcw, dcb, dwa, dba, dwx, dbx, dlam = lru_bwd(
            sv["proj"], sv["hs"], dhl, w["cw"], p["cb"], p["wa"], p["ba"], p["wx"], p["bx"], p["lam"], P, l)
        du_pool, dpw, dpb, dpsc = pool_bwd(sv["proj"], dpm, p["pw"], p["pb"], p["ps"], l)
        dproj = jnp.concatenate([du_pool, du_lru, du_gelu, dgp, dgl], axis=1)
        G[l]["w_in"] = to_slots("w_in", dw_tn("dw_in", sv["h2"], row(D), dproj, lambda tk: pl.BlockSpec((tk, ci), lambda g, k: (k, g)),
                                              N_DEV, D, ci, T))
        dx, dgm = dx_norm_bwd("mix_dx", dproj, lambda tm: pl.BlockSpec((tm, ci), lambda j, i: (i, j)), w["win"], N_DEV,
                              sv["x2"], p["gm"], dx, l)
        small["norm_mix"][l] = dgm.reshape(D)
        small["pool_w"][l], small["pool_b"][l], small["pool_scale"][l] = dpw[0], dpb.reshape(pool_b.shape[1:]), dpsc.reshape(P)
        small["conv_w"][l], small["conv_b"][l] = dcw[0], dcb.reshape(R)
        small["lru_w_a"][l], small["lru_b_a"][l] = dwa[0], dba.reshape(H, hd)
        small["lru_w_x"][l], small["lru_b_x"][l] = dwx[0], dbx.reshape(H, hd)
        small["lru_lambda"][l] = dlam.reshape(R)
        deps = boundary(l, 1, dx)
        dx = ffn_bwd(dx, sv, "1", w["wup1"], w["wd1"], p["g1"], "ffn1_w_up", "ffn1_w_down", "norm_ffn1", l, deps)
        deps = boundary(l, 0, dx)

    grad_x = dx.reshape(x.shape)

    small_parts = [jnp.stack(small[n]) for n in SMALL if n != "final_norm"] + [d_final.reshape(D)]
    small_shapes = [p.shape for p in small_parts]
    gpack = _pack(small_parts).reshape(1, N_DEV, -1, 128)
    small_pair = pair_exchange_start("rs_pair_start_small", [gpack], queued["pair"])
    while pairing:
        reduce_continue([dx])

    outs = {n: None for n in BIG}

    def unit_updates(unit, recv2, deps=()):
        l, names, recv1 = unit[0], unit[1], unit[6]
        for i, n in enumerate(names):
            outs[n] = grad_sum_adamw(G[l][n][0], recv1[i], recv2[i], W[n], M[n], V[n], place, l, outs[n], deps)
        return [outs[n][0] for n in names]

    after = [dx]
    late = in_flight[-2:]
    for k, unit in enumerate(in_flight[:-2]):
        recv2 = chip_exchange_wait(f"rs_chip_wait_{unit[2]}", unit[5], unit[3], unit[4], after)
        after = unit_updates(unit, recv2)
        if k == 0:
            recv1_s = pair_exchange_wait("rs_pair_wait_small", small_pair[2], small_pair[0], small_pair[1], after)[0]
            pair_s = pair_sum(gpack, recv1_s, place, 0)
            small_chip = chip_exchange_start("rs_chip_start_small", [pair_s], queued["chip"])
    late_recv2 = []
    for unit in late:
        late_recv2.append(chip_exchange_wait(f"rs_chip_wait_{unit[2]}", unit[5], unit[3], unit[4], after))
        after = [late_recv2[-1][0]]
    recv2_s = chip_exchange_wait("rs_chip_wait_small", small_chip[2], small_chip[0], small_chip[1], after)[0]
    gs = grad_sum(gpack, recv1_s, recv2_s, place)
    gs_slots = [place_own(gs.reshape((1,) + gs.shape), 0, place, F32)]
    send_sems, recv_sems, gs_slots, tok = gather_start("gather_start_small", gs_slots)
    after = unit_updates(late[0], late_recv2[0], (tok,))
    gs_slots = gather_wait("gather_wait_small", gs_slots, send_sems, recv_sems, after)
    send_sems, recv_sems, gs_slots, tok = gather_forward_start("gather_pass_start_small", gs_slots)
    after = unit_updates(late[1], late_recv2[1], (tok,))
    gs_all = gather_forward_wait("gather_pass_wait_small", gs_slots, send_sems, recv_sems, after)[0].reshape(-1, 128)
    for n in ("ffn1_w_up", "ffn2_w_up"):
        outs[n] = [jnp.swapaxes(o, 1, 2) for o in outs[n]]
    out_g, out_d, out_m, out_v = ({n: outs[n][k] for n in BIG} for k in range(4))

    small_g = dict(zip(SMALL, _unpack(gs_all, small_shapes)))
    for n in SMALL:
        if n != "conv_w":
            flat = lambda a: a.reshape(-1, 128)
            out_g[n] = small_g[n]
            out_d[n], out_m[n], out_v[n] = (o.reshape(W[n].shape) for o in adamw(flat(W[n]), flat(small_g[n]), flat(M[n]), flat(V[n])))
    cwc = conv_w.shape[2]
    gcw = lax.dynamic_slice_in_dim(small_g["conv_w"], dev * cwc, cwc, axis=2)
    cw2 = lambda a: a.reshape(-1, cwc)
    pad_rows = (-cw2(conv_w).shape[0]) % 8
    padr = lambda a: jnp.pad(cw2(a), ((0, pad_rows), (0, 0)))
    dcw_, mcw_, vcw_ = adamw(padr(conv_w), padr(gcw), padr(M["conv_w"]), padr(V["conv_w"]))
    nrow = cw2(conv_w).shape[0]
    out_g["conv_w"] = gcw
    out_d["conv_w"], out_m["conv_w"], out_v["conv_w"] = (a[:nrow].reshape(conv_w.shape) for a in (dcw_, mcw_, vcw_))

    return (loss, grad_x, *[out_g[n] for n in NAMES], *[out_d[n] for n in NAMES], *[out_m[n] for n in NAMES], *[out_v[n] for n in NAMES])
```

```python
import functools

import jax
import jax.numpy as jnp
from jax import lax
from jax.experimental import pallas as pl
from jax.experimental.pallas import tpu as pltpu

F32, BF16 = jnp.float32, jnp.bfloat16
EPS = 1e-6
LRU_C = 8.0
POOL_WINDOWS = (2, 4, 8, 16)
ADAM_LR, ADAM_B1, ADAM_B2, ADAM_EPS, ADAM_WD, ADAM_STEP = 0.001, 0.9, 0.999, 1e-08, 0.01, 10
N_DEV = 8
N_CHIP = 4
MESH = pl.DeviceIdType.MESH
V7X_VMEM_LIMIT = 56 * 1024 * 1024
ROW_TILE = 512
WIDE_TILE = 1024
SUM_TILE = 2048
ANY = pl.BlockSpec(memory_space=pl.ANY)

_pallas_call = pl.pallas_call


def _cp(*sem):
    return pltpu.CompilerParams(dimension_semantics=sem if sem else None, vmem_limit_bytes=V7X_VMEM_LIMIT)


def _tile(n, t):
    t = min(n, t)
    assert n % t == 0, (n, t)
    return t


def _dot(a, b):
    return jnp.dot(a, b, preferred_element_type=F32)


def _dot_nt(a, b):
    return lax.dot_general(a, b, (((1,), (1,)), ((), ())), preferred_element_type=F32)


def _dot_tn(a, b):
    return lax.dot_general(a, b, (((0,), (0,)), ((), ())), preferred_element_type=F32)


def _rms(xv):
    r = lax.rsqrt(jnp.mean(xv * xv, axis=-1, keepdims=True) + EPS)
    return xv * r, r


def _rms_bwd(dh, xv, gv, dy):
    n, r = _rms(xv)
    dn = dh * gv
    dx = dy + r * (dn - n * jnp.mean(dn * n, axis=-1, keepdims=True))
    return dx, jnp.sum(dh * n, axis=0, keepdims=True)


def _shift_down(x, k, fill=0.0):
    if k == 0:
        return x
    rows = lax.broadcasted_iota(jnp.int32, x.shape, 0)
    return jnp.where(rows >= k, pltpu.roll(x, k, 0), fill)


def _shift_up(x, k, fill=0.0):
    if k == 0:
        return x
    n = x.shape[0]
    rows = lax.broadcasted_iota(jnp.int32, x.shape, 0)
    return jnp.where(rows < n - k, pltpu.roll(x, n - k, 0), fill)


def _sigmoid(x):
    return 0.5 * jnp.tanh(0.5 * x) + 0.5


_GELU_K = 0.7978845608028654
_GELU_C = 0.044715


def _gelu(x):
    th = jnp.tanh(_GELU_K * (x + _GELU_C * x * x * x))
    return 0.5 * x * (1.0 + th), th


def _gelu_grad(x, th):
    return 0.5 * (1.0 + th) + 0.5 * x * (1.0 - th * th) * _GELU_K * (1.0 + 3.0 * _GELU_C * x * x)


def ffn_up(x, g, wup, l):
    T, D = x.shape
    cs = wup.shape[-2]
    tm = _tile(T, WIDE_TILE)
    ni = T // tm

    def body(x_ref, g_ref, wa_ref, wb_ref, h_ref, u_ref, s_ref, hs_ref):
        rows = pl.ds(pl.multiple_of(pl.program_id(1) * tm, tm), tm)

        @pl.when(pl.program_id(0) == 0)
        def _():
            n, _r = _rms(x_ref[...])
            hv = (n * g_ref[0]).astype(BF16)
            hs_ref[rows, :] = hv
            h_ref[...] = hv

        hv = hs_ref[rows, :]
        a = _dot_nt(hv, wa_ref[0, 0])
        b = _dot_nt(hv, wb_ref[0, 0])
        u_ref[0, 0] = a.astype(BF16)
        u_ref[1, 0] = b.astype(BF16)
        s_ref[0] = (a * _sigmoid(a) * b).astype(BF16)

    first = lambda j, i: (jnp.where(j == 0, i, ni - 1), 0)
    return _pallas_call(
        body, name="ffn_up", grid=(4, ni),
        in_specs=[pl.BlockSpec((tm, D), first), pl.BlockSpec((1, 1, D), lambda j, i: (l, 0, 0)),
                  pl.BlockSpec((1, 1, cs, D), lambda j, i: (0, j, 0, 0)), pl.BlockSpec((1, 1, cs, D), lambda j, i: (0, j + 4, 0, 0))],
        out_specs=[pl.BlockSpec((tm, D), first), pl.BlockSpec((2, 1, tm, cs), lambda j, i: (0, j, i, 0)),
                   pl.BlockSpec((1, tm, cs), lambda j, i: (j, i, 0))],
        out_shape=[jax.ShapeDtypeStruct((T, D), BF16), jax.ShapeDtypeStruct((2, 4, T, cs), BF16), jax.ShapeDtypeStruct((4, T, cs), BF16)],
        scratch_shapes=[pltpu.VMEM((T, D), BF16)],
        compiler_params=_cp("arbitrary", "arbitrary"),
    )(x, g, wup, wup)


def ffn_down(s, wd, x, l, deps=()):
    _, T, cs = s.shape
    D = x.shape[1]
    tm = _tile(T, WIDE_TILE)

    def body(s_ref, w_ref, x_ref, *rest):
        o_ref, acc_ref = rest[len(deps):]
        j = pl.program_id(1)

        @pl.when(j == 0)
        def _():
            acc_ref[...] = jnp.zeros_like(acc_ref)

        acc_ref[...] += _dot(s_ref[0], w_ref[0])

        @pl.when(j == 3)
        def _():
            o_ref[...] = x_ref[...] + 0.5 * acc_ref[...]

    return _pallas_call(
        body, name="ffn_down", grid=(T // tm, 4),
        in_specs=[pl.BlockSpec((1, tm, cs), lambda i, j: (j, i, 0)), pl.BlockSpec((1, cs, D), lambda i, j: (0, j, 0)),
                  pl.BlockSpec((tm, D), lambda i, j: (i, 0))] + [ANY] * len(deps),
        out_specs=pl.BlockSpec((tm, D), lambda i, j: (i, 0)),
        out_shape=jax.ShapeDtypeStruct((T, D), F32),
        scratch_shapes=[pltpu.VMEM((tm, D), F32)],
        compiler_params=_cp("parallel", "arbitrary"),
    )(s, wd, x, *deps)


def mix_in(x, g, win, l):
    T, D = x.shape
    ci = win.shape[-1]
    tm = _tile(T, WIDE_TILE)
    ni = T // tm

    def body(x_ref, g_ref, w_ref, h_ref, p_ref, hs_ref):
        rows = pl.ds(pl.multiple_of(pl.program_id(1) * tm, tm), tm)

        @pl.when(pl.program_id(0) == 0)
        def _():
            n, _r = _rms(x_ref[...])
            hv = (n * g_ref[0]).astype(BF16)
            hs_ref[rows, :] = hv
            h_ref[...] = hv

        p_ref[...] = _dot(hs_ref[rows, :], w_ref[0, 0]).astype(BF16)

    first = lambda j, i: (jnp.where(j == 0, i, ni - 1), 0)
    return _pallas_call(
        body, name="mix_in", grid=(N_DEV, ni),
        in_specs=[pl.BlockSpec((tm, D), first), pl.BlockSpec((1, 1, D), lambda j, i: (l, 0, 0)),
                  pl.BlockSpec((1, 1, D, ci), lambda j, i: (0, j, 0, 0))],
        out_specs=[pl.BlockSpec((tm, D), first), pl.BlockSpec((tm, ci), lambda j, i: (i, j))],
        out_shape=[jax.ShapeDtypeStruct((T, D), BF16), jax.ShapeDtypeStruct((T, N_DEV * ci), BF16)],
        scratch_shapes=[pltpu.VMEM((T, D), BF16)],
        compiler_params=_cp("arbitrary", "arbitrary"),
    )(x, g, win)


def _inv_count(T, w):
    t = lax.broadcasted_iota(jnp.int32, (T, 1), 0)
    return 1.0 / jnp.minimum(t + 1, w).astype(F32)


def _pooled(ug, w, inv):
    s = ug
    k = 1
    while k < w:
        s = s + _shift_down(s, k)
        k *= 2
    return s * inv - ug


def pool_fwd(proj, pw, pb, ps, l):
    T = proj.shape[0]
    _, G, gd, _ = pw.shape
    P = G * gd

    def body(u_ref, w_ref, b_ref, s_ref, o_ref):
        for gi in range(G):
            cols = slice(gi * gd, (gi + 1) * gd)
            ug = u_ref[:, cols].astype(F32)
            pooled = _pooled(ug, POOL_WINDOWS[gi], _inv_count(T, POOL_WINDOWS[gi]))
            mixed = _dot(pooled.astype(BF16), w_ref[0, gi].astype(BF16)) + b_ref[0, :, cols]
            o_ref[:, cols] = (mixed * s_ref[0, :, cols]).astype(BF16)

    return _pallas_call(
        body, name="pool_fwd", grid=(1,),
        in_specs=[pl.BlockSpec((T, P), lambda i: (0, 0)), pl.BlockSpec((1, G, gd, gd), lambda i: (l, 0, 0, 0)),
                  pl.BlockSpec((1, 1, P), lambda i: (l, 0, 0)), pl.BlockSpec((1, 1, P), lambda i: (l, 0, 0))],
        out_specs=pl.BlockSpec((T, P), lambda i: (0, 0)),
        out_shape=jax.ShapeDtypeStruct((T, P), BF16),
        compiler_params=_cp("arbitrary"),
    )(proj, pw, pb, ps)


def _conv(u, cw_ref, cb):
    CW = cw_ref.shape[1]
    v = cb
    for k in range(CW):
        v = v + cw_ref[0, k:k + 1, :] * _shift_down(u, CW - 1 - k)
    return v


def _softplus(z):
    return jnp.maximum(z, 0.0) + jnp.log1p(jnp.exp(-jnp.abs(z)))


def _lru_gates(v, wa_ref, ba, wx_ref, bx, lam):
    vb = v.astype(BF16)
    r = _sigmoid(_dot(vb, wa_ref[0, 0].astype(BF16)) + ba)
    i = _sigmoid(_dot(vb, wx_ref[0, 0].astype(BF16)) + bx)
    sp = _softplus(-lam)
    log_a = -LRU_C * r * sp
    a = jnp.exp(log_a)
    m2 = -jnp.tanh(log_a) * (a * a + 1.0)
    inv_mult = lax.rsqrt(m2)
    mult = jnp.where(m2 > 0.0, m2 * inv_mult, 0.0)
    return r, i, sp, a, mult, inv_mult


def _scan_fwd(a_ref, b_ref, o_ref):
    T, W = a_ref.shape
    rows = lax.broadcasted_iota(jnp.int32, (8, W), 0)

    def step(t, carry):
        r0 = pl.multiple_of(t * 8, 8)
        A = a_ref[pl.ds(r0, 8), :]
        B = b_ref[pl.ds(r0, 8), :]
        for s in (1, 2, 4):
            keep = rows >= s
            As = jnp.where(keep, pltpu.roll(A, s, 0), 1.0)
            Bs = jnp.where(keep, pltpu.roll(B, s, 0), 0.0)
            B = A * Bs + B
            A = A * As
        h = B + A * carry
        o_ref[pl.ds(r0, 8), :] = h
        return jnp.broadcast_to(h[7:8, :], (8, W))

    lax.fori_loop(0, T // 8, step, jnp.zeros((8, W), F32), unroll=8)


def _scan_bwd(a_ref, b_ref, o_ref):
    T, W = a_ref.shape
    rows = lax.broadcasted_iota(jnp.int32, (8, W), 0)
    nt = T // 8

    def step(t, carry):
        r0 = pl.multiple_of((nt - 1 - t) * 8, 8)
        A = a_ref[pl.ds(r0, 8), :]
        B = b_ref[pl.ds(r0, 8), :]
        for s in (1, 2, 4):
            keep = rows < 8 - s
            As = jnp.where(keep, pltpu.roll(A, 8 - s, 0), 1.0)
            Bs = jnp.where(keep, pltpu.roll(B, 8 - s, 0), 0.0)
            B = A * Bs + B
            A = A * As
        y = B + A * carry
        o_ref[pl.ds(r0, 8), :] = y
        return jnp.broadcast_to(y[0:1, :], (8, W))

    lax.fori_loop(0, nt, step, jnp.zeros((8, W), F32), unroll=8)


def _lru_specs(T, hd, P, R, CW, l):
    ob, gb = P // hd, (P + R) // hd
    vec = pl.BlockSpec((1, 1, hd), lambda h: (l, 0, h))
    mat = pl.BlockSpec((1, 1, hd, hd), lambda h: (l, h, 0, 0))
    return [pl.BlockSpec((T, hd), lambda h: (0, ob + h)), pl.BlockSpec((T, hd), lambda h: (0, gb + h)),
            pl.BlockSpec((1, CW, hd), lambda h: (0, 0, h)), vec, mat, vec, mat, vec, vec]


def lru_fwd(proj, cw, cb, wa, ba, wx, bx, lam, P, l):
    T = proj.shape[0]
    _, H, hd, _ = wa.shape
    R = H * hd
    CW = cw.shape[1]
    assert P % hd == 0 and T % 8 == 0

    def body(u_ref, ug_ref, cw_ref, cb_ref, wa_ref, ba_ref, wx_ref, bx_ref, lam_ref, hl_ref, hs_ref, a_s, b_s):
        v = _conv(u_ref[...].astype(F32), cw_ref, cb_ref[0])
        _r, i, _sp, a, mult, _im = _lru_gates(v, wa_ref, ba_ref[0], wx_ref, bx_ref[0], lam_ref[0])
        a_s[...] = a
        b_s[...] = mult * (i * v)
        _scan_fwd(a_s, b_s, hs_ref)
        ge, _th = _gelu(ug_ref[...].astype(F32))
        hl_ref[...] = (hs_ref[...] * ge).astype(BF16)

    out = pl.BlockSpec((T, hd), lambda h: (0, h))
    return _pallas_call(
        body, name="lru_fwd", grid=(H,),
        in_specs=_lru_specs(T, hd, P, R, CW, l),
        out_specs=[out, out],
        out_shape=[jax.ShapeDtypeStruct((T, R), BF16), jax.ShapeDtypeStruct((T, R), F32)],
        scratch_shapes=[pltpu.VMEM((T, hd), F32)] * 2,
        compiler_params=_cp("parallel"),
    )(proj, proj, cw, cb, wa, ba, wx, bx, lam)


def mix_out(pm, hl, proj, x, wpu, wlu, wout, P, l, deps=()):
    T, D = x.shape
    R = hl.shape[1]
    tm = _tile(T, ROW_TILE)
    assert (P + 2 * R) % D == 0
    gb = (P + 2 * R) // D

    def body(pm_ref, hl_ref, gp_ref, gl_ref, x_ref, wpu_ref, wlu_ref, wo_ref, *rest):
        o_ref, yp_ref, yl_ref, z_ref = rest[len(deps):]
        yp = _dot(pm_ref[...], wpu_ref[0])
        yl = _dot(hl_ref[...], wlu_ref[0])
        z = (_sigmoid(gp_ref[...].astype(F32)) * yp + _sigmoid(gl_ref[...].astype(F32)) * yl).astype(BF16)
        yp_ref[...] = yp.astype(BF16)
        yl_ref[...] = yl.astype(BF16)
        z_ref[...] = z
        o_ref[...] = x_ref[...] + _dot(z, wo_ref[0])

    row = lambda w: pl.BlockSpec((tm, w), lambda i: (i, 0))
    return _pallas_call(
        body, name="mix_out", grid=(T // tm,),
        in_specs=[row(P), row(R), pl.BlockSpec((tm, D), lambda i: (i, gb)), pl.BlockSpec((tm, D), lambda i: (i, gb + 1)), row(D),
                  pl.BlockSpec((1, P, D), lambda i: (0, 0, 0)), pl.BlockSpec((1, R, D), lambda i: (0, 0, 0)),
                  pl.BlockSpec((1, D, D), lambda i: (0, 0, 0))] + [ANY] * len(deps),
        out_specs=[row(D)] * 4,
        out_shape=[jax.ShapeDtypeStruct((T, D), F32)] + [jax.ShapeDtypeStruct((T, D), BF16)] * 3,
        compiler_params=_cp("parallel"),
    )(pm, hl, proj, proj, x, wpu, wlu, wout, *deps)


def loss_head(x, gf, tgt):
    T, D = x.shape
    tm = _tile(T, ROW_TILE)

    def body(x_ref, g_ref, t_ref, loss_ref, dx_ref, dg_ref):
        @pl.when(pl.program_id(0) == 0)
        def _():
            loss_ref[...] = jnp.zeros_like(loss_ref)
            dg_ref[...] = jnp.zeros_like(dg_ref)

        xv = x_ref[...]
        gv = g_ref[...]
        n, _r = _rms(xv)
        e = n * gv - t_ref[...]
        loss_ref[...] += 0.5 * jnp.sum(jnp.sum(e * e, axis=-1, keepdims=True), axis=0, keepdims=True) / D
        dx, dg = _rms_bwd(e * (1.0 / D), xv, gv, 0.0)
        dx_ref[...] = dx
        dg_ref[...] += dg

    return _pallas_call(
        body, name="loss_head", grid=(T // tm,),
        in_specs=[pl.BlockSpec((tm, D), lambda i: (i, 0)), pl.BlockSpec((1, D), lambda i: (0, 0)), pl.BlockSpec((tm, D), lambda i: (i, 0))],
        out_specs=[pl.BlockSpec((1, 1), lambda i: (0, 0)), pl.BlockSpec((tm, D), lambda i: (i, 0)), pl.BlockSpec((1, D), lambda i: (0, 0))],
        out_shape=[jax.ShapeDtypeStruct((1, 1), F32), jax.ShapeDtypeStruct((T, D), F32), jax.ShapeDtypeStruct((1, D), F32)],
        compiler_params=_cp("arbitrary"),
    )(x, gf, tgt)


def ffn_down_bwd(dy, wd, u, l, deps=()):
    T, D = dy.shape
    cs = u.shape[-1]
    tm = _tile(T, WIDE_TILE)
    ni = T // tm

    def body(dy_ref, w_ref, u_ref, *rest):
        do_ref, du_ref, dyb_ref = rest[len(deps):]
        rows = pl.ds(pl.multiple_of(pl.program_id(1) * tm, tm), tm)

        @pl.when(pl.program_id(0) == 0)
        def _():
            d = (0.5 * dy_ref[...]).astype(BF16)
            dyb_ref[rows, :] = d
            do_ref[...] = d

        ds = _dot_nt(dyb_ref[rows, :], w_ref[0])
        a = u_ref[0, 0].astype(F32)
        b = u_ref[1, 0].astype(F32)
        sg = _sigmoid(a)
        du_ref[0, 0] = (ds * b * (sg * (1.0 + a * (1.0 - sg)))).astype(BF16)
        du_ref[1, 0] = (ds * (a * sg)).astype(BF16)

    first = lambda j, i: (jnp.where(j == 0, i, ni - 1), 0)
    blk = pl.BlockSpec((2, 1, tm, cs), lambda j, i: (0, j, i, 0))
    return _pallas_call(
        body, name="ffn_down_bwd", grid=(4, ni),
        in_specs=[pl.BlockSpec((tm, D), first), pl.BlockSpec((1, cs, D), lambda j, i: (0, j, 0)), blk] + [ANY] * len(deps),
        out_specs=[pl.BlockSpec((tm, D), first), blk],
        out_shape=[jax.ShapeDtypeStruct((T, D), BF16), jax.ShapeDtypeStruct((2, 4, T, cs), BF16)],
        scratch_shapes=[pltpu.VMEM((T, D), BF16)],
        compiler_params=_cp("arbitrary", "arbitrary"),
    )(dy, wd, u, *deps)


def dw_tn(name, a, a_spec, b, b_spec, G, M, N, T):
    tk = _tile(T, SUM_TILE)
    nk = T // tk

    def body(a_ref, b_ref, o32_ref, o16_ref, acc_ref):
        k = pl.program_id(1)

        @pl.when(k == 0)
        def _():
            acc_ref[...] = jnp.zeros_like(acc_ref)

        av = a_ref[0] if len(a_ref.shape) == 3 else a_ref[...]
        bv = b_ref[0] if len(b_ref.shape) == 3 else b_ref[...]
        acc_ref[...] += _dot_tn(av, bv)

        @pl.when(k == nk - 1)
        def _():
            o32_ref[0, 0] = acc_ref[...]
            o16_ref[0, 0] = acc_ref[...].astype(BF16)

    out = pl.BlockSpec((1, 1, M, N), lambda g, k: (0, g, 0, 0))
    return _pallas_call(
        body, name=name, grid=(G, nk),
        in_specs=[a_spec(tk), b_spec(tk)], out_specs=[out, out],
        out_shape=[jax.ShapeDtypeStruct((1, G, M, N), F32), jax.ShapeDtypeStruct((1, G, M, N), BF16)],
        scratch_shapes=[pltpu.VMEM((M, N), F32)],
        compiler_params=_cp("parallel", "arbitrary"),
    )(a, b)


def dx_norm_bwd(name, dact, d_spec, w, G, x, g, dy, l, w_transposed=False):
    T, D = x.shape
    wblk = w.shape[-2:]
    tm = _tile(T, WIDE_TILE)
    ni = T // tm
    ch = _tile(tm, ROW_TILE // 2)

    def body(d_ref, w_ref, x_ref, g_ref, dy_ref, dx_ref, dg_ref, acc_ref):
        j, i = pl.program_id(0), pl.program_id(1)
        rows = pl.ds(pl.multiple_of(i * tm, tm), tm)

        @pl.when(jnp.logical_and(i == 0, j == 0))
        def _():
            dg_ref[...] = jnp.zeros_like(dg_ref)

        @pl.when(j == 0)
        def _():
            acc_ref[rows, :] = jnp.zeros((tm, D), F32)

        dv = d_ref[0] if len(d_ref.shape) == 3 else d_ref[...]
        acc_ref[rows, :] += _dot(dv, w_ref[0, 0]) if w_transposed else _dot_nt(dv, w_ref[0, 0])

        @pl.when(j == G - 1)
        def _():
            dg = jnp.zeros((1, D), F32)
            for c0 in range(0, tm, ch):
                part_rows = pl.ds(pl.multiple_of(i * tm + c0, ch), ch)
                dx, dgc = _rms_bwd(acc_ref[part_rows, :], x_ref[c0:c0 + ch, :], g_ref[0], dy_ref[c0:c0 + ch, :])
                dx_ref[c0:c0 + ch, :] = dx
                dg = dg + dgc
            dg_ref[...] += dg

    last = pl.BlockSpec((tm, D), lambda j, i: (jnp.where(j == G - 1, i, 0), 0))
    return _pallas_call(
        body, name=name, grid=(G, ni),
        in_specs=[d_spec(tm), pl.BlockSpec((1, 1) + wblk, lambda j, i: (0, j, 0, 0)), last, pl.BlockSpec((1, 1, D), lambda j, i: (l, 0, 0)), last],
        out_specs=[last, pl.BlockSpec((1, D), lambda j, i: (0, 0))],
        out_shape=[jax.ShapeDtypeStruct((T, D), F32), jax.ShapeDtypeStruct((1, D), F32)],
        scratch_shapes=[pltpu.VMEM((T, D), F32)],
        compiler_params=_cp("arbitrary", "arbitrary"),
    )(dact, w, x, g, dy)


def mix_out_bwd(dy, proj, yp, yl, wpu, wlu, wout, P, R, l, deps=()):
    T, D = dy.shape
    tm = _tile(T, ROW_TILE)
    gb = (P + 2 * R) // D

    def body(dy_ref, gp_ref, gl_ref, yp_ref, yl_ref, wpu_ref, wlu_ref, wo_ref, *rest):
        dyb_ref, dyp_ref, dyl_ref, dgp_ref, dgl_ref, dpm_ref, dhl_ref = rest[len(deps):]
        dyb = dy_ref[...].astype(BF16)
        dyb_ref[...] = dyb
        dz = _dot_nt(dyb, wo_ref[0])
        sp = _sigmoid(gp_ref[...].astype(F32))
        sl = _sigmoid(gl_ref[...].astype(F32))
        dgp_ref[...] = (dz * yp_ref[...].astype(F32) * sp * (1.0 - sp)).astype(BF16)
        dgl_ref[...] = (dz * yl_ref[...].astype(F32) * sl * (1.0 - sl)).astype(BF16)
        dyp = (dz * sp).astype(BF16)
        dyl = (dz * sl).astype(BF16)
        dyp_ref[...] = dyp
        dyl_ref[...] = dyl
        dpm_ref[...] = _dot_nt(dyp, wpu_ref[0]).astype(BF16)
        dhl_ref[...] = _dot_nt(dyl, wlu_ref[0]).astype(BF16)

    row = lambda w: pl.BlockSpec((tm, w), lambda i: (i, 0))
    return _pallas_call(
        body, name="mix_out_bwd", grid=(T // tm,),
        in_specs=[row(D), pl.BlockSpec((tm, D), lambda i: (i, gb)), pl.BlockSpec((tm, D), lambda i: (i, gb + 1)), row(D), row(D),
                  pl.BlockSpec((1, P, D), lambda i: (0, 0, 0)), pl.BlockSpec((1, R, D), lambda i: (0, 0, 0)),
                  pl.BlockSpec((1, D, D), lambda i: (0, 0, 0))] + [ANY] * len(deps),
        out_specs=[row(D)] * 5 + [row(P), row(R)],
        out_shape=[jax.ShapeDtypeStruct((T, D), BF16)] * 5 + [jax.ShapeDtypeStruct((T, P), BF16), jax.ShapeDtypeStruct((T, R), BF16)],
        compiler_params=_cp("parallel"),
    )(dy, proj, proj, yp, yl, wpu, wlu, wout, *deps)


def lru_bwd(proj, hs, dhl, cw, cb, wa, ba, wx, bx, lam, P, l):
    T = proj.shape[0]
    _, H, hd, _ = wa.shape
    R = H * hd
    CW = cw.shape[1]

    def body(u_ref, ug_ref, cw_ref, cb_ref, wa_ref, ba_ref, wx_ref, bx_ref, lam_ref, hs_ref, dhl_ref,
             du_ref, dug_ref, dcw_ref, dcb_ref, dwa_ref, dba_ref, dwx_ref, dbx_ref, dlam_ref, c_s, g_s, y_s):
        u = u_ref[...].astype(F32)
        v = _conv(u, cw_ref, cb_ref[0])
        lam = lam_ref[0]
        r, i, sp, a, mult, inv_mult = _lru_gates(v, wa_ref, ba_ref[0], wx_ref, bx_ref[0], lam)
        ug = ug_ref[...].astype(F32)
        ge, th = _gelu(ug)
        hs = hs_ref[...]
        dhl = dhl_ref[...].astype(F32)
        dug_ref[...] = (dhl * hs * _gelu_grad(ug, th)).astype(BF16)
        c_s[...] = _shift_up(a, 1)
        g_s[...] = dhl * ge
        _scan_bwd(c_s, g_s, y_s)
        y = y_s[...]
        da = y * _shift_down(hs, 1)
        iv = i * v
        dlog_a = da * a - (y * iv) * (a * a) * inv_mult
        div = y * mult
        dpa = (dlog_a * (-LRU_C) * sp) * r * (1.0 - r)
        dpx = (div * v) * i * (1.0 - i)
        dsp = jnp.sum(dlog_a * (-LRU_C) * r, axis=0, keepdims=True)
        dlam_ref[0] = -dsp * _sigmoid(-lam)
        vb = v.astype(BF16)
        dpab, dpxb = dpa.astype(BF16), dpx.astype(BF16)
        dwa_ref[0, 0] = _dot_tn(vb, dpab)
        dwx_ref[0, 0] = _dot_tn(vb, dpxb)
        dba_ref[0] = jnp.sum(dpa, axis=0, keepdims=True)
        dbx_ref[0] = jnp.sum(dpx, axis=0, keepdims=True)
        dv = div * i + _dot_nt(dpab, wa_ref[0, 0].astype(BF16)) + _dot_nt(dpxb, wx_ref[0, 0].astype(BF16))
        dcb_ref[0] = jnp.sum(dv, axis=0, keepdims=True)
        du = jnp.zeros_like(dv)
        for k in range(CW):
            du = du + cw_ref[0, k:k + 1, :] * _shift_up(dv, CW - 1 - k)
            dcw_ref[0, k:k + 1, :] = jnp.sum(dv * _shift_down(u, CW - 1 - k), axis=0, keepdims=True)
        du_ref[...] = du.astype(BF16)

    col = pl.BlockSpec((T, hd), lambda h: (0, h))
    vec = pl.BlockSpec((1, 1, hd), lambda h: (0, 0, h))
    mat = pl.BlockSpec((1, 1, hd, hd), lambda h: (0, h, 0, 0))
    vshape = jax.ShapeDtypeStruct((1, 1, R), F32)
    mshape = jax.ShapeDtypeStruct((1, H, hd, hd), F32)
    return _pallas_call(
        body, name="lru_bwd", grid=(H,),
        in_specs=_lru_specs(T, hd, P, R, CW, l) + [col, col],
        out_specs=[col, col, pl.BlockSpec((1, CW, hd), lambda h: (0, 0, h)), vec, mat, vec, mat, vec, vec],
        out_shape=[jax.ShapeDtypeStruct((T, R), BF16)] * 2 + [jax.ShapeDtypeStruct((1, CW, R), F32), vshape, mshape, vshape, mshape, vshape, vshape],
        scratch_shapes=[pltpu.VMEM((T, hd), F32)] * 3,
        compiler_params=_cp("parallel"),
    )(proj, proj, cw, cb, wa, ba, wx, bx, lam, hs, dhl)


def pool_bwd(proj, dpm, pw, pb, ps, l):
    T = proj.shape[0]
    _, G, gd, _ = pw.shape
    P = G * gd

    def body(u_ref, d_ref, w_ref, b_ref, s_ref, du_ref, dw_ref, db_ref, dsc_ref):
        for gi in range(G):
            cols = slice(gi * gd, (gi + 1) * gd)
            w = POOL_WINDOWS[gi]
            inv = _inv_count(T, w)
            ug = u_ref[:, cols].astype(F32)
            pooled = _pooled(ug, w, inv).astype(BF16)
            wb = w_ref[0, gi].astype(BF16)
            mixed = _dot(pooled, wb) + b_ref[0, :, cols]
            dpm_g = d_ref[:, cols].astype(F32)
            dsc_ref[0, :, cols] = jnp.sum(dpm_g * mixed, axis=0, keepdims=True)
            dmixed = dpm_g * s_ref[0, :, cols]
            db_ref[0, :, cols] = jnp.sum(dmixed, axis=0, keepdims=True)
            dmb = dmixed.astype(BF16)
            dw_ref[0, gi] = _dot_tn(pooled, dmb)
            dpooled = _dot_nt(dmb, wb)
            s = dpooled * inv
            k = 1
            while k < w:
                s = s + _shift_up(s, k)
                k *= 2
            du_ref[:, cols] = (s - dpooled).astype(BF16)

    vec = pl.BlockSpec((1, 1, P), lambda i: (l, 0, 0))
    ovec = pl.BlockSpec((1, 1, P), lambda i: (0, 0, 0))
    return _pallas_call(
        body, name="pool_bwd", grid=(1,),
        in_specs=[pl.BlockSpec((T, P), lambda i: (0, 0)), pl.BlockSpec((T, P), lambda i: (0, 0)),
                  pl.BlockSpec((1, G, gd, gd), lambda i: (l, 0, 0, 0)), vec, vec],
        out_specs=[pl.BlockSpec((T, P), lambda i: (0, 0)), pl.BlockSpec((1, G, gd, gd), lambda i: (0, 0, 0, 0)), ovec, ovec],
        out_shape=[jax.ShapeDtypeStruct((T, P), BF16), jax.ShapeDtypeStruct((1, G, gd, gd), F32),
                   jax.ShapeDtypeStruct((1, 1, P), F32), jax.ShapeDtypeStruct((1, 1, P), F32)],
        compiler_params=_cp("arbitrary"),
    )(proj, dpm, pw, pb, ps)


def _place():
    x, y, c = lax.axis_index("x"), lax.axis_index("y"), lax.axis_index("c")
    return x, y, c


HBM = pl.BlockSpec(memory_space=pltpu.HBM)
SEM = pl.BlockSpec(memory_space=pltpu.SEMAPHORE)
EFFECT = pltpu.SideEffectType.DATAFLOW_SIDE_EFFECTING


def _in_hbm(a):
    return pltpu.with_memory_space_constraint(a, pltpu.HBM)


def split_start(name, bufs, n_copies, copies_of, deps=()):
    nb = len(bufs)

    def body(*refs):
        buf = refs[:nb]
        send_sems, recv_sems = refs[nb + len(deps)], refs[nb + len(deps) + 1]
        token = refs[-1]
        for i, (src, dst, dev) in enumerate(copies_of(buf)):
            pltpu.make_async_remote_copy(src_ref=src, dst_ref=dst, send_sem=send_sems.at[i], recv_sem=recv_sems.at[i],
                                         device_id=dev, device_id_type=MESH).start()
        token[...] = jnp.zeros_like(token)

    outs = _pallas_call(
        body, name=name,
        in_specs=[HBM] * nb + [ANY] * len(deps),
        out_specs=(SEM, SEM, *([HBM] * nb), pl.BlockSpec(memory_space=pltpu.VMEM)),
        out_shape=(pltpu.SemaphoreType.DMA((n_copies,)), pltpu.SemaphoreType.DMA((n_copies,)),
                   *[pltpu.HBM(b.shape, b.dtype) for b in bufs], jax.ShapeDtypeStruct((8, 128), F32)),
        input_output_aliases={i: 2 + i for i in range(nb)},
        compiler_params=pltpu.CompilerParams(has_side_effects=EFFECT),
    )(*[_in_hbm(b) for b in bufs], *deps)
    return outs[0], outs[1], list(outs[2:2 + nb]), outs[-1]


def split_wait(name, bufs, send_sems, recv_sems, after, copies_of):
    nb = len(bufs)

    def body(*refs):
        buf = refs[:nb]
        send, recv = refs[nb], refs[nb + 1]
        for i, (src, dst, dev) in enumerate(copies_of(buf)):
            cp = pltpu.make_async_remote_copy(src_ref=src, dst_ref=dst, send_sem=send.at[i], recv_sem=recv.at[i],
                                              device_id=dev, device_id_type=MESH)
            cp.wait_send()
            cp.wait_recv()

    outs = _pallas_call(
        body, name=name,
        in_specs=[HBM] * nb + [SEM, SEM] + [ANY] * len(after),
        out_specs=[HBM] * nb,
        out_shape=[pltpu.HBM(b.shape, b.dtype) for b in bufs],
        input_output_aliases={i: i for i in range(nb)},
        compiler_params=pltpu.CompilerParams(has_side_effects=EFFECT),
    )(*bufs, send_sems, recv_sems, *after)
    return list(outs)


def place_own(w, l, place, dtype):
    _, rows, cols = w.shape
    tr = _rows_tile(rows, cols, 1 << 19)

    def body(p_ref, w_ref, o_ref):
        o_ref[0] = w_ref[0].astype(dtype)

    return _pallas_call(
        body, name="place_own",
        grid_spec=pltpu.PrefetchScalarGridSpec(
            num_scalar_prefetch=1, grid=(rows // tr,),
            in_specs=[pl.BlockSpec((1, tr, cols), lambda i, p: (l, i, 0))],
            out_specs=pl.BlockSpec((1, tr, cols), lambda i, p: (p[2], i, 0))),
        out_shape=jax.ShapeDtypeStruct((N_DEV, rows, cols), dtype), compiler_params=_cp("parallel"),
    )(place, w)


def _gather_copies(land):
    x, y, c = _place()
    k = 4 * x + 2 * y + c
    peers = [(x, 1 - y, c), (1 - x, y, c), (1 - x, 1 - y, c), (x, y, 1 - c)]
    return [(b.at[k], b.at[k], p) for p in peers for b in land]


def gather_start(name, land, deps=()):
    return split_start(name, land, 4 * len(land), _gather_copies, deps)


def gather_wait(name, land, send_sems, recv_sems, after):
    return split_wait(name, land, send_sems, recv_sems, after, _gather_copies)


def _forward_copies(land):
    x, y, c = _place()
    slots = [4 * px + 2 * py + c for px, py in [(x, 1 - y), (1 - x, y), (1 - x, 1 - y)]]
    return [(b.at[k], b.at[k], (x, y, 1 - c)) for k in slots for b in land]


def gather_forward_start(name, land, deps=()):
    return split_start(name, land, 3 * len(land), _forward_copies, deps)


def gather_forward_wait(name, land, send_sems, recv_sems, after):
    return split_wait(name, land, send_sems, recv_sems, after, _forward_copies)


def _chip_copies(nsrc):
    def copies(buf):
        p16, recv2 = buf[:nsrc], buf[nsrc:]
        x, y, c = _place()
        out = []
        for d in (1, 2, 3):
            px = 1 - x if d & 2 else x
            py = 1 - y if d & 1 else y
            out += [(p16[a].at[:, d - 1], recv2[a].at[:, d - 1], (px, py, c)) for a in range(nsrc)]
        return out
    return copies


def _pair_copies(nsrc):
    def copies(buf):
        g16, recv = buf[:nsrc], buf[nsrc:]
        x, y, c = _place()
        return [(g16[a].at[:, 2 * j + 1 - c], recv[a].at[:, j], (x, y, 1 - c)) for a in range(nsrc) for j in range(N_CHIP)]
    return copies


def pair_exchange_start(name, g16, deps=()):
    n = len(g16)
    land = [lax.empty((1, N_CHIP) + s.shape[2:], s.dtype) for s in g16]
    return split_start(name, list(g16) + land, N_CHIP * n, _pair_copies(n), deps)


def pair_exchange_wait(name, bufs, send_sems, recv_sems, after):
    n = len(bufs) // 2
    return split_wait(name, bufs, send_sems, recv_sems, after, _pair_copies(n))[n:]


def chip_exchange_start(name, pair16, deps=()):
    n = len(pair16)
    land = [lax.empty((s.shape[0], 3) + s.shape[2:], s.dtype) for s in pair16]
    return split_start(name, list(pair16) + land, 3 * n, _chip_copies(n), deps)


def chip_exchange_wait(name, bufs, send_sems, recv_sems, after):
    n = len(bufs) // 2
    return split_wait(name, bufs, send_sems, recv_sems, after, _chip_copies(n))[n:]


def _rows_tile(rows, cols, budget=1 << 20):
    t = rows
    while t % 2 == 0 and t * cols > budget and (t // 2) % 16 == 0:
        t //= 2
    return t


def pair_sum(g32, recv1, place, l):
    _, _, rows, cols = recv1.shape
    tr = _rows_tile(rows, cols)

    def body(p_ref, m_ref, r_ref, o_ref):
        o_ref[...] = (m_ref[...] + r_ref[...].astype(F32)).astype(o_ref.dtype)

    other = lambda d, p: jnp.bitwise_xor(p[1], d + 1)
    return _pallas_call(
        body, name="pair_sum",
        grid_spec=pltpu.PrefetchScalarGridSpec(
            num_scalar_prefetch=1, grid=(N_CHIP - 1, rows // tr),
            in_specs=[pl.BlockSpec((1, 1, tr, cols), lambda d, i, p: (l, 2 * other(d, p) + p[0], i, 0)),
                      pl.BlockSpec((1, 1, tr, cols), lambda d, i, p: (0, other(d, p), i, 0))],
            out_specs=pl.BlockSpec((1, 1, tr, cols), lambda d, i, p: (0, d, i, 0))),
        out_shape=jax.ShapeDtypeStruct((1, N_CHIP - 1, rows, cols), recv1.dtype), compiler_params=_cp("parallel", "parallel"),
    )(place, g32, recv1)


def _grad_in_specs(tr, cols, l):
    return ([pl.BlockSpec((1, 1, tr, cols), lambda i, p: (l, p[2], i, 0)), pl.BlockSpec((1, 1, tr, cols), lambda i, p: (0, p[1], i, 0))]
            + [pl.BlockSpec((1, 1, tr, cols), lambda i, p, d=d: (0, d, i, 0)) for d in range(3)])


def _grad_total(o32, o16, r0, r1, r2):
    return (o32[0, 0] + o16[0, 0].astype(F32)) + r0[0, 0].astype(F32) + r1[0, 0].astype(F32) + r2[0, 0].astype(F32)


def grad_sum(g32, recv1, recv2, place):
    _, _, rows, cols = recv1.shape
    tr = _rows_tile(rows, cols)

    def body(p_ref, o32, o16, r0, r1, r2, g_ref):
        g_ref[...] = _grad_total(o32, o16, r0, r1, r2)

    return _pallas_call(
        body, name="grad_sum",
        grid_spec=pltpu.PrefetchScalarGridSpec(
            num_scalar_prefetch=1, grid=(rows // tr,), in_specs=_grad_in_specs(tr, cols, 0),
            out_specs=pl.BlockSpec((tr, cols), lambda i, p: (i, 0))),
        out_shape=jax.ShapeDtypeStruct((rows, cols), F32), compiler_params=_cp("parallel"),
    )(place, g32, recv1, recv2, recv2, recv2)


def _adamw_math(w, g, m, v):
    m = ADAM_B1 * m + (1.0 - ADAM_B1) * g
    v = ADAM_B2 * v + (1.0 - ADAM_B2) * (g * g)
    m_hat = m / (1.0 - ADAM_B1 ** ADAM_STEP)
    v_hat = v / (1.0 - ADAM_B2 ** ADAM_STEP)
    delta = -ADAM_LR * (m_hat / (jnp.sqrt(v_hat) + ADAM_EPS) + ADAM_WD * w)
    return delta, m, v


def grad_sum_adamw(g32, recv1, recv2, w, m, v, place, l, prev, deps=()):
    L, rows, cols = w.shape
    tr = _rows_tile(rows, cols, 1 << 19)

    def body(p_ref, o32, o16, r0, r1, r2, w_ref, m_ref, v_ref, *rest):
        g_ref, d_ref, nm_ref, nv_ref = rest[-4:]
        g = _grad_total(o32, o16, r0, r1, r2)
        d, nm, nv = _adamw_math(w_ref[0], g, m_ref[0], v_ref[0])
        g_ref[0] = g
        d_ref[0] = d
        nm_ref[0] = nm
        nv_ref[0] = nv

    blk = pl.BlockSpec((1, tr, cols), lambda i, p: (l, i, 0))
    args = [g32, recv1, recv2, recv2, recv2, w, m, v]
    in_specs = _grad_in_specs(tr, cols, 0) + [blk] * 3
    aliases = {}
    if prev is not None:
        aliases = {1 + len(args) + k: k for k in range(4)}
        args += list(prev)
        in_specs += [ANY] * 4
    args += list(deps)
    in_specs += [ANY] * len(deps)
    return _pallas_call(
        body, name="grad_sum_adamw",
        grid_spec=pltpu.PrefetchScalarGridSpec(num_scalar_prefetch=1, grid=(rows // tr,), in_specs=in_specs, out_specs=[blk] * 4),
        out_shape=[jax.ShapeDtypeStruct((L, rows, cols), F32)] * 4, input_output_aliases=aliases,
        compiler_params=_cp("parallel"),
    )(place, *args)


def adamw(w, g, m, v):
    rows, cols = w.shape
    tr = _rows_tile(rows, cols, 1 << 18)

    def body(w_ref, g_ref, m_ref, v_ref, d_ref, nm_ref, nv_ref):
        d, nm, nv = _adamw_math(w_ref[...], g_ref[...], m_ref[...], v_ref[...])
        d_ref[...] = d
        nm_ref[...] = nm
        nv_ref[...] = nv

    blk = pl.BlockSpec((tr, cols), lambda i: (i, 0))
    return _pallas_call(body, name="adamw_small", grid=(rows // tr,), in_specs=[blk] * 4, out_specs=[blk] * 3,
                        out_shape=[jax.ShapeDtypeStruct((rows, cols), F32)] * 3, compiler_params=_cp("parallel"))(w, g, m, v)


SMALL = ("norm_ffn1", "norm_mix", "pool_w", "pool_b", "pool_scale", "conv_w", "conv_b", "lru_w_a", "lru_b_a", "lru_w_x", "lru_b_x",
         "lru_lambda", "norm_ffn2", "final_norm")
BIG = ("ffn1_w_up", "ffn1_w_down", "w_in", "w_pool_up", "w_lru_up", "w_out", "ffn2_w_up", "ffn2_w_down")
NAMES = ("norm_ffn1", "ffn1_w_up", "ffn1_w_down", "norm_mix", "w_in", "pool_w", "pool_b", "pool_scale", "w_pool_up", "conv_w", "conv_b",
         "lru_w_a", "lru_b_a", "lru_w_x", "lru_b_x", "lru_lambda", "w_lru_up", "w_out", "norm_ffn2", "ffn2_w_up", "ffn2_w_down", "final_norm")
SUBLAYERS = (("ffn1_w_up", "ffn1_w_down"), ("w_in", "w_pool_up", "w_lru_up", "w_out", "conv_w"), ("ffn2_w_up", "ffn2_w_down"))
PACK_ROWS = 16 * N_DEV


def _pack(parts):
    flat = jnp.concatenate([p.reshape(-1) for p in parts])
    unit = 128 * PACK_ROWS
    padded = -(-flat.size // unit) * unit
    return jnp.pad(flat, (0, padded - flat.size)).reshape(-1, 128)


def _unpack(packed, shapes):
    flat = packed.reshape(-1)
    out, off = [], 0
    for s in shapes:
        n = 1
        for d in s:
            n *= d
        out.append(flat[off:off + n].reshape(s))
        off += n
    return out


def kernel(x, norm_ffn1, ffn1_w_up, ffn1_w_down, norm_mix, w_in, pool_w, pool_b, pool_scale, w_pool_up, conv_w, conv_b, lru_w_a, lru_b_a, lru_w_x, lru_b_x, lru_lambda, w_lru_up, w_out, norm_ffn2, ffn2_w_up, ffn2_w_down, final_norm, loss_target, m_norm_ffn1, m_ffn1_w_up, m_ffn1_w_down, m_norm_mix, m_w_in, m_pool_w, m_pool_b, m_pool_scale, m_w_pool_up, m_conv_w, m_conv_b, m_lru_w_a, m_lru_b_a, m_lru_w_x, m_lru_b_x, m_lru_lambda, m_w_lru_up, m_w_out, m_norm_ffn2, m_ffn2_w_up, m_ffn2_w_down, m_final_norm, v_norm_ffn1, v_ffn1_w_up, v_ffn1_w_down, v_norm_mix, v_w_in, v_pool_w, v_pool_b, v_pool_scale, v_w_pool_up, v_conv_w, v_conv_b, v_lru_w_a, v_lru_b_a, v_lru_w_x, v_lru_b_x, v_lru_lambda, v_w_lru_up, v_w_out, v_norm_ffn2, v_ffn2_w_up, v_ffn2_w_down, v_final_norm):
    W = dict(norm_ffn1=norm_ffn1, ffn1_w_up=ffn1_w_up, ffn1_w_down=ffn1_w_down, norm_mix=norm_mix, w_in=w_in, pool_w=pool_w, pool_b=pool_b,
             pool_scale=pool_scale, w_pool_up=w_pool_up, conv_w=conv_w, conv_b=conv_b, lru_w_a=lru_w_a, lru_b_a=lru_b_a, lru_w_x=lru_w_x,
             lru_b_x=lru_b_x, lru_lambda=lru_lambda, w_lru_up=w_lru_up, w_out=w_out, norm_ffn2=norm_ffn2, ffn2_w_up=ffn2_w_up,
             ffn2_w_down=ffn2_w_down, final_norm=final_norm)
    M = dict(norm_ffn1=m_norm_ffn1, ffn1_w_up=m_ffn1_w_up, ffn1_w_down=m_ffn1_w_down, norm_mix=m_norm_mix, w_in=m_w_in, pool_w=m_pool_w,
             pool_b=m_pool_b, pool_scale=m_pool_scale, w_pool_up=m_w_pool_up, conv_w=m_conv_w, conv_b=m_conv_b, lru_w_a=m_lru_w_a,
             lru_b_a=m_lru_b_a, lru_w_x=m_lru_w_x, lru_b_x=m_lru_b_x, lru_lambda=m_lru_lambda, w_lru_up=m_w_lru_up, w_out=m_w_out,
             norm_ffn2=m_norm_ffn2, ffn2_w_up=m_ffn2_w_up, ffn2_w_down=m_ffn2_w_down, final_norm=m_final_norm)
    V = dict(norm_ffn1=v_norm_ffn1, ffn1_w_up=v_ffn1_w_up, ffn1_w_down=v_ffn1_w_down, norm_mix=v_norm_mix, w_in=v_w_in, pool_w=v_pool_w,
             pool_b=v_pool_b, pool_scale=v_pool_scale, w_pool_up=v_w_pool_up, conv_w=v_conv_w, conv_b=v_conv_b, lru_w_a=v_lru_w_a,
             lru_b_a=v_lru_b_a, lru_w_x=v_lru_w_x, lru_b_x=v_lru_b_x, lru_lambda=v_lru_lambda, w_lru_up=v_w_lru_up, w_out=v_w_out,
             norm_ffn2=v_norm_ffn2, ffn2_w_up=v_ffn2_w_up, ffn2_w_down=v_ffn2_w_down, final_norm=v_final_norm)

    for S in (W, M, V):
        for n in ("ffn1_w_up", "ffn2_w_up"):
            S[n] = jnp.swapaxes(S[n], 1, 2)

    T, D = x.shape[1], x.shape[2]
    L = norm_ffn1.shape[0]
    P = pool_scale.shape[1]
    R = lru_lambda.shape[1]
    H, hd = lru_w_a.shape[1], lru_w_a.shape[2]
    CW = conv_w.shape[1]
    cs = ffn1_w_up.shape[2]
    ci = w_in.shape[2]
    xin = x.reshape(T, D)
    tgt = loss_target.reshape(T, D)
    dev = 4 * lax.axis_index("x") + 2 * lax.axis_index("y") + lax.axis_index("c")
    place = jnp.stack([lax.axis_index("c"), 2 * lax.axis_index("x") + lax.axis_index("y"), dev]).astype(jnp.int32)

    cw_flat = conv_w.reshape(L, -1)
    cw_pad = (-cw_flat.shape[1]) % 1024
    cw_tiles = jnp.pad(cw_flat, ((0, 0), (0, cw_pad))).reshape(L, -1, 128)

    def units(l):
        return SUBLAYERS if l == 0 else (tuple(n for u in SUBLAYERS for n in u),)

    queued = {"gather": (), "pair": (), "chip": ()}

    gathering = []

    def gather_units_start(l):
        for k, names in enumerate(SUBLAYERS):
            land = [place_own(cw_tiles, l, place, F32) if n == "conv_w" else place_own(W[n], l, place, BF16) for n in names]
            send_sems, recv_sems, land, tok = gather_start(f"gather_start_l{l}_u{k}", land, queued["gather"])
            gathering.append(dict(names=names, tag=f"l{l}_u{k}", send=send_sems, recv=recv_sems, land=land, tok=tok, arrived=False))
            queued["gather"] = (tok,)

    def gather_unit_arrive(after):
        waiting = [u for u in gathering if not u["arrived"]]
        if not waiting:
            return ()
        unit, tokens = waiting[0], [u["tok"] for u in waiting[1:]]
        land = gather_wait(f"gather_wait_{unit['tag']}", unit["land"], unit["send"], unit["recv"], list(after) + tokens)
        send_sems, recv_sems, land, tok = gather_forward_start(f"gather_pass_start_{unit['tag']}", land)
        unit.update(land=land, send=send_sems, recv=recv_sems, tok=tok, arrived=True)
        return (tok,)

    def gather_unit_weights(after):
        if not gathering[0]["arrived"]:
            gather_unit_arrive(after)
        unit = gathering.pop(0)
        land = gather_forward_wait(f"gather_pass_wait_{unit['tag']}", unit["land"], unit["send"], unit["recv"], after)
        g = dict(zip(unit["names"], land))
        one = lambda a: a.reshape((1,) + a.shape)
        w = {}
        for tag_, up, dn in (("1", "ffn1_w_up", "ffn1_w_down"), ("2", "ffn2_w_up", "ffn2_w_down")):
            if up in g:
                w["wup" + tag_], w["wd" + tag_] = one(g[up]), g[dn].reshape(1, -1, D)
        if "w_in" in g:
            cw_l = g["conv_w"].reshape(N_DEV, -1)[:, :cw_flat.shape[1]].reshape((N_DEV,) + conv_w.shape[1:])
            w.update(win=one(g["w_in"]), wlu=g["w_lru_up"].reshape(1, R, D), wout=g["w_out"].reshape(1, D, D),
                     wpu=g["w_pool_up"].transpose(1, 0, 2).reshape(1, P, D),
                     cw=cw_l.transpose(1, 0, 2).reshape(1, CW, R))
        return w

    vec = lambda a: a.reshape(L, 1, -1)
    p = dict(g1=vec(norm_ffn1), gm=vec(norm_mix), g2=vec(norm_ffn2), pb=vec(pool_b), ps=vec(pool_scale), cb=vec(conv_b),
             ba=vec(lru_b_a), bx=vec(lru_b_x), lam=vec(lru_lambda), pw=pool_w, wa=lru_w_a, wx=lru_w_x)

    AHEAD = 2
    for l in range(min(AHEAD, L)):
        gather_units_start(l)
    saved, LW = [], []
    xc = xin
    for l in range(L):
        w = gather_unit_weights([xc])
        if l + AHEAD < L:
            gather_units_start(l + AHEAD)
        sv = {"x1": xc}
        sv["h1"], sv["u1"], sv["s1"] = ffn_up(xc, p["g1"], w["wup1"], l)
        xc = ffn_down(sv["s1"], w["wd1"], xc, l, gather_unit_arrive([sv["s1"]]))
        sv["x2"] = xc
        w.update(gather_unit_weights([xc]))
        sv["h2"], sv["proj"] = mix_in(xc, p["gm"], w["win"], l)
        sv["pm"] = pool_fwd(sv["proj"], p["pw"], p["pb"], p["ps"], l)
        sv["hl"], sv["hs"] = lru_fwd(sv["proj"], w["cw"], p["cb"], p["wa"], p["ba"], p["wx"], p["bx"], p["lam"], P, l)
        xc, sv["yp"], sv["yl"], sv["z"] = mix_out(sv["pm"], sv["hl"], sv["proj"], xc, w["wpu"], w["wlu"], w["wout"], P, l,
                                                  gather_unit_arrive([sv["hl"]]))
        sv["x3"] = xc
        w.update(gather_unit_weights([xc]))
        sv["h3"], sv["u3"], sv["s3"] = ffn_up(xc, p["g2"], w["wup2"], l)
        xc = ffn_down(sv["s3"], w["wd2"], xc, l, gather_unit_arrive([sv["s3"]]))
        saved.append(sv)
        LW.append(w)

    loss_part, dx, d_final = loss_head(xc, final_norm.reshape(1, D), tgt)
    loss = lax.psum(loss_part[0, 0], ("x", "y", "c"))

    G = [dict() for _ in range(L)]
    small = {n: [None] * L for n in SMALL if n != "final_norm"}

    def to_slots(name, pair):
        if name == "w_pool_up":
            return tuple(a.reshape(1, P, N_DEV, D // N_DEV).transpose(0, 2, 1, 3) for a in pair)
        return tuple(a.reshape((1, N_DEV) + W[name].shape[1:]) for a in pair)

    def ffn_bwd(dy, sv, tag, wup, wd, gn, up_name, dn_name, norm_name, l, deps=()):
        dout, du = ffn_down_bwd(dy, wd, sv["u" + tag], l, deps)
        du = du.reshape(N_DEV, T, cs)
        G[l][dn_name] = to_slots(dn_name, dw_tn("dw_down", sv["s" + tag], lambda tk: pl.BlockSpec((1, tk, cs), lambda g, k: (g, k, 0)),
                                                dout, lambda tk: pl.BlockSpec((tk, D), lambda g, k: (k, 0)), 4, cs, D, T))
        G[l][up_name] = to_slots(up_name, dw_tn("dw_up", du, lambda tk: pl.BlockSpec((1, tk, cs), lambda g, k: (g, k, 0)),
                                                sv["h" + tag], lambda tk: pl.BlockSpec((tk, D), lambda g, k: (k, 0)), N_DEV, cs, D, T))
        dxn, dg = dx_norm_bwd("ffn_dx", du, lambda tm: pl.BlockSpec((1, tm, cs), lambda j, i: (j, i, 0)), wup, N_DEV,
                              sv["x" + tag], gn, dy, l, w_transposed=True)
        small[norm_name][l] = dg.reshape(D)
        return dxn

    pairing, in_flight = [], []

    def reduce_start(l, names, tag):
        names = [n for n in names if n != "conv_w"]
        send_sems, recv_sems, bufs, tok = pair_exchange_start(f"rs_pair_start_{tag}", [G[l][n][1] for n in names], queued["pair"])
        pairing.append((l, names, tag, send_sems, recv_sems, bufs))
        queued["pair"] = (tok,)
        return (tok,)

    def reduce_continue(after):
        l, names, tag, send_sems, recv_sems, bufs = pairing.pop(0)
        recv1 = pair_exchange_wait(f"rs_pair_wait_{tag}", bufs, send_sems, recv_sems, after)
        pair16 = [pair_sum(G[l][n][0], r_, place, 0) for n, r_ in zip(names, recv1)]
        send_sems, recv_sems, bufs, tok = chip_exchange_start(f"rs_chip_start_{tag}", pair16, queued["chip"])
        in_flight.append((l, names, tag, send_sems, recv_sems, bufs, recv1))
        queued["chip"] = (tok,)
        return (tok,)

    def boundary(l, k, dx_now):
        deps = reduce_continue([dx_now]) if pairing else ()
        if len(units(l)) > 1:
            deps += reduce_start(l, units(l)[k], f"l{l}_u{k}")
        elif k == 0:
            deps += reduce_start(l, units(l)[0], f"l{l}_u0")
        return deps

    deps = ()
    for l in reversed(range(L)):
        sv, w = saved[l], LW[l]
        dx = ffn_bwd(dx, sv, "3", w["wup2"], w["wd2"], p["g2"], "ffn2_w_up", "ffn2_w_down", "norm_ffn2", l, deps)
        deps = boundary(l, 2, dx)
        dyb, dyp, dyl, dgp, dgl, dpm, dhl = mix_out_bwd(dx, sv["proj"], sv["yp"], sv["yl"], w["wpu"], w["wlu"], w["wout"], P, R, l, deps)
        row = lambda wd_: (lambda tk: pl.BlockSpec((tk, wd_), lambda g, k: (k, 0)))
        G[l]["w_out"] = to_slots("w_out", dw_tn("dw_out", sv["z"], row(D), dyb, row(D), 1, D, D, T))
        G[l]["w_lru_up"] = to_slots("w_lru_up", dw_tn("dw_lru_up", sv["hl"], row(R), dyl, row(D), 1, R, D, T))
        G[l]["w_pool_up"] = to_slots("w_pool_up", dw_tn("dw_pool_up", sv["pm"], row(P), dyp, row(D), 1, P, D, T))
        du_lru, du_gelu, dcw, dcb, dwa, dba, dwx, dbx, dlam = lru_bwd(
            sv["proj"], sv["hs"], dhl, w["cw"], p["cb"], p["wa"], p["ba"], p["wx"], p["bx"], p["lam"], P, l)
        du_pool, dpw, dpb, dpsc = pool_bwd(sv["proj"], dpm, p["pw"], p["pb"], p["ps"], l)
        dproj = jnp.concatenate([du_pool, du_lru, du_gelu, dgp, dgl], axis=1)
        G[l]["w_in"] = to_slots("w_in", dw_tn("dw_in", sv["h2"], row(D), dproj, lambda tk: pl.BlockSpec((tk, ci), lambda g, k: (k, g)),
                                              N_DEV, D, ci, T))
        dx, dgm = dx_norm_bwd("mix_dx", dproj, lambda tm: pl.BlockSpec((tm, ci), lambda j, i: (i, j)), w["win"], N_DEV,
                              sv["x2"], p["gm"], dx, l)
        small["norm_mix"][l] = dgm.reshape(D)
        small["pool_w"][l], small["pool_b"][l], small["pool_scale"][l] = dpw[0], dpb.reshape(pool_b.shape[1:]), dpsc.reshape(P)
        small["conv_w"][l], small["conv_b"][l] = dcw[0], dcb.reshape(R)
        small["lru_w_a"][l], small["lru_b_a"][l] = dwa[0], dba.reshape(H, hd)
        small["lru_w_x"][l], small["lru_b_x"][l] = dwx[0], dbx.reshape(H, hd)
        small["lru_lambda"][l] = dlam.reshape(R)
        deps = boundary(l, 1, dx)
        dx = ffn_bwd(dx, sv, "1", w["wup1"], w["wd1"], p["g1"], "ffn1_w_up", "ffn1_w_down", "norm_ffn1", l, deps)
        deps = boundary(l, 0, dx)

    grad_x = dx.reshape(x.shape)

    small_parts = [jnp.stack(small[n]) for n in SMALL if n != "final_norm"] + [d_final.reshape(D)]
    small_shapes = [p.shape for p in small_parts]
    gpack = _pack(small_parts).reshape(1, N_DEV, -1, 128)
    small_pair = pair_exchange_start("rs_pair_start_small", [gpack], queued["pair"])
    while pairing:
        reduce_continue([dx])

    outs = {n: None for n in BIG}

    def unit_updates(unit, recv2, deps=()):
        l, names, recv1 = unit[0], unit[1], unit[6]
        for i, n in enumerate(names):
            outs[n] = grad_sum_adamw(G[l][n][0], recv1[i], recv2[i], W[n], M[n], V[n], place, l, outs[n], deps)
        return [outs[n][0] for n in names]

    after = [dx]
    late = in_flight[-2:]
    for k, unit in enumerate(in_flight[:-2]):
        recv2 = chip_exchange_wait(f"rs_chip_wait_{unit[2]}", unit[5], unit[3], unit[4], after)
        after = unit_updates(unit, recv2)
        if k == 0:
            recv1_s = pair_exchange_wait("rs_pair_wait_small", small_pair[2], small_pair[0], small_pair[1], after)[0]
            pair_s = pair_sum(gpack, recv1_s, place, 0)
            small_chip = chip_exchange_start("rs_chip_start_small", [pair_s], queued["chip"])
    late_recv2 = []
    for unit in late:
        late_recv2.append(chip_exchange_wait(f"rs_chip_wait_{unit[2]}", unit[5], unit[3], unit[4], after))
        after = [late_recv2[-1][0]]
    recv2_s = chip_exchange_wait("rs_chip_wait_small", small_chip[2], small_chip[0], small_chip[1], after)[0]
    gs = grad_sum(gpack, recv1_s, recv2_s, place)
    gs_slots = [place_own(gs.reshape((1,) + gs.shape), 0, place, F32)]
    send_sems, recv_sems, gs_slots, tok = gather_start("gather_start_small", gs_slots)
    after = unit_updates(late[0], late_recv2[0], (tok,))
    gs_slots = gather_wait("gather_wait_small", gs_slots, send_sems, recv_sems, after)
    send_sems, recv_sems, gs_slots, tok = gather_forward_start("gather_pass_start_small", gs_slots)
    after = unit_updates(late[1], late_recv2[1], (tok,))
    gs_all = gather_forward_wait("gather_pass_wait_small", gs_slots, send_sems, recv_sems, after)[0].reshape(-1, 128)
    for n in ("ffn1_w_up", "ffn2_w_up"):
        outs[n] = [jnp.swapaxes(o, 1, 2) for o in outs[n]]
    out_g, out_d, out_m, out_v = ({n: outs[n][k] for n in BIG} for k in range(4))

    small_g = dict(zip(SMALL, _unpack(gs_all, small_shapes)))
    for n in SMALL:
        if n != "conv_w":
            flat = lambda a: a.reshape(-1, 128)
            out_g[n] = small_g[n]
            out_d[n], out_m[n], out_v[n] = (o.reshape(W[n].shape) for o in adamw(flat(W[n]), flat(small_g[n]), flat(M[n]), flat(V[n])))
    cwc = conv_w.shape[2]
    gcw = lax.dynamic_slice_in_dim(small_g["conv_w"], dev * cwc, cwc, axis=2)
    cw2 = lambda a: a.reshape(-1, cwc)
    pad_rows = (-cw2(conv_w).shape[0]) % 8
    padr = lambda a: jnp.pad(cw2(a), ((0, pad_rows), (0, 0)))
    dcw_, mcw_, vcw_ = adamw(padr(conv_w), padr(gcw), padr(M["conv_w"]), padr(V["conv_w"]))
    nrow = cw2(conv_w).shape[0]
    out_g["conv_w"] = gcw
    out_d["conv_w"], out_m["conv_w"], out_v["conv_w"] = (a[:nrow].reshape(conv_w.shape) for a in (dcw_, mcw_, vcw_))

    return (loss, grad_x, *[out_g[n] for n in NAMES], *[out_d[n] for n in NAMES], *[out_m[n] for n in NAMES], *[out_v[n] for n in NAMES])
```

```python
import jax
import jax.numpy as jnp
from jax import lax
from jax.experimental import pallas as pl
from jax.experimental.pallas import tpu as pltpu

F32, BF16 = jnp.float32, jnp.bfloat16
EPS = 1e-6
LRU_C = 8.0
POOL_WINDOWS = (2, 4, 8, 16)
ADAM_LR, ADAM_B1, ADAM_B2, ADAM_EPS, ADAM_WD, ADAM_STEP = 0.001, 0.9, 0.999, 1e-08, 0.01, 10
N_DEV = 8
N_CHIP = 4
MESH = pl.DeviceIdType.MESH
V7X_VMEM_LIMIT = 56 * 1024 * 1024
ROW_TILE = 512
WIDE_TILE = 1024
SUM_TILE = 2048
ANY = pl.BlockSpec(memory_space=pl.ANY)

_pallas_call = pl.pallas_call


def _cp(*sem):
    return pltpu.CompilerParams(dimension_semantics=sem if sem else None, vmem_limit_bytes=V7X_VMEM_LIMIT)


def _tile(n, t):
    t = min(n, t)
    assert n % t == 0, (n, t)
    return t


def _dot(a, b):
    return jnp.dot(a, b, preferred_element_type=F32)


def _dot_nt(a, b):
    return lax.dot_general(a, b, (((1,), (1,)), ((), ())), preferred_element_type=F32)


def _dot_tn(a, b):
    return lax.dot_general(a, b, (((0,), (0,)), ((), ())), preferred_element_type=F32)


def _rms(xv):
    r = lax.rsqrt(jnp.mean(xv * xv, axis=-1, keepdims=True) + EPS)
    return xv * r, r


def _rms_bwd(dh, xv, gv, dy):
    n, r = _rms(xv)
    dn = dh * gv
    dx = dy + r * (dn - n * jnp.mean(dn * n, axis=-1, keepdims=True))
    return dx, jnp.sum(dh * n, axis=0, keepdims=True)


def _shift_down(x, k, fill=0.0):
    if k == 0:
        return x
    rows = lax.broadcasted_iota(jnp.int32, x.shape, 0)
    return jnp.where(rows >= k, pltpu.roll(x, k, 0), fill)


def _shift_up(x, k, fill=0.0):
    if k == 0:
        return x
    n = x.shape[0]
    rows = lax.broadcasted_iota(jnp.int32, x.shape, 0)
    return jnp.where(rows < n - k, pltpu.roll(x, n - k, 0), fill)


def _sigmoid(x):
    return 0.5 * jnp.tanh(0.5 * x) + 0.5


_GELU_K = 0.7978845608028654
_GELU_C = 0.044715


def _gelu(x):
    th = jnp.tanh(_GELU_K * (x + _GELU_C * x * x * x))
    return 0.5 * x * (1.0 + th), th


def _gelu_grad(x, th):
    return 0.5 * (1.0 + th) + 0.5 * x * (1.0 - th * th) * _GELU_K * (1.0 + 3.0 * _GELU_C * x * x)


def ffn_up(x, g, wup, l):
    T, D = x.shape
    cs = wup.shape[-2]
    tm = _tile(T, WIDE_TILE)
    ni = T // tm

    def body(x_ref, g_ref, wa_ref, wb_ref, h_ref, u_ref, s_ref, hs_ref):
        rows = pl.ds(pl.multiple_of(pl.program_id(1) * tm, tm), tm)

        @pl.when(pl.program_id(0) == 0)
        def _():
            n, _r = _rms(x_ref[...])
            hv = (n * g_ref[0]).astype(BF16)
            hs_ref[rows, :] = hv
            h_ref[...] = hv

        hv = hs_ref[rows, :]
        a = _dot_nt(hv, wa_ref[0, 0])
        b = _dot_nt(hv, wb_ref[0, 0])
        u_ref[0, 0] = a.astype(BF16)
        u_ref[1, 0] = b.astype(BF16)
        s_ref[0] = (a * _sigmoid(a) * b).astype(BF16)

    first = lambda j, i: (jnp.where(j == 0, i, ni - 1), 0)
    return _pallas_call(
        body, name="ffn_up", grid=(4, ni),
        in_specs=[pl.BlockSpec((tm, D), first), pl.BlockSpec((1, 1, D), lambda j, i: (l, 0, 0)),
                  pl.BlockSpec((1, 1, cs, D), lambda j, i: (0, j, 0, 0)), pl.BlockSpec((1, 1, cs, D), lambda j, i: (0, j + 4, 0, 0))],
        out_specs=[pl.BlockSpec((tm, D), first), pl.BlockSpec((2, 1, tm, cs), lambda j, i: (0, j, i, 0)),
                   pl.BlockSpec((1, tm, cs), lambda j, i: (j, i, 0))],
        out_shape=[jax.ShapeDtypeStruct((T, D), BF16), jax.ShapeDtypeStruct((2, 4, T, cs), BF16), jax.ShapeDtypeStruct((4, T, cs), BF16)],
        scratch_shapes=[pltpu.VMEM((T, D), BF16)],
        compiler_params=_cp("arbitrary", "arbitrary"),
    )(x, g, wup, wup)


def ffn_down(s, wd, x, l, deps=()):
    _, T, cs = s.shape
    D = x.shape[1]
    tm = _tile(T, WIDE_TILE)

    def body(s_ref, w_ref, x_ref, *rest):
        o_ref, acc_ref = rest[len(deps):]
        j = pl.program_id(1)

        @pl.when(j == 0)
        def _():
            acc_ref[...] = jnp.zeros_like(acc_ref)

        acc_ref[...] += _dot(s_ref[0], w_ref[0])

        @pl.when(j == 3)
        def _():
            o_ref[...] = x_ref[...] + 0.5 * acc_ref[...]

    return _pallas_call(
        body, name="ffn_down", grid=(T // tm, 4),
        in_specs=[pl.BlockSpec((1, tm, cs), lambda i, j: (j, i, 0)), pl.BlockSpec((1, cs, D), lambda i, j: (0, j, 0)),
                  pl.BlockSpec((tm, D), lambda i, j: (i, 0))] + [ANY] * len(deps),
        out_specs=pl.BlockSpec((tm, D), lambda i, j: (i, 0)),
        out_shape=jax.ShapeDtypeStruct((T, D), F32),
        scratch_shapes=[pltpu.VMEM((tm, D), F32)],
        compiler_params=_cp("parallel", "arbitrary"),
    )(s, wd, x, *deps)


def mix_in(x, g, win, l):
    T, D = x.shape
    ci = win.shape[-1]
    tm = _tile(T, WIDE_TILE)
    ni = T // tm

    def body(x_ref, g_ref, w_ref, h_ref, p_ref, hs_ref):
        rows = pl.ds(pl.multiple_of(pl.program_id(1) * tm, tm), tm)

        @pl.when(pl.program_id(0) == 0)
        def _():
            n, _r = _rms(x_ref[...])
            hv = (n * g_ref[0]).astype(BF16)
            hs_ref[rows, :] = hv
            h_ref[...] = hv

        p_ref[...] = _dot(hs_ref[rows, :], w_ref[0, 0]).astype(BF16)

    first = lambda j, i: (jnp.where(j == 0, i, ni - 1), 0)
    return _pallas_call(
        body, name="mix_in", grid=(N_DEV, ni),
        in_specs=[pl.BlockSpec((tm, D), first), pl.BlockSpec((1, 1, D), lambda j, i: (l, 0, 0)),
                  pl.BlockSpec((1, 1, D, ci), lambda j, i: (0, j, 0, 0))],
        out_specs=[pl.BlockSpec((tm, D), first), pl.BlockSpec((tm, ci), lambda j, i: (i, j))],
        out_shape=[jax.ShapeDtypeStruct((T, D), BF16), jax.ShapeDtypeStruct((T, N_DEV * ci), BF16)],
        scratch_shapes=[pltpu.VMEM((T, D), BF16)],
        compiler_params=_cp("arbitrary", "arbitrary"),
    )(x, g, win)


def _inv_count(T, w):
    t = lax.broadcasted_iota(jnp.int32, (T, 1), 0)
    return 1.0 / jnp.minimum(t + 1, w).astype(F32)


def _pooled(ug, w, inv):
    s = ug
    k = 1
    while k < w:
        s = s + _shift_down(s, k)
        k *= 2
    return s * inv - ug


def pool_fwd(proj, pw, pb, ps, l):
    T = proj.shape[0]
    _, G, gd, _ = pw.shape
    P = G * gd

    def body(u_ref, w_ref, b_ref, s_ref, o_ref):
        for gi in range(G):
            cols = slice(gi * gd, (gi + 1) * gd)
            ug = u_ref[:, cols].astype(F32)
            pooled = _pooled(ug, POOL_WINDOWS[gi], _inv_count(T, POOL_WINDOWS[gi]))
            mixed = _dot(pooled.astype(BF16), w_ref[0, gi].astype(BF16)) + b_ref[0, :, cols]
            o_ref[:, cols] = (mixed * s_ref[0, :, cols]).astype(BF16)

    return _pallas_call(
        body, name="pool_fwd", grid=(1,),
        in_specs=[pl.BlockSpec((T, P), lambda i: (0, 0)), pl.BlockSpec((1, G, gd, gd), lambda i: (l, 0, 0, 0)),
                  pl.BlockSpec((1, 1, P), lambda i: (l, 0, 0)), pl.BlockSpec((1, 1, P), lambda i: (l, 0, 0))],
        out_specs=pl.BlockSpec((T, P), lambda i: (0, 0)),
        out_shape=jax.ShapeDtypeStruct((T, P), BF16),
        compiler_params=_cp("arbitrary"),
    )(proj, pw, pb, ps)


def _conv(u, cw_ref, cb):
    CW = cw_ref.shape[1]
    v = cb
    for k in range(CW):
        v = v + cw_ref[0, k:k + 1, :] * _shift_down(u, CW - 1 - k)
    return v


def _softplus(z):
    return jnp.maximum(z, 0.0) + jnp.log1p(jnp.exp(-jnp.abs(z)))


def _lru_gates(v, wa_ref, ba, wx_ref, bx, lam):
    vb = v.astype(BF16)
    r = _sigmoid(_dot(vb, wa_ref[0, 0].astype(BF16)) + ba)
    i = _sigmoid(_dot(vb, wx_ref[0, 0].astype(BF16)) + bx)
    sp = _softplus(-lam)
    log_a = -LRU_C * r * sp
    a = jnp.exp(log_a)
    m2 = -jnp.tanh(log_a) * (a * a + 1.0)
    inv_mult = lax.rsqrt(m2)
    mult = jnp.where(m2 > 0.0, m2 * inv_mult, 0.0)
    return r, i, sp, a, mult, inv_mult


def _scan_fwd(a_ref, b_ref, o_ref):
    T, W = a_ref.shape
    rows = lax.broadcasted_iota(jnp.int32, (8, W), 0)

    def step(t, carry):
        r0 = pl.multiple_of(t * 8, 8)
        A = a_ref[pl.ds(r0, 8), :]
        B = b_ref[pl.ds(r0, 8), :]
        for s in (1, 2, 4):
            keep = rows >= s
            As = jnp.where(keep, pltpu.roll(A, s, 0), 1.0)
            Bs = jnp.where(keep, pltpu.roll(B, s, 0), 0.0)
            B = A * Bs + B
            A = A * As
        h = B + A * carry
        o_ref[pl.ds(r0, 8), :] = h
        return jnp.broadcast_to(h[7:8, :], (8, W))

    lax.fori_loop(0, T // 8, step, jnp.zeros((8, W), F32), unroll=8)


def _scan_bwd(a_ref, b_ref, o_ref):
    T, W = a_ref.shape
    rows = lax.broadcasted_iota(jnp.int32, (8, W), 0)
    nt = T // 8

    def step(t, carry):
        r0 = pl.multiple_of((nt - 1 - t) * 8, 8)
        A = a_ref[pl.ds(r0, 8), :]
        B = b_ref[pl.ds(r0, 8), :]
        for s in (1, 2, 4):
            keep = rows < 8 - s
            As = jnp.where(keep, pltpu.roll(A, 8 - s, 0), 1.0)
            Bs = jnp.where(keep, pltpu.roll(B, 8 - s, 0), 0.0)
            B = A * Bs + B
            A = A * As
        y = B + A * carry
        o_ref[pl.ds(r0, 8), :] = y
        return jnp.broadcast_to(y[0:1, :], (8, W))

    lax.fori_loop(0, nt, step, jnp.zeros((8, W), F32), unroll=8)


def _lru_specs(T, hd, P, R, CW, l):
    ob, gb = P // hd, (P + R) // hd
    vec = pl.BlockSpec((1, 1, hd), lambda h: (l, 0, h))
    mat = pl.BlockSpec((1, 1, hd, hd), lambda h: (l, h, 0, 0))
    return [pl.BlockSpec((T, hd), lambda h: (0, ob + h)), pl.BlockSpec((T, hd), lambda h: (0, gb + h)),
            pl.BlockSpec((1, CW, hd), lambda h: (0, 0, h)), vec, mat, vec, mat, vec, vec]


def lru_fwd(proj, cw, cb, wa, ba, wx, bx, lam, P, l):
    T = proj.shape[0]
    _, H, hd, _ = wa.shape
    R = H * hd
    CW = cw.shape[1]
    assert P % hd == 0 and T % 8 == 0

    def body(u_ref, ug_ref, cw_ref, cb_ref, wa_ref, ba_ref, wx_ref, bx_ref, lam_ref, hl_ref, hs_ref, a_s, b_s):
        v = _conv(u_ref[...].astype(F32), cw_ref, cb_ref[0])
        _r, i, _sp, a, mult, _im = _lru_gates(v, wa_ref, ba_ref[0], wx_ref, bx_ref[0], lam_ref[0])
        a_s[...] = a
        b_s[...] = mult * (i * v)
        _scan_fwd(a_s, b_s, hs_ref)
        ge, _th = _gelu(ug_ref[...].astype(F32))
        hl_ref[...] = (hs_ref[...] * ge).astype(BF16)

    out = pl.BlockSpec((T, hd), lambda h: (0, h))
    return _pallas_call(
        body, name="lru_fwd", grid=(H,),
        in_specs=_lru_specs(T, hd, P, R, CW, l),
        out_specs=[out, out],
        out_shape=[jax.ShapeDtypeStruct((T, R), BF16), jax.ShapeDtypeStruct((T, R), F32)],
        scratch_shapes=[pltpu.VMEM((T, hd), F32)] * 2,
        compiler_params=_cp("parallel"),
    )(proj, proj, cw, cb, wa, ba, wx, bx, lam)


def mix_out(pm, hl, proj, x, wpu, wlu, wout, P, l, deps=()):
    T, D = x.shape
    R = hl.shape[1]
    tm = _tile(T, ROW_TILE)
    assert (P + 2 * R) % D == 0
    gb = (P + 2 * R) // D

    def body(pm_ref, hl_ref, gp_ref, gl_ref, x_ref, wpu_ref, wlu_ref, wo_ref, *rest):
        o_ref, yp_ref, yl_ref, z_ref = rest[len(deps):]
        yp = _dot(pm_ref[...], wpu_ref[0])
        yl = _dot(hl_ref[...], wlu_ref[0])
        z = (_sigmoid(gp_ref[...].astype(F32)) * yp + _sigmoid(gl_ref[...].astype(F32)) * yl).astype(BF16)
        yp_ref[...] = yp.astype(BF16)
        yl_ref[...] = yl.astype(BF16)
        z_ref[...] = z
        o_ref[...] = x_ref[...] + _dot(z, wo_ref[0])

    row = lambda w: pl.BlockSpec((tm, w), lambda i: (i, 0))
    return _pallas_call(
        body, name="mix_out", grid=(T // tm,),
        in_specs=[row(P), row(R), pl.BlockSpec((tm, D), lambda i: (i, gb)), pl.BlockSpec((tm, D), lambda i: (i, gb + 1)), row(D),
                  pl.BlockSpec((1, P, D), lambda i: (0, 0, 0)), pl.BlockSpec((1, R, D), lambda i: (0, 0, 0)),
                  pl.BlockSpec((1, D, D), lambda i: (0, 0, 0))] + [ANY] * len(deps),
        out_specs=[row(D)] * 4,
        out_shape=[jax.ShapeDtypeStruct((T, D), F32)] + [jax.ShapeDtypeStruct((T, D), BF16)] * 3,
        compiler_params=_cp("parallel"),
    )(pm, hl, proj, proj, x, wpu, wlu, wout, *deps)


def loss_head(x, gf, tgt):
    T, D = x.shape
    tm = _tile(T, ROW_TILE)

    def body(x_ref, g_ref, t_ref, loss_ref, dx_ref, dg_ref):
        @pl.when(pl.program_id(0) == 0)
        def _():
            loss_ref[...] = jnp.zeros_like(loss_ref)
            dg_ref[...] = jnp.zeros_like(dg_ref)

        xv = x_ref[...]
        gv = g_ref[...]
        n, _r = _rms(xv)
        e = n * gv - t_ref[...]
        loss_ref[...] += 0.5 * jnp.sum(jnp.sum(e * e, axis=-1, keepdims=True), axis=0, keepdims=True) / D
        dx, dg = _rms_bwd(e * (1.0 / D), xv, gv, 0.0)
        dx_ref[...] = dx
        dg_ref[...] += dg

    return _pallas_call(
        body, name="loss_head", grid=(T // tm,),
        in_specs=[pl.BlockSpec((tm, D), lambda i: (i, 0)), pl.BlockSpec((1, D), lambda i: (0, 0)), pl.BlockSpec((tm, D), lambda i: (i, 0))],
        out_specs=[pl.BlockSpec((1, 1), lambda i: (0, 0)), pl.BlockSpec((tm, D), lambda i: (i, 0)), pl.BlockSpec((1, D), lambda i: (0, 0))],
        out_shape=[jax.ShapeDtypeStruct((1, 1), F32), jax.ShapeDtypeStruct((T, D), F32), jax.ShapeDtypeStruct((1, D), F32)],
        compiler_params=_cp("arbitrary"),
    )(x, gf, tgt)


def ffn_down_bwd(dy, wd, u, l, deps=()):
    T, D = dy.shape
    cs = u.shape[-1]
    tm = _tile(T, WIDE_TILE)
    ni = T // tm

    def body(dy_ref, w_ref, u_ref, *rest):
        do_ref, du_ref, dyb_ref = rest[len(deps):]
        rows = pl.ds(pl.multiple_of(pl.program_id(1) * tm, tm), tm)

        @pl.when(pl.program_id(0) == 0)
        def _():
            d = (0.5 * dy_ref[...]).astype(BF16)
            dyb_ref[rows, :] = d
            do_ref[...] = d

        ds = _dot_nt(dyb_ref[rows, :], w_ref[0])
        a = u_ref[0, 0].astype(F32)
        b = u_ref[1, 0].astype(F32)
        sg = _sigmoid(a)
        du_ref[0, 0] = (ds * b * (sg * (1.0 + a * (1.0 - sg)))).astype(BF16)
        du_ref[1, 0] = (ds * (a * sg)).astype(BF16)

    first = lambda j, i: (jnp.where(j == 0, i, ni - 1), 0)
    blk = pl.BlockSpec((2, 1, tm, cs), lambda j, i: (0, j, i, 0))
    return _pallas_call(
        body, name="ffn_down_bwd", grid=(4, ni),
        in_specs=[pl.BlockSpec((tm, D), first), pl.BlockSpec((1, cs, D), lambda j, i: (0, j, 0)), blk] + [ANY] * len(deps),
        out_specs=[pl.BlockSpec((tm, D), first), blk],
        out_shape=[jax.ShapeDtypeStruct((T, D), BF16), jax.ShapeDtypeStruct((2, 4, T, cs), BF16)],
        scratch_shapes=[pltpu.VMEM((T, D), BF16)],
        compiler_params=_cp("arbitrary", "arbitrary"),
    )(dy, wd, u, *deps)


def dw_tn(name, a, a_spec, b, b_spec, G, M, N, T):
    tk = _tile(T, SUM_TILE)
    nk = T // tk

    def body(a_ref, b_ref, o32_ref, o16_ref, acc_ref):
        k = pl.program_id(1)

        @pl.when(k == 0)
        def _():
            acc_ref[...] = jnp.zeros_like(acc_ref)

        av = a_ref[0] if len(a_ref.shape) == 3 else a_ref[...]
        bv = b_ref[0] if len(b_ref.shape) == 3 else b_ref[...]
        acc_ref[...] += _dot_tn(av, bv)

        @pl.when(k == nk - 1)
        def _():
            o32_ref[0, 0] = acc_ref[...]
            o16_ref[0, 0] = acc_ref[...].astype(BF16)

    out = pl.BlockSpec((1, 1, M, N), lambda g, k: (0, g, 0, 0))
    return _pallas_call(
        body, name=name, grid=(G, nk),
        in_specs=[a_spec(tk), b_spec(tk)], out_specs=[out, out],
        out_shape=[jax.ShapeDtypeStruct((1, G, M, N), F32), jax.ShapeDtypeStruct((1, G, M, N), BF16)],
        scratch_shapes=[pltpu.VMEM((M, N), F32)],
        compiler_params=_cp("parallel", "arbitrary"),
    )(a, b)


def dx_norm_bwd(name, dact, d_spec, w, G, x, g, dy, l, w_transposed=False):
    T, D = x.shape
    wblk = w.shape[-2:]
    tm = _tile(T, WIDE_TILE)
    ni = T // tm
    ch = _tile(tm, ROW_TILE // 2)

    def body(d_ref, w_ref, x_ref, g_ref, dy_ref, dx_ref, dg_ref, acc_ref):
        j, i = pl.program_id(0), pl.program_id(1)
        rows = pl.ds(pl.multiple_of(i * tm, tm), tm)

        @pl.when(jnp.logical_and(i == 0, j == 0))
        def _():
            dg_ref[...] = jnp.zeros_like(dg_ref)

        @pl.when(j == 0)
        def _():
            acc_ref[rows, :] = jnp.zeros((tm, D), F32)

        dv = d_ref[0] if len(d_ref.shape) == 3 else d_ref[...]
        acc_ref[rows, :] += _dot(dv, w_ref[0, 0]) if w_transposed else _dot_nt(dv, w_ref[0, 0])

        @pl.when(j == G - 1)
        def _():
            dg = jnp.zeros((1, D), F32)
            for c0 in range(0, tm, ch):
                part_rows = pl.ds(pl.multiple_of(i * tm + c0, ch), ch)
                dx, dgc = _rms_bwd(acc_ref[part_rows, :], x_ref[c0:c0 + ch, :], g_ref[0], dy_ref[c0:c0 + ch, :])
                dx_ref[c0:c0 + ch, :] = dx
                dg = dg + dgc
            dg_ref[...] += dg

    last = pl.BlockSpec((tm, D), lambda j, i: (jnp.where(j == G - 1, i, 0), 0))
    return _pallas_call(
        body, name=name, grid=(G, ni),
        in_specs=[d_spec(tm), pl.BlockSpec((1, 1) + wblk, lambda j, i: (0, j, 0, 0)), last, pl.BlockSpec((1, 1, D), lambda j, i: (l, 0, 0)), last],
        out_specs=[last, pl.BlockSpec((1, D), lambda j, i: (0, 0))],
        out_shape=[jax.ShapeDtypeStruct((T, D), F32), jax.ShapeDtypeStruct((1, D), F32)],
        scratch_shapes=[pltpu.VMEM((T, D), F32)],
        compiler_params=_cp("arbitrary", "arbitrary"),
    )(dact, w, x, g, dy)


def mix_out_bwd(dy, proj, yp, yl, wpu, wlu, wout, P, R, l, deps=()):
    T, D = dy.shape
    tm = _tile(T, ROW_TILE)
    gb = (P + 2 * R) // D

    def body(dy_ref, gp_ref, gl_ref, yp_ref, yl_ref, wpu_ref, wlu_ref, wo_ref, *rest):
        dyb_ref, dyp_ref, dyl_ref, dgp_ref, dgl_ref, dpm_ref, dhl_ref = rest[len(deps):]
        dyb = dy_ref[...].astype(BF16)
        dyb_ref[...] = dyb
        dz = _dot_nt(dyb, wo_ref[0])
        sp = _sigmoid(gp_ref[...].astype(F32))
        sl = _sigmoid(gl_ref[...].astype(F32))
        dgp_ref[...] = (dz * yp_ref[...].astype(F32) * sp * (1.0 - sp)).astype(BF16)
        dgl_ref[...] = (dz * yl_ref[...].astype(F32) * sl * (1.0 - sl)).astype(BF16)
        dyp = (dz * sp).astype(BF16)
        dyl = (dz * sl).astype(BF16)
        dyp_ref[...] = dyp
        dyl_ref[...] = dyl
        dpm_ref[...] = _dot_nt(dyp, wpu_ref[0]).astype(BF16)
        dhl_ref[...] = _dot_nt(dyl, wlu_ref[0]).astype(BF16)

    row = lambda w: pl.BlockSpec((tm, w), lambda i: (i, 0))
    return _pallas_call(
        body, name="mix_out_bwd", grid=(T // tm,),
        in_specs=[row(D), pl.BlockSpec((tm, D), lambda i: (i, gb)), pl.BlockSpec((tm, D), lambda i: (i, gb + 1)), row(D), row(D),
                  pl.BlockSpec((1, P, D), lambda i: (0, 0, 0)), pl.BlockSpec((1, R, D), lambda i: (0, 0, 0)),
                  pl.BlockSpec((1, D, D), lambda i: (0, 0, 0))] + [ANY] * len(deps),
        out_specs=[row(D)] * 5 + [row(P), row(R)],
        out_shape=[jax.ShapeDtypeStruct((T, D), BF16)] * 5 + [jax.ShapeDtypeStruct((T, P), BF16), jax.ShapeDtypeStruct((T, R), BF16)],
        compiler_params=_cp("parallel"),
    )(dy, proj, proj, yp, yl, wpu, wlu, wout, *deps)


def lru_bwd(proj, hs, dhl, cw, cb, wa, ba, wx, bx, lam, P, l):
    T = proj.shape[0]
    _, H, hd, _ = wa.shape
    R = H * hd
    CW = cw.shape[1]

    def body(u_ref, ug_ref, cw_ref, cb_ref, wa_ref, ba_ref, wx_ref, bx_ref, lam_ref, hs_ref, dhl_ref,
             du_ref, dug_ref, dcw_ref, dcb_ref, dwa_ref, dba_ref, dwx_ref, dbx_ref, dlam_ref, c_s, g_s, y_s):
        u = u_ref[...].astype(F32)
        v = _conv(u, cw_ref, cb_ref[0])
        lam = lam_ref[0]
        r, i, sp, a, mult, inv_mult = _lru_gates(v, wa_ref, ba_ref[0], wx_ref, bx_ref[0], lam)
        ug = ug_ref[...].astype(F32)
        ge, th = _gelu(ug)
        hs = hs_ref[...]
        dhl = dhl_ref[...].astype(F32)
        dug_ref[...] = (dhl * hs * _gelu_grad(ug, th)).astype(BF16)
        c_s[...] = _shift_up(a, 1)
        g_s[...] = dhl * ge
        _scan_bwd(c_s, g_s, y_s)
        y = y_s[...]
        da = y * _shift_down(hs, 1)
        iv = i * v
        dlog_a = da * a - (y * iv) * (a * a) * inv_mult
        div = y * mult
        dpa = (dlog_a * (-LRU_C) * sp) * r * (1.0 - r)
        dpx = (div * v) * i * (1.0 - i)
        dsp = jnp.sum(dlog_a * (-LRU_C) * r, axis=0, keepdims=True)
        dlam_ref[0] = -dsp * _sigmoid(-lam)
        vb = v.astype(BF16)
        dpab, dpxb = dpa.astype(BF16), dpx.astype(BF16)
        dwa_ref[0, 0] = _dot_tn(vb, dpab)
        dwx_ref[0, 0] = _dot_tn(vb, dpxb)
        dba_ref[0] = jnp.sum(dpa, axis=0, keepdims=True)
        dbx_ref[0] = jnp.sum(dpx, axis=0, keepdims=True)
        dv = div * i + _dot_nt(dpab, wa_ref[0, 0].astype(BF16)) + _dot_nt(dpxb, wx_ref[0, 0].astype(BF16))
        dcb_ref[0] = jnp.sum(dv, axis=0, keepdims=True)
        du = jnp.zeros_like(dv)
        for k in range(CW):
            du = du + cw_ref[0, k:k + 1, :] * _shift_up(dv, CW - 1 - k)
            dcw_ref[0, k:k + 1, :] = jnp.sum(dv * _shift_down(u, CW - 1 - k), axis=0, keepdims=True)
        du_ref[...] = du.astype(BF16)

    col = pl.BlockSpec((T, hd), lambda h: (0, h))
    vec = pl.BlockSpec((1, 1, hd), lambda h: (0, 0, h))
    mat = pl.BlockSpec((1, 1, hd, hd), lambda h: (0, h, 0, 0))
    vshape = jax.ShapeDtypeStruct((1, 1, R), F32)
    mshape = jax.ShapeDtypeStruct((1, H, hd, hd), F32)
    return _pallas_call(
        body, name="lru_bwd", grid=(H,),
        in_specs=_lru_specs(T, hd, P, R, CW, l) + [col, col],
        out_specs=[col, col, pl.BlockSpec((1, CW, hd), lambda h: (0, 0, h)), vec, mat, vec, mat, vec, vec],
        out_shape=[jax.ShapeDtypeStruct((T, R), BF16)] * 2 + [jax.ShapeDtypeStruct((1, CW, R), F32), vshape, mshape, vshape, mshape, vshape, vshape],
        scratch_shapes=[pltpu.VMEM((T, hd), F32)] * 3,
        compiler_params=_cp("parallel"),
    )(proj, proj, cw, cb, wa, ba, wx, bx, lam, hs, dhl)


def pool_bwd(proj, dpm, pw, pb, ps, l):
    T = proj.shape[0]
    _, G, gd, _ = pw.shape
    P = G * gd

    def body(u_ref, d_ref, w_ref, b_ref, s_ref, du_ref, dw_ref, db_ref, dsc_ref):
        for gi in range(G):
            cols = slice(gi * gd, (gi + 1) * gd)
            w = POOL_WINDOWS[gi]
            inv = _inv_count(T, w)
            ug = u_ref[:, cols].astype(F32)
            pooled = _pooled(ug, w, inv).astype(BF16)
            wb = w_ref[0, gi].astype(BF16)
            mixed = _dot(pooled, wb) + b_ref[0, :, cols]
            dpm_g = d_ref[:, cols].astype(F32)
            dsc_ref[0, :, cols] = jnp.sum(dpm_g * mixed, axis=0, keepdims=True)
            dmixed = dpm_g * s_ref[0, :, cols]
            db_ref[0, :, cols] = jnp.sum(dmixed, axis=0, keepdims=True)
            dmb = dmixed.astype(BF16)
            dw_ref[0, gi] = _dot_tn(pooled, dmb)
            dpooled = _dot_nt(dmb, wb)
            s = dpooled * inv
            k = 1
            while k < w:
                s = s + _shift_up(s, k)
                k *= 2
            du_ref[:, cols] = (s - dpooled).astype(BF16)

    vec = pl.BlockSpec((1, 1, P), lambda i: (l, 0, 0))
    ovec = pl.BlockSpec((1, 1, P), lambda i: (0, 0, 0))
    return _pallas_call(
        body, name="pool_bwd", grid=(1,),
        in_specs=[pl.BlockSpec((T, P), lambda i: (0, 0)), pl.BlockSpec((T, P), lambda i: (0, 0)),
                  pl.BlockSpec((1, G, gd, gd), lambda i: (l, 0, 0, 0)), vec, vec],
        out_specs=[pl.BlockSpec((T, P), lambda i: (0, 0)), pl.BlockSpec((1, G, gd, gd), lambda i: (0, 0, 0, 0)), ovec, ovec],
        out_shape=[jax.ShapeDtypeStruct((T, P), BF16), jax.ShapeDtypeStruct((1, G, gd, gd), F32),
                   jax.ShapeDtypeStruct((1, 1, P), F32), jax.ShapeDtypeStruct((1, 1, P), F32)],
        compiler_params=_cp("arbitrary"),
    )(proj, dpm, pw, pb, ps)


def _place():
    x, y, c = lax.axis_index("x"), lax.axis_index("y"), lax.axis_index("c")
    return x, y, c


HBM = pl.BlockSpec(memory_space=pltpu.HBM)
SEM = pl.BlockSpec(memory_space=pltpu.SEMAPHORE)
EFFECT = pltpu.SideEffectType.DATAFLOW_SIDE_EFFECTING


def _in_hbm(a):
    return pltpu.with_memory_space_constraint(a, pltpu.HBM)


def split_start(name, bufs, n_copies, copies_of, deps=()):
    nb = len(bufs)

    def body(*refs):
        buf = refs[:nb]
        send_sems, recv_sems = refs[nb + len(deps)], refs[nb + len(deps) + 1]
        token = refs[-1]
        for i, (src, dst, dev) in enumerate(copies_of(buf)):
            pltpu.make_async_remote_copy(src_ref=src, dst_ref=dst, send_sem=send_sems.at[i], recv_sem=recv_sems.at[i],
                                         device_id=dev, device_id_type=MESH).start()
        token[...] = jnp.zeros_like(token)

    outs = _pallas_call(
        body, name=name,
        in_specs=[HBM] * nb + [ANY] * len(deps),
        out_specs=(SEM, SEM, *([HBM] * nb), pl.BlockSpec(memory_space=pltpu.VMEM)),
        out_shape=(pltpu.SemaphoreType.DMA((n_copies,)), pltpu.SemaphoreType.DMA((n_copies,)),
                   *[pltpu.HBM(b.shape, b.dtype) for b in bufs], jax.ShapeDtypeStruct((8, 128), F32)),
        input_output_aliases={i: 2 + i for i in range(nb)},
        compiler_params=pltpu.CompilerParams(has_side_effects=EFFECT),
    )(*[_in_hbm(b) for b in bufs], *deps)
    return outs[0], outs[1], list(outs[2:2 + nb]), outs[-1]


def split_wait(name, bufs, send_sems, recv_sems, after, copies_of):
    nb = len(bufs)

    def body(*refs):
        buf = refs[:nb]
        send, recv = refs[nb], refs[nb + 1]
        for i, (src, dst, dev) in enumerate(copies_of(buf)):
            cp = pltpu.make_async_remote_copy(src_ref=src, dst_ref=dst, send_sem=send.at[i], recv_sem=recv.at[i],
                                              device_id=dev, device_id_type=MESH)
            cp.wait_send()
            cp.wait_recv()

    outs = _pallas_call(
        body, name=name,
        in_specs=[HBM] * nb + [SEM, SEM] + [ANY] * len(after),
        out_specs=[HBM] * nb,
        out_shape=[pltpu.HBM(b.shape, b.dtype) for b in bufs],
        input_output_aliases={i: i for i in range(nb)},
        compiler_params=pltpu.CompilerParams(has_side_effects=EFFECT),
    )(*bufs, send_sems, recv_sems, *after)
    return list(outs)


def _two_row_blocks(rows):
    return (rows // 2, 1) if rows % 32 == 0 else (rows, 0)


def place_own(ws, dtypes, l, place):
    n = len(ws)

    def body(p_ref, *refs):
        for a in range(n):
            refs[n + a][0] = refs[a][0].astype(dtypes[a])

    in_specs, out_specs, out_shape = [], [], []
    for w, dt in zip(ws, dtypes):
        _, rows, cols = w.shape
        rb, step = _two_row_blocks(rows)
        in_specs.append(pl.BlockSpec((1, rb, cols), lambda i, p, s=step: (l, i * s, 0)))
        out_specs.append(pl.BlockSpec((1, rb, cols), lambda i, p, s=step: (p[2], i * s, 0)))
        out_shape.append(jax.ShapeDtypeStruct((N_DEV, rows, cols), dt))
    return list(_pallas_call(
        body, name="place_own",
        grid_spec=pltpu.PrefetchScalarGridSpec(num_scalar_prefetch=1, grid=(2,), in_specs=in_specs, out_specs=out_specs),
        out_shape=out_shape, compiler_params=_cp("arbitrary"),
    )(place, *ws))


def _gather_copies(land):
    x, y, c = _place()
    k = 4 * x + 2 * y + c
    peers = [(x, 1 - y, c), (1 - x, y, c), (1 - x, 1 - y, c), (x, y, 1 - c)]
    return [(b.at[k], b.at[k], p) for p in peers for b in land]


def gather_start(name, land, deps=()):
    return split_start(name, land, 4 * len(land), _gather_copies, deps)


def gather_wait(name, land, send_sems, recv_sems, after):
    return split_wait(name, land, send_sems, recv_sems, after, _gather_copies)


def _forward_copies(land):
    x, y, c = _place()
    slots = [4 * px + 2 * py + c for px, py in [(x, 1 - y), (1 - x, y), (1 - x, 1 - y)]]
    return [(b.at[k], b.at[k], (x, y, 1 - c)) for k in slots for b in land]


def gather_forward_start(name, land, deps=()):
    return split_start(name, land, 3 * len(land), _forward_copies, deps)


def gather_forward_wait(name, land, send_sems, recv_sems, after):
    return split_wait(name, land, send_sems, recv_sems, after, _forward_copies)


def _chip_copies(nsrc):
    def copies(buf):
        p16, recv2 = buf[:nsrc], buf[nsrc:]
        x, y, c = _place()
        out = []
        for d in (1, 2, 3):
            px = 1 - x if d & 2 else x
            py = 1 - y if d & 1 else y
            out += [(p16[a].at[:, d - 1], recv2[a].at[:, d - 1], (px, py, c)) for a in range(nsrc)]
        return out
    return copies


def _pair_copies(nsrc):
    def copies(buf):
        g16, recv = buf[:nsrc], buf[nsrc:]
        x, y, c = _place()
        return [(g16[a].at[:, 2 * j + 1 - c], recv[a].at[:, j], (x, y, 1 - c)) for a in range(nsrc) for j in range(N_CHIP)]
    return copies


def pair_exchange_start(name, g16, deps=()):
    n = len(g16)
    land = [lax.empty((1, N_CHIP) + s.shape[2:], s.dtype) for s in g16]
    return split_start(name, list(g16) + land, N_CHIP * n, _pair_copies(n), deps)


def pair_exchange_wait(name, bufs, send_sems, recv_sems, after):
    n = len(bufs) // 2
    return split_wait(name, bufs, send_sems, recv_sems, after, _pair_copies(n))[n:]


def chip_exchange_start(name, pair16, deps=()):
    n = len(pair16)
    land = [lax.empty((s.shape[0], 3) + s.shape[2:], s.dtype) for s in pair16]
    return split_start(name, list(pair16) + land, 3 * n, _chip_copies(n), deps)


def chip_exchange_wait(name, bufs, send_sems, recv_sems, after):
    n = len(bufs) // 2
    return split_wait(name, bufs, send_sems, recv_sems, after, _chip_copies(n))[n:]


def _rows_tile(rows, cols, budget=1 << 20):
    t = rows
    while t % 2 == 0 and t * cols > budget and (t // 2) % 16 == 0:
        t //= 2
    return t


def pair_sum(g32s, recv1s, place):
    n = len(g32s)

    def body(p_ref, *refs):
        for a in range(n):
            m_ref, r_ref, o_ref = refs[a], refs[n + a], refs[2 * n + a]
            o_ref[...] = (m_ref[...] + r_ref[...].astype(F32)).astype(o_ref.dtype)

    other = lambda d, p: jnp.bitwise_xor(p[1], d + 1)
    g_specs, r_specs, o_specs, out_shape = [], [], [], []
    for r1 in recv1s:
        _, _, rows, cols = r1.shape
        rb, step = _two_row_blocks(rows)
        g_specs.append(pl.BlockSpec((1, 1, rb, cols), lambda d, i, p, s=step: (0, 2 * other(d, p) + p[0], i * s, 0)))
        r_specs.append(pl.BlockSpec((1, 1, rb, cols), lambda d, i, p, s=step: (0, other(d, p), i * s, 0)))
        o_specs.append(pl.BlockSpec((1, 1, rb, cols), lambda d, i, p, s=step: (0, d, i * s, 0)))
        out_shape.append(jax.ShapeDtypeStruct((1, N_CHIP - 1, rows, cols), r1.dtype))
    return list(_pallas_call(
        body, name="pair_sum",
        grid_spec=pltpu.PrefetchScalarGridSpec(num_scalar_prefetch=1, grid=(N_CHIP - 1, 2), in_specs=g_specs + r_specs, out_specs=o_specs),
        out_shape=out_shape, compiler_params=_cp("arbitrary", "arbitrary"),
    )(place, *g32s, *recv1s))


def _grad_in_specs(tr, cols, l):
    return ([pl.BlockSpec((1, 1, tr, cols), lambda i, p: (l, p[2], i, 0)), pl.BlockSpec((1, 1, tr, cols), lambda i, p: (0, p[1], i, 0))]
            + [pl.BlockSpec((1, 1, tr, cols), lambda i, p, d=d: (0, d, i, 0)) for d in range(3)])


def _grad_total(o32, o16, r0, r1, r2):
    return (o32[0, 0] + o16[0, 0].astype(F32)) + r0[0, 0].astype(F32) + r1[0, 0].astype(F32) + r2[0, 0].astype(F32)


def grad_sum(g32, recv1, recv2, place):
    _, _, rows, cols = recv1.shape
    tr = _rows_tile(rows, cols)

    def body(p_ref, o32, o16, r0, r1, r2, g_ref):
        g_ref[...] = _grad_total(o32, o16, r0, r1, r2)

    return _pallas_call(
        body, name="grad_sum",
        grid_spec=pltpu.PrefetchScalarGridSpec(
            num_scalar_prefetch=1, grid=(rows // tr,), in_specs=_grad_in_specs(tr, cols, 0),
            out_specs=pl.BlockSpec((tr, cols), lambda i, p: (i, 0))),
        out_shape=jax.ShapeDtypeStruct((rows, cols), F32), compiler_params=_cp("parallel"),
    )(place, g32, recv1, recv2, recv2, recv2)


def _adamw_math(w, g, m, v):
    m = ADAM_B1 * m + (1.0 - ADAM_B1) * g
    v = ADAM_B2 * v + (1.0 - ADAM_B2) * (g * g)
    m_hat = m / (1.0 - ADAM_B1 ** ADAM_STEP)
    v_hat = v / (1.0 - ADAM_B2 ** ADAM_STEP)
    delta = -ADAM_LR * (m_hat / (jnp.sqrt(v_hat) + ADAM_EPS) + ADAM_WD * w)
    return delta, m, v


def grad_sum_adamw(g32, recv1, recv2, w, m, v, place, l, prev, deps=()):
    L, rows, cols = w.shape
    tr = _rows_tile(rows, cols, 1 << 19)

    def body(p_ref, o32, o16, r0, r1, r2, w_ref, m_ref, v_ref, *rest):
        g_ref, d_ref, nm_ref, nv_ref = rest[-4:]
        g = _grad_total(o32, o16, r0, r1, r2)
        d, nm, nv = _adamw_math(w_ref[0], g, m_ref[0], v_ref[0])
        g_ref[0] = g
        d_ref[0] = d
        nm_ref[0] = nm
        nv_ref[0] = nv

    blk = pl.BlockSpec((1, tr, cols), lambda i, p: (l, i, 0))
    args = [g32, recv1, recv2, recv2, recv2, w, m, v]
    in_specs = _grad_in_specs(tr, cols, 0) + [blk] * 3
    aliases = {}
    if prev is not None:
        aliases = {1 + len(args) + k: k for k in range(4)}
        args += list(prev)
        in_specs += [ANY] * 4
    args += list(deps)
    in_specs += [ANY] * len(deps)
    return _pallas_call(
        body, name="grad_sum_adamw",
        grid_spec=pltpu.PrefetchScalarGridSpec(num_scalar_prefetch=1, grid=(rows // tr,), in_specs=in_specs, out_specs=[blk] * 4),
        out_shape=[jax.ShapeDtypeStruct((L, rows, cols), F32)] * 4, input_output_aliases=aliases,
        compiler_params=_cp("parallel"),
    )(place, *args)


def adamw(w, g, m, v):
    rows, cols = w.shape
    tr = _rows_tile(rows, cols, 1 << 18)

    def body(w_ref, g_ref, m_ref, v_ref, d_ref, nm_ref, nv_ref):
        d, nm, nv = _adamw_math(w_ref[...], g_ref[...], m_ref[...], v_ref[...])
        d_ref[...] = d
        nm_ref[...] = nm
        nv_ref[...] = nv

    blk = pl.BlockSpec((tr, cols), lambda i: (i, 0))
    return _pallas_call(body, name="adamw_small", grid=(rows // tr,), in_specs=[blk] * 4, out_specs=[blk] * 3,
                        out_shape=[jax.ShapeDtypeStruct((rows, cols), F32)] * 3, compiler_params=_cp("parallel"))(w, g, m, v)


SMALL = ("norm_ffn1", "norm_mix", "pool_w", "pool_b", "pool_scale", "conv_w", "conv_b", "lru_w_a", "lru_b_a", "lru_w_x", "lru_b_x",
         "lru_lambda", "norm_ffn2", "final_norm")
BIG = ("ffn1_w_up", "ffn1_w_down", "w_in", "w_pool_up", "w_lru_up", "w_out", "ffn2_w_up", "ffn2_w_down")
NAMES = ("norm_ffn1", "ffn1_w_up", "ffn1_w_down", "norm_mix", "w_in", "pool_w", "pool_b", "pool_scale", "w_pool_up", "conv_w", "conv_b",
         "lru_w_a", "lru_b_a", "lru_w_x", "lru_b_x", "lru_lambda", "w_lru_up", "w_out", "norm_ffn2", "ffn2_w_up", "ffn2_w_down", "final_norm")
SUBLAYERS = (("ffn1_w_up", "ffn1_w_down"), ("w_in", "w_pool_up", "w_lru_up", "w_out", "conv_w"), ("ffn2_w_up", "ffn2_w_down"))
PACK_ROWS = 16 * N_DEV


def _pack(parts):
    flat = jnp.concatenate([p.reshape(-1) for p in parts])
    unit = 128 * PACK_ROWS
    padded = -(-flat.size // unit) * unit
    return jnp.pad(flat, (0, padded - flat.size)).reshape(-1, 128)


def _unpack(packed, shapes):
    flat = packed.reshape(-1)
    out, off = [], 0
    for s in shapes:
        n = 1
        for d in s:
            n *= d
        out.append(flat[off:off + n].reshape(s))
        off += n
    return out


def kernel(x, norm_ffn1, ffn1_w_up, ffn1_w_down, norm_mix, w_in, pool_w, pool_b, pool_scale, w_pool_up, conv_w, conv_b, lru_w_a, lru_b_a, lru_w_x, lru_b_x, lru_lambda, w_lru_up, w_out, norm_ffn2, ffn2_w_up, ffn2_w_down, final_norm, loss_target, m_norm_ffn1, m_ffn1_w_up, m_ffn1_w_down, m_norm_mix, m_w_in, m_pool_w, m_pool_b, m_pool_scale, m_w_pool_up, m_conv_w, m_conv_b, m_lru_w_a, m_lru_b_a, m_lru_w_x, m_lru_b_x, m_lru_lambda, m_w_lru_up, m_w_out, m_norm_ffn2, m_ffn2_w_up, m_ffn2_w_down, m_final_norm, v_norm_ffn1, v_ffn1_w_up, v_ffn1_w_down, v_norm_mix, v_w_in, v_pool_w, v_pool_b, v_pool_scale, v_w_pool_up, v_conv_w, v_conv_b, v_lru_w_a, v_lru_b_a, v_lru_w_x, v_lru_b_x, v_lru_lambda, v_w_lru_up, v_w_out, v_norm_ffn2, v_ffn2_w_up, v_ffn2_w_down, v_final_norm):
    W = dict(norm_ffn1=norm_ffn1, ffn1_w_up=ffn1_w_up, ffn1_w_down=ffn1_w_down, norm_mix=norm_mix, w_in=w_in, pool_w=pool_w, pool_b=pool_b,
             pool_scale=pool_scale, w_pool_up=w_pool_up, conv_w=conv_w, conv_b=conv_b, lru_w_a=lru_w_a, lru_b_a=lru_b_a, lru_w_x=lru_w_x,
             lru_b_x=lru_b_x, lru_lambda=lru_lambda, w_lru_up=w_lru_up, w_out=w_out, norm_ffn2=norm_ffn2, ffn2_w_up=ffn2_w_up,
             ffn2_w_down=ffn2_w_down, final_norm=final_norm)
    M = dict(norm_ffn1=m_norm_ffn1, ffn1_w_up=m_ffn1_w_up, ffn1_w_down=m_ffn1_w_down, norm_mix=m_norm_mix, w_in=m_w_in, pool_w=m_pool_w,
             pool_b=m_pool_b, pool_scale=m_pool_scale, w_pool_up=m_w_pool_up, conv_w=m_conv_w, conv_b=m_conv_b, lru_w_a=m_lru_w_a,
             lru_b_a=m_lru_b_a, lru_w_x=m_lru_w_x, lru_b_x=m_lru_b_x, lru_lambda=m_lru_lambda, w_lru_up=m_w_lru_up, w_out=m_w_out,
             norm_ffn2=m_norm_ffn2, ffn2_w_up=m_ffn2_w_up, ffn2_w_down=m_ffn2_w_down, final_norm=m_final_norm)
    V = dict(norm_ffn1=v_norm_ffn1, ffn1_w_up=v_ffn1_w_up, ffn1_w_down=v_ffn1_w_down, norm_mix=v_norm_mix, w_in=v_w_in, pool_w=v_pool_w,
             pool_b=v_pool_b, pool_scale=v_pool_scale, w_pool_up=v_w_pool_up, conv_w=v_conv_w, conv_b=v_conv_b, lru_w_a=v_lru_w_a,
             lru_b_a=v_lru_b_a, lru_w_x=v_lru_w_x, lru_b_x=v_lru_b_x, lru_lambda=v_lru_lambda, w_lru_up=v_w_lru_up, w_out=v_w_out,
             norm_ffn2=v_norm_ffn2, ffn2_w_up=v_ffn2_w_up, ffn2_w_down=v_ffn2_w_down, final_norm=v_final_norm)

    for S in (W, M, V):
        for n in ("ffn1_w_up", "ffn2_w_up"):
            S[n] = jnp.swapaxes(S[n], 1, 2)

    T, D = x.shape[1], x.shape[2]
    L = norm_ffn1.shape[0]
    P = pool_scale.shape[1]
    R = lru_lambda.shape[1]
    H, hd = lru_w_a.shape[1], lru_w_a.shape[2]
    CW = conv_w.shape[1]
    cs = ffn1_w_up.shape[2]
    ci = w_in.shape[2]
    xin = x.reshape(T, D)
    tgt = loss_target.reshape(T, D)
    dev = 4 * lax.axis_index("x") + 2 * lax.axis_index("y") + lax.axis_index("c")
    place = jnp.stack([lax.axis_index("c"), 2 * lax.axis_index("x") + lax.axis_index("y"), dev]).astype(jnp.int32)

    cw_flat = conv_w.reshape(L, -1)
    cw_pad = (-cw_flat.shape[1]) % 1024
    cw_tiles = jnp.pad(cw_flat, ((0, 0), (0, cw_pad))).reshape(L, -1, 128)

    def units(l):
        return SUBLAYERS if l == 0 else (tuple(n for u in SUBLAYERS for n in u),)

    queued = {"gather": (), "pair": (), "chip": ()}

    gathering = []

    def gather_units_start(l):
        for k, names in enumerate(SUBLAYERS):
            land = place_own([cw_tiles if n == "conv_w" else W[n] for n in names], [F32 if n == "conv_w" else BF16 for n in names], l, place)
            send_sems, recv_sems, land, tok = gather_start(f"gather_start_l{l}_u{k}", land, queued["gather"])
            gathering.append(dict(names=names, tag=f"l{l}_u{k}", send=send_sems, recv=recv_sems, land=land, tok=tok, arrived=False))
            queued["gather"] = (tok,)

    def gather_unit_arrive(after):
        waiting = [u for u in gathering if not u["arrived"]]
        if not waiting:
            return ()
        unit, tokens = waiting[0], [u["tok"] for u in waiting[1:]]
        land = gather_wait(f"gather_wait_{unit['tag']}", unit["land"], unit["send"], unit["recv"], list(after) + tokens)
        send_sems, recv_sems, land, tok = gather_forward_start(f"gather_pass_start_{unit['tag']}", land)
        unit.update(land=land, send=send_sems, recv=recv_sems, tok=tok, arrived=True)
        return (tok,)

    def gather_unit_weights(after):
        if not gathering[0]["arrived"]:
            gather_unit_arrive(after)
        unit = gathering.pop(0)
        land = gather_forward_wait(f"gather_pass_wait_{unit['tag']}", unit["land"], unit["send"], unit["recv"], after)
        g = dict(zip(unit["names"], land))
        one = lambda a: a.reshape((1,) + a.shape)
        w = {}
        for tag_, up, dn in (("1", "ffn1_w_up", "ffn1_w_down"), ("2", "ffn2_w_up", "ffn2_w_down")):
            if up in g:
                w["wup" + tag_], w["wd" + tag_] = one(g[up]), g[dn].reshape(1, -1, D)
        if "w_in" in g:
            cw_l = g["conv_w"].reshape(N_DEV, -1)[:, :cw_flat.shape[1]].reshape((N_DEV,) + conv_w.shape[1:])
            w.update(win=one(g["w_in"]), wlu=g["w_lru_up"].reshape(1, R, D), wout=g["w_out"].reshape(1, D, D),
                     wpu=g["w_pool_up"].transpose(1, 0, 2).reshape(1, P, D),
                     cw=cw_l.transpose(1, 0, 2).reshape(1, CW, R))
        return w

    vec = lambda a: a.reshape(L, 1, -1)
    p = dict(g1=vec(norm_ffn1), gm=vec(norm_mix), g2=vec(norm_ffn2), pb=vec(pool_b), ps=vec(pool_scale), cb=vec(conv_b),
             ba=vec(lru_b_a), bx=vec(lru_b_x), lam=vec(lru_lambda), pw=pool_w, wa=lru_w_a, wx=lru_w_x)

    AHEAD = 2
    for l in range(min(AHEAD, L)):
        gather_units_start(l)
    saved, LW = [], []
    xc = xin
    for l in range(L):
        w = gather_unit_weights([xc])
        if l + AHEAD < L:
            gather_units_start(l + AHEAD)
        sv = {"x1": xc}
        sv["h1"], sv["u1"], sv["s1"] = ffn_up(xc, p["g1"], w["wup1"], l)
        xc = ffn_down(sv["s1"], w["wd1"], xc, l, gather_unit_arrive([sv["s1"]]))
        sv["x2"] = xc
        w.update(gather_unit_weights([xc]))
        sv["h2"], sv["proj"] = mix_in(xc, p["gm"], w["win"], l)
        sv["pm"] = pool_fwd(sv["proj"], p["pw"], p["pb"], p["ps"], l)
        sv["hl"], sv["hs"] = lru_fwd(sv["proj"], w["cw"], p["cb"], p["wa"], p["ba"], p["wx"], p["bx"], p["lam"], P, l)
        xc, sv["yp"], sv["yl"], sv["z"] = mix_out(sv["pm"], sv["hl"], sv["proj"], xc, w["wpu"], w["wlu"], w["wout"], P, l,
                                                  gather_unit_arrive([sv["hl"]]))
        sv["x3"] = xc
        w.update(gather_unit_weights([xc]))
        sv["h3"], sv["u3"], sv["s3"] = ffn_up(xc, p["g2"], w["wup2"], l)
        xc = ffn_down(sv["s3"], w["wd2"], xc, l, gather_unit_arrive([sv["s3"]]))
        saved.append(sv)
        LW.append(w)

    loss_part, dx, d_final = loss_head(xc, final_norm.reshape(1, D), tgt)
    loss = lax.psum(loss_part[0, 0], ("x", "y", "c"))

    G = [dict() for _ in range(L)]
    small = {n: [None] * L for n in SMALL if n != "final_norm"}

    def to_slots(name, pair):
        if name == "w_pool_up":
            return tuple(a.reshape(1, P, N_DEV, D // N_DEV).transpose(0, 2, 1, 3) for a in pair)
        return tuple(a.reshape((1, N_DEV) + W[name].shape[1:]) for a in pair)

    def ffn_bwd(dy, sv, tag, wup, wd, gn, up_name, dn_name, norm_name, l, deps=()):
        dout, du = ffn_down_bwd(dy, wd, sv["u" + tag], l, deps)
        du = du.reshape(N_DEV, T, cs)
        G[l][dn_name] = to_slots(dn_name, dw_tn("dw_down", sv["s" + tag], lambda tk: pl.BlockSpec((1, tk, cs), lambda g, k: (g, k, 0)),
                                                dout, lambda tk: pl.BlockSpec((tk, D), lambda g, k: (k, 0)), 4, cs, D, T))
        G[l][up_name] = to_slots(up_name, dw_tn("dw_up", du, lambda tk: pl.BlockSpec((1, tk, cs), lambda g, k: (g, k, 0)),
                                                sv["h" + tag], lambda tk: pl.BlockSpec((tk, D), lambda g, k: (k, 0)), N_DEV, cs, D, T))
        dxn, dg = dx_norm_bwd("ffn_dx", du, lambda tm: pl.BlockSpec((1, tm, cs), lambda j, i: (j, i, 0)), wup, N_DEV,
                              sv["x" + tag], gn, dy, l, w_transposed=True)
        small[norm_name][l] = dg.reshape(D)
        return dxn

    pairing, in_flight = [], []

    def reduce_start(l, names, tag):
        names = [n for n in names if n != "conv_w"]
        send_sems, recv_sems, bufs, tok = pair_exchange_start(f"rs_pair_start_{tag}", [G[l][n][1] for n in names], queued["pair"])
        pairing.append((l, names, tag, send_sems, recv_sems, bufs))
        queued["pair"] = (tok,)
        return (tok,)

    def reduce_continue(after):
        l, names, tag, send_sems, recv_sems, bufs = pairing.pop(0)
        recv1 = pair_exchange_wait(f"rs_pair_wait_{tag}", bufs, send_sems, recv_sems, after)
        pair16 = pair_sum([G[l][n][0] for n in names], recv1, place)
        send_sems, recv_sems, bufs, tok = chip_exchange_start(f"rs_chip_start_{tag}", pair16, queued["chip"])
        in_flight.append((l, names, tag, send_sems, recv_sems, bufs, recv1))
        queued["chip"] = (tok,)
        return (tok,)

    def boundary(l, k, dx_now):
        deps = reduce_continue([dx_now]) if pairing else ()
        if len(units(l)) > 1:
            deps += reduce_start(l, units(l)[k], f"l{l}_u{k}")
        elif k == 0:
            deps += reduce_start(l, units(l)[0], f"l{l}_u0")
        return deps

    deps = ()
    for l in reversed(range(L)):
        sv, w = saved[l], LW[l]
        dx = ffn_bwd(dx, sv, "3", w["wup2"], w["wd2"], p["g2"], "ffn2_w_up", "ffn2_w_down", "norm_ffn2", l, deps)
        deps = boundary(l, 2, dx)
        dyb, dyp, dyl, dgp, dgl, dpm, dhl = mix_out_bwd(dx, sv["proj"], sv["yp"], sv["yl"], w["wpu"], w["wlu"], w["wout"], P, R, l, deps)
        row = lambda wd_: (lambda tk: pl.BlockSpec((tk, wd_), lambda g, k: (k, 0)))
        G[l]["w_out"] = to_slots("w_out", dw_tn("dw_out", sv["z"], row(D), dyb, row(D), 1, D, D, T))
        G[l]["w_lru_up"] = to_slots("w_lru_up", dw_tn("dw_lru_up", sv["hl"], row(R), dyl, row(D), 1, R, D, T))
        G[l]["w_pool_up"] = to_slots("w_pool_up", dw_tn("dw_pool_up", sv["pm"], row(P), dyp, row(D), 1, P, D, T))
        du_lru, du_gelu, dcw, dcb, dwa, dba, dwx, dbx, dlam = lru_bwd(
            sv["proj"], sv["hs"], dhl, w["cw"], p["cb"], p["wa"], p["ba"], p["wx"], p["bx"], p["lam"], P, l)
        du_pool, dpw, dpb, dpsc = pool_bwd(sv["proj"], dpm, p["pw"], p["pb"], p["ps"], l)
        dproj = jnp.concatenate([du_pool, du_lru, du_gelu, dgp, dgl], axis=1)
        G[l]["w_in"] = to_slots("w_in", dw_tn("dw_in", sv["h2"], row(D), dproj, lambda tk: pl.BlockSpec((tk, ci), lambda g, k: (k, g)),
                                              N_DEV, D, ci, T))
        dx, dgm = dx_norm_bwd("mix_dx", dproj, lambda tm: pl.BlockSpec((tm, ci), lambda j, i: (i, j)), w["win"], N_DEV,
                              sv["x2"], p["gm"], dx, l)
        small["norm_mix"][l] = dgm.reshape(D)
        small["pool_w"][l], small["pool_b"][l], small["pool_scale"][l] = dpw[0], dpb.reshape(pool_b.shape[1:]), dpsc.reshape(P)
        small["conv_w"][l], small["conv_b"][l] = dcw[0], dcb.reshape(R)
        small["lru_w_a"][l], small["lru_b_a"][l] = dwa[0], dba.reshape(H, hd)
        small["lru_w_x"][l], small["lru_b_x"][l] = dwx[0], dbx.reshape(H, hd)
        small["lru_lambda"][l] = dlam.reshape(R)
        deps = boundary(l, 1, dx)
        dx = ffn_bwd(dx, sv, "1", w["wup1"], w["wd1"], p["g1"], "ffn1_w_up", "ffn1_w_down", "norm_ffn1", l, deps)
        deps = boundary(l, 0, dx)

    grad_x = dx.reshape(x.shape)

    small_parts = [jnp.stack(small[n]) for n in SMALL if n != "final_norm"] + [d_final.reshape(D)]
    small_shapes = [p.shape for p in small_parts]
    gpack = _pack(small_parts).reshape(1, N_DEV, -1, 128)
    small_pair = pair_exchange_start("rs_pair_start_small", [gpack], queued["pair"])
    while pairing:
        reduce_continue([dx])

    outs = {n: None for n in BIG}

    def unit_updates(unit, recv2, deps=()):
        l, names, recv1 = unit[0], unit[1], unit[6]
        for i, n in enumerate(names):
            outs[n] = grad_sum_adamw(G[l][n][0], recv1[i], recv2[i], W[n], M[n], V[n], place, l, outs[n], deps)
        return [outs[n][0] for n in names]

    after = [dx]
    late = in_flight[-2:]
    for k, unit in enumerate(in_flight[:-2]):
        recv2 = chip_exchange_wait(f"rs_chip_wait_{unit[2]}", unit[5], unit[3], unit[4], after)
        after = unit_updates(unit, recv2)
        if k == 0:
            recv1_s = pair_exchange_wait("rs_pair_wait_small", small_pair[2], small_pair[0], small_pair[1], after)[0]
            small_chip = chip_exchange_start("rs_chip_start_small", pair_sum([gpack], [recv1_s], place), queued["chip"])
    late_recv2 = []
    for unit in late:
        late_recv2.append(chip_exchange_wait(f"rs_chip_wait_{unit[2]}", unit[5], unit[3], unit[4], after))
        after = [late_recv2[-1][0]]
    recv2_s = chip_exchange_wait("rs_chip_wait_small", small_chip[2], small_chip[0], small_chip[1], after)[0]
    gs = grad_sum(gpack, recv1_s, recv2_s, place)
    gs_slots = place_own([gs.reshape((1,) + gs.shape)], [F32], 0, place)
    send_sems, recv_sems, gs_slots, tok = gather_start("gather_start_small", gs_slots)
    after = unit_updates(late[0], late_recv2[0], (tok,))
    gs_slots = gather_wait("gather_wait_small", gs_slots, send_sems, recv_sems, after)
    send_sems, recv_sems, gs_slots, tok = gather_forward_start("gather_pass_start_small", gs_slots)
    after = unit_updates(late[1], late_recv2[1], (tok,))
    gs_all = gather_forward_wait("gather_pass_wait_small", gs_slots, send_sems, recv_sems, after)[0].reshape(-1, 128)
    for n in ("ffn1_w_up", "ffn2_w_up"):
        outs[n] = [jnp.swapaxes(o, 1, 2) for o in outs[n]]
    out_g, out_d, out_m, out_v = ({n: outs[n][k] for n in BIG} for k in range(4))

    small_g = dict(zip(SMALL, _unpack(gs_all, small_shapes)))
    for n in SMALL:
        if n != "conv_w":
            flat = lambda a: a.reshape(-1, 128)
            out_g[n] = small_g[n]
            out_d[n], out_m[n], out_v[n] = (o.reshape(W[n].shape) for o in adamw(flat(W[n]), flat(small_g[n]), flat(M[n]), flat(V[n])))
    cwc = conv_w.shape[2]
    gcw = lax.dynamic_slice_in_dim(small_g["conv_w"], dev * cwc, cwc, axis=2)
    cw2 = lambda a: a.reshape(-1, cwc)
    pad_rows = (-cw2(conv_w).shape[0]) % 8
    padr = lambda a: jnp.pad(cw2(a), ((0, pad_rows), (0, 0)))
    dcw_, mcw_, vcw_ = adamw(padr(conv_w), padr(gcw), padr(M["conv_w"]), padr(V["conv_w"]))
    nrow = cw2(conv_w).shape[0]
    out_g["conv_w"] = gcw
    out_d["conv_w"], out_m["conv_w"], out_v["conv_w"] = (a[:nrow].reshape(conv_w.shape) for a in (dcw_, mcw_, vcw_))

    return (loss, grad_x, *[out_g[n] for n in NAMES], *[out_d[n] for n in NAMES], *[out_m[n] for n in NAMES], *[out_v[n] for n in NAMES])
```

```python
import jax
import jax.numpy as jnp
from jax import lax
from jax.experimental import pallas as pl
from jax.experimental.pallas import tpu as pltpu

F32, BF16 = jnp.float32, jnp.bfloat16
EPS = 1e-6
LRU_C = 8.0
POOL_WINDOWS = (2, 4, 8, 16)
ADAM_LR, ADAM_B1, ADAM_B2, ADAM_EPS, ADAM_WD, ADAM_STEP = 0.001, 0.9, 0.999, 1e-08, 0.01, 10
N_DEV = 8
N_CHIP = 4
MESH = pl.DeviceIdType.MESH
V7X_VMEM_LIMIT = 56 * 1024 * 1024
ROW_TILE = 512
WIDE_TILE = 1024
SUM_TILE = 2048
ANY = pl.BlockSpec(memory_space=pl.ANY)

_pallas_call = pl.pallas_call


def _cp(*sem):
    return pltpu.CompilerParams(dimension_semantics=sem if sem else None, vmem_limit_bytes=V7X_VMEM_LIMIT)


def _tile(n, t):
    t = min(n, t)
    assert n % t == 0, (n, t)
    return t


def _dot(a, b):
    return jnp.dot(a, b, preferred_element_type=F32)


def _dot_nt(a, b):
    return lax.dot_general(a, b, (((1,), (1,)), ((), ())), preferred_element_type=F32)


def _dot_tn(a, b):
    return lax.dot_general(a, b, (((0,), (0,)), ((), ())), preferred_element_type=F32)


def _rms(xv):
    r = lax.rsqrt(jnp.mean(xv * xv, axis=-1, keepdims=True) + EPS)
    return xv * r, r


def _rms_bwd(dh, xv, gv, dy):
    n, r = _rms(xv)
    dn = dh * gv
    dx = dy + r * (dn - n * jnp.mean(dn * n, axis=-1, keepdims=True))
    return dx, jnp.sum(dh * n, axis=0, keepdims=True)


def _shift_down(x, k, fill=0.0):
    if k == 0:
        return x
    rows = lax.broadcasted_iota(jnp.int32, x.shape, 0)
    return jnp.where(rows >= k, pltpu.roll(x, k, 0), fill)


def _shift_up(x, k, fill=0.0):
    if k == 0:
        return x
    n = x.shape[0]
    rows = lax.broadcasted_iota(jnp.int32, x.shape, 0)
    return jnp.where(rows < n - k, pltpu.roll(x, n - k, 0), fill)


def _sigmoid(x):
    return 0.5 * jnp.tanh(0.5 * x) + 0.5


_GELU_K = 0.7978845608028654
_GELU_C = 0.044715


def _gelu(x):
    th = jnp.tanh(_GELU_K * (x + _GELU_C * x * x * x))
    return 0.5 * x * (1.0 + th), th


def _gelu_grad(x, th):
    return 0.5 * (1.0 + th) + 0.5 * x * (1.0 - th * th) * _GELU_K * (1.0 + 3.0 * _GELU_C * x * x)


def ffn_up(x, g, wup, l):
    T, D = x.shape
    cs = wup.shape[-2]
    tm = _tile(T, WIDE_TILE)
    ni = T // tm

    def body(x_ref, g_ref, wa_ref, wb_ref, h_ref, u_ref, s_ref, hs_ref):
        rows = pl.ds(pl.multiple_of(pl.program_id(1) * tm, tm), tm)

        @pl.when(pl.program_id(0) == 0)
        def _():
            n, _r = _rms(x_ref[...])
            hv = (n * g_ref[0]).astype(BF16)
            hs_ref[rows, :] = hv
            h_ref[...] = hv

        hv = hs_ref[rows, :]
        a = _dot_nt(hv, wa_ref[0, 0])
        b = _dot_nt(hv, wb_ref[0, 0])
        u_ref[0, 0] = a.astype(BF16)
        u_ref[1, 0] = b.astype(BF16)
        s_ref[0] = (a * _sigmoid(a) * b).astype(BF16)

    first = lambda j, i: (jnp.where(j == 0, i, ni - 1), 0)
    return _pallas_call(
        body, name="ffn_up", grid=(4, ni),
        in_specs=[pl.BlockSpec((tm, D), first), pl.BlockSpec((1, 1, D), lambda j, i: (l, 0, 0)),
                  pl.BlockSpec((1, 1, cs, D), lambda j, i: (0, j, 0, 0)), pl.BlockSpec((1, 1, cs, D), lambda j, i: (0, j + 4, 0, 0))],
        out_specs=[pl.BlockSpec((tm, D), first), pl.BlockSpec((2, 1, tm, cs), lambda j, i: (0, j, i, 0)),
                   pl.BlockSpec((1, tm, cs), lambda j, i: (j, i, 0))],
        out_shape=[jax.ShapeDtypeStruct((T, D), BF16), jax.ShapeDtypeStruct((2, 4, T, cs), BF16), jax.ShapeDtypeStruct((4, T, cs), BF16)],
        scratch_shapes=[pltpu.VMEM((T, D), BF16)],
        compiler_params=_cp("arbitrary", "arbitrary"),
    )(x, g, wup, wup)


def ffn_down(s, wd, x, l, deps=()):
    _, T, cs = s.shape
    D = x.shape[1]
    tm = _tile(T, WIDE_TILE)

    def body(s_ref, w_ref, x_ref, *rest):
        o_ref, acc_ref = rest[len(deps):]
        j = pl.program_id(1)

        @pl.when(j == 0)
        def _():
            acc_ref[...] = jnp.zeros_like(acc_ref)

        acc_ref[...] += _dot(s_ref[0], w_ref[0])

        @pl.when(j == 3)
        def _():
            o_ref[...] = x_ref[...] + 0.5 * acc_ref[...]

    return _pallas_call(
        body, name="ffn_down", grid=(T // tm, 4),
        in_specs=[pl.BlockSpec((1, tm, cs), lambda i, j: (j, i, 0)), pl.BlockSpec((1, cs, D), lambda i, j: (0, j, 0)),
                  pl.BlockSpec((tm, D), lambda i, j: (i, 0))] + [ANY] * len(deps),
        out_specs=pl.BlockSpec((tm, D), lambda i, j: (i, 0)),
        out_shape=jax.ShapeDtypeStruct((T, D), F32),
        scratch_shapes=[pltpu.VMEM((tm, D), F32)],
        compiler_params=_cp("parallel", "arbitrary"),
    )(s, wd, x, *deps)


def mix_in(x, g, win, l):
    T, D = x.shape
    ci = win.shape[-1]
    tm = _tile(T, WIDE_TILE)
    ni = T // tm

    def body(x_ref, g_ref, w_ref, h_ref, p_ref, hs_ref):
        rows = pl.ds(pl.multiple_of(pl.program_id(1) * tm, tm), tm)

        @pl.when(pl.program_id(0) == 0)
        def _():
            n, _r = _rms(x_ref[...])
            hv = (n * g_ref[0]).astype(BF16)
            hs_ref[rows, :] = hv
            h_ref[...] = hv

        p_ref[...] = _dot(hs_ref[rows, :], w_ref[0, 0]).astype(BF16)

    first = lambda j, i: (jnp.where(j == 0, i, ni - 1), 0)
    return _pallas_call(
        body, name="mix_in", grid=(N_DEV, ni),
        in_specs=[pl.BlockSpec((tm, D), first), pl.BlockSpec((1, 1, D), lambda j, i: (l, 0, 0)),
                  pl.BlockSpec((1, 1, D, ci), lambda j, i: (0, j, 0, 0))],
        out_specs=[pl.BlockSpec((tm, D), first), pl.BlockSpec((tm, ci), lambda j, i: (i, j))],
        out_shape=[jax.ShapeDtypeStruct((T, D), BF16), jax.ShapeDtypeStruct((T, N_DEV * ci), BF16)],
        scratch_shapes=[pltpu.VMEM((T, D), BF16)],
        compiler_params=_cp("arbitrary", "arbitrary"),
    )(x, g, win)


def _inv_count(T, w):
    t = lax.broadcasted_iota(jnp.int32, (T, 1), 0)
    return 1.0 / jnp.minimum(t + 1, w).astype(F32)


def _pooled(ug, w, inv):
    s = ug
    k = 1
    while k < w:
        s = s + _shift_down(s, k)
        k *= 2
    return s * inv - ug


def pool_fwd(proj, pw, pb, ps, l):
    T = proj.shape[0]
    _, G, gd, _ = pw.shape
    P = G * gd

    def body(u_ref, w_ref, b_ref, s_ref, o_ref):
        for gi in range(G):
            cols = slice(gi * gd, (gi + 1) * gd)
            ug = u_ref[:, cols].astype(F32)
            pooled = _pooled(ug, POOL_WINDOWS[gi], _inv_count(T, POOL_WINDOWS[gi]))
            mixed = _dot(pooled.astype(BF16), w_ref[0, gi].astype(BF16)) + b_ref[0, :, cols]
            o_ref[:, cols] = (mixed * s_ref[0, :, cols]).astype(BF16)

    return _pallas_call(
        body, name="pool_fwd", grid=(1,),
        in_specs=[pl.BlockSpec((T, P), lambda i: (0, 0)), pl.BlockSpec((1, G, gd, gd), lambda i: (l, 0, 0, 0)),
                  pl.BlockSpec((1, 1, P), lambda i: (l, 0, 0)), pl.BlockSpec((1, 1, P), lambda i: (l, 0, 0))],
        out_specs=pl.BlockSpec((T, P), lambda i: (0, 0)),
        out_shape=jax.ShapeDtypeStruct((T, P), BF16),
        compiler_params=_cp("arbitrary"),
    )(proj, pw, pb, ps)


def _conv(u, cw_ref, cb):
    CW = cw_ref.shape[1]
    v = cb
    for k in range(CW):
        v = v + cw_ref[0, k:k + 1, :] * _shift_down(u, CW - 1 - k)
    return v


def _softplus(z):
    return jnp.maximum(z, 0.0) + jnp.log1p(jnp.exp(-jnp.abs(z)))


def _lru_gates(v, wa_ref, ba, wx_ref, bx, lam):
    vb = v.astype(BF16)
    r = _sigmoid(_dot(vb, wa_ref[0, 0].astype(BF16)) + ba)
    i = _sigmoid(_dot(vb, wx_ref[0, 0].astype(BF16)) + bx)
    sp = _softplus(-lam)
    log_a = -LRU_C * r * sp
    a = jnp.exp(log_a)
    m2 = -jnp.tanh(log_a) * (a * a + 1.0)
    inv_mult = lax.rsqrt(m2)
    mult = jnp.where(m2 > 0.0, m2 * inv_mult, 0.0)
    return r, i, sp, a, mult, inv_mult


def _scan_fwd(a_ref, b_ref, o_ref):
    T, W = a_ref.shape
    rows = lax.broadcasted_iota(jnp.int32, (8, W), 0)

    def step(t, carry):
        r0 = pl.multiple_of(t * 8, 8)
        A = a_ref[pl.ds(r0, 8), :]
        B = b_ref[pl.ds(r0, 8), :]
        for s in (1, 2, 4):
            keep = rows >= s
            As = jnp.where(keep, pltpu.roll(A, s, 0), 1.0)
            Bs = jnp.where(keep, pltpu.roll(B, s, 0), 0.0)
            B = A * Bs + B
            A = A * As
        h = B + A * carry
        o_ref[pl.ds(r0, 8), :] = h
        return jnp.broadcast_to(h[7:8, :], (8, W))

    lax.fori_loop(0, T // 8, step, jnp.zeros((8, W), F32), unroll=8)


def _scan_bwd(a_ref, b_ref, o_ref):
    T, W = a_ref.shape
    rows = lax.broadcasted_iota(jnp.int32, (8, W), 0)
    nt = T // 8

    def step(t, carry):
        r0 = pl.multiple_of((nt - 1 - t) * 8, 8)
        A = a_ref[pl.ds(r0, 8), :]
        B = b_ref[pl.ds(r0, 8), :]
        for s in (1, 2, 4):
            keep = rows < 8 - s
            As = jnp.where(keep, pltpu.roll(A, 8 - s, 0), 1.0)
            Bs = jnp.where(keep, pltpu.roll(B, 8 - s, 0), 0.0)
            B = A * Bs + B
            A = A * As
        y = B + A * carry
        o_ref[pl.ds(r0, 8), :] = y
        return jnp.broadcast_to(y[0:1, :], (8, W))

    lax.fori_loop(0, nt, step, jnp.zeros((8, W), F32), unroll=8)


def _lru_specs(T, hd, P, R, CW, l):
    ob, gb = P // hd, (P + R) // hd
    vec = pl.BlockSpec((1, 1, hd), lambda h: (l, 0, h))
    mat = pl.BlockSpec((1, 1, hd, hd), lambda h: (l, h, 0, 0))
    return [pl.BlockSpec((T, hd), lambda h: (0, ob + h)), pl.BlockSpec((T, hd), lambda h: (0, gb + h)),
            pl.BlockSpec((1, CW, hd), lambda h: (0, 0, h)), vec, mat, vec, mat, vec, vec]


def lru_fwd(proj, cw, cb, wa, ba, wx, bx, lam, P, l):
    T = proj.shape[0]
    _, H, hd, _ = wa.shape
    R = H * hd
    CW = cw.shape[1]
    assert P % hd == 0 and T % 8 == 0

    def body(u_ref, ug_ref, cw_ref, cb_ref, wa_ref, ba_ref, wx_ref, bx_ref, lam_ref, hl_ref, hs_ref, a_s, b_s):
        v = _conv(u_ref[...].astype(F32), cw_ref, cb_ref[0])
        _r, i, _sp, a, mult, _im = _lru_gates(v, wa_ref, ba_ref[0], wx_ref, bx_ref[0], lam_ref[0])
        a_s[...] = a
        b_s[...] = mult * (i * v)
        _scan_fwd(a_s, b_s, hs_ref)
        ge, _th = _gelu(ug_ref[...].astype(F32))
        hl_ref[...] = (hs_ref[...] * ge).astype(BF16)

    out = pl.BlockSpec((T, hd), lambda h: (0, h))
    return _pallas_call(
        body, name="lru_fwd", grid=(H,),
        in_specs=_lru_specs(T, hd, P, R, CW, l),
        out_specs=[out, out],
        out_shape=[jax.ShapeDtypeStruct((T, R), BF16), jax.ShapeDtypeStruct((T, R), F32)],
        scratch_shapes=[pltpu.VMEM((T, hd), F32)] * 2,
        compiler_params=_cp("parallel"),
    )(proj, proj, cw, cb, wa, ba, wx, bx, lam)


def mix_out(pm, hl, proj, x, wpu, wlu, wout, P, l, deps=()):
    T, D = x.shape
    R = hl.shape[1]
    tm = _tile(T, ROW_TILE)
    assert (P + 2 * R) % D == 0
    gb = (P + 2 * R) // D

    def body(pm_ref, hl_ref, gp_ref, gl_ref, x_ref, wpu_ref, wlu_ref, wo_ref, *rest):
        o_ref, yp_ref, yl_ref, z_ref = rest[len(deps):]
        yp = _dot(pm_ref[...], wpu_ref[0])
        yl = _dot(hl_ref[...], wlu_ref[0])
        z = (_sigmoid(gp_ref[...].astype(F32)) * yp + _sigmoid(gl_ref[...].astype(F32)) * yl).astype(BF16)
        yp_ref[...] = yp.astype(BF16)
        yl_ref[...] = yl.astype(BF16)
        z_ref[...] = z
        o_ref[...] = x_ref[...] + _dot(z, wo_ref[0])

    row = lambda w: pl.BlockSpec((tm, w), lambda i: (i, 0))
    return _pallas_call(
        body, name="mix_out", grid=(T // tm,),
        in_specs=[row(P), row(R), pl.BlockSpec((tm, D), lambda i: (i, gb)), pl.BlockSpec((tm, D), lambda i: (i, gb + 1)), row(D),
                  pl.BlockSpec((1, P, D), lambda i: (0, 0, 0)), pl.BlockSpec((1, R, D), lambda i: (0, 0, 0)),
                  pl.BlockSpec((1, D, D), lambda i: (0, 0, 0))] + [ANY] * len(deps),
        out_specs=[row(D)] * 4,
        out_shape=[jax.ShapeDtypeStruct((T, D), F32)] + [jax.ShapeDtypeStruct((T, D), BF16)] * 3,
        compiler_params=_cp("parallel"),
    )(pm, hl, proj, proj, x, wpu, wlu, wout, *deps)


def loss_head(x, gf, tgt):
    T, D = x.shape
    tm = _tile(T, ROW_TILE)

    def body(x_ref, g_ref, t_ref, loss_ref, dx_ref, dg_ref):
        @pl.when(pl.program_id(0) == 0)
        def _():
            loss_ref[...] = jnp.zeros_like(loss_ref)
            dg_ref[...] = jnp.zeros_like(dg_ref)

        xv = x_ref[...]
        gv = g_ref[...]
        n, _r = _rms(xv)
        e = n * gv - t_ref[...]
        loss_ref[...] += 0.5 * jnp.sum(jnp.sum(e * e, axis=-1, keepdims=True), axis=0, keepdims=True) / D
        dx, dg = _rms_bwd(e * (1.0 / D), xv, gv, 0.0)
        dx_ref[...] = dx
        dg_ref[...] += dg

    return _pallas_call(
        body, name="loss_head", grid=(T // tm,),
        in_specs=[pl.BlockSpec((tm, D), lambda i: (i, 0)), pl.BlockSpec((1, D), lambda i: (0, 0)), pl.BlockSpec((tm, D), lambda i: (i, 0))],
        out_specs=[pl.BlockSpec((1, 1), lambda i: (0, 0)), pl.BlockSpec((tm, D), lambda i: (i, 0)), pl.BlockSpec((1, D), lambda i: (0, 0))],
        out_shape=[jax.ShapeDtypeStruct((1, 1), F32), jax.ShapeDtypeStruct((T, D), F32), jax.ShapeDtypeStruct((1, D), F32)],
        compiler_params=_cp("arbitrary"),
    )(x, gf, tgt)


def ffn_down_bwd(dy, wd, u, l, deps=()):
    T, D = dy.shape
    cs = u.shape[-1]
    tm = _tile(T, WIDE_TILE)
    ni = T // tm

    def body(dy_ref, w_ref, u_ref, *rest):
        do_ref, du_ref, dyb_ref = rest[len(deps):]
        rows = pl.ds(pl.multiple_of(pl.program_id(1) * tm, tm), tm)

        @pl.when(pl.program_id(0) == 0)
        def _():
            d = (0.5 * dy_ref[...]).astype(BF16)
            dyb_ref[rows, :] = d
            do_ref[...] = d

        ds = _dot_nt(dyb_ref[rows, :], w_ref[0])
        a = u_ref[0, 0].astype(F32)
        b = u_ref[1, 0].astype(F32)
        sg = _sigmoid(a)
        du_ref[0, 0] = (ds * b * (sg * (1.0 + a * (1.0 - sg)))).astype(BF16)
        du_ref[1, 0] = (ds * (a * sg)).astype(BF16)

    first = lambda j, i: (jnp.where(j == 0, i, ni - 1), 0)
    blk = pl.BlockSpec((2, 1, tm, cs), lambda j, i: (0, j, i, 0))
    return _pallas_call(
        body, name="ffn_down_bwd", grid=(4, ni),
        in_specs=[pl.BlockSpec((tm, D), first), pl.BlockSpec((1, cs, D), lambda j, i: (0, j, 0)), blk] + [ANY] * len(deps),
        out_specs=[pl.BlockSpec((tm, D), first), blk],
        out_shape=[jax.ShapeDtypeStruct((T, D), BF16), jax.ShapeDtypeStruct((2, 4, T, cs), BF16)],
        scratch_shapes=[pltpu.VMEM((T, D), BF16)],
        compiler_params=_cp("arbitrary", "arbitrary"),
    )(dy, wd, u, *deps)


def dw_tn(name, a, a_spec, b, b_spec, G, M, N, T):
    tk = _tile(T, SUM_TILE)
    nk = T // tk

    def body(a_ref, b_ref, o32_ref, o16_ref, acc_ref):
        k = pl.program_id(1)

        @pl.when(k == 0)
        def _():
            acc_ref[...] = jnp.zeros_like(acc_ref)

        av = a_ref[0] if len(a_ref.shape) == 3 else a_ref[...]
        bv = b_ref[0] if len(b_ref.shape) == 3 else b_ref[...]
        acc_ref[...] += _dot_tn(av, bv)

        @pl.when(k == nk - 1)
        def _():
            o32_ref[0, 0] = acc_ref[...]
            o16_ref[0, 0] = acc_ref[...].astype(BF16)

    out = pl.BlockSpec((1, 1, M, N), lambda g, k: (0, g, 0, 0))
    return _pallas_call(
        body, name=name, grid=(G, nk),
        in_specs=[a_spec(tk), b_spec(tk)], out_specs=[out, out],
        out_shape=[jax.ShapeDtypeStruct((1, G, M, N), F32), jax.ShapeDtypeStruct((1, G, M, N), BF16)],
        scratch_shapes=[pltpu.VMEM((M, N), F32)],
        compiler_params=_cp("parallel", "arbitrary"),
    )(a, b)


def dx_norm_bwd(name, dact, d_spec, w, G, x, g, dy, l, w_transposed=False):
    T, D = x.shape
    wblk = w.shape[-2:]
    tm = _tile(T, WIDE_TILE)
    ni = T // tm
    ch = _tile(tm, ROW_TILE // 2)

    def body(d_ref, w_ref, x_ref, g_ref, dy_ref, dx_ref, dg_ref, acc_ref):
        j, i = pl.program_id(0), pl.program_id(1)
        rows = pl.ds(pl.multiple_of(i * tm, tm), tm)

        @pl.when(jnp.logical_and(i == 0, j == 0))
        def _():
            dg_ref[...] = jnp.zeros_like(dg_ref)

        @pl.when(j == 0)
        def _():
            acc_ref[rows, :] = jnp.zeros((tm, D), F32)

        dv = d_ref[0] if len(d_ref.shape) == 3 else d_ref[...]
        acc_ref[rows, :] += _dot(dv, w_ref[0, 0]) if w_transposed else _dot_nt(dv, w_ref[0, 0])

        @pl.when(j == G - 1)
        def _():
            dg = jnp.zeros((1, D), F32)
            for c0 in range(0, tm, ch):
                part_rows = pl.ds(pl.multiple_of(i * tm + c0, ch), ch)
                dx, dgc = _rms_bwd(acc_ref[part_rows, :], x_ref[c0:c0 + ch, :], g_ref[0], dy_ref[c0:c0 + ch, :])
                dx_ref[c0:c0 + ch, :] = dx
                dg = dg + dgc
            dg_ref[...] += dg

    last = pl.BlockSpec((tm, D), lambda j, i: (jnp.where(j == G - 1, i, 0), 0))
    return _pallas_call(
        body, name=name, grid=(G, ni),
        in_specs=[d_spec(tm), pl.BlockSpec((1, 1) + wblk, lambda j, i: (0, j, 0, 0)), last, pl.BlockSpec((1, 1, D), lambda j, i: (l, 0, 0)), last],
        out_specs=[last, pl.BlockSpec((1, D), lambda j, i: (0, 0))],
        out_shape=[jax.ShapeDtypeStruct((T, D), F32), jax.ShapeDtypeStruct((1, D), F32)],
        scratch_shapes=[pltpu.VMEM((T, D), F32)],
        compiler_params=_cp("arbitrary", "arbitrary"),
    )(dact, w, x, g, dy)


def mix_out_bwd(dy, proj, yp, yl, wpu, wlu, wout, P, R, l, deps=()):
    T, D = dy.shape
    tm = _tile(T, ROW_TILE)
    gb = (P + 2 * R) // D

    def body(dy_ref, gp_ref, gl_ref, yp_ref, yl_ref, wpu_ref, wlu_ref, wo_ref, *rest):
        dyb_ref, dyp_ref, dyl_ref, dgp_ref, dgl_ref, dpm_ref, dhl_ref = rest[len(deps):]
        dyb = dy_ref[...].astype(BF16)
        dyb_ref[...] = dyb
        dz = _dot_nt(dyb, wo_ref[0])
        sp = _sigmoid(gp_ref[...].astype(F32))
        sl = _sigmoid(gl_ref[...].astype(F32))
        dgp_ref[...] = (dz * yp_ref[...].astype(F32) * sp * (1.0 - sp)).astype(BF16)
        dgl_ref[...] = (dz * yl_ref[...].astype(F32) * sl * (1.0 - sl)).astype(BF16)
        dyp = (dz * sp).astype(BF16)
        dyl = (dz * sl).astype(BF16)
        dyp_ref[...] = dyp
        dyl_ref[...] = dyl
        dpm_ref[...] = _dot_nt(dyp, wpu_ref[0]).astype(BF16)
        dhl_ref[...] = _dot_nt(dyl, wlu_ref[0]).astype(BF16)

    row = lambda w: pl.BlockSpec((tm, w), lambda i: (i, 0))
    return _pallas_call(
        body, name="mix_out_bwd", grid=(T // tm,),
        in_specs=[row(D), pl.BlockSpec((tm, D), lambda i: (i, gb)), pl.BlockSpec((tm, D), lambda i: (i, gb + 1)), row(D), row(D),
                  pl.BlockSpec((1, P, D), lambda i: (0, 0, 0)), pl.BlockSpec((1, R, D), lambda i: (0, 0, 0)),
                  pl.BlockSpec((1, D, D), lambda i: (0, 0, 0))] + [ANY] * len(deps),
        out_specs=[row(D)] * 5 + [row(P), row(R)],
        out_shape=[jax.ShapeDtypeStruct((T, D), BF16)] * 5 + [jax.ShapeDtypeStruct((T, P), BF16), jax.ShapeDtypeStruct((T, R), BF16)],
        compiler_params=_cp("parallel"),
    )(dy, proj, proj, yp, yl, wpu, wlu, wout, *deps)


def lru_bwd(proj, hs, dhl, cw, cb, wa, ba, wx, bx, lam, P, l):
    T = proj.shape[0]
    _, H, hd, _ = wa.shape
    R = H * hd
    CW = cw.shape[1]

    def body(u_ref, ug_ref, cw_ref, cb_ref, wa_ref, ba_ref, wx_ref, bx_ref, lam_ref, hs_ref, dhl_ref,
             du_ref, dug_ref, dcw_ref, dcb_ref, dwa_ref, dba_ref, dwx_ref, dbx_ref, dlam_ref, c_s, g_s, y_s):
        u = u_ref[...].astype(F32)
        v = _conv(u, cw_ref, cb_ref[0])
        lam = lam_ref[0]
        r, i, sp, a, mult, inv_mult = _lru_gates(v, wa_ref, ba_ref[0], wx_ref, bx_ref[0], lam)
        ug = ug_ref[...].astype(F32)
        ge, th = _gelu(ug)
        hs = hs_ref[...]
        dhl = dhl_ref[...].astype(F32)
        dug_ref[...] = (dhl * hs * _gelu_grad(ug, th)).astype(BF16)
        c_s[...] = _shift_up(a, 1)
        g_s[...] = dhl * ge
        _scan_bwd(c_s, g_s, y_s)
        y = y_s[...]
        da = y * _shift_down(hs, 1)
        iv = i * v
        dlog_a = da * a - (y * iv) * (a * a) * inv_mult
        div = y * mult
        dpa = (dlog_a * (-LRU_C) * sp) * r * (1.0 - r)
        dpx = (div * v) * i * (1.0 - i)
        dsp = jnp.sum(dlog_a * (-LRU_C) * r, axis=0, keepdims=True)
        dlam_ref[0] = -dsp * _sigmoid(-lam)
        vb = v.astype(BF16)
        dpab, dpxb = dpa.astype(BF16), dpx.astype(BF16)
        dwa_ref[0, 0] = _dot_tn(vb, dpab)
        dwx_ref[0, 0] = _dot_tn(vb, dpxb)
        dba_ref[0] = jnp.sum(dpa, axis=0, keepdims=True)
        dbx_ref[0] = jnp.sum(dpx, axis=0, keepdims=True)
        dv = div * i + _dot_nt(dpab, wa_ref[0, 0].astype(BF16)) + _dot_nt(dpxb, wx_ref[0, 0].astype(BF16))
        dcb_ref[0] = jnp.sum(dv, axis=0, keepdims=True)
        du = jnp.zeros_like(dv)
        for k in range(CW):
            du = du + cw_ref[0, k:k + 1, :] * _shift_up(dv, CW - 1 - k)
            dcw_ref[0, k:k + 1, :] = jnp.sum(dv * _shift_down(u, CW - 1 - k), axis=0, keepdims=True)
        du_ref[...] = du.astype(BF16)

    col = pl.BlockSpec((T, hd), lambda h: (0, h))
    vec = pl.BlockSpec((1, 1, hd), lambda h: (0, 0, h))
    mat = pl.BlockSpec((1, 1, hd, hd), lambda h: (0, h, 0, 0))
    vshape = jax.ShapeDtypeStruct((1, 1, R), F32)
    mshape = jax.ShapeDtypeStruct((1, H, hd, hd), F32)
    return _pallas_call(
        body, name="lru_bwd", grid=(H,),
        in_specs=_lru_specs(T, hd, P, R, CW, l) + [col, col],
        out_specs=[col, col, pl.BlockSpec((1, CW, hd), lambda h: (0, 0, h)), vec, mat, vec, mat, vec, vec],
        out_shape=[jax.ShapeDtypeStruct((T, R), BF16)] * 2 + [jax.ShapeDtypeStruct((1, CW, R), F32), vshape, mshape, vshape, mshape, vshape, vshape],
        scratch_shapes=[pltpu.VMEM((T, hd), F32)] * 3,
        compiler_params=_cp("parallel"),
    )(proj, proj, cw, cb, wa, ba, wx, bx, lam, hs, dhl)


def pool_bwd(proj, dpm, pw, pb, ps, l):
    T = proj.shape[0]
    _, G, gd, _ = pw.shape
    P = G * gd

    def body(u_ref, d_ref, w_ref, b_ref, s_ref, du_ref, dw_ref, db_ref, dsc_ref):
        for gi in range(G):
            cols = slice(gi * gd, (gi + 1) * gd)
            w = POOL_WINDOWS[gi]
            inv = _inv_count(T, w)
            ug = u_ref[:, cols].astype(F32)
            pooled = _pooled(ug, w, inv).astype(BF16)
            wb = w_ref[0, gi].astype(BF16)
            mixed = _dot(pooled, wb) + b_ref[0, :, cols]
            dpm_g = d_ref[:, cols].astype(F32)
            dsc_ref[0, :, cols] = jnp.sum(dpm_g * mixed, axis=0, keepdims=True)
            dmixed = dpm_g * s_ref[0, :, cols]
            db_ref[0, :, cols] = jnp.sum(dmixed, axis=0, keepdims=True)
            dmb = dmixed.astype(BF16)
            dw_ref[0, gi] = _dot_tn(pooled, dmb)
            dpooled = _dot_nt(dmb, wb)
            s = dpooled * inv
            k = 1
            while k < w:
                s = s + _shift_up(s, k)
                k *= 2
            du_ref[:, cols] = (s - dpooled).astype(BF16)

    vec = pl.BlockSpec((1, 1, P), lambda i: (l, 0, 0))
    ovec = pl.BlockSpec((1, 1, P), lambda i: (0, 0, 0))
    return _pallas_call(
        body, name="pool_bwd", grid=(1,),
        in_specs=[pl.BlockSpec((T, P), lambda i: (0, 0)), pl.BlockSpec((T, P), lambda i: (0, 0)),
                  pl.BlockSpec((1, G, gd, gd), lambda i: (l, 0, 0, 0)), vec, vec],
        out_specs=[pl.BlockSpec((T, P), lambda i: (0, 0)), pl.BlockSpec((1, G, gd, gd), lambda i: (0, 0, 0, 0)), ovec, ovec],
        out_shape=[jax.ShapeDtypeStruct((T, P), BF16), jax.ShapeDtypeStruct((1, G, gd, gd), F32),
                   jax.ShapeDtypeStruct((1, 1, P), F32), jax.ShapeDtypeStruct((1, 1, P), F32)],
        compiler_params=_cp("arbitrary"),
    )(proj, dpm, pw, pb, ps)


def _place():
    x, y, c = lax.axis_index("x"), lax.axis_index("y"), lax.axis_index("c")
    return x, y, c


HBM = pl.BlockSpec(memory_space=pltpu.HBM)
SEM = pl.BlockSpec(memory_space=pltpu.SEMAPHORE)
EFFECT = pltpu.SideEffectType.DATAFLOW_SIDE_EFFECTING


def _in_hbm(a):
    return pltpu.with_memory_space_constraint(a, pltpu.HBM)


def split_start(name, bufs, n_copies, copies_of, deps=()):
    nb = len(bufs)

    def body(*refs):
        buf = refs[:nb]
        send_sems, recv_sems = refs[nb + len(deps)], refs[nb + len(deps) + 1]
        token = refs[-1]
        for i, (src, dst, dev) in enumerate(copies_of(buf)):
            pltpu.make_async_remote_copy(src_ref=src, dst_ref=dst, send_sem=send_sems.at[i], recv_sem=recv_sems.at[i],
                                         device_id=dev, device_id_type=MESH).start()
        token[...] = jnp.zeros_like(token)

    outs = _pallas_call(
        body, name=name,
        in_specs=[HBM] * nb + [ANY] * len(deps),
        out_specs=(SEM, SEM, *([HBM] * nb), pl.BlockSpec(memory_space=pltpu.VMEM)),
        out_shape=(pltpu.SemaphoreType.DMA((n_copies,)), pltpu.SemaphoreType.DMA((n_copies,)),
                   *[pltpu.HBM(b.shape, b.dtype) for b in bufs], jax.ShapeDtypeStruct((8, 128), F32)),
        input_output_aliases={i: 2 + i for i in range(nb)},
        compiler_params=pltpu.CompilerParams(has_side_effects=EFFECT),
    )(*[_in_hbm(b) for b in bufs], *deps)
    return outs[0], outs[1], list(outs[2:2 + nb]), outs[-1]


def split_wait(name, bufs, send_sems, recv_sems, after, copies_of):
    nb = len(bufs)

    def body(*refs):
        buf = refs[:nb]
        send, recv = refs[nb], refs[nb + 1]
        for i, (src, dst, dev) in enumerate(copies_of(buf)):
            cp = pltpu.make_async_remote_copy(src_ref=src, dst_ref=dst, send_sem=send.at[i], recv_sem=recv.at[i],
                                              device_id=dev, device_id_type=MESH)
            cp.wait_send()
            cp.wait_recv()

    outs = _pallas_call(
        body, name=name,
        in_specs=[HBM] * nb + [SEM, SEM] + [ANY] * len(after),
        out_specs=[HBM] * nb,
        out_shape=[pltpu.HBM(b.shape, b.dtype) for b in bufs],
        input_output_aliases={i: i for i in range(nb)},
        compiler_params=pltpu.CompilerParams(has_side_effects=EFFECT),
    )(*bufs, send_sems, recv_sems, *after)
    return list(outs)


def _two_row_blocks(rows):
    return (rows // 2, 1) if rows % 32 == 0 else (rows, 0)


def place_own(ws, dtypes, l, place):
    n = len(ws)

    def body(p_ref, *refs):
        for a in range(n):
            refs[n + a][0] = refs[a][0].astype(dtypes[a])

    in_specs, out_specs, out_shape = [], [], []
    for w, dt in zip(ws, dtypes):
        _, rows, cols = w.shape
        rb, step = _two_row_blocks(rows)
        in_specs.append(pl.BlockSpec((1, rb, cols), lambda i, p, s=step: (l, i * s, 0)))
        out_specs.append(pl.BlockSpec((1, rb, cols), lambda i, p, s=step: (p[2], i * s, 0)))
        out_shape.append(jax.ShapeDtypeStruct((N_DEV, rows, cols), dt))
    return list(_pallas_call(
        body, name="place_own",
        grid_spec=pltpu.PrefetchScalarGridSpec(num_scalar_prefetch=1, grid=(2,), in_specs=in_specs, out_specs=out_specs),
        out_shape=out_shape, compiler_params=_cp("arbitrary"),
    )(place, *ws))


def _gather_copies(land):
    x, y, c = _place()
    k = 4 * x + 2 * y + c
    peers = [(x, 1 - y, c), (1 - x, y, c), (1 - x, 1 - y, c), (x, y, 1 - c)]
    return [(b.at[k], b.at[k], p) for p in peers for b in land]


def gather_start(name, land, deps=()):
    return split_start(name, land, 4 * len(land), _gather_copies, deps)


def gather_wait(name, land, send_sems, recv_sems, after):
    return split_wait(name, land, send_sems, recv_sems, after, _gather_copies)


def _forward_copies(land):
    x, y, c = _place()
    slots = [4 * px + 2 * py + c for px, py in [(x, 1 - y), (1 - x, y), (1 - x, 1 - y)]]
    return [(b.at[k], b.at[k], (x, y, 1 - c)) for k in slots for b in land]


def gather_forward_start(name, land, deps=()):
    return split_start(name, land, 3 * len(land), _forward_copies, deps)


def gather_forward_wait(name, land, send_sems, recv_sems, after):
    return split_wait(name, land, send_sems, recv_sems, after, _forward_copies)


def _chip_copies(nsrc):
    def copies(buf):
        p16, recv2 = buf[:nsrc], buf[nsrc:]
        x, y, c = _place()
        out = []
        for d in (1, 2, 3):
            px = 1 - x if d & 2 else x
            py = 1 - y if d & 1 else y
            out += [(p16[a].at[:, d - 1], recv2[a].at[:, d - 1], (px, py, c)) for a in range(nsrc)]
        return out
    return copies


def _pair_copies(nsrc):
    def copies(buf):
        g16, recv = buf[:nsrc], buf[nsrc:]
        x, y, c = _place()
        return [(g16[a].at[:, 2 * j + 1 - c], recv[a].at[:, j], (x, y, 1 - c)) for a in range(nsrc) for j in range(N_CHIP)]
    return copies


def pair_exchange_start(name, g16, deps=()):
    n = len(g16)
    land = [lax.empty((1, N_CHIP) + s.shape[2:], s.dtype) for s in g16]
    return split_start(name, list(g16) + land, N_CHIP * n, _pair_copies(n), deps)


def pair_exchange_wait(name, bufs, send_sems, recv_sems, after):
    n = len(bufs) // 2
    return split_wait(name, bufs, send_sems, recv_sems, after, _pair_copies(n))[n:]


def chip_exchange_start(name, pair16, deps=()):
    n = len(pair16)
    land = [lax.empty((s.shape[0], 3) + s.shape[2:], s.dtype) for s in pair16]
    return split_start(name, list(pair16) + land, 3 * n, _chip_copies(n), deps)


def chip_exchange_wait(name, bufs, send_sems, recv_sems, after):
    n = len(bufs) // 2
    return split_wait(name, bufs, send_sems, recv_sems, after, _chip_copies(n))[n:]


def _rows_tile(rows, cols, budget=1 << 20):
    t = rows
    while t % 2 == 0 and t * cols > budget and (t // 2) % 16 == 0:
        t //= 2
    return t


def pair_sum(g32s, recv1s, place):
    n = len(g32s)

    def body(p_ref, *refs):
        for a in range(n):
            m_ref, r_ref, o_ref = refs[a], refs[n + a], refs[2 * n + a]
            o_ref[...] = (m_ref[...] + r_ref[...].astype(F32)).astype(o_ref.dtype)

    other = lambda d, p: jnp.bitwise_xor(p[1], d + 1)
    g_specs, r_specs, o_specs, out_shape = [], [], [], []
    for r1 in recv1s:
        _, _, rows, cols = r1.shape
        rb, step = _two_row_blocks(rows)
        g_specs.append(pl.BlockSpec((1, 1, rb, cols), lambda d, i, p, s=step: (0, 2 * other(d, p) + p[0], i * s, 0)))
        r_specs.append(pl.BlockSpec((1, 1, rb, cols), lambda d, i, p, s=step: (0, other(d, p), i * s, 0)))
        o_specs.append(pl.BlockSpec((1, 1, rb, cols), lambda d, i, p, s=step: (0, d, i * s, 0)))
        out_shape.append(jax.ShapeDtypeStruct((1, N_CHIP - 1, rows, cols), r1.dtype))
    return list(_pallas_call(
        body, name="pair_sum",
        grid_spec=pltpu.PrefetchScalarGridSpec(num_scalar_prefetch=1, grid=(N_CHIP - 1, 2), in_specs=g_specs + r_specs, out_specs=o_specs),
        out_shape=out_shape, compiler_params=_cp("arbitrary", "arbitrary"),
    )(place, *g32s, *recv1s))


def _grad_in_specs(tr, cols, l):
    return ([pl.BlockSpec((1, 1, tr, cols), lambda i, p: (l, p[2], i, 0)), pl.BlockSpec((1, 1, tr, cols), lambda i, p: (0, p[1], i, 0))]
            + [pl.BlockSpec((1, 1, tr, cols), lambda i, p, d=d: (0, d, i, 0)) for d in range(3)])


def _grad_total(o32, o16, r0, r1, r2):
    return (o32[0, 0] + o16[0, 0].astype(F32)) + r0[0, 0].astype(F32) + r1[0, 0].astype(F32) + r2[0, 0].astype(F32)


def grad_sum(g32, recv1, recv2, place):
    _, _, rows, cols = recv1.shape
    tr = _rows_tile(rows, cols)

    def body(p_ref, o32, o16, r0, r1, r2, g_ref):
        g_ref[...] = _grad_total(o32, o16, r0, r1, r2)

    return _pallas_call(
        body, name="grad_sum",
        grid_spec=pltpu.PrefetchScalarGridSpec(
            num_scalar_prefetch=1, grid=(rows // tr,), in_specs=_grad_in_specs(tr, cols, 0),
            out_specs=pl.BlockSpec((tr, cols), lambda i, p: (i, 0))),
        out_shape=jax.ShapeDtypeStruct((rows, cols), F32), compiler_params=_cp("parallel"),
    )(place, g32, recv1, recv2, recv2, recv2)


def _adamw_math(w, g, m, v):
    m = ADAM_B1 * m + (1.0 - ADAM_B1) * g
    v = ADAM_B2 * v + (1.0 - ADAM_B2) * (g * g)
    m_hat = m / (1.0 - ADAM_B1 ** ADAM_STEP)
    v_hat = v / (1.0 - ADAM_B2 ** ADAM_STEP)
    delta = -ADAM_LR * (m_hat / (jnp.sqrt(v_hat) + ADAM_EPS) + ADAM_WD * w)
    return delta, m, v


UPDATE_BLOCK = 1 << 16


def _update_rows(rows, cols):
    if rows % 16:
        return rows, 1
    tiles = rows // 16
    d = max([k for k in range(1, tiles + 1) if tiles % k == 0 and 16 * k * cols <= UPDATE_BLOCK] or [1])
    return 16 * d, tiles // d


def grad_sum_adamw(g32s, recv1s, recv2s, ws, ms, vs, place, l, prevs, deps=()):
    n = len(ws)
    blocks = [_update_rows(w.shape[1], w.shape[2]) for w in ws]
    have_prev = prevs[0] is not None

    def body(p_ref, *refs):
        outs = refs[len(refs) - 4 * n:]

        def update(a):
            o32, o16, r0, r1, r2, w_ref, m_ref, v_ref = refs[8 * a:8 * a + 8]
            g_ref, d_ref, nm_ref, nv_ref = outs[4 * a:4 * a + 4]

            @pl.when(pl.program_id(0) < blocks[a][1])
            def _():
                g = _grad_total(o32, o16, r0, r1, r2)
                d, nm, nv = _adamw_math(w_ref[0], g, m_ref[0], v_ref[0])
                g_ref[0] = g
                d_ref[0] = d
                nm_ref[0] = nm
                nv_ref[0] = nv

        for a in range(n):
            update(a)

    args, in_specs, out_specs, out_shape = [], [], [], []
    for a in range(n):
        L, rows, cols = ws[a].shape
        rb, nb = blocks[a]
        at = lambda i, nb=nb: jnp.minimum(i, nb - 1)
        slot = lambda which: pl.BlockSpec((1, 1, rb, cols), lambda i, p, at=at: (0, which(p), at(i), 0))
        blk = pl.BlockSpec((1, rb, cols), lambda i, p, at=at: (l, at(i), 0))
        args += [g32s[a], recv1s[a], recv2s[a], recv2s[a], recv2s[a], ws[a], ms[a], vs[a]]
        in_specs += [slot(lambda p: p[2]), slot(lambda p: p[1])] + [slot(lambda p, d=d: d) for d in range(3)] + [blk] * 3
        out_specs += [blk] * 4
        out_shape += [jax.ShapeDtypeStruct((L, rows, cols), F32)] * 4
    aliases = {}
    if have_prev:
        aliases = {1 + len(args) + k: k for k in range(4 * n)}
        args += [buf for prev in prevs for buf in prev]
        in_specs += [ANY] * (4 * n)
    args += list(deps)
    in_specs += [ANY] * len(deps)
    outs = _pallas_call(
        body, name="grad_sum_adamw",
        grid_spec=pltpu.PrefetchScalarGridSpec(num_scalar_prefetch=1, grid=(max(nb for _, nb in blocks),),
                                               in_specs=in_specs, out_specs=out_specs),
        out_shape=out_shape, input_output_aliases=aliases, compiler_params=_cp("arbitrary"),
    )(place, *args)
    return [list(outs[4 * a:4 * a + 4]) for a in range(n)]


def adamw(w, g, m, v):
    rows, cols = w.shape
    tr = _rows_tile(rows, cols, 1 << 18)

    def body(w_ref, g_ref, m_ref, v_ref, d_ref, nm_ref, nv_ref):
        d, nm, nv = _adamw_math(w_ref[...], g_ref[...], m_ref[...], v_ref[...])
        d_ref[...] = d
        nm_ref[...] = nm
        nv_ref[...] = nv

    blk = pl.BlockSpec((tr, cols), lambda i: (i, 0))
    return _pallas_call(body, name="adamw_small", grid=(rows // tr,), in_specs=[blk] * 4, out_specs=[blk] * 3,
                        out_shape=[jax.ShapeDtypeStruct((rows, cols), F32)] * 3, compiler_params=_cp("parallel"))(w, g, m, v)


SMALL = ("norm_ffn1", "norm_mix", "pool_w", "pool_b", "pool_scale", "conv_w", "conv_b", "lru_w_a", "lru_b_a", "lru_w_x", "lru_b_x",
         "lru_lambda", "norm_ffn2", "final_norm")
BIG = ("ffn1_w_up", "ffn1_w_down", "w_in", "w_pool_up", "w_lru_up", "w_out", "ffn2_w_up", "ffn2_w_down")
NAMES = ("norm_ffn1", "ffn1_w_up", "ffn1_w_down", "norm_mix", "w_in", "pool_w", "pool_b", "pool_scale", "w_pool_up", "conv_w", "conv_b",
         "lru_w_a", "lru_b_a", "lru_w_x", "lru_b_x", "lru_lambda", "w_lru_up", "w_out", "norm_ffn2", "ffn2_w_up", "ffn2_w_down", "final_norm")
SUBLAYERS = (("ffn1_w_up", "ffn1_w_down"), ("w_in", "w_pool_up", "w_lru_up", "w_out", "conv_w"), ("ffn2_w_up", "ffn2_w_down"))
PACK_ROWS = 16 * N_DEV


def _pack(parts):
    flat = jnp.concatenate([p.reshape(-1) for p in parts])
    unit = 128 * PACK_ROWS
    padded = -(-flat.size // unit) * unit
    return jnp.pad(flat, (0, padded - flat.size)).reshape(-1, 128)


def _unpack(packed, shapes):
    flat = packed.reshape(-1)
    out, off = [], 0
    for s in shapes:
        n = 1
        for d in s:
            n *= d
        out.append(flat[off:off + n].reshape(s))
        off += n
    return out


def kernel(x, norm_ffn1, ffn1_w_up, ffn1_w_down, norm_mix, w_in, pool_w, pool_b, pool_scale, w_pool_up, conv_w, conv_b, lru_w_a, lru_b_a, lru_w_x, lru_b_x, lru_lambda, w_lru_up, w_out, norm_ffn2, ffn2_w_up, ffn2_w_down, final_norm, loss_target, m_norm_ffn1, m_ffn1_w_up, m_ffn1_w_down, m_norm_mix, m_w_in, m_pool_w, m_pool_b, m_pool_scale, m_w_pool_up, m_conv_w, m_conv_b, m_lru_w_a, m_lru_b_a, m_lru_w_x, m_lru_b_x, m_lru_lambda, m_w_lru_up, m_w_out, m_norm_ffn2, m_ffn2_w_up, m_ffn2_w_down, m_final_norm, v_norm_ffn1, v_ffn1_w_up, v_ffn1_w_down, v_norm_mix, v_w_in, v_pool_w, v_pool_b, v_pool_scale, v_w_pool_up, v_conv_w, v_conv_b, v_lru_w_a, v_lru_b_a, v_lru_w_x, v_lru_b_x, v_lru_lambda, v_w_lru_up, v_w_out, v_norm_ffn2, v_ffn2_w_up, v_ffn2_w_down, v_final_norm):
    W = dict(norm_ffn1=norm_ffn1, ffn1_w_up=ffn1_w_up, ffn1_w_down=ffn1_w_down, norm_mix=norm_mix, w_in=w_in, pool_w=pool_w, pool_b=pool_b,
             pool_scale=pool_scale, w_pool_up=w_pool_up, conv_w=conv_w, conv_b=conv_b, lru_w_a=lru_w_a, lru_b_a=lru_b_a, lru_w_x=lru_w_x,
             lru_b_x=lru_b_x, lru_lambda=lru_lambda, w_lru_up=w_lru_up, w_out=w_out, norm_ffn2=norm_ffn2, ffn2_w_up=ffn2_w_up,
             ffn2_w_down=ffn2_w_down, final_norm=final_norm)
    M = dict(norm_ffn1=m_norm_ffn1, ffn1_w_up=m_ffn1_w_up, ffn1_w_down=m_ffn1_w_down, norm_mix=m_norm_mix, w_in=m_w_in, pool_w=m_pool_w,
             pool_b=m_pool_b, pool_scale=m_pool_scale, w_pool_up=m_w_pool_up, conv_w=m_conv_w, conv_b=m_conv_b, lru_w_a=m_lru_w_a,
             lru_b_a=m_lru_b_a, lru_w_x=m_lru_w_x, lru_b_x=m_lru_b_x, lru_lambda=m_lru_lambda, w_lru_up=m_w_lru_up, w_out=m_w_out,
             norm_ffn2=m_norm_ffn2, ffn2_w_up=m_ffn2_w_up, ffn2_w_down=m_ffn2_w_down, final_norm=m_final_norm)
    V = dict(norm_ffn1=v_norm_ffn1, ffn1_w_up=v_ffn1_w_up, ffn1_w_down=v_ffn1_w_down, norm_mix=v_norm_mix, w_in=v_w_in, pool_w=v_pool_w,
             pool_b=v_pool_b, pool_scale=v_pool_scale, w_pool_up=v_w_pool_up, conv_w=v_conv_w, conv_b=v_conv_b, lru_w_a=v_lru_w_a,
             lru_b_a=v_lru_b_a, lru_w_x=v_lru_w_x, lru_b_x=v_lru_b_x, lru_lambda=v_lru_lambda, w_lru_up=v_w_lru_up, w_out=v_w_out,
             norm_ffn2=v_norm_ffn2, ffn2_w_up=v_ffn2_w_up, ffn2_w_down=v_ffn2_w_down, final_norm=v_final_norm)

    for S in (W, M, V):
        for n in ("ffn1_w_up", "ffn2_w_up"):
            S[n] = jnp.swapaxes(S[n], 1, 2)

    T, D = x.shape[1], x.shape[2]
    L = norm_ffn1.shape[0]
    P = pool_scale.shape[1]
    R = lru_lambda.shape[1]
    H, hd = lru_w_a.shape[1], lru_w_a.shape[2]
    CW = conv_w.shape[1]
    cs = ffn1_w_up.shape[2]
    ci = w_in.shape[2]
    xin = x.reshape(T, D)
    tgt = loss_target.reshape(T, D)
    dev = 4 * lax.axis_index("x") + 2 * lax.axis_index("y") + lax.axis_index("c")
    place = jnp.stack([lax.axis_index("c"), 2 * lax.axis_index("x") + lax.axis_index("y"), dev]).astype(jnp.int32)

    cw_flat = conv_w.reshape(L, -1)
    cw_pad = (-cw_flat.shape[1]) % 1024
    cw_tiles = jnp.pad(cw_flat, ((0, 0), (0, cw_pad))).reshape(L, -1, 128)

    def units(l):
        return SUBLAYERS if l == 0 else (tuple(n for u in SUBLAYERS for n in u),)

    queued = {"gather": (), "pair": (), "chip": ()}

    gathering = []

    def gather_units_start(l):
        for k, names in enumerate(SUBLAYERS):
            land = place_own([cw_tiles if n == "conv_w" else W[n] for n in names], [F32 if n == "conv_w" else BF16 for n in names], l, place)
            send_sems, recv_sems, land, tok = gather_start(f"gather_start_l{l}_u{k}", land, queued["gather"])
            gathering.append(dict(names=names, tag=f"l{l}_u{k}", send=send_sems, recv=recv_sems, land=land, tok=tok, arrived=False))
            queued["gather"] = (tok,)

    def gather_unit_arrive(after):
        waiting = [u for u in gathering if not u["arrived"]]
        if not waiting:
            return ()
        unit, tokens = waiting[0], [u["tok"] for u in waiting[1:]]
        land = gather_wait(f"gather_wait_{unit['tag']}", unit["land"], unit["send"], unit["recv"], list(after) + tokens)
        send_sems, recv_sems, land, tok = gather_forward_start(f"gather_pass_start_{unit['tag']}", land)
        unit.update(land=land, send=send_sems, recv=recv_sems, tok=tok, arrived=True)
        return (tok,)

    def gather_unit_weights(after):
        if not gathering[0]["arrived"]:
            gather_unit_arrive(after)
        unit = gathering.pop(0)
        land = gather_forward_wait(f"gather_pass_wait_{unit['tag']}", unit["land"], unit["send"], unit["recv"], after)
        g = dict(zip(unit["names"], land))
        one = lambda a: a.reshape((1,) + a.shape)
        w = {}
        for tag_, up, dn in (("1", "ffn1_w_up", "ffn1_w_down"), ("2", "ffn2_w_up", "ffn2_w_down")):
            if up in g:
                w["wup" + tag_], w["wd" + tag_] = one(g[up]), g[dn].reshape(1, -1, D)
        if "w_in" in g:
            cw_l = g["conv_w"].reshape(N_DEV, -1)[:, :cw_flat.shape[1]].reshape((N_DEV,) + conv_w.shape[1:])
            w.update(win=one(g["w_in"]), wlu=g["w_lru_up"].reshape(1, R, D), wout=g["w_out"].reshape(1, D, D),
                     wpu=g["w_pool_up"].transpose(1, 0, 2).reshape(1, P, D),
                     cw=cw_l.transpose(1, 0, 2).reshape(1, CW, R))
        return w

    vec = lambda a: a.reshape(L, 1, -1)
    p = dict(g1=vec(norm_ffn1), gm=vec(norm_mix), g2=vec(norm_ffn2), pb=vec(pool_b), ps=vec(pool_scale), cb=vec(conv_b),
             ba=vec(lru_b_a), bx=vec(lru_b_x), lam=vec(lru_lambda), pw=pool_w, wa=lru_w_a, wx=lru_w_x)

    AHEAD = 2
    for l in range(min(AHEAD, L)):
        gather_units_start(l)
    saved, LW = [], []
    xc = xin
    for l in range(L):
        w = gather_unit_weights([xc])
        if l + AHEAD < L:
            gather_units_start(l + AHEAD)
        sv = {"x1": xc}
        sv["h1"], sv["u1"], sv["s1"] = ffn_up(xc, p["g1"], w["wup1"], l)
        xc = ffn_down(sv["s1"], w["wd1"], xc, l, gather_unit_arrive([sv["s1"]]))
        sv["x2"] = xc
        w.update(gather_unit_weights([xc]))
        sv["h2"], sv["proj"] = mix_in(xc, p["gm"], w["win"], l)
        sv["pm"] = pool_fwd(sv["proj"], p["pw"], p["pb"], p["ps"], l)
        sv["hl"], sv["hs"] = lru_fwd(sv["proj"], w["cw"], p["cb"], p["wa"], p["ba"], p["wx"], p["bx"], p["lam"], P, l)
        xc, sv["yp"], sv["yl"], sv["z"] = mix_out(sv["pm"], sv["hl"], sv["proj"], xc, w["wpu"], w["wlu"], w["wout"], P, l,
                                                  gather_unit_arrive([sv["hl"]]))
        sv["x3"] = xc
        w.update(gather_unit_weights([xc]))
        sv["h3"], sv["u3"], sv["s3"] = ffn_up(xc, p["g2"], w["wup2"], l)
        xc = ffn_down(sv["s3"], w["wd2"], xc, l, gather_unit_arrive([sv["s3"]]))
        saved.append(sv)
        LW.append(w)

    loss_part, dx, d_final = loss_head(xc, final_norm.reshape(1, D), tgt)
    loss = lax.psum(loss_part[0, 0], ("x", "y", "c"))

    G = [dict() for _ in range(L)]
    small = {n: [None] * L for n in SMALL if n != "final_norm"}

    def to_slots(name, pair):
        if name == "w_pool_up":
            return tuple(a.reshape(1, P, N_DEV, D // N_DEV).transpose(0, 2, 1, 3) for a in pair)
        return tuple(a.reshape((1, N_DEV) + W[name].shape[1:]) for a in pair)

    def ffn_bwd(dy, sv, tag, wup, wd, gn, up_name, dn_name, norm_name, l, deps=()):
        dout, du = ffn_down_bwd(dy, wd, sv["u" + tag], l, deps)
        du = du.reshape(N_DEV, T, cs)
        G[l][dn_name] = to_slots(dn_name, dw_tn("dw_down", sv["s" + tag], lambda tk: pl.BlockSpec((1, tk, cs), lambda g, k: (g, k, 0)),
                                                dout, lambda tk: pl.BlockSpec((tk, D), lambda g, k: (k, 0)), 4, cs, D, T))
        G[l][up_name] = to_slots(up_name, dw_tn("dw_up", du, lambda tk: pl.BlockSpec((1, tk, cs), lambda g, k: (g, k, 0)),
                                                sv["h" + tag], lambda tk: pl.BlockSpec((tk, D), lambda g, k: (k, 0)), N_DEV, cs, D, T))
        dxn, dg = dx_norm_bwd("ffn_dx", du, lambda tm: pl.BlockSpec((1, tm, cs), lambda j, i: (j, i, 0)), wup, N_DEV,
                              sv["x" + tag], gn, dy, l, w_transposed=True)
        small[norm_name][l] = dg.reshape(D)
        return dxn

    pairing, in_flight = [], []

    def reduce_start(l, names, tag):
        names = [n for n in names if n != "conv_w"]
        send_sems, recv_sems, bufs, tok = pair_exchange_start(f"rs_pair_start_{tag}", [G[l][n][1] for n in names], queued["pair"])
        pairing.append((l, names, tag, send_sems, recv_sems, bufs))
        queued["pair"] = (tok,)
        return (tok,)

    def reduce_continue(after):
        l, names, tag, send_sems, recv_sems, bufs = pairing.pop(0)
        recv1 = pair_exchange_wait(f"rs_pair_wait_{tag}", bufs, send_sems, recv_sems, after)
        pair16 = pair_sum([G[l][n][0] for n in names], recv1, place)
        send_sems, recv_sems, bufs, tok = chip_exchange_start(f"rs_chip_start_{tag}", pair16, queued["chip"])
        in_flight.append((l, names, tag, send_sems, recv_sems, bufs, recv1))
        queued["chip"] = (tok,)
        return (tok,)

    def boundary(l, k, dx_now):
        deps = reduce_continue([dx_now]) if pairing else ()
        if len(units(l)) > 1:
            deps += reduce_start(l, units(l)[k], f"l{l}_u{k}")
        elif k == 0:
            deps += reduce_start(l, units(l)[0], f"l{l}_u0")
        return deps

    deps = ()
    for l in reversed(range(L)):
        sv, w = saved[l], LW[l]
        dx = ffn_bwd(dx, sv, "3", w["wup2"], w["wd2"], p["g2"], "ffn2_w_up", "ffn2_w_down", "norm_ffn2", l, deps)
        deps = boundary(l, 2, dx)
        dyb, dyp, dyl, dgp, dgl, dpm, dhl = mix_out_bwd(dx, sv["proj"], sv["yp"], sv["yl"], w["wpu"], w["wlu"], w["wout"], P, R, l, deps)
        row = lambda wd_: (lambda tk: pl.BlockSpec((tk, wd_), lambda g, k: (k, 0)))
        G[l]["w_out"] = to_slots("w_out", dw_tn("dw_out", sv["z"], row(D), dyb, row(D), 1, D, D, T))
        G[l]["w_lru_up"] = to_slots("w_lru_up", dw_tn("dw_lru_up", sv["hl"], row(R), dyl, row(D), 1, R, D, T))
        G[l]["w_pool_up"] = to_slots("w_pool_up", dw_tn("dw_pool_up", sv["pm"], row(P), dyp, row(D), 1, P, D, T))
        du_lru, du_gelu, dcw, dcb, dwa, dba, dwx, dbx, dlam = lru_bwd(
            sv["proj"], sv["hs"], dhl, w["cw"], p["cb"], p["wa"], p["ba"], p["wx"], p["bx"], p["lam"], P, l)
        du_pool, dpw, dpb, dpsc = pool_bwd(sv["proj"], dpm, p["pw"], p["pb"], p["ps"], l)
        dproj = jnp.concatenate([du_pool, du_lru, du_gelu, dgp, dgl], axis=1)
        G[l]["w_in"] = to_slots("w_in", dw_tn("dw_in", sv["h2"], row(D), dproj, lambda tk: pl.BlockSpec((tk, ci), lambda g, k: (k, g)),
                                              N_DEV, D, ci, T))
        dx, dgm = dx_norm_bwd("mix_dx", dproj, lambda tm: pl.BlockSpec((tm, ci), lambda j, i: (i, j)), w["win"], N_DEV,
                              sv["x2"], p["gm"], dx, l)
        small["norm_mix"][l] = dgm.reshape(D)
        small["pool_w"][l], small["pool_b"][l], small["pool_scale"][l] = dpw[0], dpb.reshape(pool_b.shape[1:]), dpsc.reshape(P)
        small["conv_w"][l], small["conv_b"][l] = dcw[0], dcb.reshape(R)
        small["lru_w_a"][l], small["lru_b_a"][l] = dwa[0], dba.reshape(H, hd)
        small["lru_w_x"][l], small["lru_b_x"][l] = dwx[0], dbx.reshape(H, hd)
        small["lru_lambda"][l] = dlam.reshape(R)
        deps = boundary(l, 1, dx)
        dx = ffn_bwd(dx, sv, "1", w["wup1"], w["wd1"], p["g1"], "ffn1_w_up", "ffn1_w_down", "norm_ffn1", l, deps)
        deps = boundary(l, 0, dx)

    grad_x = dx.reshape(x.shape)

    small_parts = [jnp.stack(small[n]) for n in SMALL if n != "final_norm"] + [d_final.reshape(D)]
    small_shapes = [p.shape for p in small_parts]
    gpack = _pack(small_parts).reshape(1, N_DEV, -1, 128)
    small_pair = pair_exchange_start("rs_pair_start_small", [gpack], queued["pair"])
    while pairing:
        reduce_continue([dx])

    outs = {n: None for n in BIG}

    def unit_updates(unit, recv2, deps=()):
        l, names, recv1 = unit[0], unit[1], unit[6]
        updated = grad_sum_adamw([G[l][n][0] for n in names], recv1, recv2, [W[n] for n in names], [M[n] for n in names],
                                 [V[n] for n in names], place, l, [outs[n] for n in names], deps)
        outs.update(zip(names, updated))
        return [outs[n][0] for n in names]

    after = [dx]
    late = in_flight[-2:]
    for k, unit in enumerate(in_flight[:-2]):
        recv2 = chip_exchange_wait(f"rs_chip_wait_{unit[2]}", unit[5], unit[3], unit[4], after)
        after = unit_updates(unit, recv2)
        if k == 0:
            recv1_s = pair_exchange_wait("rs_pair_wait_small", small_pair[2], small_pair[0], small_pair[1], after)[0]
            small_chip = chip_exchange_start("rs_chip_start_small", pair_sum([gpack], [recv1_s], place), queued["chip"])
    late_recv2 = []
    for unit in late:
        late_recv2.append(chip_exchange_wait(f"rs_chip_wait_{unit[2]}", unit[5], unit[3], unit[4], after))
        after = [late_recv2[-1][0]]
    recv2_s = chip_exchange_wait("rs_chip_wait_small", small_chip[2], small_chip[0], small_chip[1], after)[0]
    gs = grad_sum(gpack, recv1_s, recv2_s, place)
    gs_slots = place_own([gs.reshape((1,) + gs.shape)], [F32], 0, place)
    send_sems, recv_sems, gs_slots, tok = gather_start("gather_start_small", gs_slots)
    after = unit_updates(late[0], late_recv2[0], (tok,))
    gs_slots = gather_wait("gather_wait_small", gs_slots, send_sems, recv_sems, after)
    send_sems, recv_sems, gs_slots, tok = gather_forward_start("gather_pass_start_small", gs_slots)
    after = unit_updates(late[1], late_recv2[1], (tok,))
    gs_all = gather_forward_wait("gather_pass_wait_small", gs_slots, send_sems, recv_sems, after)[0].reshape(-1, 128)
    for n in ("ffn1_w_up", "ffn2_w_up"):
        outs[n] = [jnp.swapaxes(o, 1, 2) for o in outs[n]]
    out_g, out_d, out_m, out_v = ({n: outs[n][k] for n in BIG} for k in range(4))

    small_g = dict(zip(SMALL, _unpack(gs_all, small_shapes)))
    for n in SMALL:
        if n != "conv_w":
            flat = lambda a: a.reshape(-1, 128)
            out_g[n] = small_g[n]
            out_d[n], out_m[n], out_v[n] = (o.reshape(W[n].shape) for o in adamw(flat(W[n]), flat(small_g[n]), flat(M[n]), flat(V[n])))
    cwc = conv_w.shape[2]
    gcw = lax.dynamic_slice_in_dim(small_g["conv_w"], dev * cwc, cwc, axis=2)
    cw2 = lambda a: a.reshape(-1, cwc)
    pad_rows = (-cw2(conv_w).shape[0]) % 8
    padr = lambda a: jnp.pad(cw2(a), ((0, pad_rows), (0, 0)))
    dcw_, mcw_, vcw_ = adamw(padr(conv_w), padr(gcw), padr(M["conv_w"]), padr(V["conv_w"]))
    nrow = cw2(conv_w).shape[0]
    out_g["conv_w"] = gcw
    out_d["conv_w"], out_m["conv_w"], out_v["conv_w"] = (a[:nrow].reshape(conv_w.shape) for a in (dcw_, mcw_, vcw_))

    return (loss, grad_x, *[out_g[n] for n in NAMES], *[out_d[n] for n in NAMES], *[out_m[n] for n in NAMES], *[out_v[n] for n in NAMES])
```

```python
import jax
import jax.numpy as jnp
from jax import lax
from jax.experimental import pallas as pl
from jax.experimental.pallas import tpu as pltpu

F32, BF16 = jnp.float32, jnp.bfloat16
EPS = 1e-6
LRU_C = 8.0
POOL_WINDOWS = (2, 4, 8, 16)
ADAM_LR, ADAM_B1, ADAM_B2, ADAM_EPS, ADAM_WD, ADAM_STEP = 0.001, 0.9, 0.999, 1e-08, 0.01, 10
N_DEV = 8
N_CHIP = 4
MESH = pl.DeviceIdType.MESH
V7X_VMEM_LIMIT = 56 * 1024 * 1024
ROW_TILE = 512
WIDE_TILE = 1024
SUM_TILE = 2048
ANY = pl.BlockSpec(memory_space=pl.ANY)

_pallas_call = pl.pallas_call


def _cp(*sem):
    return pltpu.CompilerParams(dimension_semantics=sem if sem else None, vmem_limit_bytes=V7X_VMEM_LIMIT)


def _tile(n, t):
    t = min(n, t)
    assert n % t == 0, (n, t)
    return t


def _dot(a, b):
    return jnp.dot(a, b, preferred_element_type=F32)


def _dot_nt(a, b):
    return lax.dot_general(a, b, (((1,), (1,)), ((), ())), preferred_element_type=F32)


def _dot_tn(a, b):
    return lax.dot_general(a, b, (((0,), (0,)), ((), ())), preferred_element_type=F32)


def _rms(xv):
    r = lax.rsqrt(jnp.mean(xv * xv, axis=-1, keepdims=True) + EPS)
    return xv * r, r


def _rms_bwd(dh, xv, gv, dy):
    n, r = _rms(xv)
    dn = dh * gv
    dx = dy + r * (dn - n * jnp.mean(dn * n, axis=-1, keepdims=True))
    return dx, jnp.sum(dh * n, axis=0, keepdims=True)


def _shift_down(x, k, fill=0.0):
    if k == 0:
        return x
    rows = lax.broadcasted_iota(jnp.int32, x.shape, 0)
    return jnp.where(rows >= k, pltpu.roll(x, k, 0), fill)


def _shift_up(x, k, fill=0.0):
    if k == 0:
        return x
    n = x.shape[0]
    rows = lax.broadcasted_iota(jnp.int32, x.shape, 0)
    return jnp.where(rows < n - k, pltpu.roll(x, n - k, 0), fill)


def _sigmoid(x):
    return 0.5 * jnp.tanh(0.5 * x) + 0.5


_GELU_K = 0.7978845608028654
_GELU_C = 0.044715


def _gelu(x):
    th = jnp.tanh(_GELU_K * (x + _GELU_C * x * x * x))
    return 0.5 * x * (1.0 + th), th


def _gelu_grad(x, th):
    return 0.5 * (1.0 + th) + 0.5 * x * (1.0 - th * th) * _GELU_K * (1.0 + 3.0 * _GELU_C * x * x)


def ffn_up(x, g, wup, l):
    T, D = x.shape
    cs = wup.shape[-2]
    tm = _tile(T, WIDE_TILE)
    ni = T // tm

    def body(x_ref, g_ref, wa_ref, wb_ref, h_ref, u_ref, s_ref, hs_ref):
        rows = pl.ds(pl.multiple_of(pl.program_id(1) * tm, tm), tm)

        @pl.when(pl.program_id(0) == 0)
        def _():
            n, _r = _rms(x_ref[...])
            hv = (n * g_ref[0]).astype(BF16)
            hs_ref[rows, :] = hv
            h_ref[...] = hv

        hv = hs_ref[rows, :]
        a = _dot_nt(hv, wa_ref[0, 0])
        b = _dot_nt(hv, wb_ref[0, 0])
        u_ref[0, 0] = a.astype(BF16)
        u_ref[1, 0] = b.astype(BF16)
        s_ref[0] = (a * _sigmoid(a) * b).astype(BF16)

    first = lambda j, i: (jnp.where(j == 0, i, ni - 1), 0)
    return _pallas_call(
        body, name="ffn_up", grid=(4, ni),
        in_specs=[pl.BlockSpec((tm, D), first), pl.BlockSpec((1, 1, D), lambda j, i: (l, 0, 0)),
                  pl.BlockSpec((1, 1, cs, D), lambda j, i: (0, j, 0, 0)), pl.BlockSpec((1, 1, cs, D), lambda j, i: (0, j + 4, 0, 0))],
        out_specs=[pl.BlockSpec((tm, D), first), pl.BlockSpec((2, 1, tm, cs), lambda j, i: (0, j, i, 0)),
                   pl.BlockSpec((1, tm, cs), lambda j, i: (j, i, 0))],
        out_shape=[jax.ShapeDtypeStruct((T, D), BF16), jax.ShapeDtypeStruct((2, 4, T, cs), BF16), jax.ShapeDtypeStruct((4, T, cs), BF16)],
        scratch_shapes=[pltpu.VMEM((T, D), BF16)],
        compiler_params=_cp("arbitrary", "arbitrary"),
    )(x, g, wup, wup)


def ffn_down(s, wd, x, l, deps=()):
    ng, T, cs = s.shape
    D = x.shape[1]
    tm = _tile(T, WIDE_TILE)

    def body(s_ref, w_ref, x_ref, *rest):
        o_ref = rest[len(deps)]
        acc = _dot(s_ref[0], w_ref[0, 0:cs, :])
        for j in range(1, ng):
            acc = acc + _dot(s_ref[j], w_ref[0, j * cs:(j + 1) * cs, :])
        o_ref[...] = x_ref[...] + 0.5 * acc

    return _pallas_call(
        body, name="ffn_down", grid=(T // tm,),
        in_specs=[pl.BlockSpec((ng, tm, cs), lambda i: (0, i, 0)), pl.BlockSpec((1, ng * cs, D), lambda i: (0, 0, 0)),
                  pl.BlockSpec((tm, D), lambda i: (i, 0))] + [ANY] * len(deps),
        out_specs=pl.BlockSpec((tm, D), lambda i: (i, 0)),
        out_shape=jax.ShapeDtypeStruct((T, D), F32),
        compiler_params=_cp("parallel"),
    )(s, wd, x, *deps)


def mix_in(x, g, win, l):
    T, D = x.shape
    ci = win.shape[-1]
    tm = _tile(T, WIDE_TILE)
    ni = T // tm

    def body(x_ref, g_ref, w_ref, h_ref, p_ref, hs_ref):
        rows = pl.ds(pl.multiple_of(pl.program_id(1) * tm, tm), tm)

        @pl.when(pl.program_id(0) == 0)
        def _():
            n, _r = _rms(x_ref[...])
            hv = (n * g_ref[0]).astype(BF16)
            hs_ref[rows, :] = hv
            h_ref[...] = hv

        p_ref[...] = _dot(hs_ref[rows, :], w_ref[0, 0]).astype(BF16)

    first = lambda j, i: (jnp.where(j == 0, i, ni - 1), 0)
    return _pallas_call(
        body, name="mix_in", grid=(N_DEV, ni),
        in_specs=[pl.BlockSpec((tm, D), first), pl.BlockSpec((1, 1, D), lambda j, i: (l, 0, 0)),
                  pl.BlockSpec((1, 1, D, ci), lambda j, i: (0, j, 0, 0))],
        out_specs=[pl.BlockSpec((tm, D), first), pl.BlockSpec((tm, ci), lambda j, i: (i, j))],
        out_shape=[jax.ShapeDtypeStruct((T, D), BF16), jax.ShapeDtypeStruct((T, N_DEV * ci), BF16)],
        scratch_shapes=[pltpu.VMEM((T, D), BF16)],
        compiler_params=_cp("arbitrary", "arbitrary"),
    )(x, g, win)


def _inv_count(T, w):
    t = lax.broadcasted_iota(jnp.int32, (T, 1), 0)
    return 1.0 / jnp.minimum(t + 1, w).astype(F32)


def _pooled(ug, w, inv):
    s = ug
    k = 1
    while k < w:
        s = s + _shift_down(s, k)
        k *= 2
    return s * inv - ug


def pool_fwd(proj, pw, pb, ps, l):
    T = proj.shape[0]
    _, G, gd, _ = pw.shape
    P = G * gd

    def body(u_ref, w_ref, b_ref, s_ref, o_ref):
        for gi in range(G):
            cols = slice(gi * gd, (gi + 1) * gd)
            ug = u_ref[:, cols].astype(F32)
            pooled = _pooled(ug, POOL_WINDOWS[gi], _inv_count(T, POOL_WINDOWS[gi]))
            mixed = _dot(pooled.astype(BF16), w_ref[0, gi].astype(BF16)) + b_ref[0, :, cols]
            o_ref[:, cols] = (mixed * s_ref[0, :, cols]).astype(BF16)

    return _pallas_call(
        body, name="pool_fwd", grid=(1,),
        in_specs=[pl.BlockSpec((T, P), lambda i: (0, 0)), pl.BlockSpec((1, G, gd, gd), lambda i: (l, 0, 0, 0)),
                  pl.BlockSpec((1, 1, P), lambda i: (l, 0, 0)), pl.BlockSpec((1, 1, P), lambda i: (l, 0, 0))],
        out_specs=pl.BlockSpec((T, P), lambda i: (0, 0)),
        out_shape=jax.ShapeDtypeStruct((T, P), BF16),
        compiler_params=_cp("arbitrary"),
    )(proj, pw, pb, ps)


def _conv(u, cw_ref, cb):
    CW = cw_ref.shape[1]
    v = cb
    for k in range(CW):
        v = v + cw_ref[0, k:k + 1, :] * _shift_down(u, CW - 1 - k)
    return v


def _softplus(z):
    return jnp.maximum(z, 0.0) + jnp.log1p(jnp.exp(-jnp.abs(z)))


def _lru_gates(v, wa_ref, ba, wx_ref, bx, lam):
    vb = v.astype(BF16)
    r = _sigmoid(_dot(vb, wa_ref[0, 0].astype(BF16)) + ba)
    i = _sigmoid(_dot(vb, wx_ref[0, 0].astype(BF16)) + bx)
    sp = _softplus(-lam)
    log_a = -LRU_C * r * sp
    a = jnp.exp(log_a)
    m2 = -jnp.tanh(log_a) * (a * a + 1.0)
    inv_mult = lax.rsqrt(m2)
    mult = jnp.where(m2 > 0.0, m2 * inv_mult, 0.0)
    return r, i, sp, a, mult, inv_mult


def _scan_fwd(a_ref, b_ref, o_ref):
    T, W = a_ref.shape
    rows = lax.broadcasted_iota(jnp.int32, (8, W), 0)

    def step(t, carry):
        r0 = pl.multiple_of(t * 8, 8)
        A = a_ref[pl.ds(r0, 8), :]
        B = b_ref[pl.ds(r0, 8), :]
        for s in (1, 2, 4):
            keep = rows >= s
            As = jnp.where(keep, pltpu.roll(A, s, 0), 1.0)
            Bs = jnp.where(keep, pltpu.roll(B, s, 0), 0.0)
            B = A * Bs + B
            A = A * As
        h = B + A * carry
        o_ref[pl.ds(r0, 8), :] = h
        return jnp.broadcast_to(h[7:8, :], (8, W))

    lax.fori_loop(0, T // 8, step, jnp.zeros((8, W), F32), unroll=8)


def _scan_bwd(a_ref, b_ref, o_ref):
    T, W = a_ref.shape
    rows = lax.broadcasted_iota(jnp.int32, (8, W), 0)
    nt = T // 8

    def step(t, carry):
        r0 = pl.multiple_of((nt - 1 - t) * 8, 8)
        A = a_ref[pl.ds(r0, 8), :]
        B = b_ref[pl.ds(r0, 8), :]
        for s in (1, 2, 4):
            keep = rows < 8 - s
            As = jnp.where(keep, pltpu.roll(A, 8 - s, 0), 1.0)
            Bs = jnp.where(keep, pltpu.roll(B, 8 - s, 0), 0.0)
            B = A * Bs + B
            A = A * As
        y = B + A * carry
        o_ref[pl.ds(r0, 8), :] = y
        return jnp.broadcast_to(y[0:1, :], (8, W))

    lax.fori_loop(0, nt, step, jnp.zeros((8, W), F32), unroll=8)


def _lru_specs(T, hd, P, R, CW, l):
    ob, gb = P // hd, (P + R) // hd
    vec = pl.BlockSpec((1, 1, hd), lambda h: (l, 0, h))
    mat = pl.BlockSpec((1, 1, hd, hd), lambda h: (l, h, 0, 0))
    return [pl.BlockSpec((T, hd), lambda h: (0, ob + h)), pl.BlockSpec((T, hd), lambda h: (0, gb + h)),
            pl.BlockSpec((1, CW, hd), lambda h: (0, 0, h)), vec, mat, vec, mat, vec, vec]


def lru_fwd(proj, cw, cb, wa, ba, wx, bx, lam, P, l):
    T = proj.shape[0]
    _, H, hd, _ = wa.shape
    R = H * hd
    CW = cw.shape[1]
    assert P % hd == 0 and T % 8 == 0

    def body(u_ref, ug_ref, cw_ref, cb_ref, wa_ref, ba_ref, wx_ref, bx_ref, lam_ref, hl_ref, hs_ref, a_s, b_s):
        v = _conv(u_ref[...].astype(F32), cw_ref, cb_ref[0])
        _r, i, _sp, a, mult, _im = _lru_gates(v, wa_ref, ba_ref[0], wx_ref, bx_ref[0], lam_ref[0])
        a_s[...] = a
        b_s[...] = mult * (i * v)
        _scan_fwd(a_s, b_s, hs_ref)
        ge, _th = _gelu(ug_ref[...].astype(F32))
        hl_ref[...] = (hs_ref[...] * ge).astype(BF16)

    out = pl.BlockSpec((T, hd), lambda h: (0, h))
    return _pallas_call(
        body, name="lru_fwd", grid=(H,),
        in_specs=_lru_specs(T, hd, P, R, CW, l),
        out_specs=[out, out],
        out_shape=[jax.ShapeDtypeStruct((T, R), BF16), jax.ShapeDtypeStruct((T, R), F32)],
        scratch_shapes=[pltpu.VMEM((T, hd), F32)] * 2,
        compiler_params=_cp("parallel"),
    )(proj, proj, cw, cb, wa, ba, wx, bx, lam)


def mix_out(pm, hl, proj, x, wpu, wlu, wout, P, l, deps=()):
    T, D = x.shape
    R = hl.shape[1]
    tm = _tile(T, ROW_TILE)
    assert (P + 2 * R) % D == 0
    gb = (P + 2 * R) // D

    def body(pm_ref, hl_ref, gp_ref, gl_ref, x_ref, wpu_ref, wlu_ref, wo_ref, *rest):
        o_ref, yp_ref, yl_ref, z_ref = rest[len(deps):]
        yp = _dot(pm_ref[...], wpu_ref[0])
        yl = _dot(hl_ref[...], wlu_ref[0])
        z = (_sigmoid(gp_ref[...].astype(F32)) * yp + _sigmoid(gl_ref[...].astype(F32)) * yl).astype(BF16)
        yp_ref[...] = yp.astype(BF16)
        yl_ref[...] = yl.astype(BF16)
        z_ref[...] = z
        o_ref[...] = x_ref[...] + _dot(z, wo_ref[0])

    row = lambda w: pl.BlockSpec((tm, w), lambda i: (i, 0))
    return _pallas_call(
        body, name="mix_out", grid=(T // tm,),
        in_specs=[row(P), row(R), pl.BlockSpec((tm, D), lambda i: (i, gb)), pl.BlockSpec((tm, D), lambda i: (i, gb + 1)), row(D),
                  pl.BlockSpec((1, P, D), lambda i: (0, 0, 0)), pl.BlockSpec((1, R, D), lambda i: (0, 0, 0)),
                  pl.BlockSpec((1, D, D), lambda i: (0, 0, 0))] + [ANY] * len(deps),
        out_specs=[row(D)] * 4,
        out_shape=[jax.ShapeDtypeStruct((T, D), F32)] + [jax.ShapeDtypeStruct((T, D), BF16)] * 3,
        compiler_params=_cp("parallel"),
    )(pm, hl, proj, proj, x, wpu, wlu, wout, *deps)


def loss_head(x, gf, tgt):
    T, D = x.shape
    tm = _tile(T, ROW_TILE)

    def body(x_ref, g_ref, t_ref, loss_ref, dx_ref, dg_ref):
        @pl.when(pl.program_id(0) == 0)
        def _():
            loss_ref[...] = jnp.zeros_like(loss_ref)
            dg_ref[...] = jnp.zeros_like(dg_ref)

        xv = x_ref[...]
        gv = g_ref[...]
        n, _r = _rms(xv)
        e = n * gv - t_ref[...]
        loss_ref[...] += 0.5 * jnp.sum(jnp.sum(e * e, axis=-1, keepdims=True), axis=0, keepdims=True) / D
        dx, dg = _rms_bwd(e * (1.0 / D), xv, gv, 0.0)
        dx_ref[...] = dx
        dg_ref[...] += dg

    return _pallas_call(
        body, name="loss_head", grid=(T // tm,),
        in_specs=[pl.BlockSpec((tm, D), lambda i: (i, 0)), pl.BlockSpec((1, D), lambda i: (0, 0)), pl.BlockSpec((tm, D), lambda i: (i, 0))],
        out_specs=[pl.BlockSpec((1, 1), lambda i: (0, 0)), pl.BlockSpec((tm, D), lambda i: (i, 0)), pl.BlockSpec((1, D), lambda i: (0, 0))],
        out_shape=[jax.ShapeDtypeStruct((1, 1), F32), jax.ShapeDtypeStruct((T, D), F32), jax.ShapeDtypeStruct((1, D), F32)],
        compiler_params=_cp("arbitrary"),
    )(x, gf, tgt)


def ffn_down_bwd(dy, wd, u, l, deps=()):
    T, D = dy.shape
    cs = u.shape[-1]
    tm = _tile(T, WIDE_TILE)
    ni = T // tm

    def body(dy_ref, w_ref, u_ref, *rest):
        do_ref, du_ref, dyb_ref = rest[len(deps):]
        rows = pl.ds(pl.multiple_of(pl.program_id(1) * tm, tm), tm)

        @pl.when(pl.program_id(0) == 0)
        def _():
            d = (0.5 * dy_ref[...]).astype(BF16)
            dyb_ref[rows, :] = d
            do_ref[...] = d

        ds = _dot_nt(dyb_ref[rows, :], w_ref[0])
        a = u_ref[0, 0].astype(F32)
        b = u_ref[1, 0].astype(F32)
        sg = _sigmoid(a)
        du_ref[0, 0] = (ds * b * (sg * (1.0 + a * (1.0 - sg)))).astype(BF16)
        du_ref[1, 0] = (ds * (a * sg)).astype(BF16)

    first = lambda j, i: (jnp.where(j == 0, i, ni - 1), 0)
    blk = pl.BlockSpec((2, 1, tm, cs), lambda j, i: (0, j, i, 0))
    return _pallas_call(
        body, name="ffn_down_bwd", grid=(4, ni),
        in_specs=[pl.BlockSpec((tm, D), first), pl.BlockSpec((1, cs, D), lambda j, i: (0, j, 0)), blk] + [ANY] * len(deps),
        out_specs=[pl.BlockSpec((tm, D), first), blk],
        out_shape=[jax.ShapeDtypeStruct((T, D), BF16), jax.ShapeDtypeStruct((2, 4, T, cs), BF16)],
        scratch_shapes=[pltpu.VMEM((T, D), BF16)],
        compiler_params=_cp("arbitrary", "arbitrary"),
    )(dy, wd, u, *deps)


def dw_tn(name, a, a_spec, b, b_spec, G, M, N, T):
    tk = _tile(T, SUM_TILE)
    nk = T // tk

    def body(a_ref, b_ref, o32_ref, o16_ref, acc_ref):
        k = pl.program_id(1)

        @pl.when(k == 0)
        def _():
            acc_ref[...] = jnp.zeros_like(acc_ref)

        av = a_ref[0] if len(a_ref.shape) == 3 else a_ref[...]
        bv = b_ref[0] if len(b_ref.shape) == 3 else b_ref[...]
        acc_ref[...] += _dot_tn(av, bv)

        @pl.when(k == nk - 1)
        def _():
            o32_ref[0, 0] = acc_ref[...]
            o16_ref[0, 0] = acc_ref[...].astype(BF16)

    out = pl.BlockSpec((1, 1, M, N), lambda g, k: (0, g, 0, 0))
    return _pallas_call(
        body, name=name, grid=(G, nk),
        in_specs=[a_spec(tk), b_spec(tk)], out_specs=[out, out],
        out_shape=[jax.ShapeDtypeStruct((1, G, M, N), F32), jax.ShapeDtypeStruct((1, G, M, N), BF16)],
        scratch_shapes=[pltpu.VMEM((M, N), F32)],
        compiler_params=_cp("parallel", "arbitrary"),
    )(a, b)


def dx_norm_bwd(name, dact, d_spec, w, G, x, g, dy, l, w_transposed=False):
    T, D = x.shape
    wblk = w.shape[-2:]
    tm = _tile(T, WIDE_TILE)
    ni = T // tm
    ch = _tile(tm, ROW_TILE // 2)

    def body(d_ref, w_ref, x_ref, g_ref, dy_ref, dx_ref, dg_ref, acc_ref):
        j, i = pl.program_id(0), pl.program_id(1)
        rows = pl.ds(pl.multiple_of(i * tm, tm), tm)

        @pl.when(jnp.logical_and(i == 0, j == 0))
        def _():
            dg_ref[...] = jnp.zeros_like(dg_ref)

        @pl.when(j == 0)
        def _():
            acc_ref[rows, :] = jnp.zeros((tm, D), F32)

        dv = d_ref[0] if len(d_ref.shape) == 3 else d_ref[...]
        acc_ref[rows, :] += _dot(dv, w_ref[0, 0]) if w_transposed else _dot_nt(dv, w_ref[0, 0])

        @pl.when(j == G - 1)
        def _():
            dg = jnp.zeros((1, D), F32)
            for c0 in range(0, tm, ch):
                part_rows = pl.ds(pl.multiple_of(i * tm + c0, ch), ch)
                dx, dgc = _rms_bwd(acc_ref[part_rows, :], x_ref[c0:c0 + ch, :], g_ref[0], dy_ref[c0:c0 + ch, :])
                dx_ref[c0:c0 + ch, :] = dx
                dg = dg + dgc
            dg_ref[...] += dg

    last = pl.BlockSpec((tm, D), lambda j, i: (jnp.where(j == G - 1, i, 0), 0))
    return _pallas_call(
        body, name=name, grid=(G, ni),
        in_specs=[d_spec(tm), pl.BlockSpec((1, 1) + wblk, lambda j, i: (0, j, 0, 0)), last, pl.BlockSpec((1, 1, D), lambda j, i: (l, 0, 0)), last],
        out_specs=[last, pl.BlockSpec((1, D), lambda j, i: (0, 0))],
        out_shape=[jax.ShapeDtypeStruct((T, D), F32), jax.ShapeDtypeStruct((1, D), F32)],
        scratch_shapes=[pltpu.VMEM((T, D), F32)],
        compiler_params=_cp("arbitrary", "arbitrary"),
    )(dact, w, x, g, dy)


def mix_out_bwd(dy, proj, yp, yl, wpu, wlu, wout, P, R, l, deps=()):
    T, D = dy.shape
    tm = _tile(T, ROW_TILE)
    gb = (P + 2 * R) // D

    def body(dy_ref, gp_ref, gl_ref, yp_ref, yl_ref, wpu_ref, wlu_ref, wo_ref, *rest):
        dyb_ref, dyp_ref, dyl_ref, dgp_ref, dgl_ref, dpm_ref, dhl_ref = rest[len(deps):]
        dyb = dy_ref[...].astype(BF16)
        dyb_ref[...] = dyb
        dz = _dot_nt(dyb, wo_ref[0])
        sp = _sigmoid(gp_ref[...].astype(F32))
        sl = _sigmoid(gl_ref[...].astype(F32))
        dgp_ref[...] = (dz * yp_ref[...].astype(F32) * sp * (1.0 - sp)).astype(BF16)
        dgl_ref[...] = (dz * yl_ref[...].astype(F32) * sl * (1.0 - sl)).astype(BF16)
        dyp = (dz * sp).astype(BF16)
        dyl = (dz * sl).astype(BF16)
        dyp_ref[...] = dyp
        dyl_ref[...] = dyl
        dpm_ref[...] = _dot_nt(dyp, wpu_ref[0]).astype(BF16)
        dhl_ref[...] = _dot_nt(dyl, wlu_ref[0]).astype(BF16)

    row = lambda w: pl.BlockSpec((tm, w), lambda i: (i, 0))
    return _pallas_call(
        body, name="mix_out_bwd", grid=(T // tm,),
        in_specs=[row(D), pl.BlockSpec((tm, D), lambda i: (i, gb)), pl.BlockSpec((tm, D), lambda i: (i, gb + 1)), row(D), row(D),
                  pl.BlockSpec((1, P, D), lambda i: (0, 0, 0)), pl.BlockSpec((1, R, D), lambda i: (0, 0, 0)),
                  pl.BlockSpec((1, D, D), lambda i: (0, 0, 0))] + [ANY] * len(deps),
        out_specs=[row(D)] * 5 + [row(P), row(R)],
        out_shape=[jax.ShapeDtypeStruct((T, D), BF16)] * 5 + [jax.ShapeDtypeStruct((T, P), BF16), jax.ShapeDtypeStruct((T, R), BF16)],
        compiler_params=_cp("parallel"),
    )(dy, proj, proj, yp, yl, wpu, wlu, wout, *deps)


def lru_bwd(proj, hs, dhl, cw, cb, wa, ba, wx, bx, lam, P, l):
    T = proj.shape[0]
    _, H, hd, _ = wa.shape
    R = H * hd
    CW = cw.shape[1]

    def body(u_ref, ug_ref, cw_ref, cb_ref, wa_ref, ba_ref, wx_ref, bx_ref, lam_ref, hs_ref, dhl_ref,
             du_ref, dug_ref, dcw_ref, dcb_ref, dwa_ref, dba_ref, dwx_ref, dbx_ref, dlam_ref, c_s, g_s, y_s):
        u = u_ref[...].astype(F32)
        v = _conv(u, cw_ref, cb_ref[0])
        lam = lam_ref[0]
        r, i, sp, a, mult, inv_mult = _lru_gates(v, wa_ref, ba_ref[0], wx_ref, bx_ref[0], lam)
        ug = ug_ref[...].astype(F32)
        ge, th = _gelu(ug)
        hs = hs_ref[...]
        dhl = dhl_ref[...].astype(F32)
        dug_ref[...] = (dhl * hs * _gelu_grad(ug, th)).astype(BF16)
        c_s[...] = _shift_up(a, 1)
        g_s[...] = dhl * ge
        _scan_bwd(c_s, g_s, y_s)
        y = y_s[...]
        da = y * _shift_down(hs, 1)
        iv = i * v
        dlog_a = da * a - (y * iv) * (a * a) * inv_mult
        div = y * mult
        dpa = (dlog_a * (-LRU_C) * sp) * r * (1.0 - r)
        dpx = (div * v) * i * (1.0 - i)
        dsp = jnp.sum(dlog_a * (-LRU_C) * r, axis=0, keepdims=True)
        dlam_ref[0] = -dsp * _sigmoid(-lam)
        vb = v.astype(BF16)
        dpab, dpxb = dpa.astype(BF16), dpx.astype(BF16)
        dwa_ref[0, 0] = _dot_tn(vb, dpab)
        dwx_ref[0, 0] = _dot_tn(vb, dpxb)
        dba_ref[0] = jnp.sum(dpa, axis=0, keepdims=True)
        dbx_ref[0] = jnp.sum(dpx, axis=0, keepdims=True)
        dv = div * i + _dot_nt(dpab, wa_ref[0, 0].astype(BF16)) + _dot_nt(dpxb, wx_ref[0, 0].astype(BF16))
        dcb_ref[0] = jnp.sum(dv, axis=0, keepdims=True)
        du = jnp.zeros_like(dv)
        for k in range(CW):
            du = du + cw_ref[0, k:k + 1, :] * _shift_up(dv, CW - 1 - k)
            dcw_ref[0, k:k + 1, :] = jnp.sum(dv * _shift_down(u, CW - 1 - k), axis=0, keepdims=True)
        du_ref[...] = du.astype(BF16)

    col = pl.BlockSpec((T, hd), lambda h: (0, h))
    vec = pl.BlockSpec((1, 1, hd), lambda h: (0, 0, h))
    mat = pl.BlockSpec((1, 1, hd, hd), lambda h: (0, h, 0, 0))
    vshape = jax.ShapeDtypeStruct((1, 1, R), F32)
    mshape = jax.ShapeDtypeStruct((1, H, hd, hd), F32)
    return _pallas_call(
        body, name="lru_bwd", grid=(H,),
        in_specs=_lru_specs(T, hd, P, R, CW, l) + [col, col],
        out_specs=[col, col, pl.BlockSpec((1, CW, hd), lambda h: (0, 0, h)), vec, mat, vec, mat, vec, vec],
        out_shape=[jax.ShapeDtypeStruct((T, R), BF16)] * 2 + [jax.ShapeDtypeStruct((1, CW, R), F32), vshape, mshape, vshape, mshape, vshape, vshape],
        scratch_shapes=[pltpu.VMEM((T, hd), F32)] * 3,
        compiler_params=_cp("parallel"),
    )(proj, proj, cw, cb, wa, ba, wx, bx, lam, hs, dhl)


def pool_bwd(proj, dpm, pw, pb, ps, l):
    T = proj.shape[0]
    _, G, gd, _ = pw.shape
    P = G * gd

    def body(u_ref, d_ref, w_ref, b_ref, s_ref, du_ref, dw_ref, db_ref, dsc_ref):
        for gi in range(G):
            cols = slice(gi * gd, (gi + 1) * gd)
            w = POOL_WINDOWS[gi]
            inv = _inv_count(T, w)
            ug = u_ref[:, cols].astype(F32)
            pooled = _pooled(ug, w, inv).astype(BF16)
            wb = w_ref[0, gi].astype(BF16)
            mixed = _dot(pooled, wb) + b_ref[0, :, cols]
            dpm_g = d_ref[:, cols].astype(F32)
            dsc_ref[0, :, cols] = jnp.sum(dpm_g * mixed, axis=0, keepdims=True)
            dmixed = dpm_g * s_ref[0, :, cols]
            db_ref[0, :, cols] = jnp.sum(dmixed, axis=0, keepdims=True)
            dmb = dmixed.astype(BF16)
            dw_ref[0, gi] = _dot_tn(pooled, dmb)
            dpooled = _dot_nt(dmb, wb)
            s = dpooled * inv
            k = 1
            while k < w:
                s = s + _shift_up(s, k)
                k *= 2
            du_ref[:, cols] = (s - dpooled).astype(BF16)

    vec = pl.BlockSpec((1, 1, P), lambda i: (l, 0, 0))
    ovec = pl.BlockSpec((1, 1, P), lambda i: (0, 0, 0))
    return _pallas_call(
        body, name="pool_bwd", grid=(1,),
        in_specs=[pl.BlockSpec((T, P), lambda i: (0, 0)), pl.BlockSpec((T, P), lambda i: (0, 0)),
                  pl.BlockSpec((1, G, gd, gd), lambda i: (l, 0, 0, 0)), vec, vec],
        out_specs=[pl.BlockSpec((T, P), lambda i: (0, 0)), pl.BlockSpec((1, G, gd, gd), lambda i: (0, 0, 0, 0)), ovec, ovec],
        out_shape=[jax.ShapeDtypeStruct((T, P), BF16), jax.ShapeDtypeStruct((1, G, gd, gd), F32),
                   jax.ShapeDtypeStruct((1, 1, P), F32), jax.ShapeDtypeStruct((1, 1, P), F32)],
        compiler_params=_cp("arbitrary"),
    )(proj, dpm, pw, pb, ps)


def _place():
    x, y, c = lax.axis_index("x"), lax.axis_index("y"), lax.axis_index("c")
    return x, y, c


HBM = pl.BlockSpec(memory_space=pltpu.HBM)
SEM = pl.BlockSpec(memory_space=pltpu.SEMAPHORE)
EFFECT = pltpu.SideEffectType.DATAFLOW_SIDE_EFFECTING


def _in_hbm(a):
    return pltpu.with_memory_space_constraint(a, pltpu.HBM)


def split_start(name, bufs, n_copies, copies_of, deps=()):
    nb = len(bufs)

    def body(*refs):
        buf = refs[:nb]
        send_sems, recv_sems = refs[nb + len(deps)], refs[nb + len(deps) + 1]
        token = refs[-1]
        for i, (src, dst, dev) in enumerate(copies_of(buf)):
            pltpu.make_async_remote_copy(src_ref=src, dst_ref=dst, send_sem=send_sems.at[i], recv_sem=recv_sems.at[i],
                                         device_id=dev, device_id_type=MESH).start()
        token[...] = jnp.zeros_like(token)

    outs = _pallas_call(
        body, name=name,
        in_specs=[HBM] * nb + [ANY] * len(deps),
        out_specs=(SEM, SEM, *([HBM] * nb), pl.BlockSpec(memory_space=pltpu.VMEM)),
        out_shape=(pltpu.SemaphoreType.DMA((n_copies,)), pltpu.SemaphoreType.DMA((n_copies,)),
                   *[pltpu.HBM(b.shape, b.dtype) for b in bufs], jax.ShapeDtypeStruct((8, 128), F32)),
        input_output_aliases={i: 2 + i for i in range(nb)},
        compiler_params=pltpu.CompilerParams(has_side_effects=EFFECT),
    )(*[_in_hbm(b) for b in bufs], *deps)
    return outs[0], outs[1], list(outs[2:2 + nb]), outs[-1]


def split_wait(name, bufs, send_sems, recv_sems, after, copies_of):
    nb = len(bufs)

    def body(*refs):
        buf = refs[:nb]
        send, recv = refs[nb], refs[nb + 1]
        for i, (src, dst, dev) in enumerate(copies_of(buf)):
            cp = pltpu.make_async_remote_copy(src_ref=src, dst_ref=dst, send_sem=send.at[i], recv_sem=recv.at[i],
                                              device_id=dev, device_id_type=MESH)
            cp.wait_send()
            cp.wait_recv()

    outs = _pallas_call(
        body, name=name,
        in_specs=[HBM] * nb + [SEM, SEM] + [ANY] * len(after),
        out_specs=[HBM] * nb,
        out_shape=[pltpu.HBM(b.shape, b.dtype) for b in bufs],
        input_output_aliases={i: i for i in range(nb)},
        compiler_params=pltpu.CompilerParams(has_side_effects=EFFECT),
    )(*bufs, send_sems, recv_sems, *after)
    return list(outs)


def _two_row_blocks(rows):
    return (rows // 2, 1) if rows % 32 == 0 else (rows, 0)


def place_own(ws, dtypes, l, place):
    n = len(ws)

    def body(p_ref, *refs):
        for a in range(n):
            refs[n + a][0] = refs[a][0].astype(dtypes[a])

    in_specs, out_specs, out_shape = [], [], []
    for w, dt in zip(ws, dtypes):
        _, rows, cols = w.shape
        rb, step = _two_row_blocks(rows)
        in_specs.append(pl.BlockSpec((1, rb, cols), lambda i, p, s=step: (l, i * s, 0)))
        out_specs.append(pl.BlockSpec((1, rb, cols), lambda i, p, s=step: (p[2], i * s, 0)))
        out_shape.append(jax.ShapeDtypeStruct((N_DEV, rows, cols), dt))
    return list(_pallas_call(
        body, name="place_own",
        grid_spec=pltpu.PrefetchScalarGridSpec(num_scalar_prefetch=1, grid=(2,), in_specs=in_specs, out_specs=out_specs),
        out_shape=out_shape, compiler_params=_cp("arbitrary"),
    )(place, *ws))


def _gather_copies(land):
    x, y, c = _place()
    k = 4 * x + 2 * y + c
    peers = [(x, 1 - y, c), (1 - x, y, c), (1 - x, 1 - y, c), (x, y, 1 - c)]
    return [(b.at[k], b.at[k], p) for p in peers for b in land]


def gather_start(name, land, deps=()):
    return split_start(name, land, 4 * len(land), _gather_copies, deps)


def gather_wait(name, land, send_sems, recv_sems, after):
    return split_wait(name, land, send_sems, recv_sems, after, _gather_copies)


def _forward_copies(land):
    x, y, c = _place()
    slots = [4 * px + 2 * py + c for px, py in [(x, 1 - y), (1 - x, y), (1 - x, 1 - y)]]
    return [(b.at[k], b.at[k], (x, y, 1 - c)) for k in slots for b in land]


def gather_forward_start(name, land, deps=()):
    return split_start(name, land, 3 * len(land), _forward_copies, deps)


def gather_forward_wait(name, land, send_sems, recv_sems, after):
    return split_wait(name, land, send_sems, recv_sems, after, _forward_copies)


def _chip_copies(nsrc):
    def copies(buf):
        p16, recv2 = buf[:nsrc], buf[nsrc:]
        x, y, c = _place()
        out = []
        for d in (1, 2, 3):
            px = 1 - x if d & 2 else x
            py = 1 - y if d & 1 else y
            out += [(p16[a].at[:, d - 1], recv2[a].at[:, d - 1], (px, py, c)) for a in range(nsrc)]
        return out
    return copies


def _pair_copies(nsrc):
    def copies(buf):
        g16, recv = buf[:nsrc], buf[nsrc:]
        x, y, c = _place()
        return [(g16[a].at[:, 2 * j + 1 - c], recv[a].at[:, j], (x, y, 1 - c)) for a in range(nsrc) for j in range(N_CHIP)]
    return copies


def pair_exchange_start(name, g16, deps=()):
    n = len(g16)
    land = [lax.empty((1, N_CHIP) + s.shape[2:], s.dtype) for s in g16]
    return split_start(name, list(g16) + land, N_CHIP * n, _pair_copies(n), deps)


def pair_exchange_wait(name, bufs, send_sems, recv_sems, after):
    n = len(bufs) // 2
    return split_wait(name, bufs, send_sems, recv_sems, after, _pair_copies(n))[n:]


def chip_exchange_start(name, pair16, deps=()):
    n = len(pair16)
    land = [lax.empty((s.shape[0], 3) + s.shape[2:], s.dtype) for s in pair16]
    return split_start(name, list(pair16) + land, 3 * n, _chip_copies(n), deps)


def chip_exchange_wait(name, bufs, send_sems, recv_sems, after):
    n = len(bufs) // 2
    return split_wait(name, bufs, send_sems, recv_sems, after, _chip_copies(n))[n:]


def _rows_tile(rows, cols, budget=1 << 20):
    t = rows
    while t % 2 == 0 and t * cols > budget and (t // 2) % 16 == 0:
        t //= 2
    return t


def pair_sum(g32s, recv1s, place):
    n = len(g32s)

    def body(p_ref, *refs):
        for a in range(n):
            m_ref, r_ref, o_ref = refs[a], refs[n + a], refs[2 * n + a]
            o_ref[...] = (m_ref[...] + r_ref[...].astype(F32)).astype(o_ref.dtype)

    other = lambda d, p: jnp.bitwise_xor(p[1], d + 1)
    g_specs, r_specs, o_specs, out_shape = [], [], [], []
    for r1 in recv1s:
        _, _, rows, cols = r1.shape
        rb, step = _two_row_blocks(rows)
        g_specs.append(pl.BlockSpec((1, 1, rb, cols), lambda d, i, p, s=step: (0, 2 * other(d, p) + p[0], i * s, 0)))
        r_specs.append(pl.BlockSpec((1, 1, rb, cols), lambda d, i, p, s=step: (0, other(d, p), i * s, 0)))
        o_specs.append(pl.BlockSpec((1, 1, rb, cols), lambda d, i, p, s=step: (0, d, i * s, 0)))
        out_shape.append(jax.ShapeDtypeStruct((1, N_CHIP - 1, rows, cols), r1.dtype))
    return list(_pallas_call(
        body, name="pair_sum",
        grid_spec=pltpu.PrefetchScalarGridSpec(num_scalar_prefetch=1, grid=(N_CHIP - 1, 2), in_specs=g_specs + r_specs, out_specs=o_specs),
        out_shape=out_shape, compiler_params=_cp("arbitrary", "arbitrary"),
    )(place, *g32s, *recv1s))


def _grad_in_specs(tr, cols, l):
    return ([pl.BlockSpec((1, 1, tr, cols), lambda i, p: (l, p[2], i, 0)), pl.BlockSpec((1, 1, tr, cols), lambda i, p: (0, p[1], i, 0))]
            + [pl.BlockSpec((1, 1, tr, cols), lambda i, p, d=d: (0, d, i, 0)) for d in range(3)])


def _grad_total(o32, o16, r0, r1, r2):
    return (o32[0, 0] + o16[0, 0].astype(F32)) + r0[0, 0].astype(F32) + r1[0, 0].astype(F32) + r2[0, 0].astype(F32)


def grad_sum(g32, recv1, recv2, place):
    _, _, rows, cols = recv1.shape
    tr = _rows_tile(rows, cols)

    def body(p_ref, o32, o16, r0, r1, r2, g_ref):
        g_ref[...] = _grad_total(o32, o16, r0, r1, r2)

    return _pallas_call(
        body, name="grad_sum",
        grid_spec=pltpu.PrefetchScalarGridSpec(
            num_scalar_prefetch=1, grid=(rows // tr,), in_specs=_grad_in_specs(tr, cols, 0),
            out_specs=pl.BlockSpec((tr, cols), lambda i, p: (i, 0))),
        out_shape=jax.ShapeDtypeStruct((rows, cols), F32), compiler_params=_cp("parallel"),
    )(place, g32, recv1, recv2, recv2, recv2)


def _adamw_math(w, g, m, v):
    m = ADAM_B1 * m + (1.0 - ADAM_B1) * g
    v = ADAM_B2 * v + (1.0 - ADAM_B2) * (g * g)
    m_hat = m / (1.0 - ADAM_B1 ** ADAM_STEP)
    v_hat = v / (1.0 - ADAM_B2 ** ADAM_STEP)
    delta = -ADAM_LR * (m_hat / (jnp.sqrt(v_hat) + ADAM_EPS) + ADAM_WD * w)
    return delta, m, v


UPDATE_BLOCK = 1 << 16


def _update_rows(rows, cols):
    if rows % 16:
        return rows, 1
    tiles = rows // 16
    d = max([k for k in range(1, tiles + 1) if tiles % k == 0 and 16 * k * cols <= UPDATE_BLOCK] or [1])
    return 16 * d, tiles // d


def grad_sum_adamw(g32s, recv1s, recv2s, ws, ms, vs, place, l, prevs, deps=()):
    n = len(ws)
    blocks = [_update_rows(w.shape[1], w.shape[2]) for w in ws]
    have_prev = prevs[0] is not None

    def body(p_ref, *refs):
        outs = refs[len(refs) - 4 * n:]

        def update(a):
            o32, o16, r0, r1, r2, w_ref, m_ref, v_ref = refs[8 * a:8 * a + 8]
            g_ref, d_ref, nm_ref, nv_ref = outs[4 * a:4 * a + 4]

            @pl.when(pl.program_id(0) < blocks[a][1])
            def _():
                g = _grad_total(o32, o16, r0, r1, r2)
                d, nm, nv = _adamw_math(w_ref[0], g, m_ref[0], v_ref[0])
                g_ref[0] = g
                d_ref[0] = d
                nm_ref[0] = nm
                nv_ref[0] = nv

        for a in range(n):
            update(a)

    args, in_specs, out_specs, out_shape = [], [], [], []
    for a in range(n):
        L, rows, cols = ws[a].shape
        rb, nb = blocks[a]
        at = lambda i, nb=nb: jnp.minimum(i, nb - 1)
        slot = lambda which: pl.BlockSpec((1, 1, rb, cols), lambda i, p, at=at: (0, which(p), at(i), 0))
        blk = pl.BlockSpec((1, rb, cols), lambda i, p, at=at: (l, at(i), 0))
        args += [g32s[a], recv1s[a], recv2s[a], recv2s[a], recv2s[a], ws[a], ms[a], vs[a]]
        in_specs += [slot(lambda p: p[2]), slot(lambda p: p[1])] + [slot(lambda p, d=d: d) for d in range(3)] + [blk] * 3
        out_specs += [blk] * 4
        out_shape += [jax.ShapeDtypeStruct((L, rows, cols), F32)] * 4
    aliases = {}
    if have_prev:
        aliases = {1 + len(args) + k: k for k in range(4 * n)}
        args += [buf for prev in prevs for buf in prev]
        in_specs += [ANY] * (4 * n)
    args += list(deps)
    in_specs += [ANY] * len(deps)
    outs = _pallas_call(
        body, name="grad_sum_adamw",
        grid_spec=pltpu.PrefetchScalarGridSpec(num_scalar_prefetch=1, grid=(max(nb for _, nb in blocks),),
                                               in_specs=in_specs, out_specs=out_specs),
        out_shape=out_shape, input_output_aliases=aliases, compiler_params=_cp("arbitrary"),
    )(place, *args)
    return [list(outs[4 * a:4 * a + 4]) for a in range(n)]


def adamw(w, g, m, v):
    rows, cols = w.shape
    tr = _rows_tile(rows, cols, 1 << 18)

    def body(w_ref, g_ref, m_ref, v_ref, d_ref, nm_ref, nv_ref):
        d, nm, nv = _adamw_math(w_ref[...], g_ref[...], m_ref[...], v_ref[...])
        d_ref[...] = d
        nm_ref[...] = nm
        nv_ref[...] = nv

    blk = pl.BlockSpec((tr, cols), lambda i: (i, 0))
    return _pallas_call(body, name="adamw_small", grid=(rows // tr,), in_specs=[blk] * 4, out_specs=[blk] * 3,
                        out_shape=[jax.ShapeDtypeStruct((rows, cols), F32)] * 3, compiler_params=_cp("parallel"))(w, g, m, v)


SMALL = ("norm_ffn1", "norm_mix", "pool_w", "pool_b", "pool_scale", "conv_w", "conv_b", "lru_w_a", "lru_b_a", "lru_w_x", "lru_b_x",
         "lru_lambda", "norm_ffn2", "final_norm")
BIG = ("ffn1_w_up", "ffn1_w_down", "w_in", "w_pool_up", "w_lru_up", "w_out", "ffn2_w_up", "ffn2_w_down")
NAMES = ("norm_ffn1", "ffn1_w_up", "ffn1_w_down", "norm_mix", "w_in", "pool_w", "pool_b", "pool_scale", "w_pool_up", "conv_w", "conv_b",
         "lru_w_a", "lru_b_a", "lru_w_x", "lru_b_x", "lru_lambda", "w_lru_up", "w_out", "norm_ffn2", "ffn2_w_up", "ffn2_w_down", "final_norm")
SUBLAYERS = (("ffn1_w_up", "ffn1_w_down"), ("w_in", "w_pool_up", "w_lru_up", "w_out", "conv_w"), ("ffn2_w_up", "ffn2_w_down"))
PACK_ROWS = 16 * N_DEV


def _pack(parts):
    flat = jnp.concatenate([p.reshape(-1) for p in parts])
    unit = 128 * PACK_ROWS
    padded = -(-flat.size // unit) * unit
    return jnp.pad(flat, (0, padded - flat.size)).reshape(-1, 128)


def _unpack(packed, shapes):
    flat = packed.reshape(-1)
    out, off = [], 0
    for s in shapes:
        n = 1
        for d in s:
            n *= d
        out.append(flat[off:off + n].reshape(s))
        off += n
    return out


def kernel(x, norm_ffn1, ffn1_w_up, ffn1_w_down, norm_mix, w_in, pool_w, pool_b, pool_scale, w_pool_up, conv_w, conv_b, lru_w_a, lru_b_a, lru_w_x, lru_b_x, lru_lambda, w_lru_up, w_out, norm_ffn2, ffn2_w_up, ffn2_w_down, final_norm, loss_target, m_norm_ffn1, m_ffn1_w_up, m_ffn1_w_down, m_norm_mix, m_w_in, m_pool_w, m_pool_b, m_pool_scale, m_w_pool_up, m_conv_w, m_conv_b, m_lru_w_a, m_lru_b_a, m_lru_w_x, m_lru_b_x, m_lru_lambda, m_w_lru_up, m_w_out, m_norm_ffn2, m_ffn2_w_up, m_ffn2_w_down, m_final_norm, v_norm_ffn1, v_ffn1_w_up, v_ffn1_w_down, v_norm_mix, v_w_in, v_pool_w, v_pool_b, v_pool_scale, v_w_pool_up, v_conv_w, v_conv_b, v_lru_w_a, v_lru_b_a, v_lru_w_x, v_lru_b_x, v_lru_lambda, v_w_lru_up, v_w_out, v_norm_ffn2, v_ffn2_w_up, v_ffn2_w_down, v_final_norm):
    W = dict(norm_ffn1=norm_ffn1, ffn1_w_up=ffn1_w_up, ffn1_w_down=ffn1_w_down, norm_mix=norm_mix, w_in=w_in, pool_w=pool_w, pool_b=pool_b,
             pool_scale=pool_scale, w_pool_up=w_pool_up, conv_w=conv_w, conv_b=conv_b, lru_w_a=lru_w_a, lru_b_a=lru_b_a, lru_w_x=lru_w_x,
             lru_b_x=lru_b_x, lru_lambda=lru_lambda, w_lru_up=w_lru_up, w_out=w_out, norm_ffn2=norm_ffn2, ffn2_w_up=ffn2_w_up,
             ffn2_w_down=ffn2_w_down, final_norm=final_norm)
    M = dict(norm_ffn1=m_norm_ffn1, ffn1_w_up=m_ffn1_w_up, ffn1_w_down=m_ffn1_w_down, norm_mix=m_norm_mix, w_in=m_w_in, pool_w=m_pool_w,
             pool_b=m_pool_b, pool_scale=m_pool_scale, w_pool_up=m_w_pool_up, conv_w=m_conv_w, conv_b=m_conv_b, lru_w_a=m_lru_w_a,
             lru_b_a=m_lru_b_a, lru_w_x=m_lru_w_x, lru_b_x=m_lru_b_x, lru_lambda=m_lru_lambda, w_lru_up=m_w_lru_up, w_out=m_w_out,
             norm_ffn2=m_norm_ffn2, ffn2_w_up=m_ffn2_w_up, ffn2_w_down=m_ffn2_w_down, final_norm=m_final_norm)
    V = dict(norm_ffn1=v_norm_ffn1, ffn1_w_up=v_ffn1_w_up, ffn1_w_down=v_ffn1_w_down, norm_mix=v_norm_mix, w_in=v_w_in, pool_w=v_pool_w,
             pool_b=v_pool_b, pool_scale=v_pool_scale, w_pool_up=v_w_pool_up, conv_w=v_conv_w, conv_b=v_conv_b, lru_w_a=v_lru_w_a,
             lru_b_a=v_lru_b_a, lru_w_x=v_lru_w_x, lru_b_x=v_lru_b_x, lru_lambda=v_lru_lambda, w_lru_up=v_w_lru_up, w_out=v_w_out,
             norm_ffn2=v_norm_ffn2, ffn2_w_up=v_ffn2_w_up, ffn2_w_down=v_ffn2_w_down, final_norm=v_final_norm)

    for S in (W, M, V):
        for n in ("ffn1_w_up", "ffn2_w_up"):
            S[n] = jnp.swapaxes(S[n], 1, 2)

    T, D = x.shape[1], x.shape[2]
    L = norm_ffn1.shape[0]
    P = pool_scale.shape[1]
    R = lru_lambda.shape[1]
    H, hd = lru_w_a.shape[1], lru_w_a.shape[2]
    CW = conv_w.shape[1]
    cs = ffn1_w_up.shape[2]
    ci = w_in.shape[2]
    xin = x.reshape(T, D)
    tgt = loss_target.reshape(T, D)
    dev = 4 * lax.axis_index("x") + 2 * lax.axis_index("y") + lax.axis_index("c")
    place = jnp.stack([lax.axis_index("c"), 2 * lax.axis_index("x") + lax.axis_index("y"), dev]).astype(jnp.int32)

    cw_flat = conv_w.reshape(L, -1)
    cw_pad = (-cw_flat.shape[1]) % 1024
    cw_tiles = jnp.pad(cw_flat, ((0, 0), (0, cw_pad))).reshape(L, -1, 128)

    def units(l):
        return SUBLAYERS if l == 0 else (tuple(n for u in SUBLAYERS for n in u),)

    queued = {"gather": (), "pair": (), "chip": ()}

    gathering = []

    def gather_units_start(l):
        for k, names in enumerate(SUBLAYERS):
            land = place_own([cw_tiles if n == "conv_w" else W[n] for n in names], [F32 if n == "conv_w" else BF16 for n in names], l, place)
            send_sems, recv_sems, land, tok = gather_start(f"gather_start_l{l}_u{k}", land, queued["gather"])
            gathering.append(dict(names=names, tag=f"l{l}_u{k}", send=send_sems, recv=recv_sems, land=land, tok=tok, arrived=False))
            queued["gather"] = (tok,)

    def gather_unit_arrive(after):
        waiting = [u for u in gathering if not u["arrived"]]
        if not waiting:
            return ()
        unit, tokens = waiting[0], [u["tok"] for u in waiting[1:]]
        land = gather_wait(f"gather_wait_{unit['tag']}", unit["land"], unit["send"], unit["recv"], list(after) + tokens)
        send_sems, recv_sems, land, tok = gather_forward_start(f"gather_pass_start_{unit['tag']}", land)
        unit.update(land=land, send=send_sems, recv=recv_sems, tok=tok, arrived=True)
        return (tok,)

    def gather_unit_weights(after):
        if not gathering[0]["arrived"]:
            gather_unit_arrive(after)
        unit = gathering.pop(0)
        land = gather_forward_wait(f"gather_pass_wait_{unit['tag']}", unit["land"], unit["send"], unit["recv"], after)
        g = dict(zip(unit["names"], land))
        one = lambda a: a.reshape((1,) + a.shape)
        w = {}
        for tag_, up, dn in (("1", "ffn1_w_up", "ffn1_w_down"), ("2", "ffn2_w_up", "ffn2_w_down")):
            if up in g:
                w["wup" + tag_], w["wd" + tag_] = one(g[up]), g[dn].reshape(1, -1, D)
        if "w_in" in g:
            cw_l = g["conv_w"].reshape(N_DEV, -1)[:, :cw_flat.shape[1]].reshape((N_DEV,) + conv_w.shape[1:])
            w.update(win=one(g["w_in"]), wlu=g["w_lru_up"].reshape(1, R, D), wout=g["w_out"].reshape(1, D, D),
                     wpu=g["w_pool_up"].transpose(1, 0, 2).reshape(1, P, D),
                     cw=cw_l.transpose(1, 0, 2).reshape(1, CW, R))
        return w

    vec = lambda a: a.reshape(L, 1, -1)
    p = dict(g1=vec(norm_ffn1), gm=vec(norm_mix), g2=vec(norm_ffn2), pb=vec(pool_b), ps=vec(pool_scale), cb=vec(conv_b),
             ba=vec(lru_b_a), bx=vec(lru_b_x), lam=vec(lru_lambda), pw=pool_w, wa=lru_w_a, wx=lru_w_x)

    AHEAD = 2
    for l in range(min(AHEAD, L)):
        gather_units_start(l)
    saved, LW = [], []
    xc = xin
    for l in range(L):
        w = gather_unit_weights([xc])
        if l + AHEAD < L:
            gather_units_start(l + AHEAD)
        sv = {"x1": xc}
        sv["h1"], sv["u1"], sv["s1"] = ffn_up(xc, p["g1"], w["wup1"], l)
        xc = ffn_down(sv["s1"], w["wd1"], xc, l, gather_unit_arrive([sv["s1"]]))
        sv["x2"] = xc
        w.update(gather_unit_weights([xc]))
        sv["h2"], sv["proj"] = mix_in(xc, p["gm"], w["win"], l)
        sv["pm"] = pool_fwd(sv["proj"], p["pw"], p["pb"], p["ps"], l)
        sv["hl"], sv["hs"] = lru_fwd(sv["proj"], w["cw"], p["cb"], p["wa"], p["ba"], p["wx"], p["bx"], p["lam"], P, l)
        xc, sv["yp"], sv["yl"], sv["z"] = mix_out(sv["pm"], sv["hl"], sv["proj"], xc, w["wpu"], w["wlu"], w["wout"], P, l,
                                                  gather_unit_arrive([sv["hl"]]))
        sv["x3"] = xc
        w.update(gather_unit_weights([xc]))
        sv["h3"], sv["u3"], sv["s3"] = ffn_up(xc, p["g2"], w["wup2"], l)
        xc = ffn_down(sv["s3"], w["wd2"], xc, l, gather_unit_arrive([sv["s3"]]))
        saved.append(sv)
        LW.append(w)

    loss_part, dx, d_final = loss_head(xc, final_norm.reshape(1, D), tgt)
    loss = lax.psum(loss_part[0, 0], ("x", "y", "c"))

    G = [dict() for _ in range(L)]
    small = {n: [None] * L for n in SMALL if n != "final_norm"}

    def to_slots(name, pair):
        if name == "w_pool_up":
            return tuple(a.reshape(1, P, N_DEV, D // N_DEV).transpose(0, 2, 1, 3) for a in pair)
        return tuple(a.reshape((1, N_DEV) + W[name].shape[1:]) for a in pair)

    def ffn_bwd(dy, sv, tag, wup, wd, gn, up_name, dn_name, norm_name, l, deps=()):
        dout, du = ffn_down_bwd(dy, wd, sv["u" + tag], l, deps)
        du = du.reshape(N_DEV, T, cs)
        G[l][dn_name] = to_slots(dn_name, dw_tn("dw_down", sv["s" + tag], lambda tk: pl.BlockSpec((1, tk, cs), lambda g, k: (g, k, 0)),
                                                dout, lambda tk: pl.BlockSpec((tk, D), lambda g, k: (k, 0)), 4, cs, D, T))
        G[l][up_name] = to_slots(up_name, dw_tn("dw_up", du, lambda tk: pl.BlockSpec((1, tk, cs), lambda g, k: (g, k, 0)),
                                                sv["h" + tag], lambda tk: pl.BlockSpec((tk, D), lambda g, k: (k, 0)), N_DEV, cs, D, T))
        dxn, dg = dx_norm_bwd("ffn_dx", du, lambda tm: pl.BlockSpec((1, tm, cs), lambda j, i: (j, i, 0)), wup, N_DEV,
                              sv["x" + tag], gn, dy, l, w_transposed=True)
        small[norm_name][l] = dg.reshape(D)
        return dxn

    pairing, in_flight = [], []

    def reduce_start(l, names, tag):
        names = [n for n in names if n != "conv_w"]
        send_sems, recv_sems, bufs, tok = pair_exchange_start(f"rs_pair_start_{tag}", [G[l][n][1] for n in names], queued["pair"])
        pairing.append((l, names, tag, send_sems, recv_sems, bufs))
        queued["pair"] = (tok,)
        return (tok,)

    def reduce_continue(after):
        l, names, tag, send_sems, recv_sems, bufs = pairing.pop(0)
        recv1 = pair_exchange_wait(f"rs_pair_wait_{tag}", bufs, send_sems, recv_sems, after)
        pair16 = pair_sum([G[l][n][0] for n in names], recv1, place)
        send_sems, recv_sems, bufs, tok = chip_exchange_start(f"rs_chip_start_{tag}", pair16, queued["chip"])
        in_flight.append((l, names, tag, send_sems, recv_sems, bufs, recv1))
        queued["chip"] = (tok,)
        return (tok,)

    def boundary(l, k, dx_now):
        deps = reduce_continue([dx_now]) if pairing else ()
        if len(units(l)) > 1:
            deps += reduce_start(l, units(l)[k], f"l{l}_u{k}")
        elif k == 0:
            deps += reduce_start(l, units(l)[0], f"l{l}_u0")
        return deps

    deps = ()
    for l in reversed(range(L)):
        sv, w = saved[l], LW[l]
        dx = ffn_bwd(dx, sv, "3", w["wup2"], w["wd2"], p["g2"], "ffn2_w_up", "ffn2_w_down", "norm_ffn2", l, deps)
        deps = boundary(l, 2, dx)
        dyb, dyp, dyl, dgp, dgl, dpm, dhl = mix_out_bwd(dx, sv["proj"], sv["yp"], sv["yl"], w["wpu"], w["wlu"], w["wout"], P, R, l, deps)
        row = lambda wd_: (lambda tk: pl.BlockSpec((tk, wd_), lambda g, k: (k, 0)))
        G[l]["w_out"] = to_slots("w_out", dw_tn("dw_out", sv["z"], row(D), dyb, row(D), 1, D, D, T))
        G[l]["w_lru_up"] = to_slots("w_lru_up", dw_tn("dw_lru_up", sv["hl"], row(R), dyl, row(D), 1, R, D, T))
        G[l]["w_pool_up"] = to_slots("w_pool_up", dw_tn("dw_pool_up", sv["pm"], row(P), dyp, row(D), 1, P, D, T))
        du_lru, du_gelu, dcw, dcb, dwa, dba, dwx, dbx, dlam = lru_bwd(
            sv["proj"], sv["hs"], dhl, w["cw"], p["cb"], p["wa"], p["ba"], p["wx"], p["bx"], p["lam"], P, l)
        du_pool, dpw, dpb, dpsc = pool_bwd(sv["proj"], dpm, p["pw"], p["pb"], p["ps"], l)
        dproj = jnp.concatenate([du_pool, du_lru, du_gelu, dgp, dgl], axis=1)
        G[l]["w_in"] = to_slots("w_in", dw_tn("dw_in", sv["h2"], row(D), dproj, lambda tk: pl.BlockSpec((tk, ci), lambda g, k: (k, g)),
                                              N_DEV, D, ci, T))
        dx, dgm = dx_norm_bwd("mix_dx", dproj, lambda tm: pl.BlockSpec((tm, ci), lambda j, i: (i, j)), w["win"], N_DEV,
                              sv["x2"], p["gm"], dx, l)
        small["norm_mix"][l] = dgm.reshape(D)
        small["pool_w"][l], small["pool_b"][l], small["pool_scale"][l] = dpw[0], dpb.reshape(pool_b.shape[1:]), dpsc.reshape(P)
        small["conv_w"][l], small["conv_b"][l] = dcw[0], dcb.reshape(R)
        small["lru_w_a"][l], small["lru_b_a"][l] = dwa[0], dba.reshape(H, hd)
        small["lru_w_x"][l], small["lru_b_x"][l] = dwx[0], dbx.reshape(H, hd)
        small["lru_lambda"][l] = dlam.reshape(R)
        deps = boundary(l, 1, dx)
        dx = ffn_bwd(dx, sv, "1", w["wup1"], w["wd1"], p["g1"], "ffn1_w_up", "ffn1_w_down", "norm_ffn1", l, deps)
        deps = boundary(l, 0, dx)

    grad_x = dx.reshape(x.shape)

    small_parts = [jnp.stack(small[n]) for n in SMALL if n != "final_norm"] + [d_final.reshape(D)]
    small_shapes = [p.shape for p in small_parts]
    gpack = _pack(small_parts).reshape(1, N_DEV, -1, 128)
    small_pair = pair_exchange_start("rs_pair_start_small", [gpack], queued["pair"])
    while pairing:
        reduce_continue([dx])

    outs = {n: None for n in BIG}

    def unit_updates(unit, recv2, deps=()):
        l, names, recv1 = unit[0], unit[1], unit[6]
        updated = grad_sum_adamw([G[l][n][0] for n in names], recv1, recv2, [W[n] for n in names], [M[n] for n in names],
                                 [V[n] for n in names], place, l, [outs[n] for n in names], deps)
        outs.update(zip(names, updated))
        return [outs[n][0] for n in names]

    after = [dx]
    late = in_flight[-2:]
    for k, unit in enumerate(in_flight[:-2]):
        recv2 = chip_exchange_wait(f"rs_chip_wait_{unit[2]}", unit[5], unit[3], unit[4], after)
        after = unit_updates(unit, recv2)
        if k == 0:
            recv1_s = pair_exchange_wait("rs_pair_wait_small", small_pair[2], small_pair[0], small_pair[1], after)[0]
            small_chip = chip_exchange_start("rs_chip_start_small", pair_sum([gpack], [recv1_s], place), queued["chip"])
    late_recv2 = []
    for unit in late:
        late_recv2.append(chip_exchange_wait(f"rs_chip_wait_{unit[2]}", unit[5], unit[3], unit[4], after))
        after = [late_recv2[-1][0]]
    recv2_s = chip_exchange_wait("rs_chip_wait_small", small_chip[2], small_chip[0], small_chip[1], after)[0]
    gs = grad_sum(gpack, recv1_s, recv2_s, place)
    gs_slots = place_own([gs.reshape((1,) + gs.shape)], [F32], 0, place)
    send_sems, recv_sems, gs_slots, tok = gather_start("gather_start_small", gs_slots)
    after = unit_updates(late[0], late_recv2[0], (tok,))
    gs_slots = gather_wait("gather_wait_small", gs_slots, send_sems, recv_sems, after)
    send_sems, recv_sems, gs_slots, tok = gather_forward_start("gather_pass_start_small", gs_slots)
    after = unit_updates(late[1], late_recv2[1], (tok,))
    gs_all = gather_forward_wait("gather_pass_wait_small", gs_slots, send_sems, recv_sems, after)[0].reshape(-1, 128)
    for n in ("ffn1_w_up", "ffn2_w_up"):
        outs[n] = [jnp.swapaxes(o, 1, 2) for o in outs[n]]
    out_g, out_d, out_m, out_v = ({n: outs[n][k] for n in BIG} for k in range(4))

    small_g = dict(zip(SMALL, _unpack(gs_all, small_shapes)))
    for n in SMALL:
        if n != "conv_w":
            flat = lambda a: a.reshape(-1, 128)
            out_g[n] = small_g[n]
            out_d[n], out_m[n], out_v[n] = (o.reshape(W[n].shape) for o in adamw(flat(W[n]), flat(small_g[n]), flat(M[n]), flat(V[n])))
    cwc = conv_w.shape[2]
    gcw = lax.dynamic_slice_in_dim(small_g["conv_w"], dev * cwc, cwc, axis=2)
    cw2 = lambda a: a.reshape(-1, cwc)
    pad_rows = (-cw2(conv_w).shape[0]) % 8
    padr = lambda a: jnp.pad(cw2(a), ((0, pad_rows), (0, 0)))
    dcw_, mcw_, vcw_ = adamw(padr(conv_w), padr(gcw), padr(M["conv_w"]), padr(V["conv_w"]))
    nrow = cw2(conv_w).shape[0]
    out_g["conv_w"] = gcw
    out_d["conv_w"], out_m["conv_w"], out_v["conv_w"] = (a[:nrow].reshape(conv_w.shape) for a in (dcw_, mcw_, vcw_))

    return (loss, grad_x, *[out_g[n] for n in NAMES], *[out_d[n] for n in NAMES], *[out_m[n] for n in NAMES], *[out_v[n] for n in NAMES])
```

```python
import jax
import jax.numpy as jnp
from jax import lax
from jax.experimental import pallas as pl
from jax.experimental.pallas import tpu as pltpu

F32, BF16 = jnp.float32, jnp.bfloat16
EPS = 1e-6
LRU_C = 8.0
POOL_WINDOWS = (2, 4, 8, 16)
ADAM_LR, ADAM_B1, ADAM_B2, ADAM_EPS, ADAM_WD, ADAM_STEP = 0.001, 0.9, 0.999, 1e-08, 0.01, 10
N_DEV = 8
N_CHIP = 4
MESH = pl.DeviceIdType.MESH
V7X_VMEM_LIMIT = 56 * 1024 * 1024
ROW_TILE = 512
WIDE_TILE = 1024
SUM_TILE = 2048
ANY = pl.BlockSpec(memory_space=pl.ANY)

_pallas_call = pl.pallas_call


def _cp(*sem):
    return pltpu.CompilerParams(dimension_semantics=sem if sem else None, vmem_limit_bytes=V7X_VMEM_LIMIT)


def _tile(n, t):
    t = min(n, t)
    assert n % t == 0, (n, t)
    return t


def _dot(a, b):
    return jnp.dot(a, b, preferred_element_type=F32)


def _dot_nt(a, b):
    return lax.dot_general(a, b, (((1,), (1,)), ((), ())), preferred_element_type=F32)


def _dot_tn(a, b):
    return lax.dot_general(a, b, (((0,), (0,)), ((), ())), preferred_element_type=F32)


def _rms(xv):
    r = lax.rsqrt(jnp.mean(xv * xv, axis=-1, keepdims=True) + EPS)
    return xv * r, r


def _rms_bwd(dh, xv, gv, dy):
    n, r = _rms(xv)
    dn = dh * gv
    dx = dy + r * (dn - n * jnp.mean(dn * n, axis=-1, keepdims=True))
    return dx, jnp.sum(dh * n, axis=0, keepdims=True)


def _shift_down(x, k, fill=0.0):
    if k == 0:
        return x
    rows = lax.broadcasted_iota(jnp.int32, x.shape, 0)
    return jnp.where(rows >= k, pltpu.roll(x, k, 0), fill)


def _shift_up(x, k, fill=0.0):
    if k == 0:
        return x
    n = x.shape[0]
    rows = lax.broadcasted_iota(jnp.int32, x.shape, 0)
    return jnp.where(rows < n - k, pltpu.roll(x, n - k, 0), fill)


def _sigmoid(x):
    return 0.5 * jnp.tanh(0.5 * x) + 0.5


_GELU_K = 0.7978845608028654
_GELU_C = 0.044715


def _gelu(x):
    th = jnp.tanh(_GELU_K * (x + _GELU_C * x * x * x))
    return 0.5 * x * (1.0 + th), th


def _gelu_grad(x, th):
    return 0.5 * (1.0 + th) + 0.5 * x * (1.0 - th * th) * _GELU_K * (1.0 + 3.0 * _GELU_C * x * x)


def ffn_up(x, g, wup, l):
    T, D = x.shape
    cs = wup.shape[-2]
    tm = _tile(T, WIDE_TILE)
    ni = T // tm

    def body(x_ref, g_ref, wa_ref, wb_ref, h_ref, u_ref, s_ref, hs_ref):
        rows = pl.ds(pl.multiple_of(pl.program_id(1) * tm, tm), tm)

        @pl.when(pl.program_id(0) == 0)
        def _():
            n, _r = _rms(x_ref[...])
            hv = (n * g_ref[0]).astype(BF16)
            hs_ref[rows, :] = hv
            h_ref[...] = hv

        hv = hs_ref[rows, :]
        a = _dot_nt(hv, wa_ref[0, 0])
        b = _dot_nt(hv, wb_ref[0, 0])
        u_ref[0, 0] = a.astype(BF16)
        u_ref[1, 0] = b.astype(BF16)
        s_ref[0] = (a * _sigmoid(a) * b).astype(BF16)

    first = lambda j, i: (jnp.where(j == 0, i, ni - 1), 0)
    return _pallas_call(
        body, name="ffn_up", grid=(4, ni),
        in_specs=[pl.BlockSpec((tm, D), first), pl.BlockSpec((1, 1, D), lambda j, i: (l, 0, 0)),
                  pl.BlockSpec((1, 1, cs, D), lambda j, i: (0, j, 0, 0)), pl.BlockSpec((1, 1, cs, D), lambda j, i: (0, j + 4, 0, 0))],
        out_specs=[pl.BlockSpec((tm, D), first), pl.BlockSpec((2, 1, tm, cs), lambda j, i: (0, j, i, 0)),
                   pl.BlockSpec((1, tm, cs), lambda j, i: (j, i, 0))],
        out_shape=[jax.ShapeDtypeStruct((T, D), BF16), jax.ShapeDtypeStruct((2, 4, T, cs), BF16), jax.ShapeDtypeStruct((4, T, cs), BF16)],
        scratch_shapes=[pltpu.VMEM((T, D), BF16)],
        compiler_params=_cp("arbitrary", "arbitrary"),
    )(x, g, wup, wup)


def ffn_down(s, wd, x, l, deps=()):
    ng, T, cs = s.shape
    D = x.shape[1]
    tm = _tile(T, WIDE_TILE)

    def body(s_ref, w_ref, x_ref, *rest):
        o_ref = rest[len(deps)]
        acc = _dot(s_ref[0], w_ref[0, 0:cs, :])
        for j in range(1, ng):
            acc = acc + _dot(s_ref[j], w_ref[0, j * cs:(j + 1) * cs, :])
        o_ref[...] = x_ref[...] + 0.5 * acc

    return _pallas_call(
        body, name="ffn_down", grid=(T // tm,),
        in_specs=[pl.BlockSpec((ng, tm, cs), lambda i: (0, i, 0)), pl.BlockSpec((1, ng * cs, D), lambda i: (0, 0, 0)),
                  pl.BlockSpec((tm, D), lambda i: (i, 0))] + [ANY] * len(deps),
        out_specs=pl.BlockSpec((tm, D), lambda i: (i, 0)),
        out_shape=jax.ShapeDtypeStruct((T, D), F32),
        compiler_params=_cp("parallel"),
    )(s, wd, x, *deps)


def mix_in(x, g, win, l):
    T, D = x.shape
    ci = win.shape[-1]
    tm = _tile(T, WIDE_TILE)
    ni = T // tm

    def body(x_ref, g_ref, w_ref, h_ref, p_ref, hs_ref):
        rows = pl.ds(pl.multiple_of(pl.program_id(1) * tm, tm), tm)

        @pl.when(pl.program_id(0) == 0)
        def _():
            n, _r = _rms(x_ref[...])
            hv = (n * g_ref[0]).astype(BF16)
            hs_ref[rows, :] = hv
            h_ref[...] = hv

        p_ref[...] = _dot(hs_ref[rows, :], w_ref[0, 0]).astype(BF16)

    first = lambda j, i: (jnp.where(j == 0, i, ni - 1), 0)
    return _pallas_call(
        body, name="mix_in", grid=(N_DEV, ni),
        in_specs=[pl.BlockSpec((tm, D), first), pl.BlockSpec((1, 1, D), lambda j, i: (l, 0, 0)),
                  pl.BlockSpec((1, 1, D, ci), lambda j, i: (0, j, 0, 0))],
        out_specs=[pl.BlockSpec((tm, D), first), pl.BlockSpec((tm, ci), lambda j, i: (i, j))],
        out_shape=[jax.ShapeDtypeStruct((T, D), BF16), jax.ShapeDtypeStruct((T, N_DEV * ci), BF16)],
        scratch_shapes=[pltpu.VMEM((T, D), BF16)],
        compiler_params=_cp("arbitrary", "arbitrary"),
    )(x, g, win)


def _inv_count(T, w):
    t = lax.broadcasted_iota(jnp.int32, (T, 1), 0)
    return 1.0 / jnp.minimum(t + 1, w).astype(F32)


def _pooled(ug, w, inv):
    s = ug
    k = 1
    while k < w:
        s = s + _shift_down(s, k)
        k *= 2
    return s * inv - ug


def pool_fwd(proj, pw, pb, ps, l):
    T = proj.shape[0]
    _, G, gd, _ = pw.shape
    P = G * gd

    def body(u_ref, w_ref, b_ref, s_ref, o_ref):
        for gi in range(G):
            cols = slice(gi * gd, (gi + 1) * gd)
            ug = u_ref[:, cols].astype(F32)
            pooled = _pooled(ug, POOL_WINDOWS[gi], _inv_count(T, POOL_WINDOWS[gi]))
            mixed = _dot(pooled.astype(BF16), w_ref[0, gi].astype(BF16)) + b_ref[0, :, cols]
            o_ref[:, cols] = (mixed * s_ref[0, :, cols]).astype(BF16)

    return _pallas_call(
        body, name="pool_fwd", grid=(1,),
        in_specs=[pl.BlockSpec((T, P), lambda i: (0, 0)), pl.BlockSpec((1, G, gd, gd), lambda i: (l, 0, 0, 0)),
                  pl.BlockSpec((1, 1, P), lambda i: (l, 0, 0)), pl.BlockSpec((1, 1, P), lambda i: (l, 0, 0))],
        out_specs=pl.BlockSpec((T, P), lambda i: (0, 0)),
        out_shape=jax.ShapeDtypeStruct((T, P), BF16),
        compiler_params=_cp("arbitrary"),
    )(proj, pw, pb, ps)


def _conv(u, cw_ref, cb):
    CW = cw_ref.shape[1]
    v = cb
    for k in range(CW):
        v = v + cw_ref[0, k:k + 1, :] * _shift_down(u, CW - 1 - k)
    return v


def _softplus(z):
    return jnp.maximum(z, 0.0) + jnp.log1p(jnp.exp(-jnp.abs(z)))


def _lru_gates(v, wa_ref, ba, wx_ref, bx, lam):
    vb = v.astype(BF16)
    r = _sigmoid(_dot(vb, wa_ref[0, 0].astype(BF16)) + ba)
    i = _sigmoid(_dot(vb, wx_ref[0, 0].astype(BF16)) + bx)
    sp = _softplus(-lam)
    log_a = -LRU_C * r * sp
    a = jnp.exp(log_a)
    m2 = -jnp.tanh(log_a) * (a * a + 1.0)
    inv_mult = lax.rsqrt(m2)
    mult = jnp.where(m2 > 0.0, m2 * inv_mult, 0.0)
    return r, i, sp, a, mult, inv_mult


def _scan_fwd(a_ref, b_ref, o_ref):
    T, W = a_ref.shape
    rows = lax.broadcasted_iota(jnp.int32, (8, W), 0)

    def step(t, carry):
        r0 = pl.multiple_of(t * 8, 8)
        A = a_ref[pl.ds(r0, 8), :]
        B = b_ref[pl.ds(r0, 8), :]
        for s in (1, 2, 4):
            keep = rows >= s
            As = jnp.where(keep, pltpu.roll(A, s, 0), 1.0)
            Bs = jnp.where(keep, pltpu.roll(B, s, 0), 0.0)
            B = A * Bs + B
            A = A * As
        h = B + A * carry
        o_ref[pl.ds(r0, 8), :] = h
        return jnp.broadcast_to(h[7:8, :], (8, W))

    lax.fori_loop(0, T // 8, step, jnp.zeros((8, W), F32), unroll=8)


def _scan_bwd(a_ref, b_ref, o_ref):
    T, W = a_ref.shape
    rows = lax.broadcasted_iota(jnp.int32, (8, W), 0)
    nt = T // 8

    def step(t, carry):
        r0 = pl.multiple_of((nt - 1 - t) * 8, 8)
        A = a_ref[pl.ds(r0, 8), :]
        B = b_ref[pl.ds(r0, 8), :]
        for s in (1, 2, 4):
            keep = rows < 8 - s
            As = jnp.where(keep, pltpu.roll(A, 8 - s, 0), 1.0)
            Bs = jnp.where(keep, pltpu.roll(B, 8 - s, 0), 0.0)
            B = A * Bs + B
            A = A * As
        y = B + A * carry
        o_ref[pl.ds(r0, 8), :] = y
        return jnp.broadcast_to(y[0:1, :], (8, W))

    lax.fori_loop(0, nt, step, jnp.zeros((8, W), F32), unroll=8)


def _lru_specs(T, hd, P, R, CW, l):
    ob, gb = P // hd, (P + R) // hd
    vec = pl.BlockSpec((1, 1, hd), lambda h: (l, 0, h))
    mat = pl.BlockSpec((1, 1, hd, hd), lambda h: (l, h, 0, 0))
    return [pl.BlockSpec((T, hd), lambda h: (0, ob + h)), pl.BlockSpec((T, hd), lambda h: (0, gb + h)),
            pl.BlockSpec((1, CW, hd), lambda h: (0, 0, h)), vec, mat, vec, mat, vec, vec]


def lru_fwd(proj, cw, cb, wa, ba, wx, bx, lam, P, l):
    T = proj.shape[0]
    _, H, hd, _ = wa.shape
    R = H * hd
    CW = cw.shape[1]
    assert P % hd == 0 and T % 8 == 0

    def body(u_ref, ug_ref, cw_ref, cb_ref, wa_ref, ba_ref, wx_ref, bx_ref, lam_ref, hl_ref, hs_ref, a_s, b_s):
        v = _conv(u_ref[...].astype(F32), cw_ref, cb_ref[0])
        _r, i, _sp, a, mult, _im = _lru_gates(v, wa_ref, ba_ref[0], wx_ref, bx_ref[0], lam_ref[0])
        a_s[...] = a
        b_s[...] = mult * (i * v)
        _scan_fwd(a_s, b_s, hs_ref)
        ge, _th = _gelu(ug_ref[...].astype(F32))
        hl_ref[...] = (hs_ref[...] * ge).astype(BF16)

    out = pl.BlockSpec((T, hd), lambda h: (0, h))
    return _pallas_call(
        body, name="lru_fwd", grid=(H,),
        in_specs=_lru_specs(T, hd, P, R, CW, l),
        out_specs=[out, out],
        out_shape=[jax.ShapeDtypeStruct((T, R), BF16), jax.ShapeDtypeStruct((T, R), F32)],
        scratch_shapes=[pltpu.VMEM((T, hd), F32)] * 2,
        compiler_params=_cp("parallel"),
    )(proj, proj, cw, cb, wa, ba, wx, bx, lam)


def mix_out(pm, hl, proj, x, wpu, wlu, wout, P, l, deps=()):
    T, D = x.shape
    R = hl.shape[1]
    tm = _tile(T, ROW_TILE)
    assert (P + 2 * R) % D == 0
    gb = (P + 2 * R) // D

    def body(pm_ref, hl_ref, gp_ref, gl_ref, x_ref, wpu_ref, wlu_ref, wo_ref, *rest):
        o_ref, yp_ref, yl_ref, z_ref = rest[len(deps):]
        yp = _dot(pm_ref[...], wpu_ref[0])
        yl = _dot(hl_ref[...], wlu_ref[0])
        z = (_sigmoid(gp_ref[...].astype(F32)) * yp + _sigmoid(gl_ref[...].astype(F32)) * yl).astype(BF16)
        yp_ref[...] = yp.astype(BF16)
        yl_ref[...] = yl.astype(BF16)
        z_ref[...] = z
        o_ref[...] = x_ref[...] + _dot(z, wo_ref[0])

    row = lambda w: pl.BlockSpec((tm, w), lambda i: (i, 0))
    return _pallas_call(
        body, name="mix_out", grid=(T // tm,),
        in_specs=[row(P), row(R), pl.BlockSpec((tm, D), lambda i: (i, gb)), pl.BlockSpec((tm, D), lambda i: (i, gb + 1)), row(D),
                  pl.BlockSpec((1, P, D), lambda i: (0, 0, 0)), pl.BlockSpec((1, R, D), lambda i: (0, 0, 0)),
                  pl.BlockSpec((1, D, D), lambda i: (0, 0, 0))] + [ANY] * len(deps),
        out_specs=[row(D)] * 4,
        out_shape=[jax.ShapeDtypeStruct((T, D), F32)] + [jax.ShapeDtypeStruct((T, D), BF16)] * 3,
        compiler_params=_cp("parallel"),
    )(pm, hl, proj, proj, x, wpu, wlu, wout, *deps)


def loss_head(x, gf, tgt):
    T, D = x.shape
    tm = _tile(T, ROW_TILE)

    def body(x_ref, g_ref, t_ref, loss_ref, dx_ref, dg_ref):
        @pl.when(pl.program_id(0) == 0)
        def _():
            loss_ref[...] = jnp.zeros_like(loss_ref)
            dg_ref[...] = jnp.zeros_like(dg_ref)

        xv = x_ref[...]
        gv = g_ref[...]
        n, _r = _rms(xv)
        e = n * gv - t_ref[...]
        loss_ref[...] += 0.5 * jnp.sum(jnp.sum(e * e, axis=-1, keepdims=True), axis=0, keepdims=True) / D
        dx, dg = _rms_bwd(e * (1.0 / D), xv, gv, 0.0)
        dx_ref[...] = dx
        dg_ref[...] += dg

    return _pallas_call(
        body, name="loss_head", grid=(T // tm,),
        in_specs=[pl.BlockSpec((tm, D), lambda i: (i, 0)), pl.BlockSpec((1, D), lambda i: (0, 0)), pl.BlockSpec((tm, D), lambda i: (i, 0))],
        out_specs=[pl.BlockSpec((1, 1), lambda i: (0, 0)), pl.BlockSpec((tm, D), lambda i: (i, 0)), pl.BlockSpec((1, D), lambda i: (0, 0))],
        out_shape=[jax.ShapeDtypeStruct((1, 1), F32), jax.ShapeDtypeStruct((T, D), F32), jax.ShapeDtypeStruct((1, D), F32)],
        compiler_params=_cp("arbitrary"),
    )(x, gf, tgt)


def ffn_down_bwd(dy, wd, u, l, deps=()):
    T, D = dy.shape
    cs = u.shape[-1]
    tm = _tile(T, WIDE_TILE)
    ni = T // tm

    def body(dy_ref, w_ref, u_ref, *rest):
        do_ref, du_ref, dyb_ref = rest[len(deps):]
        rows = pl.ds(pl.multiple_of(pl.program_id(1) * tm, tm), tm)

        @pl.when(pl.program_id(0) == 0)
        def _():
            d = (0.5 * dy_ref[...]).astype(BF16)
            dyb_ref[rows, :] = d
            do_ref[...] = d

        ds = _dot_nt(dyb_ref[rows, :], w_ref[0])
        a = u_ref[0, 0].astype(F32)
        b = u_ref[1, 0].astype(F32)
        sg = _sigmoid(a)
        du_ref[0, 0] = (ds * b * (sg * (1.0 + a * (1.0 - sg)))).astype(BF16)
        du_ref[1, 0] = (ds * (a * sg)).astype(BF16)

    first = lambda j, i: (jnp.where(j == 0, i, ni - 1), 0)
    blk = pl.BlockSpec((2, 1, tm, cs), lambda j, i: (0, j, i, 0))
    return _pallas_call(
        body, name="ffn_down_bwd", grid=(4, ni),
        in_specs=[pl.BlockSpec((tm, D), first), pl.BlockSpec((1, cs, D), lambda j, i: (0, j, 0)), blk] + [ANY] * len(deps),
        out_specs=[pl.BlockSpec((tm, D), first), blk],
        out_shape=[jax.ShapeDtypeStruct((T, D), BF16), jax.ShapeDtypeStruct((2, 4, T, cs), BF16)],
        scratch_shapes=[pltpu.VMEM((T, D), BF16)],
        compiler_params=_cp("arbitrary", "arbitrary"),
    )(dy, wd, u, *deps)


def dw_tn(name, a, a_spec, b, b_spec, G, M, N, T):
    tk = _tile(T, SUM_TILE)
    nk = T // tk

    def body(a_ref, b_ref, o32_ref, o16_ref, acc_ref):
        k = pl.program_id(1)

        @pl.when(k == 0)
        def _():
            acc_ref[...] = jnp.zeros_like(acc_ref)

        av = a_ref[0] if len(a_ref.shape) == 3 else a_ref[...]
        bv = b_ref[0] if len(b_ref.shape) == 3 else b_ref[...]
        acc_ref[...] += _dot_tn(av, bv)

        @pl.when(k == nk - 1)
        def _():
            o32_ref[0, 0] = acc_ref[...]
            o16_ref[0, 0] = acc_ref[...].astype(BF16)

    out = pl.BlockSpec((1, 1, M, N), lambda g, k: (0, g, 0, 0))
    return _pallas_call(
        body, name=name, grid=(G, nk),
        in_specs=[a_spec(tk), b_spec(tk)], out_specs=[out, out],
        out_shape=[jax.ShapeDtypeStruct((1, G, M, N), F32), jax.ShapeDtypeStruct((1, G, M, N), BF16)],
        scratch_shapes=[pltpu.VMEM((M, N), F32)],
        compiler_params=_cp("parallel", "arbitrary"),
    )(a, b)


def dx_norm_bwd(name, dact, d_spec, w, G, x, g, dy, l, w_transposed=False, deps=()):
    T, D = x.shape
    wblk = w.shape[-2:]
    tm = _tile(T, WIDE_TILE)
    ni = T // tm
    ch = _tile(tm, ROW_TILE // 2)

    def body(d_ref, w_ref, x_ref, g_ref, dy_ref, *rest):
        dx_ref, dg_ref, acc_ref = rest[len(deps):]
        j, i = pl.program_id(0), pl.program_id(1)
        rows = pl.ds(pl.multiple_of(i * tm, tm), tm)

        @pl.when(jnp.logical_and(i == 0, j == 0))
        def _():
            dg_ref[...] = jnp.zeros_like(dg_ref)

        @pl.when(j == 0)
        def _():
            acc_ref[rows, :] = jnp.zeros((tm, D), F32)

        dv = d_ref[0] if len(d_ref.shape) == 3 else d_ref[...]
        acc_ref[rows, :] += _dot(dv, w_ref[0, 0]) if w_transposed else _dot_nt(dv, w_ref[0, 0])

        @pl.when(j == G - 1)
        def _():
            dg = jnp.zeros((1, D), F32)
            for c0 in range(0, tm, ch):
                part_rows = pl.ds(pl.multiple_of(i * tm + c0, ch), ch)
                dx, dgc = _rms_bwd(acc_ref[part_rows, :], x_ref[c0:c0 + ch, :], g_ref[0], dy_ref[c0:c0 + ch, :])
                dx_ref[c0:c0 + ch, :] = dx
                dg = dg + dgc
            dg_ref[...] += dg

    last = pl.BlockSpec((tm, D), lambda j, i: (jnp.where(j == G - 1, i, 0), 0))
    return _pallas_call(
        body, name=name, grid=(G, ni),
        in_specs=[d_spec(tm), pl.BlockSpec((1, 1) + wblk, lambda j, i: (0, j, 0, 0)), last, pl.BlockSpec((1, 1, D), lambda j, i: (l, 0, 0)), last]
        + [ANY] * len(deps),
        out_specs=[last, pl.BlockSpec((1, D), lambda j, i: (0, 0))],
        out_shape=[jax.ShapeDtypeStruct((T, D), F32), jax.ShapeDtypeStruct((1, D), F32)],
        scratch_shapes=[pltpu.VMEM((T, D), F32)],
        compiler_params=_cp("arbitrary", "arbitrary"),
    )(dact, w, x, g, dy, *deps)


def mix_out_bwd(dy, proj, yp, yl, wpu, wlu, wout, P, R, l, deps=()):
    T, D = dy.shape
    tm = _tile(T, ROW_TILE)
    gb = (P + 2 * R) // D

    def body(dy_ref, gp_ref, gl_ref, yp_ref, yl_ref, wpu_ref, wlu_ref, wo_ref, *rest):
        dyb_ref, dyp_ref, dyl_ref, dgp_ref, dgl_ref, dpm_ref, dhl_ref = rest[len(deps):]
        dyb = dy_ref[...].astype(BF16)
        dyb_ref[...] = dyb
        dz = _dot_nt(dyb, wo_ref[0])
        sp = _sigmoid(gp_ref[...].astype(F32))
        sl = _sigmoid(gl_ref[...].astype(F32))
        dgp_ref[...] = (dz * yp_ref[...].astype(F32) * sp * (1.0 - sp)).astype(BF16)
        dgl_ref[...] = (dz * yl_ref[...].astype(F32) * sl * (1.0 - sl)).astype(BF16)
        dyp = (dz * sp).astype(BF16)
        dyl = (dz * sl).astype(BF16)
        dyp_ref[...] = dyp
        dyl_ref[...] = dyl
        dpm_ref[...] = _dot_nt(dyp, wpu_ref[0]).astype(BF16)
        dhl_ref[...] = _dot_nt(dyl, wlu_ref[0]).astype(BF16)

    row = lambda w: pl.BlockSpec((tm, w), lambda i: (i, 0))
    return _pallas_call(
        body, name="mix_out_bwd", grid=(T // tm,),
        in_specs=[row(D), pl.BlockSpec((tm, D), lambda i: (i, gb)), pl.BlockSpec((tm, D), lambda i: (i, gb + 1)), row(D), row(D),
                  pl.BlockSpec((1, P, D), lambda i: (0, 0, 0)), pl.BlockSpec((1, R, D), lambda i: (0, 0, 0)),
                  pl.BlockSpec((1, D, D), lambda i: (0, 0, 0))] + [ANY] * len(deps),
        out_specs=[row(D)] * 5 + [row(P), row(R)],
        out_shape=[jax.ShapeDtypeStruct((T, D), BF16)] * 5 + [jax.ShapeDtypeStruct((T, P), BF16), jax.ShapeDtypeStruct((T, R), BF16)],
        compiler_params=_cp("parallel"),
    )(dy, proj, proj, yp, yl, wpu, wlu, wout, *deps)


def lru_bwd(proj, hs, dhl, cw, cb, wa, ba, wx, bx, lam, P, l):
    T = proj.shape[0]
    _, H, hd, _ = wa.shape
    R = H * hd
    CW = cw.shape[1]

    def body(u_ref, ug_ref, cw_ref, cb_ref, wa_ref, ba_ref, wx_ref, bx_ref, lam_ref, hs_ref, dhl_ref,
             du_ref, dug_ref, dcw_ref, dcb_ref, dwa_ref, dba_ref, dwx_ref, dbx_ref, dlam_ref, c_s, g_s, y_s):
        u = u_ref[...].astype(F32)
        v = _conv(u, cw_ref, cb_ref[0])
        lam = lam_ref[0]
        r, i, sp, a, mult, inv_mult = _lru_gates(v, wa_ref, ba_ref[0], wx_ref, bx_ref[0], lam)
        ug = ug_ref[...].astype(F32)
        ge, th = _gelu(ug)
        hs = hs_ref[...]
        dhl = dhl_ref[...].astype(F32)
        dug_ref[...] = (dhl * hs * _gelu_grad(ug, th)).astype(BF16)
        c_s[...] = _shift_up(a, 1)
        g_s[...] = dhl * ge
        _scan_bwd(c_s, g_s, y_s)
        y = y_s[...]
        da = y * _shift_down(hs, 1)
        iv = i * v
        dlog_a = da * a - (y * iv) * (a * a) * inv_mult
        div = y * mult
        dpa = (dlog_a * (-LRU_C) * sp) * r * (1.0 - r)
        dpx = (div * v) * i * (1.0 - i)
        dsp = jnp.sum(dlog_a * (-LRU_C) * r, axis=0, keepdims=True)
        dlam_ref[0] = -dsp * _sigmoid(-lam)
        vb = v.astype(BF16)
        dpab, dpxb = dpa.astype(BF16), dpx.astype(BF16)
        dwa_ref[0, 0] = _dot_tn(vb, dpab)
        dwx_ref[0, 0] = _dot_tn(vb, dpxb)
        dba_ref[0] = jnp.sum(dpa, axis=0, keepdims=True)
        dbx_ref[0] = jnp.sum(dpx, axis=0, keepdims=True)
        dv = div * i + _dot_nt(dpab, wa_ref[0, 0].astype(BF16)) + _dot_nt(dpxb, wx_ref[0, 0].astype(BF16))
        dcb_ref[0] = jnp.sum(dv, axis=0, keepdims=True)
        du = jnp.zeros_like(dv)
        for k in range(CW):
            du = du + cw_ref[0, k:k + 1, :] * _shift_up(dv, CW - 1 - k)
            dcw_ref[0, k:k + 1, :] = jnp.sum(dv * _shift_down(u, CW - 1 - k), axis=0, keepdims=True)
        du_ref[...] = du.astype(BF16)

    col = pl.BlockSpec((T, hd), lambda h: (0, h))
    vec = pl.BlockSpec((1, 1, hd), lambda h: (0, 0, h))
    mat = pl.BlockSpec((1, 1, hd, hd), lambda h: (0, h, 0, 0))
    vshape = jax.ShapeDtypeStruct((1, 1, R), F32)
    mshape = jax.ShapeDtypeStruct((1, H, hd, hd), F32)
    return _pallas_call(
        body, name="lru_bwd", grid=(H,),
        in_specs=_lru_specs(T, hd, P, R, CW, l) + [col, col],
        out_specs=[col, col, pl.BlockSpec((1, CW, hd), lambda h: (0, 0, h)), vec, mat, vec, mat, vec, vec],
        out_shape=[jax.ShapeDtypeStruct((T, R), BF16)] * 2 + [jax.ShapeDtypeStruct((1, CW, R), F32), vshape, mshape, vshape, mshape, vshape, vshape],
        scratch_shapes=[pltpu.VMEM((T, hd), F32)] * 3,
        compiler_params=_cp("parallel"),
    )(proj, proj, cw, cb, wa, ba, wx, bx, lam, hs, dhl)


def pool_bwd(proj, dpm, pw, pb, ps, l):
    T = proj.shape[0]
    _, G, gd, _ = pw.shape
    P = G * gd

    def body(u_ref, d_ref, w_ref, b_ref, s_ref, du_ref, dw_ref, db_ref, dsc_ref):
        for gi in range(G):
            cols = slice(gi * gd, (gi + 1) * gd)
            w = POOL_WINDOWS[gi]
            inv = _inv_count(T, w)
            ug = u_ref[:, cols].astype(F32)
            pooled = _pooled(ug, w, inv).astype(BF16)
            wb = w_ref[0, gi].astype(BF16)
            mixed = _dot(pooled, wb) + b_ref[0, :, cols]
            dpm_g = d_ref[:, cols].astype(F32)
            dsc_ref[0, :, cols] = jnp.sum(dpm_g * mixed, axis=0, keepdims=True)
            dmixed = dpm_g * s_ref[0, :, cols]
            db_ref[0, :, cols] = jnp.sum(dmixed, axis=0, keepdims=True)
            dmb = dmixed.astype(BF16)
            dw_ref[0, gi] = _dot_tn(pooled, dmb)
            dpooled = _dot_nt(dmb, wb)
            s = dpooled * inv
            k = 1
            while k < w:
                s = s + _shift_up(s, k)
                k *= 2
            du_ref[:, cols] = (s - dpooled).astype(BF16)

    vec = pl.BlockSpec((1, 1, P), lambda i: (l, 0, 0))
    ovec = pl.BlockSpec((1, 1, P), lambda i: (0, 0, 0))
    return _pallas_call(
        body, name="pool_bwd", grid=(1,),
        in_specs=[pl.BlockSpec((T, P), lambda i: (0, 0)), pl.BlockSpec((T, P), lambda i: (0, 0)),
                  pl.BlockSpec((1, G, gd, gd), lambda i: (l, 0, 0, 0)), vec, vec],
        out_specs=[pl.BlockSpec((T, P), lambda i: (0, 0)), pl.BlockSpec((1, G, gd, gd), lambda i: (0, 0, 0, 0)), ovec, ovec],
        out_shape=[jax.ShapeDtypeStruct((T, P), BF16), jax.ShapeDtypeStruct((1, G, gd, gd), F32),
                   jax.ShapeDtypeStruct((1, 1, P), F32), jax.ShapeDtypeStruct((1, 1, P), F32)],
        compiler_params=_cp("arbitrary"),
    )(proj, dpm, pw, pb, ps)


def _place():
    x, y, c = lax.axis_index("x"), lax.axis_index("y"), lax.axis_index("c")
    return x, y, c


HBM = pl.BlockSpec(memory_space=pltpu.HBM)
SEM = pl.BlockSpec(memory_space=pltpu.SEMAPHORE)
EFFECT = pltpu.SideEffectType.DATAFLOW_SIDE_EFFECTING


def _in_hbm(a):
    return pltpu.with_memory_space_constraint(a, pltpu.HBM)


def split_start(name, bufs, n_copies, copies_of, deps=()):
    nb = len(bufs)

    def body(*refs):
        buf = refs[:nb]
        send_sems, recv_sems = refs[nb + len(deps)], refs[nb + len(deps) + 1]
        token = refs[-1]
        for i, (src, dst, dev) in enumerate(copies_of(buf)):
            pltpu.make_async_remote_copy(src_ref=src, dst_ref=dst, send_sem=send_sems.at[i], recv_sem=recv_sems.at[i],
                                         device_id=dev, device_id_type=MESH).start()
        token[...] = jnp.zeros_like(token)

    outs = _pallas_call(
        body, name=name,
        in_specs=[HBM] * nb + [ANY] * len(deps),
        out_specs=(SEM, SEM, *([HBM] * nb), pl.BlockSpec(memory_space=pltpu.VMEM)),
        out_shape=(pltpu.SemaphoreType.DMA((n_copies,)), pltpu.SemaphoreType.DMA((n_copies,)),
                   *[pltpu.HBM(b.shape, b.dtype) for b in bufs], jax.ShapeDtypeStruct((8, 128), F32)),
        input_output_aliases={i: 2 + i for i in range(nb)},
        compiler_params=pltpu.CompilerParams(has_side_effects=EFFECT),
    )(*[_in_hbm(b) for b in bufs], *deps)
    return outs[0], outs[1], list(outs[2:2 + nb]), outs[-1]


def split_wait(name, bufs, send_sems, recv_sems, after, copies_of):
    nb = len(bufs)

    def body(*refs):
        buf = refs[:nb]
        send, recv = refs[nb], refs[nb + 1]
        for i, (src, dst, dev) in enumerate(copies_of(buf)):
            cp = pltpu.make_async_remote_copy(src_ref=src, dst_ref=dst, send_sem=send.at[i], recv_sem=recv.at[i],
                                              device_id=dev, device_id_type=MESH)
            cp.wait_send()
            cp.wait_recv()

    outs = _pallas_call(
        body, name=name,
        in_specs=[HBM] * nb + [SEM, SEM] + [ANY] * len(after),
        out_specs=[HBM] * nb,
        out_shape=[pltpu.HBM(b.shape, b.dtype) for b in bufs],
        input_output_aliases={i: i for i in range(nb)},
        compiler_params=pltpu.CompilerParams(has_side_effects=EFFECT),
    )(*bufs, send_sems, recv_sems, *after)
    return list(outs)


def _two_row_blocks(rows):
    return (rows // 2, 1) if rows % 32 == 0 else (rows, 0)


def place_own(ws, dtypes, l, place):
    n = len(ws)

    def body(p_ref, *refs):
        for a in range(n):
            refs[n + a][0] = refs[a][0].astype(dtypes[a])

    in_specs, out_specs, out_shape = [], [], []
    for w, dt in zip(ws, dtypes):
        _, rows, cols = w.shape
        rb, step = _two_row_blocks(rows)
        in_specs.append(pl.BlockSpec((1, rb, cols), lambda i, p, s=step: (l, i * s, 0)))
        out_specs.append(pl.BlockSpec((1, rb, cols), lambda i, p, s=step: (p[2], i * s, 0)))
        out_shape.append(jax.ShapeDtypeStruct((N_DEV, rows, cols), dt))
    return list(_pallas_call(
        body, name="place_own",
        grid_spec=pltpu.PrefetchScalarGridSpec(num_scalar_prefetch=1, grid=(2,), in_specs=in_specs, out_specs=out_specs),
        out_shape=out_shape, compiler_params=_cp("arbitrary"),
    )(place, *ws))


def _gather_copies(land):
    x, y, c = _place()
    k = 4 * x + 2 * y + c
    peers = [(x, 1 - y, c), (1 - x, y, c), (1 - x, 1 - y, c), (x, y, 1 - c)]
    return [(b.at[k], b.at[k], p) for p in peers for b in land]


def gather_start(name, land, deps=()):
    return split_start(name, land, 4 * len(land), _gather_copies, deps)


def gather_wait(name, land, send_sems, recv_sems, after):
    return split_wait(name, land, send_sems, recv_sems, after, _gather_copies)


def _forward_copies(land):
    x, y, c = _place()
    slots = [4 * px + 2 * py + c for px, py in [(x, 1 - y), (1 - x, y), (1 - x, 1 - y)]]
    return [(b.at[k], b.at[k], (x, y, 1 - c)) for k in slots for b in land]


def gather_forward_start(name, land, deps=()):
    return split_start(name, land, 3 * len(land), _forward_copies, deps)


def gather_forward_wait(name, land, send_sems, recv_sems, after):
    return split_wait(name, land, send_sems, recv_sems, after, _forward_copies)


def _chip_copies(nsrc):
    def copies(buf):
        p16, recv2 = buf[:nsrc], buf[nsrc:]
        x, y, c = _place()
        out = []
        for d in (1, 2, 3):
            px = 1 - x if d & 2 else x
            py = 1 - y if d & 1 else y
            out += [(p16[a].at[:, d - 1], recv2[a].at[:, d - 1], (px, py, c)) for a in range(nsrc)]
        return out
    return copies


def _pair_copies(nsrc):
    def copies(buf):
        g16, recv = buf[:nsrc], buf[nsrc:]
        x, y, c = _place()
        return [(g16[a].at[:, 2 * j + 1 - c], recv[a].at[:, j], (x, y, 1 - c)) for a in range(nsrc) for j in range(N_CHIP)]
    return copies


def pair_exchange_start(name, g16, deps=()):
    n = len(g16)
    land = [lax.empty((1, N_CHIP) + s.shape[2:], s.dtype) for s in g16]
    return split_start(name, list(g16) + land, N_CHIP * n, _pair_copies(n), deps)


def pair_exchange_wait(name, bufs, send_sems, recv_sems, after):
    n = len(bufs) // 2
    return split_wait(name, bufs, send_sems, recv_sems, after, _pair_copies(n))[n:]


def chip_exchange_start(name, pair16, deps=()):
    n = len(pair16)
    land = [lax.empty((s.shape[0], 3) + s.shape[2:], s.dtype) for s in pair16]
    return split_start(name, list(pair16) + land, 3 * n, _chip_copies(n), deps)


def chip_exchange_wait(name, bufs, send_sems, recv_sems, after):
    n = len(bufs) // 2
    return split_wait(name, bufs, send_sems, recv_sems, after, _chip_copies(n))[n:]


def _rows_tile(rows, cols, budget=1 << 20):
    t = rows
    while t % 2 == 0 and t * cols > budget and (t // 2) % 16 == 0:
        t //= 2
    return t


def pair_sum(g32s, recv1s, place):
    n = len(g32s)

    def body(p_ref, *refs):
        for a in range(n):
            m_ref, r_ref, o_ref = refs[a], refs[n + a], refs[2 * n + a]
            o_ref[...] = (m_ref[...] + r_ref[...].astype(F32)).astype(o_ref.dtype)

    other = lambda d, p: jnp.bitwise_xor(p[1], d + 1)
    g_specs, r_specs, o_specs, out_shape = [], [], [], []
    for r1 in recv1s:
        _, _, rows, cols = r1.shape
        rb, step = _two_row_blocks(rows)
        g_specs.append(pl.BlockSpec((1, 1, rb, cols), lambda d, i, p, s=step: (0, 2 * other(d, p) + p[0], i * s, 0)))
        r_specs.append(pl.BlockSpec((1, 1, rb, cols), lambda d, i, p, s=step: (0, other(d, p), i * s, 0)))
        o_specs.append(pl.BlockSpec((1, 1, rb, cols), lambda d, i, p, s=step: (0, d, i * s, 0)))
        out_shape.append(jax.ShapeDtypeStruct((1, N_CHIP - 1, rows, cols), r1.dtype))
    return list(_pallas_call(
        body, name="pair_sum",
        grid_spec=pltpu.PrefetchScalarGridSpec(num_scalar_prefetch=1, grid=(N_CHIP - 1, 2), in_specs=g_specs + r_specs, out_specs=o_specs),
        out_shape=out_shape, compiler_params=_cp("arbitrary", "arbitrary"),
    )(place, *g32s, *recv1s))


def _grad_in_specs(tr, cols, l):
    return ([pl.BlockSpec((1, 1, tr, cols), lambda i, p: (l, p[2], i, 0)), pl.BlockSpec((1, 1, tr, cols), lambda i, p: (0, p[1], i, 0))]
            + [pl.BlockSpec((1, 1, tr, cols), lambda i, p, d=d: (0, d, i, 0)) for d in range(3)])


def _grad_total(o32, o16, r0, r1, r2):
    return (o32[0, 0] + o16[0, 0].astype(F32)) + r0[0, 0].astype(F32) + r1[0, 0].astype(F32) + r2[0, 0].astype(F32)


def grad_sum(g32, recv1, recv2, place):
    _, _, rows, cols = recv1.shape
    tr = _rows_tile(rows, cols)

    def body(p_ref, o32, o16, r0, r1, r2, g_ref):
        g_ref[...] = _grad_total(o32, o16, r0, r1, r2)

    return _pallas_call(
        body, name="grad_sum",
        grid_spec=pltpu.PrefetchScalarGridSpec(
            num_scalar_prefetch=1, grid=(rows // tr,), in_specs=_grad_in_specs(tr, cols, 0),
            out_specs=pl.BlockSpec((tr, cols), lambda i, p: (i, 0))),
        out_shape=jax.ShapeDtypeStruct((rows, cols), F32), compiler_params=_cp("parallel"),
    )(place, g32, recv1, recv2, recv2, recv2)


def _adamw_math(w, g, m, v):
    m = ADAM_B1 * m + (1.0 - ADAM_B1) * g
    v = ADAM_B2 * v + (1.0 - ADAM_B2) * (g * g)
    m_hat = m / (1.0 - ADAM_B1 ** ADAM_STEP)
    v_hat = v / (1.0 - ADAM_B2 ** ADAM_STEP)
    delta = -ADAM_LR * (m_hat / (jnp.sqrt(v_hat) + ADAM_EPS) + ADAM_WD * w)
    return delta, m, v


UPDATE_BLOCK = 1 << 16


def _update_rows(rows, cols):
    if rows % 16:
        return rows, 1
    tiles = rows // 16
    d = max([k for k in range(1, tiles + 1) if tiles % k == 0 and 16 * k * cols <= UPDATE_BLOCK] or [1])
    return 16 * d, tiles // d


def grad_sum_adamw(g32s, recv1s, recv2s, ws, ms, vs, place, l, prevs, deps=()):
    n = len(ws)
    blocks = [_update_rows(w.shape[1], w.shape[2]) for w in ws]
    have_prev = prevs[0] is not None

    def body(p_ref, *refs):
        outs = refs[len(refs) - 4 * n:]

        def update(a):
            o32, o16, r0, r1, r2, w_ref, m_ref, v_ref = refs[8 * a:8 * a + 8]
            g_ref, d_ref, nm_ref, nv_ref = outs[4 * a:4 * a + 4]

            @pl.when(pl.program_id(0) < blocks[a][1])
            def _():
                g = _grad_total(o32, o16, r0, r1, r2)
                d, nm, nv = _adamw_math(w_ref[0], g, m_ref[0], v_ref[0])
                g_ref[0] = g
                d_ref[0] = d
                nm_ref[0] = nm
                nv_ref[0] = nv

        for a in range(n):
            update(a)

    args, in_specs, out_specs, out_shape = [], [], [], []
    for a in range(n):
        L, rows, cols = ws[a].shape
        rb, nb = blocks[a]
        at = lambda i, nb=nb: jnp.minimum(i, nb - 1)
        slot = lambda which: pl.BlockSpec((1, 1, rb, cols), lambda i, p, at=at: (0, which(p), at(i), 0))
        blk = pl.BlockSpec((1, rb, cols), lambda i, p, at=at: (l, at(i), 0))
        args += [g32s[a], recv1s[a], recv2s[a], recv2s[a], recv2s[a], ws[a], ms[a], vs[a]]
        in_specs += [slot(lambda p: p[2]), slot(lambda p: p[1])] + [slot(lambda p, d=d: d) for d in range(3)] + [blk] * 3
        out_specs += [blk] * 4
        out_shape += [jax.ShapeDtypeStruct((L, rows, cols), F32)] * 4
    aliases = {}
    if have_prev:
        aliases = {1 + len(args) + k: k for k in range(4 * n)}
        args += [buf for prev in prevs for buf in prev]
        in_specs += [ANY] * (4 * n)
    args += list(deps)
    in_specs += [ANY] * len(deps)
    outs = _pallas_call(
        body, name="grad_sum_adamw",
        grid_spec=pltpu.PrefetchScalarGridSpec(num_scalar_prefetch=1, grid=(max(nb for _, nb in blocks),),
                                               in_specs=in_specs, out_specs=out_specs),
        out_shape=out_shape, input_output_aliases=aliases, compiler_params=_cp("arbitrary"),
    )(place, *args)
    return [list(outs[4 * a:4 * a + 4]) for a in range(n)]


def adamw(w, g, m, v):
    rows, cols = w.shape
    tr = _rows_tile(rows, cols, 1 << 18)

    def body(w_ref, g_ref, m_ref, v_ref, d_ref, nm_ref, nv_ref):
        d, nm, nv = _adamw_math(w_ref[...], g_ref[...], m_ref[...], v_ref[...])
        d_ref[...] = d
        nm_ref[...] = nm
        nv_ref[...] = nv

    blk = pl.BlockSpec((tr, cols), lambda i: (i, 0))
    return _pallas_call(body, name="adamw_small", grid=(rows // tr,), in_specs=[blk] * 4, out_specs=[blk] * 3,
                        out_shape=[jax.ShapeDtypeStruct((rows, cols), F32)] * 3, compiler_params=_cp("parallel"))(w, g, m, v)


SMALL = ("norm_ffn1", "norm_mix", "pool_w", "pool_b", "pool_scale", "conv_w", "conv_b", "lru_w_a", "lru_b_a", "lru_w_x", "lru_b_x",
         "lru_lambda", "norm_ffn2", "final_norm")
BIG = ("ffn1_w_up", "ffn1_w_down", "w_in", "w_pool_up", "w_lru_up", "w_out", "ffn2_w_up", "ffn2_w_down")
NAMES = ("norm_ffn1", "ffn1_w_up", "ffn1_w_down", "norm_mix", "w_in", "pool_w", "pool_b", "pool_scale", "w_pool_up", "conv_w", "conv_b",
         "lru_w_a", "lru_b_a", "lru_w_x", "lru_b_x", "lru_lambda", "w_lru_up", "w_out", "norm_ffn2", "ffn2_w_up", "ffn2_w_down", "final_norm")
SUBLAYERS = (("ffn1_w_up", "ffn1_w_down"), ("w_in", "w_pool_up", "w_lru_up", "w_out", "conv_w"), ("ffn2_w_up", "ffn2_w_down"))
PACK_ROWS = 16 * N_DEV


def _pack(parts):
    flat = jnp.concatenate([p.reshape(-1) for p in parts])
    unit = 128 * PACK_ROWS
    padded = -(-flat.size // unit) * unit
    return jnp.pad(flat, (0, padded - flat.size)).reshape(-1, 128)


def _unpack(packed, shapes):
    flat = packed.reshape(-1)
    out, off = [], 0
    for s in shapes:
        n = 1
        for d in s:
            n *= d
        out.append(flat[off:off + n].reshape(s))
        off += n
    return out


def kernel(x, norm_ffn1, ffn1_w_up, ffn1_w_down, norm_mix, w_in, pool_w, pool_b, pool_scale, w_pool_up, conv_w, conv_b, lru_w_a, lru_b_a, lru_w_x, lru_b_x, lru_lambda, w_lru_up, w_out, norm_ffn2, ffn2_w_up, ffn2_w_down, final_norm, loss_target, m_norm_ffn1, m_ffn1_w_up, m_ffn1_w_down, m_norm_mix, m_w_in, m_pool_w, m_pool_b, m_pool_scale, m_w_pool_up, m_conv_w, m_conv_b, m_lru_w_a, m_lru_b_a, m_lru_w_x, m_lru_b_x, m_lru_lambda, m_w_lru_up, m_w_out, m_norm_ffn2, m_ffn2_w_up, m_ffn2_w_down, m_final_norm, v_norm_ffn1, v_ffn1_w_up, v_ffn1_w_down, v_norm_mix, v_w_in, v_pool_w, v_pool_b, v_pool_scale, v_w_pool_up, v_conv_w, v_conv_b, v_lru_w_a, v_lru_b_a, v_lru_w_x, v_lru_b_x, v_lru_lambda, v_w_lru_up, v_w_out, v_norm_ffn2, v_ffn2_w_up, v_ffn2_w_down, v_final_norm):
    W = dict(norm_ffn1=norm_ffn1, ffn1_w_up=ffn1_w_up, ffn1_w_down=ffn1_w_down, norm_mix=norm_mix, w_in=w_in, pool_w=pool_w, pool_b=pool_b,
             pool_scale=pool_scale, w_pool_up=w_pool_up, conv_w=conv_w, conv_b=conv_b, lru_w_a=lru_w_a, lru_b_a=lru_b_a, lru_w_x=lru_w_x,
             lru_b_x=lru_b_x, lru_lambda=lru_lambda, w_lru_up=w_lru_up, w_out=w_out, norm_ffn2=norm_ffn2, ffn2_w_up=ffn2_w_up,
             ffn2_w_down=ffn2_w_down, final_norm=final_norm)
    M = dict(norm_ffn1=m_norm_ffn1, ffn1_w_up=m_ffn1_w_up, ffn1_w_down=m_ffn1_w_down, norm_mix=m_norm_mix, w_in=m_w_in, pool_w=m_pool_w,
             pool_b=m_pool_b, pool_scale=m_pool_scale, w_pool_up=m_w_pool_up, conv_w=m_conv_w, conv_b=m_conv_b, lru_w_a=m_lru_w_a,
             lru_b_a=m_lru_b_a, lru_w_x=m_lru_w_x, lru_b_x=m_lru_b_x, lru_lambda=m_lru_lambda, w_lru_up=m_w_lru_up, w_out=m_w_out,
             norm_ffn2=m_norm_ffn2, ffn2_w_up=m_ffn2_w_up, ffn2_w_down=m_ffn2_w_down, final_norm=m_final_norm)
    V = dict(norm_ffn1=v_norm_ffn1, ffn1_w_up=v_ffn1_w_up, ffn1_w_down=v_ffn1_w_down, norm_mix=v_norm_mix, w_in=v_w_in, pool_w=v_pool_w,
             pool_b=v_pool_b, pool_scale=v_pool_scale, w_pool_up=v_w_pool_up, conv_w=v_conv_w, conv_b=v_conv_b, lru_w_a=v_lru_w_a,
             lru_b_a=v_lru_b_a, lru_w_x=v_lru_w_x, lru_b_x=v_lru_b_x, lru_lambda=v_lru_lambda, w_lru_up=v_w_lru_up, w_out=v_w_out,
             norm_ffn2=v_norm_ffn2, ffn2_w_up=v_ffn2_w_up, ffn2_w_down=v_ffn2_w_down, final_norm=v_final_norm)

    for S in (W, M, V):
        for n in ("ffn1_w_up", "ffn2_w_up"):
            S[n] = jnp.swapaxes(S[n], 1, 2)

    T, D = x.shape[1], x.shape[2]
    L = norm_ffn1.shape[0]
    P = pool_scale.shape[1]
    R = lru_lambda.shape[1]
    H, hd = lru_w_a.shape[1], lru_w_a.shape[2]
    CW = conv_w.shape[1]
    cs = ffn1_w_up.shape[2]
    ci = w_in.shape[2]
    xin = x.reshape(T, D)
    tgt = loss_target.reshape(T, D)
    dev = 4 * lax.axis_index("x") + 2 * lax.axis_index("y") + lax.axis_index("c")
    place = jnp.stack([lax.axis_index("c"), 2 * lax.axis_index("x") + lax.axis_index("y"), dev]).astype(jnp.int32)

    cw_flat = conv_w.reshape(L, -1)
    cw_pad = (-cw_flat.shape[1]) % 1024
    cw_tiles = jnp.pad(cw_flat, ((0, 0), (0, cw_pad))).reshape(L, -1, 128)

    def units(l):
        return SUBLAYERS if l == 0 else (tuple(n for u in SUBLAYERS for n in u),)

    queued = {"gather": (), "pair": (), "chip": ()}

    gathering = []

    def gather_units_start(l):
        first_layer = (("ffn1_w_up",), ("ffn1_w_down",)) + SUBLAYERS[1:]
        for k, names in enumerate(first_layer if l == 0 else SUBLAYERS):
            land = place_own([cw_tiles if n == "conv_w" else W[n] for n in names], [F32 if n == "conv_w" else BF16 for n in names], l, place)
            send_sems, recv_sems, land, tok = gather_start(f"gather_start_l{l}_u{k}", land, queued["gather"])
            gathering.append(dict(names=names, tag=f"l{l}_u{k}", send=send_sems, recv=recv_sems, land=land, tok=tok, arrived=False))
            queued["gather"] = (tok,)

    def gather_unit_arrive(after):
        waiting = [u for u in gathering if not u["arrived"]]
        if not waiting:
            return ()
        unit, tokens = waiting[0], [u["tok"] for u in waiting[1:]]
        land = gather_wait(f"gather_wait_{unit['tag']}", unit["land"], unit["send"], unit["recv"], list(after) + tokens)
        send_sems, recv_sems, land, tok = gather_forward_start(f"gather_pass_start_{unit['tag']}", land)
        unit.update(land=land, send=send_sems, recv=recv_sems, tok=tok, arrived=True)
        return (tok,)

    def gather_unit_weights(after):
        if not gathering[0]["arrived"]:
            gather_unit_arrive(after)
        unit = gathering.pop(0)
        land = gather_forward_wait(f"gather_pass_wait_{unit['tag']}", unit["land"], unit["send"], unit["recv"], after)
        g = dict(zip(unit["names"], land))
        one = lambda a: a.reshape((1,) + a.shape)
        w = {}
        for tag_, up, dn in (("1", "ffn1_w_up", "ffn1_w_down"), ("2", "ffn2_w_up", "ffn2_w_down")):
            if up in g:
                w["wup" + tag_] = one(g[up])
            if dn in g:
                w["wd" + tag_] = g[dn].reshape(1, -1, D)
        if "w_in" in g:
            cw_l = g["conv_w"].reshape(N_DEV, -1)[:, :cw_flat.shape[1]].reshape((N_DEV,) + conv_w.shape[1:])
            w.update(win=one(g["w_in"]), wlu=g["w_lru_up"].reshape(1, R, D), wout=g["w_out"].reshape(1, D, D),
                     wpu=g["w_pool_up"].transpose(1, 0, 2).reshape(1, P, D),
                     cw=cw_l.transpose(1, 0, 2).reshape(1, CW, R))
        return w

    vec = lambda a: a.reshape(L, 1, -1)
    p = dict(g1=vec(norm_ffn1), gm=vec(norm_mix), g2=vec(norm_ffn2), pb=vec(pool_b), ps=vec(pool_scale), cb=vec(conv_b),
             ba=vec(lru_b_a), bx=vec(lru_b_x), lam=vec(lru_lambda), pw=pool_w, wa=lru_w_a, wx=lru_w_x)

    AHEAD = 2
    for l in range(min(AHEAD, L)):
        gather_units_start(l)
    saved, LW = [], []
    xc = xin
    for l in range(L):
        w = gather_unit_weights([xc])
        if l + AHEAD < L:
            gather_units_start(l + AHEAD)
        sv = {"x1": xc}
        sv["h1"], sv["u1"], sv["s1"] = ffn_up(xc, p["g1"], w["wup1"], l)
        if "wd1" not in w:
            w.update(gather_unit_weights([sv["s1"]]))
        xc = ffn_down(sv["s1"], w["wd1"], xc, l, gather_unit_arrive([sv["s1"]]))
        sv["x2"] = xc
        w.update(gather_unit_weights([xc]))
        sv["h2"], sv["proj"] = mix_in(xc, p["gm"], w["win"], l)
        sv["pm"] = pool_fwd(sv["proj"], p["pw"], p["pb"], p["ps"], l)
        sv["hl"], sv["hs"] = lru_fwd(sv["proj"], w["cw"], p["cb"], p["wa"], p["ba"], p["wx"], p["bx"], p["lam"], P, l)
        xc, sv["yp"], sv["yl"], sv["z"] = mix_out(sv["pm"], sv["hl"], sv["proj"], xc, w["wpu"], w["wlu"], w["wout"], P, l,
                                                  gather_unit_arrive([sv["hl"]]))
        sv["x3"] = xc
        w.update(gather_unit_weights([xc]))
        sv["h3"], sv["u3"], sv["s3"] = ffn_up(xc, p["g2"], w["wup2"], l)
        xc = ffn_down(sv["s3"], w["wd2"], xc, l, gather_unit_arrive([sv["s3"]]))
        saved.append(sv)
        LW.append(w)

    loss_part, dx, d_final = loss_head(xc, final_norm.reshape(1, D), tgt)
    loss = lax.psum(loss_part[0, 0], ("x", "y", "c"))

    G = [dict() for _ in range(L)]
    small = {n: [None] * L for n in SMALL if n != "final_norm"}

    def to_slots(name, pair):
        if name == "w_pool_up":
            return tuple(a.reshape(1, P, N_DEV, D // N_DEV).transpose(0, 2, 1, 3) for a in pair)
        return tuple(a.reshape((1, N_DEV) + W[name].shape[1:]) for a in pair)

    def ffn_bwd(dy, sv, tag, wup, wd, gn, up_name, dn_name, norm_name, l, k, deps=()):
        dout, du = ffn_down_bwd(dy, wd, sv["u" + tag], l, deps)
        du = du.reshape(N_DEV, T, cs)
        G[l][dn_name] = to_slots(dn_name, dw_tn("dw_down", sv["s" + tag], lambda tk: pl.BlockSpec((1, tk, cs), lambda g, k: (g, k, 0)),
                                                dout, lambda tk: pl.BlockSpec((tk, D), lambda g, k: (k, 0)), 4, cs, D, T))
        G[l][up_name] = to_slots(up_name, dw_tn("dw_up", du, lambda tk: pl.BlockSpec((1, tk, cs), lambda g, k: (g, k, 0)),
                                                sv["h" + tag], lambda tk: pl.BlockSpec((tk, D), lambda g, k: (k, 0)), N_DEV, cs, D, T))
        dxn, dg = dx_norm_bwd("ffn_dx", du, lambda tm: pl.BlockSpec((1, tm, cs), lambda j, i: (j, i, 0)), wup, N_DEV,
                              sv["x" + tag], gn, dy, l, w_transposed=True, deps=boundary(l, k, du))
        small[norm_name][l] = dg.reshape(D)
        return dxn

    pairing, in_flight = [], []

    def reduce_start(l, names, tag):
        names = [n for n in names if n != "conv_w"]
        send_sems, recv_sems, bufs, tok = pair_exchange_start(f"rs_pair_start_{tag}", [G[l][n][1] for n in names], queued["pair"])
        pairing.append((l, names, tag, send_sems, recv_sems, bufs))
        queued["pair"] = (tok,)
        return (tok,)

    def reduce_continue(after):
        l, names, tag, send_sems, recv_sems, bufs = pairing.pop(0)
        recv1 = pair_exchange_wait(f"rs_pair_wait_{tag}", bufs, send_sems, recv_sems, after)
        pair16 = pair_sum([G[l][n][0] for n in names], recv1, place)
        send_sems, recv_sems, bufs, tok = chip_exchange_start(f"rs_chip_start_{tag}", pair16, queued["chip"])
        in_flight.append((l, names, tag, send_sems, recv_sems, bufs, recv1))
        queued["chip"] = (tok,)
        return (tok,)

    def boundary(l, k, dx_now):
        deps = reduce_continue([dx_now]) if pairing else ()
        if len(units(l)) > 1:
            deps += reduce_start(l, units(l)[k], f"l{l}_u{k}")
        elif k == 0:
            deps += reduce_start(l, units(l)[0], f"l{l}_u0")
        return deps

    deps = ()
    for l in reversed(range(L)):
        sv, w = saved[l], LW[l]
        dx = ffn_bwd(dx, sv, "3", w["wup2"], w["wd2"], p["g2"], "ffn2_w_up", "ffn2_w_down", "norm_ffn2", l, 2, deps)
        deps = ()
        dyb, dyp, dyl, dgp, dgl, dpm, dhl = mix_out_bwd(dx, sv["proj"], sv["yp"], sv["yl"], w["wpu"], w["wlu"], w["wout"], P, R, l, deps)
        row = lambda wd_: (lambda tk: pl.BlockSpec((tk, wd_), lambda g, k: (k, 0)))
        G[l]["w_out"] = to_slots("w_out", dw_tn("dw_out", sv["z"], row(D), dyb, row(D), 1, D, D, T))
        G[l]["w_lru_up"] = to_slots("w_lru_up", dw_tn("dw_lru_up", sv["hl"], row(R), dyl, row(D), 1, R, D, T))
        G[l]["w_pool_up"] = to_slots("w_pool_up", dw_tn("dw_pool_up", sv["pm"], row(P), dyp, row(D), 1, P, D, T))
        du_lru, du_gelu, dcw, dcb, dwa, dba, dwx, dbx, dlam = lru_bwd(
            sv["proj"], sv["hs"], dhl, w["cw"], p["cb"], p["wa"], p["ba"], p["wx"], p["bx"], p["lam"], P, l)
        du_pool, dpw, dpb, dpsc = pool_bwd(sv["proj"], dpm, p["pw"], p["pb"], p["ps"], l)
        dproj = jnp.concatenate([du_pool, du_lru, du_gelu, dgp, dgl], axis=1)
        G[l]["w_in"] = to_slots("w_in", dw_tn("dw_in", sv["h2"], row(D), dproj, lambda tk: pl.BlockSpec((tk, ci), lambda g, k: (k, g)),
                                              N_DEV, D, ci, T))
        dx, dgm = dx_norm_bwd("mix_dx", dproj, lambda tm: pl.BlockSpec((tm, ci), lambda j, i: (i, j)), w["win"], N_DEV,
                              sv["x2"], p["gm"], dx, l)
        small["norm_mix"][l] = dgm.reshape(D)
        small["pool_w"][l], small["pool_b"][l], small["pool_scale"][l] = dpw[0], dpb.reshape(pool_b.shape[1:]), dpsc.reshape(P)
        small["conv_w"][l], small["conv_b"][l] = dcw[0], dcb.reshape(R)
        small["lru_w_a"][l], small["lru_b_a"][l] = dwa[0], dba.reshape(H, hd)
        small["lru_w_x"][l], small["lru_b_x"][l] = dwx[0], dbx.reshape(H, hd)
        small["lru_lambda"][l] = dlam.reshape(R)
        deps = boundary(l, 1, dx)
        dx = ffn_bwd(dx, sv, "1", w["wup1"], w["wd1"], p["g1"], "ffn1_w_up", "ffn1_w_down", "norm_ffn1", l, 0, deps)
        deps = ()

    grad_x = dx.reshape(x.shape)

    small_parts = [jnp.stack(small[n]) for n in SMALL if n != "final_norm"] + [d_final.reshape(D)]
    small_shapes = [p.shape for p in small_parts]
    gpack = _pack(small_parts).reshape(1, N_DEV, -1, 128)
    small_pair = pair_exchange_start("rs_pair_start_small", [gpack], queued["pair"])
    while pairing:
        reduce_continue([dx])

    outs = {n: None for n in BIG}

    def unit_updates(unit, recv2, deps=()):
        l, names, recv1 = unit[0], unit[1], unit[6]
        updated = grad_sum_adamw([G[l][n][0] for n in names], recv1, recv2, [W[n] for n in names], [M[n] for n in names],
                                 [V[n] for n in names], place, l, [outs[n] for n in names], deps)
        outs.update(zip(names, updated))
        return [outs[n][0] for n in names]

    after = [dx]
    late = in_flight[-2:]
    for k, unit in enumerate(in_flight[:-2]):
        recv2 = chip_exchange_wait(f"rs_chip_wait_{unit[2]}", unit[5], unit[3], unit[4], after)
        after = unit_updates(unit, recv2)
        if k == 0:
            recv1_s = pair_exchange_wait("rs_pair_wait_small", small_pair[2], small_pair[0], small_pair[1], after)[0]
            small_chip = chip_exchange_start("rs_chip_start_small", pair_sum([gpack], [recv1_s], place), queued["chip"])
    late_recv2 = []
    for unit in late:
        late_recv2.append(chip_exchange_wait(f"rs_chip_wait_{unit[2]}", unit[5], unit[3], unit[4], after))
        after = [late_recv2[-1][0]]
    recv2_s = chip_exchange_wait("rs_chip_wait_small", small_chip[2], small_chip[0], small_chip[1], after)[0]
    gs = grad_sum(gpack, recv1_s, recv2_s, place)
    gs_slots = place_own([gs.reshape((1,) + gs.shape)], [F32], 0, place)
    send_sems, recv_sems, gs_slots, tok = gather_start("gather_start_small", gs_slots)
    after = unit_updates(late[0], late_recv2[0], (tok,))
    gs_slots = gather_wait("gather_wait_small", gs_slots, send_sems, recv_sems, after)
    send_sems, recv_sems, gs_slots, tok = gather_forward_start("gather_pass_start_small", gs_slots)
    after = unit_updates(late[1], late_recv2[1], (tok,))
    gs_all = gather_forward_wait("gather_pass_wait_small", gs_slots, send_sems, recv_sems, after)[0].reshape(-1, 128)
    for n in ("ffn1_w_up", "ffn2_w_up"):
        outs[n] = [jnp.swapaxes(o, 1, 2) for o in outs[n]]
    out_g, out_d, out_m, out_v = ({n: outs[n][k] for n in BIG} for k in range(4))

    small_g = dict(zip(SMALL, _unpack(gs_all, small_shapes)))
    for n in SMALL:
        if n != "conv_w":
            flat = lambda a: a.reshape(-1, 128)
            out_g[n] = small_g[n]
            out_d[n], out_m[n], out_v[n] = (o.reshape(W[n].shape) for o in adamw(flat(W[n]), flat(small_g[n]), flat(M[n]), flat(V[n])))
    cwc = conv_w.shape[2]
    gcw = lax.dynamic_slice_in_dim(small_g["conv_w"], dev * cwc, cwc, axis=2)
    cw2 = lambda a: a.reshape(-1, cwc)
    pad_rows = (-cw2(conv_w).shape[0]) % 8
    padr = lambda a: jnp.pad(cw2(a), ((0, pad_rows), (0, 0)))
    dcw_, mcw_, vcw_ = adamw(padr(conv_w), padr(gcw), padr(M["conv_w"]), padr(V["conv_w"]))
    nrow = cw2(conv_w).shape[0]
    out_g["conv_w"] = gcw
    out_d["conv_w"], out_m["conv_w"], out_v["conv_w"] = (a[:nrow].reshape(conv_w.shape) for a in (dcw_, mcw_, vcw_))

    return (loss, grad_x, *[out_g[n] for n in NAMES], *[out_d[n] for n in NAMES], *[out_m[n] for n in NAMES], *[out_v[n] for n in NAMES])
```

```python
import jax
import jax.numpy as jnp
from jax import lax
from jax.experimental import pallas as pl
from jax.experimental.pallas import tpu as pltpu

F32, BF16 = jnp.float32, jnp.bfloat16
EPS = 1e-6
LRU_C = 8.0
POOL_WINDOWS = (2, 4, 8, 16)
ADAM_LR, ADAM_B1, ADAM_B2, ADAM_EPS, ADAM_WD, ADAM_STEP = 0.001, 0.9, 0.999, 1e-08, 0.01, 10
N_DEV = 8
N_CHIP = 4
MESH = pl.DeviceIdType.MESH
V7X_VMEM_LIMIT = 56 * 1024 * 1024
ROW_TILE = 512
WIDE_TILE = 1024
SUM_TILE = 2048
LRU_CHUNK = 32
ANY = pl.BlockSpec(memory_space=pl.ANY)

_pallas_call = pl.pallas_call


def _cp(*sem):
    return pltpu.CompilerParams(dimension_semantics=sem if sem else None, vmem_limit_bytes=V7X_VMEM_LIMIT)


def _tile(n, t):
    t = min(n, t)
    assert n % t == 0, (n, t)
    return t


def _dot(a, b):
    return jnp.dot(a, b, preferred_element_type=F32)


def _dot_nt(a, b):
    return lax.dot_general(a, b, (((1,), (1,)), ((), ())), preferred_element_type=F32)


def _dot_tn(a, b):
    return lax.dot_general(a, b, (((0,), (0,)), ((), ())), preferred_element_type=F32)


def _rms(xv):
    r = lax.rsqrt(jnp.mean(xv * xv, axis=-1, keepdims=True) + EPS)
    return xv * r, r


def _rms_bwd(dh, xv, gv, dy):
    n, r = _rms(xv)
    dn = dh * gv
    dx = dy + r * (dn - n * jnp.mean(dn * n, axis=-1, keepdims=True))
    return dx, jnp.sum(dh * n, axis=0, keepdims=True)


def _shift_down(x, k, fill=0.0):
    if k == 0:
        return x
    rows = lax.broadcasted_iota(jnp.int32, x.shape, 0)
    return jnp.where(rows >= k, pltpu.roll(x, k, 0), fill)


def _shift_up(x, k, fill=0.0):
    if k == 0:
        return x
    n = x.shape[0]
    rows = lax.broadcasted_iota(jnp.int32, x.shape, 0)
    return jnp.where(rows < n - k, pltpu.roll(x, n - k, 0), fill)


def _sigmoid(x):
    return 0.5 * jnp.tanh(0.5 * x) + 0.5


_GELU_K = 0.7978845608028654
_GELU_C = 0.044715


def _gelu(x):
    th = jnp.tanh(_GELU_K * (x + _GELU_C * x * x * x))
    return 0.5 * x * (1.0 + th), th


def _gelu_grad(x, th):
    return 0.5 * (1.0 + th) + 0.5 * x * (1.0 - th * th) * _GELU_K * (1.0 + 3.0 * _GELU_C * x * x)


def ffn_up(x, g, wup, l):
    T, D = x.shape
    cs = wup.shape[-2]
    tm = _tile(T, WIDE_TILE)
    ni = T // tm

    def body(x_ref, g_ref, wa_ref, wb_ref, h_ref, u_ref, s_ref, hs_ref):
        rows = pl.ds(pl.multiple_of(pl.program_id(1) * tm, tm), tm)

        @pl.when(pl.program_id(0) == 0)
        def _():
            n, _r = _rms(x_ref[...])
            hv = (n * g_ref[0]).astype(BF16)
            hs_ref[rows, :] = hv
            h_ref[...] = hv

        hv = hs_ref[rows, :]
        a = _dot_nt(hv, wa_ref[0, 0])
        b = _dot_nt(hv, wb_ref[0, 0])
        u_ref[0, 0] = a.astype(BF16)
        u_ref[1, 0] = b.astype(BF16)
        s_ref[0] = (a * _sigmoid(a) * b).astype(BF16)

    first = lambda j, i: (jnp.where(j == 0, i, ni - 1), 0)
    return _pallas_call(
        body, name="ffn_up", grid=(4, ni),
        in_specs=[pl.BlockSpec((tm, D), first), pl.BlockSpec((1, 1, D), lambda j, i: (l, 0, 0)),
                  pl.BlockSpec((1, 1, cs, D), lambda j, i: (0, j, 0, 0)), pl.BlockSpec((1, 1, cs, D), lambda j, i: (0, j + 4, 0, 0))],
        out_specs=[pl.BlockSpec((tm, D), first), pl.BlockSpec((2, 1, tm, cs), lambda j, i: (0, j, i, 0)),
                   pl.BlockSpec((1, tm, cs), lambda j, i: (j, i, 0))],
        out_shape=[jax.ShapeDtypeStruct((T, D), BF16), jax.ShapeDtypeStruct((2, 4, T, cs), BF16), jax.ShapeDtypeStruct((4, T, cs), BF16)],
        scratch_shapes=[pltpu.VMEM((T, D), BF16)],
        compiler_params=_cp("arbitrary", "arbitrary"),
    )(x, g, wup, wup)


def ffn_down(s, wd, x, l, deps=()):
    ng, T, cs = s.shape
    D = x.shape[1]
    tm = _tile(T, WIDE_TILE)

    def body(s_ref, w_ref, x_ref, *rest):
        o_ref = rest[len(deps)]
        acc = _dot(s_ref[0], w_ref[0, 0:cs, :])
        for j in range(1, ng):
            acc = acc + _dot(s_ref[j], w_ref[0, j * cs:(j + 1) * cs, :])
        o_ref[...] = x_ref[...] + 0.5 * acc

    return _pallas_call(
        body, name="ffn_down", grid=(T // tm,),
        in_specs=[pl.BlockSpec((ng, tm, cs), lambda i: (0, i, 0)), pl.BlockSpec((1, ng * cs, D), lambda i: (0, 0, 0)),
                  pl.BlockSpec((tm, D), lambda i: (i, 0))] + [ANY] * len(deps),
        out_specs=pl.BlockSpec((tm, D), lambda i: (i, 0)),
        out_shape=jax.ShapeDtypeStruct((T, D), F32),
        compiler_params=_cp("parallel"),
    )(s, wd, x, *deps)


def mix_in(x, g, win, l):
    T, D = x.shape
    ci = win.shape[-1]
    tm = _tile(T, WIDE_TILE)
    ni = T // tm

    def body(x_ref, g_ref, w_ref, h_ref, p_ref, hs_ref):
        rows = pl.ds(pl.multiple_of(pl.program_id(1) * tm, tm), tm)

        @pl.when(pl.program_id(0) == 0)
        def _():
            n, _r = _rms(x_ref[...])
            hv = (n * g_ref[0]).astype(BF16)
            hs_ref[rows, :] = hv
            h_ref[...] = hv

        p_ref[...] = _dot(hs_ref[rows, :], w_ref[0, 0]).astype(BF16)

    first = lambda j, i: (jnp.where(j == 0, i, ni - 1), 0)
    return _pallas_call(
        body, name="mix_in", grid=(N_DEV, ni),
        in_specs=[pl.BlockSpec((tm, D), first), pl.BlockSpec((1, 1, D), lambda j, i: (l, 0, 0)),
                  pl.BlockSpec((1, 1, D, ci), lambda j, i: (0, j, 0, 0))],
        out_specs=[pl.BlockSpec((tm, D), first), pl.BlockSpec((tm, ci), lambda j, i: (i, j))],
        out_shape=[jax.ShapeDtypeStruct((T, D), BF16), jax.ShapeDtypeStruct((T, N_DEV * ci), BF16)],
        scratch_shapes=[pltpu.VMEM((T, D), BF16)],
        compiler_params=_cp("arbitrary", "arbitrary"),
    )(x, g, win)


def _inv_count(T, w):
    t = lax.broadcasted_iota(jnp.int32, (T, 1), 0)
    return 1.0 / jnp.minimum(t + 1, w).astype(F32)


def _pooled(ug, w, inv):
    s = ug
    k = 1
    while k < w:
        s = s + _shift_down(s, k)
        k *= 2
    return s * inv - ug


def pool_fwd(proj, pw, pb, ps, l):
    T = proj.shape[0]
    _, G, gd, _ = pw.shape
    P = G * gd

    def body(u_ref, w_ref, b_ref, s_ref, o_ref):
        for gi in range(G):
            cols = slice(gi * gd, (gi + 1) * gd)
            ug = u_ref[:, cols].astype(F32)
            pooled = _pooled(ug, POOL_WINDOWS[gi], _inv_count(T, POOL_WINDOWS[gi]))
            mixed = _dot(pooled.astype(BF16), w_ref[0, gi].astype(BF16)) + b_ref[0, :, cols]
            o_ref[:, cols] = (mixed * s_ref[0, :, cols]).astype(BF16)

    return _pallas_call(
        body, name="pool_fwd", grid=(1,),
        in_specs=[pl.BlockSpec((T, P), lambda i: (0, 0)), pl.BlockSpec((1, G, gd, gd), lambda i: (l, 0, 0, 0)),
                  pl.BlockSpec((1, 1, P), lambda i: (l, 0, 0)), pl.BlockSpec((1, 1, P), lambda i: (l, 0, 0))],
        out_specs=pl.BlockSpec((T, P), lambda i: (0, 0)),
        out_shape=jax.ShapeDtypeStruct((T, P), BF16),
        compiler_params=_cp("arbitrary"),
    )(proj, pw, pb, ps)


def _conv(u, cw_ref, cb):
    CW = cw_ref.shape[1]
    v = cb
    for k in range(CW):
        v = v + cw_ref[0, k:k + 1, :] * _shift_down(u, CW - 1 - k)
    return v


def _softplus(z):
    return jnp.maximum(z, 0.0) + jnp.log1p(jnp.exp(-jnp.abs(z)))


def _lru_gates(v, wa_ref, ba, wx_ref, bx, lam):
    vb = v.astype(BF16)
    r = _sigmoid(_dot(vb, wa_ref[0, 0].astype(BF16)) + ba)
    i = _sigmoid(_dot(vb, wx_ref[0, 0].astype(BF16)) + bx)
    sp = _softplus(-lam)
    log_a = -LRU_C * r * sp
    a = jnp.exp(log_a)
    m2 = -jnp.tanh(log_a) * (a * a + 1.0)
    inv_mult = lax.rsqrt(m2)
    mult = jnp.where(m2 > 0.0, m2 * inv_mult, 0.0)
    return r, i, sp, a, mult, inv_mult


def _scan_fwd(a_ref, b_ref, o_ref):
    T, W = a_ref.shape
    rows = lax.broadcasted_iota(jnp.int32, (8, W), 0)

    def step(t, carry):
        r0 = pl.multiple_of(t * 8, 8)
        A = a_ref[pl.ds(r0, 8), :]
        B = b_ref[pl.ds(r0, 8), :]
        for s in (1, 2, 4):
            keep = rows >= s
            As = jnp.where(keep, pltpu.roll(A, s, 0), 1.0)
            Bs = jnp.where(keep, pltpu.roll(B, s, 0), 0.0)
            B = A * Bs + B
            A = A * As
        h = B + A * carry
        o_ref[pl.ds(r0, 8), :] = h
        return jnp.broadcast_to(h[7:8, :], (8, W))

    lax.fori_loop(0, T // 8, step, jnp.zeros((8, W), F32), unroll=8)


def _scan_bwd(a_ref, b_ref, o_ref):
    T, W = a_ref.shape
    rows = lax.broadcasted_iota(jnp.int32, (8, W), 0)
    nt = T // 8

    def step(t, carry):
        r0 = pl.multiple_of((nt - 1 - t) * 8, 8)
        A = a_ref[pl.ds(r0, 8), :]
        B = b_ref[pl.ds(r0, 8), :]
        for s in (1, 2, 4):
            keep = rows < 8 - s
            As = jnp.where(keep, pltpu.roll(A, 8 - s, 0), 1.0)
            Bs = jnp.where(keep, pltpu.roll(B, 8 - s, 0), 0.0)
            B = A * Bs + B
            A = A * As
        y = B + A * carry
        o_ref[pl.ds(r0, 8), :] = y
        return jnp.broadcast_to(y[0:1, :], (8, W))

    lax.fori_loop(0, nt, step, jnp.zeros((8, W), F32), unroll=8)


def _lru_specs(T, hd, P, R, CW, l):
    ob, gb = P // hd, (P + R) // hd
    vec = pl.BlockSpec((1, 1, hd), lambda h: (l, 0, h))
    mat = pl.BlockSpec((1, 1, hd, hd), lambda h: (l, h, 0, 0))
    return [pl.BlockSpec((T, hd), lambda h: (0, ob + h)), pl.BlockSpec((T, hd), lambda h: (0, gb + h)),
            pl.BlockSpec((1, CW, hd), lambda h: (0, 0, h)), vec, mat, vec, mat, vec, vec]


def lru_fwd(proj, cw, cb, wa, ba, wx, bx, lam, P, l):
    T = proj.shape[0]
    _, H, hd, _ = wa.shape
    R = H * hd
    CW = cw.shape[1]
    assert P % hd == 0 and T % 8 == 0

    def body(u_ref, ug_ref, cw_ref, cb_ref, wa_ref, ba_ref, wx_ref, bx_ref, lam_ref, hl_ref, hs_ref, a_s, b_s):
        v = _conv(u_ref[...].astype(F32), cw_ref, cb_ref[0])
        _r, i, _sp, a, mult, _im = _lru_gates(v, wa_ref, ba_ref[0], wx_ref, bx_ref[0], lam_ref[0])
        a_s[...] = a
        b_s[...] = mult * (i * v)
        _scan_fwd(a_s, b_s, hs_ref)
        ge, _th = _gelu(ug_ref[...].astype(F32))
        hl_ref[...] = (hs_ref[...] * ge).astype(BF16)

    out = pl.BlockSpec((T, hd), lambda h: (0, h))
    return _pallas_call(
        body, name="lru_fwd", grid=(H,),
        in_specs=_lru_specs(T, hd, P, R, CW, l),
        out_specs=[out, out],
        out_shape=[jax.ShapeDtypeStruct((T, R), BF16), jax.ShapeDtypeStruct((T, R), F32)],
        scratch_shapes=[pltpu.VMEM((T, hd), F32)] * 2,
        compiler_params=_cp("parallel"),
    )(proj, proj, cw, cb, wa, ba, wx, bx, lam)


def mix_out(pm, hl, proj, x, wpu, wlu, wout, P, l, deps=()):
    T, D = x.shape
    R = hl.shape[1]
    tm = _tile(T, ROW_TILE)
    assert (P + 2 * R) % D == 0
    gb = (P + 2 * R) // D

    def body(pm_ref, hl_ref, gp_ref, gl_ref, x_ref, wpu_ref, wlu_ref, wo_ref, *rest):
        o_ref, yp_ref, yl_ref, z_ref = rest[len(deps):]
        yp = _dot(pm_ref[...], wpu_ref[0])
        yl = _dot(hl_ref[...], wlu_ref[0])
        z = (_sigmoid(gp_ref[...].astype(F32)) * yp + _sigmoid(gl_ref[...].astype(F32)) * yl).astype(BF16)
        yp_ref[...] = yp.astype(BF16)
        yl_ref[...] = yl.astype(BF16)
        z_ref[...] = z
        o_ref[...] = x_ref[...] + _dot(z, wo_ref[0])

    row = lambda w: pl.BlockSpec((tm, w), lambda i: (i, 0))
    return _pallas_call(
        body, name="mix_out", grid=(T // tm,),
        in_specs=[row(P), row(R), pl.BlockSpec((tm, D), lambda i: (i, gb)), pl.BlockSpec((tm, D), lambda i: (i, gb + 1)), row(D),
                  pl.BlockSpec((1, P, D), lambda i: (0, 0, 0)), pl.BlockSpec((1, R, D), lambda i: (0, 0, 0)),
                  pl.BlockSpec((1, D, D), lambda i: (0, 0, 0))] + [ANY] * len(deps),
        out_specs=[row(D)] * 4,
        out_shape=[jax.ShapeDtypeStruct((T, D), F32)] + [jax.ShapeDtypeStruct((T, D), BF16)] * 3,
        compiler_params=_cp("parallel"),
    )(pm, hl, proj, proj, x, wpu, wlu, wout, *deps)


def loss_head(x, gf, tgt):
    T, D = x.shape
    tm = _tile(T, ROW_TILE)

    def body(x_ref, g_ref, t_ref, loss_ref, dx_ref, dg_ref):
        @pl.when(pl.program_id(0) == 0)
        def _():
            loss_ref[...] = jnp.zeros_like(loss_ref)
            dg_ref[...] = jnp.zeros_like(dg_ref)

        xv = x_ref[...]
        gv = g_ref[...]
        n, _r = _rms(xv)
        e = n * gv - t_ref[...]
        loss_ref[...] += 0.5 * jnp.sum(jnp.sum(e * e, axis=-1, keepdims=True), axis=0, keepdims=True) / D
        dx, dg = _rms_bwd(e * (1.0 / D), xv, gv, 0.0)
        dx_ref[...] = dx
        dg_ref[...] += dg

    return _pallas_call(
        body, name="loss_head", grid=(T // tm,),
        in_specs=[pl.BlockSpec((tm, D), lambda i: (i, 0)), pl.BlockSpec((1, D), lambda i: (0, 0)), pl.BlockSpec((tm, D), lambda i: (i, 0))],
        out_specs=[pl.BlockSpec((1, 1), lambda i: (0, 0)), pl.BlockSpec((tm, D), lambda i: (i, 0)), pl.BlockSpec((1, D), lambda i: (0, 0))],
        out_shape=[jax.ShapeDtypeStruct((1, 1), F32), jax.ShapeDtypeStruct((T, D), F32), jax.ShapeDtypeStruct((1, D), F32)],
        compiler_params=_cp("arbitrary"),
    )(x, gf, tgt)


def ffn_down_bwd(dy, wd, u, l, deps=()):
    T, D = dy.shape
    cs = u.shape[-1]
    tm = _tile(T, WIDE_TILE)
    ni = T // tm

    def body(dy_ref, w_ref, u_ref, *rest):
        do_ref, du_ref, dyb_ref = rest[len(deps):]
        rows = pl.ds(pl.multiple_of(pl.program_id(1) * tm, tm), tm)

        @pl.when(pl.program_id(0) == 0)
        def _():
            d = (0.5 * dy_ref[...]).astype(BF16)
            dyb_ref[rows, :] = d
            do_ref[...] = d

        ds = _dot_nt(dyb_ref[rows, :], w_ref[0])
        a = u_ref[0, 0].astype(F32)
        b = u_ref[1, 0].astype(F32)
        sg = _sigmoid(a)
        du_ref[0, 0] = (ds * b * (sg * (1.0 + a * (1.0 - sg)))).astype(BF16)
        du_ref[1, 0] = (ds * (a * sg)).astype(BF16)

    first = lambda j, i: (jnp.where(j == 0, i, ni - 1), 0)
    blk = pl.BlockSpec((2, 1, tm, cs), lambda j, i: (0, j, i, 0))
    return _pallas_call(
        body, name="ffn_down_bwd", grid=(4, ni),
        in_specs=[pl.BlockSpec((tm, D), first), pl.BlockSpec((1, cs, D), lambda j, i: (0, j, 0)), blk] + [ANY] * len(deps),
        out_specs=[pl.BlockSpec((tm, D), first), blk],
        out_shape=[jax.ShapeDtypeStruct((T, D), BF16), jax.ShapeDtypeStruct((2, 4, T, cs), BF16)],
        scratch_shapes=[pltpu.VMEM((T, D), BF16)],
        compiler_params=_cp("arbitrary", "arbitrary"),
    )(dy, wd, u, *deps)


def dw_tn(name, a, a_spec, b, b_spec, G, M, N, T):
    tk = _tile(T, SUM_TILE)
    nk = T // tk

    def body(a_ref, b_ref, o32_ref, o16_ref, acc_ref):
        k = pl.program_id(1)

        @pl.when(k == 0)
        def _():
            acc_ref[...] = jnp.zeros_like(acc_ref)

        av = a_ref[0] if len(a_ref.shape) == 3 else a_ref[...]
        bv = b_ref[0] if len(b_ref.shape) == 3 else b_ref[...]
        acc_ref[...] += _dot_tn(av, bv)

        @pl.when(k == nk - 1)
        def _():
            o32_ref[0, 0] = acc_ref[...]
            o16_ref[0, 0] = acc_ref[...].astype(BF16)

    out = pl.BlockSpec((1, 1, M, N), lambda g, k: (0, g, 0, 0))
    return _pallas_call(
        body, name=name, grid=(G, nk),
        in_specs=[a_spec(tk), b_spec(tk)], out_specs=[out, out],
        out_shape=[jax.ShapeDtypeStruct((1, G, M, N), F32), jax.ShapeDtypeStruct((1, G, M, N), BF16)],
        scratch_shapes=[pltpu.VMEM((M, N), F32)],
        compiler_params=_cp("parallel", "arbitrary"),
    )(a, b)


def dx_norm_bwd(name, dact, d_spec, w, G, x, g, dy, l, w_transposed=False):
    T, D = x.shape
    wblk = w.shape[-2:]
    tm = _tile(T, WIDE_TILE)
    ni = T // tm
    ch = _tile(tm, ROW_TILE // 2)

    def body(d_ref, w_ref, x_ref, g_ref, dy_ref, dx_ref, dg_ref, acc_ref):
        j, i = pl.program_id(0), pl.program_id(1)
        rows = pl.ds(pl.multiple_of(i * tm, tm), tm)

        @pl.when(jnp.logical_and(i == 0, j == 0))
        def _():
            dg_ref[...] = jnp.zeros_like(dg_ref)

        @pl.when(j == 0)
        def _():
            acc_ref[rows, :] = jnp.zeros((tm, D), F32)

        dv = d_ref[0] if len(d_ref.shape) == 3 else d_ref[...]
        acc_ref[rows, :] += _dot(dv, w_ref[0, 0]) if w_transposed else _dot_nt(dv, w_ref[0, 0])

        @pl.when(j == G - 1)
        def _():
            dg = jnp.zeros((1, D), F32)
            for c0 in range(0, tm, ch):
                part_rows = pl.ds(pl.multiple_of(i * tm + c0, ch), ch)
                dx, dgc = _rms_bwd(acc_ref[part_rows, :], x_ref[c0:c0 + ch, :], g_ref[0], dy_ref[c0:c0 + ch, :])
                dx_ref[c0:c0 + ch, :] = dx
                dg = dg + dgc
            dg_ref[...] += dg

    last = pl.BlockSpec((tm, D), lambda j, i: (jnp.where(j == G - 1, i, 0), 0))
    return _pallas_call(
        body, name=name, grid=(G, ni),
        in_specs=[d_spec(tm), pl.BlockSpec((1, 1) + wblk, lambda j, i: (0, j, 0, 0)), last, pl.BlockSpec((1, 1, D), lambda j, i: (l, 0, 0)), last],
        out_specs=[last, pl.BlockSpec((1, D), lambda j, i: (0, 0))],
        out_shape=[jax.ShapeDtypeStruct((T, D), F32), jax.ShapeDtypeStruct((1, D), F32)],
        scratch_shapes=[pltpu.VMEM((T, D), F32)],
        compiler_params=_cp("arbitrary", "arbitrary"),
    )(dact, w, x, g, dy)


def mix_out_bwd(dy, proj, yp, yl, wpu, wlu, wout, P, R, l, deps=()):
    T, D = dy.shape
    tm = _tile(T, ROW_TILE)
    gb = (P + 2 * R) // D

    def body(dy_ref, gp_ref, gl_ref, yp_ref, yl_ref, wpu_ref, wlu_ref, wo_ref, *rest):
        dyb_ref, dyp_ref, dyl_ref, dgp_ref, dgl_ref, dpm_ref, dhl_ref = rest[len(deps):]
        dyb = dy_ref[...].astype(BF16)
        dyb_ref[...] = dyb
        dz = _dot_nt(dyb, wo_ref[0])
        sp = _sigmoid(gp_ref[...].astype(F32))
        sl = _sigmoid(gl_ref[...].astype(F32))
        dgp_ref[...] = (dz * yp_ref[...].astype(F32) * sp * (1.0 - sp)).astype(BF16)
        dgl_ref[...] = (dz * yl_ref[...].astype(F32) * sl * (1.0 - sl)).astype(BF16)
        dyp = (dz * sp).astype(BF16)
        dyl = (dz * sl).astype(BF16)
        dyp_ref[...] = dyp
        dyl_ref[...] = dyl
        dpm_ref[...] = _dot_nt(dyp, wpu_ref[0]).astype(BF16)
        dhl_ref[...] = _dot_nt(dyl, wlu_ref[0]).astype(BF16)

    row = lambda w: pl.BlockSpec((tm, w), lambda i: (i, 0))
    return _pallas_call(
        body, name="mix_out_bwd", grid=(T // tm,),
        in_specs=[row(D), pl.BlockSpec((tm, D), lambda i: (i, gb)), pl.BlockSpec((tm, D), lambda i: (i, gb + 1)), row(D), row(D),
                  pl.BlockSpec((1, P, D), lambda i: (0, 0, 0)), pl.BlockSpec((1, R, D), lambda i: (0, 0, 0)),
                  pl.BlockSpec((1, D, D), lambda i: (0, 0, 0))] + [ANY] * len(deps),
        out_specs=[row(D)] * 5 + [row(P), row(R)],
        out_shape=[jax.ShapeDtypeStruct((T, D), BF16)] * 5 + [jax.ShapeDtypeStruct((T, P), BF16), jax.ShapeDtypeStruct((T, R), BF16)],
        compiler_params=_cp("parallel"),
    )(dy, proj, proj, yp, yl, wpu, wlu, wout, *deps)


def lru_bwd(proj, hs, dhl, cw, cb, wa, ba, wx, bx, lam, P, l):
    T = proj.shape[0]
    _, H, hd, _ = wa.shape
    R = H * hd
    CW = cw.shape[1]

    def body(u_ref, ug_ref, cw_ref, cb_ref, wa_ref, ba_ref, wx_ref, bx_ref, lam_ref, hs_ref, dhl_ref,
             du_ref, dug_ref, dcw_ref, dcb_ref, dwa_ref, dba_ref, dwx_ref, dbx_ref, dlam_ref, c_s, g_s, y_s, pre_s, post_s):
        u = u_ref[...].astype(F32)
        v = _conv(u, cw_ref, cb_ref[0])
        lam = lam_ref[0]
        r, i, sp, a, mult, inv_mult = _lru_gates(v, wa_ref, ba_ref[0], wx_ref, bx_ref[0], lam)
        ug = ug_ref[...].astype(F32)
        ge, th = _gelu(ug)
        hs = hs_ref[...]
        dhl = dhl_ref[...].astype(F32)
        dug_ref[...] = (dhl * hs * _gelu_grad(ug, th)).astype(BF16)
        for k, arr in enumerate((r, i, a, mult, inv_mult, v, _shift_down(hs, 1))):
            pre_s[k] = arr
        c_s[...] = _shift_up(a, 1)
        g_s[...] = dhl * ge
        _scan_bwd(c_s, g_s, y_s)

        def chunk(t, sums):
            rows = pl.ds(pl.multiple_of(t * LRU_CHUNK, LRU_CHUNK), LRU_CHUNK)
            rr, ii, aa, mm, im, vv, hp = (pre_s[k, rows, :] for k in range(7))
            yy = y_s[rows, :]
            dlog_a = (yy * hp) * aa - (yy * (ii * vv)) * (aa * aa) * im
            dvv = yy * mm
            dpa_c = (dlog_a * (-LRU_C) * sp) * rr * (1.0 - rr)
            dpx_c = (dvv * vv) * ii * (1.0 - ii)
            post_s[0, rows, :] = dpa_c
            post_s[1, rows, :] = dpx_c
            post_s[2, rows, :] = dvv * ii
            return sums[0] + dlog_a * rr, sums[1] + dpa_c, sums[2] + dpx_c

        zero = jnp.zeros((LRU_CHUNK, hd), F32)
        s_lr, s_pa, s_px = lax.fori_loop(0, T // LRU_CHUNK, chunk, (zero, zero, zero))
        dsp = -LRU_C * jnp.sum(s_lr, axis=0, keepdims=True)
        dlam_ref[0] = -dsp * _sigmoid(-lam)
        vb = v.astype(BF16)
        dpab, dpxb = post_s[0].astype(BF16), post_s[1].astype(BF16)
        dwa_ref[0, 0] = _dot_tn(vb, dpab)
        dwx_ref[0, 0] = _dot_tn(vb, dpxb)
        dba_ref[0] = jnp.sum(s_pa, axis=0, keepdims=True)
        dbx_ref[0] = jnp.sum(s_px, axis=0, keepdims=True)
        dv = post_s[2] + _dot_nt(dpab, wa_ref[0, 0].astype(BF16)) + _dot_nt(dpxb, wx_ref[0, 0].astype(BF16))
        dcb_ref[0] = jnp.sum(dv, axis=0, keepdims=True)
        du = jnp.zeros_like(dv)
        for k in range(CW):
            du = du + cw_ref[0, k:k + 1, :] * _shift_up(dv, CW - 1 - k)
            dcw_ref[0, k:k + 1, :] = jnp.sum(dv * _shift_down(u, CW - 1 - k), axis=0, keepdims=True)
        du_ref[...] = du.astype(BF16)

    col = pl.BlockSpec((T, hd), lambda h: (0, h))
    vec = pl.BlockSpec((1, 1, hd), lambda h: (0, 0, h))
    mat = pl.BlockSpec((1, 1, hd, hd), lambda h: (0, h, 0, 0))
    vshape = jax.ShapeDtypeStruct((1, 1, R), F32)
    mshape = jax.ShapeDtypeStruct((1, H, hd, hd), F32)
    return _pallas_call(
        body, name="lru_bwd", grid=(H,),
        in_specs=_lru_specs(T, hd, P, R, CW, l) + [col, col],
        out_specs=[col, col, pl.BlockSpec((1, CW, hd), lambda h: (0, 0, h)), vec, mat, vec, mat, vec, vec],
        out_shape=[jax.ShapeDtypeStruct((T, R), BF16)] * 2 + [jax.ShapeDtypeStruct((1, CW, R), F32), vshape, mshape, vshape, mshape, vshape, vshape],
        scratch_shapes=[pltpu.VMEM((T, hd), F32)] * 3 + [pltpu.VMEM((7, T, hd), F32), pltpu.VMEM((3, T, hd), F32)],
        compiler_params=_cp("parallel"),
    )(proj, proj, cw, cb, wa, ba, wx, bx, lam, hs, dhl)


def pool_bwd(proj, dpm, pw, pb, ps, l):
    T = proj.shape[0]
    _, G, gd, _ = pw.shape
    P = G * gd

    def body(u_ref, d_ref, w_ref, b_ref, s_ref, du_ref, dw_ref, db_ref, dsc_ref):
        for gi in range(G):
            cols = slice(gi * gd, (gi + 1) * gd)
            w = POOL_WINDOWS[gi]
            inv = _inv_count(T, w)
            ug = u_ref[:, cols].astype(F32)
            pooled = _pooled(ug, w, inv).astype(BF16)
            wb = w_ref[0, gi].astype(BF16)
            mixed = _dot(pooled, wb) + b_ref[0, :, cols]
            dpm_g = d_ref[:, cols].astype(F32)
            dsc_ref[0, :, cols] = jnp.sum(dpm_g * mixed, axis=0, keepdims=True)
            dmixed = dpm_g * s_ref[0, :, cols]
            db_ref[0, :, cols] = jnp.sum(dmixed, axis=0, keepdims=True)
            dmb = dmixed.astype(BF16)
            dw_ref[0, gi] = _dot_tn(pooled, dmb)
            dpooled = _dot_nt(dmb, wb)
            s = dpooled * inv
            k = 1
            while k < w:
                s = s + _shift_up(s, k)
                k *= 2
            du_ref[:, cols] = (s - dpooled).astype(BF16)

    vec = pl.BlockSpec((1, 1, P), lambda i: (l, 0, 0))
    ovec = pl.BlockSpec((1, 1, P), lambda i: (0, 0, 0))
    return _pallas_call(
        body, name="pool_bwd", grid=(1,),
        in_specs=[pl.BlockSpec((T, P), lambda i: (0, 0)), pl.BlockSpec((T, P), lambda i: (0, 0)),
                  pl.BlockSpec((1, G, gd, gd), lambda i: (l, 0, 0, 0)), vec, vec],
        out_specs=[pl.BlockSpec((T, P), lambda i: (0, 0)), pl.BlockSpec((1, G, gd, gd), lambda i: (0, 0, 0, 0)), ovec, ovec],
        out_shape=[jax.ShapeDtypeStruct((T, P), BF16), jax.ShapeDtypeStruct((1, G, gd, gd), F32),
                   jax.ShapeDtypeStruct((1, 1, P), F32), jax.ShapeDtypeStruct((1, 1, P), F32)],
        compiler_params=_cp("arbitrary"),
    )(proj, dpm, pw, pb, ps)


def _place():
    x, y, c = lax.axis_index("x"), lax.axis_index("y"), lax.axis_index("c")
    return x, y, c


HBM = pl.BlockSpec(memory_space=pltpu.HBM)
SEM = pl.BlockSpec(memory_space=pltpu.SEMAPHORE)
EFFECT = pltpu.SideEffectType.DATAFLOW_SIDE_EFFECTING


def _in_hbm(a):
    return pltpu.with_memory_space_constraint(a, pltpu.HBM)


def split_start(name, bufs, n_copies, copies_of, deps=()):
    nb = len(bufs)

    def body(*refs):
        buf = refs[:nb]
        send_sems, recv_sems = refs[nb + len(deps)], refs[nb + len(deps) + 1]
        token = refs[-1]
        for i, (src, dst, dev) in enumerate(copies_of(buf)):
            pltpu.make_async_remote_copy(src_ref=src, dst_ref=dst, send_sem=send_sems.at[i], recv_sem=recv_sems.at[i],
                                         device_id=dev, device_id_type=MESH).start()
        token[...] = jnp.zeros_like(token)

    outs = _pallas_call(
        body, name=name,
        in_specs=[HBM] * nb + [ANY] * len(deps),
        out_specs=(SEM, SEM, *([HBM] * nb), pl.BlockSpec(memory_space=pltpu.VMEM)),
        out_shape=(pltpu.SemaphoreType.DMA((n_copies,)), pltpu.SemaphoreType.DMA((n_copies,)),
                   *[pltpu.HBM(b.shape, b.dtype) for b in bufs], jax.ShapeDtypeStruct((8, 128), F32)),
        input_output_aliases={i: 2 + i for i in range(nb)},
        compiler_params=pltpu.CompilerParams(has_side_effects=EFFECT),
    )(*[_in_hbm(b) for b in bufs], *deps)
    return outs[0], outs[1], list(outs[2:2 + nb]), outs[-1]


def split_wait(name, bufs, send_sems, recv_sems, after, copies_of):
    nb = len(bufs)

    def body(*refs):
        buf = refs[:nb]
        send, recv = refs[nb], refs[nb + 1]
        for i, (src, dst, dev) in enumerate(copies_of(buf)):
            cp = pltpu.make_async_remote_copy(src_ref=src, dst_ref=dst, send_sem=send.at[i], recv_sem=recv.at[i],
                                              device_id=dev, device_id_type=MESH)
            cp.wait_send()
            cp.wait_recv()

    outs = _pallas_call(
        body, name=name,
        in_specs=[HBM] * nb + [SEM, SEM] + [ANY] * len(after),
        out_specs=[HBM] * nb,
        out_shape=[pltpu.HBM(b.shape, b.dtype) for b in bufs],
        input_output_aliases={i: i for i in range(nb)},
        compiler_params=pltpu.CompilerParams(has_side_effects=EFFECT),
    )(*bufs, send_sems, recv_sems, *after)
    return list(outs)


def _two_row_blocks(rows):
    return (rows // 2, 1) if rows % 32 == 0 else (rows, 0)


def place_own(ws, dtypes, l, place):
    n = len(ws)

    def body(p_ref, *refs):
        for a in range(n):
            refs[n + a][0] = refs[a][0].astype(dtypes[a])

    in_specs, out_specs, out_shape = [], [], []
    for w, dt in zip(ws, dtypes):
        _, rows, cols = w.shape
        rb, step = _two_row_blocks(rows)
        in_specs.append(pl.BlockSpec((1, rb, cols), lambda i, p, s=step: (l, i * s, 0)))
        out_specs.append(pl.BlockSpec((1, rb, cols), lambda i, p, s=step: (p[2], i * s, 0)))
        out_shape.append(jax.ShapeDtypeStruct((N_DEV, rows, cols), dt))
    return list(_pallas_call(
        body, name="place_own",
        grid_spec=pltpu.PrefetchScalarGridSpec(num_scalar_prefetch=1, grid=(2,), in_specs=in_specs, out_specs=out_specs),
        out_shape=out_shape, compiler_params=_cp("arbitrary"),
    )(place, *ws))


def _gather_copies(land):
    x, y, c = _place()
    k = 4 * x + 2 * y + c
    peers = [(x, 1 - y, c), (1 - x, y, c), (1 - x, 1 - y, c), (x, y, 1 - c)]
    return [(b.at[k], b.at[k], p) for p in peers for b in land]


def gather_start(name, land, deps=()):
    return split_start(name, land, 4 * len(land), _gather_copies, deps)


def gather_wait(name, land, send_sems, recv_sems, after):
    return split_wait(name, land, send_sems, recv_sems, after, _gather_copies)


def _forward_copies(land):
    x, y, c = _place()
    slots = [4 * px + 2 * py + c for px, py in [(x, 1 - y), (1 - x, y), (1 - x, 1 - y)]]
    return [(b.at[k], b.at[k], (x, y, 1 - c)) for k in slots for b in land]


def gather_forward_start(name, land, deps=()):
    return split_start(name, land, 3 * len(land), _forward_copies, deps)


def gather_forward_wait(name, land, send_sems, recv_sems, after):
    return split_wait(name, land, send_sems, recv_sems, after, _forward_copies)


def _chip_copies(nsrc):
    def copies(buf):
        p16, recv2 = buf[:nsrc], buf[nsrc:]
        x, y, c = _place()
        out = []
        for d in (1, 2, 3):
            px = 1 - x if d & 2 else x
            py = 1 - y if d & 1 else y
            out += [(p16[a].at[:, d - 1], recv2[a].at[:, d - 1], (px, py, c)) for a in range(nsrc)]
        return out
    return copies


def _pair_copies(nsrc):
    def copies(buf):
        g16, recv = buf[:nsrc], buf[nsrc:]
        x, y, c = _place()
        return [(g16[a].at[:, 2 * j + 1 - c], recv[a].at[:, j], (x, y, 1 - c)) for a in range(nsrc) for j in range(N_CHIP)]
    return copies


def pair_exchange_start(name, g16, deps=()):
    n = len(g16)
    land = [lax.empty((1, N_CHIP) + s.shape[2:], s.dtype) for s in g16]
    return split_start(name, list(g16) + land, N_CHIP * n, _pair_copies(n), deps)


def pair_exchange_wait(name, bufs, send_sems, recv_sems, after):
    n = len(bufs) // 2
    return split_wait(name, bufs, send_sems, recv_sems, after, _pair_copies(n))[n:]


def chip_exchange_start(name, pair16, deps=()):
    n = len(pair16)
    land = [lax.empty((s.shape[0], 3) + s.shape[2:], s.dtype) for s in pair16]
    return split_start(name, list(pair16) + land, 3 * n, _chip_copies(n), deps)


def chip_exchange_wait(name, bufs, send_sems, recv_sems, after):
    n = len(bufs) // 2
    return split_wait(name, bufs, send_sems, recv_sems, after, _chip_copies(n))[n:]


def _rows_tile(rows, cols, budget=1 << 20):
    t = rows
    while t % 2 == 0 and t * cols > budget and (t // 2) % 16 == 0:
        t //= 2
    return t


def pair_sum(g32s, recv1s, place):
    n = len(g32s)

    def body(p_ref, *refs):
        for a in range(n):
            m_ref, r_ref, o_ref = refs[a], refs[n + a], refs[2 * n + a]
            o_ref[...] = (m_ref[...] + r_ref[...].astype(F32)).astype(o_ref.dtype)

    other = lambda d, p: jnp.bitwise_xor(p[1], d + 1)
    g_specs, r_specs, o_specs, out_shape = [], [], [], []
    for r1 in recv1s:
        _, _, rows, cols = r1.shape
        rb, step = _two_row_blocks(rows)
        g_specs.append(pl.BlockSpec((1, 1, rb, cols), lambda d, i, p, s=step: (0, 2 * other(d, p) + p[0], i * s, 0)))
        r_specs.append(pl.BlockSpec((1, 1, rb, cols), lambda d, i, p, s=step: (0, other(d, p), i * s, 0)))
        o_specs.append(pl.BlockSpec((1, 1, rb, cols), lambda d, i, p, s=step: (0, d, i * s, 0)))
        out_shape.append(jax.ShapeDtypeStruct((1, N_CHIP - 1, rows, cols), r1.dtype))
    return list(_pallas_call(
        body, name="pair_sum",
        grid_spec=pltpu.PrefetchScalarGridSpec(num_scalar_prefetch=1, grid=(N_CHIP - 1, 2), in_specs=g_specs + r_specs, out_specs=o_specs),
        out_shape=out_shape, compiler_params=_cp("arbitrary", "arbitrary"),
    )(place, *g32s, *recv1s))


def _grad_in_specs(tr, cols, l):
    return ([pl.BlockSpec((1, 1, tr, cols), lambda i, p: (l, p[2], i, 0)), pl.BlockSpec((1, 1, tr, cols), lambda i, p: (0, p[1], i, 0))]
            + [pl.BlockSpec((1, 1, tr, cols), lambda i, p, d=d: (0, d, i, 0)) for d in range(3)])


def _grad_total(o32, o16, r0, r1, r2):
    return (o32[0, 0] + o16[0, 0].astype(F32)) + r0[0, 0].astype(F32) + r1[0, 0].astype(F32) + r2[0, 0].astype(F32)


def grad_sum(g32, recv1, recv2, place):
    _, _, rows, cols = recv1.shape
    tr = _rows_tile(rows, cols)

    def body(p_ref, o32, o16, r0, r1, r2, g_ref):
        g_ref[...] = _grad_total(o32, o16, r0, r1, r2)

    return _pallas_call(
        body, name="grad_sum",
        grid_spec=pltpu.PrefetchScalarGridSpec(
            num_scalar_prefetch=1, grid=(rows // tr,), in_specs=_grad_in_specs(tr, cols, 0),
            out_specs=pl.BlockSpec((tr, cols), lambda i, p: (i, 0))),
        out_shape=jax.ShapeDtypeStruct((rows, cols), F32), compiler_params=_cp("parallel"),
    )(place, g32, recv1, recv2, recv2, recv2)


def _adamw_math(w, g, m, v):
    m = ADAM_B1 * m + (1.0 - ADAM_B1) * g
    v = ADAM_B2 * v + (1.0 - ADAM_B2) * (g * g)
    m_hat = m / (1.0 - ADAM_B1 ** ADAM_STEP)
    v_hat = v / (1.0 - ADAM_B2 ** ADAM_STEP)
    delta = -ADAM_LR * (m_hat / (jnp.sqrt(v_hat) + ADAM_EPS) + ADAM_WD * w)
    return delta, m, v


UPDATE_BLOCK = 1 << 16


def _update_rows(rows, cols):
    if rows % 16:
        return rows, 1
    tiles = rows // 16
    d = max([k for k in range(1, tiles + 1) if tiles % k == 0 and 16 * k * cols <= UPDATE_BLOCK] or [1])
    return 16 * d, tiles // d


def grad_sum_adamw(g32s, recv1s, recv2s, ws, ms, vs, place, l, prevs, deps=()):
    n = len(ws)
    blocks = [_update_rows(w.shape[1], w.shape[2]) for w in ws]
    have_prev = prevs[0] is not None

    def body(p_ref, *refs):
        outs = refs[len(refs) - 4 * n:]

        def update(a):
            o32, o16, r0, r1, r2, w_ref, m_ref, v_ref = refs[8 * a:8 * a + 8]
            g_ref, d_ref, nm_ref, nv_ref = outs[4 * a:4 * a + 4]

            @pl.when(pl.program_id(0) < blocks[a][1])
            def _():
                g = _grad_total(o32, o16, r0, r1, r2)
                d, nm, nv = _adamw_math(w_ref[0], g, m_ref[0], v_ref[0])
                g_ref[0] = g
                d_ref[0] = d
                nm_ref[0] = nm
                nv_ref[0] = nv

        for a in range(n):
            update(a)

    args, in_specs, out_specs, out_shape = [], [], [], []
    for a in range(n):
        L, rows, cols = ws[a].shape
        rb, nb = blocks[a]
        at = lambda i, nb=nb: jnp.minimum(i, nb - 1)
        slot = lambda which: pl.BlockSpec((1, 1, rb, cols), lambda i, p, at=at: (0, which(p), at(i), 0))
        blk = pl.BlockSpec((1, rb, cols), lambda i, p, at=at: (l, at(i), 0))
        args += [g32s[a], recv1s[a], recv2s[a], recv2s[a], recv2s[a], ws[a], ms[a], vs[a]]
        in_specs += [slot(lambda p: p[2]), slot(lambda p: p[1])] + [slot(lambda p, d=d: d) for d in range(3)] + [blk] * 3
        out_specs += [blk] * 4
        out_shape += [jax.ShapeDtypeStruct((L, rows, cols), F32)] * 4
    aliases = {}
    if have_prev:
        aliases = {1 + len(args) + k: k for k in range(4 * n)}
        args += [buf for prev in prevs for buf in prev]
        in_specs += [ANY] * (4 * n)
    args += list(deps)
    in_specs += [ANY] * len(deps)
    outs = _pallas_call(
        body, name="grad_sum_adamw",
        grid_spec=pltpu.PrefetchScalarGridSpec(num_scalar_prefetch=1, grid=(max(nb for _, nb in blocks),),
                                               in_specs=in_specs, out_specs=out_specs),
        out_shape=out_shape, input_output_aliases=aliases, compiler_params=_cp("arbitrary"),
    )(place, *args)
    return [list(outs[4 * a:4 * a + 4]) for a in range(n)]


def adamw(w, g, m, v):
    rows, cols = w.shape
    tr = _rows_tile(rows, cols, 1 << 18)

    def body(w_ref, g_ref, m_ref, v_ref, d_ref, nm_ref, nv_ref):
        d, nm, nv = _adamw_math(w_ref[...], g_ref[...], m_ref[...], v_ref[...])
        d_ref[...] = d
        nm_ref[...] = nm
        nv_ref[...] = nv

    blk = pl.BlockSpec((tr, cols), lambda i: (i, 0))
    return _pallas_call(body, name="adamw_small", grid=(rows // tr,), in_specs=[blk] * 4, out_specs=[blk] * 3,
                        out_shape=[jax.ShapeDtypeStruct((rows, cols), F32)] * 3, compiler_params=_cp("parallel"))(w, g, m, v)


SMALL = ("norm_ffn1", "norm_mix", "pool_w", "pool_b", "pool_scale", "conv_w", "conv_b", "lru_w_a", "lru_b_a", "lru_w_x", "lru_b_x",
         "lru_lambda", "norm_ffn2", "final_norm")
BIG = ("ffn1_w_up", "ffn1_w_down", "w_in", "w_pool_up", "w_lru_up", "w_out", "ffn2_w_up", "ffn2_w_down")
NAMES = ("norm_ffn1", "ffn1_w_up", "ffn1_w_down", "norm_mix", "w_in", "pool_w", "pool_b", "pool_scale", "w_pool_up", "conv_w", "conv_b",
         "lru_w_a", "lru_b_a", "lru_w_x", "lru_b_x", "lru_lambda", "w_lru_up", "w_out", "norm_ffn2", "ffn2_w_up", "ffn2_w_down", "final_norm")
SUBLAYERS = (("ffn1_w_up", "ffn1_w_down"), ("w_in", "w_pool_up", "w_lru_up", "w_out", "conv_w"), ("ffn2_w_up", "ffn2_w_down"))
PACK_ROWS = 16 * N_DEV


def _pack(parts):
    flat = jnp.concatenate([p.reshape(-1) for p in parts])
    unit = 128 * PACK_ROWS
    padded = -(-flat.size // unit) * unit
    return jnp.pad(flat, (0, padded - flat.size)).reshape(-1, 128)


def _unpack(packed, shapes):
    flat = packed.reshape(-1)
    out, off = [], 0
    for s in shapes:
        n = 1
        for d in s:
            n *= d
        out.append(flat[off:off + n].reshape(s))
        off += n
    return out


def kernel(x, norm_ffn1, ffn1_w_up, ffn1_w_down, norm_mix, w_in, pool_w, pool_b, pool_scale, w_pool_up, conv_w, conv_b, lru_w_a, lru_b_a, lru_w_x, lru_b_x, lru_lambda, w_lru_up, w_out, norm_ffn2, ffn2_w_up, ffn2_w_down, final_norm, loss_target, m_norm_ffn1, m_ffn1_w_up, m_ffn1_w_down, m_norm_mix, m_w_in, m_pool_w, m_pool_b, m_pool_scale, m_w_pool_up, m_conv_w, m_conv_b, m_lru_w_a, m_lru_b_a, m_lru_w_x, m_lru_b_x, m_lru_lambda, m_w_lru_up, m_w_out, m_norm_ffn2, m_ffn2_w_up, m_ffn2_w_down, m_final_norm, v_norm_ffn1, v_ffn1_w_up, v_ffn1_w_down, v_norm_mix, v_w_in, v_pool_w, v_pool_b, v_pool_scale, v_w_pool_up, v_conv_w, v_conv_b, v_lru_w_a, v_lru_b_a, v_lru_w_x, v_lru_b_x, v_lru_lambda, v_w_lru_up, v_w_out, v_norm_ffn2, v_ffn2_w_up, v_ffn2_w_down, v_final_norm):
    W = dict(norm_ffn1=norm_ffn1, ffn1_w_up=ffn1_w_up, ffn1_w_down=ffn1_w_down, norm_mix=norm_mix, w_in=w_in, pool_w=pool_w, pool_b=pool_b,
             pool_scale=pool_scale, w_pool_up=w_pool_up, conv_w=conv_w, conv_b=conv_b, lru_w_a=lru_w_a, lru_b_a=lru_b_a, lru_w_x=lru_w_x,
             lru_b_x=lru_b_x, lru_lambda=lru_lambda, w_lru_up=w_lru_up, w_out=w_out, norm_ffn2=norm_ffn2, ffn2_w_up=ffn2_w_up,
             ffn2_w_down=ffn2_w_down, final_norm=final_norm)
    M = dict(norm_ffn1=m_norm_ffn1, ffn1_w_up=m_ffn1_w_up, ffn1_w_down=m_ffn1_w_down, norm_mix=m_norm_mix, w_in=m_w_in, pool_w=m_pool_w,
             pool_b=m_pool_b, pool_scale=m_pool_scale, w_pool_up=m_w_pool_up, conv_w=m_conv_w, conv_b=m_conv_b, lru_w_a=m_lru_w_a,
             lru_b_a=m_lru_b_a, lru_w_x=m_lru_w_x, lru_b_x=m_lru_b_x, lru_lambda=m_lru_lambda, w_lru_up=m_w_lru_up, w_out=m_w_out,
             norm_ffn2=m_norm_ffn2, ffn2_w_up=m_ffn2_w_up, ffn2_w_down=m_ffn2_w_down, final_norm=m_final_norm)
    V = dict(norm_ffn1=v_norm_ffn1, ffn1_w_up=v_ffn1_w_up, ffn1_w_down=v_ffn1_w_down, norm_mix=v_norm_mix, w_in=v_w_in, pool_w=v_pool_w,
             pool_b=v_pool_b, pool_scale=v_pool_scale, w_pool_up=v_w_pool_up, conv_w=v_conv_w, conv_b=v_conv_b, lru_w_a=v_lru_w_a,
             lru_b_a=v_lru_b_a, lru_w_x=v_lru_w_x, lru_b_x=v_lru_b_x, lru_lambda=v_lru_lambda, w_lru_up=v_w_lru_up, w_out=v_w_out,
             norm_ffn2=v_norm_ffn2, ffn2_w_up=v_ffn2_w_up, ffn2_w_down=v_ffn2_w_down, final_norm=v_final_norm)

    for S in (W, M, V):
        for n in ("ffn1_w_up", "ffn2_w_up"):
            S[n] = jnp.swapaxes(S[n], 1, 2)

    T, D = x.shape[1], x.shape[2]
    L = norm_ffn1.shape[0]
    P = pool_scale.shape[1]
    R = lru_lambda.shape[1]
    H, hd = lru_w_a.shape[1], lru_w_a.shape[2]
    CW = conv_w.shape[1]
    cs = ffn1_w_up.shape[2]
    ci = w_in.shape[2]
    xin = x.reshape(T, D)
    tgt = loss_target.reshape(T, D)
    dev = 4 * lax.axis_index("x") + 2 * lax.axis_index("y") + lax.axis_index("c")
    place = jnp.stack([lax.axis_index("c"), 2 * lax.axis_index("x") + lax.axis_index("y"), dev]).astype(jnp.int32)

    cw_flat = conv_w.reshape(L, -1)
    cw_pad = (-cw_flat.shape[1]) % 1024
    cw_tiles = jnp.pad(cw_flat, ((0, 0), (0, cw_pad))).reshape(L, -1, 128)

    def units(l):
        return SUBLAYERS if l == 0 else (tuple(n for u in SUBLAYERS for n in u),)

    queued = {"gather": (), "pair": (), "chip": ()}

    gathering = []

    def gather_units_start(l):
        for k, names in enumerate(SUBLAYERS):
            land = place_own([cw_tiles if n == "conv_w" else W[n] for n in names], [F32 if n == "conv_w" else BF16 for n in names], l, place)
            send_sems, recv_sems, land, tok = gather_start(f"gather_start_l{l}_u{k}", land, queued["gather"])
            gathering.append(dict(names=names, tag=f"l{l}_u{k}", send=send_sems, recv=recv_sems, land=land, tok=tok, arrived=False))
            queued["gather"] = (tok,)

    def gather_unit_arrive(after):
        waiting = [u for u in gathering if not u["arrived"]]
        if not waiting:
            return ()
        unit, tokens = waiting[0], [u["tok"] for u in waiting[1:]]
        land = gather_wait(f"gather_wait_{unit['tag']}", unit["land"], unit["send"], unit["recv"], list(after) + tokens)
        send_sems, recv_sems, land, tok = gather_forward_start(f"gather_pass_start_{unit['tag']}", land)
        unit.update(land=land, send=send_sems, recv=recv_sems, tok=tok, arrived=True)
        return (tok,)

    def gather_unit_weights(after):
        if not gathering[0]["arrived"]:
            gather_unit_arrive(after)
        unit = gathering.pop(0)
        land = gather_forward_wait(f"gather_pass_wait_{unit['tag']}", unit["land"], unit["send"], unit["recv"], after)
        g = dict(zip(unit["names"], land))
        one = lambda a: a.reshape((1,) + a.shape)
        w = {}
        for tag_, up, dn in (("1", "ffn1_w_up", "ffn1_w_down"), ("2", "ffn2_w_up", "ffn2_w_down")):
            if up in g:
                w["wup" + tag_], w["wd" + tag_] = one(g[up]), g[dn].reshape(1, -1, D)
        if "w_in" in g:
            cw_l = g["conv_w"].reshape(N_DEV, -1)[:, :cw_flat.shape[1]].reshape((N_DEV,) + conv_w.shape[1:])
            w.update(win=one(g["w_in"]), wlu=g["w_lru_up"].reshape(1, R, D), wout=g["w_out"].reshape(1, D, D),
                     wpu=g["w_pool_up"].transpose(1, 0, 2).reshape(1, P, D),
                     cw=cw_l.transpose(1, 0, 2).reshape(1, CW, R))
        return w

    vec = lambda a: a.reshape(L, 1, -1)
    p = dict(g1=vec(norm_ffn1), gm=vec(norm_mix), g2=vec(norm_ffn2), pb=vec(pool_b), ps=vec(pool_scale), cb=vec(conv_b),
             ba=vec(lru_b_a), bx=vec(lru_b_x), lam=vec(lru_lambda), pw=pool_w, wa=lru_w_a, wx=lru_w_x)

    AHEAD = 2
    for l in range(min(AHEAD, L)):
        gather_units_start(l)
    saved, LW = [], []
    xc = xin
    for l in range(L):
        w = gather_unit_weights([xc])
        if l + AHEAD < L:
            gather_units_start(l + AHEAD)
        sv = {"x1": xc}
        sv["h1"], sv["u1"], sv["s1"] = ffn_up(xc, p["g1"], w["wup1"], l)
        xc = ffn_down(sv["s1"], w["wd1"], xc, l, gather_unit_arrive([sv["s1"]]))
        sv["x2"] = xc
        w.update(gather_unit_weights([xc]))
        sv["h2"], sv["proj"] = mix_in(xc, p["gm"], w["win"], l)
        sv["pm"] = pool_fwd(sv["proj"], p["pw"], p["pb"], p["ps"], l)
        sv["hl"], sv["hs"] = lru_fwd(sv["proj"], w["cw"], p["cb"], p["wa"], p["ba"], p["wx"], p["bx"], p["lam"], P, l)
        xc, sv["yp"], sv["yl"], sv["z"] = mix_out(sv["pm"], sv["hl"], sv["proj"], xc, w["wpu"], w["wlu"], w["wout"], P, l,
                                                  gather_unit_arrive([sv["hl"]]))
        sv["x3"] = xc
        w.update(gather_unit_weights([xc]))
        sv["h3"], sv["u3"], sv["s3"] = ffn_up(xc, p["g2"], w["wup2"], l)
        xc = ffn_down(sv["s3"], w["wd2"], xc, l, gather_unit_arrive([sv["s3"]]))
        saved.append(sv)
        LW.append(w)

    loss_part, dx, d_final = loss_head(xc, final_norm.reshape(1, D), tgt)
    loss = lax.psum(loss_part[0, 0], ("x", "y", "c"))

    G = [dict() for _ in range(L)]
    small = {n: [None] * L for n in SMALL if n != "final_norm"}

    def to_slots(name, pair):
        if name == "w_pool_up":
            return tuple(a.reshape(1, P, N_DEV, D // N_DEV).transpose(0, 2, 1, 3) for a in pair)
        return tuple(a.reshape((1, N_DEV) + W[name].shape[1:]) for a in pair)

    def ffn_bwd(dy, sv, tag, wup, wd, gn, up_name, dn_name, norm_name, l, deps=()):
        dout, du = ffn_down_bwd(dy, wd, sv["u" + tag], l, deps)
        du = du.reshape(N_DEV, T, cs)
        G[l][dn_name] = to_slots(dn_name, dw_tn("dw_down", sv["s" + tag], lambda tk: pl.BlockSpec((1, tk, cs), lambda g, k: (g, k, 0)),
                                                dout, lambda tk: pl.BlockSpec((tk, D), lambda g, k: (k, 0)), 4, cs, D, T))
        G[l][up_name] = to_slots(up_name, dw_tn("dw_up", du, lambda tk: pl.BlockSpec((1, tk, cs), lambda g, k: (g, k, 0)),
                                                sv["h" + tag], lambda tk: pl.BlockSpec((tk, D), lambda g, k: (k, 0)), N_DEV, cs, D, T))
        dxn, dg = dx_norm_bwd("ffn_dx", du, lambda tm: pl.BlockSpec((1, tm, cs), lambda j, i: (j, i, 0)), wup, N_DEV,
                              sv["x" + tag], gn, dy, l, w_transposed=True)
        small[norm_name][l] = dg.reshape(D)
        return dxn

    pairing, in_flight = [], []

    def reduce_start(l, names, tag):
        names = [n for n in names if n != "conv_w"]
        send_sems, recv_sems, bufs, tok = pair_exchange_start(f"rs_pair_start_{tag}", [G[l][n][1] for n in names], queued["pair"])
        pairing.append((l, names, tag, send_sems, recv_sems, bufs))
        queued["pair"] = (tok,)
        return (tok,)

    def reduce_continue(after):
        l, names, tag, send_sems, recv_sems, bufs = pairing.pop(0)
        recv1 = pair_exchange_wait(f"rs_pair_wait_{tag}", bufs, send_sems, recv_sems, after)
        pair16 = pair_sum([G[l][n][0] for n in names], recv1, place)
        send_sems, recv_sems, bufs, tok = chip_exchange_start(f"rs_chip_start_{tag}", pair16, queued["chip"])
        in_flight.append((l, names, tag, send_sems, recv_sems, bufs, recv1))
        queued["chip"] = (tok,)
        return (tok,)

    def boundary(l, k, dx_now):
        deps = reduce_continue([dx_now]) if pairing else ()
        if len(units(l)) > 1:
            deps += reduce_start(l, units(l)[k], f"l{l}_u{k}")
        elif k == 0:
            deps += reduce_start(l, units(l)[0], f"l{l}_u0")
        return deps

    deps = ()
    for l in reversed(range(L)):
        sv, w = saved[l], LW[l]
        dx = ffn_bwd(dx, sv, "3", w["wup2"], w["wd2"], p["g2"], "ffn2_w_up", "ffn2_w_down", "norm_ffn2", l, deps)
        deps = boundary(l, 2, dx)
        dyb, dyp, dyl, dgp, dgl, dpm, dhl = mix_out_bwd(dx, sv["proj"], sv["yp"], sv["yl"], w["wpu"], w["wlu"], w["wout"], P, R, l, deps)
        row = lambda wd_: (lambda tk: pl.BlockSpec((tk, wd_), lambda g, k: (k, 0)))
        G[l]["w_out"] = to_slots("w_out", dw_tn("dw_out", sv["z"], row(D), dyb, row(D), 1, D, D, T))
        G[l]["w_lru_up"] = to_slots("w_lru_up", dw_tn("dw_lru_up", sv["hl"], row(R), dyl, row(D), 1, R, D, T))
        G[l]["w_pool_up"] = to_slots("w_pool_up", dw_tn("dw_pool_up", sv["pm"], row(P), dyp, row(D), 1, P, D, T))
        du_lru, du_gelu, dcw, dcb, dwa, dba, dwx, dbx, dlam = lru_bwd(
            sv["proj"], sv["hs"], dhl, w["cw"], p["cb"], p["wa"], p["ba"], p["wx"], p["bx"], p["lam"], P, l)
        du_pool, dpw, dpb, dpsc = pool_bwd(sv["proj"], dpm, p["pw"], p["pb"], p["ps"], l)
        dproj = jnp.concatenate([du_pool, du_lru, du_gelu, dgp, dgl], axis=1)
        G[l]["w_in"] = to_slots("w_in", dw_tn("dw_in", sv["h2"], row(D), dproj, lambda tk: pl.BlockSpec((tk, ci), lambda g, k: (k, g)),
                                              N_DEV, D, ci, T))
        dx, dgm = dx_norm_bwd("mix_dx", dproj, lambda tm: pl.BlockSpec((tm, ci), lambda j, i: (i, j)), w["win"], N_DEV,
                              sv["x2"], p["gm"], dx, l)
        small["norm_mix"][l] = dgm.reshape(D)
        small["pool_w"][l], small["pool_b"][l], small["pool_scale"][l] = dpw[0], dpb.reshape(pool_b.shape[1:]), dpsc.reshape(P)
        small["conv_w"][l], small["conv_b"][l] = dcw[0], dcb.reshape(R)
        small["lru_w_a"][l], small["lru_b_a"][l] = dwa[0], dba.reshape(H, hd)
        small["lru_w_x"][l], small["lru_b_x"][l] = dwx[0], dbx.reshape(H, hd)
        small["lru_lambda"][l] = dlam.reshape(R)
        deps = boundary(l, 1, dx)
        dx = ffn_bwd(dx, sv, "1", w["wup1"], w["wd1"], p["g1"], "ffn1_w_up", "ffn1_w_down", "norm_ffn1", l, deps)
        deps = boundary(l, 0, dx)

    grad_x = dx.reshape(x.shape)

    small_parts = [jnp.stack(small[n]) for n in SMALL if n != "final_norm"] + [d_final.reshape(D)]
    small_shapes = [p.shape for p in small_parts]
    gpack = _pack(small_parts).reshape(1, N_DEV, -1, 128)
    small_pair = pair_exchange_start("rs_pair_start_small", [gpack], queued["pair"])
    while pairing:
        reduce_continue([dx])

    outs = {n: None for n in BIG}

    def unit_updates(unit, recv2, deps=()):
        l, names, recv1 = unit[0], unit[1], unit[6]
        updated = grad_sum_adamw([G[l][n][0] for n in names], recv1, recv2, [W[n] for n in names], [M[n] for n in names],
                                 [V[n] for n in names], place, l, [outs[n] for n in names], deps)
        outs.update(zip(names, updated))
        return [outs[n][0] for n in names]

    after = [dx]
    late = in_flight[-2:]
    for k, unit in enumerate(in_flight[:-2]):
        recv2 = chip_exchange_wait(f"rs_chip_wait_{unit[2]}", unit[5], unit[3], unit[4], after)
        after = unit_updates(unit, recv2)
        if k == 0:
            recv1_s = pair_exchange_wait("rs_pair_wait_small", small_pair[2], small_pair[0], small_pair[1], after)[0]
            small_chip = chip_exchange_start("rs_chip_start_small", pair_sum([gpack], [recv1_s], place), queued["chip"])
    late_recv2 = []
    for unit in late:
        late_recv2.append(chip_exchange_wait(f"rs_chip_wait_{unit[2]}", unit[5], unit[3], unit[4], after))
        after = [late_recv2[-1][0]]
    recv2_s = chip_exchange_wait("rs_chip_wait_small", small_chip[2], small_chip[0], small_chip[1], after)[0]
    gs = grad_sum(gpack, recv1_s, recv2_s, place)
    gs_slots = place_own([gs.reshape((1,) + gs.shape)], [F32], 0, place)
    send_sems, recv_sems, gs_slots, tok = gather_start("gather_start_small", gs_slots)
    after = unit_updates(late[0], late_recv2[0], (tok,))
    gs_slots = gather_wait("gather_wait_small", gs_slots, send_sems, recv_sems, after)
    send_sems, recv_sems, gs_slots, tok = gather_forward_start("gather_pass_start_small", gs_slots)
    after = unit_updates(late[1], late_recv2[1], (tok,))
    gs_all = gather_forward_wait("gather_pass_wait_small", gs_slots, send_sems, recv_sems, after)[0].reshape(-1, 128)
    for n in ("ffn1_w_up", "ffn2_w_up"):
        outs[n] = [jnp.swapaxes(o, 1, 2) for o in outs[n]]
    out_g, out_d, out_m, out_v = ({n: outs[n][k] for n in BIG} for k in range(4))

    small_g = dict(zip(SMALL, _unpack(gs_all, small_shapes)))
    for n in SMALL:
        if n != "conv_w":
            flat = lambda a: a.reshape(-1, 128)
            out_g[n] = small_g[n]
            out_d[n], out_m[n], out_v[n] = (o.reshape(W[n].shape) for o in adamw(flat(W[n]), flat(small_g[n]), flat(M[n]), flat(V[n])))
    cwc = conv_w.shape[2]
    gcw = lax.dynamic_slice_in_dim(small_g["conv_w"], dev * cwc, cwc, axis=2)
    cw2 = lambda a: a.reshape(-1, cwc)
    pad_rows = (-cw2(conv_w).shape[0]) % 8
    padr = lambda a: jnp.pad(cw2(a), ((0, pad_rows), (0, 0)))
    dcw_, mcw_, vcw_ = adamw(padr(conv_w), padr(gcw), padr(M["conv_w"]), padr(V["conv_w"]))
    nrow = cw2(conv_w).shape[0]
    out_g["conv_w"] = gcw
    out_d["conv_w"], out_m["conv_w"], out_v["conv_w"] = (a[:nrow].reshape(conv_w.shape) for a in (dcw_, mcw_, vcw_))

    return (loss, grad_x, *[out_g[n] for n in NAMES], *[out_d[n] for n in NAMES], *[out_m[n] for n in NAMES], *[out_v[n] for n in NAMES])
```

```python
import jax
import jax.numpy as jnp
from jax import lax
from jax.experimental import pallas as pl
from jax.experimental.pallas import tpu as pltpu

F32, BF16 = jnp.float32, jnp.bfloat16
EPS = 1e-6
LRU_C = 8.0
POOL_WINDOWS = (2, 4, 8, 16)
ADAM_LR, ADAM_B1, ADAM_B2, ADAM_EPS, ADAM_WD, ADAM_STEP = 0.001, 0.9, 0.999, 1e-08, 0.01, 10
N_DEV = 8
N_CHIP = 4
MESH = pl.DeviceIdType.MESH
V7X_VMEM_LIMIT = 56 * 1024 * 1024
ROW_TILE = 512
WIDE_TILE = 1024
SUM_TILE = 2048
ANY = pl.BlockSpec(memory_space=pl.ANY)

_pallas_call = pl.pallas_call


def _cp(*sem):
    return pltpu.CompilerParams(dimension_semantics=sem if sem else None, vmem_limit_bytes=V7X_VMEM_LIMIT)


def _tile(n, t):
    t = min(n, t)
    assert n % t == 0, (n, t)
    return t


def _dot(a, b):
    return jnp.dot(a, b, preferred_element_type=F32)


def _dot_nt(a, b):
    return lax.dot_general(a, b, (((1,), (1,)), ((), ())), preferred_element_type=F32)


def _dot_tn(a, b):
    return lax.dot_general(a, b, (((0,), (0,)), ((), ())), preferred_element_type=F32)


def _rms(xv):
    r = lax.rsqrt(jnp.mean(xv * xv, axis=-1, keepdims=True) + EPS)
    return xv * r, r


def _rms_bwd(dh, xv, gv, dy):
    n, r = _rms(xv)
    dn = dh * gv
    dx = dy + r * (dn - n * jnp.mean(dn * n, axis=-1, keepdims=True))
    return dx, jnp.sum(dh * n, axis=0, keepdims=True)


def _shift_down(x, k, fill=0.0):
    if k == 0:
        return x
    rows = lax.broadcasted_iota(jnp.int32, x.shape, 0)
    return jnp.where(rows >= k, pltpu.roll(x, k, 0), fill)


def _shift_up(x, k, fill=0.0):
    if k == 0:
        return x
    n = x.shape[0]
    rows = lax.broadcasted_iota(jnp.int32, x.shape, 0)
    return jnp.where(rows < n - k, pltpu.roll(x, n - k, 0), fill)


def _sigmoid(x):
    return 0.5 * jnp.tanh(0.5 * x) + 0.5


_GELU_K = 0.7978845608028654
_GELU_C = 0.044715


def _gelu(x):
    th = jnp.tanh(_GELU_K * (x + _GELU_C * x * x * x))
    return 0.5 * x * (1.0 + th), th


def _gelu_grad(x, th):
    return 0.5 * (1.0 + th) + 0.5 * x * (1.0 - th * th) * _GELU_K * (1.0 + 3.0 * _GELU_C * x * x)


def ffn_up(x, g, wup, l):
    T, D = x.shape
    cs = wup.shape[-2]
    tm = _tile(T, WIDE_TILE)
    ni = T // tm

    def body(x_ref, g_ref, wa_ref, wb_ref, h_ref, u_ref, s_ref, hs_ref):
        rows = pl.ds(pl.multiple_of(pl.program_id(1) * tm, tm), tm)

        @pl.when(pl.program_id(0) == 0)
        def _():
            n, _r = _rms(x_ref[...])
            hv = (n * g_ref[0]).astype(BF16)
            hs_ref[rows, :] = hv
            h_ref[...] = hv

        hv = hs_ref[rows, :]
        a = _dot_nt(hv, wa_ref[0, 0])
        b = _dot_nt(hv, wb_ref[0, 0])
        u_ref[0, 0] = a.astype(BF16)
        u_ref[1, 0] = b.astype(BF16)
        s_ref[0] = (a * _sigmoid(a) * b).astype(BF16)

    first = lambda j, i: (jnp.where(j == 0, i, ni - 1), 0)
    return _pallas_call(
        body, name="ffn_up", grid=(4, ni),
        in_specs=[pl.BlockSpec((tm, D), first), pl.BlockSpec((1, 1, D), lambda j, i: (l, 0, 0)),
                  pl.BlockSpec((1, 1, cs, D), lambda j, i: (0, j, 0, 0)), pl.BlockSpec((1, 1, cs, D), lambda j, i: (0, j + 4, 0, 0))],
        out_specs=[pl.BlockSpec((tm, D), first), pl.BlockSpec((2, 1, tm, cs), lambda j, i: (0, j, i, 0)),
                   pl.BlockSpec((1, tm, cs), lambda j, i: (j, i, 0))],
        out_shape=[jax.ShapeDtypeStruct((T, D), BF16), jax.ShapeDtypeStruct((2, 4, T, cs), BF16), jax.ShapeDtypeStruct((4, T, cs), BF16)],
        scratch_shapes=[pltpu.VMEM((T, D), BF16)],
        compiler_params=_cp("arbitrary", "arbitrary"),
    )(x, g, wup, wup)


def ffn_down(s, wd, x, l, deps=()):
    ng, T, cs = s.shape
    D = x.shape[1]
    tm = _tile(T, WIDE_TILE)

    def body(s_ref, w_ref, x_ref, *rest):
        o_ref = rest[len(deps)]
        acc = _dot(s_ref[0], w_ref[0, 0:cs, :])
        for j in range(1, ng):
            acc = acc + _dot(s_ref[j], w_ref[0, j * cs:(j + 1) * cs, :])
        o_ref[...] = x_ref[...] + 0.5 * acc

    return _pallas_call(
        body, name="ffn_down", grid=(T // tm,),
        in_specs=[pl.BlockSpec((ng, tm, cs), lambda i: (0, i, 0)), pl.BlockSpec((1, ng * cs, D), lambda i: (0, 0, 0)),
                  pl.BlockSpec((tm, D), lambda i: (i, 0))] + [ANY] * len(deps),
        out_specs=pl.BlockSpec((tm, D), lambda i: (i, 0)),
        out_shape=jax.ShapeDtypeStruct((T, D), F32),
        compiler_params=_cp("parallel"),
    )(s, wd, x, *deps)


def mix_in(x, g, win, l):
    T, D = x.shape
    ci = win.shape[-1]
    tm = _tile(T, 2 * WIDE_TILE)
    ni = T // tm

    def body(x_ref, g_ref, w_ref, h_ref, p_ref, hs_ref):
        rows = pl.ds(pl.multiple_of(pl.program_id(1) * tm, tm), tm)

        @pl.when(pl.program_id(0) == 0)
        def _():
            n, _r = _rms(x_ref[...])
            hv = (n * g_ref[0]).astype(BF16)
            hs_ref[rows, :] = hv
            h_ref[...] = hv

        p_ref[...] = _dot(hs_ref[rows, :], w_ref[0, 0]).astype(BF16)

    first = lambda j, i: (jnp.where(j == 0, i, ni - 1), 0)
    return _pallas_call(
        body, name="mix_in", grid=(N_DEV, ni),
        in_specs=[pl.BlockSpec((tm, D), first), pl.BlockSpec((1, 1, D), lambda j, i: (l, 0, 0)),
                  pl.BlockSpec((1, 1, D, ci), lambda j, i: (0, j, 0, 0))],
        out_specs=[pl.BlockSpec((tm, D), first), pl.BlockSpec((tm, ci), lambda j, i: (i, j))],
        out_shape=[jax.ShapeDtypeStruct((T, D), BF16), jax.ShapeDtypeStruct((T, N_DEV * ci), BF16)],
        scratch_shapes=[pltpu.VMEM((T, D), BF16)],
        compiler_params=_cp("arbitrary", "arbitrary"),
    )(x, g, win)


def _inv_count(T, w):
    t = lax.broadcasted_iota(jnp.int32, (T, 1), 0)
    return 1.0 / jnp.minimum(t + 1, w).astype(F32)


def _pooled(ug, w, inv):
    s = ug
    k = 1
    while k < w:
        s = s + _shift_down(s, k)
        k *= 2
    return s * inv - ug


def pool_fwd(proj, pw, pb, ps, l):
    T = proj.shape[0]
    _, G, gd, _ = pw.shape
    P = G * gd

    def body(u_ref, w_ref, b_ref, s_ref, o_ref):
        for gi in range(G):
            cols = slice(gi * gd, (gi + 1) * gd)
            ug = u_ref[:, cols].astype(F32)
            pooled = _pooled(ug, POOL_WINDOWS[gi], _inv_count(T, POOL_WINDOWS[gi]))
            mixed = _dot(pooled.astype(BF16), w_ref[0, gi].astype(BF16)) + b_ref[0, :, cols]
            o_ref[:, cols] = (mixed * s_ref[0, :, cols]).astype(BF16)

    return _pallas_call(
        body, name="pool_fwd", grid=(1,),
        in_specs=[pl.BlockSpec((T, P), lambda i: (0, 0)), pl.BlockSpec((1, G, gd, gd), lambda i: (l, 0, 0, 0)),
                  pl.BlockSpec((1, 1, P), lambda i: (l, 0, 0)), pl.BlockSpec((1, 1, P), lambda i: (l, 0, 0))],
        out_specs=pl.BlockSpec((T, P), lambda i: (0, 0)),
        out_shape=jax.ShapeDtypeStruct((T, P), BF16),
        compiler_params=_cp("arbitrary"),
    )(proj, pw, pb, ps)


def _conv(u, cw_ref, cb):
    CW = cw_ref.shape[1]
    v = cb
    for k in range(CW):
        v = v + cw_ref[0, k:k + 1, :] * _shift_down(u, CW - 1 - k)
    return v


def _softplus(z):
    return jnp.maximum(z, 0.0) + jnp.log1p(jnp.exp(-jnp.abs(z)))


def _lru_gates(v, wa_ref, ba, wx_ref, bx, lam):
    vb = v.astype(BF16)
    r = _sigmoid(_dot(vb, wa_ref[0, 0].astype(BF16)) + ba)
    i = _sigmoid(_dot(vb, wx_ref[0, 0].astype(BF16)) + bx)
    sp = _softplus(-lam)
    log_a = -LRU_C * r * sp
    a = jnp.exp(log_a)
    m2 = -jnp.tanh(log_a) * (a * a + 1.0)
    inv_mult = lax.rsqrt(m2)
    mult = jnp.where(m2 > 0.0, m2 * inv_mult, 0.0)
    return r, i, sp, a, mult, inv_mult


def _scan_fwd(a_ref, b_ref, o_ref):
    T, W = a_ref.shape
    rows = lax.broadcasted_iota(jnp.int32, (8, W), 0)

    def step(t, carry):
        r0 = pl.multiple_of(t * 8, 8)
        A = a_ref[pl.ds(r0, 8), :]
        B = b_ref[pl.ds(r0, 8), :]
        for s in (1, 2, 4):
            keep = rows >= s
            As = jnp.where(keep, pltpu.roll(A, s, 0), 1.0)
            Bs = jnp.where(keep, pltpu.roll(B, s, 0), 0.0)
            B = A * Bs + B
            A = A * As
        h = B + A * carry
        o_ref[pl.ds(r0, 8), :] = h
        return jnp.broadcast_to(h[7:8, :], (8, W))

    lax.fori_loop(0, T // 8, step, jnp.zeros((8, W), F32), unroll=8)


def _scan_bwd(a_ref, b_ref, o_ref):
    T, W = a_ref.shape
    rows = lax.broadcasted_iota(jnp.int32, (8, W), 0)
    nt = T // 8

    def step(t, carry):
        r0 = pl.multiple_of((nt - 1 - t) * 8, 8)
        A = a_ref[pl.ds(r0, 8), :]
        B = b_ref[pl.ds(r0, 8), :]
        for s in (1, 2, 4):
            keep = rows < 8 - s
            As = jnp.where(keep, pltpu.roll(A, 8 - s, 0), 1.0)
            Bs = jnp.where(keep, pltpu.roll(B, 8 - s, 0), 0.0)
            B = A * Bs + B
            A = A * As
        y = B + A * carry
        o_ref[pl.ds(r0, 8), :] = y
        return jnp.broadcast_to(y[0:1, :], (8, W))

    lax.fori_loop(0, nt, step, jnp.zeros((8, W), F32), unroll=8)


def _lru_specs(T, hd, P, R, CW, l):
    ob, gb = P // hd, (P + R) // hd
    vec = pl.BlockSpec((1, 1, hd), lambda h: (l, 0, h))
    mat = pl.BlockSpec((1, 1, hd, hd), lambda h: (l, h, 0, 0))
    return [pl.BlockSpec((T, hd), lambda h: (0, ob + h)), pl.BlockSpec((T, hd), lambda h: (0, gb + h)),
            pl.BlockSpec((1, CW, hd), lambda h: (0, 0, h)), vec, mat, vec, mat, vec, vec]


def lru_fwd(proj, cw, cb, wa, ba, wx, bx, lam, P, l):
    T = proj.shape[0]
    _, H, hd, _ = wa.shape
    R = H * hd
    CW = cw.shape[1]
    assert P % hd == 0 and T % 8 == 0

    def body(u_ref, ug_ref, cw_ref, cb_ref, wa_ref, ba_ref, wx_ref, bx_ref, lam_ref, hl_ref, hs_ref, a_s, b_s):
        v = _conv(u_ref[...].astype(F32), cw_ref, cb_ref[0])
        _r, i, _sp, a, mult, _im = _lru_gates(v, wa_ref, ba_ref[0], wx_ref, bx_ref[0], lam_ref[0])
        a_s[...] = a
        b_s[...] = mult * (i * v)
        _scan_fwd(a_s, b_s, hs_ref)
        ge, _th = _gelu(ug_ref[...].astype(F32))
        hl_ref[...] = (hs_ref[...] * ge).astype(BF16)

    out = pl.BlockSpec((T, hd), lambda h: (0, h))
    return _pallas_call(
        body, name="lru_fwd", grid=(H,),
        in_specs=_lru_specs(T, hd, P, R, CW, l),
        out_specs=[out, out],
        out_shape=[jax.ShapeDtypeStruct((T, R), BF16), jax.ShapeDtypeStruct((T, R), F32)],
        scratch_shapes=[pltpu.VMEM((T, hd), F32)] * 2,
        compiler_params=_cp("parallel"),
    )(proj, proj, cw, cb, wa, ba, wx, bx, lam)


def mix_out(pm, hl, proj, x, wpu, wlu, wout, P, l, deps=()):
    T, D = x.shape
    R = hl.shape[1]
    tm = _tile(T, ROW_TILE)
    assert (P + 2 * R) % D == 0
    gb = (P + 2 * R) // D

    def body(pm_ref, hl_ref, gp_ref, gl_ref, x_ref, wpu_ref, wlu_ref, wo_ref, *rest):
        o_ref, yp_ref, yl_ref, z_ref = rest[len(deps):]
        yp = _dot(pm_ref[...], wpu_ref[0])
        yl = _dot(hl_ref[...], wlu_ref[0])
        z = (_sigmoid(gp_ref[...].astype(F32)) * yp + _sigmoid(gl_ref[...].astype(F32)) * yl).astype(BF16)
        yp_ref[...] = yp.astype(BF16)
        yl_ref[...] = yl.astype(BF16)
        z_ref[...] = z
        o_ref[...] = x_ref[...] + _dot(z, wo_ref[0])

    row = lambda w: pl.BlockSpec((tm, w), lambda i: (i, 0))
    return _pallas_call(
        body, name="mix_out", grid=(T // tm,),
        in_specs=[row(P), row(R), pl.BlockSpec((tm, D), lambda i: (i, gb)), pl.BlockSpec((tm, D), lambda i: (i, gb + 1)), row(D),
                  pl.BlockSpec((1, P, D), lambda i: (0, 0, 0)), pl.BlockSpec((1, R, D), lambda i: (0, 0, 0)),
                  pl.BlockSpec((1, D, D), lambda i: (0, 0, 0))] + [ANY] * len(deps),
        out_specs=[row(D)] * 4,
        out_shape=[jax.ShapeDtypeStruct((T, D), F32)] + [jax.ShapeDtypeStruct((T, D), BF16)] * 3,
        compiler_params=_cp("parallel"),
    )(pm, hl, proj, proj, x, wpu, wlu, wout, *deps)


def loss_head(x, gf, tgt):
    T, D = x.shape
    tm = _tile(T, ROW_TILE)

    def body(x_ref, g_ref, t_ref, loss_ref, dx_ref, dg_ref):
        @pl.when(pl.program_id(0) == 0)
        def _():
            loss_ref[...] = jnp.zeros_like(loss_ref)
            dg_ref[...] = jnp.zeros_like(dg_ref)

        xv = x_ref[...]
        gv = g_ref[...]
        n, _r = _rms(xv)
        e = n * gv - t_ref[...]
        loss_ref[...] += 0.5 * jnp.sum(jnp.sum(e * e, axis=-1, keepdims=True), axis=0, keepdims=True) / D
        dx, dg = _rms_bwd(e * (1.0 / D), xv, gv, 0.0)
        dx_ref[...] = dx
        dg_ref[...] += dg

    return _pallas_call(
        body, name="loss_head", grid=(T // tm,),
        in_specs=[pl.BlockSpec((tm, D), lambda i: (i, 0)), pl.BlockSpec((1, D), lambda i: (0, 0)), pl.BlockSpec((tm, D), lambda i: (i, 0))],
        out_specs=[pl.BlockSpec((1, 1), lambda i: (0, 0)), pl.BlockSpec((tm, D), lambda i: (i, 0)), pl.BlockSpec((1, D), lambda i: (0, 0))],
        out_shape=[jax.ShapeDtypeStruct((1, 1), F32), jax.ShapeDtypeStruct((T, D), F32), jax.ShapeDtypeStruct((1, D), F32)],
        compiler_params=_cp("arbitrary"),
    )(x, gf, tgt)


def ffn_down_bwd(dy, wd, u, l, deps=()):
    T, D = dy.shape
    cs = u.shape[-1]
    tm = _tile(T, WIDE_TILE)
    ni = T // tm

    def body(dy_ref, w_ref, u_ref, *rest):
        do_ref, du_ref, dyb_ref = rest[len(deps):]
        rows = pl.ds(pl.multiple_of(pl.program_id(1) * tm, tm), tm)

        @pl.when(pl.program_id(0) == 0)
        def _():
            d = (0.5 * dy_ref[...]).astype(BF16)
            dyb_ref[rows, :] = d
            do_ref[...] = d

        ds = _dot_nt(dyb_ref[rows, :], w_ref[0])
        a = u_ref[0, 0].astype(F32)
        b = u_ref[1, 0].astype(F32)
        sg = _sigmoid(a)
        du_ref[0, 0] = (ds * b * (sg * (1.0 + a * (1.0 - sg)))).astype(BF16)
        du_ref[1, 0] = (ds * (a * sg)).astype(BF16)

    first = lambda j, i: (jnp.where(j == 0, i, ni - 1), 0)
    blk = pl.BlockSpec((2, 1, tm, cs), lambda j, i: (0, j, i, 0))
    return _pallas_call(
        body, name="ffn_down_bwd", grid=(4, ni),
        in_specs=[pl.BlockSpec((tm, D), first), pl.BlockSpec((1, cs, D), lambda j, i: (0, j, 0)), blk] + [ANY] * len(deps),
        out_specs=[pl.BlockSpec((tm, D), first), blk],
        out_shape=[jax.ShapeDtypeStruct((T, D), BF16), jax.ShapeDtypeStruct((2, 4, T, cs), BF16)],
        scratch_shapes=[pltpu.VMEM((T, D), BF16)],
        compiler_params=_cp("arbitrary", "arbitrary"),
    )(dy, wd, u, *deps)


def dw_tn(name, a, a_spec, b, b_spec, G, M, N, T):
    tk = _tile(T, SUM_TILE)
    nk = T // tk

    def body(a_ref, b_ref, o32_ref, o16_ref, acc_ref):
        k = pl.program_id(1)

        @pl.when(k == 0)
        def _():
            acc_ref[...] = jnp.zeros_like(acc_ref)

        av = a_ref[0] if len(a_ref.shape) == 3 else a_ref[...]
        bv = b_ref[0] if len(b_ref.shape) == 3 else b_ref[...]
        acc_ref[...] += _dot_tn(av, bv)

        @pl.when(k == nk - 1)
        def _():
            o32_ref[0, 0] = acc_ref[...]
            o16_ref[0, 0] = acc_ref[...].astype(BF16)

    out = pl.BlockSpec((1, 1, M, N), lambda g, k: (0, g, 0, 0))
    return _pallas_call(
        body, name=name, grid=(G, nk),
        in_specs=[a_spec(tk), b_spec(tk)], out_specs=[out, out],
        out_shape=[jax.ShapeDtypeStruct((1, G, M, N), F32), jax.ShapeDtypeStruct((1, G, M, N), BF16)],
        scratch_shapes=[pltpu.VMEM((M, N), F32)],
        compiler_params=_cp("parallel", "arbitrary"),
    )(a, b)


def dx_norm_bwd(name, dact, d_spec, w, G, x, g, dy, l, w_transposed=False):
    T, D = x.shape
    wblk = w.shape[-2:]
    tm = _tile(T, WIDE_TILE)
    ni = T // tm
    ch = _tile(tm, ROW_TILE // 2)

    def body(d_ref, w_ref, x_ref, g_ref, dy_ref, dx_ref, dg_ref, acc_ref):
        j, i = pl.program_id(0), pl.program_id(1)
        rows = pl.ds(pl.multiple_of(i * tm, tm), tm)

        @pl.when(jnp.logical_and(i == 0, j == 0))
        def _():
            dg_ref[...] = jnp.zeros_like(dg_ref)

        @pl.when(j == 0)
        def _():
            acc_ref[rows, :] = jnp.zeros((tm, D), F32)

        dv = d_ref[0] if len(d_ref.shape) == 3 else d_ref[...]
        acc_ref[rows, :] += _dot(dv, w_ref[0, 0]) if w_transposed else _dot_nt(dv, w_ref[0, 0])

        @pl.when(j == G - 1)
        def _():
            dg = jnp.zeros((1, D), F32)
            for c0 in range(0, tm, ch):
                part_rows = pl.ds(pl.multiple_of(i * tm + c0, ch), ch)
                dx, dgc = _rms_bwd(acc_ref[part_rows, :], x_ref[c0:c0 + ch, :], g_ref[0], dy_ref[c0:c0 + ch, :])
                dx_ref[c0:c0 + ch, :] = dx
                dg = dg + dgc
            dg_ref[...] += dg

    last = pl.BlockSpec((tm, D), lambda j, i: (jnp.where(j == G - 1, i, 0), 0))
    return _pallas_call(
        body, name=name, grid=(G, ni),
        in_specs=[d_spec(tm), pl.BlockSpec((1, 1) + wblk, lambda j, i: (0, j, 0, 0)), last, pl.BlockSpec((1, 1, D), lambda j, i: (l, 0, 0)), last],
        out_specs=[last, pl.BlockSpec((1, D), lambda j, i: (0, 0))],
        out_shape=[jax.ShapeDtypeStruct((T, D), F32), jax.ShapeDtypeStruct((1, D), F32)],
        scratch_shapes=[pltpu.VMEM((T, D), F32)],
        compiler_params=_cp("arbitrary", "arbitrary"),
    )(dact, w, x, g, dy)


def mix_out_bwd(dy, proj, yp, yl, wpu, wlu, wout, P, R, l, deps=()):
    T, D = dy.shape
    tm = _tile(T, ROW_TILE)
    gb = (P + 2 * R) // D

    def body(dy_ref, gp_ref, gl_ref, yp_ref, yl_ref, wpu_ref, wlu_ref, wo_ref, *rest):
        dyb_ref, dyp_ref, dyl_ref, dgp_ref, dgl_ref, dpm_ref, dhl_ref = rest[len(deps):]
        dyb = dy_ref[...].astype(BF16)
        dyb_ref[...] = dyb
        dz = _dot_nt(dyb, wo_ref[0])
        sp = _sigmoid(gp_ref[...].astype(F32))
        sl = _sigmoid(gl_ref[...].astype(F32))
        dgp_ref[...] = (dz * yp_ref[...].astype(F32) * sp * (1.0 - sp)).astype(BF16)
        dgl_ref[...] = (dz * yl_ref[...].astype(F32) * sl * (1.0 - sl)).astype(BF16)
        dyp = (dz * sp).astype(BF16)
        dyl = (dz * sl).astype(BF16)
        dyp_ref[...] = dyp
        dyl_ref[...] = dyl
        dpm_ref[...] = _dot_nt(dyp, wpu_ref[0]).astype(BF16)
        dhl_ref[...] = _dot_nt(dyl, wlu_ref[0]).astype(BF16)

    row = lambda w: pl.BlockSpec((tm, w), lambda i: (i, 0))
    return _pallas_call(
        body, name="mix_out_bwd", grid=(T // tm,),
        in_specs=[row(D), pl.BlockSpec((tm, D), lambda i: (i, gb)), pl.BlockSpec((tm, D), lambda i: (i, gb + 1)), row(D), row(D),
                  pl.BlockSpec((1, P, D), lambda i: (0, 0, 0)), pl.BlockSpec((1, R, D), lambda i: (0, 0, 0)),
                  pl.BlockSpec((1, D, D), lambda i: (0, 0, 0))] + [ANY] * len(deps),
        out_specs=[row(D)] * 5 + [row(P), row(R)],
        out_shape=[jax.ShapeDtypeStruct((T, D), BF16)] * 5 + [jax.ShapeDtypeStruct((T, P), BF16), jax.ShapeDtypeStruct((T, R), BF16)],
        compiler_params=_cp("parallel"),
    )(dy, proj, proj, yp, yl, wpu, wlu, wout, *deps)


def lru_bwd(proj, hs, dhl, cw, cb, wa, ba, wx, bx, lam, P, l):
    T = proj.shape[0]
    _, H, hd, _ = wa.shape
    R = H * hd
    CW = cw.shape[1]

    def body(u_ref, ug_ref, cw_ref, cb_ref, wa_ref, ba_ref, wx_ref, bx_ref, lam_ref, hs_ref, dhl_ref,
             du_ref, dug_ref, dcw_ref, dcb_ref, dwa_ref, dba_ref, dwx_ref, dbx_ref, dlam_ref, c_s, g_s, y_s):
        u = u_ref[...].astype(F32)
        v = _conv(u, cw_ref, cb_ref[0])
        lam = lam_ref[0]
        r, i, sp, a, mult, inv_mult = _lru_gates(v, wa_ref, ba_ref[0], wx_ref, bx_ref[0], lam)
        ug = ug_ref[...].astype(F32)
        ge, th = _gelu(ug)
        hs = hs_ref[...]
        dhl = dhl_ref[...].astype(F32)
        dug_ref[...] = (dhl * hs * _gelu_grad(ug, th)).astype(BF16)
        c_s[...] = _shift_up(a, 1)
        g_s[...] = dhl * ge
        _scan_bwd(c_s, g_s, y_s)
        y = y_s[...]
        da = y * _shift_down(hs, 1)
        iv = i * v
        dlog_a = da * a - (y * iv) * (a * a) * inv_mult
        div = y * mult
        dpa = (dlog_a * (-LRU_C) * sp) * r * (1.0 - r)
        dpx = (div * v) * i * (1.0 - i)
        dsp = jnp.sum(dlog_a * (-LRU_C) * r, axis=0, keepdims=True)
        dlam_ref[0] = -dsp * _sigmoid(-lam)
        vb = v.astype(BF16)
        dpab, dpxb = dpa.astype(BF16), dpx.astype(BF16)
        dwa_ref[0, 0] = _dot_tn(vb, dpab)
        dwx_ref[0, 0] = _dot_tn(vb, dpxb)
        dba_ref[0] = jnp.sum(dpa, axis=0, keepdims=True)
        dbx_ref[0] = jnp.sum(dpx, axis=0, keepdims=True)
        dv = div * i + _dot_nt(dpab, wa_ref[0, 0].astype(BF16)) + _dot_nt(dpxb, wx_ref[0, 0].astype(BF16))
        dcb_ref[0] = jnp.sum(dv, axis=0, keepdims=True)
        du = jnp.zeros_like(dv)
        for k in range(CW):
            du = du + cw_ref[0, k:k + 1, :] * _shift_up(dv, CW - 1 - k)
            dcw_ref[0, k:k + 1, :] = jnp.sum(dv * _shift_down(u, CW - 1 - k), axis=0, keepdims=True)
        du_ref[...] = du.astype(BF16)

    col = pl.BlockSpec((T, hd), lambda h: (0, h))
    vec = pl.BlockSpec((1, 1, hd), lambda h: (0, 0, h))
    mat = pl.BlockSpec((1, 1, hd, hd), lambda h: (0, h, 0, 0))
    vshape = jax.ShapeDtypeStruct((1, 1, R), F32)
    mshape = jax.ShapeDtypeStruct((1, H, hd, hd), F32)
    return _pallas_call(
        body, name="lru_bwd", grid=(H,),
        in_specs=_lru_specs(T, hd, P, R, CW, l) + [col, col],
        out_specs=[col, col, pl.BlockSpec((1, CW, hd), lambda h: (0, 0, h)), vec, mat, vec, mat, vec, vec],
        out_shape=[jax.ShapeDtypeStruct((T, R), BF16)] * 2 + [jax.ShapeDtypeStruct((1, CW, R), F32), vshape, mshape, vshape, mshape, vshape, vshape],
        scratch_shapes=[pltpu.VMEM((T, hd), F32)] * 3,
        compiler_params=_cp("parallel"),
    )(proj, proj, cw, cb, wa, ba, wx, bx, lam, hs, dhl)


def pool_bwd(proj, dpm, pw, pb, ps, l):
    T = proj.shape[0]
    _, G, gd, _ = pw.shape
    P = G * gd

    def body(u_ref, d_ref, w_ref, b_ref, s_ref, du_ref, dw_ref, db_ref, dsc_ref):
        for gi in range(G):
            cols = slice(gi * gd, (gi + 1) * gd)
            w = POOL_WINDOWS[gi]
            inv = _inv_count(T, w)
            ug = u_ref[:, cols].astype(F32)
            pooled = _pooled(ug, w, inv).astype(BF16)
            wb = w_ref[0, gi].astype(BF16)
            mixed = _dot(pooled, wb) + b_ref[0, :, cols]
            dpm_g = d_ref[:, cols].astype(F32)
            dsc_ref[0, :, cols] = jnp.sum(dpm_g * mixed, axis=0, keepdims=True)
            dmixed = dpm_g * s_ref[0, :, cols]
            db_ref[0, :, cols] = jnp.sum(dmixed, axis=0, keepdims=True)
            dmb = dmixed.astype(BF16)
            dw_ref[0, gi] = _dot_tn(pooled, dmb)
            dpooled = _dot_nt(dmb, wb)
            s = dpooled * inv
            k = 1
            while k < w:
                s = s + _shift_up(s, k)
                k *= 2
            du_ref[:, cols] = (s - dpooled).astype(BF16)

    vec = pl.BlockSpec((1, 1, P), lambda i: (l, 0, 0))
    ovec = pl.BlockSpec((1, 1, P), lambda i: (0, 0, 0))
    return _pallas_call(
        body, name="pool_bwd", grid=(1,),
        in_specs=[pl.BlockSpec((T, P), lambda i: (0, 0)), pl.BlockSpec((T, P), lambda i: (0, 0)),
                  pl.BlockSpec((1, G, gd, gd), lambda i: (l, 0, 0, 0)), vec, vec],
        out_specs=[pl.BlockSpec((T, P), lambda i: (0, 0)), pl.BlockSpec((1, G, gd, gd), lambda i: (0, 0, 0, 0)), ovec, ovec],
        out_shape=[jax.ShapeDtypeStruct((T, P), BF16), jax.ShapeDtypeStruct((1, G, gd, gd), F32),
                   jax.ShapeDtypeStruct((1, 1, P), F32), jax.ShapeDtypeStruct((1, 1, P), F32)],
        compiler_params=_cp("arbitrary"),
    )(proj, dpm, pw, pb, ps)


def _place():
    x, y, c = lax.axis_index("x"), lax.axis_index("y"), lax.axis_index("c")
    return x, y, c


HBM = pl.BlockSpec(memory_space=pltpu.HBM)
SEM = pl.BlockSpec(memory_space=pltpu.SEMAPHORE)
EFFECT = pltpu.SideEffectType.DATAFLOW_SIDE_EFFECTING


def _in_hbm(a):
    return pltpu.with_memory_space_constraint(a, pltpu.HBM)


def split_start(name, bufs, n_copies, copies_of, deps=()):
    nb = len(bufs)

    def body(*refs):
        buf = refs[:nb]
        send_sems, recv_sems = refs[nb + len(deps)], refs[nb + len(deps) + 1]
        token = refs[-1]
        for i, (src, dst, dev) in enumerate(copies_of(buf)):
            pltpu.make_async_remote_copy(src_ref=src, dst_ref=dst, send_sem=send_sems.at[i], recv_sem=recv_sems.at[i],
                                         device_id=dev, device_id_type=MESH).start()
        token[...] = jnp.zeros_like(token)

    outs = _pallas_call(
        body, name=name,
        in_specs=[HBM] * nb + [ANY] * len(deps),
        out_specs=(SEM, SEM, *([HBM] * nb), pl.BlockSpec(memory_space=pltpu.VMEM)),
        out_shape=(pltpu.SemaphoreType.DMA((n_copies,)), pltpu.SemaphoreType.DMA((n_copies,)),
                   *[pltpu.HBM(b.shape, b.dtype) for b in bufs], jax.ShapeDtypeStruct((8, 128), F32)),
        input_output_aliases={i: 2 + i for i in range(nb)},
        compiler_params=pltpu.CompilerParams(has_side_effects=EFFECT),
    )(*[_in_hbm(b) for b in bufs], *deps)
    return outs[0], outs[1], list(outs[2:2 + nb]), outs[-1]


def split_wait(name, bufs, send_sems, recv_sems, after, copies_of):
    nb = len(bufs)

    def body(*refs):
        buf = refs[:nb]
        send, recv = refs[nb], refs[nb + 1]
        for i, (src, dst, dev) in enumerate(copies_of(buf)):
            cp = pltpu.make_async_remote_copy(src_ref=src, dst_ref=dst, send_sem=send.at[i], recv_sem=recv.at[i],
                                              device_id=dev, device_id_type=MESH)
            cp.wait_send()
            cp.wait_recv()

    outs = _pallas_call(
        body, name=name,
        in_specs=[HBM] * nb + [SEM, SEM] + [ANY] * len(after),
        out_specs=[HBM] * nb,
        out_shape=[pltpu.HBM(b.shape, b.dtype) for b in bufs],
        input_output_aliases={i: i for i in range(nb)},
        compiler_params=pltpu.CompilerParams(has_side_effects=EFFECT),
    )(*bufs, send_sems, recv_sems, *after)
    return list(outs)


def _two_row_blocks(rows):
    return (rows // 2, 1) if rows % 32 == 0 else (rows, 0)


def place_own(ws, dtypes, l, place):
    n = len(ws)

    def body(p_ref, *refs):
        for a in range(n):
            refs[n + a][0] = refs[a][0].astype(dtypes[a])

    in_specs, out_specs, out_shape = [], [], []
    for w, dt in zip(ws, dtypes):
        _, rows, cols = w.shape
        rb, step = _two_row_blocks(rows)
        in_specs.append(pl.BlockSpec((1, rb, cols), lambda i, p, s=step: (l, i * s, 0)))
        out_specs.append(pl.BlockSpec((1, rb, cols), lambda i, p, s=step: (p[2], i * s, 0)))
        out_shape.append(jax.ShapeDtypeStruct((N_DEV, rows, cols), dt))
    return list(_pallas_call(
        body, name="place_own",
        grid_spec=pltpu.PrefetchScalarGridSpec(num_scalar_prefetch=1, grid=(2,), in_specs=in_specs, out_specs=out_specs),
        out_shape=out_shape, compiler_params=_cp("arbitrary"),
    )(place, *ws))


def _gather_copies(land):
    x, y, c = _place()
    k = 4 * x + 2 * y + c
    peers = [(x, 1 - y, c), (1 - x, y, c), (1 - x, 1 - y, c), (x, y, 1 - c)]
    return [(b.at[k], b.at[k], p) for p in peers for b in land]


def gather_start(name, land, deps=()):
    return split_start(name, land, 4 * len(land), _gather_copies, deps)


def gather_wait(name, land, send_sems, recv_sems, after):
    return split_wait(name, land, send_sems, recv_sems, after, _gather_copies)


def _forward_copies(land):
    x, y, c = _place()
    slots = [4 * px + 2 * py + c for px, py in [(x, 1 - y), (1 - x, y), (1 - x, 1 - y)]]
    return [(b.at[k], b.at[k], (x, y, 1 - c)) for k in slots for b in land]


def gather_forward_start(name, land, deps=()):
    return split_start(name, land, 3 * len(land), _forward_copies, deps)


def gather_forward_wait(name, land, send_sems, recv_sems, after):
    return split_wait(name, land, send_sems, recv_sems, after, _forward_copies)


def _chip_copies(nsrc):
    def copies(buf):
        p16, recv2 = buf[:nsrc], buf[nsrc:]
        x, y, c = _place()
        out = []
        for d in (1, 2, 3):
            px = 1 - x if d & 2 else x
            py = 1 - y if d & 1 else y
            out += [(p16[a].at[:, d - 1], recv2[a].at[:, d - 1], (px, py, c)) for a in range(nsrc)]
        return out
    return copies


def _pair_copies(nsrc):
    def copies(buf):
        g16, recv = buf[:nsrc], buf[nsrc:]
        x, y, c = _place()
        return [(g16[a].at[:, 2 * j + 1 - c], recv[a].at[:, j], (x, y, 1 - c)) for a in range(nsrc) for j in range(N_CHIP)]
    return copies


def pair_exchange_start(name, g16, deps=()):
    n = len(g16)
    land = [lax.empty((1, N_CHIP) + s.shape[2:], s.dtype) for s in g16]
    return split_start(name, list(g16) + land, N_CHIP * n, _pair_copies(n), deps)


def pair_exchange_wait(name, bufs, send_sems, recv_sems, after):
    n = len(bufs) // 2
    return split_wait(name, bufs, send_sems, recv_sems, after, _pair_copies(n))[n:]


def chip_exchange_start(name, pair16, deps=()):
    n = len(pair16)
    land = [lax.empty((s.shape[0], 3) + s.shape[2:], s.dtype) for s in pair16]
    return split_start(name, list(pair16) + land, 3 * n, _chip_copies(n), deps)


def chip_exchange_wait(name, bufs, send_sems, recv_sems, after):
    n = len(bufs) // 2
    return split_wait(name, bufs, send_sems, recv_sems, after, _chip_copies(n))[n:]


def _rows_tile(rows, cols, budget=1 << 20):
    t = rows
    while t % 2 == 0 and t * cols > budget and (t // 2) % 16 == 0:
        t //= 2
    return t


def pair_sum(g32s, recv1s, place):
    n = len(g32s)

    def body(p_ref, *refs):
        for a in range(n):
            m_ref, r_ref, o_ref = refs[a], refs[n + a], refs[2 * n + a]
            o_ref[...] = (m_ref[...] + r_ref[...].astype(F32)).astype(o_ref.dtype)

    other = lambda d, p: jnp.bitwise_xor(p[1], d + 1)
    g_specs, r_specs, o_specs, out_shape = [], [], [], []
    for r1 in recv1s:
        _, _, rows, cols = r1.shape
        rb, step = _two_row_blocks(rows)
        g_specs.append(pl.BlockSpec((1, 1, rb, cols), lambda d, i, p, s=step: (0, 2 * other(d, p) + p[0], i * s, 0)))
        r_specs.append(pl.BlockSpec((1, 1, rb, cols), lambda d, i, p, s=step: (0, other(d, p), i * s, 0)))
        o_specs.append(pl.BlockSpec((1, 1, rb, cols), lambda d, i, p, s=step: (0, d, i * s, 0)))
        out_shape.append(jax.ShapeDtypeStruct((1, N_CHIP - 1, rows, cols), r1.dtype))
    return list(_pallas_call(
        body, name="pair_sum",
        grid_spec=pltpu.PrefetchScalarGridSpec(num_scalar_prefetch=1, grid=(N_CHIP - 1, 2), in_specs=g_specs + r_specs, out_specs=o_specs),
        out_shape=out_shape, compiler_params=_cp("arbitrary", "arbitrary"),
    )(place, *g32s, *recv1s))


def _grad_in_specs(tr, cols, l):
    return ([pl.BlockSpec((1, 1, tr, cols), lambda i, p: (l, p[2], i, 0)), pl.BlockSpec((1, 1, tr, cols), lambda i, p: (0, p[1], i, 0))]
            + [pl.BlockSpec((1, 1, tr, cols), lambda i, p, d=d: (0, d, i, 0)) for d in range(3)])


def _grad_total(o32, o16, r0, r1, r2):
    return (o32[0, 0] + o16[0, 0].astype(F32)) + r0[0, 0].astype(F32) + r1[0, 0].astype(F32) + r2[0, 0].astype(F32)


def grad_sum(g32, recv1, recv2, place):
    _, _, rows, cols = recv1.shape
    tr = _rows_tile(rows, cols)

    def body(p_ref, o32, o16, r0, r1, r2, g_ref):
        g_ref[...] = _grad_total(o32, o16, r0, r1, r2)

    return _pallas_call(
        body, name="grad_sum",
        grid_spec=pltpu.PrefetchScalarGridSpec(
            num_scalar_prefetch=1, grid=(rows // tr,), in_specs=_grad_in_specs(tr, cols, 0),
            out_specs=pl.BlockSpec((tr, cols), lambda i, p: (i, 0))),
        out_shape=jax.ShapeDtypeStruct((rows, cols), F32), compiler_params=_cp("parallel"),
    )(place, g32, recv1, recv2, recv2, recv2)


def _adamw_math(w, g, m, v):
    m = ADAM_B1 * m + (1.0 - ADAM_B1) * g
    v = ADAM_B2 * v + (1.0 - ADAM_B2) * (g * g)
    m_hat = m / (1.0 - ADAM_B1 ** ADAM_STEP)
    v_hat = v / (1.0 - ADAM_B2 ** ADAM_STEP)
    delta = -ADAM_LR * (m_hat / (jnp.sqrt(v_hat) + ADAM_EPS) + ADAM_WD * w)
    return delta, m, v


UPDATE_BLOCK = 1 << 16


def _update_rows(rows, cols):
    if rows % 16:
        return rows, 1
    tiles = rows // 16
    d = max([k for k in range(1, tiles + 1) if tiles % k == 0 and 16 * k * cols <= UPDATE_BLOCK] or [1])
    return 16 * d, tiles // d


def grad_sum_adamw(g32s, recv1s, recv2s, ws, ms, vs, place, l, prevs, deps=()):
    n = len(ws)
    blocks = [_update_rows(w.shape[1], w.shape[2]) for w in ws]
    have_prev = prevs[0] is not None

    def body(p_ref, *refs):
        outs = refs[len(refs) - 4 * n:]

        def update(a):
            o32, o16, r0, r1, r2, w_ref, m_ref, v_ref = refs[8 * a:8 * a + 8]
            g_ref, d_ref, nm_ref, nv_ref = outs[4 * a:4 * a + 4]

            @pl.when(pl.program_id(0) < blocks[a][1])
            def _():
                g = _grad_total(o32, o16, r0, r1, r2)
                d, nm, nv = _adamw_math(w_ref[0], g, m_ref[0], v_ref[0])
                g_ref[0] = g
                d_ref[0] = d
                nm_ref[0] = nm
                nv_ref[0] = nv

        for a in range(n):
            update(a)

    args, in_specs, out_specs, out_shape = [], [], [], []
    for a in range(n):
        L, rows, cols = ws[a].shape
        rb, nb = blocks[a]
        at = lambda i, nb=nb: jnp.minimum(i, nb - 1)
        slot = lambda which: pl.BlockSpec((1, 1, rb, cols), lambda i, p, at=at: (0, which(p), at(i), 0))
        blk = pl.BlockSpec((1, rb, cols), lambda i, p, at=at: (l, at(i), 0))
        args += [g32s[a], recv1s[a], recv2s[a], recv2s[a], recv2s[a], ws[a], ms[a], vs[a]]
        in_specs += [slot(lambda p: p[2]), slot(lambda p: p[1])] + [slot(lambda p, d=d: d) for d in range(3)] + [blk] * 3
        out_specs += [blk] * 4
        out_shape += [jax.ShapeDtypeStruct((L, rows, cols), F32)] * 4
    aliases = {}
    if have_prev:
        aliases = {1 + len(args) + k: k for k in range(4 * n)}
        args += [buf for prev in prevs for buf in prev]
        in_specs += [ANY] * (4 * n)
    args += list(deps)
    in_specs += [ANY] * len(deps)
    outs = _pallas_call(
        body, name="grad_sum_adamw",
        grid_spec=pltpu.PrefetchScalarGridSpec(num_scalar_prefetch=1, grid=(max(nb for _, nb in blocks),),
                                               in_specs=in_specs, out_specs=out_specs),
        out_shape=out_shape, input_output_aliases=aliases, compiler_params=_cp("arbitrary"),
    )(place, *args)
    return [list(outs[4 * a:4 * a + 4]) for a in range(n)]


def adamw(w, g, m, v):
    rows, cols = w.shape
    tr = _rows_tile(rows, cols, 1 << 18)

    def body(w_ref, g_ref, m_ref, v_ref, d_ref, nm_ref, nv_ref):
        d, nm, nv = _adamw_math(w_ref[...], g_ref[...], m_ref[...], v_ref[...])
        d_ref[...] = d
        nm_ref[...] = nm
        nv_ref[...] = nv

    blk = pl.BlockSpec((tr, cols), lambda i: (i, 0))
    return _pallas_call(body, name="adamw_small", grid=(rows // tr,), in_specs=[blk] * 4, out_specs=[blk] * 3,
                        out_shape=[jax.ShapeDtypeStruct((rows, cols), F32)] * 3, compiler_params=_cp("parallel"))(w, g, m, v)


SMALL = ("norm_ffn1", "norm_mix", "pool_w", "pool_b", "pool_scale", "conv_w", "conv_b", "lru_w_a", "lru_b_a", "lru_w_x", "lru_b_x",
         "lru_lambda", "norm_ffn2", "final_norm")
BIG = ("ffn1_w_up", "ffn1_w_down", "w_in", "w_pool_up", "w_lru_up", "w_out", "ffn2_w_up", "ffn2_w_down")
NAMES = ("norm_ffn1", "ffn1_w_up", "ffn1_w_down", "norm_mix", "w_in", "pool_w", "pool_b", "pool_scale", "w_pool_up", "conv_w", "conv_b",
         "lru_w_a", "lru_b_a", "lru_w_x", "lru_b_x", "lru_lambda", "w_lru_up", "w_out", "norm_ffn2", "ffn2_w_up", "ffn2_w_down", "final_norm")
SUBLAYERS = (("ffn1_w_up", "ffn1_w_down"), ("w_in", "w_pool_up", "w_lru_up", "w_out", "conv_w"), ("ffn2_w_up", "ffn2_w_down"))
PACK_ROWS = 16 * N_DEV


def _pack(parts):
    flat = jnp.concatenate([p.reshape(-1) for p in parts])
    unit = 128 * PACK_ROWS
    padded = -(-flat.size // unit) * unit
    return jnp.pad(flat, (0, padded - flat.size)).reshape(-1, 128)


def _unpack(packed, shapes):
    flat = packed.reshape(-1)
    out, off = [], 0
    for s in shapes:
        n = 1
        for d in s:
            n *= d
        out.append(flat[off:off + n].reshape(s))
        off += n
    return out


def kernel(x, norm_ffn1, ffn1_w_up, ffn1_w_down, norm_mix, w_in, pool_w, pool_b, pool_scale, w_pool_up, conv_w, conv_b, lru_w_a, lru_b_a, lru_w_x, lru_b_x, lru_lambda, w_lru_up, w_out, norm_ffn2, ffn2_w_up, ffn2_w_down, final_norm, loss_target, m_norm_ffn1, m_ffn1_w_up, m_ffn1_w_down, m_norm_mix, m_w_in, m_pool_w, m_pool_b, m_pool_scale, m_w_pool_up, m_conv_w, m_conv_b, m_lru_w_a, m_lru_b_a, m_lru_w_x, m_lru_b_x, m_lru_lambda, m_w_lru_up, m_w_out, m_norm_ffn2, m_ffn2_w_up, m_ffn2_w_down, m_final_norm, v_norm_ffn1, v_ffn1_w_up, v_ffn1_w_down, v_norm_mix, v_w_in, v_pool_w, v_pool_b, v_pool_scale, v_w_pool_up, v_conv_w, v_conv_b, v_lru_w_a, v_lru_b_a, v_lru_w_x, v_lru_b_x, v_lru_lambda, v_w_lru_up, v_w_out, v_norm_ffn2, v_ffn2_w_up, v_ffn2_w_down, v_final_norm):
    W = dict(norm_ffn1=norm_ffn1, ffn1_w_up=ffn1_w_up, ffn1_w_down=ffn1_w_down, norm_mix=norm_mix, w_in=w_in, pool_w=pool_w, pool_b=pool_b,
             pool_scale=pool_scale, w_pool_up=w_pool_up, conv_w=conv_w, conv_b=conv_b, lru_w_a=lru_w_a, lru_b_a=lru_b_a, lru_w_x=lru_w_x,
             lru_b_x=lru_b_x, lru_lambda=lru_lambda, w_lru_up=w_lru_up, w_out=w_out, norm_ffn2=norm_ffn2, ffn2_w_up=ffn2_w_up,
             ffn2_w_down=ffn2_w_down, final_norm=final_norm)
    M = dict(norm_ffn1=m_norm_ffn1, ffn1_w_up=m_ffn1_w_up, ffn1_w_down=m_ffn1_w_down, norm_mix=m_norm_mix, w_in=m_w_in, pool_w=m_pool_w,
             pool_b=m_pool_b, pool_scale=m_pool_scale, w_pool_up=m_w_pool_up, conv_w=m_conv_w, conv_b=m_conv_b, lru_w_a=m_lru_w_a,
             lru_b_a=m_lru_b_a, lru_w_x=m_lru_w_x, lru_b_x=m_lru_b_x, lru_lambda=m_lru_lambda, w_lru_up=m_w_lru_up, w_out=m_w_out,
             norm_ffn2=m_norm_ffn2, ffn2_w_up=m_ffn2_w_up, ffn2_w_down=m_ffn2_w_down, final_norm=m_final_norm)
    V = dict(norm_ffn1=v_norm_ffn1, ffn1_w_up=v_ffn1_w_up, ffn1_w_down=v_ffn1_w_down, norm_mix=v_norm_mix, w_in=v_w_in, pool_w=v_pool_w,
             pool_b=v_pool_b, pool_scale=v_pool_scale, w_pool_up=v_w_pool_up, conv_w=v_conv_w, conv_b=v_conv_b, lru_w_a=v_lru_w_a,
             lru_b_a=v_lru_b_a, lru_w_x=v_lru_w_x, lru_b_x=v_lru_b_x, lru_lambda=v_lru_lambda, w_lru_up=v_w_lru_up, w_out=v_w_out,
             norm_ffn2=v_norm_ffn2, ffn2_w_up=v_ffn2_w_up, ffn2_w_down=v_ffn2_w_down, final_norm=v_final_norm)

    for S in (W, M, V):
        for n in ("ffn1_w_up", "ffn2_w_up"):
            S[n] = jnp.swapaxes(S[n], 1, 2)

    T, D = x.shape[1], x.shape[2]
    L = norm_ffn1.shape[0]
    P = pool_scale.shape[1]
    R = lru_lambda.shape[1]
    H, hd = lru_w_a.shape[1], lru_w_a.shape[2]
    CW = conv_w.shape[1]
    cs = ffn1_w_up.shape[2]
    ci = w_in.shape[2]
    xin = x.reshape(T, D)
    tgt = loss_target.reshape(T, D)
    dev = 4 * lax.axis_index("x") + 2 * lax.axis_index("y") + lax.axis_index("c")
    place = jnp.stack([lax.axis_index("c"), 2 * lax.axis_index("x") + lax.axis_index("y"), dev]).astype(jnp.int32)

    cw_flat = conv_w.reshape(L, -1)
    cw_pad = (-cw_flat.shape[1]) % 1024
    cw_tiles = jnp.pad(cw_flat, ((0, 0), (0, cw_pad))).reshape(L, -1, 128)

    def units(l):
        return SUBLAYERS if l == 0 else (tuple(n for u in SUBLAYERS for n in u),)

    queued = {"gather": (), "pair": (), "chip": ()}

    gathering = []

    def gather_units_start(l):
        for k, names in enumerate(SUBLAYERS):
            land = place_own([cw_tiles if n == "conv_w" else W[n] for n in names], [F32 if n == "conv_w" else BF16 for n in names], l, place)
            send_sems, recv_sems, land, tok = gather_start(f"gather_start_l{l}_u{k}", land, queued["gather"])
            gathering.append(dict(names=names, tag=f"l{l}_u{k}", send=send_sems, recv=recv_sems, land=land, tok=tok, arrived=False))
            queued["gather"] = (tok,)

    def gather_unit_arrive(after):
        waiting = [u for u in gathering if not u["arrived"]]
        if not waiting:
            return ()
        unit, tokens = waiting[0], [u["tok"] for u in waiting[1:]]
        land = gather_wait(f"gather_wait_{unit['tag']}", unit["land"], unit["send"], unit["recv"], list(after) + tokens)
        send_sems, recv_sems, land, tok = gather_forward_start(f"gather_pass_start_{unit['tag']}", land)
        unit.update(land=land, send=send_sems, recv=recv_sems, tok=tok, arrived=True)
        return (tok,)

    def gather_unit_weights(after):
        if not gathering[0]["arrived"]:
            gather_unit_arrive(after)
        unit = gathering.pop(0)
        land = gather_forward_wait(f"gather_pass_wait_{unit['tag']}", unit["land"], unit["send"], unit["recv"], after)
        g = dict(zip(unit["names"], land))
        one = lambda a: a.reshape((1,) + a.shape)
        w = {}
        for tag_, up, dn in (("1", "ffn1_w_up", "ffn1_w_down"), ("2", "ffn2_w_up", "ffn2_w_down")):
            if up in g:
                w["wup" + tag_], w["wd" + tag_] = one(g[up]), g[dn].reshape(1, -1, D)
        if "w_in" in g:
            cw_l = g["conv_w"].reshape(N_DEV, -1)[:, :cw_flat.shape[1]].reshape((N_DEV,) + conv_w.shape[1:])
            w.update(win=one(g["w_in"]), wlu=g["w_lru_up"].reshape(1, R, D), wout=g["w_out"].reshape(1, D, D),
                     wpu=g["w_pool_up"].transpose(1, 0, 2).reshape(1, P, D),
                     cw=cw_l.transpose(1, 0, 2).reshape(1, CW, R))
        return w

    vec = lambda a: a.reshape(L, 1, -1)
    p = dict(g1=vec(norm_ffn1), gm=vec(norm_mix), g2=vec(norm_ffn2), pb=vec(pool_b), ps=vec(pool_scale), cb=vec(conv_b),
             ba=vec(lru_b_a), bx=vec(lru_b_x), lam=vec(lru_lambda), pw=pool_w, wa=lru_w_a, wx=lru_w_x)

    AHEAD = 2
    for l in range(min(AHEAD, L)):
        gather_units_start(l)
    saved, LW = [], []
    xc = xin
    for l in range(L):
        w = gather_unit_weights([xc])
        if l + AHEAD < L:
            gather_units_start(l + AHEAD)
        sv = {"x1": xc}
        sv["h1"], sv["u1"], sv["s1"] = ffn_up(xc, p["g1"], w["wup1"], l)
        xc = ffn_down(sv["s1"], w["wd1"], xc, l, gather_unit_arrive([sv["s1"]]))
        sv["x2"] = xc
        w.update(gather_unit_weights([xc]))
        sv["h2"], sv["proj"] = mix_in(xc, p["gm"], w["win"], l)
        sv["pm"] = pool_fwd(sv["proj"], p["pw"], p["pb"], p["ps"], l)
        sv["hl"], sv["hs"] = lru_fwd(sv["proj"], w["cw"], p["cb"], p["wa"], p["ba"], p["wx"], p["bx"], p["lam"], P, l)
        xc, sv["yp"], sv["yl"], sv["z"] = mix_out(sv["pm"], sv["hl"], sv["proj"], xc, w["wpu"], w["wlu"], w["wout"], P, l,
                                                  gather_unit_arrive([sv["hl"]]))
        sv["x3"] = xc
        w.update(gather_unit_weights([xc]))
        sv["h3"], sv["u3"], sv["s3"] = ffn_up(xc, p["g2"], w["wup2"], l)
        xc = ffn_down(sv["s3"], w["wd2"], xc, l, gather_unit_arrive([sv["s3"]]))
        saved.append(sv)
        LW.append(w)

    loss_part, dx, d_final = loss_head(xc, final_norm.reshape(1, D), tgt)
    loss = lax.psum(loss_part[0, 0], ("x", "y", "c"))

    G = [dict() for _ in range(L)]
    small = {n: [None] * L for n in SMALL if n != "final_norm"}

    def to_slots(name, pair):
        if name == "w_pool_up":
            return tuple(a.reshape(1, P, N_DEV, D // N_DEV).transpose(0, 2, 1, 3) for a in pair)
        return tuple(a.reshape((1, N_DEV) + W[name].shape[1:]) for a in pair)

    def ffn_bwd(dy, sv, tag, wup, wd, gn, up_name, dn_name, norm_name, l, deps=()):
        dout, du = ffn_down_bwd(dy, wd, sv["u" + tag], l, deps)
        du = du.reshape(N_DEV, T, cs)
        G[l][dn_name] = to_slots(dn_name, dw_tn("dw_down", sv["s" + tag], lambda tk: pl.BlockSpec((1, tk, cs), lambda g, k: (g, k, 0)),
                                                dout, lambda tk: pl.BlockSpec((tk, D), lambda g, k: (k, 0)), 4, cs, D, T))
        G[l][up_name] = to_slots(up_name, dw_tn("dw_up", du, lambda tk: pl.BlockSpec((1, tk, cs), lambda g, k: (g, k, 0)),
                                                sv["h" + tag], lambda tk: pl.BlockSpec((tk, D), lambda g, k: (k, 0)), N_DEV, cs, D, T))
        dxn, dg = dx_norm_bwd("ffn_dx", du, lambda tm: pl.BlockSpec((1, tm, cs), lambda j, i: (j, i, 0)), wup, N_DEV,
                              sv["x" + tag], gn, dy, l, w_transposed=True)
        small[norm_name][l] = dg.reshape(D)
        return dxn

    pairing, in_flight = [], []

    def reduce_start(l, names, tag):
        names = [n for n in names if n != "conv_w"]
        send_sems, recv_sems, bufs, tok = pair_exchange_start(f"rs_pair_start_{tag}", [G[l][n][1] for n in names], queued["pair"])
        pairing.append((l, names, tag, send_sems, recv_sems, bufs))
        queued["pair"] = (tok,)
        return (tok,)

    def reduce_continue(after):
        l, names, tag, send_sems, recv_sems, bufs = pairing.pop(0)
        recv1 = pair_exchange_wait(f"rs_pair_wait_{tag}", bufs, send_sems, recv_sems, after)
        pair16 = pair_sum([G[l][n][0] for n in names], recv1, place)
        send_sems, recv_sems, bufs, tok = chip_exchange_start(f"rs_chip_start_{tag}", pair16, queued["chip"])
        in_flight.append((l, names, tag, send_sems, recv_sems, bufs, recv1))
        queued["chip"] = (tok,)
        return (tok,)

    def boundary(l, k, dx_now):
        deps = reduce_continue([dx_now]) if pairing else ()
        if len(units(l)) > 1:
            deps += reduce_start(l, units(l)[k], f"l{l}_u{k}")
        elif k == 0:
            deps += reduce_start(l, units(l)[0], f"l{l}_u0")
        return deps

    deps = ()
    for l in reversed(range(L)):
        sv, w = saved[l], LW[l]
        dx = ffn_bwd(dx, sv, "3", w["wup2"], w["wd2"], p["g2"], "ffn2_w_up", "ffn2_w_down", "norm_ffn2", l, deps)
        deps = boundary(l, 2, dx)
        dyb, dyp, dyl, dgp, dgl, dpm, dhl = mix_out_bwd(dx, sv["proj"], sv["yp"], sv["yl"], w["wpu"], w["wlu"], w["wout"], P, R, l, deps)
        row = lambda wd_: (lambda tk: pl.BlockSpec((tk, wd_), lambda g, k: (k, 0)))
        G[l]["w_out"] = to_slots("w_out", dw_tn("dw_out", sv["z"], row(D), dyb, row(D), 1, D, D, T))
        G[l]["w_lru_up"] = to_slots("w_lru_up", dw_tn("dw_lru_up", sv["hl"], row(R), dyl, row(D), 1, R, D, T))
        G[l]["w_pool_up"] = to_slots("w_pool_up", dw_tn("dw_pool_up", sv["pm"], row(P), dyp, row(D), 1, P, D, T))
        du_lru, du_gelu, dcw, dcb, dwa, dba, dwx, dbx, dlam = lru_bwd(
            sv["proj"], sv["hs"], dhl, w["cw"], p["cb"], p["wa"], p["ba"], p["wx"], p["bx"], p["lam"], P, l)
        du_pool, dpw, dpb, dpsc = pool_bwd(sv["proj"], dpm, p["pw"], p["pb"], p["ps"], l)
        dproj = jnp.concatenate([du_pool, du_lru, du_gelu, dgp, dgl], axis=1)
        G[l]["w_in"] = to_slots("w_in", dw_tn("dw_in", sv["h2"], row(D), dproj, lambda tk: pl.BlockSpec((tk, ci), lambda g, k: (k, g)),
                                              N_DEV, D, ci, T))
        dx, dgm = dx_norm_bwd("mix_dx", dproj, lambda tm: pl.BlockSpec((tm, ci), lambda j, i: (i, j)), w["win"], N_DEV,
                              sv["x2"], p["gm"], dx, l)
        small["norm_mix"][l] = dgm.reshape(D)
        small["pool_w"][l], small["pool_b"][l], small["pool_scale"][l] = dpw[0], dpb.reshape(pool_b.shape[1:]), dpsc.reshape(P)
        small["conv_w"][l], small["conv_b"][l] = dcw[0], dcb.reshape(R)
        small["lru_w_a"][l], small["lru_b_a"][l] = dwa[0], dba.reshape(H, hd)
        small["lru_w_x"][l], small["lru_b_x"][l] = dwx[0], dbx.reshape(H, hd)
        small["lru_lambda"][l] = dlam.reshape(R)
        deps = boundary(l, 1, dx)
        dx = ffn_bwd(dx, sv, "1", w["wup1"], w["wd1"], p["g1"], "ffn1_w_up", "ffn1_w_down", "norm_ffn1", l, deps)
        deps = boundary(l, 0, dx)

    grad_x = dx.reshape(x.shape)

    small_parts = [jnp.stack(small[n]) for n in SMALL if n != "final_norm"] + [d_final.reshape(D)]
    small_shapes = [p.shape for p in small_parts]
    gpack = _pack(small_parts).reshape(1, N_DEV, -1, 128)
    small_pair = pair_exchange_start("rs_pair_start_small", [gpack], queued["pair"])
    while pairing:
        reduce_continue([dx])

    outs = {n: None for n in BIG}

    def unit_updates(unit, recv2, deps=()):
        l, names, recv1 = unit[0], unit[1], unit[6]
        updated = grad_sum_adamw([G[l][n][0] for n in names], recv1, recv2, [W[n] for n in names], [M[n] for n in names],
                                 [V[n] for n in names], place, l, [outs[n] for n in names], deps)
        outs.update(zip(names, updated))
        return [outs[n][0] for n in names]

    after = [dx]
    late = in_flight[-2:]
    for k, unit in enumerate(in_flight[:-2]):
        recv2 = chip_exchange_wait(f"rs_chip_wait_{unit[2]}", unit[5], unit[3], unit[4], after)
        after = unit_updates(unit, recv2)
        if k == 0:
            recv1_s = pair_exchange_wait("rs_pair_wait_small", small_pair[2], small_pair[0], small_pair[1], after)[0]
            small_chip = chip_exchange_start("rs_chip_start_small", pair_sum([gpack], [recv1_s], place), queued["chip"])
    late_recv2 = []
    for unit in late:
        late_recv2.append(chip_exchange_wait(f"rs_chip_wait_{unit[2]}", unit[5], unit[3], unit[4], after))
        after = [late_recv2[-1][0]]
    recv2_s = chip_exchange_wait("rs_chip_wait_small", small_chip[2], small_chip[0], small_chip[1], after)[0]
    gs = grad_sum(gpack, recv1_s, recv2_s, place)
    gs_slots = place_own([gs.reshape((1,) + gs.shape)], [F32], 0, place)
    send_sems, recv_sems, gs_slots, tok = gather_start("gather_start_small", gs_slots)
    after = unit_updates(late[0], late_recv2[0], (tok,))
    gs_slots = gather_wait("gather_wait_small", gs_slots, send_sems, recv_sems, after)
    send_sems, recv_sems, gs_slots, tok = gather_forward_start("gather_pass_start_small", gs_slots)
    after = unit_updates(late[1], late_recv2[1], (tok,))
    gs_all = gather_forward_wait("gather_pass_wait_small", gs_slots, send_sems, recv_sems, after)[0].reshape(-1, 128)
    for n in ("ffn1_w_up", "ffn2_w_up"):
        outs[n] = [jnp.swapaxes(o, 1, 2) for o in outs[n]]
    out_g, out_d, out_m, out_v = ({n: outs[n][k] for n in BIG} for k in range(4))

    small_g = dict(zip(SMALL, _unpack(gs_all, small_shapes)))
    for n in SMALL:
        if n != "conv_w":
            flat = lambda a: a.reshape(-1, 128)
            out_g[n] = small_g[n]
            out_d[n], out_m[n], out_v[n] = (o.reshape(W[n].shape) for o in adamw(flat(W[n]), flat(small_g[n]), flat(M[n]), flat(V[n])))
    cwc = conv_w.shape[2]
    gcw = lax.dynamic_slice_in_dim(small_g["conv_w"], dev * cwc, cwc, axis=2)
    cw2 = lambda a: a.reshape(-1, cwc)
    pad_rows = (-cw2(conv_w).shape[0]) % 8
    padr = lambda a: jnp.pad(cw2(a), ((0, pad_rows), (0, 0)))
    dcw_, mcw_, vcw_ = adamw(padr(conv_w), padr(gcw), padr(M["conv_w"]), padr(V["conv_w"]))
    nrow = cw2(conv_w).shape[0]
    out_g["conv_w"] = gcw
    out_d["conv_w"], out_m["conv_w"], out_v["conv_w"] = (a[:nrow].reshape(conv_w.shape) for a in (dcw_, mcw_, vcw_))

    return (loss, grad_x, *[out_g[n] for n in NAMES], *[out_d[n] for n in NAMES], *[out_m[n] for n in NAMES], *[out_v[n] for n in NAMES])
```
